```python
import jax, jax.numpy as jnp
from jax import lax
import numpy as np

D_MODEL = 1024
BATCH = 16
SEQ = 2048
DEPTH = 2

N_Q_HEADS = 8
N_KV_HEADS = 2
HEAD_DIM = 64
Q_GROUP = N_Q_HEADS // N_KV_HEADS
ATT_WIDTH = N_Q_HEADS * HEAD_DIM
KV_WIDTH = N_KV_HEADS * HEAD_DIM
WINDOW = 128
ATT_BLOCK = 128
SGU_WIDTH = D_MODEL // 2
SGU_GROUPS = 8
SGU_GROUP_DIM = SGU_WIDTH // SGU_GROUPS
SGU_CHUNK = 128
N_BRANCHES = 2
IN_WIDTH = ATT_WIDTH + 2 * KV_WIDTH + 2 * SGU_WIDTH + N_BRANCHES * D_MODEL
D_FF = 256 * ((8 * D_MODEL // 3 + 255) // 256)
CONV_WIDTH = 3
NORM_EPS = 1e-6
NEG_INF = -1e30

kernel_name = "hybrid_gated_swa_sgu_convffn"


def rmsnorm(x, gain):
    xf = x.astype(jnp.float32)
    y = xf * lax.rsqrt(jnp.mean(xf * xf, axis=-1, keepdims=True) + NORM_EPS)
    return (y * gain.astype(jnp.float32)).astype(x.dtype)


def alibi_slopes():
    return jnp.exp2(-8.0 * (jnp.arange(N_Q_HEADS, dtype=jnp.float32) + 1.0) / N_Q_HEADS)


def sliding_window_attention(q, k, v, q_gain, k_gain, sinks):
    B, S, _ = q.shape
    nb = S // ATT_BLOCK
    q = rmsnorm(q.reshape(B, S, N_Q_HEADS, HEAD_DIM), q_gain)
    k = rmsnorm(k.reshape(B, S, N_KV_HEADS, HEAD_DIM), k_gain)
    v = v.reshape(B, S, N_KV_HEADS, HEAD_DIM)
    qb = q.reshape(B, nb, ATT_BLOCK, N_KV_HEADS, Q_GROUP, HEAD_DIM)

    def band(t):
        tp = jnp.pad(t, ((0, 0), (ATT_BLOCK, 0), (0, 0), (0, 0)))
        tp = tp.reshape(B, nb + 1, ATT_BLOCK, N_KV_HEADS, HEAD_DIM)
        return jnp.concatenate([tp[:, :-1], tp[:, 1:]], axis=2)

    kb, vb = band(k), band(v)
    scores = jnp.einsum('bnqhgd,bnkhd->bnhgqk', qb, kb,
                        preferred_element_type=jnp.float32) * (HEAD_DIM ** -0.5)
    qi = jnp.arange(ATT_BLOCK)[:, None]
    kj = jnp.arange(2 * ATT_BLOCK)[None, :]
    dist = qi + ATT_BLOCK - kj
    key_pos = jnp.arange(nb)[:, None] * ATT_BLOCK - ATT_BLOCK + kj
    valid = ((dist >= 0) & (dist < WINDOW))[None] & (key_pos >= 0)[:, None, :]
    slopes = alibi_slopes().reshape(N_KV_HEADS, Q_GROUP)
    alibi = -slopes[:, :, None, None] * dist.astype(jnp.float32)[None, None]
    scores = jnp.where(valid[None, :, None, None], scores + alibi[None, None], NEG_INF)
    sink = jnp.broadcast_to(
        sinks.astype(jnp.float32).reshape(1, 1, N_KV_HEADS, Q_GROUP, 1, 1),
        scores.shape[:-1] + (1,))
    probs = jax.nn.softmax(jnp.concatenate([scores, sink], axis=-1), axis=-1)[..., :-1]
    out = jnp.einsum('bnhgqk,bnkhd->bnqhgd', probs.astype(v.dtype), vb)
    return out.reshape(B, S, ATT_WIDTH)


def chunked_spatial_gating(u, v, sgu_gain, w_s, b_s):
    B, S, _ = u.shape
    nc = S // SGU_CHUNK
    u = jax.nn.gelu(u)
    v = rmsnorm(jax.nn.gelu(v), sgu_gain)
    vc = v.reshape(B, nc, SGU_CHUNK, SGU_GROUPS, SGU_GROUP_DIM)
    causal = jnp.tril(jnp.ones((SGU_CHUNK, SGU_CHUNK), dtype=bool))
    w = jnp.where(causal[None], w_s, 0)
    mixed = jnp.einsum('gij,bcjgd->bcigd', w, vc) + b_s.T[:, :, None]
    return u * mixed.reshape(B, S, SGU_WIDTH)


def conv_gated_mlp(h, w_up, conv_w, conv_b, w_down):
    S = h.shape[1]
    z = h @ w_up
    zp = jnp.pad(z, ((0, 0), (CONV_WIDTH - 1, 0), (0, 0)))
    zc = conv_b
    for tap in range(CONV_WIDTH):
        zc = zc + conv_w[tap] * zp[:, tap:tap + S]
    gate, val = jnp.split(zc, 2, axis=-1)
    return (jax.nn.silu(gate) * val) @ w_down


def _fwd_setup_inputs(seed: int = 0) -> dict:
    key = jax.random.key(seed)
    ks = jax.random.split(key, 20)
    f32 = jnp.float32

    def nrm(k, shape, scale):
        return jax.random.normal(k, shape, f32) * scale

    return {
        "x": nrm(ks[0], (BATCH, SEQ, D_MODEL), 1.0),
        "mix_norm": 1.0 + nrm(ks[1], (DEPTH, D_MODEL), 0.05),
        "w_in": nrm(ks[2], (DEPTH, D_MODEL, IN_WIDTH), D_MODEL ** -0.5),
        "q_norm": 1.0 + nrm(ks[3], (DEPTH, HEAD_DIM), 0.05),
        "k_norm": 1.0 + nrm(ks[4], (DEPTH, HEAD_DIM), 0.05),
        "sinks": nrm(ks[5], (DEPTH, N_Q_HEADS), 0.5),
        "sgu_norm": 1.0 + nrm(ks[6], (DEPTH, SGU_WIDTH), 0.05),
        "w_s": nrm(ks[7], (DEPTH, SGU_GROUPS, SGU_CHUNK, SGU_CHUNK), SGU_CHUNK ** -0.5),
        "b_s": 1.0 + nrm(ks[8], (DEPTH, SGU_GROUPS, SGU_CHUNK), 0.1),
        "w_oa": nrm(ks[9], (DEPTH, ATT_WIDTH, D_MODEL), ATT_WIDTH ** -0.5),
        "w_ob": nrm(ks[10], (DEPTH, SGU_WIDTH, D_MODEL), SGU_WIDTH ** -0.5),
        "w_out": nrm(ks[11], (DEPTH, D_MODEL, D_MODEL), D_MODEL ** -0.5),
        "ffn_norm": 1.0 + nrm(ks[12], (DEPTH, D_MODEL), 0.05),
        "w_up": nrm(ks[13], (DEPTH, D_MODEL, 2 * D_FF), D_MODEL ** -0.5),
        "conv_w": nrm(ks[14], (DEPTH, CONV_WIDTH, 2 * D_FF), CONV_WIDTH ** -0.5),
        "conv_b": nrm(ks[15], (DEPTH, 2 * D_FF), 0.02),
        "w_down": nrm(ks[16], (DEPTH, D_FF, D_MODEL), D_FF ** -0.5),
    }


def _fwd_reference(x, mix_norm, w_in, q_norm, k_norm, sinks, sgu_norm, w_s, b_s,
              w_oa, w_ob, w_out, ffn_norm, w_up, conv_w, conv_b, w_down):
    splits = (ATT_WIDTH,
              ATT_WIDTH + KV_WIDTH,
              ATT_WIDTH + 2 * KV_WIDTH,
              ATT_WIDTH + 2 * KV_WIDTH + SGU_WIDTH,
              ATT_WIDTH + 2 * KV_WIDTH + 2 * SGU_WIDTH,
              ATT_WIDTH + 2 * KV_WIDTH + 2 * SGU_WIDTH + D_MODEL)
    for l in range(DEPTH):
        h = rmsnorm(x, mix_norm[l])
        proj = h @ w_in[l]
        q, k, v, su, sv, g_a, g_b = jnp.split(proj, splits, axis=-1)
        y_att = sliding_window_attention(q, k, v, q_norm[l], k_norm[l], sinks[l])
        y_sgu = chunked_spatial_gating(su, sv, sgu_norm[l], w_s[l], b_s[l])
        merged = (jax.nn.sigmoid(g_a) * (y_att @ w_oa[l])
                  + jax.nn.sigmoid(g_b) * (y_sgu @ w_ob[l]))
        x = x + merged @ w_out[l]
        x = x + conv_gated_mlp(rmsnorm(x, ffn_norm[l]), w_up[l], conv_w[l], conv_b[l], w_down[l])
    return x


import jax as _jax
import jax.numpy as _jnp

TWIN_FORMAT = 'train_step'
FWD_PARAMS = ['x', 'mix_norm', 'w_in', 'q_norm', 'k_norm', 'sinks', 'sgu_norm', 'w_s', 'b_s', 'w_oa', 'w_ob', 'w_out', 'ffn_norm', 'w_up', 'conv_w', 'conv_b', 'w_down']
TWIN_WEIGHTS = ['mix_norm', 'w_in', 'q_norm', 'k_norm', 'sinks', 'sgu_norm', 'w_s', 'b_s', 'w_oa', 'w_ob', 'w_out', 'ffn_norm', 'w_up', 'conv_w', 'conv_b', 'w_down']
TWIN_DIFF_INPUT = 'x'
TWIN_INPUTS = ['x', 'mix_norm', 'w_in', 'q_norm', 'k_norm', 'sinks', 'sgu_norm', 'w_s', 'b_s', 'w_oa', 'w_ob', 'w_out', 'ffn_norm', 'w_up', 'conv_w', 'conv_b', 'w_down', 'loss_target', 'm_mix_norm', 'm_w_in', 'm_q_norm', 'm_k_norm', 'm_sinks', 'm_sgu_norm', 'm_w_s', 'm_b_s', 'm_w_oa', 'm_w_ob', 'm_w_out', 'm_ffn_norm', 'm_w_up', 'm_conv_w', 'm_conv_b', 'm_w_down', 'v_mix_norm', 'v_w_in', 'v_q_norm', 'v_k_norm', 'v_sinks', 'v_sgu_norm', 'v_w_s', 'v_b_s', 'v_w_oa', 'v_w_ob', 'v_w_out', 'v_ffn_norm', 'v_w_up', 'v_conv_w', 'v_conv_b', 'v_w_down']
TWIN_OUTPUTS = ['loss', 'grad_x', 'grad_mix_norm', 'grad_w_in', 'grad_q_norm', 'grad_k_norm', 'grad_sinks', 'grad_sgu_norm', 'grad_w_s', 'grad_b_s', 'grad_w_oa', 'grad_w_ob', 'grad_w_out', 'grad_ffn_norm', 'grad_w_up', 'grad_conv_w', 'grad_conv_b', 'grad_w_down', 'delta_mix_norm', 'delta_w_in', 'delta_q_norm', 'delta_k_norm', 'delta_sinks', 'delta_sgu_norm', 'delta_w_s', 'delta_b_s', 'delta_w_oa', 'delta_w_ob', 'delta_w_out', 'delta_ffn_norm', 'delta_w_up', 'delta_conv_w', 'delta_conv_b', 'delta_w_down', 'new_m_mix_norm', 'new_m_w_in', 'new_m_q_norm', 'new_m_k_norm', 'new_m_sinks', 'new_m_sgu_norm', 'new_m_w_s', 'new_m_b_s', 'new_m_w_oa', 'new_m_w_ob', 'new_m_w_out', 'new_m_ffn_norm', 'new_m_w_up', 'new_m_conv_w', 'new_m_conv_b', 'new_m_w_down', 'new_v_mix_norm', 'new_v_w_in', 'new_v_q_norm', 'new_v_k_norm', 'new_v_sinks', 'new_v_sgu_norm', 'new_v_w_s', 'new_v_b_s', 'new_v_w_oa', 'new_v_w_ob', 'new_v_w_out', 'new_v_ffn_norm', 'new_v_w_up', 'new_v_conv_w', 'new_v_conv_b', 'new_v_w_down']
TWIN_LEAF_KINDS = {'loss': 'loss', 'grad_x': 'grad_x', 'grad_mix_norm': 'grad_w', 'grad_w_in': 'grad_w', 'grad_q_norm': 'grad_w', 'grad_k_norm': 'grad_w', 'grad_sinks': 'grad_w', 'grad_sgu_norm': 'grad_w', 'grad_w_s': 'grad_w', 'grad_b_s': 'grad_w', 'grad_w_oa': 'grad_w', 'grad_w_ob': 'grad_w', 'grad_w_out': 'grad_w', 'grad_ffn_norm': 'grad_w', 'grad_w_up': 'grad_w', 'grad_conv_w': 'grad_w', 'grad_conv_b': 'grad_w', 'grad_w_down': 'grad_w', 'delta_mix_norm': 'delta_w', 'delta_w_in': 'delta_w', 'delta_q_norm': 'delta_w', 'delta_k_norm': 'delta_w', 'delta_sinks': 'delta_w', 'delta_sgu_norm': 'delta_w', 'delta_w_s': 'delta_w', 'delta_b_s': 'delta_w', 'delta_w_oa': 'delta_w', 'delta_w_ob': 'delta_w', 'delta_w_out': 'delta_w', 'delta_ffn_norm': 'delta_w', 'delta_w_up': 'delta_w', 'delta_conv_w': 'delta_w', 'delta_conv_b': 'delta_w', 'delta_w_down': 'delta_w', 'new_m_mix_norm': 'new_m', 'new_m_w_in': 'new_m', 'new_m_q_norm': 'new_m', 'new_m_k_norm': 'new_m', 'new_m_sinks': 'new_m', 'new_m_sgu_norm': 'new_m', 'new_m_w_s': 'new_m', 'new_m_b_s': 'new_m', 'new_m_w_oa': 'new_m', 'new_m_w_ob': 'new_m', 'new_m_w_out': 'new_m', 'new_m_ffn_norm': 'new_m', 'new_m_w_up': 'new_m', 'new_m_conv_w': 'new_m', 'new_m_conv_b': 'new_m', 'new_m_w_down': 'new_m', 'new_v_mix_norm': 'new_v', 'new_v_w_in': 'new_v', 'new_v_q_norm': 'new_v', 'new_v_k_norm': 'new_v', 'new_v_sinks': 'new_v', 'new_v_sgu_norm': 'new_v', 'new_v_w_s': 'new_v', 'new_v_b_s': 'new_v', 'new_v_w_oa': 'new_v', 'new_v_w_ob': 'new_v', 'new_v_w_out': 'new_v', 'new_v_ffn_norm': 'new_v', 'new_v_w_up': 'new_v', 'new_v_conv_w': 'new_v', 'new_v_conv_b': 'new_v', 'new_v_w_down': 'new_v'}


def _forward(args):
    return _fwd_reference(*[args[k] for k in FWD_PARAMS])


def _output_shape():
    out = _jax.eval_shape(lambda: _forward(_fwd_setup_inputs(0)))
    return out.shape, out.dtype

N_MICROBATCH = 1
ADAM_LR = 0.001
ADAM_B1 = 0.9
ADAM_B2 = 0.999
ADAM_EPS = 1e-08
ADAM_WD = 0.01
ADAM_STEP = 10
PER_EXAMPLE_BATCH_AXIS = {'x': 0, 'loss_target': 0}
SHARED_INPUTS = []
_WEIGHT_DTYPES = {'mix_norm': _jnp.float32, 'w_in': _jnp.float32, 'q_norm': _jnp.float32, 'k_norm': _jnp.float32, 'sinks': _jnp.float32, 'sgu_norm': _jnp.float32, 'w_s': _jnp.float32, 'b_s': _jnp.float32, 'w_oa': _jnp.float32, 'w_ob': _jnp.float32, 'w_out': _jnp.float32, 'ffn_norm': _jnp.float32, 'w_up': _jnp.float32, 'conv_w': _jnp.float32, 'conv_b': _jnp.float32, 'w_down': _jnp.float32}
MOMENT_SCALE = {'mix_norm': 8.085481e+00, 'w_in': 3.736482e-01, 'q_norm': 5.334945e+00, 'k_norm': 5.311041e+00, 'sinks': 2.208999e+01, 'sgu_norm': 3.916703e+00, 'w_s': 1.135302e+00, 'b_s': 4.011993e+00, 'w_oa': 2.859775e-01, 'w_ob': 1.960660e+00, 'w_out': 1.729884e+00, 'ffn_norm': 2.533554e+01, 'w_up': 5.416816e-01, 'conv_w': 3.590569e+00, 'conv_b': 3.338238e+00, 'w_down': 4.461365e-01}


def _to_microbatches(a, axis):
    t = _jnp.moveaxis(a, axis, 0)
    t = t.reshape((N_MICROBATCH, t.shape[0] // N_MICROBATCH) + t.shape[1:])
    return _jnp.moveaxis(t, 1, axis + 1)


def setup_inputs(seed: int = 0) -> dict:
    inp = _fwd_setup_inputs(seed)
    key = _jax.random.fold_in(_jax.random.key(seed), 7919)
    shape, _ = _output_shape()
    out = dict(inp)
    out["loss_target"] = _jax.random.normal(_jax.random.fold_in(key, 0), shape, _jnp.float32)
    for i, name in enumerate(TWIN_WEIGHTS):
        w = inp[name].astype(_jnp.float32)
        if MOMENT_SCALE is None:
            s = _jnp.sqrt(_jnp.mean(_jnp.square(w)) + 1e-30)
        else:
            s = MOMENT_SCALE[name]
        km, kv = _jax.random.split(_jax.random.fold_in(key, i + 1))
        out[name] = w
        out["m_" + name] = s * _jax.random.normal(km, w.shape, _jnp.float32)
        out["v_" + name] = (s * s) * _jax.random.uniform(kv, w.shape, _jnp.float32, 0.5, 1.5)
    if N_MICROBATCH > 1:
        for name, axis in PER_EXAMPLE_BATCH_AXIS.items():
            out[name] = _to_microbatches(out[name], axis)
    return {'x': out['x'], 'mix_norm': out['mix_norm'], 'w_in': out['w_in'], 'q_norm': out['q_norm'], 'k_norm': out['k_norm'], 'sinks': out['sinks'], 'sgu_norm': out['sgu_norm'], 'w_s': out['w_s'], 'b_s': out['b_s'], 'w_oa': out['w_oa'], 'w_ob': out['w_ob'], 'w_out': out['w_out'], 'ffn_norm': out['ffn_norm'], 'w_up': out['w_up'], 'conv_w': out['conv_w'], 'conv_b': out['conv_b'], 'w_down': out['w_down'], 'loss_target': out['loss_target'], 'm_mix_norm': out['m_mix_norm'], 'm_w_in': out['m_w_in'], 'm_q_norm': out['m_q_norm'], 'm_k_norm': out['m_k_norm'], 'm_sinks': out['m_sinks'], 'm_sgu_norm': out['m_sgu_norm'], 'm_w_s': out['m_w_s'], 'm_b_s': out['m_b_s'], 'm_w_oa': out['m_w_oa'], 'm_w_ob': out['m_w_ob'], 'm_w_out': out['m_w_out'], 'm_ffn_norm': out['m_ffn_norm'], 'm_w_up': out['m_w_up'], 'm_conv_w': out['m_conv_w'], 'm_conv_b': out['m_conv_b'], 'm_w_down': out['m_w_down'], 'v_mix_norm': out['v_mix_norm'], 'v_w_in': out['v_w_in'], 'v_q_norm': out['v_q_norm'], 'v_k_norm': out['v_k_norm'], 'v_sinks': out['v_sinks'], 'v_sgu_norm': out['v_sgu_norm'], 'v_w_s': out['v_w_s'], 'v_b_s': out['v_b_s'], 'v_w_oa': out['v_w_oa'], 'v_w_ob': out['v_w_ob'], 'v_w_out': out['v_w_out'], 'v_ffn_norm': out['v_ffn_norm'], 'v_w_up': out['v_w_up'], 'v_conv_w': out['v_conv_w'], 'v_conv_b': out['v_conv_b'], 'v_w_down': out['v_w_down']}


def _loss(weights, diff, rest, loss_target):
    with _jax.named_scope("forward"):
        args = {**rest, TWIN_DIFF_INPUT: diff, **{k: w.astype(_WEIGHT_DTYPES[k]) for k, w in weights.items()}}
        y = _forward(args)
    with _jax.named_scope("loss_head"):
        err = _jnp.square(y.astype(_jnp.float32) - loss_target)
        return 0.5 * _jnp.sum(_jnp.mean(err, axis=-1)) if err.ndim else 0.5 * err


def _adamw(w, g, m, v):
    m = ADAM_B1 * m + (1.0 - ADAM_B1) * g
    v = ADAM_B2 * v + (1.0 - ADAM_B2) * _jnp.square(g)
    m_hat = m / (1.0 - ADAM_B1 ** ADAM_STEP)
    v_hat = v / (1.0 - ADAM_B2 ** ADAM_STEP)
    delta = -ADAM_LR * (m_hat / (_jnp.sqrt(v_hat) + ADAM_EPS) + ADAM_WD * w)
    return delta, m, v


def reference(x, mix_norm, w_in, q_norm, k_norm, sinks, sgu_norm, w_s, b_s, w_oa, w_ob, w_out, ffn_norm, w_up, conv_w, conv_b, w_down, loss_target, m_mix_norm, m_w_in, m_q_norm, m_k_norm, m_sinks, m_sgu_norm, m_w_s, m_b_s, m_w_oa, m_w_ob, m_w_out, m_ffn_norm, m_w_up, m_conv_w, m_conv_b, m_w_down, v_mix_norm, v_w_in, v_q_norm, v_k_norm, v_sinks, v_sgu_norm, v_w_s, v_b_s, v_w_oa, v_w_ob, v_w_out, v_ffn_norm, v_w_up, v_conv_w, v_conv_b, v_w_down):
    given = dict(x=x, mix_norm=mix_norm, w_in=w_in, q_norm=q_norm, k_norm=k_norm, sinks=sinks, sgu_norm=sgu_norm, w_s=w_s, b_s=b_s, w_oa=w_oa, w_ob=w_ob, w_out=w_out, ffn_norm=ffn_norm, w_up=w_up, conv_w=conv_w, conv_b=conv_b, w_down=w_down, loss_target=loss_target, m_mix_norm=m_mix_norm, m_w_in=m_w_in, m_q_norm=m_q_norm, m_k_norm=m_k_norm, m_sinks=m_sinks, m_sgu_norm=m_sgu_norm, m_w_s=m_w_s, m_b_s=m_b_s, m_w_oa=m_w_oa, m_w_ob=m_w_ob, m_w_out=m_w_out, m_ffn_norm=m_ffn_norm, m_w_up=m_w_up, m_conv_w=m_conv_w, m_conv_b=m_conv_b, m_w_down=m_w_down, v_mix_norm=v_mix_norm, v_w_in=v_w_in, v_q_norm=v_q_norm, v_k_norm=v_k_norm, v_sinks=v_sinks, v_sgu_norm=v_sgu_norm, v_w_s=v_w_s, v_b_s=v_b_s, v_w_oa=v_w_oa, v_w_ob=v_w_ob, v_w_out=v_w_out, v_ffn_norm=v_ffn_norm, v_w_up=v_w_up, v_conv_w=v_conv_w, v_conv_b=v_conv_b, v_w_down=v_w_down)
    weights = {n: given[n] for n in TWIN_WEIGHTS}
    shared = {n: given[n] for n in SHARED_INPUTS}
    per_example = {n: given[n] for n in ['x']}
    grad_fn = _jax.value_and_grad(_loss, argnums=(0, 1))

    def one_microbatch(ex, loss_target):
        ex = dict(ex)
        diff = ex.pop(TWIN_DIFF_INPUT)
        return grad_fn(weights, diff, {**shared, **ex}, loss_target)

    if N_MICROBATCH == 1:
        loss, (grad_w, grad_x) = one_microbatch(per_example, given["loss_target"])
    else:
        def body(carry, xs):
            loss_sum, grad_sum = carry
            l_k, (gw_k, gx_k) = one_microbatch(xs[0], xs[1])
            with _jax.named_scope("update"):
                return (loss_sum + l_k, _jax.tree.map(_jnp.add, grad_sum, gw_k)), gx_k

        init = (_jnp.zeros((), _jnp.float32), _jax.tree.map(_jnp.zeros_like, weights))
        (loss, grad_w), grad_x = _jax.lax.scan(body, init, (per_example, given["loss_target"]))
    with _jax.named_scope("update"):
        delta_w, new_m, new_v = {}, {}, {}
        for n in TWIN_WEIGHTS:
            delta_w[n], new_m[n], new_v[n] = _adamw(weights[n], grad_w[n], given["m_" + n], given["v_" + n])
    return (loss, grad_x, *[grad_w[n] for n in TWIN_WEIGHTS], *[delta_w[n] for n in TWIN_WEIGHTS],
            *[new_m[n] for n in TWIN_WEIGHTS], *[new_v[n] for n in TWIN_WEIGHTS])
```

```python
import functools

import numpy as np
import jax
import jax.numpy as jnp
from jax import lax
from jax.experimental import pallas as pl
from jax.experimental.pallas import tpu as pltpu

bf16 = jnp.bfloat16
f32 = jnp.float32

D_MODEL = 1024
ATT_WIDTH = 512
KV_WIDTH = 128
SGU_WIDTH = 512
HEAD_DIM = 64
N_KV_HEADS = 2
Q_GROUP = 4
BLOCK = 128
SGU_GROUPS = 8
IN_WIDTH = 3840
D_FF = 2816
NORM_EPS = 1e-6
NEG_INF = -1e30
N_DEV = 8
N_CHIP = 4

ADAM_LR = 0.001
ADAM_B1 = 0.9
ADAM_B2 = 0.999
ADAM_EPS = 1e-08
ADAM_WD = 0.01
ADAM_STEP = 10

V7X_VMEM_LIMIT = 56 * 1024 * 1024
FF_CHUNK = 256

REGIONS = (
    ("w_in", IN_WIDTH, D_MODEL, True),
    ("w_oa", D_MODEL, ATT_WIDTH, True),
    ("w_ob", D_MODEL, SGU_WIDTH, True),
    ("w_out", D_MODEL, D_MODEL, False),
    ("w_up", 2 * D_FF, D_MODEL, True),
    ("w_down", D_FF, D_MODEL, False),
)
MESH = pl.DeviceIdType.MESH
ANY = pl.BlockSpec(memory_space=pl.ANY)


def _params(sem=None, **kw):
    return pltpu.CompilerParams(dimension_semantics=sem, vmem_limit_bytes=V7X_VMEM_LIMIT, **kw)


def _wspec(rows, cols, layer):
    return pl.BlockSpec((None, rows, cols), lambda *_: (layer, 0, 0), pipeline_mode=pl.Buffered(1))


def _full(shape):
    nd = len(shape)
    return pl.BlockSpec(shape, lambda *_: (0,) * nd)


def _dot_nn(a, b):
    return jnp.dot(a, b, preferred_element_type=f32)


def _dot_nt(a, b):
    return lax.dot_general(a, b, (((1,), (1,)), ((), ())), preferred_element_type=f32)


def _dot_tn(a, b):
    return lax.dot_general(a, b, (((0,), (0,)), ((), ())), preferred_element_type=f32)


_GELU_C = float(np.sqrt(2.0 / np.pi))


def _gelu(x):
    return 0.5 * x * (1.0 + jnp.tanh(_GELU_C * (x + 0.044715 * x * x * x)))


def _gelu_grad(x):
    t = jnp.tanh(_GELU_C * (x + 0.044715 * x * x * x))
    du = _GELU_C * (1.0 + 3.0 * 0.044715 * x * x)
    return 0.5 * (1.0 + t) + 0.5 * x * (1.0 - t * t) * du


def _rms(x):
    return lax.rsqrt(jnp.mean(x * x, axis=-1, keepdims=True) + NORM_EPS)


def _mesh_pos():
    return lax.axis_index("x"), lax.axis_index("y"), lax.axis_index("c")


def _in_proj(x, gain, w_in_t, layer, tm):
    T = x.shape[0]

    def body(x_ref, g_ref, w_ref, qkv_ref, su_ref, sv_ref, ga_ref, gb_ref, h_ref):
        xf = x_ref[...]
        h = (xf * _rms(xf) * g_ref[...]).astype(bf16)
        h_ref[...] = h
        qkv_ref[...] = _dot_nt(h, w_ref[0:768, :])
        su_ref[...] = _dot_nt(h, w_ref[768:1280, :]).astype(bf16)
        sv_ref[...] = _dot_nt(h, w_ref[1280:1792, :]).astype(bf16)
        ga_ref[...] = _dot_nt(h, w_ref[1792:2816, :]).astype(bf16)
        gb_ref[...] = _dot_nt(h, w_ref[2816:3840, :]).astype(bf16)

    row = lambda w: pl.BlockSpec((tm, w), lambda i: (i, 0))
    return pl.pallas_call(
        body, name=f"in_proj_{layer}", grid=(T // tm,),
        in_specs=[row(D_MODEL), _full((1, D_MODEL)), _wspec(IN_WIDTH, D_MODEL, layer)],
        out_specs=[row(768), row(512), row(512), row(1024), row(1024), row(D_MODEL)],
        out_shape=[jax.ShapeDtypeStruct((T, 768), f32), jax.ShapeDtypeStruct((T, 512), bf16),
                   jax.ShapeDtypeStruct((T, 512), bf16), jax.ShapeDtypeStruct((T, 1024), bf16),
                   jax.ShapeDtypeStruct((T, 1024), bf16), jax.ShapeDtypeStruct((T, D_MODEL), bf16)],
        compiler_params=_params(("parallel",)),
    )(x, gain, w_in_t)


def _attn_head_group(cur, prev, qg, kg, sink_ref, n, hk):
    lo = hk * HEAD_DIM
    k_raw = jnp.concatenate([prev[:, lo:lo + HEAD_DIM], cur[:, 512 + lo:512 + lo + HEAD_DIM]], axis=0)
    v_band = jnp.concatenate([prev[:, 128 + lo:128 + lo + HEAD_DIM], cur[:, 640 + lo:640 + lo + HEAD_DIM]], axis=0)
    rk = _rms(k_raw)
    k_hat = k_raw * rk
    kn = (k_hat * kg).astype(bf16)
    q_raw = jnp.concatenate(
        [cur[:, (hk * Q_GROUP + g) * HEAD_DIM:(hk * Q_GROUP + g + 1) * HEAD_DIM] for g in range(Q_GROUP)], axis=0)
    rq = _rms(q_raw)
    q_hat = q_raw * rq
    qn = (q_hat * qg * (HEAD_DIM ** -0.5)).astype(bf16)
    s = _dot_nt(qn, kn)
    rows = lax.broadcasted_iota(jnp.int32, (Q_GROUP * BLOCK, 1), 0)
    g_of_row = rows // BLOCK
    qi = rows - g_of_row * BLOCK
    kj = lax.broadcasted_iota(jnp.int32, (1, 2 * BLOCK), 1)
    dist = qi + BLOCK - kj
    valid = (dist >= 0) & (dist < BLOCK) & ((kj >= BLOCK) | (n > 0))
    slope = jnp.zeros((Q_GROUP * BLOCK, 1), f32)
    sink = jnp.zeros((Q_GROUP * BLOCK, 1), f32)
    for g in range(Q_GROUP):
        head = hk * Q_GROUP + g
        slope = jnp.where(g_of_row == g, float(np.exp2(-8.0 * (head + 1.0) / 8.0)), slope)
        sink = jnp.where(g_of_row == g, sink_ref[head], sink)
    s = jnp.where(valid, s - slope * dist.astype(f32), NEG_INF)
    m = jnp.maximum(jnp.max(s, axis=-1, keepdims=True), sink)
    e = jnp.exp(s - m)
    e_sink = jnp.exp(sink - m)
    inv = 1.0 / (jnp.sum(e, axis=-1, keepdims=True) + e_sink)
    return dict(k_raw=k_raw, rk=rk, k_hat=k_hat, kn=kn, v=v_band.astype(bf16), q_hat=q_hat, rq=rq, qn=qn,
                p=e * inv, p_sink=e_sink * inv)


def _attn_fwd(qkv, qg, kg, sinks, n_seq, seq):
    T = n_seq * seq
    nb = seq // BLOCK

    def body(cur_ref, prev_ref, qg_ref, kg_ref, sink_ref, y_ref):
        n = pl.program_id(1)
        cur = cur_ref[...]
        prev = prev_ref[...]
        pieces = [None] * (N_KV_HEADS * Q_GROUP)
        for hk in range(N_KV_HEADS):
            a = _attn_head_group(cur, prev, qg_ref[...], kg_ref[...], sink_ref, n, hk)
            o = _dot_nn(a["p"].astype(bf16), a["v"])
            for g in range(Q_GROUP):
                pieces[hk * Q_GROUP + g] = o[g * BLOCK:(g + 1) * BLOCK]
        y_ref[...] = jnp.concatenate(pieces, axis=1).astype(bf16)

    return pl.pallas_call(
        body, name="attn_fwd", grid=(n_seq, nb),
        in_specs=[pl.BlockSpec((BLOCK, 768), lambda b, n: (b * nb + n, 0)),
                  pl.BlockSpec((BLOCK, 256), lambda b, n: (b * nb + jnp.maximum(n - 1, 0), 2)),
                  _full((1, HEAD_DIM)), _full((1, HEAD_DIM)),
                  pl.BlockSpec(memory_space=pltpu.SMEM)],
        out_specs=pl.BlockSpec((BLOCK, ATT_WIDTH), lambda b, n: (b * nb + n, 0)),
        out_shape=jax.ShapeDtypeStruct((T, ATT_WIDTH), bf16),
        compiler_params=_params(("parallel", "parallel")),
    )(qkv, qkv, qg, kg, sinks)


def _sgu_chunk(su, sv, gain, w_ref, b_ref):
    u = _gelu(su)
    vg = _gelu(sv)
    rv = _rms(vg)
    v_hat = vg * rv
    vn = (v_hat * gain).astype(bf16)
    causal = (lax.broadcasted_iota(jnp.int32, (BLOCK, BLOCK), 0) >= lax.broadcasted_iota(jnp.int32, (BLOCK, BLOCK), 1))
    w_tril = [jnp.where(causal, w_ref[g], 0.0).astype(bf16) for g in range(SGU_GROUPS)]
    gd = SGU_WIDTH // SGU_GROUPS
    mixed = jnp.concatenate(
        [_dot_nn(w_tril[g], vn[:, g * gd:(g + 1) * gd]) + b_ref[g] for g in range(SGU_GROUPS)], axis=1)
    return u, rv, v_hat, vn, w_tril, mixed


def _sgu_fwd(su, sv, gain, w_s, b_s, tm):
    T = su.shape[0]

    def body(su_ref, sv_ref, g_ref, w_ref, b_ref, y_ref):
        for ch in range(tm // BLOCK):
            rows = slice(ch * BLOCK, (ch + 1) * BLOCK)
            u, _, _, _, _, mixed = _sgu_chunk(su_ref[rows, :].astype(f32), sv_ref[rows, :].astype(f32),
                                              g_ref[...], w_ref, b_ref)
            y_ref[rows, :] = (u * mixed).astype(bf16)

    row = pl.BlockSpec((tm, SGU_WIDTH), lambda i: (i, 0))
    return pl.pallas_call(
        body, name="sgu_fwd", grid=(T // tm,),
        in_specs=[row, row, _full((1, SGU_WIDTH)), _full((SGU_GROUPS, BLOCK, BLOCK)), _full((SGU_GROUPS, BLOCK, 1))],
        out_specs=row, out_shape=jax.ShapeDtypeStruct((T, SGU_WIDTH), bf16),
        compiler_params=_params(("parallel",)),
    )(su, sv, gain, w_s, b_s)


def _merge_fwd(x, y_att, y_sgu, ga, gb, w_oa_t, w_ob_t, w_out, layer, tm):
    T = x.shape[0]

    def body(x_ref, ya_ref, ys_ref, ga_ref, gb_ref, woa_ref, wob_ref, wout_ref, x1_ref, m_ref, a_ref, b_ref):
        a = _dot_nt(ya_ref[...], woa_ref[...])
        b = _dot_nt(ys_ref[...], wob_ref[...])
        a_ref[...] = a.astype(bf16)
        b_ref[...] = b.astype(bf16)
        merged = (jax.nn.sigmoid(ga_ref[...].astype(f32)) * a + jax.nn.sigmoid(gb_ref[...].astype(f32)) * b).astype(bf16)
        m_ref[...] = merged
        x1_ref[...] = x_ref[...] + _dot_nn(merged, wout_ref[...])

    row = lambda w: pl.BlockSpec((tm, w), lambda i: (i, 0))
    return pl.pallas_call(
        body, name=f"merge_fwd_{layer}", grid=(T // tm,),
        in_specs=[row(D_MODEL), row(512), row(512), row(1024), row(1024),
                  _wspec(D_MODEL, ATT_WIDTH, layer), _wspec(D_MODEL, SGU_WIDTH, layer), _wspec(D_MODEL, D_MODEL, layer)],
        out_specs=[row(D_MODEL)] * 4,
        out_shape=[jax.ShapeDtypeStruct((T, D_MODEL), f32)] + [jax.ShapeDtypeStruct((T, D_MODEL), bf16)] * 3,
        compiler_params=_params(("parallel",)),
    )(x, y_att, y_sgu, ga, gb, w_oa_t, w_ob_t, w_out)


def _conv_taps(zz, h1, h2, rowid):
    z1 = jnp.where(rowid == 0, h1, pltpu.roll(zz, 1, 0))
    z2 = jnp.where(rowid == 0, h2, jnp.where(rowid == 1, h1, pltpu.roll(zz, 2, 0)))
    return z1, z2


def _ffn_up(x1, gain, w_up_t, conv_w, conv_b, layer, seq, tm):
    T = x1.shape[0]
    tps = seq // tm

    def body(x_ref, g_ref, w_ref, cw_ref, cb_ref, h2_ref, z_ref, act_ref, carry_ref):
        i = pl.program_id(0)

        @pl.when(i % tps == 0)
        def _():
            carry_ref[...] = jnp.zeros_like(carry_ref)

        xf = x_ref[...]
        h2 = (xf * _rms(xf) * g_ref[...]).astype(bf16)
        h2_ref[...] = h2
        rowid = lax.broadcasted_iota(jnp.int32, (tm, 1), 0)
        for cc in range(D_FF // FF_CHUNK):
            zc = []
            for part in range(2):
                lo = part * D_FF + cc * FF_CHUNK
                cols = slice(lo, lo + FF_CHUNK)
                zb = _dot_nt(h2, w_ref[cols, :]).astype(bf16)
                z_ref[:, cols] = zb
                zz = zb.astype(f32)
                halo = carry_ref[:, cols]
                z1, z2 = _conv_taps(zz, halo[7:8], halo[6:7], rowid)
                carry_ref[:, cols] = zz[tm - 8:tm]
                zc.append(cb_ref[:, cols] + cw_ref[0:1, cols] * z2 + cw_ref[1:2, cols] * z1 + cw_ref[2:3, cols] * zz)
            act_ref[:, cc * FF_CHUNK:(cc + 1) * FF_CHUNK] = (zc[0] * jax.nn.sigmoid(zc[0]) * zc[1]).astype(bf16)

    row = lambda w: pl.BlockSpec((tm, w), lambda i: (i, 0))
    return pl.pallas_call(
        body, name=f"ffn_up_{layer}", grid=(T // tm,),
        in_specs=[row(D_MODEL), _full((1, D_MODEL)), _wspec(2 * D_FF, D_MODEL, layer),
                  _full((3, 2 * D_FF)), _full((1, 2 * D_FF))],
        out_specs=[row(D_MODEL), row(2 * D_FF), row(D_FF)],
        out_shape=[jax.ShapeDtypeStruct((T, D_MODEL), bf16), jax.ShapeDtypeStruct((T, 2 * D_FF), bf16),
                   jax.ShapeDtypeStruct((T, D_FF), bf16)],
        scratch_shapes=[pltpu.VMEM((8, 2 * D_FF), f32)],
        compiler_params=_params(("arbitrary",)),
    )(x1, gain, w_up_t, conv_w, conv_b)


def _ffn_down(x1, act, w_down, layer, tm):
    T = x1.shape[0]

    def body(x_ref, a_ref, w_ref, o_ref):
        o_ref[...] = x_ref[...] + _dot_nn(a_ref[...], w_ref[...])

    row = lambda w: pl.BlockSpec((tm, w), lambda i: (i, 0))
    return pl.pallas_call(
        body, name=f"ffn_down_{layer}", grid=(T // tm,),
        in_specs=[row(D_MODEL), row(D_FF), _wspec(D_FF, D_MODEL, layer)],
        out_specs=row(D_MODEL), out_shape=jax.ShapeDtypeStruct((T, D_MODEL), f32),
        compiler_params=_params(("parallel",)),
    )(x1, act, w_down)


def _loss_head(y, target, tm):
    T = y.shape[0]

    def body(y_ref, t_ref, dy_ref, dyb_ref, loss_ref):
        @pl.when(pl.program_id(0) == 0)
        def _():
            loss_ref[...] = jnp.zeros_like(loss_ref)

        diff = y_ref[...] - t_ref[...]
        loss_ref[...] += 0.5 * jnp.sum(jnp.mean(diff * diff, axis=-1, keepdims=True), axis=0, keepdims=True)
        dy = diff * (1.0 / D_MODEL)
        dy_ref[...] = dy
        dyb_ref[...] = dy.astype(bf16)

    row = pl.BlockSpec((tm, D_MODEL), lambda i: (i, 0))
    return pl.pallas_call(
        body, name="loss_head", grid=(T // tm,),
        in_specs=[row, row], out_specs=[row, row, _full((8, 128))],
        out_shape=[jax.ShapeDtypeStruct((T, D_MODEL), f32), jax.ShapeDtypeStruct((T, D_MODEL), bf16),
                   jax.ShapeDtypeStruct((8, 128), f32)],
        compiler_params=_params(("arbitrary",)),
    )(y, target)


def _ffn_bwd(dx2b, z, conv_w, conv_b, w_down, layer, seq, tm):
    T = z.shape[0]
    nt = T // tm
    tps = seq // tm

    def body(dx_ref, z_ref, zh_ref, cw_ref, cb_ref, wd_ref, dz_ref, dconv_ref, carry_ref):
        i = pl.program_id(0)
        pos = (nt - 1 - i) % tps

        @pl.when(i == 0)
        def _():
            dconv_ref[...] = jnp.zeros_like(dconv_ref)

        @pl.when(pos == tps - 1)
        def _():
            carry_ref[...] = jnp.zeros_like(carry_ref)

        dxb = dx_ref[...]
        rowid = lax.broadcasted_iota(jnp.int32, (tm, 1), 0)
        halo_on = (pos > 0).astype(f32)
        for cc in range(D_FF // FF_CHUNK):
            zz, z1, z2, zc, colss = [], [], [], [], []
            for part in range(2):
                lo = part * D_FF + cc * FF_CHUNK
                cols = slice(lo, lo + FF_CHUNK)
                colss.append(cols)
                zp = z_ref[:, cols].astype(f32)
                halo = zh_ref[:, cols].astype(f32) * halo_on
                a1, a2 = _conv_taps(zp, halo[7:8], halo[6:7], rowid)
                zz.append(zp), z1.append(a1), z2.append(a2)
                zc.append(cb_ref[:, cols] + cw_ref[0:1, cols] * a2 + cw_ref[1:2, cols] * a1 + cw_ref[2:3, cols] * zp)
            d_act = _dot_nt(dxb, wd_ref[cc * FF_CHUNK:(cc + 1) * FF_CHUNK, :])
            sg = jax.nn.sigmoid(zc[0])
            silu = zc[0] * sg
            dzc = [d_act * zc[1] * sg * (1.0 + zc[0] * (1.0 - sg)), d_act * silu]
            for part in range(2):
                cols = colss[part]
                g = dzc[part]
                dconv_ref[0:1, cols] += jnp.sum(g * z2[part], axis=0, keepdims=True)
                dconv_ref[1:2, cols] += jnp.sum(g * z1[part], axis=0, keepdims=True)
                dconv_ref[2:3, cols] += jnp.sum(g * zz[part], axis=0, keepdims=True)
                dconv_ref[3:4, cols] += jnp.sum(g, axis=0, keepdims=True)
                nxt = carry_ref[:, cols]
                d1 = jnp.where(rowid == tm - 1, nxt[0:1], pltpu.roll(g, tm - 1, 0))
                d2 = jnp.where(rowid == tm - 1, nxt[1:2], jnp.where(rowid == tm - 2, nxt[0:1], pltpu.roll(g, tm - 2, 0)))
                carry_ref[:, cols] = g[0:8]
                dz_ref[:, cols] = (cw_ref[2:3, cols] * g + cw_ref[1:2, cols] * d1 + cw_ref[0:1, cols] * d2).astype(bf16)

    rev = lambda w: pl.BlockSpec((tm, w), lambda i: (nt - 1 - i, 0))
    return pl.pallas_call(
        body, name=f"ffn_bwd_{layer}", grid=(nt,),
        in_specs=[rev(D_MODEL), rev(2 * D_FF),
                  pl.BlockSpec((8, 2 * D_FF), lambda i: (jnp.maximum((nt - 1 - i) * (tm // 8) - 1, 0), 0)),
                  _full((3, 2 * D_FF)), _full((1, 2 * D_FF)), _wspec(D_FF, D_MODEL, layer)],
        out_specs=[rev(2 * D_FF), _full((8, 2 * D_FF))],
        out_shape=[jax.ShapeDtypeStruct((T, 2 * D_FF), bf16), jax.ShapeDtypeStruct((8, 2 * D_FF), f32)],
        scratch_shapes=[pltpu.VMEM((8, 2 * D_FF), f32)],
        compiler_params=_params(("arbitrary",)),
    )(dx2b, z, z, conv_w, conv_b, w_down)


def _norm_bwd(dy, w, layer, x, gain, dres, tm, name):
    T, K = dy.shape

    def body(dy_ref, w_ref, x_ref, g_ref, dres_ref, dx_ref, dxb_ref, dg_ref):
        @pl.when(pl.program_id(0) == 0)
        def _():
            dg_ref[...] = jnp.zeros_like(dg_ref)

        dh = _dot_nn(dy_ref[...], w_ref[...])
        xf = x_ref[...]
        r = _rms(xf)
        x_hat = xf * r
        dg_ref[...] += jnp.sum(dh * x_hat, axis=0, keepdims=True)
        dxh = dh * g_ref[...]
        dx = dres_ref[...] + r * (dxh - x_hat * jnp.mean(dxh * x_hat, axis=-1, keepdims=True))
        dx_ref[...] = dx
        dxb_ref[...] = dx.astype(bf16)

    row = lambda w_: pl.BlockSpec((tm, w_), lambda i: (i, 0))
    return pl.pallas_call(
        body, name=name, grid=(T // tm,),
        in_specs=[row(K), _wspec(K, D_MODEL, layer), row(D_MODEL), _full((1, D_MODEL)), row(D_MODEL)],
        out_specs=[row(D_MODEL), row(D_MODEL), _full((1, D_MODEL))],
        out_shape=[jax.ShapeDtypeStruct((T, D_MODEL), f32), jax.ShapeDtypeStruct((T, D_MODEL), bf16),
                   jax.ShapeDtypeStruct((1, D_MODEL), f32)],
        compiler_params=_params(("arbitrary",)),
    )(dy, w, x, gain, dres)


def _merge_bwd(dx1b, ga, gb, a, b, w_oa_t, w_ob_t, w_out, layer, tm):
    T = dx1b.shape[0]

    def body(dx_ref, ga_ref, gb_ref, a_ref, b_ref, woa_ref, wob_ref, wout_ref,
             da_ref, db_ref, dga_ref, dgb_ref, dya_ref, dys_ref):
        dm = _dot_nt(dx_ref[...], wout_ref[...])
        sa = jax.nn.sigmoid(ga_ref[...].astype(f32))
        sb = jax.nn.sigmoid(gb_ref[...].astype(f32))
        da = (dm * sa).astype(bf16)
        db = (dm * sb).astype(bf16)
        da_ref[...] = da
        db_ref[...] = db
        dga_ref[...] = (dm * a_ref[...].astype(f32) * sa * (1.0 - sa)).astype(bf16)
        dgb_ref[...] = (dm * b_ref[...].astype(f32) * sb * (1.0 - sb)).astype(bf16)
        dya_ref[...] = _dot_nn(da, woa_ref[...]).astype(bf16)
        dys_ref[...] = _dot_nn(db, wob_ref[...]).astype(bf16)

    row = lambda w: pl.BlockSpec((tm, w), lambda i: (i, 0))
    return pl.pallas_call(
        body, name=f"merge_bwd_{layer}", grid=(T // tm,),
        in_specs=[row(D_MODEL)] * 5 + [_wspec(D_MODEL, ATT_WIDTH, layer), _wspec(D_MODEL, SGU_WIDTH, layer),
                                       _wspec(D_MODEL, D_MODEL, layer)],
        out_specs=[row(D_MODEL)] * 4 + [row(512)] * 2,
        out_shape=[jax.ShapeDtypeStruct((T, D_MODEL), bf16)] * 4 + [jax.ShapeDtypeStruct((T, 512), bf16)] * 2,
        compiler_params=_params(("parallel",)),
    )(dx1b, ga, gb, a, b, w_oa_t, w_ob_t, w_out)


def _sgu_bwd(dy, su, sv, gain, w_s, b_s, tm):
    T = su.shape[0]
    gd = SGU_WIDTH // SGU_GROUPS

    def body(dy_ref, su_ref, sv_ref, g_ref, w_ref, b_ref, dsu_ref, dsv_ref, dw_ref, db_ref, dg_ref):
        @pl.when(pl.program_id(0) == 0)
        def _():
            dw_ref[...] = jnp.zeros_like(dw_ref)
            db_ref[...] = jnp.zeros_like(db_ref)
            dg_ref[...] = jnp.zeros_like(dg_ref)

        gain_v = g_ref[...]
        for ch in range(tm // BLOCK):
            rows = slice(ch * BLOCK, (ch + 1) * BLOCK)
            su_c = su_ref[rows, :].astype(f32)
            sv_c = sv_ref[rows, :].astype(f32)
            u, rv, v_hat, vn, w_tril, mixed = _sgu_chunk(su_c, sv_c, gain_v, w_ref, b_ref)
            dyc = dy_ref[rows, :].astype(f32)
            dsu_ref[rows, :] = (dyc * mixed * _gelu_grad(su_c)).astype(bf16)
            dmix = dyc * u
            dmix_b = dmix.astype(bf16)
            dvn = []
            for g in range(SGU_GROUPS):
                gs = slice(g * gd, (g + 1) * gd)
                db_ref[g] += jnp.sum(dmix[:, gs], axis=1, keepdims=True)
                dw_ref[g] += _dot_nt(dmix_b[:, gs], vn[:, gs])
                dvn.append(_dot_tn(w_tril[g], dmix_b[:, gs]))
            dvn = jnp.concatenate(dvn, axis=1)
            dg_ref[...] += jnp.sum(dvn * v_hat, axis=0, keepdims=True)
            dxh = dvn * gain_v
            dvg = rv * (dxh - v_hat * jnp.mean(dxh * v_hat, axis=-1, keepdims=True))
            dsv_ref[rows, :] = (dvg * _gelu_grad(sv_c)).astype(bf16)

    row = pl.BlockSpec((tm, SGU_WIDTH), lambda i: (i, 0))
    return pl.pallas_call(
        body, name="sgu_bwd", grid=(T // tm,),
        in_specs=[row, row, row, _full((1, SGU_WIDTH)), _full((SGU_GROUPS, BLOCK, BLOCK)),
                  _full((SGU_GROUPS, BLOCK, 1))],
        out_specs=[row, row, _full((SGU_GROUPS, BLOCK, BLOCK)), _full((SGU_GROUPS, BLOCK, 1)), _full((1, SGU_WIDTH))],
        out_shape=[jax.ShapeDtypeStruct((T, SGU_WIDTH), bf16)] * 2 + [
            jax.ShapeDtypeStruct((SGU_GROUPS, BLOCK, BLOCK), f32), jax.ShapeDtypeStruct((SGU_GROUPS, BLOCK, 1), f32),
            jax.ShapeDtypeStruct((1, SGU_WIDTH), f32)],
        compiler_params=_params(("arbitrary",)),
    )(dy, su, sv, gain, w_s, b_s)


def _attn_bwd(dy, qkv, qg, kg, sinks, n_seq, seq):
    T = n_seq * seq
    nb = seq // BLOCK
    scale = HEAD_DIM ** -0.5

    def body(dy_ref, cur_ref, prev_ref, qg_ref, kg_ref, sink_ref, dqkv_ref, dqg_ref, dkg_ref, dsink_ref,
             carry_k, carry_v):
        b = pl.program_id(0)
        j = pl.program_id(1)
        n = nb - 1 - j

        @pl.when((b == 0) & (j == 0))
        def _():
            dqg_ref[...] = jnp.zeros_like(dqg_ref)
            dkg_ref[...] = jnp.zeros_like(dkg_ref)
            dsink_ref[...] = jnp.zeros_like(dsink_ref)

        @pl.when(j == 0)
        def _():
            carry_k[...] = jnp.zeros_like(carry_k)
            carry_v[...] = jnp.zeros_like(carry_v)

        cur = cur_ref[...]
        prev = prev_ref[...]
        dyf = dy_ref[...].astype(f32)
        qg_v = qg_ref[...]
        kg_v = kg_ref[...]
        dq_pieces = [None] * (N_KV_HEADS * Q_GROUP)
        dk_pieces, dv_pieces = [], []
        for hk in range(N_KV_HEADS):
            a = _attn_head_group(cur, prev, qg_v, kg_v, sink_ref, n, hk)
            do = jnp.concatenate(
                [dyf[:, (hk * Q_GROUP + g) * HEAD_DIM:(hk * Q_GROUP + g + 1) * HEAD_DIM] for g in range(Q_GROUP)],
                axis=0).astype(bf16)
            p = a["p"]
            dp = _dot_nt(do, a["v"])
            dv_band = _dot_tn(p.astype(bf16), do)
            dsum = jnp.sum(p * dp, axis=-1, keepdims=True)
            ds = (p * (dp - dsum)).astype(bf16)
            dsink_col = -a["p_sink"] * dsum
            for g in range(Q_GROUP):
                head = hk * Q_GROUP + g
                dsink_ref[head:head + 1, :] += jnp.sum(dsink_col[g * BLOCK:(g + 1) * BLOCK], axis=0, keepdims=True)
            dqn = _dot_nn(ds, a["kn"])
            dkn_band = _dot_tn(ds, a["qn"])
            dq_hat_g = dqn * scale
            dqg_ref[...] += jnp.sum(dq_hat_g * a["q_hat"], axis=0, keepdims=True)
            dxh = dq_hat_g * qg_v
            dq = a["rq"] * (dxh - a["q_hat"] * jnp.mean(dxh * a["q_hat"], axis=-1, keepdims=True))
            for g in range(Q_GROUP):
                dq_pieces[hk * Q_GROUP + g] = dq[g * BLOCK:(g + 1) * BLOCK]
            dkn = dkn_band[BLOCK:] + carry_k[hk]
            dv_pieces.append(dv_band[BLOCK:] + carry_v[hk])
            carry_k[hk] = dkn_band[:BLOCK]
            carry_v[hk] = dv_band[:BLOCK]
            k_hat = a["k_hat"][BLOCK:]
            dkg_ref[...] += jnp.sum(dkn * k_hat, axis=0, keepdims=True)
            dxk = dkn * kg_v
            dk_pieces.append(a["rk"][BLOCK:] * (dxk - k_hat * jnp.mean(dxk * k_hat, axis=-1, keepdims=True)))
        dqkv_ref[...] = jnp.concatenate(dq_pieces + dk_pieces + dv_pieces, axis=1).astype(bf16)

    blk = lambda w: pl.BlockSpec((BLOCK, w), lambda b, j: (b * nb + nb - 1 - j, 0))
    return pl.pallas_call(
        body, name="attn_bwd", grid=(n_seq, nb),
        in_specs=[blk(ATT_WIDTH), blk(768),
                  pl.BlockSpec((BLOCK, 256), lambda b, j: (b * nb + jnp.maximum(nb - 2 - j, 0), 2)),
                  _full((1, HEAD_DIM)), _full((1, HEAD_DIM)), pl.BlockSpec(memory_space=pltpu.SMEM)],
        out_specs=[blk(768), _full((1, HEAD_DIM)), _full((1, HEAD_DIM)), _full((8, 128))],
        out_shape=[jax.ShapeDtypeStruct((T, 768), bf16), jax.ShapeDtypeStruct((1, HEAD_DIM), f32),
                   jax.ShapeDtypeStruct((1, HEAD_DIM), f32), jax.ShapeDtypeStruct((8, 128), f32)],
        scratch_shapes=[pltpu.VMEM((N_KV_HEADS, BLOCK, HEAD_DIM), f32), pltpu.VMEM((N_KV_HEADS, BLOCK, HEAD_DIM), f32)],
        compiler_params=_params(("arbitrary", "arbitrary")),
    )(dy, qkv, qkv, qg, kg, sinks)


def _weight_grad(a, b, layer, prev, tm, tk, name):
    T, M = a.shape
    N = b.shape[1]
    nk = T // tk

    def body(*refs):
        a_ref, b_ref = refs[0], refs[1]
        o_ref, acc_ref = refs[-2], refs[-1]
        k = pl.program_id(1)

        @pl.when(k == 0)
        def _():
            acc_ref[...] = jnp.zeros_like(acc_ref)

        acc_ref[...] += _dot_tn(a_ref[...], b_ref[...])

        @pl.when(k == nk - 1)
        def _():
            o_ref[...] = acc_ref[...].astype(bf16)

    in_specs = [pl.BlockSpec((tk, tm), lambda i, k: (k, i)), pl.BlockSpec((tk, N), lambda i, k: (k, 0))]
    args = [a, b]
    aliases = {}
    if prev is not None:
        in_specs.append(ANY)
        args.append(prev)
        aliases = {2: 0}
    return pl.pallas_call(
        body, name=name, grid=(M // tm, nk),
        in_specs=in_specs, out_specs=pl.BlockSpec((None, tm, N), lambda i, k: (layer, i, 0)),
        out_shape=jax.ShapeDtypeStruct((2, M, N), bf16),
        scratch_shapes=[pltpu.VMEM((tm, N), f32)],
        input_output_aliases=aliases,
        compiler_params=_params(("parallel", "arbitrary")),
    )(*args)


def _all_gather_weights(pieces, conv_w):
    nr = len(pieces)
    per = 7
    n_sem = 2 * nr * per

    def body(*refs):
        src = refs[:nr]
        cw_src = refs[nr]
        out = refs[nr + 1:2 * nr + 1]
        cw_out = refs[2 * nr + 1]
        send, recv, local, cw_send, cw_recv = refs[2 * nr + 2:]
        x, y, c = _mesh_pos()
        me = 4 * x + 2 * y + c
        sibling = (x, y, 1 - c)
        chips = [(1 - x, y), (x, 1 - y), (1 - x, 1 - y)]

        def copy(layer, r, k, piece_of, to, from_src):
            hr = pieces[r].shape[1]
            p = 4 * piece_of[0] + 2 * piece_of[1] + piece_of[2]
            rows = out[r].at[layer, pl.ds(pl.multiple_of(p * hr, 16), hr), :]
            idx = (layer * nr + r) * per + k
            return pltpu.make_async_remote_copy(
                src_ref=src[r].at[layer] if from_src else rows, dst_ref=rows,
                send_sem=send.at[idx], recv_sem=recv.at[idx], device_id=to, device_id_type=MESH)

        started = []
        locals_ = []
        for layer in range(2):
            for r in range(nr):
                hr = pieces[r].shape[1]
                mine = pltpu.make_async_copy(
                    src[r].at[layer], out[r].at[layer, pl.ds(pl.multiple_of(me * hr, 16), hr), :],
                    local.at[layer * nr + r])
                mine.start()
                locals_.append(mine)
                first = [copy(layer, r, 0, (x, y, c), sibling, True)]
                first += [copy(layer, r, 1 + j, (x, y, c), (*chip, c), True) for j, chip in enumerate(chips)]
                for cp in first:
                    cp.start()
                started += first
        cw_first = []
        for j, chip in enumerate(chips):
            cp = pltpu.make_async_remote_copy(
                src_ref=cw_src, dst_ref=cw_out.at[2 * x + y], send_sem=cw_send.at[j], recv_sem=cw_recv.at[j],
                device_id=(*chip, c), device_id_type=MESH)
            cp.start()
            cw_first.append(cp)
        cw_mine = pltpu.make_async_copy(cw_src, cw_out.at[2 * x + y], local.at[2 * nr])
        cw_mine.start()
        for layer in range(2):
            for r in range(nr):
                for j, chip in enumerate(chips):
                    copy(layer, r, 1 + j, (*chip, c), (x, y, c), False).wait_recv()
                    fwd = copy(layer, r, 4 + j, (*chip, c), sibling, False)
                    fwd.start()
                    started.append(fwd)
        for layer in range(2):
            for r in range(nr):
                copy(layer, r, 0, (x, y, 1 - c), (x, y, c), False).wait_recv()
                for j, chip in enumerate(chips):
                    copy(layer, r, 4 + j, (*chip, 1 - c), (x, y, c), False).wait_recv()
        for j, chip in enumerate(chips):
            pltpu.make_async_remote_copy(
                src_ref=cw_src, dst_ref=cw_out.at[2 * chip[0] + chip[1]], send_sem=cw_send.at[j],
                recv_sem=cw_recv.at[j], device_id=(*chip, c), device_id_type=MESH).wait_recv()
        for cp in started + cw_first:
            cp.wait_send()
        for cp in locals_:
            cp.wait()
        cw_mine.wait()

    out_shape = [jax.ShapeDtypeStruct((2, N_DEV * p.shape[1], p.shape[2]), bf16) for p in pieces]
    out_shape.append(jax.ShapeDtypeStruct((N_CHIP,) + conv_w.shape, f32))
    res = pl.pallas_call(
        body, name="all_gather_weights",
        in_specs=[ANY] * (nr + 1), out_specs=[ANY] * (nr + 1), out_shape=out_shape,
        scratch_shapes=[pltpu.SemaphoreType.DMA((n_sem,)), pltpu.SemaphoreType.DMA((n_sem,)),
                        pltpu.SemaphoreType.DMA((2 * nr + 1,)), pltpu.SemaphoreType.DMA((3,)),
                        pltpu.SemaphoreType.DMA((3,))],
        compiler_params=pltpu.CompilerParams(has_side_effects=True),
    )(*pieces, conv_w)
    return res[:nr], res[nr]


def _pair_exchange(grads):
    nr = len(grads)
    n_sem = nr * 2 * N_CHIP

    def body(*refs):
        src = refs[:nr]
        out = refs[nr:2 * nr]
        send, recv = refs[2 * nr:]
        x, y, c = _mesh_pos()
        copies = []
        for r in range(nr):
            hr = grads[r].shape[1] // N_DEV
            for layer in range(2):
                for j in range(N_CHIP):
                    idx = (r * 2 + layer) * N_CHIP + j
                    start = pl.multiple_of((2 * j + 1 - c) * hr, 16)
                    cp = pltpu.make_async_remote_copy(
                        src_ref=src[r].at[layer, pl.ds(start, hr), :], dst_ref=out[r].at[layer, j],
                        send_sem=send.at[idx], recv_sem=recv.at[idx], device_id=(x, y, 1 - c), device_id_type=MESH)
                    cp.start()
                    copies.append(cp)
        for cp in copies:
            cp.wait()

    return pl.pallas_call(
        body, name="grad_pair_exchange",
        in_specs=[ANY] * nr, out_specs=[ANY] * nr,
        out_shape=[jax.ShapeDtypeStruct((2, N_CHIP, g.shape[1] // N_DEV, g.shape[2]), bf16) for g in grads],
        scratch_shapes=[pltpu.SemaphoreType.DMA((n_sem,)), pltpu.SemaphoreType.DMA((n_sem,))],
        compiler_params=pltpu.CompilerParams(has_side_effects=True),
    )(*grads)


def _pair_sum(grad, other, core, name):
    _, rows, cols = grad.shape
    hr = rows // N_DEV
    g5 = grad.reshape(2, N_CHIP, 2, hr, cols)

    def body(core_ref, g_ref, o_ref, s_ref):
        s_ref[...] = (g_ref[...].astype(f32) + o_ref[...].astype(f32)).astype(bf16)

    return pl.pallas_call(
        body, name=name,
        grid_spec=pltpu.PrefetchScalarGridSpec(
            num_scalar_prefetch=1, grid=(2, N_CHIP),
            in_specs=[pl.BlockSpec((None, None, None, hr, cols), lambda l, j, core_ref: (l, j, core_ref[0], 0, 0)),
                      pl.BlockSpec((None, None, hr, cols), lambda l, j, core_ref: (l, j, 0, 0))],
            out_specs=pl.BlockSpec((None, None, hr, cols), lambda l, j, core_ref: (l, j, 0, 0))),
        out_shape=jax.ShapeDtypeStruct((2, N_CHIP, hr, cols), bf16),
        compiler_params=_params(("parallel", "parallel")),
    )(core, g5, other)


def _chip_exchange(sums):
    nr = len(sums)
    n_sem = nr * 2 * 3

    def body(*refs):
        src = refs[:nr]
        out = refs[nr:2 * nr]
        send, recv, local = refs[2 * nr:]
        x, y, c = _mesh_pos()
        my_chip = 2 * x + y
        chips = [(1 - x, y), (x, 1 - y), (1 - x, 1 - y)]
        copies = []
        for r in range(nr):
            for layer in range(2):
                mine = pltpu.make_async_copy(src[r].at[layer, my_chip], out[r].at[layer, my_chip], local.at[r * 2 + layer])
                mine.start()
                copies.append(mine)
                for k, chip in enumerate(chips):
                    idx = (r * 2 + layer) * 3 + k
                    cp = pltpu.make_async_remote_copy(
                        src_ref=src[r].at[layer, 2 * chip[0] + chip[1]], dst_ref=out[r].at[layer, my_chip],
                        send_sem=send.at[idx], recv_sem=recv.at[idx], device_id=(*chip, c), device_id_type=MESH)
                    cp.start()
                    copies.append(cp)
        for cp in copies:
            cp.wait()

    return pl.pallas_call(
        body, name="grad_chip_exchange",
        in_specs=[ANY] * nr, out_specs=[ANY] * nr,
        out_shape=[jax.ShapeDtypeStruct(s.shape, bf16) for s in sums],
        scratch_shapes=[pltpu.SemaphoreType.DMA((n_sem,)), pltpu.SemaphoreType.DMA((n_sem,)),
                        pltpu.SemaphoreType.DMA((nr * 2,))],
        compiler_params=pltpu.CompilerParams(has_side_effects=True),
    )(*sums)


def _chip_sum(parts, name):
    _, _, hr, cols = parts.shape

    def body(p_ref, o_ref):
        acc = p_ref[0].astype(f32) + p_ref[1].astype(f32)
        acc = acc + p_ref[2].astype(f32)
        o_ref[...] = acc + p_ref[3].astype(f32)

    return pl.pallas_call(
        body, name=name, grid=(2,),
        in_specs=[pl.BlockSpec((None, N_CHIP, hr, cols), lambda l: (l, 0, 0, 0))],
        out_specs=pl.BlockSpec((None, hr, cols), lambda l: (l, 0, 0)),
        out_shape=jax.ShapeDtypeStruct((2, hr, cols), f32),
        compiler_params=_params(("parallel",)),
    )(parts)


def _share_halves(halves):
    nr = len(halves)

    def body(*refs):
        src = refs[:nr]
        out = refs[nr:2 * nr]
        send, recv, local = refs[2 * nr:]
        x, y, c = _mesh_pos()
        copies = []
        for r in range(nr):
            for layer in range(2):
                idx = r * 2 + layer
                mine = pltpu.make_async_copy(src[r].at[layer], out[r].at[layer, c], local.at[idx])
                mine.start()
                cp = pltpu.make_async_remote_copy(
                    src_ref=src[r].at[layer], dst_ref=out[r].at[layer, c], send_sem=send.at[idx],
                    recv_sem=recv.at[idx], device_id=(x, y, 1 - c), device_id_type=MESH)
                cp.start()
                copies += [mine, cp]
        for cp in copies:
            cp.wait()

    return pl.pallas_call(
        body, name="grad_share_halves",
        in_specs=[ANY] * nr, out_specs=[ANY] * nr,
        out_shape=[jax.ShapeDtypeStruct((2, 2) + h.shape[1:], f32) for h in halves],
        scratch_shapes=[pltpu.SemaphoreType.DMA((nr * 2,))] * 3,
        compiler_params=pltpu.CompilerParams(has_side_effects=True),
    )(*halves)


def _gather_small(pack):
    def body(src, out, send, recv, local):
        x, y, c = _mesh_pos()
        me = 4 * x + 2 * y + c
        mine = pltpu.make_async_copy(src, out.at[me], local)
        mine.start()
        copies = []
        k = 0
        for fx in range(2):
            for fy in range(2):
                for fc in range(2):
                    if fx == 0 and fy == 0 and fc == 0:
                        continue
                    to = (1 - x if fx else x, 1 - y if fy else y, 1 - c if fc else c)
                    cp = pltpu.make_async_remote_copy(
                        src_ref=src, dst_ref=out.at[me], send_sem=send.at[k], recv_sem=recv.at[k],
                        device_id=to, device_id_type=MESH)
                    cp.start()
                    copies.append((cp, to))
                    k += 1
        for k, (cp, to) in enumerate(copies):
            cp.wait_send()
            pltpu.make_async_remote_copy(
                src_ref=src, dst_ref=out.at[4 * to[0] + 2 * to[1] + to[2]], send_sem=send.at[k], recv_sem=recv.at[k],
                device_id=to, device_id_type=MESH).wait_recv()
        mine.wait()

    return pl.pallas_call(
        body, name="gather_small_grads",
        in_specs=[ANY], out_specs=ANY, out_shape=jax.ShapeDtypeStruct((N_DEV,) + pack.shape, f32),
        scratch_shapes=[pltpu.SemaphoreType.DMA((7,)), pltpu.SemaphoreType.DMA((7,)), pltpu.SemaphoreType.DMA],
        compiler_params=pltpu.CompilerParams(has_side_effects=True),
    )(pack)


def _sum_small(parts):
    n, rows, cols = parts.shape

    def body(p_ref, o_ref):
        acc = p_ref[0]
        for d in range(1, n):
            acc = acc + p_ref[d]
        o_ref[...] = acc

    return pl.pallas_call(
        body, name="sum_small_grads", grid=(rows // 8,),
        in_specs=[pl.BlockSpec((n, 8, cols), lambda i: (0, i, 0))], out_specs=pl.BlockSpec((8, cols), lambda i: (i, 0)),
        out_shape=jax.ShapeDtypeStruct((rows, cols), f32),
        compiler_params=_params(("parallel",)),
    )(parts)


def _adamw(w, g, m, v, name):
    rows, cols = w.shape
    tr = rows
    for cand in (512, 256, 128, 64, 32, 16, 8):
        if rows % cand == 0 and rows > cand:
            tr = cand
            break

    def body(w_ref, g_ref, m_ref, v_ref, d_ref, nm_ref, nv_ref):
        gg = g_ref[...]
        nm = ADAM_B1 * m_ref[...] + (1.0 - ADAM_B1) * gg
        nv = ADAM_B2 * v_ref[...] + (1.0 - ADAM_B2) * (gg * gg)
        m_hat = nm / (1.0 - ADAM_B1 ** ADAM_STEP)
        v_hat = nv / (1.0 - ADAM_B2 ** ADAM_STEP)
        d_ref[...] = -ADAM_LR * (m_hat / (jnp.sqrt(v_hat) + ADAM_EPS) + ADAM_WD * w_ref[...])
        nm_ref[...] = nm
        nv_ref[...] = nv

    blk = pl.BlockSpec((tr, cols), lambda i: (i, 0))
    return pl.pallas_call(
        body, name=name, grid=(rows // tr,),
        in_specs=[blk] * 4, out_specs=[blk] * 3, out_shape=[jax.ShapeDtypeStruct((rows, cols), f32)] * 3,
        compiler_params=_params(("parallel",)),
    )(w, g, m, v)


SMALL = ("mix_norm", "q_norm", "k_norm", "sinks", "sgu_norm", "w_s", "b_s", "ffn_norm", "conv_b", "conv_w")


def _pack_small(arrs):
    flat = jnp.concatenate([a.reshape(-1) for a in arrs])
    pad = (-flat.shape[0]) % (8 * 1024)
    return jnp.pad(flat, (0, pad)).reshape(-1, 1024)


def _unpack_small(pack, shapes):
    flat = pack.reshape(-1)
    out, off = [], 0
    for s in shapes:
        n = int(np.prod(s))
        out.append(flat[off:off + n].reshape(s))
        off += n
    return out


def kernel(x, mix_norm, w_in, q_norm, k_norm, sinks, sgu_norm, w_s, b_s, w_oa, w_ob, w_out, ffn_norm, w_up, conv_w, conv_b, w_down, loss_target, m_mix_norm, m_w_in, m_q_norm, m_k_norm, m_sinks, m_sgu_norm, m_w_s, m_b_s, m_w_oa, m_w_ob, m_w_out, m_ffn_norm, m_w_up, m_conv_w, m_conv_b, m_w_down, v_mix_norm, v_w_in, v_q_norm, v_k_norm, v_sinks, v_sgu_norm, v_w_s, v_b_s, v_w_oa, v_w_ob, v_w_out, v_ffn_norm, v_w_up, v_conv_w, v_conv_b, v_w_down):
    weights = dict(mix_norm=mix_norm, w_in=w_in, q_norm=q_norm, k_norm=k_norm, sinks=sinks, sgu_norm=sgu_norm,
                   w_s=w_s, b_s=b_s, w_oa=w_oa, w_ob=w_ob, w_out=w_out, ffn_norm=ffn_norm, w_up=w_up,
                   conv_w=conv_w, conv_b=conv_b, w_down=w_down)
    mom_m = dict(mix_norm=m_mix_norm, w_in=m_w_in, q_norm=m_q_norm, k_norm=m_k_norm, sinks=m_sinks,
                 sgu_norm=m_sgu_norm, w_s=m_w_s, b_s=m_b_s, w_oa=m_w_oa, w_ob=m_w_ob, w_out=m_w_out,
                 ffn_norm=m_ffn_norm, w_up=m_w_up, conv_w=m_conv_w, conv_b=m_conv_b, w_down=m_w_down)
    mom_v = dict(mix_norm=v_mix_norm, w_in=v_w_in, q_norm=v_q_norm, k_norm=v_k_norm, sinks=v_sinks,
                 sgu_norm=v_sgu_norm, w_s=v_w_s, b_s=v_b_s, w_oa=v_w_oa, w_ob=v_w_ob, w_out=v_w_out,
                 ffn_norm=v_ffn_norm, w_up=v_w_up, conv_w=v_conv_w, conv_b=v_conv_b, w_down=v_w_down)
    n_seq, seq, _ = x.shape
    T = n_seq * seq
    core = lax.axis_index("c")
    chip = 2 * lax.axis_index("x") + lax.axis_index("y")
    tm = min(512, seq)
    tm_ff = min(256, seq)

    pieces = []
    for name, rows, cols, transposed in REGIONS:
        w = weights[name]
        shard = jnp.swapaxes(w, 1, 2) if transposed else w
        hr = rows // N_DEV
        pieces.append(lax.dynamic_slice_in_dim(shard, core * hr, hr, axis=1).astype(bf16))
    (w_in_t, w_oa_t, w_ob_t, w_out_f, w_up_t, w_down_f), conv_w_all = _all_gather_weights(pieces, conv_w)
    conv_w_full = jnp.concatenate([conv_w_all[j] for j in range(N_CHIP)], axis=-1)

    xs = x.reshape(T, D_MODEL)
    saved = []
    cur = xs
    for l in range(2):
        b_col = b_s[l].reshape(SGU_GROUPS, BLOCK, 1)
        qkv, su, sv, ga, gb, h = _in_proj(cur, mix_norm[l][None], w_in_t, l, tm)
        y_att = _attn_fwd(qkv, q_norm[l][None], k_norm[l][None], sinks[l], n_seq, seq)
        y_sgu = _sgu_fwd(su, sv, sgu_norm[l][None], w_s[l], b_col, tm)
        x1, merged, a_o, b_o = _merge_fwd(cur, y_att, y_sgu, ga, gb, w_oa_t, w_ob_t, w_out_f, l, tm)
        h2, z, act = _ffn_up(x1, ffn_norm[l][None], w_up_t, conv_w_full[l], conv_b[l][None], l, seq, tm_ff)
        x2 = _ffn_down(x1, act, w_down_f, l, tm)
        saved.append(dict(x=cur, qkv=qkv, su=su, sv=sv, ga=ga, gb=gb, h=h, y_att=y_att, y_sgu=y_sgu, x1=x1,
                          merged=merged, a=a_o, b=b_o, h2=h2, z=z, act=act, b_col=b_col))
        cur = x2

    dy, dyb, loss_part = _loss_head(cur, loss_target.reshape(T, D_MODEL), tm)
    loss = lax.psum(loss_part[0, 0], ("x", "y", "c"))

    big = {name: None for name, *_ in REGIONS}
    small = {name: [None, None] for name in SMALL}
    for l in (1, 0):
        s = saved[l]
        dz, dconv = _ffn_bwd(dyb, s["z"], conv_w_full[l], conv_b[l][None], w_down_f, l, seq, tm_ff)
        big["w_down"] = _weight_grad(s["act"], dyb, l, big["w_down"], 1408, tm, f"dw_down_{l}")
        big["w_up"] = _weight_grad(dz, s["h2"], l, big["w_up"], 1408, tm, f"dw_up_{l}")
        dx1, dx1b, d_ffn = _norm_bwd(dz, w_up_t, l, s["x1"], ffn_norm[l][None], dy, tm, f"ffn_norm_bwd_{l}")
        small["conv_w"][l] = dconv[0:3]
        small["conv_b"][l] = dconv[3]
        small["ffn_norm"][l] = d_ffn[0]
        da, db, dga, dgb, dya, dys = _merge_bwd(dx1b, s["ga"], s["gb"], s["a"], s["b"], w_oa_t, w_ob_t, w_out_f, l, tm)
        big["w_out"] = _weight_grad(s["merged"], dx1b, l, big["w_out"], 1024, tm, f"dw_out_{l}")
        big["w_oa"] = _weight_grad(da, s["y_att"], l, big["w_oa"], 1024, tm, f"dw_oa_{l}")
        big["w_ob"] = _weight_grad(db, s["y_sgu"], l, big["w_ob"], 1024, tm, f"dw_ob_{l}")
        dsu, dsv, d_ws, d_bs, d_sgu = _sgu_bwd(dys, s["su"], s["sv"], sgu_norm[l][None], w_s[l], s["b_col"], tm)
        causal = np.tril(np.ones((BLOCK, BLOCK), bool))
        small["w_s"][l] = jnp.where(causal[None], d_ws, 0.0)
        small["b_s"][l] = d_bs[:, :, 0]
        small["sgu_norm"][l] = d_sgu[0]
        dqkv, d_qg, d_kg, d_sink = _attn_bwd(dya, s["qkv"], q_norm[l][None], k_norm[l][None], sinks[l], n_seq, seq)
        small["q_norm"][l] = d_qg[0]
        small["k_norm"][l] = d_kg[0]
        small["sinks"][l] = d_sink[:, 0]
        dproj = jnp.concatenate([dqkv, dsu, dsv, dga, dgb], axis=1)
        big["w_in"] = _weight_grad(dproj, s["h"], l, big["w_in"], 1280, tm, f"dw_in_{l}")
        dy, dyb, d_mix = _norm_bwd(dproj, w_in_t, l, s["x"], mix_norm[l][None], dx1, tm, f"mix_norm_bwd_{l}")
        small["mix_norm"][l] = d_mix[0]
    grad_x = dy.reshape(n_seq, seq, D_MODEL)

    grads = [big[name] for name, *_ in REGIONS]
    from_sibling = _pair_exchange(grads)
    core_arr = core.astype(jnp.int32).reshape(1)
    pair = [_pair_sum(g, o, core_arr, f"pair_sum_{name}") for g, o, (name, *_) in zip(grads, from_sibling, REGIONS)]
    parts = _chip_exchange(pair)
    halves = [_chip_sum(p, f"chip_sum_{name}") for p, (name, *_) in zip(parts, REGIONS)]
    shared = _share_halves(halves)
    grad_big = {}
    for sh, (name, rows, cols, transposed) in zip(shared, REGIONS):
        g = sh.reshape(2, rows // N_CHIP, cols)
        grad_big[name] = jnp.swapaxes(g, 1, 2) if transposed else g

    small_shapes = [weights[n].shape if n != "conv_w" else (2, 3, 2 * D_FF) for n in SMALL]
    pack = _pack_small([jnp.stack(small[n]) for n in SMALL])
    total = _sum_small(_gather_small(pack))
    grad_small = dict(zip(SMALL, _unpack_small(total, small_shapes)))
    cw_cols = conv_w.shape[-1]
    grad_small["conv_w"] = lax.dynamic_slice_in_dim(grad_small["conv_w"], chip * cw_cols, cw_cols, axis=2)

    grad, delta, new_m, new_v = {}, {}, {}, {}
    for name, *_ in REGIONS:
        shp = weights[name].shape
        two_d = lambda a: a.reshape(shp[0] * shp[1], shp[2])
        grad[name] = grad_big[name]
        d, nm, nv = _adamw(two_d(weights[name]), two_d(grad[name]), two_d(mom_m[name]), two_d(mom_v[name]), f"adamw_{name}")
        delta[name], new_m[name], new_v[name] = d.reshape(shp), nm.reshape(shp), nv.reshape(shp)
    shapes = [weights[n].shape for n in SMALL]
    d, nm, nv = _adamw(_pack_small([weights[n] for n in SMALL]), _pack_small([grad_small[n] for n in SMALL]),
                       _pack_small([mom_m[n] for n in SMALL]), _pack_small([mom_v[n] for n in SMALL]), "adamw_small")
    for n, dd, mm, vv in zip(SMALL, _unpack_small(d, shapes), _unpack_small(nm, shapes), _unpack_small(nv, shapes)):
        grad[n], delta[n], new_m[n], new_v[n] = grad_small[n], dd, mm, vv

    order = ["mix_norm", "w_in", "q_norm", "k_norm", "sinks", "sgu_norm", "w_s", "b_s", "w_oa", "w_ob", "w_out",
             "ffn_norm", "w_up", "conv_w", "conv_b", "w_down"]
    return (loss, grad_x, *[grad[n] for n in order], *[delta[n] for n in order],
            *[new_m[n] for n in order], *[new_v[n] for n in order])
```

```python
import functools

import numpy as np
import jax
import jax.numpy as jnp
from jax import lax
from jax.experimental import pallas as pl
from jax.experimental.pallas import tpu as pltpu

bf16 = jnp.bfloat16
f32 = jnp.float32

D_MODEL = 1024
ATT_WIDTH = 512
KV_WIDTH = 128
SGU_WIDTH = 512
HEAD_DIM = 64
N_KV_HEADS = 2
Q_GROUP = 4
BLOCK = 128
SGU_GROUPS = 8
IN_WIDTH = 3840
D_FF = 2816
NORM_EPS = 1e-6
NEG_INF = -1e30
N_DEV = 8
N_CHIP = 4

ADAM_LR = 0.001
ADAM_B1 = 0.9
ADAM_B2 = 0.999
ADAM_EPS = 1e-08
ADAM_WD = 0.01
ADAM_STEP = 10

V7X_VMEM_LIMIT = 56 * 1024 * 1024
FF_CHUNK = 256

REGIONS = (
    ("w_in", IN_WIDTH, D_MODEL, True),
    ("w_oa", D_MODEL, ATT_WIDTH, True),
    ("w_ob", D_MODEL, SGU_WIDTH, True),
    ("w_out", D_MODEL, D_MODEL, False),
    ("w_up", 2 * D_FF, D_MODEL, True),
    ("w_down", D_FF, D_MODEL, False),
)
MESH = pl.DeviceIdType.MESH
ANY = pl.BlockSpec(memory_space=pl.ANY)


def _params(sem=None, **kw):
    return pltpu.CompilerParams(dimension_semantics=sem, vmem_limit_bytes=V7X_VMEM_LIMIT, **kw)


def _wspec(rows, cols, layer):
    return pl.BlockSpec((None, rows, cols), lambda *_: (layer, 0, 0), pipeline_mode=pl.Buffered(1))


def _full(shape):
    nd = len(shape)
    return pl.BlockSpec(shape, lambda *_: (0,) * nd)


def _dot_nn(a, b):
    return jnp.dot(a, b, preferred_element_type=f32)


def _dot_nt(a, b):
    return lax.dot_general(a, b, (((1,), (1,)), ((), ())), preferred_element_type=f32)


def _dot_tn(a, b):
    return lax.dot_general(a, b, (((0,), (0,)), ((), ())), preferred_element_type=f32)


_GELU_C = float(np.sqrt(2.0 / np.pi))


def _gelu(x):
    return 0.5 * x * (1.0 + jnp.tanh(_GELU_C * (x + 0.044715 * x * x * x)))


def _gelu_grad(x):
    t = jnp.tanh(_GELU_C * (x + 0.044715 * x * x * x))
    du = _GELU_C * (1.0 + 3.0 * 0.044715 * x * x)
    return 0.5 * (1.0 + t) + 0.5 * x * (1.0 - t * t) * du


def _rms(x):
    return lax.rsqrt(jnp.mean(x * x, axis=-1, keepdims=True) + NORM_EPS)


def _mesh_pos():
    return lax.axis_index("x"), lax.axis_index("y"), lax.axis_index("c")


def _in_proj(x, gain, w_in_t, layer, tm):
    T = x.shape[0]

    def body(x_ref, g_ref, w_ref, qkv_ref, su_ref, sv_ref, ga_ref, gb_ref, h_ref):
        xf = x_ref[...]
        h = (xf * _rms(xf) * g_ref[...]).astype(bf16)
        h_ref[...] = h
        qkv_ref[...] = _dot_nt(h, w_ref[0:768, :])
        su_ref[...] = _dot_nt(h, w_ref[768:1280, :]).astype(bf16)
        sv_ref[...] = _dot_nt(h, w_ref[1280:1792, :]).astype(bf16)
        ga_ref[...] = _dot_nt(h, w_ref[1792:2816, :]).astype(bf16)
        gb_ref[...] = _dot_nt(h, w_ref[2816:3840, :]).astype(bf16)

    row = lambda w: pl.BlockSpec((tm, w), lambda i: (i, 0))
    return pl.pallas_call(
        body, name=f"in_proj_{layer}", grid=(T // tm,),
        in_specs=[row(D_MODEL), _full((1, D_MODEL)), _wspec(IN_WIDTH, D_MODEL, layer)],
        out_specs=[row(768), row(512), row(512), row(1024), row(1024), row(D_MODEL)],
        out_shape=[jax.ShapeDtypeStruct((T, 768), f32), jax.ShapeDtypeStruct((T, 512), bf16),
                   jax.ShapeDtypeStruct((T, 512), bf16), jax.ShapeDtypeStruct((T, 1024), bf16),
                   jax.ShapeDtypeStruct((T, 1024), bf16), jax.ShapeDtypeStruct((T, D_MODEL), bf16)],
        compiler_params=_params(("parallel",)),
    )(x, gain, w_in_t)


def _attn_head_group(cur, prev, qg, kg, sink_ref, n, hk):
    lo = hk * HEAD_DIM
    k_raw = jnp.concatenate([prev[:, lo:lo + HEAD_DIM], cur[:, 512 + lo:512 + lo + HEAD_DIM]], axis=0)
    v_band = jnp.concatenate([prev[:, 128 + lo:128 + lo + HEAD_DIM], cur[:, 640 + lo:640 + lo + HEAD_DIM]], axis=0)
    rk = _rms(k_raw)
    k_hat = k_raw * rk
    kn = (k_hat * kg).astype(bf16)
    q_raw = jnp.concatenate(
        [cur[:, (hk * Q_GROUP + g) * HEAD_DIM:(hk * Q_GROUP + g + 1) * HEAD_DIM] for g in range(Q_GROUP)], axis=0)
    rq = _rms(q_raw)
    q_hat = q_raw * rq
    qn = (q_hat * qg * (HEAD_DIM ** -0.5)).astype(bf16)
    s = _dot_nt(qn, kn)
    rows = lax.broadcasted_iota(jnp.int32, (Q_GROUP * BLOCK, 1), 0)
    g_of_row = rows // BLOCK
    qi = rows - g_of_row * BLOCK
    kj = lax.broadcasted_iota(jnp.int32, (1, 2 * BLOCK), 1)
    dist = qi + BLOCK - kj
    valid = (dist >= 0) & (dist < BLOCK) & ((kj >= BLOCK) | (n > 0))
    slope = jnp.zeros((Q_GROUP * BLOCK, 1), f32)
    sink = jnp.zeros((Q_GROUP * BLOCK, 1), f32)
    for g in range(Q_GROUP):
        head = hk * Q_GROUP + g
        slope = jnp.where(g_of_row == g, float(np.exp2(-8.0 * (head + 1.0) / 8.0)), slope)
        sink = jnp.where(g_of_row == g, sink_ref[head], sink)
    s = jnp.where(valid, s - slope * dist.astype(f32), NEG_INF)
    m = jnp.maximum(jnp.max(s, axis=-1, keepdims=True), sink)
    e = jnp.exp(s - m)
    e_sink = jnp.exp(sink - m)
    inv = 1.0 / (jnp.sum(e, axis=-1, keepdims=True) + e_sink)
    return dict(k_raw=k_raw, rk=rk, k_hat=k_hat, kn=kn, v=v_band.astype(bf16), q_hat=q_hat, rq=rq, qn=qn,
                p=e * inv, p_sink=e_sink * inv)


def _attn_fwd(qkv, qg, kg, sinks, n_seq, seq):
    T = n_seq * seq
    nb = seq // BLOCK

    def body(cur_ref, prev_ref, qg_ref, kg_ref, sink_ref, y_ref):
        n = pl.program_id(1)
        cur = cur_ref[...]
        prev = prev_ref[...]
        pieces = [None] * (N_KV_HEADS * Q_GROUP)
        for hk in range(N_KV_HEADS):
            a = _attn_head_group(cur, prev, qg_ref[...], kg_ref[...], sink_ref, n, hk)
            o = _dot_nn(a["p"].astype(bf16), a["v"])
            for g in range(Q_GROUP):
                pieces[hk * Q_GROUP + g] = o[g * BLOCK:(g + 1) * BLOCK]
        y_ref[...] = jnp.concatenate(pieces, axis=1).astype(bf16)

    return pl.pallas_call(
        body, name="attn_fwd", grid=(n_seq, nb),
        in_specs=[pl.BlockSpec((BLOCK, 768), lambda b, n: (b * nb + n, 0)),
                  pl.BlockSpec((BLOCK, 256), lambda b, n: (b * nb + jnp.maximum(n - 1, 0), 2)),
                  _full((1, HEAD_DIM)), _full((1, HEAD_DIM)),
                  pl.BlockSpec(memory_space=pltpu.SMEM)],
        out_specs=pl.BlockSpec((BLOCK, ATT_WIDTH), lambda b, n: (b * nb + n, 0)),
        out_shape=jax.ShapeDtypeStruct((T, ATT_WIDTH), bf16),
        compiler_params=_params(("parallel", "parallel")),
    )(qkv, qkv, qg, kg, sinks)


def _sgu_chunk(su, sv, gain, w_ref, b_ref):
    u = _gelu(su)
    vg = _gelu(sv)
    rv = _rms(vg)
    v_hat = vg * rv
    vn = (v_hat * gain).astype(bf16)
    causal = (lax.broadcasted_iota(jnp.int32, (BLOCK, BLOCK), 0) >= lax.broadcasted_iota(jnp.int32, (BLOCK, BLOCK), 1))
    w_tril = [jnp.where(causal, w_ref[g], 0.0).astype(bf16) for g in range(SGU_GROUPS)]
    gd = SGU_WIDTH // SGU_GROUPS
    mixed = jnp.concatenate(
        [_dot_nn(w_tril[g], vn[:, g * gd:(g + 1) * gd]) + b_ref[g] for g in range(SGU_GROUPS)], axis=1)
    return u, rv, v_hat, vn, w_tril, mixed


def _sgu_fwd(su, sv, gain, w_s, b_s, tm):
    T = su.shape[0]

    def body(su_ref, sv_ref, g_ref, w_ref, b_ref, y_ref):
        for ch in range(tm // BLOCK):
            rows = slice(ch * BLOCK, (ch + 1) * BLOCK)
            u, _, _, _, _, mixed = _sgu_chunk(su_ref[rows, :].astype(f32), sv_ref[rows, :].astype(f32),
                                              g_ref[...], w_ref, b_ref)
            y_ref[rows, :] = (u * mixed).astype(bf16)

    row = pl.BlockSpec((tm, SGU_WIDTH), lambda i: (i, 0))
    return pl.pallas_call(
        body, name="sgu_fwd", grid=(T // tm,),
        in_specs=[row, row, _full((1, SGU_WIDTH)), _full((SGU_GROUPS, BLOCK, BLOCK)), _full((SGU_GROUPS, BLOCK, 1))],
        out_specs=row, out_shape=jax.ShapeDtypeStruct((T, SGU_WIDTH), bf16),
        compiler_params=_params(("parallel",)),
    )(su, sv, gain, w_s, b_s)


def _merge_fwd(x, y_att, y_sgu, ga, gb, w_oa_t, w_ob_t, w_out, layer, tm):
    T = x.shape[0]

    def body(x_ref, ya_ref, ys_ref, ga_ref, gb_ref, woa_ref, wob_ref, wout_ref, x1_ref, m_ref, a_ref, b_ref):
        a = _dot_nt(ya_ref[...], woa_ref[...])
        b = _dot_nt(ys_ref[...], wob_ref[...])
        a_ref[...] = a.astype(bf16)
        b_ref[...] = b.astype(bf16)
        merged = (jax.nn.sigmoid(ga_ref[...].astype(f32)) * a + jax.nn.sigmoid(gb_ref[...].astype(f32)) * b).astype(bf16)
        m_ref[...] = merged
        x1_ref[...] = x_ref[...] + _dot_nn(merged, wout_ref[...])

    row = lambda w: pl.BlockSpec((tm, w), lambda i: (i, 0))
    return pl.pallas_call(
        body, name=f"merge_fwd_{layer}", grid=(T // tm,),
        in_specs=[row(D_MODEL), row(512), row(512), row(1024), row(1024),
                  _wspec(D_MODEL, ATT_WIDTH, layer), _wspec(D_MODEL, SGU_WIDTH, layer), _wspec(D_MODEL, D_MODEL, layer)],
        out_specs=[row(D_MODEL)] * 4,
        out_shape=[jax.ShapeDtypeStruct((T, D_MODEL), f32)] + [jax.ShapeDtypeStruct((T, D_MODEL), bf16)] * 3,
        compiler_params=_params(("parallel",)),
    )(x, y_att, y_sgu, ga, gb, w_oa_t, w_ob_t, w_out)


def _conv_taps(zz, h1, h2, rowid):
    z1 = jnp.where(rowid == 0, h1, pltpu.roll(zz, 1, 0))
    z2 = jnp.where(rowid == 0, h2, jnp.where(rowid == 1, h1, pltpu.roll(zz, 2, 0)))
    return z1, z2


def _ffn_up(x1, gain, w_up_t, conv_w, conv_b, layer, seq, tm):
    T = x1.shape[0]
    tps = seq // tm

    def body(x_ref, g_ref, w_ref, cw_ref, cb_ref, h2_ref, z_ref, act_ref, carry_ref):
        i = pl.program_id(0)

        @pl.when(i % tps == 0)
        def _():
            carry_ref[...] = jnp.zeros_like(carry_ref)

        xf = x_ref[...]
        h2 = (xf * _rms(xf) * g_ref[...]).astype(bf16)
        h2_ref[...] = h2
        rowid = lax.broadcasted_iota(jnp.int32, (tm, 1), 0)
        for cc in range(D_FF // FF_CHUNK):
            zc = []
            for part in range(2):
                lo = part * D_FF + cc * FF_CHUNK
                cols = slice(lo, lo + FF_CHUNK)
                zb = _dot_nt(h2, w_ref[cols, :]).astype(bf16)
                z_ref[:, cols] = zb
                zz = zb.astype(f32)
                halo = carry_ref[:, cols]
                z1, z2 = _conv_taps(zz, halo[7:8], halo[6:7], rowid)
                carry_ref[:, cols] = zz[tm - 8:tm]
                zc.append(cb_ref[:, cols] + cw_ref[0:1, cols] * z2 + cw_ref[1:2, cols] * z1 + cw_ref[2:3, cols] * zz)
            act_ref[:, cc * FF_CHUNK:(cc + 1) * FF_CHUNK] = (zc[0] * jax.nn.sigmoid(zc[0]) * zc[1]).astype(bf16)

    row = lambda w: pl.BlockSpec((tm, w), lambda i: (i, 0))
    return pl.pallas_call(
        body, name=f"ffn_up_{layer}", grid=(T // tm,),
        in_specs=[row(D_MODEL), _full((1, D_MODEL)), _wspec(2 * D_FF, D_MODEL, layer),
                  _full((3, 2 * D_FF)), _full((1, 2 * D_FF))],
        out_specs=[row(D_MODEL), row(2 * D_FF), row(D_FF)],
        out_shape=[jax.ShapeDtypeStruct((T, D_MODEL), bf16), jax.ShapeDtypeStruct((T, 2 * D_FF), bf16),
                   jax.ShapeDtypeStruct((T, D_FF), bf16)],
        scratch_shapes=[pltpu.VMEM((8, 2 * D_FF), f32)],
        compiler_params=_params(("arbitrary",)),
    )(x1, gain, w_up_t, conv_w, conv_b)


def _ffn_down(x1, act, w_down, layer, tm):
    T = x1.shape[0]

    def body(x_ref, a_ref, w_ref, o_ref):
        o_ref[...] = x_ref[...] + _dot_nn(a_ref[...], w_ref[...])

    row = lambda w: pl.BlockSpec((tm, w), lambda i: (i, 0))
    return pl.pallas_call(
        body, name=f"ffn_down_{layer}", grid=(T // tm,),
        in_specs=[row(D_MODEL), row(D_FF), _wspec(D_FF, D_MODEL, layer)],
        out_specs=row(D_MODEL), out_shape=jax.ShapeDtypeStruct((T, D_MODEL), f32),
        compiler_params=_params(("parallel",)),
    )(x1, act, w_down)


def _loss_head(y, target, tm):
    T = y.shape[0]

    def body(y_ref, t_ref, dy_ref, dyb_ref, loss_ref):
        @pl.when(pl.program_id(0) == 0)
        def _():
            loss_ref[...] = jnp.zeros_like(loss_ref)

        diff = y_ref[...] - t_ref[...]
        loss_ref[...] += 0.5 * jnp.sum(jnp.mean(diff * diff, axis=-1, keepdims=True), axis=0, keepdims=True)
        dy = diff * (1.0 / D_MODEL)
        dy_ref[...] = dy
        dyb_ref[...] = dy.astype(bf16)

    row = pl.BlockSpec((tm, D_MODEL), lambda i: (i, 0))
    return pl.pallas_call(
        body, name="loss_head", grid=(T // tm,),
        in_specs=[row, row], out_specs=[row, row, _full((8, 128))],
        out_shape=[jax.ShapeDtypeStruct((T, D_MODEL), f32), jax.ShapeDtypeStruct((T, D_MODEL), bf16),
                   jax.ShapeDtypeStruct((8, 128), f32)],
        compiler_params=_params(("arbitrary",)),
    )(y, target)


def _ffn_bwd(dx2b, z, conv_w, conv_b, w_down, layer, seq, tm):
    T = z.shape[0]
    nt = T // tm
    tps = seq // tm

    def body(dx_ref, z_ref, zh_ref, cw_ref, cb_ref, wd_ref, dz_ref, dconv_ref, carry_ref):
        i = pl.program_id(0)
        pos = (nt - 1 - i) % tps

        @pl.when(i == 0)
        def _():
            dconv_ref[...] = jnp.zeros_like(dconv_ref)

        @pl.when(pos == tps - 1)
        def _():
            carry_ref[...] = jnp.zeros_like(carry_ref)

        dxb = dx_ref[...]
        rowid = lax.broadcasted_iota(jnp.int32, (tm, 1), 0)
        halo_on = (pos > 0).astype(f32)
        for cc in range(D_FF // FF_CHUNK):
            zz, z1, z2, zc, colss = [], [], [], [], []
            for part in range(2):
                lo = part * D_FF + cc * FF_CHUNK
                cols = slice(lo, lo + FF_CHUNK)
                colss.append(cols)
                zp = z_ref[:, cols].astype(f32)
                halo = zh_ref[:, cols].astype(f32) * halo_on
                a1, a2 = _conv_taps(zp, halo[7:8], halo[6:7], rowid)
                zz.append(zp), z1.append(a1), z2.append(a2)
                zc.append(cb_ref[:, cols] + cw_ref[0:1, cols] * a2 + cw_ref[1:2, cols] * a1 + cw_ref[2:3, cols] * zp)
            d_act = _dot_nt(dxb, wd_ref[cc * FF_CHUNK:(cc + 1) * FF_CHUNK, :])
            sg = jax.nn.sigmoid(zc[0])
            silu = zc[0] * sg
            dzc = [d_act * zc[1] * sg * (1.0 + zc[0] * (1.0 - sg)), d_act * silu]
            for part in range(2):
                cols = colss[part]
                g = dzc[part]
                dconv_ref[0:1, cols] += jnp.sum(g * z2[part], axis=0, keepdims=True)
                dconv_ref[1:2, cols] += jnp.sum(g * z1[part], axis=0, keepdims=True)
                dconv_ref[2:3, cols] += jnp.sum(g * zz[part], axis=0, keepdims=True)
                dconv_ref[3:4, cols] += jnp.sum(g, axis=0, keepdims=True)
                nxt = carry_ref[:, cols]
                d1 = jnp.where(rowid == tm - 1, nxt[0:1], pltpu.roll(g, tm - 1, 0))
                d2 = jnp.where(rowid == tm - 1, nxt[1:2], jnp.where(rowid == tm - 2, nxt[0:1], pltpu.roll(g, tm - 2, 0)))
                carry_ref[:, cols] = g[0:8]
                dz_ref[:, cols] = (cw_ref[2:3, cols] * g + cw_ref[1:2, cols] * d1 + cw_ref[0:1, cols] * d2).astype(bf16)

    rev = lambda w: pl.BlockSpec((tm, w), lambda i: (nt - 1 - i, 0))
    return pl.pallas_call(
        body, name=f"ffn_bwd_{layer}", grid=(nt,),
        in_specs=[rev(D_MODEL), rev(2 * D_FF),
                  pl.BlockSpec((8, 2 * D_FF), lambda i: (jnp.maximum((nt - 1 - i) * (tm // 8) - 1, 0), 0)),
                  _full((3, 2 * D_FF)), _full((1, 2 * D_FF)), _wspec(D_FF, D_MODEL, layer)],
        out_specs=[rev(2 * D_FF), _full((8, 2 * D_FF))],
        out_shape=[jax.ShapeDtypeStruct((T, 2 * D_FF), bf16), jax.ShapeDtypeStruct((8, 2 * D_FF), f32)],
        scratch_shapes=[pltpu.VMEM((8, 2 * D_FF), f32)],
        compiler_params=_params(("arbitrary",)),
    )(dx2b, z, z, conv_w, conv_b, w_down)


def _norm_bwd(dy, w, layer, x, gain, dres, tm, name):
    T, K = dy.shape

    def body(dy_ref, w_ref, x_ref, g_ref, dres_ref, dx_ref, dxb_ref, dg_ref):
        @pl.when(pl.program_id(0) == 0)
        def _():
            dg_ref[...] = jnp.zeros_like(dg_ref)

        dh = _dot_nn(dy_ref[...], w_ref[...])
        xf = x_ref[...]
        r = _rms(xf)
        x_hat = xf * r
        dg_ref[...] += jnp.sum(dh * x_hat, axis=0, keepdims=True)
        dxh = dh * g_ref[...]
        dx = dres_ref[...] + r * (dxh - x_hat * jnp.mean(dxh * x_hat, axis=-1, keepdims=True))
        dx_ref[...] = dx
        dxb_ref[...] = dx.astype(bf16)

    row = lambda w_: pl.BlockSpec((tm, w_), lambda i: (i, 0))
    return pl.pallas_call(
        body, name=name, grid=(T // tm,),
        in_specs=[row(K), _wspec(K, D_MODEL, layer), row(D_MODEL), _full((1, D_MODEL)), row(D_MODEL)],
        out_specs=[row(D_MODEL), row(D_MODEL), _full((1, D_MODEL))],
        out_shape=[jax.ShapeDtypeStruct((T, D_MODEL), f32), jax.ShapeDtypeStruct((T, D_MODEL), bf16),
                   jax.ShapeDtypeStruct((1, D_MODEL), f32)],
        compiler_params=_params(("arbitrary",)),
    )(dy, w, x, gain, dres)


def _merge_bwd(dx1b, ga, gb, a, b, w_oa_t, w_ob_t, w_out, layer, tm):
    T = dx1b.shape[0]

    def body(dx_ref, ga_ref, gb_ref, a_ref, b_ref, woa_ref, wob_ref, wout_ref,
             da_ref, db_ref, dga_ref, dgb_ref, dya_ref, dys_ref):
        dm = _dot_nt(dx_ref[...], wout_ref[...])
        sa = jax.nn.sigmoid(ga_ref[...].astype(f32))
        sb = jax.nn.sigmoid(gb_ref[...].astype(f32))
        da = (dm * sa).astype(bf16)
        db = (dm * sb).astype(bf16)
        da_ref[...] = da
        db_ref[...] = db
        dga_ref[...] = (dm * a_ref[...].astype(f32) * sa * (1.0 - sa)).astype(bf16)
        dgb_ref[...] = (dm * b_ref[...].astype(f32) * sb * (1.0 - sb)).astype(bf16)
        dya_ref[...] = _dot_nn(da, woa_ref[...]).astype(bf16)
        dys_ref[...] = _dot_nn(db, wob_ref[...]).astype(bf16)

    row = lambda w: pl.BlockSpec((tm, w), lambda i: (i, 0))
    return pl.pallas_call(
        body, name=f"merge_bwd_{layer}", grid=(T // tm,),
        in_specs=[row(D_MODEL)] * 5 + [_wspec(D_MODEL, ATT_WIDTH, layer), _wspec(D_MODEL, SGU_WIDTH, layer),
                                       _wspec(D_MODEL, D_MODEL, layer)],
        out_specs=[row(D_MODEL)] * 4 + [row(512)] * 2,
        out_shape=[jax.ShapeDtypeStruct((T, D_MODEL), bf16)] * 4 + [jax.ShapeDtypeStruct((T, 512), bf16)] * 2,
        compiler_params=_params(("parallel",)),
    )(dx1b, ga, gb, a, b, w_oa_t, w_ob_t, w_out)


def _sgu_bwd(dy, su, sv, gain, w_s, b_s, tm):
    T = su.shape[0]
    gd = SGU_WIDTH // SGU_GROUPS

    def body(dy_ref, su_ref, sv_ref, g_ref, w_ref, b_ref, dsu_ref, dsv_ref, dw_ref, db_ref, dg_ref):
        @pl.when(pl.program_id(0) == 0)
        def _():
            dw_ref[...] = jnp.zeros_like(dw_ref)
            db_ref[...] = jnp.zeros_like(db_ref)
            dg_ref[...] = jnp.zeros_like(dg_ref)

        gain_v = g_ref[...]
        for ch in range(tm // BLOCK):
            rows = slice(ch * BLOCK, (ch + 1) * BLOCK)
            su_c = su_ref[rows, :].astype(f32)
            sv_c = sv_ref[rows, :].astype(f32)
            u, rv, v_hat, vn, w_tril, mixed = _sgu_chunk(su_c, sv_c, gain_v, w_ref, b_ref)
            dyc = dy_ref[rows, :].astype(f32)
            dsu_ref[rows, :] = (dyc * mixed * _gelu_grad(su_c)).astype(bf16)
            dmix = dyc * u
            dmix_b = dmix.astype(bf16)
            dvn = []
            for g in range(SGU_GROUPS):
                gs = slice(g * gd, (g + 1) * gd)
                db_ref[g] += jnp.sum(dmix[:, gs], axis=1, keepdims=True)
                dw_ref[g] += _dot_nt(dmix_b[:, gs], vn[:, gs])
                dvn.append(_dot_tn(w_tril[g], dmix_b[:, gs]))
            dvn = jnp.concatenate(dvn, axis=1)
            dg_ref[...] += jnp.sum(dvn * v_hat, axis=0, keepdims=True)
            dxh = dvn * gain_v
            dvg = rv * (dxh - v_hat * jnp.mean(dxh * v_hat, axis=-1, keepdims=True))
            dsv_ref[rows, :] = (dvg * _gelu_grad(sv_c)).astype(bf16)

    row = pl.BlockSpec((tm, SGU_WIDTH), lambda i: (i, 0))
    return pl.pallas_call(
        body, name="sgu_bwd", grid=(T // tm,),
        in_specs=[row, row, row, _full((1, SGU_WIDTH)), _full((SGU_GROUPS, BLOCK, BLOCK)),
                  _full((SGU_GROUPS, BLOCK, 1))],
        out_specs=[row, row, _full((SGU_GROUPS, BLOCK, BLOCK)), _full((SGU_GROUPS, BLOCK, 1)), _full((1, SGU_WIDTH))],
        out_shape=[jax.ShapeDtypeStruct((T, SGU_WIDTH), bf16)] * 2 + [
            jax.ShapeDtypeStruct((SGU_GROUPS, BLOCK, BLOCK), f32), jax.ShapeDtypeStruct((SGU_GROUPS, BLOCK, 1), f32),
            jax.ShapeDtypeStruct((1, SGU_WIDTH), f32)],
        compiler_params=_params(("arbitrary",)),
    )(dy, su, sv, gain, w_s, b_s)


def _attn_bwd(dy, qkv, qg, kg, sinks, n_seq, seq):
    T = n_seq * seq
    nb = seq // BLOCK
    scale = HEAD_DIM ** -0.5

    def body(dy_ref, cur_ref, prev_ref, qg_ref, kg_ref, sink_ref, dqkv_ref, dqg_ref, dkg_ref, dsink_ref,
             carry_k, carry_v):
        b = pl.program_id(0)
        j = pl.program_id(1)
        n = nb - 1 - j

        @pl.when((b == 0) & (j == 0))
        def _():
            dqg_ref[...] = jnp.zeros_like(dqg_ref)
            dkg_ref[...] = jnp.zeros_like(dkg_ref)
            dsink_ref[...] = jnp.zeros_like(dsink_ref)

        @pl.when(j == 0)
        def _():
            carry_k[...] = jnp.zeros_like(carry_k)
            carry_v[...] = jnp.zeros_like(carry_v)

        cur = cur_ref[...]
        prev = prev_ref[...]
        dyf = dy_ref[...].astype(f32)
        qg_v = qg_ref[...]
        kg_v = kg_ref[...]
        dq_pieces = [None] * (N_KV_HEADS * Q_GROUP)
        dk_pieces, dv_pieces = [], []
        for hk in range(N_KV_HEADS):
            a = _attn_head_group(cur, prev, qg_v, kg_v, sink_ref, n, hk)
            do = jnp.concatenate(
                [dyf[:, (hk * Q_GROUP + g) * HEAD_DIM:(hk * Q_GROUP + g + 1) * HEAD_DIM] for g in range(Q_GROUP)],
                axis=0).astype(bf16)
            p = a["p"]
            dp = _dot_nt(do, a["v"])
            dv_band = _dot_tn(p.astype(bf16), do)
            dsum = jnp.sum(p * dp, axis=-1, keepdims=True)
            ds = (p * (dp - dsum)).astype(bf16)
            dsink_col = -a["p_sink"] * dsum
            for g in range(Q_GROUP):
                head = hk * Q_GROUP + g
                dsink_ref[head:head + 1, :] += jnp.sum(dsink_col[g * BLOCK:(g + 1) * BLOCK], axis=0, keepdims=True)
            dqn = _dot_nn(ds, a["kn"])
            dkn_band = _dot_tn(ds, a["qn"])
            dq_hat_g = dqn * scale
            dqg_ref[...] += jnp.sum(dq_hat_g * a["q_hat"], axis=0, keepdims=True)
            dxh = dq_hat_g * qg_v
            dq = a["rq"] * (dxh - a["q_hat"] * jnp.mean(dxh * a["q_hat"], axis=-1, keepdims=True))
            for g in range(Q_GROUP):
                dq_pieces[hk * Q_GROUP + g] = dq[g * BLOCK:(g + 1) * BLOCK]
            dkn = dkn_band[BLOCK:] + carry_k[hk]
            dv_pieces.append(dv_band[BLOCK:] + carry_v[hk])
            carry_k[hk] = dkn_band[:BLOCK]
            carry_v[hk] = dv_band[:BLOCK]
            k_hat = a["k_hat"][BLOCK:]
            dkg_ref[...] += jnp.sum(dkn * k_hat, axis=0, keepdims=True)
            dxk = dkn * kg_v
            dk_pieces.append(a["rk"][BLOCK:] * (dxk - k_hat * jnp.mean(dxk * k_hat, axis=-1, keepdims=True)))
        dqkv_ref[...] = jnp.concatenate(dq_pieces + dk_pieces + dv_pieces, axis=1).astype(bf16)

    blk = lambda w: pl.BlockSpec((BLOCK, w), lambda b, j: (b * nb + nb - 1 - j, 0))
    return pl.pallas_call(
        body, name="attn_bwd", grid=(n_seq, nb),
        in_specs=[blk(ATT_WIDTH), blk(768),
                  pl.BlockSpec((BLOCK, 256), lambda b, j: (b * nb + jnp.maximum(nb - 2 - j, 0), 2)),
                  _full((1, HEAD_DIM)), _full((1, HEAD_DIM)), pl.BlockSpec(memory_space=pltpu.SMEM)],
        out_specs=[blk(768), _full((1, HEAD_DIM)), _full((1, HEAD_DIM)), _full((8, 128))],
        out_shape=[jax.ShapeDtypeStruct((T, 768), bf16), jax.ShapeDtypeStruct((1, HEAD_DIM), f32),
                   jax.ShapeDtypeStruct((1, HEAD_DIM), f32), jax.ShapeDtypeStruct((8, 128), f32)],
        scratch_shapes=[pltpu.VMEM((N_KV_HEADS, BLOCK, HEAD_DIM), f32), pltpu.VMEM((N_KV_HEADS, BLOCK, HEAD_DIM), f32)],
        compiler_params=_params(("arbitrary", "arbitrary")),
    )(dy, qkv, qkv, qg, kg, sinks)


def _weight_grad(a, b, layer, prev, tm, tk, name):
    T, M = a.shape
    N = b.shape[1]
    nk = T // tk

    def body(*refs):
        a_ref, b_ref = refs[0], refs[1]
        o_ref, acc_ref = refs[-2], refs[-1]
        k = pl.program_id(1)

        @pl.when(k == 0)
        def _():
            acc_ref[...] = jnp.zeros_like(acc_ref)

        acc_ref[...] += _dot_tn(a_ref[...], b_ref[...])

        @pl.when(k == nk - 1)
        def _():
            o_ref[...] = acc_ref[...].astype(bf16)

    in_specs = [pl.BlockSpec((tk, tm), lambda i, k: (k, i)), pl.BlockSpec((tk, N), lambda i, k: (k, 0))]
    args = [a, b]
    aliases = {}
    if prev is not None:
        in_specs.append(ANY)
        args.append(prev)
        aliases = {2: 0}
    return pl.pallas_call(
        body, name=name, grid=(M // tm, nk),
        in_specs=in_specs, out_specs=pl.BlockSpec((None, tm, N), lambda i, k: (layer, i, 0)),
        out_shape=jax.ShapeDtypeStruct((2, M, N), bf16),
        scratch_shapes=[pltpu.VMEM((tm, N), f32)],
        input_output_aliases=aliases,
        compiler_params=_params(("parallel", "arbitrary")),
    )(*args)


def _place(src, src_slot, n_slots, dst_slot, dtype, name):
    n_l, _, rows, cols = src.shape
    slots = jnp.stack([src_slot, dst_slot]).astype(jnp.int32)

    def body(slots_ref, s_ref, o_ref):
        o_ref[...] = s_ref[...].astype(dtype)

    return pl.pallas_call(
        body, name=name,
        grid_spec=pltpu.PrefetchScalarGridSpec(
            num_scalar_prefetch=1, grid=(n_l,),
            in_specs=[pl.BlockSpec((None, None, rows, cols), lambda l, sl: (l, sl[0], 0, 0))],
            out_specs=pl.BlockSpec((None, None, rows, cols), lambda l, sl: (l, sl[1], 0, 0))),
        out_shape=jax.ShapeDtypeStruct((n_l, n_slots, rows, cols), dtype),
        compiler_params=_params(("arbitrary",)),
    )(slots, src)


def _all_gather_weights(placed, conv_w):
    nr = len(placed)
    per = 7
    n_sem = 2 * nr * per

    def body(*refs):
        cw_src = refs[nr]
        out = refs[nr + 1:2 * nr + 1]
        cw_out = refs[2 * nr + 1]
        send, recv, local, cw_send, cw_recv = refs[2 * nr + 2:]
        x, y, c = _mesh_pos()
        sibling = (x, y, 1 - c)
        chips = [(1 - x, y), (x, 1 - y), (1 - x, 1 - y)]

        def copy(layer, r, k, piece_of, to):
            hr = placed[r].shape[1] // N_DEV
            p = 4 * piece_of[0] + 2 * piece_of[1] + piece_of[2]
            rows = out[r].at[layer, pl.ds(pl.multiple_of(p * hr, 16), hr), :]
            idx = (layer * nr + r) * per + k
            return pltpu.make_async_remote_copy(
                src_ref=rows, dst_ref=rows, send_sem=send.at[idx], recv_sem=recv.at[idx],
                device_id=to, device_id_type=MESH)

        started = []
        for layer in range(2):
            for r in range(nr):
                first = [copy(layer, r, 0, (x, y, c), sibling)]
                first += [copy(layer, r, 1 + j, (x, y, c), (*chip, c)) for j, chip in enumerate(chips)]
                for cp in first:
                    cp.start()
                started += first
        cw_first = []
        for j, chip in enumerate(chips):
            cp = pltpu.make_async_remote_copy(
                src_ref=cw_src, dst_ref=cw_out.at[2 * x + y], send_sem=cw_send.at[j], recv_sem=cw_recv.at[j],
                device_id=(*chip, c), device_id_type=MESH)
            cp.start()
            cw_first.append(cp)
        cw_mine = pltpu.make_async_copy(cw_src, cw_out.at[2 * x + y], local)
        cw_mine.start()
        for layer in range(2):
            for r in range(nr):
                for j, chip in enumerate(chips):
                    copy(layer, r, 1 + j, (*chip, c), (x, y, c)).wait_recv()
                    fwd = copy(layer, r, 4 + j, (*chip, c), sibling)
                    fwd.start()
                    started.append(fwd)
        for layer in range(2):
            for r in range(nr):
                copy(layer, r, 0, (x, y, 1 - c), (x, y, c)).wait_recv()
                for j, chip in enumerate(chips):
                    copy(layer, r, 4 + j, (*chip, 1 - c), (x, y, c)).wait_recv()
        for j, chip in enumerate(chips):
            pltpu.make_async_remote_copy(
                src_ref=cw_src, dst_ref=cw_out.at[2 * chip[0] + chip[1]], send_sem=cw_send.at[j],
                recv_sem=cw_recv.at[j], device_id=(*chip, c), device_id_type=MESH).wait_recv()
        for cp in started + cw_first:
            cp.wait_send()
        cw_mine.wait()

    out_shape = [jax.ShapeDtypeStruct(p.shape, bf16) for p in placed]
    out_shape.append(jax.ShapeDtypeStruct((N_CHIP,) + conv_w.shape, f32))
    res = pl.pallas_call(
        body, name="all_gather_weights",
        in_specs=[ANY] * (nr + 1), out_specs=[ANY] * (nr + 1), out_shape=out_shape,
        input_output_aliases={r: r for r in range(nr)},
        scratch_shapes=[pltpu.SemaphoreType.DMA((n_sem,)), pltpu.SemaphoreType.DMA((n_sem,)),
                        pltpu.SemaphoreType.DMA, pltpu.SemaphoreType.DMA((3,)), pltpu.SemaphoreType.DMA((3,))],
        compiler_params=pltpu.CompilerParams(has_side_effects=True),
    )(*placed, conv_w)
    return res[:nr], res[nr]


def _pair_exchange(grads):
    nr = len(grads)
    n_sem = nr * 2 * N_CHIP

    def body(*refs):
        src = refs[:nr]
        out = refs[nr:2 * nr]
        send, recv = refs[2 * nr:]
        x, y, c = _mesh_pos()
        copies = []
        for r in range(nr):
            hr = grads[r].shape[1] // N_DEV
            for layer in range(2):
                for j in range(N_CHIP):
                    idx = (r * 2 + layer) * N_CHIP + j
                    start = pl.multiple_of((2 * j + 1 - c) * hr, 16)
                    cp = pltpu.make_async_remote_copy(
                        src_ref=src[r].at[layer, pl.ds(start, hr), :], dst_ref=out[r].at[layer, j],
                        send_sem=send.at[idx], recv_sem=recv.at[idx], device_id=(x, y, 1 - c), device_id_type=MESH)
                    cp.start()
                    copies.append(cp)
        for cp in copies:
            cp.wait()

    return pl.pallas_call(
        body, name="grad_pair_exchange",
        in_specs=[ANY] * nr, out_specs=[ANY] * nr,
        out_shape=[jax.ShapeDtypeStruct((2, N_CHIP, g.shape[1] // N_DEV, g.shape[2]), bf16) for g in grads],
        scratch_shapes=[pltpu.SemaphoreType.DMA((n_sem,)), pltpu.SemaphoreType.DMA((n_sem,))],
        compiler_params=pltpu.CompilerParams(has_side_effects=True),
    )(*grads)


def _pair_sum(grad, other, core, chip, name):
    _, rows, cols = grad.shape
    hr = rows // N_DEV
    g5 = grad.reshape(2, N_CHIP, 2, hr, cols)
    where = jnp.stack([core, chip]).astype(jnp.int32)

    def body(where_ref, g_ref, o_ref, s_ref, mine_ref):
        s = (g_ref[...].astype(f32) + o_ref[...].astype(f32)).astype(bf16)
        s_ref[...] = s

        @pl.when(pl.program_id(1) == where_ref[1])
        def _():
            mine_ref[...] = s

    return pl.pallas_call(
        body, name=name,
        grid_spec=pltpu.PrefetchScalarGridSpec(
            num_scalar_prefetch=1, grid=(2, N_CHIP),
            in_specs=[pl.BlockSpec((None, None, None, hr, cols), lambda l, j, w: (l, j, w[0], 0, 0)),
                      pl.BlockSpec((None, None, hr, cols), lambda l, j, w: (l, j, 0, 0))],
            out_specs=[pl.BlockSpec((None, None, hr, cols), lambda l, j, w: (l, j, 0, 0)),
                       pl.BlockSpec((None, None, hr, cols), lambda l, j, w: (l, w[1], 0, 0))]),
        out_shape=[jax.ShapeDtypeStruct((2, N_CHIP, hr, cols), bf16)] * 2,
        compiler_params=_params(("arbitrary", "arbitrary")),
    )(where, g5, other)


def _chip_exchange(sums, landing):
    nr = len(sums)
    n_sem = nr * 2 * 3

    def body(*refs):
        src = refs[:nr]
        out = refs[2 * nr:3 * nr]
        send, recv = refs[3 * nr:]
        x, y, c = _mesh_pos()
        my_chip = 2 * x + y
        chips = [(1 - x, y), (x, 1 - y), (1 - x, 1 - y)]
        copies = []
        for r in range(nr):
            for layer in range(2):
                for k, chip in enumerate(chips):
                    idx = (r * 2 + layer) * 3 + k
                    cp = pltpu.make_async_remote_copy(
                        src_ref=src[r].at[layer, 2 * chip[0] + chip[1]], dst_ref=out[r].at[layer, my_chip],
                        send_sem=send.at[idx], recv_sem=recv.at[idx], device_id=(*chip, c), device_id_type=MESH)
                    cp.start()
                    copies.append(cp)
        for cp in copies:
            cp.wait()

    return pl.pallas_call(
        body, name="grad_chip_exchange",
        in_specs=[ANY] * (2 * nr), out_specs=[ANY] * nr,
        out_shape=[jax.ShapeDtypeStruct(s.shape, s.dtype) for s in landing],
        input_output_aliases={nr + r: r for r in range(nr)},
        scratch_shapes=[pltpu.SemaphoreType.DMA((n_sem,)), pltpu.SemaphoreType.DMA((n_sem,))],
        compiler_params=pltpu.CompilerParams(has_side_effects=True),
    )(*sums, *landing)


def _chip_sum(parts, core, name):
    _, _, hr, cols = parts.shape

    def body(core_ref, p_ref, o_ref):
        acc = p_ref[0].astype(f32) + p_ref[1].astype(f32)
        acc = acc + p_ref[2].astype(f32)
        o_ref[...] = acc + p_ref[3].astype(f32)

    return pl.pallas_call(
        body, name=name,
        grid_spec=pltpu.PrefetchScalarGridSpec(
            num_scalar_prefetch=1, grid=(2,),
            in_specs=[pl.BlockSpec((None, N_CHIP, hr, cols), lambda l, cr: (l, 0, 0, 0))],
            out_specs=pl.BlockSpec((None, None, hr, cols), lambda l, cr: (l, cr[0], 0, 0))),
        out_shape=jax.ShapeDtypeStruct((2, 2, hr, cols), f32),
        compiler_params=_params(("arbitrary",)),
    )(core, parts)


def _share_halves(halves):
    nr = len(halves)

    def body(*refs):
        out = refs[nr:2 * nr]
        send, recv = refs[2 * nr:]
        x, y, c = _mesh_pos()
        copies = []
        for r in range(nr):
            for layer in range(2):
                idx = r * 2 + layer
                cp = pltpu.make_async_remote_copy(
                    src_ref=out[r].at[layer, c], dst_ref=out[r].at[layer, c], send_sem=send.at[idx],
                    recv_sem=recv.at[idx], device_id=(x, y, 1 - c), device_id_type=MESH)
                cp.start()
                copies.append(cp)
        for r in range(nr):
            for layer in range(2):
                idx = r * 2 + layer
                copies[idx].wait_send()
                pltpu.make_async_remote_copy(
                    src_ref=out[r].at[layer, 1 - c], dst_ref=out[r].at[layer, 1 - c], send_sem=send.at[idx],
                    recv_sem=recv.at[idx], device_id=(x, y, 1 - c), device_id_type=MESH).wait_recv()

    return pl.pallas_call(
        body, name="grad_share_halves",
        in_specs=[ANY] * nr, out_specs=[ANY] * nr,
        out_shape=[jax.ShapeDtypeStruct(h.shape, h.dtype) for h in halves],
        input_output_aliases={r: r for r in range(nr)},
        scratch_shapes=[pltpu.SemaphoreType.DMA((nr * 2,))] * 2,
        compiler_params=pltpu.CompilerParams(has_side_effects=True),
    )(*halves)


def _gather_small(placed):
    n = len(placed)

    def body(*refs):
        out = refs[n:2 * n]
        send, recv = refs[2 * n:]
        x, y, c = _mesh_pos()
        me = 4 * x + 2 * y + c
        peers = []
        for fx in range(2):
            for fy in range(2):
                for fc in range(2):
                    if fx or fy or fc:
                        peers.append((1 - x if fx else x, 1 - y if fy else y, 1 - c if fc else c))
        copies = []
        for i in range(n):
            for k, to in enumerate(peers):
                cp = pltpu.make_async_remote_copy(
                    src_ref=out[i].at[0, me], dst_ref=out[i].at[0, me], send_sem=send.at[i * 7 + k],
                    recv_sem=recv.at[i * 7 + k], device_id=to, device_id_type=MESH)
                cp.start()
                copies.append(cp)
        for i in range(n):
            for k, to in enumerate(peers):
                slot = out[i].at[0, 4 * to[0] + 2 * to[1] + to[2]]
                copies[i * 7 + k].wait_send()
                pltpu.make_async_remote_copy(
                    src_ref=slot, dst_ref=slot, send_sem=send.at[i * 7 + k], recv_sem=recv.at[i * 7 + k],
                    device_id=to, device_id_type=MESH).wait_recv()

    return pl.pallas_call(
        body, name="gather_small_grads",
        in_specs=[ANY] * n, out_specs=[ANY] * n,
        out_shape=[jax.ShapeDtypeStruct(p.shape, p.dtype) for p in placed],
        input_output_aliases={i: i for i in range(n)},
        scratch_shapes=[pltpu.SemaphoreType.DMA((7 * n,)), pltpu.SemaphoreType.DMA((7 * n,))],
        compiler_params=pltpu.CompilerParams(has_side_effects=True),
    )(*placed)


def _sum_small(parts):
    n, rows, cols = parts.shape

    def body(p_ref, o_ref):
        acc = p_ref[0].astype(f32)
        for d in range(1, n):
            acc = acc + p_ref[d].astype(f32)
        o_ref[...] = acc

    return pl.pallas_call(
        body, name=f"sum_small_grads_{parts.dtype.name}", grid=(rows // 16,),
        in_specs=[pl.BlockSpec((n, 16, cols), lambda i: (0, i, 0))], out_specs=pl.BlockSpec((16, cols), lambda i: (i, 0)),
        out_shape=jax.ShapeDtypeStruct((rows, cols), f32),
        compiler_params=_params(("parallel",)),
    )(parts)


def _adamw(w, g, m, v, name):
    rows, cols = w.shape
    tr = rows
    for cand in (512, 256, 128, 64, 32, 16, 8):
        if rows % cand == 0 and rows > cand:
            tr = cand
            break

    def body(w_ref, g_ref, m_ref, v_ref, d_ref, nm_ref, nv_ref):
        gg = g_ref[...]
        nm = ADAM_B1 * m_ref[...] + (1.0 - ADAM_B1) * gg
        nv = ADAM_B2 * v_ref[...] + (1.0 - ADAM_B2) * (gg * gg)
        m_hat = nm / (1.0 - ADAM_B1 ** ADAM_STEP)
        v_hat = nv / (1.0 - ADAM_B2 ** ADAM_STEP)
        d_ref[...] = -ADAM_LR * (m_hat / (jnp.sqrt(v_hat) + ADAM_EPS) + ADAM_WD * w_ref[...])
        nm_ref[...] = nm
        nv_ref[...] = nv

    blk = pl.BlockSpec((tr, cols), lambda i: (i, 0))
    return pl.pallas_call(
        body, name=name, grid=(rows // tr,),
        in_specs=[blk] * 4, out_specs=[blk] * 3, out_shape=[jax.ShapeDtypeStruct((rows, cols), f32)] * 3,
        compiler_params=_params(("parallel",)),
    )(w, g, m, v)


SMALL = ("mix_norm", "q_norm", "k_norm", "sinks", "sgu_norm", "w_s", "b_s", "ffn_norm", "conv_b", "conv_w")


def _pack_small(arrs):
    flat = jnp.concatenate([a.reshape(-1) for a in arrs])
    pad = (-flat.shape[0]) % (16 * 1024)
    return jnp.pad(flat, (0, pad)).reshape(-1, 1024)


def _unpack_small(pack, shapes):
    flat = pack.reshape(-1)
    out, off = [], 0
    for s in shapes:
        n = int(np.prod(s))
        out.append(flat[off:off + n].reshape(s))
        off += n
    return out


def kernel(x, mix_norm, w_in, q_norm, k_norm, sinks, sgu_norm, w_s, b_s, w_oa, w_ob, w_out, ffn_norm, w_up, conv_w, conv_b, w_down, loss_target, m_mix_norm, m_w_in, m_q_norm, m_k_norm, m_sinks, m_sgu_norm, m_w_s, m_b_s, m_w_oa, m_w_ob, m_w_out, m_ffn_norm, m_w_up, m_conv_w, m_conv_b, m_w_down, v_mix_norm, v_w_in, v_q_norm, v_k_norm, v_sinks, v_sgu_norm, v_w_s, v_b_s, v_w_oa, v_w_ob, v_w_out, v_ffn_norm, v_w_up, v_conv_w, v_conv_b, v_w_down):
    weights = dict(mix_norm=mix_norm, w_in=w_in, q_norm=q_norm, k_norm=k_norm, sinks=sinks, sgu_norm=sgu_norm,
                   w_s=w_s, b_s=b_s, w_oa=w_oa, w_ob=w_ob, w_out=w_out, ffn_norm=ffn_norm, w_up=w_up,
                   conv_w=conv_w, conv_b=conv_b, w_down=w_down)
    mom_m = dict(mix_norm=m_mix_norm, w_in=m_w_in, q_norm=m_q_norm, k_norm=m_k_norm, sinks=m_sinks,
                 sgu_norm=m_sgu_norm, w_s=m_w_s, b_s=m_b_s, w_oa=m_w_oa, w_ob=m_w_ob, w_out=m_w_out,
                 ffn_norm=m_ffn_norm, w_up=m_w_up, conv_w=m_conv_w, conv_b=m_conv_b, w_down=m_w_down)
    mom_v = dict(mix_norm=v_mix_norm, w_in=v_w_in, q_norm=v_q_norm, k_norm=v_k_norm, sinks=v_sinks,
                 sgu_norm=v_sgu_norm, w_s=v_w_s, b_s=v_b_s, w_oa=v_w_oa, w_ob=v_w_ob, w_out=v_w_out,
                 ffn_norm=v_ffn_norm, w_up=v_w_up, conv_w=v_conv_w, conv_b=v_conv_b, w_down=v_w_down)
    n_seq, seq, _ = x.shape
    T = n_seq * seq
    core = lax.axis_index("c")
    chip = 2 * lax.axis_index("x") + lax.axis_index("y")
    tm = min(512, seq)
    tm_ff = min(256, seq)

    me = 2 * chip + core
    placed = []
    for name, rows, cols, transposed in REGIONS:
        w = weights[name]
        shard = jnp.swapaxes(w, 1, 2) if transposed else w
        hr = rows // N_DEV
        p = _place(shard.reshape(2, 2, hr, cols), core, N_DEV, me, bf16, f"place_{name}")
        placed.append(p.reshape(2, rows, cols))
    (w_in_t, w_oa_t, w_ob_t, w_out_f, w_up_t, w_down_f), conv_w_all = _all_gather_weights(placed, conv_w)
    conv_w_full = jnp.concatenate([conv_w_all[j] for j in range(N_CHIP)], axis=-1)

    xs = x.reshape(T, D_MODEL)
    saved = []
    cur = xs
    for l in range(2):
        b_col = b_s[l].reshape(SGU_GROUPS, BLOCK, 1)
        qkv, su, sv, ga, gb, h = _in_proj(cur, mix_norm[l][None], w_in_t, l, tm)
        y_att = _attn_fwd(qkv, q_norm[l][None], k_norm[l][None], sinks[l], n_seq, seq)
        y_sgu = _sgu_fwd(su, sv, sgu_norm[l][None], w_s[l], b_col, tm)
        x1, merged, a_o, b_o = _merge_fwd(cur, y_att, y_sgu, ga, gb, w_oa_t, w_ob_t, w_out_f, l, tm)
        h2, z, act = _ffn_up(x1, ffn_norm[l][None], w_up_t, conv_w_full[l], conv_b[l][None], l, seq, tm_ff)
        x2 = _ffn_down(x1, act, w_down_f, l, tm)
        saved.append(dict(x=cur, qkv=qkv, su=su, sv=sv, ga=ga, gb=gb, h=h, y_att=y_att, y_sgu=y_sgu, x1=x1,
                          merged=merged, a=a_o, b=b_o, h2=h2, z=z, act=act, b_col=b_col))
        cur = x2

    dy, dyb, loss_part = _loss_head(cur, loss_target.reshape(T, D_MODEL), tm)
    loss = lax.psum(loss_part[0, 0], ("x", "y", "c"))

    big = {name: None for name, *_ in REGIONS}
    small = {name: [None, None] for name in SMALL}
    for l in (1, 0):
        s = saved[l]
        dz, dconv = _ffn_bwd(dyb, s["z"], conv_w_full[l], conv_b[l][None], w_down_f, l, seq, tm_ff)
        big["w_down"] = _weight_grad(s["act"], dyb, l, big["w_down"], 1408, tm, f"dw_down_{l}")
        big["w_up"] = _weight_grad(dz, s["h2"], l, big["w_up"], 1408, tm, f"dw_up_{l}")
        dx1, dx1b, d_ffn = _norm_bwd(dz, w_up_t, l, s["x1"], ffn_norm[l][None], dy, tm, f"ffn_norm_bwd_{l}")
        small["conv_w"][l] = dconv[0:3]
        small["conv_b"][l] = dconv[3]
        small["ffn_norm"][l] = d_ffn[0]
        da, db, dga, dgb, dya, dys = _merge_bwd(dx1b, s["ga"], s["gb"], s["a"], s["b"], w_oa_t, w_ob_t, w_out_f, l, tm)
        big["w_out"] = _weight_grad(s["merged"], dx1b, l, big["w_out"], 1024, tm, f"dw_out_{l}")
        big["w_oa"] = _weight_grad(da, s["y_att"], l, big["w_oa"], 1024, tm, f"dw_oa_{l}")
        big["w_ob"] = _weight_grad(db, s["y_sgu"], l, big["w_ob"], 1024, tm, f"dw_ob_{l}")
        dsu, dsv, d_ws, d_bs, d_sgu = _sgu_bwd(dys, s["su"], s["sv"], sgu_norm[l][None], w_s[l], s["b_col"], tm)
        causal = np.tril(np.ones((BLOCK, BLOCK), bool))
        small["w_s"][l] = jnp.where(causal[None], d_ws, 0.0)
        small["b_s"][l] = d_bs[:, :, 0]
        small["sgu_norm"][l] = d_sgu[0]
        dqkv, d_qg, d_kg, d_sink = _attn_bwd(dya, s["qkv"], q_norm[l][None], k_norm[l][None], sinks[l], n_seq, seq)
        small["q_norm"][l] = d_qg[0]
        small["k_norm"][l] = d_kg[0]
        small["sinks"][l] = d_sink[:, 0]
        dproj = jnp.concatenate([dqkv, dsu, dsv, dga, dgb], axis=1)
        big["w_in"] = _weight_grad(dproj, s["h"], l, big["w_in"], 1280, tm, f"dw_in_{l}")
        dy, dyb, d_mix = _norm_bwd(dproj, w_in_t, l, s["x"], mix_norm[l][None], dx1, tm, f"mix_norm_bwd_{l}")
        small["mix_norm"][l] = d_mix[0]
    grad_x = dy.reshape(n_seq, seq, D_MODEL)

    grads = [big[name] for name, *_ in REGIONS]
    from_sibling = _pair_exchange(grads)
    core_arr = core.astype(jnp.int32).reshape(1)
    pair = [_pair_sum(g, o, core, chip, f"pair_sum_{name}") for g, o, (name, *_) in zip(grads, from_sibling, REGIONS)]
    parts = _chip_exchange([p[0] for p in pair], [p[1] for p in pair])
    halves = [_chip_sum(p, core_arr, f"chip_sum_{name}") for p, (name, *_) in zip(parts, REGIONS)]
    shared = _share_halves(halves)
    grad_big = {}
    for sh, (name, rows, cols, transposed) in zip(shared, REGIONS):
        g = sh.reshape(2, rows // N_CHIP, cols)
        grad_big[name] = jnp.swapaxes(g, 1, 2) if transposed else g

    rest = [n for n in SMALL if n != "w_s"]
    rest_shapes = [weights[n].shape if n != "conv_w" else (2, 3, 2 * D_FF) for n in rest]
    pack_rest = _pack_small([jnp.stack(small[n]) for n in rest])
    pack_ws = jnp.stack(small["w_s"]).reshape(-1, 1024)
    zero = jnp.zeros((), jnp.int32)
    gathered = _gather_small([_place(pack_rest[None, None], zero, N_DEV, me, f32, "place_small"),
                              _place(pack_ws[None, None], zero, N_DEV, me, bf16, "place_w_s")])
    grad_small = dict(zip(rest, _unpack_small(_sum_small(gathered[0][0]), rest_shapes)))
    grad_small["w_s"] = _sum_small(gathered[1][0]).reshape(w_s.shape)
    cw_cols = conv_w.shape[-1]
    grad_small["conv_w"] = lax.dynamic_slice_in_dim(grad_small["conv_w"], chip * cw_cols, cw_cols, axis=2)

    grad, delta, new_m, new_v = {}, {}, {}, {}
    for name, *_ in REGIONS:
        shp = weights[name].shape
        two_d = lambda a: a.reshape(shp[0] * shp[1], shp[2])
        grad[name] = grad_big[name]
        d, nm, nv = _adamw(two_d(weights[name]), two_d(grad[name]), two_d(mom_m[name]), two_d(mom_v[name]), f"adamw_{name}")
        delta[name], new_m[name], new_v[name] = d.reshape(shp), nm.reshape(shp), nv.reshape(shp)
    shapes = [weights[n].shape for n in SMALL]
    d, nm, nv = _adamw(_pack_small([weights[n] for n in SMALL]), _pack_small([grad_small[n] for n in SMALL]),
                       _pack_small([mom_m[n] for n in SMALL]), _pack_small([mom_v[n] for n in SMALL]), "adamw_small")
    for n, dd, mm, vv in zip(SMALL, _unpack_small(d, shapes), _unpack_small(nm, shapes), _unpack_small(nv, shapes)):
        grad[n], delta[n], new_m[n], new_v[n] = grad_small[n], dd, mm, vv

    order = ["mix_norm", "w_in", "q_norm", "k_norm", "sinks", "sgu_norm", "w_s", "b_s", "w_oa", "w_ob", "w_out",
             "ffn_norm", "w_up", "conv_w", "conv_b", "w_down"]
    return (loss, grad_x, *[grad[n] for n in order], *[delta[n] for n in order],
            *[new_m[n] for n in order], *[new_v[n] for n in order])
```

```python
import functools

import numpy as np
import jax
import jax.numpy as jnp
from jax import lax
from jax.experimental import pallas as pl
from jax.experimental.pallas import tpu as pltpu

bf16 = jnp.bfloat16
f32 = jnp.float32

D_MODEL = 1024
ATT_WIDTH = 512
KV_WIDTH = 128
SGU_WIDTH = 512
HEAD_DIM = 64
N_KV_HEADS = 2
Q_GROUP = 4
BLOCK = 128
SGU_GROUPS = 8
IN_WIDTH = 3840
D_FF = 2816
NORM_EPS = 1e-6
NEG_INF = -1e30
N_DEV = 8
N_CHIP = 4

ADAM_LR = 0.001
ADAM_B1 = 0.9
ADAM_B2 = 0.999
ADAM_EPS = 1e-08
ADAM_WD = 0.01
ADAM_STEP = 10

V7X_VMEM_LIMIT = 56 * 1024 * 1024
FF_CHUNK = 256

REGIONS = (
    ("w_in", IN_WIDTH, D_MODEL, True),
    ("w_oa", D_MODEL, ATT_WIDTH, True),
    ("w_ob", D_MODEL, SGU_WIDTH, True),
    ("w_out", D_MODEL, D_MODEL, False),
    ("w_up", 2 * D_FF, D_MODEL, True),
    ("w_down", D_FF, D_MODEL, False),
)
MESH = pl.DeviceIdType.MESH
ANY = pl.BlockSpec(memory_space=pl.ANY)


def _params(sem=None, **kw):
    return pltpu.CompilerParams(dimension_semantics=sem, vmem_limit_bytes=V7X_VMEM_LIMIT, **kw)


def _wspec(rows, cols, layer=None):
    del layer
    return pl.BlockSpec((rows, cols), lambda *_: (0, 0), pipeline_mode=pl.Buffered(1))


def _full(shape):
    nd = len(shape)
    return pl.BlockSpec(shape, lambda *_: (0,) * nd)


def _dot_nn(a, b):
    return jnp.dot(a, b, preferred_element_type=f32)


def _dot_nt(a, b):
    return lax.dot_general(a, b, (((1,), (1,)), ((), ())), preferred_element_type=f32)


def _dot_tn(a, b):
    return lax.dot_general(a, b, (((0,), (0,)), ((), ())), preferred_element_type=f32)


_GELU_C = float(np.sqrt(2.0 / np.pi))


def _gelu(x):
    return 0.5 * x * (1.0 + jnp.tanh(_GELU_C * (x + 0.044715 * x * x * x)))


def _gelu_grad(x):
    t = jnp.tanh(_GELU_C * (x + 0.044715 * x * x * x))
    du = _GELU_C * (1.0 + 3.0 * 0.044715 * x * x)
    return 0.5 * (1.0 + t) + 0.5 * x * (1.0 - t * t) * du


def _rms(x):
    return lax.rsqrt(jnp.mean(x * x, axis=-1, keepdims=True) + NORM_EPS)


def _mesh_pos():
    return lax.axis_index("x"), lax.axis_index("y"), lax.axis_index("c")


def _in_proj(x, gain, w_in_t, layer, tm):
    T = x.shape[0]

    def body(x_ref, g_ref, w_ref, qkv_ref, su_ref, sv_ref, ga_ref, gb_ref, h_ref):
        xf = x_ref[...]
        h = (xf * _rms(xf) * g_ref[...]).astype(bf16)
        h_ref[...] = h
        qkv_ref[...] = _dot_nt(h, w_ref[0:768, :])
        su_ref[...] = _dot_nt(h, w_ref[768:1280, :]).astype(bf16)
        sv_ref[...] = _dot_nt(h, w_ref[1280:1792, :]).astype(bf16)
        ga_ref[...] = _dot_nt(h, w_ref[1792:2816, :]).astype(bf16)
        gb_ref[...] = _dot_nt(h, w_ref[2816:3840, :]).astype(bf16)

    row = lambda w: pl.BlockSpec((tm, w), lambda i: (i, 0))
    return pl.pallas_call(
        body, name=f"in_proj_{layer}", grid=(T // tm,),
        in_specs=[row(D_MODEL), _full((1, D_MODEL)), _wspec(IN_WIDTH, D_MODEL, layer)],
        out_specs=[row(768), row(512), row(512), row(1024), row(1024), row(D_MODEL)],
        out_shape=[jax.ShapeDtypeStruct((T, 768), f32), jax.ShapeDtypeStruct((T, 512), bf16),
                   jax.ShapeDtypeStruct((T, 512), bf16), jax.ShapeDtypeStruct((T, 1024), bf16),
                   jax.ShapeDtypeStruct((T, 1024), bf16), jax.ShapeDtypeStruct((T, D_MODEL), bf16)],
        compiler_params=_params(("parallel",)),
    )(x, gain, w_in_t)


def _attn_head_group(cur, prev, qg, kg, sink_ref, n, hk):
    lo = hk * HEAD_DIM
    k_raw = jnp.concatenate([prev[:, lo:lo + HEAD_DIM], cur[:, 512 + lo:512 + lo + HEAD_DIM]], axis=0)
    v_band = jnp.concatenate([prev[:, 128 + lo:128 + lo + HEAD_DIM], cur[:, 640 + lo:640 + lo + HEAD_DIM]], axis=0)
    rk = _rms(k_raw)
    k_hat = k_raw * rk
    kn = (k_hat * kg).astype(bf16)
    q_raw = jnp.concatenate(
        [cur[:, (hk * Q_GROUP + g) * HEAD_DIM:(hk * Q_GROUP + g + 1) * HEAD_DIM] for g in range(Q_GROUP)], axis=0)
    rq = _rms(q_raw)
    q_hat = q_raw * rq
    qn = (q_hat * qg * (HEAD_DIM ** -0.5)).astype(bf16)
    s = _dot_nt(qn, kn)
    rows = lax.broadcasted_iota(jnp.int32, (Q_GROUP * BLOCK, 1), 0)
    g_of_row = rows // BLOCK
    qi = rows - g_of_row * BLOCK
    kj = lax.broadcasted_iota(jnp.int32, (1, 2 * BLOCK), 1)
    dist = qi + BLOCK - kj
    valid = (dist >= 0) & (dist < BLOCK) & ((kj >= BLOCK) | (n > 0))
    slope = jnp.zeros((Q_GROUP * BLOCK, 1), f32)
    sink = jnp.zeros((Q_GROUP * BLOCK, 1), f32)
    for g in range(Q_GROUP):
        head = hk * Q_GROUP + g
        slope = jnp.where(g_of_row == g, float(np.exp2(-8.0 * (head + 1.0) / 8.0)), slope)
        sink = jnp.where(g_of_row == g, sink_ref[head], sink)
    s = jnp.where(valid, s - slope * dist.astype(f32), NEG_INF)
    m = jnp.maximum(jnp.max(s, axis=-1, keepdims=True), sink)
    e = jnp.exp(s - m)
    e_sink = jnp.exp(sink - m)
    inv = 1.0 / (jnp.sum(e, axis=-1, keepdims=True) + e_sink)
    return dict(k_raw=k_raw, rk=rk, k_hat=k_hat, kn=kn, v=v_band.astype(bf16), q_hat=q_hat, rq=rq, qn=qn,
                p=e * inv, p_sink=e_sink * inv)


def _attn_fwd(qkv, qg, kg, sinks, n_seq, seq):
    T = n_seq * seq
    nb = seq // BLOCK

    def body(cur_ref, prev_ref, qg_ref, kg_ref, sink_ref, y_ref):
        n = pl.program_id(1)
        cur = cur_ref[...]
        prev = prev_ref[...]
        pieces = [None] * (N_KV_HEADS * Q_GROUP)
        for hk in range(N_KV_HEADS):
            a = _attn_head_group(cur, prev, qg_ref[...], kg_ref[...], sink_ref, n, hk)
            o = _dot_nn(a["p"].astype(bf16), a["v"])
            for g in range(Q_GROUP):
                pieces[hk * Q_GROUP + g] = o[g * BLOCK:(g + 1) * BLOCK]
        y_ref[...] = jnp.concatenate(pieces, axis=1).astype(bf16)

    return pl.pallas_call(
        body, name="attn_fwd", grid=(n_seq, nb),
        in_specs=[pl.BlockSpec((BLOCK, 768), lambda b, n: (b * nb + n, 0)),
                  pl.BlockSpec((BLOCK, 256), lambda b, n: (b * nb + jnp.maximum(n - 1, 0), 2)),
                  _full((1, HEAD_DIM)), _full((1, HEAD_DIM)),
                  pl.BlockSpec(memory_space=pltpu.SMEM)],
        out_specs=pl.BlockSpec((BLOCK, ATT_WIDTH), lambda b, n: (b * nb + n, 0)),
        out_shape=jax.ShapeDtypeStruct((T, ATT_WIDTH), bf16),
        compiler_params=_params(("parallel", "parallel")),
    )(qkv, qkv, qg, kg, sinks)


def _sgu_chunk(su, sv, gain, w_ref, b_ref):
    u = _gelu(su)
    vg = _gelu(sv)
    rv = _rms(vg)
    v_hat = vg * rv
    vn = (v_hat * gain).astype(bf16)
    causal = (lax.broadcasted_iota(jnp.int32, (BLOCK, BLOCK), 0) >= lax.broadcasted_iota(jnp.int32, (BLOCK, BLOCK), 1))
    w_tril = [jnp.where(causal, w_ref[g], 0.0).astype(bf16) for g in range(SGU_GROUPS)]
    gd = SGU_WIDTH // SGU_GROUPS
    mixed = jnp.concatenate(
        [_dot_nn(w_tril[g], vn[:, g * gd:(g + 1) * gd]) + b_ref[g] for g in range(SGU_GROUPS)], axis=1)
    return u, rv, v_hat, vn, w_tril, mixed


def _sgu_fwd(su, sv, gain, w_s, b_s, tm):
    T = su.shape[0]

    def body(su_ref, sv_ref, g_ref, w_ref, b_ref, y_ref):
        for ch in range(tm // BLOCK):
            rows = slice(ch * BLOCK, (ch + 1) * BLOCK)
            u, _, _, _, _, mixed = _sgu_chunk(su_ref[rows, :].astype(f32), sv_ref[rows, :].astype(f32),
                                              g_ref[...], w_ref, b_ref)
            y_ref[rows, :] = (u * mixed).astype(bf16)

    row = pl.BlockSpec((tm, SGU_WIDTH), lambda i: (i, 0))
    return pl.pallas_call(
        body, name="sgu_fwd", grid=(T // tm,),
        in_specs=[row, row, _full((1, SGU_WIDTH)), _full((SGU_GROUPS, BLOCK, BLOCK)), _full((SGU_GROUPS, BLOCK, 1))],
        out_specs=row, out_shape=jax.ShapeDtypeStruct((T, SGU_WIDTH), bf16),
        compiler_params=_params(("parallel",)),
    )(su, sv, gain, w_s, b_s)


def _merge_fwd(x, y_att, y_sgu, ga, gb, w_oa_t, w_ob_t, w_out, layer, tm):
    T = x.shape[0]

    def body(x_ref, ya_ref, ys_ref, ga_ref, gb_ref, woa_ref, wob_ref, wout_ref, x1_ref, m_ref, a_ref, b_ref):
        a = _dot_nt(ya_ref[...], woa_ref[...])
        b = _dot_nt(ys_ref[...], wob_ref[...])
        a_ref[...] = a.astype(bf16)
        b_ref[...] = b.astype(bf16)
        merged = (jax.nn.sigmoid(ga_ref[...].astype(f32)) * a + jax.nn.sigmoid(gb_ref[...].astype(f32)) * b).astype(bf16)
        m_ref[...] = merged
        x1_ref[...] = x_ref[...] + _dot_nn(merged, wout_ref[...])

    row = lambda w: pl.BlockSpec((tm, w), lambda i: (i, 0))
    return pl.pallas_call(
        body, name=f"merge_fwd_{layer}", grid=(T // tm,),
        in_specs=[row(D_MODEL), row(512), row(512), row(1024), row(1024),
                  _wspec(D_MODEL, ATT_WIDTH, layer), _wspec(D_MODEL, SGU_WIDTH, layer), _wspec(D_MODEL, D_MODEL, layer)],
        out_specs=[row(D_MODEL)] * 4,
        out_shape=[jax.ShapeDtypeStruct((T, D_MODEL), f32)] + [jax.ShapeDtypeStruct((T, D_MODEL), bf16)] * 3,
        compiler_params=_params(("parallel",)),
    )(x, y_att, y_sgu, ga, gb, w_oa_t, w_ob_t, w_out)


def _conv_taps(zz, h1, h2, rowid):
    z1 = jnp.where(rowid == 0, h1, pltpu.roll(zz, 1, 0))
    z2 = jnp.where(rowid == 0, h2, jnp.where(rowid == 1, h1, pltpu.roll(zz, 2, 0)))
    return z1, z2


def _ffn_up(x1, gain, w_up_t, conv_w, conv_b, layer, seq, tm):
    T = x1.shape[0]
    tps = seq // tm

    def body(x_ref, g_ref, w_ref, cw_ref, cb_ref, h2_ref, z_ref, act_ref, carry_ref):
        i = pl.program_id(0)

        @pl.when(i % tps == 0)
        def _():
            carry_ref[...] = jnp.zeros_like(carry_ref)

        xf = x_ref[...]
        h2 = (xf * _rms(xf) * g_ref[...]).astype(bf16)
        h2_ref[...] = h2
        rowid = lax.broadcasted_iota(jnp.int32, (tm, 1), 0)
        for cc in range(D_FF // FF_CHUNK):
            zc = []
            for part in range(2):
                lo = part * D_FF + cc * FF_CHUNK
                cols = slice(lo, lo + FF_CHUNK)
                zb = _dot_nt(h2, w_ref[cols, :]).astype(bf16)
                z_ref[:, cols] = zb
                zz = zb.astype(f32)
                halo = carry_ref[:, cols]
                z1, z2 = _conv_taps(zz, halo[7:8], halo[6:7], rowid)
                carry_ref[:, cols] = zz[tm - 8:tm]
                zc.append(cb_ref[:, cols] + cw_ref[0:1, cols] * z2 + cw_ref[1:2, cols] * z1 + cw_ref[2:3, cols] * zz)
            act_ref[:, cc * FF_CHUNK:(cc + 1) * FF_CHUNK] = (zc[0] * jax.nn.sigmoid(zc[0]) * zc[1]).astype(bf16)

    row = lambda w: pl.BlockSpec((tm, w), lambda i: (i, 0))
    return pl.pallas_call(
        body, name=f"ffn_up_{layer}", grid=(T // tm,),
        in_specs=[row(D_MODEL), _full((1, D_MODEL)), _wspec(2 * D_FF, D_MODEL, layer),
                  _full((3, 2 * D_FF)), _full((1, 2 * D_FF))],
        out_specs=[row(D_MODEL), row(2 * D_FF), row(D_FF)],
        out_shape=[jax.ShapeDtypeStruct((T, D_MODEL), bf16), jax.ShapeDtypeStruct((T, 2 * D_FF), bf16),
                   jax.ShapeDtypeStruct((T, D_FF), bf16)],
        scratch_shapes=[pltpu.VMEM((8, 2 * D_FF), f32)],
        compiler_params=_params(("arbitrary",)),
    )(x1, gain, w_up_t, conv_w, conv_b)


def _ffn_down(x1, act, w_down, layer, tm):
    T = x1.shape[0]

    def body(x_ref, a_ref, w_ref, o_ref):
        o_ref[...] = x_ref[...] + _dot_nn(a_ref[...], w_ref[...])

    row = lambda w: pl.BlockSpec((tm, w), lambda i: (i, 0))
    return pl.pallas_call(
        body, name=f"ffn_down_{layer}", grid=(T // tm,),
        in_specs=[row(D_MODEL), row(D_FF), _wspec(D_FF, D_MODEL, layer)],
        out_specs=row(D_MODEL), out_shape=jax.ShapeDtypeStruct((T, D_MODEL), f32),
        compiler_params=_params(("parallel",)),
    )(x1, act, w_down)


def _loss_head(y, target, tm):
    T = y.shape[0]

    def body(y_ref, t_ref, dy_ref, dyb_ref, loss_ref):
        @pl.when(pl.program_id(0) == 0)
        def _():
            loss_ref[...] = jnp.zeros_like(loss_ref)

        diff = y_ref[...] - t_ref[...]
        loss_ref[...] += 0.5 * jnp.sum(jnp.mean(diff * diff, axis=-1, keepdims=True), axis=0, keepdims=True)
        dy = diff * (1.0 / D_MODEL)
        dy_ref[...] = dy
        dyb_ref[...] = dy.astype(bf16)

    row = pl.BlockSpec((tm, D_MODEL), lambda i: (i, 0))
    return pl.pallas_call(
        body, name="loss_head", grid=(T // tm,),
        in_specs=[row, row], out_specs=[row, row, _full((8, 128))],
        out_shape=[jax.ShapeDtypeStruct((T, D_MODEL), f32), jax.ShapeDtypeStruct((T, D_MODEL), bf16),
                   jax.ShapeDtypeStruct((8, 128), f32)],
        compiler_params=_params(("arbitrary",)),
    )(y, target)


def _ffn_bwd(dx2b, z, conv_w, conv_b, w_down, layer, seq, tm):
    T = z.shape[0]
    nt = T // tm
    tps = seq // tm

    def body(dx_ref, z_ref, zh_ref, cw_ref, cb_ref, wd_ref, dz_ref, dconv_ref, carry_ref):
        i = pl.program_id(0)
        pos = (nt - 1 - i) % tps

        @pl.when(i == 0)
        def _():
            dconv_ref[...] = jnp.zeros_like(dconv_ref)

        @pl.when(pos == tps - 1)
        def _():
            carry_ref[...] = jnp.zeros_like(carry_ref)

        dxb = dx_ref[...]
        rowid = lax.broadcasted_iota(jnp.int32, (tm, 1), 0)
        halo_on = (pos > 0).astype(f32)
        for cc in range(D_FF // FF_CHUNK):
            zz, z1, z2, zc, colss = [], [], [], [], []
            for part in range(2):
                lo = part * D_FF + cc * FF_CHUNK
                cols = slice(lo, lo + FF_CHUNK)
                colss.append(cols)
                zp = z_ref[:, cols].astype(f32)
                halo = zh_ref[:, cols].astype(f32) * halo_on
                a1, a2 = _conv_taps(zp, halo[7:8], halo[6:7], rowid)
                zz.append(zp), z1.append(a1), z2.append(a2)
                zc.append(cb_ref[:, cols] + cw_ref[0:1, cols] * a2 + cw_ref[1:2, cols] * a1 + cw_ref[2:3, cols] * zp)
            d_act = _dot_nt(dxb, wd_ref[cc * FF_CHUNK:(cc + 1) * FF_CHUNK, :])
            sg = jax.nn.sigmoid(zc[0])
            silu = zc[0] * sg
            dzc = [d_act * zc[1] * sg * (1.0 + zc[0] * (1.0 - sg)), d_act * silu]
            for part in range(2):
                cols = colss[part]
                g = dzc[part]
                dconv_ref[0:1, cols] += jnp.sum(g * z2[part], axis=0, keepdims=True)
                dconv_ref[1:2, cols] += jnp.sum(g * z1[part], axis=0, keepdims=True)
                dconv_ref[2:3, cols] += jnp.sum(g * zz[part], axis=0, keepdims=True)
                dconv_ref[3:4, cols] += jnp.sum(g, axis=0, keepdims=True)
                nxt = carry_ref[:, cols]
                d1 = jnp.where(rowid == tm - 1, nxt[0:1], pltpu.roll(g, tm - 1, 0))
                d2 = jnp.where(rowid == tm - 1, nxt[1:2], jnp.where(rowid == tm - 2, nxt[0:1], pltpu.roll(g, tm - 2, 0)))
                carry_ref[:, cols] = g[0:8]
                dz_ref[:, cols] = (cw_ref[2:3, cols] * g + cw_ref[1:2, cols] * d1 + cw_ref[0:1, cols] * d2).astype(bf16)

    rev = lambda w: pl.BlockSpec((tm, w), lambda i: (nt - 1 - i, 0))
    return pl.pallas_call(
        body, name=f"ffn_bwd_{layer}", grid=(nt,),
        in_specs=[rev(D_MODEL), rev(2 * D_FF),
                  pl.BlockSpec((8, 2 * D_FF), lambda i: (jnp.maximum((nt - 1 - i) * (tm // 8) - 1, 0), 0)),
                  _full((3, 2 * D_FF)), _full((1, 2 * D_FF)), _wspec(D_FF, D_MODEL, layer)],
        out_specs=[rev(2 * D_FF), _full((8, 2 * D_FF))],
        out_shape=[jax.ShapeDtypeStruct((T, 2 * D_FF), bf16), jax.ShapeDtypeStruct((8, 2 * D_FF), f32)],
        scratch_shapes=[pltpu.VMEM((8, 2 * D_FF), f32)],
        compiler_params=_params(("arbitrary",)),
    )(dx2b, z, z, conv_w, conv_b, w_down)


def _norm_bwd(dy, w, layer, x, gain, dres, tm, name):
    T, K = dy.shape

    def body(dy_ref, w_ref, x_ref, g_ref, dres_ref, dx_ref, dxb_ref, dg_ref):
        @pl.when(pl.program_id(0) == 0)
        def _():
            dg_ref[...] = jnp.zeros_like(dg_ref)

        dh = _dot_nn(dy_ref[...], w_ref[...])
        xf = x_ref[...]
        r = _rms(xf)
        x_hat = xf * r
        dg_ref[...] += jnp.sum(dh * x_hat, axis=0, keepdims=True)
        dxh = dh * g_ref[...]
        dx = dres_ref[...] + r * (dxh - x_hat * jnp.mean(dxh * x_hat, axis=-1, keepdims=True))
        dx_ref[...] = dx
        dxb_ref[...] = dx.astype(bf16)

    row = lambda w_: pl.BlockSpec((tm, w_), lambda i: (i, 0))
    return pl.pallas_call(
        body, name=name, grid=(T // tm,),
        in_specs=[row(K), _wspec(K, D_MODEL, layer), row(D_MODEL), _full((1, D_MODEL)), row(D_MODEL)],
        out_specs=[row(D_MODEL), row(D_MODEL), _full((1, D_MODEL))],
        out_shape=[jax.ShapeDtypeStruct((T, D_MODEL), f32), jax.ShapeDtypeStruct((T, D_MODEL), bf16),
                   jax.ShapeDtypeStruct((1, D_MODEL), f32)],
        compiler_params=_params(("arbitrary",)),
    )(dy, w, x, gain, dres)


def _merge_bwd(dx1b, ga, gb, a, b, w_oa_t, w_ob_t, w_out, layer, tm):
    T = dx1b.shape[0]

    def body(dx_ref, ga_ref, gb_ref, a_ref, b_ref, woa_ref, wob_ref, wout_ref,
             da_ref, db_ref, dga_ref, dgb_ref, dya_ref, dys_ref):
        dm = _dot_nt(dx_ref[...], wout_ref[...])
        sa = jax.nn.sigmoid(ga_ref[...].astype(f32))
        sb = jax.nn.sigmoid(gb_ref[...].astype(f32))
        da = (dm * sa).astype(bf16)
        db = (dm * sb).astype(bf16)
        da_ref[...] = da
        db_ref[...] = db
        dga_ref[...] = (dm * a_ref[...].astype(f32) * sa * (1.0 - sa)).astype(bf16)
        dgb_ref[...] = (dm * b_ref[...].astype(f32) * sb * (1.0 - sb)).astype(bf16)
        dya_ref[...] = _dot_nn(da, woa_ref[...]).astype(bf16)
        dys_ref[...] = _dot_nn(db, wob_ref[...]).astype(bf16)

    row = lambda w: pl.BlockSpec((tm, w), lambda i: (i, 0))
    return pl.pallas_call(
        body, name=f"merge_bwd_{layer}", grid=(T // tm,),
        in_specs=[row(D_MODEL)] * 5 + [_wspec(D_MODEL, ATT_WIDTH, layer), _wspec(D_MODEL, SGU_WIDTH, layer),
                                       _wspec(D_MODEL, D_MODEL, layer)],
        out_specs=[row(D_MODEL)] * 4 + [row(512)] * 2,
        out_shape=[jax.ShapeDtypeStruct((T, D_MODEL), bf16)] * 4 + [jax.ShapeDtypeStruct((T, 512), bf16)] * 2,
        compiler_params=_params(("parallel",)),
    )(dx1b, ga, gb, a, b, w_oa_t, w_ob_t, w_out)


def _sgu_bwd(dy, su, sv, gain, w_s, b_s, tm):
    T = su.shape[0]
    gd = SGU_WIDTH // SGU_GROUPS

    def body(dy_ref, su_ref, sv_ref, g_ref, w_ref, b_ref, dsu_ref, dsv_ref, dw_ref, db_ref, dg_ref):
        @pl.when(pl.program_id(0) == 0)
        def _():
            dw_ref[...] = jnp.zeros_like(dw_ref)
            db_ref[...] = jnp.zeros_like(db_ref)
            dg_ref[...] = jnp.zeros_like(dg_ref)

        gain_v = g_ref[...]
        for ch in range(tm // BLOCK):
            rows = slice(ch * BLOCK, (ch + 1) * BLOCK)
            su_c = su_ref[rows, :].astype(f32)
            sv_c = sv_ref[rows, :].astype(f32)
            u, rv, v_hat, vn, w_tril, mixed = _sgu_chunk(su_c, sv_c, gain_v, w_ref, b_ref)
            dyc = dy_ref[rows, :].astype(f32)
            dsu_ref[rows, :] = (dyc * mixed * _gelu_grad(su_c)).astype(bf16)
            dmix = dyc * u
            dmix_b = dmix.astype(bf16)
            dvn = []
            for g in range(SGU_GROUPS):
                gs = slice(g * gd, (g + 1) * gd)
                db_ref[g] += jnp.sum(dmix[:, gs], axis=1, keepdims=True)
                dw_ref[g] += _dot_nt(dmix_b[:, gs], vn[:, gs])
                dvn.append(_dot_tn(w_tril[g], dmix_b[:, gs]))
            dvn = jnp.concatenate(dvn, axis=1)
            dg_ref[...] += jnp.sum(dvn * v_hat, axis=0, keepdims=True)
            dxh = dvn * gain_v
            dvg = rv * (dxh - v_hat * jnp.mean(dxh * v_hat, axis=-1, keepdims=True))
            dsv_ref[rows, :] = (dvg * _gelu_grad(sv_c)).astype(bf16)

    row = pl.BlockSpec((tm, SGU_WIDTH), lambda i: (i, 0))
    return pl.pallas_call(
        body, name="sgu_bwd", grid=(T // tm,),
        in_specs=[row, row, row, _full((1, SGU_WIDTH)), _full((SGU_GROUPS, BLOCK, BLOCK)),
                  _full((SGU_GROUPS, BLOCK, 1))],
        out_specs=[row, row, _full((SGU_GROUPS, BLOCK, BLOCK)), _full((SGU_GROUPS, BLOCK, 1)), _full((1, SGU_WIDTH))],
        out_shape=[jax.ShapeDtypeStruct((T, SGU_WIDTH), bf16)] * 2 + [
            jax.ShapeDtypeStruct((SGU_GROUPS, BLOCK, BLOCK), f32), jax.ShapeDtypeStruct((SGU_GROUPS, BLOCK, 1), f32),
            jax.ShapeDtypeStruct((1, SGU_WIDTH), f32)],
        compiler_params=_params(("arbitrary",)),
    )(dy, su, sv, gain, w_s, b_s)


def _attn_bwd(dy, qkv, qg, kg, sinks, n_seq, seq):
    T = n_seq * seq
    nb = seq // BLOCK
    scale = HEAD_DIM ** -0.5

    def body(dy_ref, cur_ref, prev_ref, qg_ref, kg_ref, sink_ref, dqkv_ref, dqg_ref, dkg_ref, dsink_ref,
             carry_k, carry_v):
        b = pl.program_id(0)
        j = pl.program_id(1)
        n = nb - 1 - j

        @pl.when((b == 0) & (j == 0))
        def _():
            dqg_ref[...] = jnp.zeros_like(dqg_ref)
            dkg_ref[...] = jnp.zeros_like(dkg_ref)
            dsink_ref[...] = jnp.zeros_like(dsink_ref)

        @pl.when(j == 0)
        def _():
            carry_k[...] = jnp.zeros_like(carry_k)
            carry_v[...] = jnp.zeros_like(carry_v)

        cur = cur_ref[...]
        prev = prev_ref[...]
        dyf = dy_ref[...].astype(f32)
        qg_v = qg_ref[...]
        kg_v = kg_ref[...]
        dq_pieces = [None] * (N_KV_HEADS * Q_GROUP)
        dk_pieces, dv_pieces = [], []
        for hk in range(N_KV_HEADS):
            a = _attn_head_group(cur, prev, qg_v, kg_v, sink_ref, n, hk)
            do = jnp.concatenate(
                [dyf[:, (hk * Q_GROUP + g) * HEAD_DIM:(hk * Q_GROUP + g + 1) * HEAD_DIM] for g in range(Q_GROUP)],
                axis=0).astype(bf16)
            p = a["p"]
            dp = _dot_nt(do, a["v"])
            dv_band = _dot_tn(p.astype(bf16), do)
            dsum = jnp.sum(p * dp, axis=-1, keepdims=True)
            ds = (p * (dp - dsum)).astype(bf16)
            dsink_col = -a["p_sink"] * dsum
            for g in range(Q_GROUP):
                head = hk * Q_GROUP + g
                dsink_ref[head:head + 1, :] += jnp.sum(dsink_col[g * BLOCK:(g + 1) * BLOCK], axis=0, keepdims=True)
            dqn = _dot_nn(ds, a["kn"])
            dkn_band = _dot_tn(ds, a["qn"])
            dq_hat_g = dqn * scale
            dqg_ref[...] += jnp.sum(dq_hat_g * a["q_hat"], axis=0, keepdims=True)
            dxh = dq_hat_g * qg_v
            dq = a["rq"] * (dxh - a["q_hat"] * jnp.mean(dxh * a["q_hat"], axis=-1, keepdims=True))
            for g in range(Q_GROUP):
                dq_pieces[hk * Q_GROUP + g] = dq[g * BLOCK:(g + 1) * BLOCK]
            dkn = dkn_band[BLOCK:] + carry_k[hk]
            dv_pieces.append(dv_band[BLOCK:] + carry_v[hk])
            carry_k[hk] = dkn_band[:BLOCK]
            carry_v[hk] = dv_band[:BLOCK]
            k_hat = a["k_hat"][BLOCK:]
            dkg_ref[...] += jnp.sum(dkn * k_hat, axis=0, keepdims=True)
            dxk = dkn * kg_v
            dk_pieces.append(a["rk"][BLOCK:] * (dxk - k_hat * jnp.mean(dxk * k_hat, axis=-1, keepdims=True)))
        dqkv_ref[...] = jnp.concatenate(dq_pieces + dk_pieces + dv_pieces, axis=1).astype(bf16)

    blk = lambda w: pl.BlockSpec((BLOCK, w), lambda b, j: (b * nb + nb - 1 - j, 0))
    return pl.pallas_call(
        body, name="attn_bwd", grid=(n_seq, nb),
        in_specs=[blk(ATT_WIDTH), blk(768),
                  pl.BlockSpec((BLOCK, 256), lambda b, j: (b * nb + jnp.maximum(nb - 2 - j, 0), 2)),
                  _full((1, HEAD_DIM)), _full((1, HEAD_DIM)), pl.BlockSpec(memory_space=pltpu.SMEM)],
        out_specs=[blk(768), _full((1, HEAD_DIM)), _full((1, HEAD_DIM)), _full((8, 128))],
        out_shape=[jax.ShapeDtypeStruct((T, 768), bf16), jax.ShapeDtypeStruct((1, HEAD_DIM), f32),
                   jax.ShapeDtypeStruct((1, HEAD_DIM), f32), jax.ShapeDtypeStruct((8, 128), f32)],
        scratch_shapes=[pltpu.VMEM((N_KV_HEADS, BLOCK, HEAD_DIM), f32), pltpu.VMEM((N_KV_HEADS, BLOCK, HEAD_DIM), f32)],
        compiler_params=_params(("arbitrary", "arbitrary")),
    )(dy, qkv, qkv, qg, kg, sinks)


def _weight_grad(a, b, tm, tk, name):
    T, M = a.shape
    N = b.shape[1]
    nk = T // tk

    def body(a_ref, b_ref, o_ref, acc_ref):
        k = pl.program_id(1)

        @pl.when(k == 0)
        def _():
            acc_ref[...] = jnp.zeros_like(acc_ref)

        acc_ref[...] += _dot_tn(a_ref[...], b_ref[...])

        @pl.when(k == nk - 1)
        def _():
            o_ref[...] = acc_ref[...].astype(bf16)

    return pl.pallas_call(
        body, name=name, grid=(M // tm, nk),
        in_specs=[pl.BlockSpec((tk, tm), lambda i, k: (k, i)), pl.BlockSpec((tk, N), lambda i, k: (k, 0))],
        out_specs=pl.BlockSpec((None, tm, N), lambda i, k: (0, i, 0)),
        out_shape=jax.ShapeDtypeStruct((1, M, N), bf16),
        scratch_shapes=[pltpu.VMEM((tm, N), f32)],
        compiler_params=_params(("parallel", "arbitrary")),
    )(a, b)


def _place(src, layer, src_slot, n_slots, dst_slot, dtype, name):
    _, _, rows, cols = src.shape
    slots = jnp.stack([src_slot, dst_slot]).astype(jnp.int32)

    def body(slots_ref, s_ref, o_ref):
        o_ref[...] = s_ref[...].astype(dtype)

    return pl.pallas_call(
        body, name=name,
        grid_spec=pltpu.PrefetchScalarGridSpec(
            num_scalar_prefetch=1, grid=(1,),
            in_specs=[pl.BlockSpec((None, None, rows, cols), lambda i, sl: (layer, sl[0], 0, 0))],
            out_specs=pl.BlockSpec((None, rows, cols), lambda i, sl: (sl[1], 0, 0))),
        out_shape=jax.ShapeDtypeStruct((n_slots, rows, cols), dtype),
        compiler_params=_params(("arbitrary",)),
    )(slots, src)


HBM = pl.BlockSpec(memory_space=pltpu.HBM)
SEM = pl.BlockSpec(memory_space=pltpu.SEMAPHORE)
DATAFLOW = pltpu.SideEffectType.DATAFLOW_SIDE_EFFECTING


def _other_chips(x, y):
    return [(1 - x, y), (x, 1 - y), (1 - x, 1 - y)]


def _split_start(groups, name):
    nb = [len(bufs) for bufs, _ in groups]
    flat = [b for bufs, _ in groups for b in bufs]
    ns = [len(plan(bufs, dry=True)) for bufs, plan in groups]
    ng = len(groups)

    def body(*refs):
        n_in = len(flat)
        sems = refs[n_in:n_in + 2 * ng]
        thru = refs[n_in + 2 * ng:2 * n_in + 2 * ng]
        token = refs[2 * n_in + 2 * ng]
        off = 0
        for g, (bufs, plan) in enumerate(groups):
            mine = thru[off:off + nb[g]]
            off += nb[g]
            for k, (src, dst, to) in enumerate(plan(mine)):
                pltpu.make_async_remote_copy(
                    src_ref=src, dst_ref=dst, send_sem=sems[2 * g].at[k], recv_sem=sems[2 * g + 1].at[k],
                    device_id=to, device_id_type=MESH).start()
        token[...] = jnp.zeros_like(token)

    out_shape = []
    for n in ns:
        out_shape += [pltpu.SemaphoreType.DMA((n,)), pltpu.SemaphoreType.DMA((n,))]
    out_shape += [pltpu.HBM(b.shape, b.dtype) for b in flat]
    out_shape.append(jax.ShapeDtypeStruct((8, 128), f32))
    res = pl.pallas_call(
        body, name=name, out_shape=tuple(out_shape),
        in_specs=[HBM] * len(flat),
        out_specs=tuple([SEM] * (2 * ng) + [HBM] * len(flat) + [pl.BlockSpec(memory_space=pltpu.VMEM)]),
        input_output_aliases={i: 2 * ng + i for i in range(len(flat))},
        compiler_params=pltpu.CompilerParams(has_side_effects=DATAFLOW),
    )(*[pltpu.with_memory_space_constraint(b, pltpu.HBM) for b in flat])
    out, off = [], 2 * ng
    for g in range(ng):
        out.append((res[2 * g], res[2 * g + 1], list(res[off:off + nb[g]])))
        off += nb[g]
    return out, res[-1]


def _split_wait(bufs, send, recv, plan, after, name):
    nb = len(bufs)

    def body(*refs):
        thru = refs[:nb]
        send_ref, recv_ref = refs[nb], refs[nb + 1]
        for k, (src, dst, to) in enumerate(plan(thru)):
            cp = pltpu.make_async_remote_copy(
                src_ref=src, dst_ref=dst, send_sem=send_ref.at[k], recv_sem=recv_ref.at[k],
                device_id=to, device_id_type=MESH)
            cp.wait_send()
            cp.wait_recv()

    res = pl.pallas_call(
        body, name=name, out_shape=tuple(pltpu.HBM(b.shape, b.dtype) for b in bufs),
        in_specs=[HBM] * nb + [SEM, SEM, ANY], out_specs=tuple([HBM] * nb),
        input_output_aliases={i: i for i in range(nb)},
        compiler_params=pltpu.CompilerParams(has_side_effects=DATAFLOW),
    )(*bufs, send, recv, after)
    return list(res)


def _gather_plan(hrs):
    def plan(refs, dry=False):
        if dry:
            return [None] * (4 * len(hrs))
        x, y, c = _mesh_pos()
        me = 4 * x + 2 * y + c
        out = []
        for ref, hr in zip(refs, hrs):
            rows = ref.at[pl.ds(pl.multiple_of(me * hr, 16), hr), :]
            out.append((rows, rows, (x, y, 1 - c)))
            out += [(rows, rows, (*chip, c)) for chip in _other_chips(x, y)]
        return out
    return plan


def _pass_to_sibling(bufs, hrs, conv_w, name):
    nb = len(bufs)
    with_cw = conv_w is not None

    def body(*refs):
        n_in = nb + (1 if with_cw else 0)
        out = refs[n_in:n_in + nb]
        scratch = refs[n_in + nb + (1 if with_cw else 0):]
        send, recv = scratch[0], scratch[1]
        x, y, c = _mesh_pos()
        chips = _other_chips(x, y)
        started = []
        if with_cw:
            cw_src, cw_out = refs[nb], refs[n_in + nb]
            cw_send, cw_recv, cw_local = scratch[2], scratch[3], scratch[4]
            for j, chip in enumerate(chips):
                cp = pltpu.make_async_remote_copy(
                    src_ref=cw_src, dst_ref=cw_out.at[2 * x + y], send_sem=cw_send.at[j], recv_sem=cw_recv.at[j],
                    device_id=(*chip, c), device_id_type=MESH)
                cp.start()
                started.append(cp)
            cw_mine = pltpu.make_async_copy(cw_src, cw_out.at[2 * x + y], cw_local)
            cw_mine.start()
        for i in range(nb):
            for j, chip in enumerate(chips):
                rows = out[i].at[pl.ds(pl.multiple_of((4 * chip[0] + 2 * chip[1] + c) * hrs[i], 16), hrs[i]), :]
                cp = pltpu.make_async_remote_copy(
                    src_ref=rows, dst_ref=rows, send_sem=send.at[3 * i + j], recv_sem=recv.at[3 * i + j],
                    device_id=(x, y, 1 - c), device_id_type=MESH)
                cp.start()
                started.append(cp)
        for i in range(nb):
            for j, chip in enumerate(chips):
                rows = out[i].at[pl.ds(pl.multiple_of((4 * chip[0] + 2 * chip[1] + 1 - c) * hrs[i], 16), hrs[i]), :]
                pltpu.make_async_remote_copy(
                    src_ref=rows, dst_ref=rows, send_sem=send.at[3 * i + j], recv_sem=recv.at[3 * i + j],
                    device_id=(x, y, 1 - c), device_id_type=MESH).wait_recv()
        if with_cw:
            for j, chip in enumerate(chips):
                pltpu.make_async_remote_copy(
                    src_ref=cw_src, dst_ref=cw_out.at[2 * chip[0] + chip[1]], send_sem=cw_send.at[j],
                    recv_sem=cw_recv.at[j], device_id=(*chip, c), device_id_type=MESH).wait_recv()
            cw_mine.wait()
        for cp in started:
            cp.wait_send()

    out_shape = [jax.ShapeDtypeStruct(b.shape, b.dtype) for b in bufs]
    scratch = [pltpu.SemaphoreType.DMA((3 * nb,)), pltpu.SemaphoreType.DMA((3 * nb,))]
    args = list(bufs)
    if with_cw:
        out_shape.append(jax.ShapeDtypeStruct((N_CHIP,) + conv_w.shape, f32))
        scratch += [pltpu.SemaphoreType.DMA((3,)), pltpu.SemaphoreType.DMA((3,)), pltpu.SemaphoreType.DMA]
        args.append(conv_w)
    res = pl.pallas_call(
        body, name=name, in_specs=[ANY] * len(args), out_specs=[ANY] * len(out_shape), out_shape=out_shape,
        input_output_aliases={i: i for i in range(nb)}, scratch_shapes=scratch,
        compiler_params=pltpu.CompilerParams(has_side_effects=True),
    )(*args)
    return (list(res[:nb]), res[nb]) if with_cw else (list(res), None)


def _pair_exchange(grads, name):
    nr = len(grads)
    n_l = grads[0].shape[0]
    n_sem = nr * n_l * N_CHIP

    def body(*refs):
        src = refs[:nr]
        out = refs[nr:2 * nr]
        send, recv = refs[2 * nr:]
        x, y, c = _mesh_pos()
        copies = []
        for r in range(nr):
            hr = grads[r].shape[1] // N_DEV
            for layer in range(n_l):
                for j in range(N_CHIP):
                    idx = (r * n_l + layer) * N_CHIP + j
                    start = pl.multiple_of((2 * j + 1 - c) * hr, 16)
                    cp = pltpu.make_async_remote_copy(
                        src_ref=src[r].at[layer, pl.ds(start, hr), :], dst_ref=out[r].at[layer, j],
                        send_sem=send.at[idx], recv_sem=recv.at[idx], device_id=(x, y, 1 - c), device_id_type=MESH)
                    cp.start()
                    copies.append(cp)
        for cp in copies:
            cp.wait()

    return pl.pallas_call(
        body, name=name,
        in_specs=[ANY] * nr, out_specs=[ANY] * nr,
        out_shape=[jax.ShapeDtypeStruct((n_l, N_CHIP, g.shape[1] // N_DEV, g.shape[2]), bf16) for g in grads],
        scratch_shapes=[pltpu.SemaphoreType.DMA((n_sem,)), pltpu.SemaphoreType.DMA((n_sem,))],
        compiler_params=pltpu.CompilerParams(has_side_effects=True),
    )(*grads)


def _pair_sum(grad, other, core, chip, name):
    n_l, rows, cols = grad.shape
    hr = rows // N_DEV
    g5 = grad.reshape(n_l, N_CHIP, 2, hr, cols)
    where = jnp.stack([core, chip]).astype(jnp.int32)

    def body(where_ref, g_ref, o_ref, s_ref, mine_ref):
        s = (g_ref[...].astype(f32) + o_ref[...].astype(f32)).astype(bf16)
        s_ref[...] = s

        @pl.when(pl.program_id(1) == where_ref[1])
        def _():
            mine_ref[...] = s

    return pl.pallas_call(
        body, name=name,
        grid_spec=pltpu.PrefetchScalarGridSpec(
            num_scalar_prefetch=1, grid=(n_l, N_CHIP),
            in_specs=[pl.BlockSpec((None, None, None, hr, cols), lambda l, j, w: (l, j, w[0], 0, 0)),
                      pl.BlockSpec((None, None, hr, cols), lambda l, j, w: (l, j, 0, 0))],
            out_specs=[pl.BlockSpec((None, None, hr, cols), lambda l, j, w: (l, j, 0, 0)),
                       pl.BlockSpec((None, None, hr, cols), lambda l, j, w: (l, w[1], 0, 0))]),
        out_shape=[jax.ShapeDtypeStruct((n_l, N_CHIP, hr, cols), bf16)] * 2,
        compiler_params=_params(("arbitrary", "arbitrary")),
    )(where, g5, other)


def _chip_plan(nr, n_l):
    def plan(refs, dry=False):
        if dry:
            return [None] * (nr * n_l * 3)
        x, y, c = _mesh_pos()
        out = []
        for r in range(nr):
            for layer in range(n_l):
                for chip in _other_chips(x, y):
                    out.append((refs[r].at[layer, 2 * chip[0] + chip[1]], refs[nr + r].at[layer, 2 * x + y], (*chip, c)))
        return out
    return plan


def _chip_sum(parts, core, name):
    n_l, _, hr, cols = parts.shape

    def body(core_ref, p_ref, o_ref):
        acc = p_ref[0].astype(f32) + p_ref[1].astype(f32)
        acc = acc + p_ref[2].astype(f32)
        o_ref[...] = acc + p_ref[3].astype(f32)

    return pl.pallas_call(
        body, name=name,
        grid_spec=pltpu.PrefetchScalarGridSpec(
            num_scalar_prefetch=1, grid=(n_l,),
            in_specs=[pl.BlockSpec((None, N_CHIP, hr, cols), lambda l, cr: (l, 0, 0, 0))],
            out_specs=pl.BlockSpec((None, None, hr, cols), lambda l, cr: (l, cr[0], 0, 0))),
        out_shape=jax.ShapeDtypeStruct((n_l, 2, hr, cols), f32),
        compiler_params=_params(("arbitrary",)),
    )(core, parts)


def _share_halves(halves):
    nr = len(halves)

    def body(*refs):
        out = refs[nr:2 * nr]
        send, recv = refs[2 * nr:]
        x, y, c = _mesh_pos()
        copies = []
        for r in range(nr):
            cp = pltpu.make_async_remote_copy(
                src_ref=out[r].at[0, c], dst_ref=out[r].at[0, c], send_sem=send.at[r],
                recv_sem=recv.at[r], device_id=(x, y, 1 - c), device_id_type=MESH)
            cp.start()
            copies.append(cp)
        for r in range(nr):
            copies[r].wait_send()
            pltpu.make_async_remote_copy(
                src_ref=out[r].at[0, 1 - c], dst_ref=out[r].at[0, 1 - c], send_sem=send.at[r],
                recv_sem=recv.at[r], device_id=(x, y, 1 - c), device_id_type=MESH).wait_recv()

    return pl.pallas_call(
        body, name="grad_share_halves",
        in_specs=[ANY] * nr, out_specs=[ANY] * nr,
        out_shape=[jax.ShapeDtypeStruct(h.shape, h.dtype) for h in halves],
        input_output_aliases={r: r for r in range(nr)},
        scratch_shapes=[pltpu.SemaphoreType.DMA((nr,))] * 2,
        compiler_params=pltpu.CompilerParams(has_side_effects=True),
    )(*halves)


def _gather_small(placed):
    n = len(placed)

    def body(*refs):
        out = refs[n:2 * n]
        send, recv = refs[2 * n:]
        x, y, c = _mesh_pos()
        me = 4 * x + 2 * y + c
        peers = []
        for fx in range(2):
            for fy in range(2):
                for fc in range(2):
                    if fx or fy or fc:
                        peers.append((1 - x if fx else x, 1 - y if fy else y, 1 - c if fc else c))
        copies = []
        for i in range(n):
            for k, to in enumerate(peers):
                cp = pltpu.make_async_remote_copy(
                    src_ref=out[i].at[me], dst_ref=out[i].at[me], send_sem=send.at[i * 7 + k],
                    recv_sem=recv.at[i * 7 + k], device_id=to, device_id_type=MESH)
                cp.start()
                copies.append(cp)
        for i in range(n):
            for k, to in enumerate(peers):
                slot = out[i].at[4 * to[0] + 2 * to[1] + to[2]]
                copies[i * 7 + k].wait_send()
                pltpu.make_async_remote_copy(
                    src_ref=slot, dst_ref=slot, send_sem=send.at[i * 7 + k], recv_sem=recv.at[i * 7 + k],
                    device_id=to, device_id_type=MESH).wait_recv()

    return pl.pallas_call(
        body, name="gather_small_grads",
        in_specs=[ANY] * n, out_specs=[ANY] * n,
        out_shape=[jax.ShapeDtypeStruct(p.shape, p.dtype) for p in placed],
        input_output_aliases={i: i for i in range(n)},
        scratch_shapes=[pltpu.SemaphoreType.DMA((7 * n,)), pltpu.SemaphoreType.DMA((7 * n,))],
        compiler_params=pltpu.CompilerParams(has_side_effects=True),
    )(*placed)


def _sum_small(parts):
    n, rows, cols = parts.shape

    def body(p_ref, o_ref):
        acc = p_ref[0].astype(f32)
        for d in range(1, n):
            acc = acc + p_ref[d].astype(f32)
        o_ref[...] = acc

    return pl.pallas_call(
        body, name=f"sum_small_grads_{parts.dtype.name}", grid=(rows // 16,),
        in_specs=[pl.BlockSpec((n, 16, cols), lambda i: (0, i, 0))], out_specs=pl.BlockSpec((16, cols), lambda i: (i, 0)),
        out_shape=jax.ShapeDtypeStruct((rows, cols), f32),
        compiler_params=_params(("parallel",)),
    )(parts)


def _adamw(w, g, m, v, name):
    rows, cols = w.shape
    tr = rows
    for cand in (512, 256, 128, 64, 32, 16, 8):
        if rows % cand == 0 and rows > cand:
            tr = cand
            break

    def body(w_ref, g_ref, m_ref, v_ref, d_ref, nm_ref, nv_ref):
        gg = g_ref[...]
        nm = ADAM_B1 * m_ref[...] + (1.0 - ADAM_B1) * gg
        nv = ADAM_B2 * v_ref[...] + (1.0 - ADAM_B2) * (gg * gg)
        m_hat = nm / (1.0 - ADAM_B1 ** ADAM_STEP)
        v_hat = nv / (1.0 - ADAM_B2 ** ADAM_STEP)
        d_ref[...] = -ADAM_LR * (m_hat / (jnp.sqrt(v_hat) + ADAM_EPS) + ADAM_WD * w_ref[...])
        nm_ref[...] = nm
        nv_ref[...] = nv

    blk = pl.BlockSpec((tr, cols), lambda i: (i, 0))
    return pl.pallas_call(
        body, name=name, grid=(rows // tr,),
        in_specs=[blk] * 4, out_specs=[blk] * 3, out_shape=[jax.ShapeDtypeStruct((rows, cols), f32)] * 3,
        compiler_params=_params(("parallel",)),
    )(w, g, m, v)


SMALL = ("mix_norm", "q_norm", "k_norm", "sinks", "sgu_norm", "w_s", "b_s", "ffn_norm", "conv_b", "conv_w")


def _pack_small(arrs):
    flat = jnp.concatenate([a.reshape(-1) for a in arrs])
    pad = (-flat.shape[0]) % (16 * 1024)
    return jnp.pad(flat, (0, pad)).reshape(-1, 1024)


def _unpack_small(pack, shapes):
    flat = pack.reshape(-1)
    out, off = [], 0
    for s in shapes:
        n = int(np.prod(s))
        out.append(flat[off:off + n].reshape(s))
        off += n
    return out


def kernel(x, mix_norm, w_in, q_norm, k_norm, sinks, sgu_norm, w_s, b_s, w_oa, w_ob, w_out, ffn_norm, w_up, conv_w, conv_b, w_down, loss_target, m_mix_norm, m_w_in, m_q_norm, m_k_norm, m_sinks, m_sgu_norm, m_w_s, m_b_s, m_w_oa, m_w_ob, m_w_out, m_ffn_norm, m_w_up, m_conv_w, m_conv_b, m_w_down, v_mix_norm, v_w_in, v_q_norm, v_k_norm, v_sinks, v_sgu_norm, v_w_s, v_b_s, v_w_oa, v_w_ob, v_w_out, v_ffn_norm, v_w_up, v_conv_w, v_conv_b, v_w_down):
    weights = dict(mix_norm=mix_norm, w_in=w_in, q_norm=q_norm, k_norm=k_norm, sinks=sinks, sgu_norm=sgu_norm,
                   w_s=w_s, b_s=b_s, w_oa=w_oa, w_ob=w_ob, w_out=w_out, ffn_norm=ffn_norm, w_up=w_up,
                   conv_w=conv_w, conv_b=conv_b, w_down=w_down)
    mom_m = dict(mix_norm=m_mix_norm, w_in=m_w_in, q_norm=m_q_norm, k_norm=m_k_norm, sinks=m_sinks,
                 sgu_norm=m_sgu_norm, w_s=m_w_s, b_s=m_b_s, w_oa=m_w_oa, w_ob=m_w_ob, w_out=m_w_out,
                 ffn_norm=m_ffn_norm, w_up=m_w_up, conv_w=m_conv_w, conv_b=m_conv_b, w_down=m_w_down)
    mom_v = dict(mix_norm=v_mix_norm, w_in=v_w_in, q_norm=v_q_norm, k_norm=v_k_norm, sinks=v_sinks,
                 sgu_norm=v_sgu_norm, w_s=v_w_s, b_s=v_b_s, w_oa=v_w_oa, w_ob=v_w_ob, w_out=v_w_out,
                 ffn_norm=v_ffn_norm, w_up=v_w_up, conv_w=v_conv_w, conv_b=v_conv_b, w_down=v_w_down)
    n_seq, seq, _ = x.shape
    T = n_seq * seq
    core = lax.axis_index("c")
    chip = 2 * lax.axis_index("x") + lax.axis_index("y")
    tm = min(512, seq)
    tm_ff = min(256, seq)

    me = 2 * chip + core
    names = [r[0] for r in REGIONS]
    hrs = {name: rows // N_DEV for name, rows, _, _ in REGIONS}
    placed = [{}, {}]
    for name, rows, cols, transposed in REGIONS:
        shard = (jnp.swapaxes(weights[name], 1, 2) if transposed else weights[name]).reshape(2, 2, hrs[name], cols)
        for l in range(2):
            placed[l][name] = _place(shard, l, core, N_DEV, me, bf16, f"place_{name}_{l}").reshape(rows, cols)
    group_keys = [[(0, "w_in")], [(0, n) for n in names[1:]], [(1, n) for n in names]]
    started, _ = _split_start(
        [([placed[l][n] for l, n in keys], _gather_plan([hrs[n] for _, n in keys])) for keys in group_keys],
        "gather_start")
    gathered = [{}, {}]

    def finish_gather(g, after):
        send, recv, bufs = started[g]
        hr_list = [hrs[n] for _, n in group_keys[g]]
        bufs = _split_wait(bufs, send, recv, _gather_plan(hr_list), after, f"gather_wait_{g}")
        bufs, cw = _pass_to_sibling(bufs, hr_list, conv_w if g == 0 else None, f"gather_pass_{g}")
        for (l, n), b in zip(group_keys[g], bufs):
            gathered[l][n] = b
        return cw

    xs = x.reshape(T, D_MODEL)
    conv_w_all = finish_gather(0, xs)
    conv_w_full = jnp.concatenate([conv_w_all[j] for j in range(N_CHIP)], axis=-1)
    saved = []
    cur = xs
    for l in range(2):
        wl = gathered[l]
        b_col = b_s[l].reshape(SGU_GROUPS, BLOCK, 1)
        qkv, su, sv, ga, gb, h = _in_proj(cur, mix_norm[l][None], wl["w_in"], l, tm)
        y_att = _attn_fwd(qkv, q_norm[l][None], k_norm[l][None], sinks[l], n_seq, seq)
        y_sgu = _sgu_fwd(su, sv, sgu_norm[l][None], w_s[l], b_col, tm)
        if l == 0:
            finish_gather(1, y_sgu)
        x1, merged, a_o, b_o = _merge_fwd(cur, y_att, y_sgu, ga, gb, wl["w_oa"], wl["w_ob"], wl["w_out"], l, tm)
        h2, z, act = _ffn_up(x1, ffn_norm[l][None], wl["w_up"], conv_w_full[l], conv_b[l][None], l, seq, tm_ff)
        x2 = _ffn_down(x1, act, wl["w_down"], l, tm)
        if l == 0:
            finish_gather(2, x2)
        saved.append(dict(x=cur, qkv=qkv, su=su, sv=sv, ga=ga, gb=gb, h=h, y_att=y_att, y_sgu=y_sgu, x1=x1,
                          merged=merged, a=a_o, b=b_o, h2=h2, z=z, act=act, b_col=b_col))
        cur = x2

    dy, dyb, loss_part = _loss_head(cur, loss_target.reshape(T, D_MODEL), tm)
    loss = lax.psum(loss_part[0, 0], ("x", "y", "c"))

    core_arr = core.astype(jnp.int32).reshape(1)
    big = [{}, {}]
    small = {name: [None, None] for name in SMALL}

    def start_reduce(l, keys, tag):
        gl = [big[l][n] for n in keys]
        from_sibling = _pair_exchange(gl, f"pair_exchange_{tag}")
        pairs = [_pair_sum(g, o, core, chip, f"pair_sum_{n}_{l}") for g, o, n in zip(gl, from_sibling, keys)]
        bufs = [p[0] for p in pairs] + [p[1] for p in pairs]
        (res,), token = _split_start([(bufs, _chip_plan(len(keys), 1))], f"chip_start_{tag}")
        return (l, keys, res, tag), token[0:1, 0:1]

    def finish_reduce(state, after):
        l, keys, (send, recv, bufs), tag = state
        bufs = _split_wait(bufs, send, recv, _chip_plan(len(keys), 1), after, f"chip_wait_{tag}")
        return {(l, n): _chip_sum(p, core_arr, f"chip_sum_{n}_{l}") for n, p in zip(keys, bufs[len(keys):])}

    pending = []
    after_start = jnp.zeros((1, 1), f32)
    for l in (1, 0):
        s = saved[l]
        wl = gathered[l]
        dz, dconv = _ffn_bwd(dyb, s["z"], conv_w_full[l], conv_b[l][None] + after_start, wl["w_down"], l, seq, tm_ff)
        big[l]["w_down"] = _weight_grad(s["act"], dyb, 1408, tm, f"dw_down_{l}")
        big[l]["w_up"] = _weight_grad(dz, s["h2"], 1408, tm, f"dw_up_{l}")
        ffn_gain = ffn_norm[l][None]
        if l == 0:
            state, tok = start_reduce(0, ["w_down", "w_up"], "0a")
            pending.append(state)
            ffn_gain = ffn_gain + tok
        dx1, dx1b, d_ffn = _norm_bwd(dz, wl["w_up"], l, s["x1"], ffn_gain, dy, tm, f"ffn_norm_bwd_{l}")
        small["conv_w"][l] = dconv[0:3]
        small["conv_b"][l] = dconv[3]
        small["ffn_norm"][l] = d_ffn[0]
        da, db, dga, dgb, dya, dys = _merge_bwd(dx1b, s["ga"], s["gb"], s["a"], s["b"],
                                                wl["w_oa"], wl["w_ob"], wl["w_out"], l, tm)
        big[l]["w_out"] = _weight_grad(s["merged"], dx1b, 1024, tm, f"dw_out_{l}")
        big[l]["w_oa"] = _weight_grad(da, s["y_att"], 1024, tm, f"dw_oa_{l}")
        big[l]["w_ob"] = _weight_grad(db, s["y_sgu"], 1024, tm, f"dw_ob_{l}")
        dsu, dsv, d_ws, d_bs, d_sgu = _sgu_bwd(dys, s["su"], s["sv"], sgu_norm[l][None], w_s[l], s["b_col"], tm)
        causal = np.tril(np.ones((BLOCK, BLOCK), bool))
        small["w_s"][l] = jnp.where(causal[None], d_ws, 0.0)
        small["b_s"][l] = d_bs[:, :, 0]
        small["sgu_norm"][l] = d_sgu[0]
        dqkv, d_qg, d_kg, d_sink = _attn_bwd(dya, s["qkv"], q_norm[l][None], k_norm[l][None], sinks[l], n_seq, seq)
        small["q_norm"][l] = d_qg[0]
        small["k_norm"][l] = d_kg[0]
        small["sinks"][l] = d_sink[:, 0]
        dproj = jnp.concatenate([dqkv, dsu, dsv, dga, dgb], axis=1)
        big[l]["w_in"] = _weight_grad(dproj, s["h"], 1280, tm, f"dw_in_{l}")
        if l == 1:
            state, after_start = start_reduce(1, names, "1")
            pending.append(state)
        dy, dyb, d_mix = _norm_bwd(dproj, wl["w_in"], l, s["x"], mix_norm[l][None], dx1, tm, f"mix_norm_bwd_{l}")
        small["mix_norm"][l] = d_mix[0]
    grad_x = dy.reshape(n_seq, seq, D_MODEL)

    state, _ = start_reduce(0, ["w_in", "w_oa", "w_ob", "w_out"], "0b")
    pending.append(state)
    halves = {}
    for state in pending:
        halves.update(finish_reduce(state, dyb))
    half_keys = [(l, n) for l in range(2) for n in names]
    shared = dict(zip(half_keys, _share_halves([halves[k] for k in half_keys])))
    grad_big = {}
    for name, rows, cols, transposed in REGIONS:
        per_layer = [shared[(l, name)].reshape(rows // N_CHIP, cols) for l in range(2)]
        grad_big[name] = jnp.stack([g.T if transposed else g for g in per_layer])

    rest = [n for n in SMALL if n != "w_s"]
    rest_shapes = [weights[n].shape if n != "conv_w" else (2, 3, 2 * D_FF) for n in rest]
    pack_rest = _pack_small([jnp.stack(small[n]) for n in rest])
    pack_ws = jnp.stack(small["w_s"]).reshape(-1, 1024)
    zero = jnp.zeros((), jnp.int32)
    all_small = _gather_small([_place(pack_rest[None, None], 0, zero, N_DEV, me, f32, "place_small"),
                               _place(pack_ws[None, None], 0, zero, N_DEV, me, bf16, "place_w_s")])
    grad_small = dict(zip(rest, _unpack_small(_sum_small(all_small[0]), rest_shapes)))
    grad_small["w_s"] = _sum_small(all_small[1]).reshape(w_s.shape)
    cw_cols = conv_w.shape[-1]
    grad_small["conv_w"] = lax.dynamic_slice_in_dim(grad_small["conv_w"], chip * cw_cols, cw_cols, axis=2)

    grad, delta, new_m, new_v = {}, {}, {}, {}
    for name, *_ in REGIONS:
        shp = weights[name].shape
        two_d = lambda a: a.reshape(shp[0] * shp[1], shp[2])
        grad[name] = grad_big[name]
        d, nm, nv = _adamw(two_d(weights[name]), two_d(grad[name]), two_d(mom_m[name]), two_d(mom_v[name]), f"adamw_{name}")
        delta[name], new_m[name], new_v[name] = d.reshape(shp), nm.reshape(shp), nv.reshape(shp)
    as_rows = lambda a: a.reshape(-1, BLOCK)
    d, nm, nv = _adamw(as_rows(w_s), as_rows(grad_small["w_s"]), as_rows(m_w_s), as_rows(v_w_s), "adamw_w_s")
    grad["w_s"], delta["w_s"], new_m["w_s"], new_v["w_s"] = (
        grad_small["w_s"], d.reshape(w_s.shape), nm.reshape(w_s.shape), nv.reshape(w_s.shape))
    shapes = [weights[n].shape for n in rest]
    d, nm, nv = _adamw(_pack_small([weights[n] for n in rest]), _pack_small([grad_small[n] for n in rest]),
                       _pack_small([mom_m[n] for n in rest]), _pack_small([mom_v[n] for n in rest]), "adamw_small")
    for n, dd, mm, vv in zip(rest, _unpack_small(d, shapes), _unpack_small(nm, shapes), _unpack_small(nv, shapes)):
        grad[n], delta[n], new_m[n], new_v[n] = grad_small[n], dd, mm, vv

    order = ["mix_norm", "w_in", "q_norm", "k_norm", "sinks", "sgu_norm", "w_s", "b_s", "w_oa", "w_ob", "w_out",
             "ffn_norm", "w_up", "conv_w", "conv_b", "w_down"]
    return (loss, grad_x, *[grad[n] for n in order], *[delta[n] for n in order],
            *[new_m[n] for n in order], *[new_v[n] for n in order])
```

```python
import functools

import numpy as np
import jax
import jax.numpy as jnp
from jax import lax
from jax.experimental import pallas as pl
from jax.experimental.pallas import tpu as pltpu

bf16 = jnp.bfloat16
f32 = jnp.float32

D_MODEL = 1024
ATT_WIDTH = 512
KV_WIDTH = 128
SGU_WIDTH = 512
HEAD_DIM = 64
N_KV_HEADS = 2
Q_GROUP = 4
BLOCK = 128
SGU_GROUPS = 8
IN_WIDTH = 3840
D_FF = 2816
NORM_EPS = 1e-6
NEG_INF = -1e30
N_DEV = 8
N_CHIP = 4

ADAM_LR = 0.001
ADAM_B1 = 0.9
ADAM_B2 = 0.999
ADAM_EPS = 1e-08
ADAM_WD = 0.01
ADAM_STEP = 10

V7X_VMEM_LIMIT = 56 * 1024 * 1024
FF_CHUNK = 256

REGIONS = (
    ("w_in", IN_WIDTH, D_MODEL, True),
    ("w_oa", D_MODEL, ATT_WIDTH, True),
    ("w_ob", D_MODEL, SGU_WIDTH, True),
    ("w_out", D_MODEL, D_MODEL, False),
    ("w_up", 2 * D_FF, D_MODEL, True),
    ("w_down", D_FF, D_MODEL, False),
)
MESH = pl.DeviceIdType.MESH
ANY = pl.BlockSpec(memory_space=pl.ANY)


def _params(sem=None, **kw):
    return pltpu.CompilerParams(dimension_semantics=sem, vmem_limit_bytes=V7X_VMEM_LIMIT, **kw)


def _wspec(rows, cols, layer=None):
    del layer
    return pl.BlockSpec((rows, cols), lambda *_: (0, 0), pipeline_mode=pl.Buffered(1))


def _full(shape):
    nd = len(shape)
    return pl.BlockSpec(shape, lambda *_: (0,) * nd)


def _dot_nn(a, b):
    return jnp.dot(a, b, preferred_element_type=f32)


def _dot_nt(a, b):
    return lax.dot_general(a, b, (((1,), (1,)), ((), ())), preferred_element_type=f32)


def _dot_tn(a, b):
    return lax.dot_general(a, b, (((0,), (0,)), ((), ())), preferred_element_type=f32)


_GELU_C = float(np.sqrt(2.0 / np.pi))


def _gelu(x):
    return 0.5 * x * (1.0 + jnp.tanh(_GELU_C * (x + 0.044715 * x * x * x)))


def _gelu_grad(x):
    t = jnp.tanh(_GELU_C * (x + 0.044715 * x * x * x))
    du = _GELU_C * (1.0 + 3.0 * 0.044715 * x * x)
    return 0.5 * (1.0 + t) + 0.5 * x * (1.0 - t * t) * du


def _rms(x):
    return lax.rsqrt(jnp.mean(x * x, axis=-1, keepdims=True) + NORM_EPS)


def _mesh_pos():
    return lax.axis_index("x"), lax.axis_index("y"), lax.axis_index("c")


def _in_proj(x, gain, w_in_t, layer, tm):
    T = x.shape[0]

    def body(x_ref, g_ref, w_ref, qkv_ref, su_ref, sv_ref, ga_ref, gb_ref, h_ref):
        xf = x_ref[...]
        h = (xf * _rms(xf) * g_ref[...]).astype(bf16)
        h_ref[...] = h
        qkv_ref[...] = _dot_nt(h, w_ref[0:768, :])
        su_ref[...] = _dot_nt(h, w_ref[768:1280, :]).astype(bf16)
        sv_ref[...] = _dot_nt(h, w_ref[1280:1792, :]).astype(bf16)
        ga_ref[...] = _dot_nt(h, w_ref[1792:2816, :]).astype(bf16)
        gb_ref[...] = _dot_nt(h, w_ref[2816:3840, :]).astype(bf16)

    row = lambda w: pl.BlockSpec((tm, w), lambda i: (i, 0))
    return pl.pallas_call(
        body, name=f"in_proj_{layer}", grid=(T // tm,),
        in_specs=[row(D_MODEL), _full((1, D_MODEL)), _wspec(IN_WIDTH, D_MODEL, layer)],
        out_specs=[row(768), row(512), row(512), row(1024), row(1024), row(D_MODEL)],
        out_shape=[jax.ShapeDtypeStruct((T, 768), f32), jax.ShapeDtypeStruct((T, 512), bf16),
                   jax.ShapeDtypeStruct((T, 512), bf16), jax.ShapeDtypeStruct((T, 1024), bf16),
                   jax.ShapeDtypeStruct((T, 1024), bf16), jax.ShapeDtypeStruct((T, D_MODEL), bf16)],
        compiler_params=_params(("parallel",)),
    )(x, gain, w_in_t)


def _attn_head_group(cur, prev, qg, kg, sink_ref, n, hk):
    lo = hk * HEAD_DIM
    k_raw = jnp.concatenate([prev[:, lo:lo + HEAD_DIM], cur[:, 512 + lo:512 + lo + HEAD_DIM]], axis=0)
    v_band = jnp.concatenate([prev[:, 128 + lo:128 + lo + HEAD_DIM], cur[:, 640 + lo:640 + lo + HEAD_DIM]], axis=0)
    rk = _rms(k_raw)
    k_hat = k_raw * rk
    kn = (k_hat * kg).astype(bf16)
    q_raw = jnp.concatenate(
        [cur[:, (hk * Q_GROUP + g) * HEAD_DIM:(hk * Q_GROUP + g + 1) * HEAD_DIM] for g in range(Q_GROUP)], axis=0)
    rq = _rms(q_raw)
    q_hat = q_raw * rq
    qn = (q_hat * qg * (HEAD_DIM ** -0.5)).astype(bf16)
    s = _dot_nt(qn, kn)
    rows = lax.broadcasted_iota(jnp.int32, (Q_GROUP * BLOCK, 1), 0)
    g_of_row = rows // BLOCK
    qi = rows - g_of_row * BLOCK
    kj = lax.broadcasted_iota(jnp.int32, (1, 2 * BLOCK), 1)
    dist = qi + BLOCK - kj
    valid = (dist >= 0) & (dist < BLOCK) & ((kj >= BLOCK) | (n > 0))
    slope = jnp.zeros((Q_GROUP * BLOCK, 1), f32)
    sink = jnp.zeros((Q_GROUP * BLOCK, 1), f32)
    for g in range(Q_GROUP):
        head = hk * Q_GROUP + g
        slope = jnp.where(g_of_row == g, float(np.exp2(-8.0 * (head + 1.0) / 8.0)), slope)
        sink = jnp.where(g_of_row == g, sink_ref[head], sink)
    s = jnp.where(valid, s - slope * dist.astype(f32), NEG_INF)
    m = jnp.maximum(jnp.max(s, axis=-1, keepdims=True), sink)
    e = jnp.exp(s - m)
    e_sink = jnp.exp(sink - m)
    inv = 1.0 / (jnp.sum(e, axis=-1, keepdims=True) + e_sink)
    return dict(k_raw=k_raw, rk=rk, k_hat=k_hat, kn=kn, v=v_band.astype(bf16), q_hat=q_hat, rq=rq, qn=qn,
                p=e * inv, p_sink=e_sink * inv)


def _attn_fwd(qkv, qg, kg, sinks, n_seq, seq):
    T = n_seq * seq
    nb = seq // BLOCK

    def body(cur_ref, prev_ref, qg_ref, kg_ref, sink_ref, y_ref):
        n = pl.program_id(1)
        cur = cur_ref[...]
        prev = prev_ref[...]
        pieces = [None] * (N_KV_HEADS * Q_GROUP)
        for hk in range(N_KV_HEADS):
            a = _attn_head_group(cur, prev, qg_ref[...], kg_ref[...], sink_ref, n, hk)
            o = _dot_nn(a["p"].astype(bf16), a["v"])
            for g in range(Q_GROUP):
                pieces[hk * Q_GROUP + g] = o[g * BLOCK:(g + 1) * BLOCK]
        y_ref[...] = jnp.concatenate(pieces, axis=1).astype(bf16)

    return pl.pallas_call(
        body, name="attn_fwd", grid=(n_seq, nb),
        in_specs=[pl.BlockSpec((BLOCK, 768), lambda b, n: (b * nb + n, 0)),
                  pl.BlockSpec((BLOCK, 256), lambda b, n: (b * nb + jnp.maximum(n - 1, 0), 2)),
                  _full((1, HEAD_DIM)), _full((1, HEAD_DIM)),
                  pl.BlockSpec(memory_space=pltpu.SMEM)],
        out_specs=pl.BlockSpec((BLOCK, ATT_WIDTH), lambda b, n: (b * nb + n, 0)),
        out_shape=jax.ShapeDtypeStruct((T, ATT_WIDTH), bf16),
        compiler_params=_params(("parallel", "parallel")),
    )(qkv, qkv, qg, kg, sinks)


def _sgu_chunk(su, sv, gain, w_ref, b_ref):
    u = _gelu(su)
    vg = _gelu(sv)
    rv = _rms(vg)
    v_hat = vg * rv
    vn = (v_hat * gain).astype(bf16)
    causal = (lax.broadcasted_iota(jnp.int32, (BLOCK, BLOCK), 0) >= lax.broadcasted_iota(jnp.int32, (BLOCK, BLOCK), 1))
    w_tril = [jnp.where(causal, w_ref[g], 0.0).astype(bf16) for g in range(SGU_GROUPS)]
    gd = SGU_WIDTH // SGU_GROUPS
    mixed = jnp.concatenate(
        [_dot_nn(w_tril[g], vn[:, g * gd:(g + 1) * gd]) + b_ref[g] for g in range(SGU_GROUPS)], axis=1)
    return u, rv, v_hat, vn, w_tril, mixed


def _sgu_fwd(su, sv, gain, w_s, b_s, tm):
    T = su.shape[0]

    def body(su_ref, sv_ref, g_ref, w_ref, b_ref, y_ref):
        for ch in range(tm // BLOCK):
            rows = slice(ch * BLOCK, (ch + 1) * BLOCK)
            u, _, _, _, _, mixed = _sgu_chunk(su_ref[rows, :].astype(f32), sv_ref[rows, :].astype(f32),
                                              g_ref[...], w_ref, b_ref)
            y_ref[rows, :] = (u * mixed).astype(bf16)

    row = pl.BlockSpec((tm, SGU_WIDTH), lambda i: (i, 0))
    return pl.pallas_call(
        body, name="sgu_fwd", grid=(T // tm,),
        in_specs=[row, row, _full((1, SGU_WIDTH)), _full((SGU_GROUPS, BLOCK, BLOCK)), _full((SGU_GROUPS, BLOCK, 1))],
        out_specs=row, out_shape=jax.ShapeDtypeStruct((T, SGU_WIDTH), bf16),
        compiler_params=_params(("parallel",)),
    )(su, sv, gain, w_s, b_s)


def _merge_fwd(x, y_att, y_sgu, ga, gb, w_oa_t, w_ob_t, w_out, layer, tm):
    T = x.shape[0]

    def body(x_ref, ya_ref, ys_ref, ga_ref, gb_ref, woa_ref, wob_ref, wout_ref, x1_ref, m_ref, a_ref, b_ref):
        a = _dot_nt(ya_ref[...], woa_ref[...])
        b = _dot_nt(ys_ref[...], wob_ref[...])
        a_ref[...] = a.astype(bf16)
        b_ref[...] = b.astype(bf16)
        merged = (jax.nn.sigmoid(ga_ref[...].astype(f32)) * a + jax.nn.sigmoid(gb_ref[...].astype(f32)) * b).astype(bf16)
        m_ref[...] = merged
        x1_ref[...] = x_ref[...] + _dot_nn(merged, wout_ref[...])

    row = lambda w: pl.BlockSpec((tm, w), lambda i: (i, 0))
    return pl.pallas_call(
        body, name=f"merge_fwd_{layer}", grid=(T // tm,),
        in_specs=[row(D_MODEL), row(512), row(512), row(1024), row(1024),
                  _wspec(D_MODEL, ATT_WIDTH, layer), _wspec(D_MODEL, SGU_WIDTH, layer), _wspec(D_MODEL, D_MODEL, layer)],
        out_specs=[row(D_MODEL)] * 4,
        out_shape=[jax.ShapeDtypeStruct((T, D_MODEL), f32)] + [jax.ShapeDtypeStruct((T, D_MODEL), bf16)] * 3,
        compiler_params=_params(("parallel",)),
    )(x, y_att, y_sgu, ga, gb, w_oa_t, w_ob_t, w_out)


def _conv_taps(zz, h1, h2, rowid):
    z1 = jnp.where(rowid == 0, h1, pltpu.roll(zz, 1, 0))
    z2 = jnp.where(rowid == 0, h2, jnp.where(rowid == 1, h1, pltpu.roll(zz, 2, 0)))
    return z1, z2


def _ffn_up(x1, gain, w_up_t, conv_w, conv_b, layer, seq, tm):
    T = x1.shape[0]
    tps = seq // tm

    def body(x_ref, g_ref, w_ref, cw_ref, cb_ref, h2_ref, z_ref, act_ref, carry_ref):
        i = pl.program_id(0)

        @pl.when(i % tps == 0)
        def _():
            carry_ref[...] = jnp.zeros_like(carry_ref)

        xf = x_ref[...]
        h2 = (xf * _rms(xf) * g_ref[...]).astype(bf16)
        h2_ref[...] = h2
        rowid = lax.broadcasted_iota(jnp.int32, (tm, 1), 0)
        for cc in range(D_FF // FF_CHUNK):
            zc = []
            for part in range(2):
                lo = part * D_FF + cc * FF_CHUNK
                cols = slice(lo, lo + FF_CHUNK)
                zb = _dot_nt(h2, w_ref[cols, :]).astype(bf16)
                z_ref[:, cols] = zb
                zz = zb.astype(f32)
                halo = carry_ref[:, cols]
                z1, z2 = _conv_taps(zz, halo[7:8], halo[6:7], rowid)
                carry_ref[:, cols] = zz[tm - 8:tm]
                zc.append(cb_ref[:, cols] + cw_ref[0:1, cols] * z2 + cw_ref[1:2, cols] * z1 + cw_ref[2:3, cols] * zz)
            act_ref[:, cc * FF_CHUNK:(cc + 1) * FF_CHUNK] = (zc[0] * jax.nn.sigmoid(zc[0]) * zc[1]).astype(bf16)

    row = lambda w: pl.BlockSpec((tm, w), lambda i: (i, 0))
    return pl.pallas_call(
        body, name=f"ffn_up_{layer}", grid=(T // tm,),
        in_specs=[row(D_MODEL), _full((1, D_MODEL)), _wspec(2 * D_FF, D_MODEL, layer),
                  _full((3, 2 * D_FF)), _full((1, 2 * D_FF))],
        out_specs=[row(D_MODEL), row(2 * D_FF), row(D_FF)],
        out_shape=[jax.ShapeDtypeStruct((T, D_MODEL), bf16), jax.ShapeDtypeStruct((T, 2 * D_FF), bf16),
                   jax.ShapeDtypeStruct((T, D_FF), bf16)],
        scratch_shapes=[pltpu.VMEM((8, 2 * D_FF), f32)],
        compiler_params=_params(("arbitrary",)),
    )(x1, gain, w_up_t, conv_w, conv_b)


def _ffn_down(x1, act, w_down, layer, tm):
    T = x1.shape[0]

    def body(x_ref, a_ref, w_ref, o_ref):
        o_ref[...] = x_ref[...] + _dot_nn(a_ref[...], w_ref[...])

    row = lambda w: pl.BlockSpec((tm, w), lambda i: (i, 0))
    return pl.pallas_call(
        body, name=f"ffn_down_{layer}", grid=(T // tm,),
        in_specs=[row(D_MODEL), row(D_FF), _wspec(D_FF, D_MODEL, layer)],
        out_specs=row(D_MODEL), out_shape=jax.ShapeDtypeStruct((T, D_MODEL), f32),
        compiler_params=_params(("parallel",)),
    )(x1, act, w_down)


def _loss_head(y, target, tm):
    T = y.shape[0]

    def body(y_ref, t_ref, dy_ref, dyb_ref, loss_ref):
        @pl.when(pl.program_id(0) == 0)
        def _():
            loss_ref[...] = jnp.zeros_like(loss_ref)

        diff = y_ref[...] - t_ref[...]
        loss_ref[...] += 0.5 * jnp.sum(jnp.mean(diff * diff, axis=-1, keepdims=True), axis=0, keepdims=True)
        dy = diff * (1.0 / D_MODEL)
        dy_ref[...] = dy
        dyb_ref[...] = dy.astype(bf16)

    row = pl.BlockSpec((tm, D_MODEL), lambda i: (i, 0))
    return pl.pallas_call(
        body, name="loss_head", grid=(T // tm,),
        in_specs=[row, row], out_specs=[row, row, _full((8, 128))],
        out_shape=[jax.ShapeDtypeStruct((T, D_MODEL), f32), jax.ShapeDtypeStruct((T, D_MODEL), bf16),
                   jax.ShapeDtypeStruct((8, 128), f32)],
        compiler_params=_params(("arbitrary",)),
    )(y, target)


def _ffn_bwd(dx2b, z, conv_w, conv_b, w_down, layer, seq, tm):
    T = z.shape[0]
    nt = T // tm
    tps = seq // tm

    def body(dx_ref, z_ref, zh_ref, cw_ref, cb_ref, wd_ref, dz_ref, dconv_ref, carry_ref):
        i = pl.program_id(0)
        pos = (nt - 1 - i) % tps

        @pl.when(i == 0)
        def _():
            dconv_ref[...] = jnp.zeros_like(dconv_ref)

        @pl.when(pos == tps - 1)
        def _():
            carry_ref[...] = jnp.zeros_like(carry_ref)

        dxb = dx_ref[...]
        rowid = lax.broadcasted_iota(jnp.int32, (tm, 1), 0)
        halo_on = (pos > 0).astype(f32)
        for cc in range(D_FF // FF_CHUNK):
            zz, z1, z2, zc, colss = [], [], [], [], []
            for part in range(2):
                lo = part * D_FF + cc * FF_CHUNK
                cols = slice(lo, lo + FF_CHUNK)
                colss.append(cols)
                zp = z_ref[:, cols].astype(f32)
                halo = zh_ref[:, cols].astype(f32) * halo_on
                a1, a2 = _conv_taps(zp, halo[7:8], halo[6:7], rowid)
                zz.append(zp), z1.append(a1), z2.append(a2)
                zc.append(cb_ref[:, cols] + cw_ref[0:1, cols] * a2 + cw_ref[1:2, cols] * a1 + cw_ref[2:3, cols] * zp)
            d_act = _dot_nt(dxb, wd_ref[cc * FF_CHUNK:(cc + 1) * FF_CHUNK, :])
            sg = jax.nn.sigmoid(zc[0])
            silu = zc[0] * sg
            dzc = [d_act * zc[1] * sg * (1.0 + zc[0] * (1.0 - sg)), d_act * silu]
            for part in range(2):
                cols = colss[part]
                g = dzc[part]
                dconv_ref[0:1, cols] += jnp.sum(g * z2[part], axis=0, keepdims=True)
                dconv_ref[1:2, cols] += jnp.sum(g * z1[part], axis=0, keepdims=True)
                dconv_ref[2:3, cols] += jnp.sum(g * zz[part], axis=0, keepdims=True)
                dconv_ref[3:4, cols] += jnp.sum(g, axis=0, keepdims=True)
                nxt = carry_ref[:, cols]
                d1 = jnp.where(rowid == tm - 1, nxt[0:1], pltpu.roll(g, tm - 1, 0))
                d2 = jnp.where(rowid == tm - 1, nxt[1:2], jnp.where(rowid == tm - 2, nxt[0:1], pltpu.roll(g, tm - 2, 0)))
                carry_ref[:, cols] = g[0:8]
                dz_ref[:, cols] = (cw_ref[2:3, cols] * g + cw_ref[1:2, cols] * d1 + cw_ref[0:1, cols] * d2).astype(bf16)

    rev = lambda w: pl.BlockSpec((tm, w), lambda i: (nt - 1 - i, 0))
    return pl.pallas_call(
        body, name=f"ffn_bwd_{layer}", grid=(nt,),
        in_specs=[rev(D_MODEL), rev(2 * D_FF),
                  pl.BlockSpec((8, 2 * D_FF), lambda i: (jnp.maximum((nt - 1 - i) * (tm // 8) - 1, 0), 0)),
                  _full((3, 2 * D_FF)), _full((1, 2 * D_FF)), _wspec(D_FF, D_MODEL, layer)],
        out_specs=[rev(2 * D_FF), _full((8, 2 * D_FF))],
        out_shape=[jax.ShapeDtypeStruct((T, 2 * D_FF), bf16), jax.ShapeDtypeStruct((8, 2 * D_FF), f32)],
        scratch_shapes=[pltpu.VMEM((8, 2 * D_FF), f32)],
        compiler_params=_params(("arbitrary",)),
    )(dx2b, z, z, conv_w, conv_b, w_down)


def _norm_bwd(dy, w, layer, x, gain, dres, tm, name):
    T, K = dy.shape

    def body(dy_ref, w_ref, x_ref, g_ref, dres_ref, dx_ref, dxb_ref, dg_ref):
        @pl.when(pl.program_id(0) == 0)
        def _():
            dg_ref[...] = jnp.zeros_like(dg_ref)

        dh = _dot_nn(dy_ref[...], w_ref[...])
        xf = x_ref[...]
        r = _rms(xf)
        x_hat = xf * r
        dg_ref[...] += jnp.sum(dh * x_hat, axis=0, keepdims=True)
        dxh = dh * g_ref[...]
        dx = dres_ref[...] + r * (dxh - x_hat * jnp.mean(dxh * x_hat, axis=-1, keepdims=True))
        dx_ref[...] = dx
        dxb_ref[...] = dx.astype(bf16)

    row = lambda w_: pl.BlockSpec((tm, w_), lambda i: (i, 0))
    return pl.pallas_call(
        body, name=name, grid=(T // tm,),
        in_specs=[row(K), _wspec(K, D_MODEL, layer), row(D_MODEL), _full((1, D_MODEL)), row(D_MODEL)],
        out_specs=[row(D_MODEL), row(D_MODEL), _full((1, D_MODEL))],
        out_shape=[jax.ShapeDtypeStruct((T, D_MODEL), f32), jax.ShapeDtypeStruct((T, D_MODEL), bf16),
                   jax.ShapeDtypeStruct((1, D_MODEL), f32)],
        compiler_params=_params(("arbitrary",)),
    )(dy, w, x, gain, dres)


def _merge_bwd(dx1b, ga, gb, a, b, w_oa_t, w_ob_t, w_out, layer, tm):
    T = dx1b.shape[0]

    def body(dx_ref, ga_ref, gb_ref, a_ref, b_ref, woa_ref, wob_ref, wout_ref,
             da_ref, db_ref, dga_ref, dgb_ref, dya_ref, dys_ref):
        dm = _dot_nt(dx_ref[...], wout_ref[...])
        sa = jax.nn.sigmoid(ga_ref[...].astype(f32))
        sb = jax.nn.sigmoid(gb_ref[...].astype(f32))
        da = (dm * sa).astype(bf16)
        db = (dm * sb).astype(bf16)
        da_ref[...] = da
        db_ref[...] = db
        dga_ref[...] = (dm * a_ref[...].astype(f32) * sa * (1.0 - sa)).astype(bf16)
        dgb_ref[...] = (dm * b_ref[...].astype(f32) * sb * (1.0 - sb)).astype(bf16)
        dya_ref[...] = _dot_nn(da, woa_ref[...]).astype(bf16)
        dys_ref[...] = _dot_nn(db, wob_ref[...]).astype(bf16)

    row = lambda w: pl.BlockSpec((tm, w), lambda i: (i, 0))
    return pl.pallas_call(
        body, name=f"merge_bwd_{layer}", grid=(T // tm,),
        in_specs=[row(D_MODEL)] * 5 + [_wspec(D_MODEL, ATT_WIDTH, layer), _wspec(D_MODEL, SGU_WIDTH, layer),
                                       _wspec(D_MODEL, D_MODEL, layer)],
        out_specs=[row(D_MODEL)] * 4 + [row(512)] * 2,
        out_shape=[jax.ShapeDtypeStruct((T, D_MODEL), bf16)] * 4 + [jax.ShapeDtypeStruct((T, 512), bf16)] * 2,
        compiler_params=_params(("parallel",)),
    )(dx1b, ga, gb, a, b, w_oa_t, w_ob_t, w_out)


def _sgu_bwd(dy, su, sv, gain, w_s, b_s, tm):
    T = su.shape[0]
    gd = SGU_WIDTH // SGU_GROUPS

    def body(dy_ref, su_ref, sv_ref, g_ref, w_ref, b_ref, dsu_ref, dsv_ref, dw_ref, db_ref, dg_ref):
        @pl.when(pl.program_id(0) == 0)
        def _():
            dw_ref[...] = jnp.zeros_like(dw_ref)
            db_ref[...] = jnp.zeros_like(db_ref)
            dg_ref[...] = jnp.zeros_like(dg_ref)

        gain_v = g_ref[...]
        for ch in range(tm // BLOCK):
            rows = slice(ch * BLOCK, (ch + 1) * BLOCK)
            su_c = su_ref[rows, :].astype(f32)
            sv_c = sv_ref[rows, :].astype(f32)
            u, rv, v_hat, vn, w_tril, mixed = _sgu_chunk(su_c, sv_c, gain_v, w_ref, b_ref)
            dyc = dy_ref[rows, :].astype(f32)
            dsu_ref[rows, :] = (dyc * mixed * _gelu_grad(su_c)).astype(bf16)
            dmix = dyc * u
            dmix_b = dmix.astype(bf16)
            dvn = []
            for g in range(SGU_GROUPS):
                gs = slice(g * gd, (g + 1) * gd)
                db_ref[g] += jnp.sum(dmix[:, gs], axis=1, keepdims=True)
                dw_ref[g] += _dot_nt(dmix_b[:, gs], vn[:, gs])
                dvn.append(_dot_tn(w_tril[g], dmix_b[:, gs]))
            dvn = jnp.concatenate(dvn, axis=1)
            dg_ref[...] += jnp.sum(dvn * v_hat, axis=0, keepdims=True)
            dxh = dvn * gain_v
            dvg = rv * (dxh - v_hat * jnp.mean(dxh * v_hat, axis=-1, keepdims=True))
            dsv_ref[rows, :] = (dvg * _gelu_grad(sv_c)).astype(bf16)

    row = pl.BlockSpec((tm, SGU_WIDTH), lambda i: (i, 0))
    return pl.pallas_call(
        body, name="sgu_bwd", grid=(T // tm,),
        in_specs=[row, row, row, _full((1, SGU_WIDTH)), _full((SGU_GROUPS, BLOCK, BLOCK)),
                  _full((SGU_GROUPS, BLOCK, 1))],
        out_specs=[row, row, _full((SGU_GROUPS, BLOCK, BLOCK)), _full((SGU_GROUPS, BLOCK, 1)), _full((1, SGU_WIDTH))],
        out_shape=[jax.ShapeDtypeStruct((T, SGU_WIDTH), bf16)] * 2 + [
            jax.ShapeDtypeStruct((SGU_GROUPS, BLOCK, BLOCK), f32), jax.ShapeDtypeStruct((SGU_GROUPS, BLOCK, 1), f32),
            jax.ShapeDtypeStruct((1, SGU_WIDTH), f32)],
        compiler_params=_params(("arbitrary",)),
    )(dy, su, sv, gain, w_s, b_s)


def _attn_bwd(dy, qkv, qg, kg, sinks, n_seq, seq):
    T = n_seq * seq
    nb = seq // BLOCK
    scale = HEAD_DIM ** -0.5

    def body(dy_ref, cur_ref, prev_ref, qg_ref, kg_ref, sink_ref, dqkv_ref, dqg_ref, dkg_ref, dsink_ref,
             carry_k, carry_v):
        b = pl.program_id(0)
        j = pl.program_id(1)
        n = nb - 1 - j

        @pl.when((b == 0) & (j == 0))
        def _():
            dqg_ref[...] = jnp.zeros_like(dqg_ref)
            dkg_ref[...] = jnp.zeros_like(dkg_ref)
            dsink_ref[...] = jnp.zeros_like(dsink_ref)

        @pl.when(j == 0)
        def _():
            carry_k[...] = jnp.zeros_like(carry_k)
            carry_v[...] = jnp.zeros_like(carry_v)

        cur = cur_ref[...]
        prev = prev_ref[...]
        dyf = dy_ref[...].astype(f32)
        qg_v = qg_ref[...]
        kg_v = kg_ref[...]
        dq_pieces = [None] * (N_KV_HEADS * Q_GROUP)
        dk_pieces, dv_pieces = [], []
        for hk in range(N_KV_HEADS):
            a = _attn_head_group(cur, prev, qg_v, kg_v, sink_ref, n, hk)
            do = jnp.concatenate(
                [dyf[:, (hk * Q_GROUP + g) * HEAD_DIM:(hk * Q_GROUP + g + 1) * HEAD_DIM] for g in range(Q_GROUP)],
                axis=0).astype(bf16)
            p = a["p"]
            dp = _dot_nt(do, a["v"])
            dv_band = _dot_tn(p.astype(bf16), do)
            dsum = jnp.sum(p * dp, axis=-1, keepdims=True)
            ds = (p * (dp - dsum)).astype(bf16)
            dsink_col = -a["p_sink"] * dsum
            for g in range(Q_GROUP):
                head = hk * Q_GROUP + g
                dsink_ref[head:head + 1, :] += jnp.sum(dsink_col[g * BLOCK:(g + 1) * BLOCK], axis=0, keepdims=True)
            dqn = _dot_nn(ds, a["kn"])
            dkn_band = _dot_tn(ds, a["qn"])
            dq_hat_g = dqn * scale
            dqg_ref[...] += jnp.sum(dq_hat_g * a["q_hat"], axis=0, keepdims=True)
            dxh = dq_hat_g * qg_v
            dq = a["rq"] * (dxh - a["q_hat"] * jnp.mean(dxh * a["q_hat"], axis=-1, keepdims=True))
            for g in range(Q_GROUP):
                dq_pieces[hk * Q_GROUP + g] = dq[g * BLOCK:(g + 1) * BLOCK]
            dkn = dkn_band[BLOCK:] + carry_k[hk]
            dv_pieces.append(dv_band[BLOCK:] + carry_v[hk])
            carry_k[hk] = dkn_band[:BLOCK]
            carry_v[hk] = dv_band[:BLOCK]
            k_hat = a["k_hat"][BLOCK:]
            dkg_ref[...] += jnp.sum(dkn * k_hat, axis=0, keepdims=True)
            dxk = dkn * kg_v
            dk_pieces.append(a["rk"][BLOCK:] * (dxk - k_hat * jnp.mean(dxk * k_hat, axis=-1, keepdims=True)))
        dqkv_ref[...] = jnp.concatenate(dq_pieces + dk_pieces + dv_pieces, axis=1).astype(bf16)

    blk = lambda w: pl.BlockSpec((BLOCK, w), lambda b, j: (b * nb + nb - 1 - j, 0))
    return pl.pallas_call(
        body, name="attn_bwd", grid=(n_seq, nb),
        in_specs=[blk(ATT_WIDTH), blk(768),
                  pl.BlockSpec((BLOCK, 256), lambda b, j: (b * nb + jnp.maximum(nb - 2 - j, 0), 2)),
                  _full((1, HEAD_DIM)), _full((1, HEAD_DIM)), pl.BlockSpec(memory_space=pltpu.SMEM)],
        out_specs=[blk(768), _full((1, HEAD_DIM)), _full((1, HEAD_DIM)), _full((8, 128))],
        out_shape=[jax.ShapeDtypeStruct((T, 768), bf16), jax.ShapeDtypeStruct((1, HEAD_DIM), f32),
                   jax.ShapeDtypeStruct((1, HEAD_DIM), f32), jax.ShapeDtypeStruct((8, 128), f32)],
        scratch_shapes=[pltpu.VMEM((N_KV_HEADS, BLOCK, HEAD_DIM), f32), pltpu.VMEM((N_KV_HEADS, BLOCK, HEAD_DIM), f32)],
        compiler_params=_params(("arbitrary", "arbitrary")),
    )(dy, qkv, qkv, qg, kg, sinks)


def _weight_grad(a, b, tm, tk, name):
    T, M = a.shape
    N = b.shape[1]
    nk = T // tk

    def body(a_ref, b_ref, o_ref, acc_ref):
        k = pl.program_id(1)

        @pl.when(k == 0)
        def _():
            acc_ref[...] = jnp.zeros_like(acc_ref)

        acc_ref[...] += _dot_tn(a_ref[...], b_ref[...])

        @pl.when(k == nk - 1)
        def _():
            o_ref[...] = acc_ref[...].astype(bf16)

    return pl.pallas_call(
        body, name=name, grid=(M // tm, nk),
        in_specs=[pl.BlockSpec((tk, tm), lambda i, k: (k, i)), pl.BlockSpec((tk, N), lambda i, k: (k, 0))],
        out_specs=pl.BlockSpec((None, tm, N), lambda i, k: (0, i, 0)),
        out_shape=jax.ShapeDtypeStruct((1, M, N), bf16),
        scratch_shapes=[pltpu.VMEM((tm, N), f32)],
        compiler_params=_params(("parallel", "arbitrary")),
    )(a, b)


def _place(src, layer, src_slot, n_slots, dst_slot, dtype, name):
    _, _, rows, cols = src.shape
    slots = jnp.stack([src_slot, dst_slot]).astype(jnp.int32)

    def body(slots_ref, s_ref, o_ref):
        o_ref[...] = s_ref[...].astype(dtype)

    return pl.pallas_call(
        body, name=name,
        grid_spec=pltpu.PrefetchScalarGridSpec(
            num_scalar_prefetch=1, grid=(1,),
            in_specs=[pl.BlockSpec((None, None, rows, cols), lambda i, sl: (layer, sl[0], 0, 0))],
            out_specs=pl.BlockSpec((None, rows, cols), lambda i, sl: (sl[1], 0, 0))),
        out_shape=jax.ShapeDtypeStruct((n_slots, rows, cols), dtype),
        compiler_params=_params(("arbitrary",)),
    )(slots, src)


HBM = pl.BlockSpec(memory_space=pltpu.HBM)
SEM = pl.BlockSpec(memory_space=pltpu.SEMAPHORE)
DATAFLOW = pltpu.SideEffectType.DATAFLOW_SIDE_EFFECTING


def _other_chips(x, y):
    return [(1 - x, y), (x, 1 - y), (1 - x, 1 - y)]


def _split_start(groups, name):
    nb = [len(bufs) for bufs, _ in groups]
    flat = [b for bufs, _ in groups for b in bufs]
    ns = [len(plan(bufs, dry=True)) for bufs, plan in groups]
    ng = len(groups)

    def body(*refs):
        n_in = len(flat)
        sems = refs[n_in:n_in + 2 * ng]
        thru = refs[n_in + 2 * ng:2 * n_in + 2 * ng]
        token = refs[2 * n_in + 2 * ng]
        off = 0
        for g, (bufs, plan) in enumerate(groups):
            mine = thru[off:off + nb[g]]
            off += nb[g]
            for k, (src, dst, to) in enumerate(plan(mine)):
                pltpu.make_async_remote_copy(
                    src_ref=src, dst_ref=dst, send_sem=sems[2 * g].at[k], recv_sem=sems[2 * g + 1].at[k],
                    device_id=to, device_id_type=MESH).start()
        token[...] = jnp.zeros_like(token)

    out_shape = []
    for n in ns:
        out_shape += [pltpu.SemaphoreType.DMA((n,)), pltpu.SemaphoreType.DMA((n,))]
    out_shape += [pltpu.HBM(b.shape, b.dtype) for b in flat]
    out_shape.append(jax.ShapeDtypeStruct((8, 128), f32))
    res = pl.pallas_call(
        body, name=name, out_shape=tuple(out_shape),
        in_specs=[HBM] * len(flat),
        out_specs=tuple([SEM] * (2 * ng) + [HBM] * len(flat) + [pl.BlockSpec(memory_space=pltpu.VMEM)]),
        input_output_aliases={i: 2 * ng + i for i in range(len(flat))},
        compiler_params=pltpu.CompilerParams(has_side_effects=DATAFLOW),
    )(*[pltpu.with_memory_space_constraint(b, pltpu.HBM) for b in flat])
    out, off = [], 2 * ng
    for g in range(ng):
        out.append((res[2 * g], res[2 * g + 1], list(res[off:off + nb[g]])))
        off += nb[g]
    return out, res[-1]


def _split_wait(bufs, send, recv, plan, after, name):
    nb = len(bufs)

    def body(*refs):
        thru = refs[:nb]
        send_ref, recv_ref = refs[nb], refs[nb + 1]
        for k, (src, dst, to) in enumerate(plan(thru)):
            cp = pltpu.make_async_remote_copy(
                src_ref=src, dst_ref=dst, send_sem=send_ref.at[k], recv_sem=recv_ref.at[k],
                device_id=to, device_id_type=MESH)
            cp.wait_send()
            cp.wait_recv()

    res = pl.pallas_call(
        body, name=name, out_shape=tuple(pltpu.HBM(b.shape, b.dtype) for b in bufs),
        in_specs=[HBM] * nb + [SEM, SEM, ANY], out_specs=tuple([HBM] * nb),
        input_output_aliases={i: i for i in range(nb)},
        compiler_params=pltpu.CompilerParams(has_side_effects=DATAFLOW),
    )(*bufs, send, recv, after)
    return list(res)


def _gather_plan(hrs, n_direct=0):
    def plan(refs, dry=False):
        if dry:
            return [None] * (4 * len(hrs) + 3 * n_direct)
        x, y, c = _mesh_pos()
        me = 4 * x + 2 * y + c
        out = []
        for i in range(n_direct):
            src, land = refs[len(hrs) + 2 * i], refs[len(hrs) + 2 * i + 1]
            out += [(src, land.at[2 * x + y], (*chip, c)) for chip in _other_chips(x, y)]
        for ref, hr in zip(refs, hrs):
            rows = ref.at[pl.ds(pl.multiple_of(me * hr, 16), hr), :]
            out.append((rows, rows, (x, y, 1 - c)))
            out += [(rows, rows, (*chip, c)) for chip in _other_chips(x, y)]
        return out
    return plan


def _all_to_all_plan(n):
    def plan(refs, dry=False):
        if dry:
            return [None] * (7 * n)
        x, y, c = _mesh_pos()
        out = []
        for ref in refs:
            mine = ref.at[4 * x + 2 * y + c]
            for fx in range(2):
                for fy in range(2):
                    for fc in range(2):
                        if fx or fy or fc:
                            out.append((mine, mine, (1 - x if fx else x, 1 - y if fy else y, 1 - c if fc else c)))
        return out
    return plan


def _pass_to_sibling(bufs, hrs, name):
    nb = len(bufs)

    def body(*refs):
        out = refs[nb:2 * nb]
        send, recv = refs[2 * nb:]
        x, y, c = _mesh_pos()
        chips = _other_chips(x, y)
        started = []
        for i in range(nb):
            for j, chip in enumerate(chips):
                rows = out[i].at[pl.ds(pl.multiple_of((4 * chip[0] + 2 * chip[1] + c) * hrs[i], 16), hrs[i]), :]
                cp = pltpu.make_async_remote_copy(
                    src_ref=rows, dst_ref=rows, send_sem=send.at[3 * i + j], recv_sem=recv.at[3 * i + j],
                    device_id=(x, y, 1 - c), device_id_type=MESH)
                cp.start()
                started.append(cp)
        for i in range(nb):
            for j, chip in enumerate(chips):
                rows = out[i].at[pl.ds(pl.multiple_of((4 * chip[0] + 2 * chip[1] + 1 - c) * hrs[i], 16), hrs[i]), :]
                pltpu.make_async_remote_copy(
                    src_ref=rows, dst_ref=rows, send_sem=send.at[3 * i + j], recv_sem=recv.at[3 * i + j],
                    device_id=(x, y, 1 - c), device_id_type=MESH).wait_recv()
        for cp in started:
            cp.wait_send()

    return list(pl.pallas_call(
        body, name=name, in_specs=[ANY] * nb, out_specs=[ANY] * nb,
        out_shape=[jax.ShapeDtypeStruct(b.shape, b.dtype) for b in bufs],
        input_output_aliases={i: i for i in range(nb)},
        scratch_shapes=[pltpu.SemaphoreType.DMA((3 * nb,)), pltpu.SemaphoreType.DMA((3 * nb,))],
        compiler_params=pltpu.CompilerParams(has_side_effects=True),
    )(*bufs))


def _pair_exchange(grads, name):
    nr = len(grads)
    n_l = grads[0].shape[0]
    n_sem = nr * n_l * N_CHIP

    def body(*refs):
        src = refs[:nr]
        out = refs[nr:2 * nr]
        send, recv = refs[2 * nr:]
        x, y, c = _mesh_pos()
        copies = []
        for r in range(nr):
            hr = grads[r].shape[1] // N_DEV
            for layer in range(n_l):
                for j in range(N_CHIP):
                    idx = (r * n_l + layer) * N_CHIP + j
                    start = pl.multiple_of((2 * j + 1 - c) * hr, 16)
                    cp = pltpu.make_async_remote_copy(
                        src_ref=src[r].at[layer, pl.ds(start, hr), :], dst_ref=out[r].at[layer, j],
                        send_sem=send.at[idx], recv_sem=recv.at[idx], device_id=(x, y, 1 - c), device_id_type=MESH)
                    cp.start()
                    copies.append(cp)
        for cp in copies:
            cp.wait()

    return pl.pallas_call(
        body, name=name,
        in_specs=[ANY] * nr, out_specs=[ANY] * nr,
        out_shape=[jax.ShapeDtypeStruct((n_l, N_CHIP, g.shape[1] // N_DEV, g.shape[2]), bf16) for g in grads],
        scratch_shapes=[pltpu.SemaphoreType.DMA((n_sem,)), pltpu.SemaphoreType.DMA((n_sem,))],
        compiler_params=pltpu.CompilerParams(has_side_effects=True),
    )(*grads)


def _pair_sum(grad, other, core, chip, name):
    n_l, rows, cols = grad.shape
    hr = rows // N_DEV
    g5 = grad.reshape(n_l, N_CHIP, 2, hr, cols)
    where = jnp.stack([core, chip]).astype(jnp.int32)

    def body(where_ref, g_ref, o_ref, s_ref, mine_ref):
        s = (g_ref[...].astype(f32) + o_ref[...].astype(f32)).astype(bf16)
        s_ref[...] = s

        @pl.when(pl.program_id(1) == where_ref[1])
        def _():
            mine_ref[...] = s

    return pl.pallas_call(
        body, name=name,
        grid_spec=pltpu.PrefetchScalarGridSpec(
            num_scalar_prefetch=1, grid=(n_l, N_CHIP),
            in_specs=[pl.BlockSpec((None, None, None, hr, cols), lambda l, j, w: (l, j, w[0], 0, 0)),
                      pl.BlockSpec((None, None, hr, cols), lambda l, j, w: (l, j, 0, 0))],
            out_specs=[pl.BlockSpec((None, None, hr, cols), lambda l, j, w: (l, j, 0, 0)),
                       pl.BlockSpec((None, None, hr, cols), lambda l, j, w: (l, w[1], 0, 0))]),
        out_shape=[jax.ShapeDtypeStruct((n_l, N_CHIP, hr, cols), bf16)] * 2,
        compiler_params=_params(("arbitrary", "arbitrary")),
    )(where, g5, other)


def _chip_plan(nr, n_l):
    def plan(refs, dry=False):
        if dry:
            return [None] * (nr * n_l * 3)
        x, y, c = _mesh_pos()
        out = []
        for r in range(nr):
            for layer in range(n_l):
                for chip in _other_chips(x, y):
                    out.append((refs[r].at[layer, 2 * chip[0] + chip[1]], refs[nr + r].at[layer, 2 * x + y], (*chip, c)))
        return out
    return plan


def _chip_sum(parts, core, name):
    n_l, _, hr, cols = parts.shape

    def body(core_ref, p_ref, o_ref):
        acc = p_ref[0].astype(f32) + p_ref[1].astype(f32)
        acc = acc + p_ref[2].astype(f32)
        o_ref[...] = acc + p_ref[3].astype(f32)

    return pl.pallas_call(
        body, name=name,
        grid_spec=pltpu.PrefetchScalarGridSpec(
            num_scalar_prefetch=1, grid=(n_l,),
            in_specs=[pl.BlockSpec((None, N_CHIP, hr, cols), lambda l, cr: (l, 0, 0, 0))],
            out_specs=pl.BlockSpec((None, None, hr, cols), lambda l, cr: (l, cr[0], 0, 0))),
        out_shape=jax.ShapeDtypeStruct((n_l, 2, hr, cols), f32),
        compiler_params=_params(("arbitrary",)),
    )(core, parts)


def _share_halves(halves):
    nr = len(halves)

    def body(*refs):
        out = refs[nr:2 * nr]
        send, recv = refs[2 * nr:]
        x, y, c = _mesh_pos()
        copies = []
        for r in range(nr):
            cp = pltpu.make_async_remote_copy(
                src_ref=out[r].at[0, c], dst_ref=out[r].at[0, c], send_sem=send.at[r],
                recv_sem=recv.at[r], device_id=(x, y, 1 - c), device_id_type=MESH)
            cp.start()
            copies.append(cp)
        for r in range(nr):
            copies[r].wait_send()
            pltpu.make_async_remote_copy(
                src_ref=out[r].at[0, 1 - c], dst_ref=out[r].at[0, 1 - c], send_sem=send.at[r],
                recv_sem=recv.at[r], device_id=(x, y, 1 - c), device_id_type=MESH).wait_recv()

    return pl.pallas_call(
        body, name="grad_share_halves",
        in_specs=[ANY] * nr, out_specs=[ANY] * nr,
        out_shape=[jax.ShapeDtypeStruct(h.shape, h.dtype) for h in halves],
        input_output_aliases={r: r for r in range(nr)},
        scratch_shapes=[pltpu.SemaphoreType.DMA((nr,))] * 2,
        compiler_params=pltpu.CompilerParams(has_side_effects=True),
    )(*halves)


def _sum_small(parts, name):
    n, rows, cols = parts.shape

    def body(p_ref, o_ref):
        acc = p_ref[0].astype(f32)
        for d in range(1, n):
            acc = acc + p_ref[d].astype(f32)
        o_ref[...] = acc

    return pl.pallas_call(
        body, name=name, grid=(rows // 16,),
        in_specs=[pl.BlockSpec((n, 16, cols), lambda i: (0, i, 0))], out_specs=pl.BlockSpec((16, cols), lambda i: (i, 0)),
        out_shape=jax.ShapeDtypeStruct((rows, cols), f32),
        compiler_params=_params(("parallel",)),
    )(parts)


def _adamw(w, g, m, v, name):
    rows, cols = w.shape
    tr = rows
    for cand in (512, 256, 128, 64, 32, 16, 8):
        if rows % cand == 0 and rows > cand:
            tr = cand
            break

    def body(w_ref, g_ref, m_ref, v_ref, d_ref, nm_ref, nv_ref):
        gg = g_ref[...]
        nm = ADAM_B1 * m_ref[...] + (1.0 - ADAM_B1) * gg
        nv = ADAM_B2 * v_ref[...] + (1.0 - ADAM_B2) * (gg * gg)
        m_hat = nm / (1.0 - ADAM_B1 ** ADAM_STEP)
        v_hat = nv / (1.0 - ADAM_B2 ** ADAM_STEP)
        d_ref[...] = -ADAM_LR * (m_hat / (jnp.sqrt(v_hat) + ADAM_EPS) + ADAM_WD * w_ref[...])
        nm_ref[...] = nm
        nv_ref[...] = nv

    blk = pl.BlockSpec((tr, cols), lambda i: (i, 0))
    return pl.pallas_call(
        body, name=name, grid=(rows // tr,),
        in_specs=[blk] * 4, out_specs=[blk] * 3, out_shape=[jax.ShapeDtypeStruct((rows, cols), f32)] * 3,
        compiler_params=_params(("parallel",)),
    )(w, g, m, v)


SMALL = ("mix_norm", "q_norm", "k_norm", "sinks", "sgu_norm", "w_s", "b_s", "ffn_norm", "conv_b", "conv_w")


def _pack_small(arrs):
    flat = jnp.concatenate([a.reshape(-1) for a in arrs])
    pad = (-flat.shape[0]) % (16 * 1024)
    return jnp.pad(flat, (0, pad)).reshape(-1, 1024)


def _unpack_small(pack, shapes):
    flat = pack.reshape(-1)
    out, off = [], 0
    for s in shapes:
        n = int(np.prod(s))
        out.append(flat[off:off + n].reshape(s))
        off += n
    return out


def kernel(x, mix_norm, w_in, q_norm, k_norm, sinks, sgu_norm, w_s, b_s, w_oa, w_ob, w_out, ffn_norm, w_up, conv_w, conv_b, w_down, loss_target, m_mix_norm, m_w_in, m_q_norm, m_k_norm, m_sinks, m_sgu_norm, m_w_s, m_b_s, m_w_oa, m_w_ob, m_w_out, m_ffn_norm, m_w_up, m_conv_w, m_conv_b, m_w_down, v_mix_norm, v_w_in, v_q_norm, v_k_norm, v_sinks, v_sgu_norm, v_w_s, v_b_s, v_w_oa, v_w_ob, v_w_out, v_ffn_norm, v_w_up, v_conv_w, v_conv_b, v_w_down):
    weights = dict(mix_norm=mix_norm, w_in=w_in, q_norm=q_norm, k_norm=k_norm, sinks=sinks, sgu_norm=sgu_norm,
                   w_s=w_s, b_s=b_s, w_oa=w_oa, w_ob=w_ob, w_out=w_out, ffn_norm=ffn_norm, w_up=w_up,
                   conv_w=conv_w, conv_b=conv_b, w_down=w_down)
    mom_m = dict(mix_norm=m_mix_norm, w_in=m_w_in, q_norm=m_q_norm, k_norm=m_k_norm, sinks=m_sinks,
                 sgu_norm=m_sgu_norm, w_s=m_w_s, b_s=m_b_s, w_oa=m_w_oa, w_ob=m_w_ob, w_out=m_w_out,
                 ffn_norm=m_ffn_norm, w_up=m_w_up, conv_w=m_conv_w, conv_b=m_conv_b, w_down=m_w_down)
    mom_v = dict(mix_norm=v_mix_norm, w_in=v_w_in, q_norm=v_q_norm, k_norm=v_k_norm, sinks=v_sinks,
                 sgu_norm=v_sgu_norm, w_s=v_w_s, b_s=v_b_s, w_oa=v_w_oa, w_ob=v_w_ob, w_out=v_w_out,
                 ffn_norm=v_ffn_norm, w_up=v_w_up, conv_w=v_conv_w, conv_b=v_conv_b, w_down=v_w_down)
    n_seq, seq, _ = x.shape
    T = n_seq * seq
    core = lax.axis_index("c")
    chip = 2 * lax.axis_index("x") + lax.axis_index("y")
    tm = min(512, seq)
    tm_ff = min(256, seq)

    me = 2 * chip + core
    names = [r[0] for r in REGIONS]
    hrs = {name: rows // N_DEV for name, rows, _, _ in REGIONS}
    placed = [{}, {}]
    for name, rows, cols, transposed in REGIONS:
        shard = (jnp.swapaxes(weights[name], 1, 2) if transposed else weights[name]).reshape(2, 2, hrs[name], cols)
        for l in range(2):
            placed[l][name] = _place(shard, l, core, N_DEV, me, bf16, f"place_{name}_{l}").reshape(rows, cols)
    group_keys = [[(0, "w_in")], [(0, n) for n in names[1:]], [(1, n) for n in names]]
    group_bufs = [[placed[l][n] for l, n in keys] for keys in group_keys]
    group_bufs[0] += [conv_w, jnp.zeros((N_CHIP,) + conv_w.shape, f32)]
    n_direct = [1, 0, 0]
    plans = [_gather_plan([hrs[n] for _, n in keys], nd) for keys, nd in zip(group_keys, n_direct)]
    started, _ = _split_start(list(zip(group_bufs, plans)), "gather_start")
    gathered = [{}, {}]

    def finish_gather(g, after):
        send, recv, bufs = started[g]
        keys = group_keys[g]
        hr_list = [hrs[n] for _, n in keys]
        bufs = _split_wait(bufs, send, recv, _gather_plan(hr_list, n_direct[g]), after, f"gather_wait_{g}")
        passed = _pass_to_sibling(bufs[:len(keys)], hr_list, f"gather_pass_{g}")
        for (l, n), b in zip(keys, passed):
            gathered[l][n] = b
        return bufs[len(keys):]

    xs = x.reshape(T, D_MODEL)
    _, conv_w_land = finish_gather(0, xs)
    conv_w_all = lax.dynamic_update_slice(conv_w_land, conv_w[None], (chip, 0, 0, 0))
    conv_w_full = jnp.concatenate([conv_w_all[j] for j in range(N_CHIP)], axis=-1)
    saved = []
    cur = xs
    for l in range(2):
        wl = gathered[l]
        b_col = b_s[l].reshape(SGU_GROUPS, BLOCK, 1)
        qkv, su, sv, ga, gb, h = _in_proj(cur, mix_norm[l][None], wl["w_in"], l, tm)
        y_att = _attn_fwd(qkv, q_norm[l][None], k_norm[l][None], sinks[l], n_seq, seq)
        y_sgu = _sgu_fwd(su, sv, sgu_norm[l][None], w_s[l], b_col, tm)
        if l == 0:
            finish_gather(1, y_sgu)
        x1, merged, a_o, b_o = _merge_fwd(cur, y_att, y_sgu, ga, gb, wl["w_oa"], wl["w_ob"], wl["w_out"], l, tm)
        h2, z, act = _ffn_up(x1, ffn_norm[l][None], wl["w_up"], conv_w_full[l], conv_b[l][None], l, seq, tm_ff)
        x2 = _ffn_down(x1, act, wl["w_down"], l, tm)
        if l == 0:
            finish_gather(2, x2)
        saved.append(dict(x=cur, qkv=qkv, su=su, sv=sv, ga=ga, gb=gb, h=h, y_att=y_att, y_sgu=y_sgu, x1=x1,
                          merged=merged, a=a_o, b=b_o, h2=h2, z=z, act=act, b_col=b_col))
        cur = x2

    dy, dyb, loss_part = _loss_head(cur, loss_target.reshape(T, D_MODEL), tm)
    loss = lax.psum(loss_part[0, 0], ("x", "y", "c"))

    core_arr = core.astype(jnp.int32).reshape(1)
    big = [{}, {}]
    small = {name: [None, None] for name in SMALL}

    def start_reduce(l, keys, tag):
        gl = [big[l][n] for n in keys]
        from_sibling = _pair_exchange(gl, f"pair_exchange_{tag}")
        pairs = [_pair_sum(g, o, core, chip, f"pair_sum_{n}_{l}") for g, o, n in zip(gl, from_sibling, keys)]
        bufs = [p[0] for p in pairs] + [p[1] for p in pairs]
        (res,), token = _split_start([(bufs, _chip_plan(len(keys), 1))], f"chip_start_{tag}")
        return (l, keys, res, tag), token[0:1, 0:1]

    def finish_reduce(state, after):
        l, keys, (send, recv, bufs), tag = state
        bufs = _split_wait(bufs, send, recv, _chip_plan(len(keys), 1), after, f"chip_wait_{tag}")
        return {(l, n): _chip_sum(p, core_arr, f"chip_sum_{n}_{l}") for n, p in zip(keys, bufs[len(keys):])}

    rest = [n for n in SMALL if n != "w_s"]
    rest_shapes = [weights[n].shape[1:] if n != "conv_w" else (3, 2 * D_FF) for n in rest]
    zero = jnp.zeros((), jnp.int32)

    def start_small(l):
        packs = [(_pack_small([small[n][l] for n in rest]), f32, "small"), (small["w_s"][l].reshape(-1, 1024), bf16, "w_s")]
        bufs = [_place(p[None, None], 0, zero, N_DEV, me, dt, f"place_{tag}_{l}") for p, dt, tag in packs]
        (res,), token = _split_start([(bufs, _all_to_all_plan(2))], f"small_start_{l}")
        return res, token[0:1, 0:1]

    def finish_small(l, res, after):
        send, recv, bufs = res
        bufs = _split_wait(bufs, send, recv, _all_to_all_plan(2), after, f"small_wait_{l}")
        out = dict(zip(rest, _unpack_small(_sum_small(bufs[0], f"sum_small_{l}"), rest_shapes)))
        out["w_s"] = _sum_small(bufs[1], f"sum_w_s_{l}").reshape(w_s.shape[1:])
        return out

    pending = []
    after_start = jnp.zeros((1, 1), f32)
    for l in (1, 0):
        s = saved[l]
        wl = gathered[l]
        dz, dconv = _ffn_bwd(dyb, s["z"], conv_w_full[l], conv_b[l][None] + after_start, wl["w_down"], l, seq, tm_ff)
        big[l]["w_down"] = _weight_grad(s["act"], dyb, 1408, tm, f"dw_down_{l}")
        big[l]["w_up"] = _weight_grad(dz, s["h2"], 1408, tm, f"dw_up_{l}")
        ffn_gain = ffn_norm[l][None]
        if l == 0:
            state, tok = start_reduce(0, ["w_down", "w_up"], "0a")
            pending.append(state)
            ffn_gain = ffn_gain + tok
        dx1, dx1b, d_ffn = _norm_bwd(dz, wl["w_up"], l, s["x1"], ffn_gain, dy, tm, f"ffn_norm_bwd_{l}")
        small["conv_w"][l] = dconv[0:3]
        small["conv_b"][l] = dconv[3]
        small["ffn_norm"][l] = d_ffn[0]
        da, db, dga, dgb, dya, dys = _merge_bwd(dx1b, s["ga"], s["gb"], s["a"], s["b"],
                                                wl["w_oa"], wl["w_ob"], wl["w_out"], l, tm)
        big[l]["w_out"] = _weight_grad(s["merged"], dx1b, 1024, tm, f"dw_out_{l}")
        big[l]["w_oa"] = _weight_grad(da, s["y_att"], 1024, tm, f"dw_oa_{l}")
        big[l]["w_ob"] = _weight_grad(db, s["y_sgu"], 1024, tm, f"dw_ob_{l}")
        sgu_gain = sgu_norm[l][None]
        if l == 0:
            state, tok = start_reduce(0, ["w_out", "w_oa", "w_ob"], "0m")
            pending.append(state)
            sgu_gain = sgu_gain + tok
        dsu, dsv, d_ws, d_bs, d_sgu = _sgu_bwd(dys, s["su"], s["sv"], sgu_gain, w_s[l], s["b_col"], tm)
        causal = np.tril(np.ones((BLOCK, BLOCK), bool))
        small["w_s"][l] = jnp.where(causal[None], d_ws, 0.0)
        small["b_s"][l] = d_bs[:, :, 0]
        small["sgu_norm"][l] = d_sgu[0]
        dqkv, d_qg, d_kg, d_sink = _attn_bwd(dya, s["qkv"], q_norm[l][None], k_norm[l][None], sinks[l], n_seq, seq)
        small["q_norm"][l] = d_qg[0]
        small["k_norm"][l] = d_kg[0]
        small["sinks"][l] = d_sink[:, 0]
        dproj = jnp.concatenate([dqkv, dsu, dsv, dga, dgb], axis=1)
        big[l]["w_in"] = _weight_grad(dproj, s["h"], 1280, tm, f"dw_in_{l}")
        state, tok = start_reduce(l, names if l == 1 else ["w_in"], "1" if l == 1 else "0b")
        pending.append(state)
        dy, dyb, d_mix = _norm_bwd(dproj, wl["w_in"], l, s["x"], mix_norm[l][None] + tok, dx1, tm, f"mix_norm_bwd_{l}")
        small["mix_norm"][l] = d_mix[0]
        if l == 1:
            small_1, after_start = start_small(1)
    grad_x = dy.reshape(n_seq, seq, D_MODEL)

    small_0, _ = start_small(0)
    halves = {}
    for state in pending:
        halves.update(finish_reduce(state, dyb))
    half_keys = [(l, n) for l in range(2) for n in names]
    shared = dict(zip(half_keys, _share_halves([halves[k] for k in half_keys])))
    grad_big = {}
    for name, rows, cols, transposed in REGIONS:
        per_layer = [shared[(l, name)].reshape(rows // N_CHIP, cols) for l in range(2)]
        grad_big[name] = jnp.stack([g.T if transposed else g for g in per_layer])

    per_layer = [finish_small(0, small_0, dyb), finish_small(1, small_1, dyb)]
    grad_small = {n: jnp.stack([per_layer[0][n], per_layer[1][n]]) for n in SMALL}
    cw_cols = conv_w.shape[-1]
    grad_small["conv_w"] = lax.dynamic_slice_in_dim(grad_small["conv_w"], chip * cw_cols, cw_cols, axis=2)

    grad, delta, new_m, new_v = {}, {}, {}, {}
    for name, *_ in REGIONS:
        shp = weights[name].shape
        two_d = lambda a: a.reshape(shp[0] * shp[1], shp[2])
        grad[name] = grad_big[name]
        d, nm, nv = _adamw(two_d(weights[name]), two_d(grad[name]), two_d(mom_m[name]), two_d(mom_v[name]), f"adamw_{name}")
        delta[name], new_m[name], new_v[name] = d.reshape(shp), nm.reshape(shp), nv.reshape(shp)
    as_rows = lambda a: a.reshape(-1, BLOCK)
    d, nm, nv = _adamw(as_rows(w_s), as_rows(grad_small["w_s"]), as_rows(m_w_s), as_rows(v_w_s), "adamw_w_s")
    grad["w_s"], delta["w_s"], new_m["w_s"], new_v["w_s"] = (
        grad_small["w_s"], d.reshape(w_s.shape), nm.reshape(w_s.shape), nv.reshape(w_s.shape))
    shapes = [weights[n].shape for n in rest]
    d, nm, nv = _adamw(_pack_small([weights[n] for n in rest]), _pack_small([grad_small[n] for n in rest]),
                       _pack_small([mom_m[n] for n in rest]), _pack_small([mom_v[n] for n in rest]), "adamw_small")
    for n, dd, mm, vv in zip(rest, _unpack_small(d, shapes), _unpack_small(nm, shapes), _unpack_small(nv, shapes)):
        grad[n], delta[n], new_m[n], new_v[n] = grad_small[n], dd, mm, vv

    order = ["mix_norm", "w_in", "q_norm", "k_norm", "sinks", "sgu_norm", "w_s", "b_s", "w_oa", "w_ob", "w_out",
             "ffn_norm", "w_up", "conv_w", "conv_b", "w_down"]
    return (loss, grad_x, *[grad[n] for n in order], *[delta[n] for n in order],
            *[new_m[n] for n in order], *[new_v[n] for n in order])
```

```python
import functools

import numpy as np
import jax
import jax.numpy as jnp
from jax import lax
from jax.experimental import pallas as pl
from jax.experimental.pallas import tpu as pltpu

bf16 = jnp.bfloat16
f32 = jnp.float32

D_MODEL = 1024
ATT_WIDTH = 512
KV_WIDTH = 128
SGU_WIDTH = 512
HEAD_DIM = 64
N_KV_HEADS = 2
Q_GROUP = 4
BLOCK = 128
SGU_GROUPS = 8
IN_WIDTH = 3840
D_FF = 2816
NORM_EPS = 1e-6
NEG_INF = -1e30
N_DEV = 8
N_CHIP = 4

ADAM_LR = 0.001
ADAM_B1 = 0.9
ADAM_B2 = 0.999
ADAM_EPS = 1e-08
ADAM_WD = 0.01
ADAM_STEP = 10

V7X_VMEM_LIMIT = 56 * 1024 * 1024
FF_CHUNK = 256

REGIONS = (
    ("w_in", IN_WIDTH, D_MODEL, True),
    ("w_oa", D_MODEL, ATT_WIDTH, True),
    ("w_ob", D_MODEL, SGU_WIDTH, True),
    ("w_out", D_MODEL, D_MODEL, False),
    ("w_up", 2 * D_FF, D_MODEL, True),
    ("w_down", D_FF, D_MODEL, False),
)
MESH = pl.DeviceIdType.MESH
ANY = pl.BlockSpec(memory_space=pl.ANY)


def _params(sem=None, **kw):
    return pltpu.CompilerParams(dimension_semantics=sem, vmem_limit_bytes=V7X_VMEM_LIMIT, **kw)


def _wspec(rows, cols, layer=None):
    del layer
    return pl.BlockSpec((rows, cols), lambda *_: (0, 0), pipeline_mode=pl.Buffered(1))


def _full(shape):
    nd = len(shape)
    return pl.BlockSpec(shape, lambda *_: (0,) * nd)


def _dot_nn(a, b):
    return jnp.dot(a, b, preferred_element_type=f32)


def _dot_nt(a, b):
    return lax.dot_general(a, b, (((1,), (1,)), ((), ())), preferred_element_type=f32)


def _dot_tn(a, b):
    return lax.dot_general(a, b, (((0,), (0,)), ((), ())), preferred_element_type=f32)


_GELU_C = float(np.sqrt(2.0 / np.pi))


def _gelu(x):
    return 0.5 * x * (1.0 + jnp.tanh(_GELU_C * (x + 0.044715 * x * x * x)))


def _gelu_grad(x):
    t = jnp.tanh(_GELU_C * (x + 0.044715 * x * x * x))
    du = _GELU_C * (1.0 + 3.0 * 0.044715 * x * x)
    return 0.5 * (1.0 + t) + 0.5 * x * (1.0 - t * t) * du


def _rms(x):
    return lax.rsqrt(jnp.mean(x * x, axis=-1, keepdims=True) + NORM_EPS)


def _mesh_pos():
    return lax.axis_index("x"), lax.axis_index("y"), lax.axis_index("c")


def _in_proj(x, gain, w_in_t, layer, tm):
    T = x.shape[0]

    def body(x_ref, g_ref, w_ref, qkv_ref, su_ref, sv_ref, ga_ref, gb_ref, h_ref):
        xf = x_ref[...]
        h = (xf * _rms(xf) * g_ref[...]).astype(bf16)
        h_ref[...] = h
        qkv_ref[...] = _dot_nt(h, w_ref[0:768, :])
        su_ref[...] = _dot_nt(h, w_ref[768:1280, :]).astype(bf16)
        sv_ref[...] = _dot_nt(h, w_ref[1280:1792, :]).astype(bf16)
        ga_ref[...] = _dot_nt(h, w_ref[1792:2816, :]).astype(bf16)
        gb_ref[...] = _dot_nt(h, w_ref[2816:3840, :]).astype(bf16)

    row = lambda w: pl.BlockSpec((tm, w), lambda i: (i, 0))
    return pl.pallas_call(
        body, name=f"in_proj_{layer}", grid=(T // tm,),
        in_specs=[row(D_MODEL), _full((1, D_MODEL)), _wspec(IN_WIDTH, D_MODEL, layer)],
        out_specs=[row(768), row(512), row(512), row(1024), row(1024), row(D_MODEL)],
        out_shape=[jax.ShapeDtypeStruct((T, 768), f32), jax.ShapeDtypeStruct((T, 512), bf16),
                   jax.ShapeDtypeStruct((T, 512), bf16), jax.ShapeDtypeStruct((T, 1024), bf16),
                   jax.ShapeDtypeStruct((T, 1024), bf16), jax.ShapeDtypeStruct((T, D_MODEL), bf16)],
        compiler_params=_params(("parallel",)),
    )(x, gain, w_in_t)


def _attn_head_group(cur, prev, qg, kg, sink_ref, n, hk):
    lo = hk * HEAD_DIM
    k_raw = jnp.concatenate([prev[:, lo:lo + HEAD_DIM], cur[:, 512 + lo:512 + lo + HEAD_DIM]], axis=0)
    v_band = jnp.concatenate([prev[:, 128 + lo:128 + lo + HEAD_DIM], cur[:, 640 + lo:640 + lo + HEAD_DIM]], axis=0)
    rk = _rms(k_raw)
    k_hat = k_raw * rk
    kn = (k_hat * kg).astype(bf16)
    q_raw = jnp.concatenate(
        [cur[:, (hk * Q_GROUP + g) * HEAD_DIM:(hk * Q_GROUP + g + 1) * HEAD_DIM] for g in range(Q_GROUP)], axis=0)
    rq = _rms(q_raw)
    q_hat = q_raw * rq
    qn = (q_hat * qg * (HEAD_DIM ** -0.5)).astype(bf16)
    s = _dot_nt(qn, kn)
    rows = lax.broadcasted_iota(jnp.int32, (Q_GROUP * BLOCK, 1), 0)
    g_of_row = rows // BLOCK
    qi = rows - g_of_row * BLOCK
    kj = lax.broadcasted_iota(jnp.int32, (1, 2 * BLOCK), 1)
    dist = qi + BLOCK - kj
    valid = (dist >= 0) & (dist < BLOCK) & ((kj >= BLOCK) | (n > 0))
    slope = jnp.zeros((Q_GROUP * BLOCK, 1), f32)
    sink = jnp.zeros((Q_GROUP * BLOCK, 1), f32)
    for g in range(Q_GROUP):
        head = hk * Q_GROUP + g
        slope = jnp.where(g_of_row == g, float(np.exp2(-8.0 * (head + 1.0) / 8.0)), slope)
        sink = jnp.where(g_of_row == g, sink_ref[head], sink)
    s = jnp.where(valid, s - slope * dist.astype(f32), NEG_INF)
    m = jnp.maximum(jnp.max(s, axis=-1, keepdims=True), sink)
    e = jnp.exp(s - m)
    e_sink = jnp.exp(sink - m)
    inv = 1.0 / (jnp.sum(e, axis=-1, keepdims=True) + e_sink)
    return dict(k_raw=k_raw, rk=rk, k_hat=k_hat, kn=kn, v=v_band.astype(bf16), q_hat=q_hat, rq=rq, qn=qn,
                p=e * inv, p_sink=e_sink * inv)


def _attn_fwd(qkv, qg, kg, sinks, n_seq, seq):
    T = n_seq * seq
    nb = seq // BLOCK

    def body(cur_ref, prev_ref, qg_ref, kg_ref, sink_ref, y_ref):
        n = pl.program_id(1)
        cur = cur_ref[...]
        prev = prev_ref[...]
        pieces = [None] * (N_KV_HEADS * Q_GROUP)
        for hk in range(N_KV_HEADS):
            a = _attn_head_group(cur, prev, qg_ref[...], kg_ref[...], sink_ref, n, hk)
            o = _dot_nn(a["p"].astype(bf16), a["v"])
            for g in range(Q_GROUP):
                pieces[hk * Q_GROUP + g] = o[g * BLOCK:(g + 1) * BLOCK]
        y_ref[...] = jnp.concatenate(pieces, axis=1).astype(bf16)

    return pl.pallas_call(
        body, name="attn_fwd", grid=(n_seq, nb),
        in_specs=[pl.BlockSpec((BLOCK, 768), lambda b, n: (b * nb + n, 0)),
                  pl.BlockSpec((BLOCK, 256), lambda b, n: (b * nb + jnp.maximum(n - 1, 0), 2)),
                  _full((1, HEAD_DIM)), _full((1, HEAD_DIM)),
                  pl.BlockSpec(memory_space=pltpu.SMEM)],
        out_specs=pl.BlockSpec((BLOCK, ATT_WIDTH), lambda b, n: (b * nb + n, 0)),
        out_shape=jax.ShapeDtypeStruct((T, ATT_WIDTH), bf16),
        compiler_params=_params(("parallel", "parallel")),
    )(qkv, qkv, qg, kg, sinks)


def _sgu_chunk(su, sv, gain, w_ref, b_ref):
    u = _gelu(su)
    vg = _gelu(sv)
    rv = _rms(vg)
    v_hat = vg * rv
    vn = (v_hat * gain).astype(bf16)
    causal = (lax.broadcasted_iota(jnp.int32, (BLOCK, BLOCK), 0) >= lax.broadcasted_iota(jnp.int32, (BLOCK, BLOCK), 1))
    w_tril = [jnp.where(causal, w_ref[g], 0.0).astype(bf16) for g in range(SGU_GROUPS)]
    gd = SGU_WIDTH // SGU_GROUPS
    mixed = jnp.concatenate(
        [_dot_nn(w_tril[g], vn[:, g * gd:(g + 1) * gd]) + b_ref[g] for g in range(SGU_GROUPS)], axis=1)
    return u, rv, v_hat, vn, w_tril, mixed


def _sgu_fwd(su, sv, gain, w_s, b_s, tm):
    T = su.shape[0]

    def body(su_ref, sv_ref, g_ref, w_ref, b_ref, y_ref):
        for ch in range(tm // BLOCK):
            rows = slice(ch * BLOCK, (ch + 1) * BLOCK)
            u, _, _, _, _, mixed = _sgu_chunk(su_ref[rows, :].astype(f32), sv_ref[rows, :].astype(f32),
                                              g_ref[...], w_ref, b_ref)
            y_ref[rows, :] = (u * mixed).astype(bf16)

    row = pl.BlockSpec((tm, SGU_WIDTH), lambda i: (i, 0))
    return pl.pallas_call(
        body, name="sgu_fwd", grid=(T // tm,),
        in_specs=[row, row, _full((1, SGU_WIDTH)), _full((SGU_GROUPS, BLOCK, BLOCK)), _full((SGU_GROUPS, BLOCK, 1))],
        out_specs=row, out_shape=jax.ShapeDtypeStruct((T, SGU_WIDTH), bf16),
        compiler_params=_params(("parallel",)),
    )(su, sv, gain, w_s, b_s)


def _merge_fwd(x, y_att, y_sgu, ga, gb, w_oa_t, w_ob_t, w_out, layer, tm):
    T = x.shape[0]

    def body(x_ref, ya_ref, ys_ref, ga_ref, gb_ref, woa_ref, wob_ref, wout_ref, x1_ref, m_ref, a_ref, b_ref):
        a = _dot_nt(ya_ref[...], woa_ref[...])
        b = _dot_nt(ys_ref[...], wob_ref[...])
        a_ref[...] = a.astype(bf16)
        b_ref[...] = b.astype(bf16)
        merged = (jax.nn.sigmoid(ga_ref[...].astype(f32)) * a + jax.nn.sigmoid(gb_ref[...].astype(f32)) * b).astype(bf16)
        m_ref[...] = merged
        x1_ref[...] = x_ref[...] + _dot_nn(merged, wout_ref[...])

    row = lambda w: pl.BlockSpec((tm, w), lambda i: (i, 0))
    return pl.pallas_call(
        body, name=f"merge_fwd_{layer}", grid=(T // tm,),
        in_specs=[row(D_MODEL), row(512), row(512), row(1024), row(1024),
                  _wspec(D_MODEL, ATT_WIDTH, layer), _wspec(D_MODEL, SGU_WIDTH, layer), _wspec(D_MODEL, D_MODEL, layer)],
        out_specs=[row(D_MODEL)] * 4,
        out_shape=[jax.ShapeDtypeStruct((T, D_MODEL), f32)] + [jax.ShapeDtypeStruct((T, D_MODEL), bf16)] * 3,
        compiler_params=_params(("parallel",)),
    )(x, y_att, y_sgu, ga, gb, w_oa_t, w_ob_t, w_out)


def _conv_taps(zz, h1, h2, rowid):
    z1 = jnp.where(rowid == 0, h1, pltpu.roll(zz, 1, 0))
    z2 = jnp.where(rowid == 0, h2, jnp.where(rowid == 1, h1, pltpu.roll(zz, 2, 0)))
    return z1, z2


def _ffn_up(x1, gain, w_up_t, conv_w, conv_b, layer, seq, tm):
    T = x1.shape[0]
    tps = seq // tm

    def body(x_ref, g_ref, w_ref, cw_ref, cb_ref, h2_ref, z_ref, act_ref, carry_ref):
        i = pl.program_id(0)

        @pl.when(i % tps == 0)
        def _():
            carry_ref[...] = jnp.zeros_like(carry_ref)

        xf = x_ref[...]
        h2 = (xf * _rms(xf) * g_ref[...]).astype(bf16)
        h2_ref[...] = h2
        rowid = lax.broadcasted_iota(jnp.int32, (tm, 1), 0)
        for cc in range(D_FF // FF_CHUNK):
            zc = []
            for part in range(2):
                lo = part * D_FF + cc * FF_CHUNK
                cols = slice(lo, lo + FF_CHUNK)
                zb = _dot_nt(h2, w_ref[cols, :]).astype(bf16)
                z_ref[:, cols] = zb
                zz = zb.astype(f32)
                halo = carry_ref[:, cols]
                z1, z2 = _conv_taps(zz, halo[7:8], halo[6:7], rowid)
                carry_ref[:, cols] = zz[tm - 8:tm]
                zc.append(cb_ref[:, cols] + cw_ref[0:1, cols] * z2 + cw_ref[1:2, cols] * z1 + cw_ref[2:3, cols] * zz)
            act_ref[:, cc * FF_CHUNK:(cc + 1) * FF_CHUNK] = (zc[0] * jax.nn.sigmoid(zc[0]) * zc[1]).astype(bf16)

    row = lambda w: pl.BlockSpec((tm, w), lambda i: (i, 0))
    return pl.pallas_call(
        body, name=f"ffn_up_{layer}", grid=(T // tm,),
        in_specs=[row(D_MODEL), _full((1, D_MODEL)), _wspec(2 * D_FF, D_MODEL, layer),
                  _full((3, 2 * D_FF)), _full((1, 2 * D_FF))],
        out_specs=[row(D_MODEL), row(2 * D_FF), row(D_FF)],
        out_shape=[jax.ShapeDtypeStruct((T, D_MODEL), bf16), jax.ShapeDtypeStruct((T, 2 * D_FF), bf16),
                   jax.ShapeDtypeStruct((T, D_FF), bf16)],
        scratch_shapes=[pltpu.VMEM((8, 2 * D_FF), f32)],
        compiler_params=_params(("arbitrary",)),
    )(x1, gain, w_up_t, conv_w, conv_b)


def _ffn_down(x1, act, w_down, layer, tm):
    T = x1.shape[0]

    def body(x_ref, a_ref, w_ref, o_ref):
        o_ref[...] = x_ref[...] + _dot_nn(a_ref[...], w_ref[...])

    row = lambda w: pl.BlockSpec((tm, w), lambda i: (i, 0))
    return pl.pallas_call(
        body, name=f"ffn_down_{layer}", grid=(T // tm,),
        in_specs=[row(D_MODEL), row(D_FF), _wspec(D_FF, D_MODEL, layer)],
        out_specs=row(D_MODEL), out_shape=jax.ShapeDtypeStruct((T, D_MODEL), f32),
        compiler_params=_params(("parallel",)),
    )(x1, act, w_down)


def _loss_head(y, target, tm):
    T = y.shape[0]

    def body(y_ref, t_ref, dy_ref, dyb_ref, loss_ref):
        @pl.when(pl.program_id(0) == 0)
        def _():
            loss_ref[...] = jnp.zeros_like(loss_ref)

        diff = y_ref[...] - t_ref[...]
        loss_ref[...] += 0.5 * jnp.sum(jnp.mean(diff * diff, axis=-1, keepdims=True), axis=0, keepdims=True)
        dy = diff * (1.0 / D_MODEL)
        dy_ref[...] = dy
        dyb_ref[...] = dy.astype(bf16)

    row = pl.BlockSpec((tm, D_MODEL), lambda i: (i, 0))
    return pl.pallas_call(
        body, name="loss_head", grid=(T // tm,),
        in_specs=[row, row], out_specs=[row, row, _full((8, 128))],
        out_shape=[jax.ShapeDtypeStruct((T, D_MODEL), f32), jax.ShapeDtypeStruct((T, D_MODEL), bf16),
                   jax.ShapeDtypeStruct((8, 128), f32)],
        compiler_params=_params(("arbitrary",)),
    )(y, target)


def _ffn_bwd(dx2b, z, conv_w, conv_b, w_down, layer, seq, tm):
    T = z.shape[0]
    nt = T // tm
    tps = seq // tm

    def body(dx_ref, z_ref, zh_ref, cw_ref, cb_ref, wd_ref, dz_ref, dconv_ref, carry_ref):
        i = pl.program_id(0)
        pos = (nt - 1 - i) % tps

        @pl.when(i == 0)
        def _():
            dconv_ref[...] = jnp.zeros_like(dconv_ref)

        @pl.when(pos == tps - 1)
        def _():
            carry_ref[...] = jnp.zeros_like(carry_ref)

        dxb = dx_ref[...]
        rowid = lax.broadcasted_iota(jnp.int32, (tm, 1), 0)
        halo_on = (pos > 0).astype(f32)
        for cc in range(D_FF // FF_CHUNK):
            zz, z1, z2, zc, colss = [], [], [], [], []
            for part in range(2):
                lo = part * D_FF + cc * FF_CHUNK
                cols = slice(lo, lo + FF_CHUNK)
                colss.append(cols)
                zp = z_ref[:, cols].astype(f32)
                halo = zh_ref[:, cols].astype(f32) * halo_on
                a1, a2 = _conv_taps(zp, halo[7:8], halo[6:7], rowid)
                zz.append(zp), z1.append(a1), z2.append(a2)
                zc.append(cb_ref[:, cols] + cw_ref[0:1, cols] * a2 + cw_ref[1:2, cols] * a1 + cw_ref[2:3, cols] * zp)
            d_act = _dot_nt(dxb, wd_ref[cc * FF_CHUNK:(cc + 1) * FF_CHUNK, :])
            sg = jax.nn.sigmoid(zc[0])
            silu = zc[0] * sg
            dzc = [d_act * zc[1] * sg * (1.0 + zc[0] * (1.0 - sg)), d_act * silu]
            for part in range(2):
                cols = colss[part]
                g = dzc[part]
                dconv_ref[0:1, cols] += jnp.sum(g * z2[part], axis=0, keepdims=True)
                dconv_ref[1:2, cols] += jnp.sum(g * z1[part], axis=0, keepdims=True)
                dconv_ref[2:3, cols] += jnp.sum(g * zz[part], axis=0, keepdims=True)
                dconv_ref[3:4, cols] += jnp.sum(g, axis=0, keepdims=True)
                nxt = carry_ref[:, cols]
                d1 = jnp.where(rowid == tm - 1, nxt[0:1], pltpu.roll(g, tm - 1, 0))
                d2 = jnp.where(rowid == tm - 1, nxt[1:2], jnp.where(rowid == tm - 2, nxt[0:1], pltpu.roll(g, tm - 2, 0)))
                carry_ref[:, cols] = g[0:8]
                dz_ref[:, cols] = (cw_ref[2:3, cols] * g + cw_ref[1:2, cols] * d1 + cw_ref[0:1, cols] * d2).astype(bf16)

    rev = lambda w: pl.BlockSpec((tm, w), lambda i: (nt - 1 - i, 0))
    return pl.pallas_call(
        body, name=f"ffn_bwd_{layer}", grid=(nt,),
        in_specs=[rev(D_MODEL), rev(2 * D_FF),
                  pl.BlockSpec((8, 2 * D_FF), lambda i: (jnp.maximum((nt - 1 - i) * (tm // 8) - 1, 0), 0)),
                  _full((3, 2 * D_FF)), _full((1, 2 * D_FF)), _wspec(D_FF, D_MODEL, layer)],
        out_specs=[rev(2 * D_FF), _full((8, 2 * D_FF))],
        out_shape=[jax.ShapeDtypeStruct((T, 2 * D_FF), bf16), jax.ShapeDtypeStruct((8, 2 * D_FF), f32)],
        scratch_shapes=[pltpu.VMEM((8, 2 * D_FF), f32)],
        compiler_params=_params(("arbitrary",)),
    )(dx2b, z, z, conv_w, conv_b, w_down)


def _norm_bwd(dys, w, layer, x, gain, dres, tm, name):
    T = dys[0].shape[0]
    widths = [d.shape[1] for d in dys]
    K = sum(widths)
    n = len(dys)

    def body(*refs):
        dy_refs = refs[:n]
        w_ref, x_ref, g_ref, dres_ref, dx_ref, dxb_ref, dg_ref = refs[n:]

        @pl.when(pl.program_id(0) == 0)
        def _():
            dg_ref[...] = jnp.zeros_like(dg_ref)

        dh, lo = None, 0
        for dy_ref, wd in zip(dy_refs, widths):
            part = _dot_nn(dy_ref[...], w_ref[lo:lo + wd, :])
            dh = part if dh is None else dh + part
            lo += wd
        xf = x_ref[...]
        r = _rms(xf)
        x_hat = xf * r
        dg_ref[...] += jnp.sum(dh * x_hat, axis=0, keepdims=True)
        dxh = dh * g_ref[...]
        dx = dres_ref[...] + r * (dxh - x_hat * jnp.mean(dxh * x_hat, axis=-1, keepdims=True))
        dx_ref[...] = dx
        dxb_ref[...] = dx.astype(bf16)

    row = lambda w_: pl.BlockSpec((tm, w_), lambda i: (i, 0))
    return pl.pallas_call(
        body, name=name, grid=(T // tm,),
        in_specs=[row(wd) for wd in widths] + [_wspec(K, D_MODEL, layer), row(D_MODEL), _full((1, D_MODEL)), row(D_MODEL)],
        out_specs=[row(D_MODEL), row(D_MODEL), _full((1, D_MODEL))],
        out_shape=[jax.ShapeDtypeStruct((T, D_MODEL), f32), jax.ShapeDtypeStruct((T, D_MODEL), bf16),
                   jax.ShapeDtypeStruct((1, D_MODEL), f32)],
        compiler_params=_params(("arbitrary",)),
    )(*dys, w, x, gain, dres)


def _merge_bwd(dx1b, ga, gb, a, b, w_oa_t, w_ob_t, w_out, layer, tm):
    T = dx1b.shape[0]

    def body(dx_ref, ga_ref, gb_ref, a_ref, b_ref, woa_ref, wob_ref, wout_ref,
             da_ref, db_ref, dga_ref, dgb_ref, dya_ref, dys_ref):
        dm = _dot_nt(dx_ref[...], wout_ref[...])
        sa = jax.nn.sigmoid(ga_ref[...].astype(f32))
        sb = jax.nn.sigmoid(gb_ref[...].astype(f32))
        da = (dm * sa).astype(bf16)
        db = (dm * sb).astype(bf16)
        da_ref[...] = da
        db_ref[...] = db
        dga_ref[...] = (dm * a_ref[...].astype(f32) * sa * (1.0 - sa)).astype(bf16)
        dgb_ref[...] = (dm * b_ref[...].astype(f32) * sb * (1.0 - sb)).astype(bf16)
        dya_ref[...] = _dot_nn(da, woa_ref[...]).astype(bf16)
        dys_ref[...] = _dot_nn(db, wob_ref[...]).astype(bf16)

    row = lambda w: pl.BlockSpec((tm, w), lambda i: (i, 0))
    return pl.pallas_call(
        body, name=f"merge_bwd_{layer}", grid=(T // tm,),
        in_specs=[row(D_MODEL)] * 5 + [_wspec(D_MODEL, ATT_WIDTH, layer), _wspec(D_MODEL, SGU_WIDTH, layer),
                                       _wspec(D_MODEL, D_MODEL, layer)],
        out_specs=[row(D_MODEL)] * 4 + [row(512)] * 2,
        out_shape=[jax.ShapeDtypeStruct((T, D_MODEL), bf16)] * 4 + [jax.ShapeDtypeStruct((T, 512), bf16)] * 2,
        compiler_params=_params(("parallel",)),
    )(dx1b, ga, gb, a, b, w_oa_t, w_ob_t, w_out)


def _sgu_bwd(dy, su, sv, gain, w_s, b_s, tm):
    T = su.shape[0]
    gd = SGU_WIDTH // SGU_GROUPS

    def body(dy_ref, su_ref, sv_ref, g_ref, w_ref, b_ref, dsu_ref, dsv_ref, dw_ref, db_ref, dg_ref):
        @pl.when(pl.program_id(0) == 0)
        def _():
            dw_ref[...] = jnp.zeros_like(dw_ref)
            db_ref[...] = jnp.zeros_like(db_ref)
            dg_ref[...] = jnp.zeros_like(dg_ref)

        gain_v = g_ref[...]
        for ch in range(tm // BLOCK):
            rows = slice(ch * BLOCK, (ch + 1) * BLOCK)
            su_c = su_ref[rows, :].astype(f32)
            sv_c = sv_ref[rows, :].astype(f32)
            u, rv, v_hat, vn, w_tril, mixed = _sgu_chunk(su_c, sv_c, gain_v, w_ref, b_ref)
            dyc = dy_ref[rows, :].astype(f32)
            dsu_ref[rows, :] = (dyc * mixed * _gelu_grad(su_c)).astype(bf16)
            dmix = dyc * u
            dmix_b = dmix.astype(bf16)
            dvn = []
            for g in range(SGU_GROUPS):
                gs = slice(g * gd, (g + 1) * gd)
                db_ref[g] += jnp.sum(dmix[:, gs], axis=1, keepdims=True)
                dw_ref[g] += _dot_nt(dmix_b[:, gs], vn[:, gs])
                dvn.append(_dot_tn(w_tril[g], dmix_b[:, gs]))
            dvn = jnp.concatenate(dvn, axis=1)
            dg_ref[...] += jnp.sum(dvn * v_hat, axis=0, keepdims=True)
            dxh = dvn * gain_v
            dvg = rv * (dxh - v_hat * jnp.mean(dxh * v_hat, axis=-1, keepdims=True))
            dsv_ref[rows, :] = (dvg * _gelu_grad(sv_c)).astype(bf16)

    row = pl.BlockSpec((tm, SGU_WIDTH), lambda i: (i, 0))
    return pl.pallas_call(
        body, name="sgu_bwd", grid=(T // tm,),
        in_specs=[row, row, row, _full((1, SGU_WIDTH)), _full((SGU_GROUPS, BLOCK, BLOCK)),
                  _full((SGU_GROUPS, BLOCK, 1))],
        out_specs=[row, row, _full((SGU_GROUPS, BLOCK, BLOCK)), _full((SGU_GROUPS, BLOCK, 1)), _full((1, SGU_WIDTH))],
        out_shape=[jax.ShapeDtypeStruct((T, SGU_WIDTH), bf16)] * 2 + [
            jax.ShapeDtypeStruct((SGU_GROUPS, BLOCK, BLOCK), f32), jax.ShapeDtypeStruct((SGU_GROUPS, BLOCK, 1), f32),
            jax.ShapeDtypeStruct((1, SGU_WIDTH), f32)],
        compiler_params=_params(("arbitrary",)),
    )(dy, su, sv, gain, w_s, b_s)


def _attn_bwd(dy, qkv, qg, kg, sinks, n_seq, seq):
    T = n_seq * seq
    nb = seq // BLOCK
    scale = HEAD_DIM ** -0.5

    def body(dy_ref, cur_ref, prev_ref, qg_ref, kg_ref, sink_ref, dqkv_ref, dqg_ref, dkg_ref, dsink_ref,
             carry_k, carry_v):
        b = pl.program_id(0)
        j = pl.program_id(1)
        n = nb - 1 - j

        @pl.when((b == 0) & (j == 0))
        def _():
            dqg_ref[...] = jnp.zeros_like(dqg_ref)
            dkg_ref[...] = jnp.zeros_like(dkg_ref)
            dsink_ref[...] = jnp.zeros_like(dsink_ref)

        @pl.when(j == 0)
        def _():
            carry_k[...] = jnp.zeros_like(carry_k)
            carry_v[...] = jnp.zeros_like(carry_v)

        cur = cur_ref[...]
        prev = prev_ref[...]
        dyf = dy_ref[...].astype(f32)
        qg_v = qg_ref[...]
        kg_v = kg_ref[...]
        dq_pieces = [None] * (N_KV_HEADS * Q_GROUP)
        dk_pieces, dv_pieces = [], []
        for hk in range(N_KV_HEADS):
            a = _attn_head_group(cur, prev, qg_v, kg_v, sink_ref, n, hk)
            do = jnp.concatenate(
                [dyf[:, (hk * Q_GROUP + g) * HEAD_DIM:(hk * Q_GROUP + g + 1) * HEAD_DIM] for g in range(Q_GROUP)],
                axis=0).astype(bf16)
            p = a["p"]
            dp = _dot_nt(do, a["v"])
            dv_band = _dot_tn(p.astype(bf16), do)
            dsum = jnp.sum(p * dp, axis=-1, keepdims=True)
            ds = (p * (dp - dsum)).astype(bf16)
            dsink_col = -a["p_sink"] * dsum
            for g in range(Q_GROUP):
                head = hk * Q_GROUP + g
                dsink_ref[head:head + 1, :] += jnp.sum(dsink_col[g * BLOCK:(g + 1) * BLOCK], axis=0, keepdims=True)
            dqn = _dot_nn(ds, a["kn"])
            dkn_band = _dot_tn(ds, a["qn"])
            dq_hat_g = dqn * scale
            dqg_ref[...] += jnp.sum(dq_hat_g * a["q_hat"], axis=0, keepdims=True)
            dxh = dq_hat_g * qg_v
            dq = a["rq"] * (dxh - a["q_hat"] * jnp.mean(dxh * a["q_hat"], axis=-1, keepdims=True))
            for g in range(Q_GROUP):
                dq_pieces[hk * Q_GROUP + g] = dq[g * BLOCK:(g + 1) * BLOCK]
            dkn = dkn_band[BLOCK:] + carry_k[hk]
            dv_pieces.append(dv_band[BLOCK:] + carry_v[hk])
            carry_k[hk] = dkn_band[:BLOCK]
            carry_v[hk] = dv_band[:BLOCK]
            k_hat = a["k_hat"][BLOCK:]
            dkg_ref[...] += jnp.sum(dkn * k_hat, axis=0, keepdims=True)
            dxk = dkn * kg_v
            dk_pieces.append(a["rk"][BLOCK:] * (dxk - k_hat * jnp.mean(dxk * k_hat, axis=-1, keepdims=True)))
        dqkv_ref[...] = jnp.concatenate(dq_pieces + dk_pieces + dv_pieces, axis=1).astype(bf16)

    blk = lambda w: pl.BlockSpec((BLOCK, w), lambda b, j: (b * nb + nb - 1 - j, 0))
    return pl.pallas_call(
        body, name="attn_bwd", grid=(n_seq, nb),
        in_specs=[blk(ATT_WIDTH), blk(768),
                  pl.BlockSpec((BLOCK, 256), lambda b, j: (b * nb + jnp.maximum(nb - 2 - j, 0), 2)),
                  _full((1, HEAD_DIM)), _full((1, HEAD_DIM)), pl.BlockSpec(memory_space=pltpu.SMEM)],
        out_specs=[blk(768), _full((1, HEAD_DIM)), _full((1, HEAD_DIM)), _full((8, 128))],
        out_shape=[jax.ShapeDtypeStruct((T, 768), bf16), jax.ShapeDtypeStruct((1, HEAD_DIM), f32),
                   jax.ShapeDtypeStruct((1, HEAD_DIM), f32), jax.ShapeDtypeStruct((8, 128), f32)],
        scratch_shapes=[pltpu.VMEM((N_KV_HEADS, BLOCK, HEAD_DIM), f32), pltpu.VMEM((N_KV_HEADS, BLOCK, HEAD_DIM), f32)],
        compiler_params=_params(("arbitrary", "arbitrary")),
    )(dy, qkv, qkv, qg, kg, sinks)


def _weight_grad(a, b, tm, tk, name):
    T, M = a.shape
    N = b.shape[1]
    nk = T // tk

    def body(a_ref, b_ref, o_ref, acc_ref):
        k = pl.program_id(1)

        @pl.when(k == 0)
        def _():
            acc_ref[...] = jnp.zeros_like(acc_ref)

        acc_ref[...] += _dot_tn(a_ref[...], b_ref[...])

        @pl.when(k == nk - 1)
        def _():
            o_ref[...] = acc_ref[...].astype(bf16)

    return pl.pallas_call(
        body, name=name, grid=(M // tm, nk),
        in_specs=[pl.BlockSpec((tk, tm), lambda i, k: (k, i)), pl.BlockSpec((tk, N), lambda i, k: (k, 0))],
        out_specs=pl.BlockSpec((None, tm, N), lambda i, k: (0, i, 0)),
        out_shape=jax.ShapeDtypeStruct((1, M, N), bf16),
        scratch_shapes=[pltpu.VMEM((tm, N), f32)],
        compiler_params=_params(("parallel", "arbitrary")),
    )(a, b)


def _weight_grad_rows(a_list, b, tk, name):
    T, N = b.shape
    widths = [a.shape[1] for a in a_list]
    M = sum(widths)
    nk = T // tk
    n = len(a_list)

    def body(*refs):
        a_refs = refs[:n]
        b_ref, o_ref, acc_ref = refs[n:]
        k = pl.program_id(0)

        @pl.when(k == 0)
        def _():
            acc_ref[...] = jnp.zeros_like(acc_ref)

        lo = 0
        for a_ref, wd in zip(a_refs, widths):
            acc_ref[lo:lo + wd, :] += _dot_tn(a_ref[...], b_ref[...])
            lo += wd

        @pl.when(k == nk - 1)
        def _():
            o_ref[...] = acc_ref[...].astype(bf16)

    return pl.pallas_call(
        body, name=name, grid=(nk,),
        in_specs=[pl.BlockSpec((tk, wd), lambda k: (k, 0)) for wd in widths] + [pl.BlockSpec((tk, N), lambda k: (k, 0))],
        out_specs=pl.BlockSpec((None, M, N), lambda k: (0, 0, 0)),
        out_shape=jax.ShapeDtypeStruct((1, M, N), bf16),
        scratch_shapes=[pltpu.VMEM((M, N), f32)],
        compiler_params=_params(("arbitrary",)),
    )(*a_list, b)


def _place(src, layer, src_slot, n_slots, dst_slot, dtype, name):
    _, _, rows, cols = src.shape
    slots = jnp.stack([src_slot, dst_slot]).astype(jnp.int32)

    def body(slots_ref, s_ref, o_ref):
        o_ref[...] = s_ref[...].astype(dtype)

    return pl.pallas_call(
        body, name=name,
        grid_spec=pltpu.PrefetchScalarGridSpec(
            num_scalar_prefetch=1, grid=(1,),
            in_specs=[pl.BlockSpec((None, None, rows, cols), lambda i, sl: (layer, sl[0], 0, 0))],
            out_specs=pl.BlockSpec((None, rows, cols), lambda i, sl: (sl[1], 0, 0))),
        out_shape=jax.ShapeDtypeStruct((n_slots, rows, cols), dtype),
        compiler_params=_params(("arbitrary",)),
    )(slots, src)


HBM = pl.BlockSpec(memory_space=pltpu.HBM)
SEM = pl.BlockSpec(memory_space=pltpu.SEMAPHORE)
DATAFLOW = pltpu.SideEffectType.DATAFLOW_SIDE_EFFECTING


def _other_chips(x, y):
    return [(1 - x, y), (x, 1 - y), (1 - x, 1 - y)]


def _split_start(groups, name):
    nb = [len(bufs) for bufs, _ in groups]
    flat = [b for bufs, _ in groups for b in bufs]
    ns = [len(plan(bufs, dry=True)) for bufs, plan in groups]
    ng = len(groups)

    def body(*refs):
        n_in = len(flat)
        sems = refs[n_in:n_in + 2 * ng]
        thru = refs[n_in + 2 * ng:2 * n_in + 2 * ng]
        token = refs[2 * n_in + 2 * ng]
        off = 0
        for g, (bufs, plan) in enumerate(groups):
            mine = thru[off:off + nb[g]]
            off += nb[g]
            for k, (src, dst, to) in enumerate(plan(mine)):
                pltpu.make_async_remote_copy(
                    src_ref=src, dst_ref=dst, send_sem=sems[2 * g].at[k], recv_sem=sems[2 * g + 1].at[k],
                    device_id=to, device_id_type=MESH).start()
        token[...] = jnp.zeros_like(token)

    out_shape = []
    for n in ns:
        out_shape += [pltpu.SemaphoreType.DMA((n,)), pltpu.SemaphoreType.DMA((n,))]
    out_shape += [pltpu.HBM(b.shape, b.dtype) for b in flat]
    out_shape.append(jax.ShapeDtypeStruct((8, 128), f32))
    res = pl.pallas_call(
        body, name=name, out_shape=tuple(out_shape),
        in_specs=[HBM] * len(flat),
        out_specs=tuple([SEM] * (2 * ng) + [HBM] * len(flat) + [pl.BlockSpec(memory_space=pltpu.VMEM)]),
        input_output_aliases={i: 2 * ng + i for i in range(len(flat))},
        compiler_params=pltpu.CompilerParams(has_side_effects=DATAFLOW),
    )(*[pltpu.with_memory_space_constraint(b, pltpu.HBM) for b in flat])
    out, off = [], 2 * ng
    for g in range(ng):
        out.append((res[2 * g], res[2 * g + 1], list(res[off:off + nb[g]])))
        off += nb[g]
    return out, res[-1]


def _split_wait(bufs, send, recv, plan, after, name):
    nb = len(bufs)

    def body(*refs):
        thru = refs[:nb]
        send_ref, recv_ref = refs[nb], refs[nb + 1]
        for k, (src, dst, to) in enumerate(plan(thru)):
            cp = pltpu.make_async_remote_copy(
                src_ref=src, dst_ref=dst, send_sem=send_ref.at[k], recv_sem=recv_ref.at[k],
                device_id=to, device_id_type=MESH)
            cp.wait_send()
            cp.wait_recv()

    res = pl.pallas_call(
        body, name=name, out_shape=tuple(pltpu.HBM(b.shape, b.dtype) for b in bufs),
        in_specs=[HBM] * nb + [SEM, SEM, ANY], out_specs=tuple([HBM] * nb),
        input_output_aliases={i: i for i in range(nb)},
        compiler_params=pltpu.CompilerParams(has_side_effects=DATAFLOW),
    )(*bufs, send, recv, after)
    return list(res)


def _gather_plan(hrs, n_direct=0):
    def plan(refs, dry=False):
        if dry:
            return [None] * (4 * len(hrs) + 3 * n_direct)
        x, y, c = _mesh_pos()
        me = 4 * x + 2 * y + c
        out = []
        for i in range(n_direct):
            src, land = refs[len(hrs) + 2 * i], refs[len(hrs) + 2 * i + 1]
            out += [(src, land.at[2 * x + y], (*chip, c)) for chip in _other_chips(x, y)]
        for ref, hr in zip(refs, hrs):
            rows = ref.at[pl.ds(pl.multiple_of(me * hr, 16), hr), :]
            out.append((rows, rows, (x, y, 1 - c)))
            out += [(rows, rows, (*chip, c)) for chip in _other_chips(x, y)]
        return out
    return plan


def _all_to_all_plan(n):
    def plan(refs, dry=False):
        if dry:
            return [None] * (7 * n)
        x, y, c = _mesh_pos()
        out = []
        for ref in refs:
            mine = ref.at[4 * x + 2 * y + c]
            for fx in range(2):
                for fy in range(2):
                    for fc in range(2):
                        if fx or fy or fc:
                            out.append((mine, mine, (1 - x if fx else x, 1 - y if fy else y, 1 - c if fc else c)))
        return out
    return plan


def _pass_to_sibling(bufs, hrs, name):
    nb = len(bufs)

    def body(*refs):
        out = refs[nb:2 * nb]
        send, recv = refs[2 * nb:]
        x, y, c = _mesh_pos()
        chips = _other_chips(x, y)
        started = []
        for i in range(nb):
            for j, chip in enumerate(chips):
                rows = out[i].at[pl.ds(pl.multiple_of((4 * chip[0] + 2 * chip[1] + c) * hrs[i], 16), hrs[i]), :]
                cp = pltpu.make_async_remote_copy(
                    src_ref=rows, dst_ref=rows, send_sem=send.at[3 * i + j], recv_sem=recv.at[3 * i + j],
                    device_id=(x, y, 1 - c), device_id_type=MESH)
                cp.start()
                started.append(cp)
        for i in range(nb):
            for j, chip in enumerate(chips):
                rows = out[i].at[pl.ds(pl.multiple_of((4 * chip[0] + 2 * chip[1] + 1 - c) * hrs[i], 16), hrs[i]), :]
                pltpu.make_async_remote_copy(
                    src_ref=rows, dst_ref=rows, send_sem=send.at[3 * i + j], recv_sem=recv.at[3 * i + j],
                    device_id=(x, y, 1 - c), device_id_type=MESH).wait_recv()
        for cp in started:
            cp.wait_send()

    return list(pl.pallas_call(
        body, name=name, in_specs=[ANY] * nb, out_specs=[ANY] * nb,
        out_shape=[jax.ShapeDtypeStruct(b.shape, b.dtype) for b in bufs],
        input_output_aliases={i: i for i in range(nb)},
        scratch_shapes=[pltpu.SemaphoreType.DMA((3 * nb,)), pltpu.SemaphoreType.DMA((3 * nb,))],
        compiler_params=pltpu.CompilerParams(has_side_effects=True),
    )(*bufs))


def _pair_exchange(grads, name):
    nr = len(grads)
    n_l = grads[0].shape[0]
    n_sem = nr * n_l * N_CHIP

    def body(*refs):
        src = refs[:nr]
        out = refs[nr:2 * nr]
        send, recv = refs[2 * nr:]
        x, y, c = _mesh_pos()
        copies = []
        for r in range(nr):
            hr = grads[r].shape[1] // N_DEV
            for layer in range(n_l):
                for j in range(N_CHIP):
                    idx = (r * n_l + layer) * N_CHIP + j
                    start = pl.multiple_of((2 * j + 1 - c) * hr, 16)
                    cp = pltpu.make_async_remote_copy(
                        src_ref=src[r].at[layer, pl.ds(start, hr), :], dst_ref=out[r].at[layer, j],
                        send_sem=send.at[idx], recv_sem=recv.at[idx], device_id=(x, y, 1 - c), device_id_type=MESH)
                    cp.start()
                    copies.append(cp)
        for cp in copies:
            cp.wait()

    return pl.pallas_call(
        body, name=name,
        in_specs=[ANY] * nr, out_specs=[ANY] * nr,
        out_shape=[jax.ShapeDtypeStruct((n_l, N_CHIP, g.shape[1] // N_DEV, g.shape[2]), bf16) for g in grads],
        scratch_shapes=[pltpu.SemaphoreType.DMA((n_sem,)), pltpu.SemaphoreType.DMA((n_sem,))],
        compiler_params=pltpu.CompilerParams(has_side_effects=True),
    )(*grads)


def _pair_sum(grad, other, core, chip, name):
    n_l, rows, cols = grad.shape
    hr = rows // N_DEV
    g5 = grad.reshape(n_l, N_CHIP, 2, hr, cols)
    where = jnp.stack([core, chip]).astype(jnp.int32)

    def body(where_ref, g_ref, o_ref, s_ref, mine_ref):
        s = (g_ref[...].astype(f32) + o_ref[...].astype(f32)).astype(bf16)
        s_ref[...] = s

        @pl.when(pl.program_id(1) == where_ref[1])
        def _():
            mine_ref[...] = s

    return pl.pallas_call(
        body, name=name,
        grid_spec=pltpu.PrefetchScalarGridSpec(
            num_scalar_prefetch=1, grid=(n_l, N_CHIP),
            in_specs=[pl.BlockSpec((None, None, None, hr, cols), lambda l, j, w: (l, j, w[0], 0, 0)),
                      pl.BlockSpec((None, None, hr, cols), lambda l, j, w: (l, j, 0, 0))],
            out_specs=[pl.BlockSpec((None, None, hr, cols), lambda l, j, w: (l, j, 0, 0)),
                       pl.BlockSpec((None, None, hr, cols), lambda l, j, w: (l, w[1], 0, 0))]),
        out_shape=[jax.ShapeDtypeStruct((n_l, N_CHIP, hr, cols), bf16)] * 2,
        compiler_params=_params(("arbitrary", "arbitrary")),
    )(where, g5, other)


def _chip_plan(nr, n_l):
    def plan(refs, dry=False):
        if dry:
            return [None] * (nr * n_l * 3)
        x, y, c = _mesh_pos()
        out = []
        for r in range(nr):
            for layer in range(n_l):
                for chip in _other_chips(x, y):
                    out.append((refs[r].at[layer, 2 * chip[0] + chip[1]], refs[nr + r].at[layer, 2 * x + y], (*chip, c)))
        return out
    return plan


def _chip_sum(parts, core, name):
    n_l, _, hr, cols = parts.shape

    def body(core_ref, p_ref, o_ref):
        acc = p_ref[0].astype(f32) + p_ref[1].astype(f32)
        acc = acc + p_ref[2].astype(f32)
        o_ref[...] = acc + p_ref[3].astype(f32)

    return pl.pallas_call(
        body, name=name,
        grid_spec=pltpu.PrefetchScalarGridSpec(
            num_scalar_prefetch=1, grid=(n_l,),
            in_specs=[pl.BlockSpec((None, N_CHIP, hr, cols), lambda l, cr: (l, 0, 0, 0))],
            out_specs=pl.BlockSpec((None, None, hr, cols), lambda l, cr: (l, cr[0], 0, 0))),
        out_shape=jax.ShapeDtypeStruct((n_l, 2, hr, cols), f32),
        compiler_params=_params(("arbitrary",)),
    )(core, parts)


def _share_halves(halves):
    nr = len(halves)

    def body(*refs):
        out = refs[nr:2 * nr]
        send, recv = refs[2 * nr:]
        x, y, c = _mesh_pos()
        copies = []
        for r in range(nr):
            cp = pltpu.make_async_remote_copy(
                src_ref=out[r].at[0, c], dst_ref=out[r].at[0, c], send_sem=send.at[r],
                recv_sem=recv.at[r], device_id=(x, y, 1 - c), device_id_type=MESH)
            cp.start()
            copies.append(cp)
        for r in range(nr):
            copies[r].wait_send()
            pltpu.make_async_remote_copy(
                src_ref=out[r].at[0, 1 - c], dst_ref=out[r].at[0, 1 - c], send_sem=send.at[r],
                recv_sem=recv.at[r], device_id=(x, y, 1 - c), device_id_type=MESH).wait_recv()

    return pl.pallas_call(
        body, name="grad_share_halves",
        in_specs=[ANY] * nr, out_specs=[ANY] * nr,
        out_shape=[jax.ShapeDtypeStruct(h.shape, h.dtype) for h in halves],
        input_output_aliases={r: r for r in range(nr)},
        scratch_shapes=[pltpu.SemaphoreType.DMA((nr,))] * 2,
        compiler_params=pltpu.CompilerParams(has_side_effects=True),
    )(*halves)


def _sum_small(parts, name):
    n, rows, cols = parts.shape

    def body(p_ref, o_ref):
        acc = p_ref[0].astype(f32)
        for d in range(1, n):
            acc = acc + p_ref[d].astype(f32)
        o_ref[...] = acc

    return pl.pallas_call(
        body, name=name, grid=(rows // 16,),
        in_specs=[pl.BlockSpec((n, 16, cols), lambda i: (0, i, 0))], out_specs=pl.BlockSpec((16, cols), lambda i: (i, 0)),
        out_shape=jax.ShapeDtypeStruct((rows, cols), f32),
        compiler_params=_params(("parallel",)),
    )(parts)


def _adamw(w, g, m, v, name):
    n_l, rows, cols = w.shape
    tr = rows
    for cand in (512, 256, 128, 64, 32, 16, 8):
        if rows % cand == 0 and rows > cand:
            tr = cand
            break

    def body(w_ref, g_ref, m_ref, v_ref, d_ref, nm_ref, nv_ref):
        gg = g_ref[...]
        nm = ADAM_B1 * m_ref[...] + (1.0 - ADAM_B1) * gg
        nv = ADAM_B2 * v_ref[...] + (1.0 - ADAM_B2) * (gg * gg)
        m_hat = nm / (1.0 - ADAM_B1 ** ADAM_STEP)
        v_hat = nv / (1.0 - ADAM_B2 ** ADAM_STEP)
        d_ref[...] = -ADAM_LR * (m_hat / (jnp.sqrt(v_hat) + ADAM_EPS) + ADAM_WD * w_ref[...])
        nm_ref[...] = nm
        nv_ref[...] = nv

    blk = pl.BlockSpec((None, tr, cols), lambda l, i: (l, i, 0))
    return pl.pallas_call(
        body, name=name, grid=(n_l, rows // tr),
        in_specs=[blk] * 4, out_specs=[blk] * 3, out_shape=[jax.ShapeDtypeStruct((n_l, rows, cols), f32)] * 3,
        compiler_params=_params(("parallel", "parallel")),
    )(w, g, m, v)


SMALL = ("mix_norm", "q_norm", "k_norm", "sinks", "sgu_norm", "w_s", "b_s", "ffn_norm", "conv_b", "conv_w")


def _pack_small(arrs):
    flat = jnp.concatenate([a.reshape(-1) for a in arrs])
    pad = (-flat.shape[0]) % (16 * 1024)
    return jnp.pad(flat, (0, pad)).reshape(-1, 1024)


def _unpack_small(pack, shapes):
    flat = pack.reshape(-1)
    out, off = [], 0
    for s in shapes:
        n = int(np.prod(s))
        out.append(flat[off:off + n].reshape(s))
        off += n
    return out


def kernel(x, mix_norm, w_in, q_norm, k_norm, sinks, sgu_norm, w_s, b_s, w_oa, w_ob, w_out, ffn_norm, w_up, conv_w, conv_b, w_down, loss_target, m_mix_norm, m_w_in, m_q_norm, m_k_norm, m_sinks, m_sgu_norm, m_w_s, m_b_s, m_w_oa, m_w_ob, m_w_out, m_ffn_norm, m_w_up, m_conv_w, m_conv_b, m_w_down, v_mix_norm, v_w_in, v_q_norm, v_k_norm, v_sinks, v_sgu_norm, v_w_s, v_b_s, v_w_oa, v_w_ob, v_w_out, v_ffn_norm, v_w_up, v_conv_w, v_conv_b, v_w_down):
    weights = dict(mix_norm=mix_norm, w_in=w_in, q_norm=q_norm, k_norm=k_norm, sinks=sinks, sgu_norm=sgu_norm,
                   w_s=w_s, b_s=b_s, w_oa=w_oa, w_ob=w_ob, w_out=w_out, ffn_norm=ffn_norm, w_up=w_up,
                   conv_w=conv_w, conv_b=conv_b, w_down=w_down)
    mom_m = dict(mix_norm=m_mix_norm, w_in=m_w_in, q_norm=m_q_norm, k_norm=m_k_norm, sinks=m_sinks,
                 sgu_norm=m_sgu_norm, w_s=m_w_s, b_s=m_b_s, w_oa=m_w_oa, w_ob=m_w_ob, w_out=m_w_out,
                 ffn_norm=m_ffn_norm, w_up=m_w_up, conv_w=m_conv_w, conv_b=m_conv_b, w_down=m_w_down)
    mom_v = dict(mix_norm=v_mix_norm, w_in=v_w_in, q_norm=v_q_norm, k_norm=v_k_norm, sinks=v_sinks,
                 sgu_norm=v_sgu_norm, w_s=v_w_s, b_s=v_b_s, w_oa=v_w_oa, w_ob=v_w_ob, w_out=v_w_out,
                 ffn_norm=v_ffn_norm, w_up=v_w_up, conv_w=v_conv_w, conv_b=v_conv_b, w_down=v_w_down)
    n_seq, seq, _ = x.shape
    T = n_seq * seq
    core = lax.axis_index("c")
    chip = 2 * lax.axis_index("x") + lax.axis_index("y")
    tm = min(512, seq)
    tm_ff = min(256, seq)

    me = 2 * chip + core
    names = [r[0] for r in REGIONS]
    hrs = {name: rows // N_DEV for name, rows, _, _ in REGIONS}
    placed = [{}, {}]
    for name, rows, cols, transposed in REGIONS:
        shard = (jnp.swapaxes(weights[name], 1, 2) if transposed else weights[name]).reshape(2, 2, hrs[name], cols)
        for l in range(2):
            placed[l][name] = _place(shard, l, core, N_DEV, me, bf16, f"place_{name}_{l}").reshape(rows, cols)
    group_keys = [[(0, "w_in")], [(0, n) for n in names[1:]], [(1, n) for n in names]]
    group_bufs = [[placed[l][n] for l, n in keys] for keys in group_keys]
    group_bufs[0] += [conv_w, jnp.zeros((N_CHIP,) + conv_w.shape, f32)]
    n_direct = [1, 0, 0]
    plans = [_gather_plan([hrs[n] for _, n in keys], nd) for keys, nd in zip(group_keys, n_direct)]
    started, _ = _split_start(list(zip(group_bufs, plans)), "gather_start")
    gathered = [{}, {}]

    def finish_gather(g, after):
        send, recv, bufs = started[g]
        keys = group_keys[g]
        hr_list = [hrs[n] for _, n in keys]
        bufs = _split_wait(bufs, send, recv, _gather_plan(hr_list, n_direct[g]), after, f"gather_wait_{g}")
        passed = _pass_to_sibling(bufs[:len(keys)], hr_list, f"gather_pass_{g}")
        for (l, n), b in zip(keys, passed):
            gathered[l][n] = b
        return bufs[len(keys):]

    xs = x.reshape(T, D_MODEL)
    _, conv_w_land = finish_gather(0, xs)
    conv_w_all = lax.dynamic_update_slice(conv_w_land, conv_w[None], (chip, 0, 0, 0))
    conv_w_full = jnp.concatenate([conv_w_all[j] for j in range(N_CHIP)], axis=-1)
    saved = []
    cur = xs
    for l in range(2):
        wl = gathered[l]
        b_col = b_s[l].reshape(SGU_GROUPS, BLOCK, 1)
        qkv, su, sv, ga, gb, h = _in_proj(cur, mix_norm[l][None], wl["w_in"], l, tm)
        y_att = _attn_fwd(qkv, q_norm[l][None], k_norm[l][None], sinks[l], n_seq, seq)
        y_sgu = _sgu_fwd(su, sv, sgu_norm[l][None], w_s[l], b_col, tm)
        if l == 0:
            finish_gather(1, y_sgu)
        x1, merged, a_o, b_o = _merge_fwd(cur, y_att, y_sgu, ga, gb, wl["w_oa"], wl["w_ob"], wl["w_out"], l, tm)
        h2, z, act = _ffn_up(x1, ffn_norm[l][None], wl["w_up"], conv_w_full[l], conv_b[l][None], l, seq, tm_ff)
        x2 = _ffn_down(x1, act, wl["w_down"], l, tm)
        if l == 0:
            finish_gather(2, x2)
        saved.append(dict(x=cur, qkv=qkv, su=su, sv=sv, ga=ga, gb=gb, h=h, y_att=y_att, y_sgu=y_sgu, x1=x1,
                          merged=merged, a=a_o, b=b_o, h2=h2, z=z, act=act, b_col=b_col))
        cur = x2

    dy, dyb, loss_part = _loss_head(cur, loss_target.reshape(T, D_MODEL), tm)
    loss = lax.psum(loss_part[0, 0], ("x", "y", "c"))

    core_arr = core.astype(jnp.int32).reshape(1)
    big = [{}, {}]
    small = {name: [None, None] for name in SMALL}

    def start_reduce(l, keys, tag):
        gl = [big[l][n] for n in keys]
        from_sibling = _pair_exchange(gl, f"pair_exchange_{tag}")
        pairs = [_pair_sum(g, o, core, chip, f"pair_sum_{n}_{l}") for g, o, n in zip(gl, from_sibling, keys)]
        bufs = [p[0] for p in pairs] + [p[1] for p in pairs]
        (res,), token = _split_start([(bufs, _chip_plan(len(keys), 1))], f"chip_start_{tag}")
        return (l, keys, res, tag), token[0:1, 0:1]

    def finish_reduce(state, after):
        l, keys, (send, recv, bufs), tag = state
        bufs = _split_wait(bufs, send, recv, _chip_plan(len(keys), 1), after, f"chip_wait_{tag}")
        return {(l, n): _chip_sum(p, core_arr, f"chip_sum_{n}_{l}") for n, p in zip(keys, bufs[len(keys):])}

    rest = [n for n in SMALL if n != "w_s"]
    rest_shapes = [weights[n].shape[1:] if n != "conv_w" else (3, 2 * D_FF) for n in rest]
    zero = jnp.zeros((), jnp.int32)

    def start_small(l):
        packs = [(_pack_small([small[n][l] for n in rest]), f32, "small"), (small["w_s"][l].reshape(-1, 1024), bf16, "w_s")]
        bufs = [_place(p[None, None], 0, zero, N_DEV, me, dt, f"place_{tag}_{l}") for p, dt, tag in packs]
        (res,), token = _split_start([(bufs, _all_to_all_plan(2))], f"small_start_{l}")
        return res, token[0:1, 0:1]

    def finish_small(l, res, after):
        send, recv, bufs = res
        bufs = _split_wait(bufs, send, recv, _all_to_all_plan(2), after, f"small_wait_{l}")
        out = dict(zip(rest, _unpack_small(_sum_small(bufs[0], f"sum_small_{l}"), rest_shapes)))
        out["w_s"] = _sum_small(bufs[1], f"sum_w_s_{l}").reshape(w_s.shape[1:])
        return out

    pending = []
    after_start = jnp.zeros((1, 1), f32)
    for l in (1, 0):
        s = saved[l]
        wl = gathered[l]
        dz, dconv = _ffn_bwd(dyb, s["z"], conv_w_full[l], conv_b[l][None] + after_start, wl["w_down"], l, seq, tm_ff)
        big[l]["w_down"] = _weight_grad(s["act"], dyb, 1408, tm, f"dw_down_{l}")
        big[l]["w_up"] = _weight_grad(dz, s["h2"], 1408, tm, f"dw_up_{l}")
        ffn_gain = ffn_norm[l][None]
        if l == 0:
            state, tok = start_reduce(0, ["w_down", "w_up"], "0a")
            pending.append(state)
            ffn_gain = ffn_gain + tok
        dx1, dx1b, d_ffn = _norm_bwd([dz], wl["w_up"], l, s["x1"], ffn_gain, dy, tm, f"ffn_norm_bwd_{l}")
        small["conv_w"][l] = dconv[0:3]
        small["conv_b"][l] = dconv[3]
        small["ffn_norm"][l] = d_ffn[0]
        da, db, dga, dgb, dya, dys = _merge_bwd(dx1b, s["ga"], s["gb"], s["a"], s["b"],
                                                wl["w_oa"], wl["w_ob"], wl["w_out"], l, tm)
        big[l]["w_out"] = _weight_grad(s["merged"], dx1b, 1024, tm, f"dw_out_{l}")
        big[l]["w_oa"] = _weight_grad(da, s["y_att"], 1024, tm, f"dw_oa_{l}")
        big[l]["w_ob"] = _weight_grad(db, s["y_sgu"], 1024, tm, f"dw_ob_{l}")
        sgu_gain = sgu_norm[l][None]
        if l == 0:
            state, tok = start_reduce(0, ["w_out", "w_oa", "w_ob"], "0m")
            pending.append(state)
            sgu_gain = sgu_gain + tok
        dsu, dsv, d_ws, d_bs, d_sgu = _sgu_bwd(dys, s["su"], s["sv"], sgu_gain, w_s[l], s["b_col"], tm)
        causal = np.tril(np.ones((BLOCK, BLOCK), bool))
        small["w_s"][l] = jnp.where(causal[None], d_ws, 0.0)
        small["b_s"][l] = d_bs[:, :, 0]
        small["sgu_norm"][l] = d_sgu[0]
        dqkv, d_qg, d_kg, d_sink = _attn_bwd(dya, s["qkv"], q_norm[l][None], k_norm[l][None], sinks[l], n_seq, seq)
        small["q_norm"][l] = d_qg[0]
        small["k_norm"][l] = d_kg[0]
        small["sinks"][l] = d_sink[:, 0]
        dproj = [dqkv, dsu, dsv, dga, dgb]
        big[l]["w_in"] = _weight_grad_rows(dproj, s["h"], tm, f"dw_in_{l}")
        state, tok = start_reduce(l, names if l == 1 else ["w_in"], "1" if l == 1 else "0b")
        pending.append(state)
        dy, dyb, d_mix = _norm_bwd(dproj, wl["w_in"], l, s["x"], mix_norm[l][None] + tok, dx1, tm, f"mix_norm_bwd_{l}")
        small["mix_norm"][l] = d_mix[0]
        if l == 1:
            small_1, after_start = start_small(1)
    grad_x = dy.reshape(n_seq, seq, D_MODEL)

    small_0, _ = start_small(0)
    halves = {}
    for state in pending:
        halves.update(finish_reduce(state, dyb))
    half_keys = [(l, n) for l in range(2) for n in names]
    shared = dict(zip(half_keys, _share_halves([halves[k] for k in half_keys])))
    grad_big = {}
    for name, rows, cols, transposed in REGIONS:
        per_layer = [shared[(l, name)].reshape(rows // N_CHIP, cols) for l in range(2)]
        grad_big[name] = jnp.stack([g.T if transposed else g for g in per_layer])

    grad, delta, new_m, new_v = {}, {}, {}, {}
    for name, *_ in REGIONS:
        grad[name] = grad_big[name]
        delta[name], new_m[name], new_v[name] = _adamw(weights[name], grad[name], mom_m[name], mom_v[name], f"adamw_{name}")

    per_layer = [finish_small(0, small_0, delta["w_down"]), finish_small(1, small_1, dyb)]
    grad_small = {n: jnp.stack([per_layer[0][n], per_layer[1][n]]) for n in SMALL}
    cw_cols = conv_w.shape[-1]
    grad_small["conv_w"] = lax.dynamic_slice_in_dim(grad_small["conv_w"], chip * cw_cols, cw_cols, axis=2)

    as_rows = lambda a: a.reshape(2, -1, BLOCK)
    d, nm, nv = _adamw(as_rows(w_s), as_rows(grad_small["w_s"]), as_rows(m_w_s), as_rows(v_w_s), "adamw_w_s")
    grad["w_s"], delta["w_s"], new_m["w_s"], new_v["w_s"] = (
        grad_small["w_s"], d.reshape(w_s.shape), nm.reshape(w_s.shape), nv.reshape(w_s.shape))
    shapes = [weights[n].shape for n in rest]
    d, nm, nv = _adamw(_pack_small([weights[n] for n in rest])[None], _pack_small([grad_small[n] for n in rest])[None],
                       _pack_small([mom_m[n] for n in rest])[None], _pack_small([mom_v[n] for n in rest])[None],
                       "adamw_small")
    for n, dd, mm, vv in zip(rest, _unpack_small(d, shapes), _unpack_small(nm, shapes), _unpack_small(nv, shapes)):
        grad[n], delta[n], new_m[n], new_v[n] = grad_small[n], dd, mm, vv

    order = ["mix_norm", "w_in", "q_norm", "k_norm", "sinks", "sgu_norm", "w_s", "b_s", "w_oa", "w_ob", "w_out",
             "ffn_norm", "w_up", "conv_w", "conv_b", "w_down"]
    return (loss, grad_x, *[grad[n] for n in order], *[delta[n] for n in order],
            *[new_m[n] for n in order], *[new_v[n] for n in order])
```

```python
import functools

import numpy as np
import jax
import jax.numpy as jnp
from jax import lax
from jax.experimental import pallas as pl
from jax.experimental.pallas import tpu as pltpu

bf16 = jnp.bfloat16
f32 = jnp.float32

D_MODEL = 1024
ATT_WIDTH = 512
KV_WIDTH = 128
SGU_WIDTH = 512
HEAD_DIM = 64
N_KV_HEADS = 2
Q_GROUP = 4
BLOCK = 128
SGU_GROUPS = 8
IN_WIDTH = 3840
D_FF = 2816
NORM_EPS = 1e-6
NEG_INF = -1e30
N_DEV = 8
N_CHIP = 4

ADAM_LR = 0.001
ADAM_B1 = 0.9
ADAM_B2 = 0.999
ADAM_EPS = 1e-08
ADAM_WD = 0.01
ADAM_STEP = 10

V7X_VMEM_LIMIT = 56 * 1024 * 1024
FF_CHUNK = 256

REGIONS = (
    ("w_in", IN_WIDTH, D_MODEL, True),
    ("w_oa", D_MODEL, ATT_WIDTH, True),
    ("w_ob", D_MODEL, SGU_WIDTH, True),
    ("w_out", D_MODEL, D_MODEL, False),
    ("w_up", 2 * D_FF, D_MODEL, True),
    ("w_down", D_FF, D_MODEL, False),
)
MESH = pl.DeviceIdType.MESH
ANY = pl.BlockSpec(memory_space=pl.ANY)


def _params(sem=None, **kw):
    return pltpu.CompilerParams(dimension_semantics=sem, vmem_limit_bytes=V7X_VMEM_LIMIT, **kw)


def _wspec(rows, cols, layer=None):
    del layer
    return pl.BlockSpec((rows, cols), lambda *_: (0, 0), pipeline_mode=pl.Buffered(1))


def _full(shape):
    nd = len(shape)
    return pl.BlockSpec(shape, lambda *_: (0,) * nd)


def _dot_nn(a, b):
    return jnp.dot(a, b, preferred_element_type=f32)


def _dot_nt(a, b):
    return lax.dot_general(a, b, (((1,), (1,)), ((), ())), preferred_element_type=f32)


def _dot_tn(a, b):
    return lax.dot_general(a, b, (((0,), (0,)), ((), ())), preferred_element_type=f32)


_GELU_C = float(np.sqrt(2.0 / np.pi))


def _gelu(x):
    return 0.5 * x * (1.0 + jnp.tanh(_GELU_C * (x + 0.044715 * x * x * x)))


def _gelu_grad(x):
    t = jnp.tanh(_GELU_C * (x + 0.044715 * x * x * x))
    du = _GELU_C * (1.0 + 3.0 * 0.044715 * x * x)
    return 0.5 * (1.0 + t) + 0.5 * x * (1.0 - t * t) * du


def _rms(x):
    return lax.rsqrt(jnp.mean(x * x, axis=-1, keepdims=True) + NORM_EPS)


def _mesh_pos():
    return lax.axis_index("x"), lax.axis_index("y"), lax.axis_index("c")


def _in_proj(x, gain, w_in_t, layer, tm):
    T = x.shape[0]

    def body(x_ref, g_ref, w_ref, qkv_ref, su_ref, sv_ref, ga_ref, gb_ref, h_ref):
        xf = x_ref[...]
        h = (xf * _rms(xf) * g_ref[...]).astype(bf16)
        h_ref[...] = h
        qkv_ref[...] = _dot_nt(h, w_ref[0:768, :])
        su_ref[...] = _dot_nt(h, w_ref[768:1280, :]).astype(bf16)
        sv_ref[...] = _dot_nt(h, w_ref[1280:1792, :]).astype(bf16)
        ga_ref[...] = _dot_nt(h, w_ref[1792:2816, :]).astype(bf16)
        gb_ref[...] = _dot_nt(h, w_ref[2816:3840, :]).astype(bf16)

    row = lambda w: pl.BlockSpec((tm, w), lambda i: (i, 0))
    return pl.pallas_call(
        body, name=f"in_proj_{layer}", grid=(T // tm,),
        in_specs=[row(D_MODEL), _full((1, D_MODEL)), _wspec(IN_WIDTH, D_MODEL, layer)],
        out_specs=[row(768), row(512), row(512), row(1024), row(1024), row(D_MODEL)],
        out_shape=[jax.ShapeDtypeStruct((T, 768), f32), jax.ShapeDtypeStruct((T, 512), bf16),
                   jax.ShapeDtypeStruct((T, 512), bf16), jax.ShapeDtypeStruct((T, 1024), bf16),
                   jax.ShapeDtypeStruct((T, 1024), bf16), jax.ShapeDtypeStruct((T, D_MODEL), bf16)],
        compiler_params=_params(("parallel",)),
    )(x, gain, w_in_t)


def _attn_head_group(cur, prev, qg, kg, sink_ref, n, hk):
    lo = hk * HEAD_DIM
    k_raw = jnp.concatenate([prev[:, lo:lo + HEAD_DIM], cur[:, 512 + lo:512 + lo + HEAD_DIM]], axis=0)
    v_band = jnp.concatenate([prev[:, 128 + lo:128 + lo + HEAD_DIM], cur[:, 640 + lo:640 + lo + HEAD_DIM]], axis=0)
    rk = _rms(k_raw)
    k_hat = k_raw * rk
    kn = (k_hat * kg).astype(bf16)
    q_raw = jnp.concatenate(
        [cur[:, (hk * Q_GROUP + g) * HEAD_DIM:(hk * Q_GROUP + g + 1) * HEAD_DIM] for g in range(Q_GROUP)], axis=0)
    rq = _rms(q_raw)
    q_hat = q_raw * rq
    qn = (q_hat * qg * (HEAD_DIM ** -0.5)).astype(bf16)
    s = _dot_nt(qn, kn)
    rows = lax.broadcasted_iota(jnp.int32, (Q_GROUP * BLOCK, 1), 0)
    g_of_row = rows // BLOCK
    qi = rows - g_of_row * BLOCK
    kj = lax.broadcasted_iota(jnp.int32, (1, 2 * BLOCK), 1)
    dist = qi + BLOCK - kj
    valid = (dist >= 0) & (dist < BLOCK) & ((kj >= BLOCK) | (n > 0))
    slope = jnp.zeros((Q_GROUP * BLOCK, 1), f32)
    sink = jnp.zeros((Q_GROUP * BLOCK, 1), f32)
    for g in range(Q_GROUP):
        head = hk * Q_GROUP + g
        slope = jnp.where(g_of_row == g, float(np.exp2(-8.0 * (head + 1.0) / 8.0)), slope)
        sink = jnp.where(g_of_row == g, sink_ref[head], sink)
    s = jnp.where(valid, s - slope * dist.astype(f32), NEG_INF)
    m = jnp.maximum(jnp.max(s, axis=-1, keepdims=True), sink)
    e = jnp.exp(s - m)
    e_sink = jnp.exp(sink - m)
    inv = 1.0 / (jnp.sum(e, axis=-1, keepdims=True) + e_sink)
    return dict(k_raw=k_raw, rk=rk, k_hat=k_hat, kn=kn, v=v_band.astype(bf16), q_hat=q_hat, rq=rq, qn=qn,
                p=e * inv, p_sink=e_sink * inv)


def _attn_fwd(qkv, qg, kg, sinks, n_seq, seq):
    T = n_seq * seq
    nb = seq // BLOCK

    def body(cur_ref, prev_ref, qg_ref, kg_ref, sink_ref, y_ref):
        n = pl.program_id(1)
        cur = cur_ref[...]
        prev = prev_ref[...]
        pieces = [None] * (N_KV_HEADS * Q_GROUP)
        for hk in range(N_KV_HEADS):
            a = _attn_head_group(cur, prev, qg_ref[...], kg_ref[...], sink_ref, n, hk)
            o = _dot_nn(a["p"].astype(bf16), a["v"])
            for g in range(Q_GROUP):
                pieces[hk * Q_GROUP + g] = o[g * BLOCK:(g + 1) * BLOCK]
        y_ref[...] = jnp.concatenate(pieces, axis=1).astype(bf16)

    return pl.pallas_call(
        body, name="attn_fwd", grid=(n_seq, nb),
        in_specs=[pl.BlockSpec((BLOCK, 768), lambda b, n: (b * nb + n, 0)),
                  pl.BlockSpec((BLOCK, 256), lambda b, n: (b * nb + jnp.maximum(n - 1, 0), 2)),
                  _full((1, HEAD_DIM)), _full((1, HEAD_DIM)),
                  pl.BlockSpec(memory_space=pltpu.SMEM)],
        out_specs=pl.BlockSpec((BLOCK, ATT_WIDTH), lambda b, n: (b * nb + n, 0)),
        out_shape=jax.ShapeDtypeStruct((T, ATT_WIDTH), bf16),
        compiler_params=_params(("parallel", "parallel")),
    )(qkv, qkv, qg, kg, sinks)


def _sgu_chunk(su, sv, gain, w_ref, b_ref):
    u = _gelu(su)
    vg = _gelu(sv)
    rv = _rms(vg)
    v_hat = vg * rv
    vn = (v_hat * gain).astype(bf16)
    causal = (lax.broadcasted_iota(jnp.int32, (BLOCK, BLOCK), 0) >= lax.broadcasted_iota(jnp.int32, (BLOCK, BLOCK), 1))
    w_tril = [jnp.where(causal, w_ref[g], 0.0).astype(bf16) for g in range(SGU_GROUPS)]
    gd = SGU_WIDTH // SGU_GROUPS
    mixed = jnp.concatenate(
        [_dot_nn(w_tril[g], vn[:, g * gd:(g + 1) * gd]) + b_ref[g] for g in range(SGU_GROUPS)], axis=1)
    return u, rv, v_hat, vn, w_tril, mixed


def _sgu_fwd(su, sv, gain, w_s, b_s, tm):
    T = su.shape[0]

    def body(su_ref, sv_ref, g_ref, w_ref, b_ref, y_ref):
        for ch in range(tm // BLOCK):
            rows = slice(ch * BLOCK, (ch + 1) * BLOCK)
            u, _, _, _, _, mixed = _sgu_chunk(su_ref[rows, :].astype(f32), sv_ref[rows, :].astype(f32),
                                              g_ref[...], w_ref, b_ref)
            y_ref[rows, :] = (u * mixed).astype(bf16)

    row = pl.BlockSpec((tm, SGU_WIDTH), lambda i: (i, 0))
    return pl.pallas_call(
        body, name="sgu_fwd", grid=(T // tm,),
        in_specs=[row, row, _full((1, SGU_WIDTH)), _full((SGU_GROUPS, BLOCK, BLOCK)), _full((SGU_GROUPS, BLOCK, 1))],
        out_specs=row, out_shape=jax.ShapeDtypeStruct((T, SGU_WIDTH), bf16),
        compiler_params=_params(("parallel",)),
    )(su, sv, gain, w_s, b_s)


def _merge_fwd(x, y_att, y_sgu, ga, gb, w_oa_t, w_ob_t, w_out, layer, tm):
    T = x.shape[0]

    def body(x_ref, ya_ref, ys_ref, ga_ref, gb_ref, woa_ref, wob_ref, wout_ref, x1_ref, m_ref, a_ref, b_ref):
        a = _dot_nt(ya_ref[...], woa_ref[...])
        b = _dot_nt(ys_ref[...], wob_ref[...])
        a_ref[...] = a.astype(bf16)
        b_ref[...] = b.astype(bf16)
        merged = (jax.nn.sigmoid(ga_ref[...].astype(f32)) * a + jax.nn.sigmoid(gb_ref[...].astype(f32)) * b).astype(bf16)
        m_ref[...] = merged
        x1_ref[...] = x_ref[...] + _dot_nn(merged, wout_ref[...])

    row = lambda w: pl.BlockSpec((tm, w), lambda i: (i, 0))
    return pl.pallas_call(
        body, name=f"merge_fwd_{layer}", grid=(T // tm,),
        in_specs=[row(D_MODEL), row(512), row(512), row(1024), row(1024),
                  _wspec(D_MODEL, ATT_WIDTH, layer), _wspec(D_MODEL, SGU_WIDTH, layer), _wspec(D_MODEL, D_MODEL, layer)],
        out_specs=[row(D_MODEL)] * 4,
        out_shape=[jax.ShapeDtypeStruct((T, D_MODEL), f32)] + [jax.ShapeDtypeStruct((T, D_MODEL), bf16)] * 3,
        compiler_params=_params(("parallel",)),
    )(x, y_att, y_sgu, ga, gb, w_oa_t, w_ob_t, w_out)


def _tile_permutation(tm):
    r = np.arange(tm)
    p = np.zeros((tm, tm), np.float32)
    p[r, (r % 8) * (tm // 8) + r // 8] = 1.0
    return jnp.asarray(p, bf16), jnp.asarray(p.T, bf16)


def _taps_before(zz, prev, tm):
    first = lax.broadcasted_iota(jnp.int32, (8, 1), 0) == 0
    f8a = jnp.where(first, prev[7:8], pltpu.roll(zz[tm - 16:tm - 8], 1, 0))
    f8b = jnp.where(first, prev[15:16], pltpu.roll(zz[tm - 8:tm], 1, 0))
    z1 = jnp.concatenate([f8b, zz[:tm - 8]], axis=0)
    z2 = jnp.concatenate([f8a, f8b, zz[:tm - 16]], axis=0)
    return z1, z2


def _taps_after(g, nxt, tm):
    last = lax.broadcasted_iota(jnp.int32, (8, 1), 0) == 7
    l8a = jnp.where(last, nxt[0:1], pltpu.roll(g[0:8], 7, 0))
    l8b = jnp.where(last, nxt[8:9], pltpu.roll(g[8:16], 7, 0))
    g1 = jnp.concatenate([g[8:], l8a], axis=0)
    g2 = jnp.concatenate([g[16:], l8a, l8b], axis=0)
    return g1, g2


def _ffn_up(x1, gain, w_up_t, conv_w, conv_b, layer, seq, tm):
    T = x1.shape[0]
    tps = seq // tm
    perm, perm_t = _tile_permutation(tm)

    def body(x_ref, g_ref, w_ref, cw_ref, cb_ref, p_ref, pt_ref, h2_ref, z_ref, act_ref, carry_ref):
        i = pl.program_id(0)

        @pl.when(i % tps == 0)
        def _():
            carry_ref[...] = jnp.zeros_like(carry_ref)

        xf = x_ref[...]
        h2 = (xf * _rms(xf) * g_ref[...]).astype(bf16)
        h2_ref[...] = h2
        h2p = _dot_nn(p_ref[...], h2).astype(bf16)
        for cc in range(D_FF // FF_CHUNK):
            zc = []
            for part in range(2):
                lo = part * D_FF + cc * FF_CHUNK
                cols = slice(lo, lo + FF_CHUNK)
                zb = _dot_nt(h2p, w_ref[cols, :]).astype(bf16)
                z_ref[:, cols] = zb
                zz = zb.astype(f32)
                z1, z2 = _taps_before(zz, carry_ref[:, cols], tm)
                carry_ref[:, cols] = zz[tm - 16:tm]
                zc.append(cb_ref[:, cols] + cw_ref[0:1, cols] * z2 + cw_ref[1:2, cols] * z1 + cw_ref[2:3, cols] * zz)
            act_p = (zc[0] * jax.nn.sigmoid(zc[0]) * zc[1]).astype(bf16)
            act_ref[:, cc * FF_CHUNK:(cc + 1) * FF_CHUNK] = _dot_nn(pt_ref[...], act_p).astype(bf16)

    row = lambda w: pl.BlockSpec((tm, w), lambda i: (i, 0))
    return pl.pallas_call(
        body, name=f"ffn_up_{layer}", grid=(T // tm,),
        in_specs=[row(D_MODEL), _full((1, D_MODEL)), _wspec(2 * D_FF, D_MODEL, layer),
                  _full((3, 2 * D_FF)), _full((1, 2 * D_FF)), _full((tm, tm)), _full((tm, tm))],
        out_specs=[row(D_MODEL), row(2 * D_FF), row(D_FF)],
        out_shape=[jax.ShapeDtypeStruct((T, D_MODEL), bf16), jax.ShapeDtypeStruct((T, 2 * D_FF), bf16),
                   jax.ShapeDtypeStruct((T, D_FF), bf16)],
        scratch_shapes=[pltpu.VMEM((16, 2 * D_FF), f32)],
        compiler_params=_params(("arbitrary",)),
    )(x1, gain, w_up_t, conv_w, conv_b, perm, perm_t)


def _ffn_down(x1, act, w_down, layer, tm):
    T = x1.shape[0]

    def body(x_ref, a_ref, w_ref, o_ref):
        o_ref[...] = x_ref[...] + _dot_nn(a_ref[...], w_ref[...])

    row = lambda w: pl.BlockSpec((tm, w), lambda i: (i, 0))
    return pl.pallas_call(
        body, name=f"ffn_down_{layer}", grid=(T // tm,),
        in_specs=[row(D_MODEL), row(D_FF), _wspec(D_FF, D_MODEL, layer)],
        out_specs=row(D_MODEL), out_shape=jax.ShapeDtypeStruct((T, D_MODEL), f32),
        compiler_params=_params(("parallel",)),
    )(x1, act, w_down)


def _loss_head(y, target, tm):
    T = y.shape[0]

    def body(y_ref, t_ref, dy_ref, dyb_ref, loss_ref):
        @pl.when(pl.program_id(0) == 0)
        def _():
            loss_ref[...] = jnp.zeros_like(loss_ref)

        diff = y_ref[...] - t_ref[...]
        loss_ref[...] += 0.5 * jnp.sum(jnp.mean(diff * diff, axis=-1, keepdims=True), axis=0, keepdims=True)
        dy = diff * (1.0 / D_MODEL)
        dy_ref[...] = dy
        dyb_ref[...] = dy.astype(bf16)

    row = pl.BlockSpec((tm, D_MODEL), lambda i: (i, 0))
    return pl.pallas_call(
        body, name="loss_head", grid=(T // tm,),
        in_specs=[row, row], out_specs=[row, row, _full((8, 128))],
        out_shape=[jax.ShapeDtypeStruct((T, D_MODEL), f32), jax.ShapeDtypeStruct((T, D_MODEL), bf16),
                   jax.ShapeDtypeStruct((8, 128), f32)],
        compiler_params=_params(("arbitrary",)),
    )(y, target)


def _ffn_bwd(dx2b, z, conv_w, conv_b, w_down, layer, seq, tm):
    T = z.shape[0]
    nt = T // tm
    tps = seq // tm

    perm, perm_t = _tile_permutation(tm)

    def body(dx_ref, z_ref, zh_ref, cw_ref, cb_ref, wd_ref, p_ref, pt_ref, dz_ref, dconv_ref, carry_ref):
        i = pl.program_id(0)
        pos = (nt - 1 - i) % tps

        @pl.when(i == 0)
        def _():
            dconv_ref[...] = jnp.zeros_like(dconv_ref)

        @pl.when(pos == tps - 1)
        def _():
            carry_ref[...] = jnp.zeros_like(carry_ref)

        dxp = _dot_nn(p_ref[...], dx_ref[...]).astype(bf16)
        halo_on = (pos > 0).astype(f32)
        for cc in range(D_FF // FF_CHUNK):
            zz, z1, z2, zc, colss = [], [], [], [], []
            for part in range(2):
                lo = part * D_FF + cc * FF_CHUNK
                cols = slice(lo, lo + FF_CHUNK)
                colss.append(cols)
                zp = z_ref[:, cols].astype(f32)
                a1, a2 = _taps_before(zp, zh_ref[:, cols].astype(f32) * halo_on, tm)
                zz.append(zp), z1.append(a1), z2.append(a2)
                zc.append(cb_ref[:, cols] + cw_ref[0:1, cols] * a2 + cw_ref[1:2, cols] * a1 + cw_ref[2:3, cols] * zp)
            d_act = _dot_nt(dxp, wd_ref[cc * FF_CHUNK:(cc + 1) * FF_CHUNK, :])
            sg = jax.nn.sigmoid(zc[0])
            silu = zc[0] * sg
            dzc = [d_act * zc[1] * sg * (1.0 + zc[0] * (1.0 - sg)), d_act * silu]
            for part in range(2):
                cols = colss[part]
                g = dzc[part]
                dconv_ref[0:1, cols] += jnp.sum(g * z2[part], axis=0, keepdims=True)
                dconv_ref[1:2, cols] += jnp.sum(g * z1[part], axis=0, keepdims=True)
                dconv_ref[2:3, cols] += jnp.sum(g * zz[part], axis=0, keepdims=True)
                dconv_ref[3:4, cols] += jnp.sum(g, axis=0, keepdims=True)
                d1, d2 = _taps_after(g, carry_ref[:, cols], tm)
                carry_ref[:, cols] = g[0:16]
                dz_p = (cw_ref[2:3, cols] * g + cw_ref[1:2, cols] * d1 + cw_ref[0:1, cols] * d2).astype(bf16)
                dz_ref[:, cols] = _dot_nn(pt_ref[...], dz_p).astype(bf16)

    rev = lambda w: pl.BlockSpec((tm, w), lambda i: (nt - 1 - i, 0))
    return pl.pallas_call(
        body, name=f"ffn_bwd_{layer}", grid=(nt,),
        in_specs=[rev(D_MODEL), rev(2 * D_FF),
                  pl.BlockSpec((16, 2 * D_FF), lambda i: (jnp.maximum((nt - 1 - i) * (tm // 16) - 1, 0), 0)),
                  _full((3, 2 * D_FF)), _full((1, 2 * D_FF)), _wspec(D_FF, D_MODEL, layer),
                  _full((tm, tm)), _full((tm, tm))],
        out_specs=[rev(2 * D_FF), _full((8, 2 * D_FF))],
        out_shape=[jax.ShapeDtypeStruct((T, 2 * D_FF), bf16), jax.ShapeDtypeStruct((8, 2 * D_FF), f32)],
        scratch_shapes=[pltpu.VMEM((16, 2 * D_FF), f32)],
        compiler_params=_params(("arbitrary",)),
    )(dx2b, z, z, conv_w, conv_b, w_down, perm, perm_t)


def _norm_bwd(dys, w, layer, x, gain, dres, tm, name):
    T = dys[0].shape[0]
    widths = [d.shape[1] for d in dys]
    K = sum(widths)
    n = len(dys)

    def body(*refs):
        dy_refs = refs[:n]
        w_ref, x_ref, g_ref, dres_ref, dx_ref, dxb_ref, dg_ref = refs[n:]

        @pl.when(pl.program_id(0) == 0)
        def _():
            dg_ref[...] = jnp.zeros_like(dg_ref)

        dh, lo = None, 0
        for dy_ref, wd in zip(dy_refs, widths):
            part = _dot_nn(dy_ref[...], w_ref[lo:lo + wd, :])
            dh = part if dh is None else dh + part
            lo += wd
        xf = x_ref[...]
        r = _rms(xf)
        x_hat = xf * r
        dg_ref[...] += jnp.sum(dh * x_hat, axis=0, keepdims=True)
        dxh = dh * g_ref[...]
        dx = dres_ref[...] + r * (dxh - x_hat * jnp.mean(dxh * x_hat, axis=-1, keepdims=True))
        dx_ref[...] = dx
        dxb_ref[...] = dx.astype(bf16)

    row = lambda w_: pl.BlockSpec((tm, w_), lambda i: (i, 0))
    return pl.pallas_call(
        body, name=name, grid=(T // tm,),
        in_specs=[row(wd) for wd in widths] + [_wspec(K, D_MODEL, layer), row(D_MODEL), _full((1, D_MODEL)), row(D_MODEL)],
        out_specs=[row(D_MODEL), row(D_MODEL), _full((1, D_MODEL))],
        out_shape=[jax.ShapeDtypeStruct((T, D_MODEL), f32), jax.ShapeDtypeStruct((T, D_MODEL), bf16),
                   jax.ShapeDtypeStruct((1, D_MODEL), f32)],
        compiler_params=_params(("arbitrary",)),
    )(*dys, w, x, gain, dres)


def _merge_bwd(dx1b, ga, gb, a, b, w_oa_t, w_ob_t, w_out, layer, tm):
    T = dx1b.shape[0]

    def body(dx_ref, ga_ref, gb_ref, a_ref, b_ref, woa_ref, wob_ref, wout_ref,
             da_ref, db_ref, dga_ref, dgb_ref, dya_ref, dys_ref):
        dm = _dot_nt(dx_ref[...], wout_ref[...])
        sa = jax.nn.sigmoid(ga_ref[...].astype(f32))
        sb = jax.nn.sigmoid(gb_ref[...].astype(f32))
        da = (dm * sa).astype(bf16)
        db = (dm * sb).astype(bf16)
        da_ref[...] = da
        db_ref[...] = db
        dga_ref[...] = (dm * a_ref[...].astype(f32) * sa * (1.0 - sa)).astype(bf16)
        dgb_ref[...] = (dm * b_ref[...].astype(f32) * sb * (1.0 - sb)).astype(bf16)
        dya_ref[...] = _dot_nn(da, woa_ref[...]).astype(bf16)
        dys_ref[...] = _dot_nn(db, wob_ref[...]).astype(bf16)

    row = lambda w: pl.BlockSpec((tm, w), lambda i: (i, 0))
    return pl.pallas_call(
        body, name=f"merge_bwd_{layer}", grid=(T // tm,),
        in_specs=[row(D_MODEL)] * 5 + [_wspec(D_MODEL, ATT_WIDTH, layer), _wspec(D_MODEL, SGU_WIDTH, layer),
                                       _wspec(D_MODEL, D_MODEL, layer)],
        out_specs=[row(D_MODEL)] * 4 + [row(512)] * 2,
        out_shape=[jax.ShapeDtypeStruct((T, D_MODEL), bf16)] * 4 + [jax.ShapeDtypeStruct((T, 512), bf16)] * 2,
        compiler_params=_params(("parallel",)),
    )(dx1b, ga, gb, a, b, w_oa_t, w_ob_t, w_out)


def _sgu_bwd(dy, su, sv, gain, w_s, b_s, tm):
    T = su.shape[0]
    gd = SGU_WIDTH // SGU_GROUPS

    def body(dy_ref, su_ref, sv_ref, g_ref, w_ref, b_ref, dsu_ref, dsv_ref, dw_ref, db_ref, dg_ref):
        @pl.when(pl.program_id(0) == 0)
        def _():
            dw_ref[...] = jnp.zeros_like(dw_ref)
            db_ref[...] = jnp.zeros_like(db_ref)
            dg_ref[...] = jnp.zeros_like(dg_ref)

        gain_v = g_ref[...]
        for ch in range(tm // BLOCK):
            rows = slice(ch * BLOCK, (ch + 1) * BLOCK)
            su_c = su_ref[rows, :].astype(f32)
            sv_c = sv_ref[rows, :].astype(f32)
            u, rv, v_hat, vn, w_tril, mixed = _sgu_chunk(su_c, sv_c, gain_v, w_ref, b_ref)
            dyc = dy_ref[rows, :].astype(f32)
            dsu_ref[rows, :] = (dyc * mixed * _gelu_grad(su_c)).astype(bf16)
            dmix = dyc * u
            dmix_b = dmix.astype(bf16)
            dvn = []
            for g in range(SGU_GROUPS):
                gs = slice(g * gd, (g + 1) * gd)
                db_ref[g] += jnp.sum(dmix[:, gs], axis=1, keepdims=True)
                dw_ref[g] += _dot_nt(dmix_b[:, gs], vn[:, gs])
                dvn.append(_dot_tn(w_tril[g], dmix_b[:, gs]))
            dvn = jnp.concatenate(dvn, axis=1)
            dg_ref[...] += jnp.sum(dvn * v_hat, axis=0, keepdims=True)
            dxh = dvn * gain_v
            dvg = rv * (dxh - v_hat * jnp.mean(dxh * v_hat, axis=-1, keepdims=True))
            dsv_ref[rows, :] = (dvg * _gelu_grad(sv_c)).astype(bf16)

    row = pl.BlockSpec((tm, SGU_WIDTH), lambda i: (i, 0))
    return pl.pallas_call(
        body, name="sgu_bwd", grid=(T // tm,),
        in_specs=[row, row, row, _full((1, SGU_WIDTH)), _full((SGU_GROUPS, BLOCK, BLOCK)),
                  _full((SGU_GROUPS, BLOCK, 1))],
        out_specs=[row, row, _full((SGU_GROUPS, BLOCK, BLOCK)), _full((SGU_GROUPS, BLOCK, 1)), _full((1, SGU_WIDTH))],
        out_shape=[jax.ShapeDtypeStruct((T, SGU_WIDTH), bf16)] * 2 + [
            jax.ShapeDtypeStruct((SGU_GROUPS, BLOCK, BLOCK), f32), jax.ShapeDtypeStruct((SGU_GROUPS, BLOCK, 1), f32),
            jax.ShapeDtypeStruct((1, SGU_WIDTH), f32)],
        compiler_params=_params(("arbitrary",)),
    )(dy, su, sv, gain, w_s, b_s)


def _attn_bwd(dy, qkv, qg, kg, sinks, n_seq, seq):
    T = n_seq * seq
    nb = seq // BLOCK
    scale = HEAD_DIM ** -0.5

    def body(dy_ref, cur_ref, prev_ref, qg_ref, kg_ref, sink_ref, dqkv_ref, dqg_ref, dkg_ref, dsink_ref,
             carry_k, carry_v):
        b = pl.program_id(0)
        j = pl.program_id(1)
        n = nb - 1 - j

        @pl.when((b == 0) & (j == 0))
        def _():
            dqg_ref[...] = jnp.zeros_like(dqg_ref)
            dkg_ref[...] = jnp.zeros_like(dkg_ref)
            dsink_ref[...] = jnp.zeros_like(dsink_ref)

        @pl.when(j == 0)
        def _():
            carry_k[...] = jnp.zeros_like(carry_k)
            carry_v[...] = jnp.zeros_like(carry_v)

        cur = cur_ref[...]
        prev = prev_ref[...]
        dyf = dy_ref[...].astype(f32)
        qg_v = qg_ref[...]
        kg_v = kg_ref[...]
        dq_pieces = [None] * (N_KV_HEADS * Q_GROUP)
        dk_pieces, dv_pieces = [], []
        for hk in range(N_KV_HEADS):
            a = _attn_head_group(cur, prev, qg_v, kg_v, sink_ref, n, hk)
            do = jnp.concatenate(
                [dyf[:, (hk * Q_GROUP + g) * HEAD_DIM:(hk * Q_GROUP + g + 1) * HEAD_DIM] for g in range(Q_GROUP)],
                axis=0).astype(bf16)
            p = a["p"]
            dp = _dot_nt(do, a["v"])
            dv_band = _dot_tn(p.astype(bf16), do)
            dsum = jnp.sum(p * dp, axis=-1, keepdims=True)
            ds = (p * (dp - dsum)).astype(bf16)
            dsink_col = -a["p_sink"] * dsum
            for g in range(Q_GROUP):
                head = hk * Q_GROUP + g
                dsink_ref[head:head + 1, :] += jnp.sum(dsink_col[g * BLOCK:(g + 1) * BLOCK], axis=0, keepdims=True)
            dqn = _dot_nn(ds, a["kn"])
            dkn_band = _dot_tn(ds, a["qn"])
            dq_hat_g = dqn * scale
            dqg_ref[...] += jnp.sum(dq_hat_g * a["q_hat"], axis=0, keepdims=True)
            dxh = dq_hat_g * qg_v
            dq = a["rq"] * (dxh - a["q_hat"] * jnp.mean(dxh * a["q_hat"], axis=-1, keepdims=True))
            for g in range(Q_GROUP):
                dq_pieces[hk * Q_GROUP + g] = dq[g * BLOCK:(g + 1) * BLOCK]
            dkn = dkn_band[BLOCK:] + carry_k[hk]
            dv_pieces.append(dv_band[BLOCK:] + carry_v[hk])
            carry_k[hk] = dkn_band[:BLOCK]
            carry_v[hk] = dv_band[:BLOCK]
            k_hat = a["k_hat"][BLOCK:]
            dkg_ref[...] += jnp.sum(dkn * k_hat, axis=0, keepdims=True)
            dxk = dkn * kg_v
            dk_pieces.append(a["rk"][BLOCK:] * (dxk - k_hat * jnp.mean(dxk * k_hat, axis=-1, keepdims=True)))
        dqkv_ref[...] = jnp.concatenate(dq_pieces + dk_pieces + dv_pieces, axis=1).astype(bf16)

    blk = lambda w: pl.BlockSpec((BLOCK, w), lambda b, j: (b * nb + nb - 1 - j, 0))
    return pl.pallas_call(
        body, name="attn_bwd", grid=(n_seq, nb),
        in_specs=[blk(ATT_WIDTH), blk(768),
                  pl.BlockSpec((BLOCK, 256), lambda b, j: (b * nb + jnp.maximum(nb - 2 - j, 0), 2)),
                  _full((1, HEAD_DIM)), _full((1, HEAD_DIM)), pl.BlockSpec(memory_space=pltpu.SMEM)],
        out_specs=[blk(768), _full((1, HEAD_DIM)), _full((1, HEAD_DIM)), _full((8, 128))],
        out_shape=[jax.ShapeDtypeStruct((T, 768), bf16), jax.ShapeDtypeStruct((1, HEAD_DIM), f32),
                   jax.ShapeDtypeStruct((1, HEAD_DIM), f32), jax.ShapeDtypeStruct((8, 128), f32)],
        scratch_shapes=[pltpu.VMEM((N_KV_HEADS, BLOCK, HEAD_DIM), f32), pltpu.VMEM((N_KV_HEADS, BLOCK, HEAD_DIM), f32)],
        compiler_params=_params(("arbitrary", "arbitrary")),
    )(dy, qkv, qkv, qg, kg, sinks)


def _weight_grad(a, b, tm, tk, name):
    T, M = a.shape
    N = b.shape[1]
    nk = T // tk

    def body(a_ref, b_ref, o_ref, acc_ref):
        k = pl.program_id(1)

        @pl.when(k == 0)
        def _():
            acc_ref[...] = jnp.zeros_like(acc_ref)

        acc_ref[...] += _dot_tn(a_ref[...], b_ref[...])

        @pl.when(k == nk - 1)
        def _():
            o_ref[...] = acc_ref[...].astype(bf16)

    return pl.pallas_call(
        body, name=name, grid=(M // tm, nk),
        in_specs=[pl.BlockSpec((tk, tm), lambda i, k: (k, i)), pl.BlockSpec((tk, N), lambda i, k: (k, 0))],
        out_specs=pl.BlockSpec((None, tm, N), lambda i, k: (0, i, 0)),
        out_shape=jax.ShapeDtypeStruct((1, M, N), bf16),
        scratch_shapes=[pltpu.VMEM((tm, N), f32)],
        compiler_params=_params(("parallel", "arbitrary")),
    )(a, b)


def _weight_grad_rows(a_list, b, tk, name):
    T, N = b.shape
    widths = [a.shape[1] for a in a_list]
    M = sum(widths)
    nk = T // tk
    n = len(a_list)

    def body(*refs):
        a_refs = refs[:n]
        b_ref, o_ref, acc_ref = refs[n:]
        k = pl.program_id(0)

        @pl.when(k == 0)
        def _():
            acc_ref[...] = jnp.zeros_like(acc_ref)

        lo = 0
        for a_ref, wd in zip(a_refs, widths):
            acc_ref[lo:lo + wd, :] += _dot_tn(a_ref[...], b_ref[...])
            lo += wd

        @pl.when(k == nk - 1)
        def _():
            o_ref[...] = acc_ref[...].astype(bf16)

    return pl.pallas_call(
        body, name=name, grid=(nk,),
        in_specs=[pl.BlockSpec((tk, wd), lambda k: (k, 0)) for wd in widths] + [pl.BlockSpec((tk, N), lambda k: (k, 0))],
        out_specs=pl.BlockSpec((None, M, N), lambda k: (0, 0, 0)),
        out_shape=jax.ShapeDtypeStruct((1, M, N), bf16),
        scratch_shapes=[pltpu.VMEM((M, N), f32)],
        compiler_params=_params(("arbitrary",)),
    )(*a_list, b)


def _place(src, layer, src_slot, n_slots, dst_slot, dtype, name):
    _, _, rows, cols = src.shape
    slots = jnp.stack([src_slot, dst_slot]).astype(jnp.int32)

    def body(slots_ref, s_ref, o_ref):
        o_ref[...] = s_ref[...].astype(dtype)

    return pl.pallas_call(
        body, name=name,
        grid_spec=pltpu.PrefetchScalarGridSpec(
            num_scalar_prefetch=1, grid=(1,),
            in_specs=[pl.BlockSpec((None, None, rows, cols), lambda i, sl: (layer, sl[0], 0, 0))],
            out_specs=pl.BlockSpec((None, rows, cols), lambda i, sl: (sl[1], 0, 0))),
        out_shape=jax.ShapeDtypeStruct((n_slots, rows, cols), dtype),
        compiler_params=_params(("arbitrary",)),
    )(slots, src)


HBM = pl.BlockSpec(memory_space=pltpu.HBM)
SEM = pl.BlockSpec(memory_space=pltpu.SEMAPHORE)
DATAFLOW = pltpu.SideEffectType.DATAFLOW_SIDE_EFFECTING


def _other_chips(x, y):
    return [(1 - x, y), (x, 1 - y), (1 - x, 1 - y)]


def _split_start(groups, name):
    nb = [len(bufs) for bufs, _ in groups]
    flat = [b for bufs, _ in groups for b in bufs]
    ns = [len(plan(bufs, dry=True)) for bufs, plan in groups]
    ng = len(groups)

    def body(*refs):
        n_in = len(flat)
        sems = refs[n_in:n_in + 2 * ng]
        thru = refs[n_in + 2 * ng:2 * n_in + 2 * ng]
        token = refs[2 * n_in + 2 * ng]
        off = 0
        for g, (bufs, plan) in enumerate(groups):
            mine = thru[off:off + nb[g]]
            off += nb[g]
            for k, (src, dst, to) in enumerate(plan(mine)):
                pltpu.make_async_remote_copy(
                    src_ref=src, dst_ref=dst, send_sem=sems[2 * g].at[k], recv_sem=sems[2 * g + 1].at[k],
                    device_id=to, device_id_type=MESH).start()
        token[...] = jnp.zeros_like(token)

    out_shape = []
    for n in ns:
        out_shape += [pltpu.SemaphoreType.DMA((n,)), pltpu.SemaphoreType.DMA((n,))]
    out_shape += [pltpu.HBM(b.shape, b.dtype) for b in flat]
    out_shape.append(jax.ShapeDtypeStruct((8, 128), f32))
    res = pl.pallas_call(
        body, name=name, out_shape=tuple(out_shape),
        in_specs=[HBM] * len(flat),
        out_specs=tuple([SEM] * (2 * ng) + [HBM] * len(flat) + [pl.BlockSpec(memory_space=pltpu.VMEM)]),
        input_output_aliases={i: 2 * ng + i for i in range(len(flat))},
        compiler_params=pltpu.CompilerParams(has_side_effects=DATAFLOW),
    )(*[pltpu.with_memory_space_constraint(b, pltpu.HBM) for b in flat])
    out, off = [], 2 * ng
    for g in range(ng):
        out.append((res[2 * g], res[2 * g + 1], list(res[off:off + nb[g]])))
        off += nb[g]
    return out, res[-1]


def _split_wait(bufs, send, recv, plan, after, name):
    nb = len(bufs)

    def body(*refs):
        thru = refs[:nb]
        send_ref, recv_ref = refs[nb], refs[nb + 1]
        for k, (src, dst, to) in enumerate(plan(thru)):
            cp = pltpu.make_async_remote_copy(
                src_ref=src, dst_ref=dst, send_sem=send_ref.at[k], recv_sem=recv_ref.at[k],
                device_id=to, device_id_type=MESH)
            cp.wait_send()
            cp.wait_recv()

    res = pl.pallas_call(
        body, name=name, out_shape=tuple(pltpu.HBM(b.shape, b.dtype) for b in bufs),
        in_specs=[HBM] * nb + [SEM, SEM, ANY], out_specs=tuple([HBM] * nb),
        input_output_aliases={i: i for i in range(nb)},
        compiler_params=pltpu.CompilerParams(has_side_effects=DATAFLOW),
    )(*bufs, send, recv, after)
    return list(res)


def _gather_plan(hrs, n_direct=0):
    def plan(refs, dry=False):
        if dry:
            return [None] * (4 * len(hrs) + 3 * n_direct)
        x, y, c = _mesh_pos()
        me = 4 * x + 2 * y + c
        out = []
        for i in range(n_direct):
            src, land = refs[len(hrs) + 2 * i], refs[len(hrs) + 2 * i + 1]
            out += [(src, land.at[2 * x + y], (*chip, c)) for chip in _other_chips(x, y)]
        for ref, hr in zip(refs, hrs):
            rows = ref.at[pl.ds(pl.multiple_of(me * hr, 16), hr), :]
            out.append((rows, rows, (x, y, 1 - c)))
            out += [(rows, rows, (*chip, c)) for chip in _other_chips(x, y)]
        return out
    return plan


def _all_to_all_plan(n):
    def plan(refs, dry=False):
        if dry:
            return [None] * (7 * n)
        x, y, c = _mesh_pos()
        out = []
        for ref in refs:
            mine = ref.at[4 * x + 2 * y + c]
            for fx in range(2):
                for fy in range(2):
                    for fc in range(2):
                        if fx or fy or fc:
                            out.append((mine, mine, (1 - x if fx else x, 1 - y if fy else y, 1 - c if fc else c)))
        return out
    return plan


def _pass_to_sibling(bufs, hrs, name):
    nb = len(bufs)

    def body(*refs):
        out = refs[nb:2 * nb]
        send, recv = refs[2 * nb:]
        x, y, c = _mesh_pos()
        chips = _other_chips(x, y)
        started = []
        for i in range(nb):
            for j, chip in enumerate(chips):
                rows = out[i].at[pl.ds(pl.multiple_of((4 * chip[0] + 2 * chip[1] + c) * hrs[i], 16), hrs[i]), :]
                cp = pltpu.make_async_remote_copy(
                    src_ref=rows, dst_ref=rows, send_sem=send.at[3 * i + j], recv_sem=recv.at[3 * i + j],
                    device_id=(x, y, 1 - c), device_id_type=MESH)
                cp.start()
                started.append(cp)
        for i in range(nb):
            for j, chip in enumerate(chips):
                rows = out[i].at[pl.ds(pl.multiple_of((4 * chip[0] + 2 * chip[1] + 1 - c) * hrs[i], 16), hrs[i]), :]
                pltpu.make_async_remote_copy(
                    src_ref=rows, dst_ref=rows, send_sem=send.at[3 * i + j], recv_sem=recv.at[3 * i + j],
                    device_id=(x, y, 1 - c), device_id_type=MESH).wait_recv()
        for cp in started:
            cp.wait_send()

    return list(pl.pallas_call(
        body, name=name, in_specs=[ANY] * nb, out_specs=[ANY] * nb,
        out_shape=[jax.ShapeDtypeStruct(b.shape, b.dtype) for b in bufs],
        input_output_aliases={i: i for i in range(nb)},
        scratch_shapes=[pltpu.SemaphoreType.DMA((3 * nb,)), pltpu.SemaphoreType.DMA((3 * nb,))],
        compiler_params=pltpu.CompilerParams(has_side_effects=True),
    )(*bufs))


def _pair_exchange(grads, name):
    nr = len(grads)
    n_l = grads[0].shape[0]
    n_sem = nr * n_l * N_CHIP

    def body(*refs):
        src = refs[:nr]
        out = refs[nr:2 * nr]
        send, recv = refs[2 * nr:]
        x, y, c = _mesh_pos()
        copies = []
        for r in range(nr):
            hr = grads[r].shape[1] // N_DEV
            for layer in range(n_l):
                for j in range(N_CHIP):
                    idx = (r * n_l + layer) * N_CHIP + j
                    start = pl.multiple_of((2 * j + 1 - c) * hr, 16)
                    cp = pltpu.make_async_remote_copy(
                        src_ref=src[r].at[layer, pl.ds(start, hr), :], dst_ref=out[r].at[layer, j],
                        send_sem=send.at[idx], recv_sem=recv.at[idx], device_id=(x, y, 1 - c), device_id_type=MESH)
                    cp.start()
                    copies.append(cp)
        for cp in copies:
            cp.wait()

    return pl.pallas_call(
        body, name=name,
        in_specs=[ANY] * nr, out_specs=[ANY] * nr,
        out_shape=[jax.ShapeDtypeStruct((n_l, N_CHIP, g.shape[1] // N_DEV, g.shape[2]), bf16) for g in grads],
        scratch_shapes=[pltpu.SemaphoreType.DMA((n_sem,)), pltpu.SemaphoreType.DMA((n_sem,))],
        compiler_params=pltpu.CompilerParams(has_side_effects=True),
    )(*grads)


def _pair_sum(grad, other, core, chip, name):
    n_l, rows, cols = grad.shape
    hr = rows // N_DEV
    g5 = grad.reshape(n_l, N_CHIP, 2, hr, cols)
    where = jnp.stack([core, chip]).astype(jnp.int32)

    def body(where_ref, g_ref, o_ref, s_ref, mine_ref):
        s = (g_ref[...].astype(f32) + o_ref[...].astype(f32)).astype(bf16)
        s_ref[...] = s

        @pl.when(pl.program_id(1) == where_ref[1])
        def _():
            mine_ref[...] = s

    return pl.pallas_call(
        body, name=name,
        grid_spec=pltpu.PrefetchScalarGridSpec(
            num_scalar_prefetch=1, grid=(n_l, N_CHIP),
            in_specs=[pl.BlockSpec((None, None, None, hr, cols), lambda l, j, w: (l, j, w[0], 0, 0)),
                      pl.BlockSpec((None, None, hr, cols), lambda l, j, w: (l, j, 0, 0))],
            out_specs=[pl.BlockSpec((None, None, hr, cols), lambda l, j, w: (l, j, 0, 0)),
                       pl.BlockSpec((None, None, hr, cols), lambda l, j, w: (l, w[1], 0, 0))]),
        out_shape=[jax.ShapeDtypeStruct((n_l, N_CHIP, hr, cols), bf16)] * 2,
        compiler_params=_params(("arbitrary", "arbitrary")),
    )(where, g5, other)


def _chip_plan(nr, n_l):
    def plan(refs, dry=False):
        if dry:
            return [None] * (nr * n_l * 3)
        x, y, c = _mesh_pos()
        out = []
        for r in range(nr):
            for layer in range(n_l):
                for chip in _other_chips(x, y):
                    out.append((refs[r].at[layer, 2 * chip[0] + chip[1]], refs[nr + r].at[layer, 2 * x + y], (*chip, c)))
        return out
    return plan


def _chip_sum(parts, core, name):
    n_l, _, hr, cols = parts.shape

    def body(core_ref, p_ref, o_ref):
        acc = p_ref[0].astype(f32) + p_ref[1].astype(f32)
        acc = acc + p_ref[2].astype(f32)
        o_ref[...] = acc + p_ref[3].astype(f32)

    return pl.pallas_call(
        body, name=name,
        grid_spec=pltpu.PrefetchScalarGridSpec(
            num_scalar_prefetch=1, grid=(n_l,),
            in_specs=[pl.BlockSpec((None, N_CHIP, hr, cols), lambda l, cr: (l, 0, 0, 0))],
            out_specs=pl.BlockSpec((None, None, hr, cols), lambda l, cr: (l, cr[0], 0, 0))),
        out_shape=jax.ShapeDtypeStruct((n_l, 2, hr, cols), f32),
        compiler_params=_params(("arbitrary",)),
    )(core, parts)


def _share_halves(halves):
    nr = len(halves)

    def body(*refs):
        out = refs[nr:2 * nr]
        send, recv = refs[2 * nr:]
        x, y, c = _mesh_pos()
        copies = []
        for r in range(nr):
            cp = pltpu.make_async_remote_copy(
                src_ref=out[r].at[0, c], dst_ref=out[r].at[0, c], send_sem=send.at[r],
                recv_sem=recv.at[r], device_id=(x, y, 1 - c), device_id_type=MESH)
            cp.start()
            copies.append(cp)
        for r in range(nr):
            copies[r].wait_send()
            pltpu.make_async_remote_copy(
                src_ref=out[r].at[0, 1 - c], dst_ref=out[r].at[0, 1 - c], send_sem=send.at[r],
                recv_sem=recv.at[r], device_id=(x, y, 1 - c), device_id_type=MESH).wait_recv()

    return pl.pallas_call(
        body, name="grad_share_halves",
        in_specs=[ANY] * nr, out_specs=[ANY] * nr,
        out_shape=[jax.ShapeDtypeStruct(h.shape, h.dtype) for h in halves],
        input_output_aliases={r: r for r in range(nr)},
        scratch_shapes=[pltpu.SemaphoreType.DMA((nr,))] * 2,
        compiler_params=pltpu.CompilerParams(has_side_effects=True),
    )(*halves)


def _sum_small(parts, name):
    n, rows, cols = parts.shape

    def body(p_ref, o_ref):
        acc = p_ref[0].astype(f32)
        for d in range(1, n):
            acc = acc + p_ref[d].astype(f32)
        o_ref[...] = acc

    return pl.pallas_call(
        body, name=name, grid=(rows // 16,),
        in_specs=[pl.BlockSpec((n, 16, cols), lambda i: (0, i, 0))], out_specs=pl.BlockSpec((16, cols), lambda i: (i, 0)),
        out_shape=jax.ShapeDtypeStruct((rows, cols), f32),
        compiler_params=_params(("parallel",)),
    )(parts)


def _adamw(w, g, m, v, name):
    n_l, rows, cols = w.shape
    tr = rows
    for cand in (512, 256, 128, 64, 32, 16, 8):
        if rows % cand == 0 and rows > cand:
            tr = cand
            break

    def body(w_ref, g_ref, m_ref, v_ref, d_ref, nm_ref, nv_ref):
        gg = g_ref[...]
        nm = ADAM_B1 * m_ref[...] + (1.0 - ADAM_B1) * gg
        nv = ADAM_B2 * v_ref[...] + (1.0 - ADAM_B2) * (gg * gg)
        m_hat = nm / (1.0 - ADAM_B1 ** ADAM_STEP)
        v_hat = nv / (1.0 - ADAM_B2 ** ADAM_STEP)
        d_ref[...] = -ADAM_LR * (m_hat / (jnp.sqrt(v_hat) + ADAM_EPS) + ADAM_WD * w_ref[...])
        nm_ref[...] = nm
        nv_ref[...] = nv

    blk = pl.BlockSpec((None, tr, cols), lambda l, i: (l, i, 0))
    return pl.pallas_call(
        body, name=name, grid=(n_l, rows // tr),
        in_specs=[blk] * 4, out_specs=[blk] * 3, out_shape=[jax.ShapeDtypeStruct((n_l, rows, cols), f32)] * 3,
        compiler_params=_params(("parallel", "parallel")),
    )(w, g, m, v)


SMALL = ("mix_norm", "q_norm", "k_norm", "sinks", "sgu_norm", "w_s", "b_s", "ffn_norm", "conv_b", "conv_w")


def _pack_small(arrs):
    flat = jnp.concatenate([a.reshape(-1) for a in arrs])
    pad = (-flat.shape[0]) % (16 * 1024)
    return jnp.pad(flat, (0, pad)).reshape(-1, 1024)


def _unpack_small(pack, shapes):
    flat = pack.reshape(-1)
    out, off = [], 0
    for s in shapes:
        n = int(np.prod(s))
        out.append(flat[off:off + n].reshape(s))
        off += n
    return out


def kernel(x, mix_norm, w_in, q_norm, k_norm, sinks, sgu_norm, w_s, b_s, w_oa, w_ob, w_out, ffn_norm, w_up, conv_w, conv_b, w_down, loss_target, m_mix_norm, m_w_in, m_q_norm, m_k_norm, m_sinks, m_sgu_norm, m_w_s, m_b_s, m_w_oa, m_w_ob, m_w_out, m_ffn_norm, m_w_up, m_conv_w, m_conv_b, m_w_down, v_mix_norm, v_w_in, v_q_norm, v_k_norm, v_sinks, v_sgu_norm, v_w_s, v_b_s, v_w_oa, v_w_ob, v_w_out, v_ffn_norm, v_w_up, v_conv_w, v_conv_b, v_w_down):
    weights = dict(mix_norm=mix_norm, w_in=w_in, q_norm=q_norm, k_norm=k_norm, sinks=sinks, sgu_norm=sgu_norm,
                   w_s=w_s, b_s=b_s, w_oa=w_oa, w_ob=w_ob, w_out=w_out, ffn_norm=ffn_norm, w_up=w_up,
                   conv_w=conv_w, conv_b=conv_b, w_down=w_down)
    mom_m = dict(mix_norm=m_mix_norm, w_in=m_w_in, q_norm=m_q_norm, k_norm=m_k_norm, sinks=m_sinks,
                 sgu_norm=m_sgu_norm, w_s=m_w_s, b_s=m_b_s, w_oa=m_w_oa, w_ob=m_w_ob, w_out=m_w_out,
                 ffn_norm=m_ffn_norm, w_up=m_w_up, conv_w=m_conv_w, conv_b=m_conv_b, w_down=m_w_down)
    mom_v = dict(mix_norm=v_mix_norm, w_in=v_w_in, q_norm=v_q_norm, k_norm=v_k_norm, sinks=v_sinks,
                 sgu_norm=v_sgu_norm, w_s=v_w_s, b_s=v_b_s, w_oa=v_w_oa, w_ob=v_w_ob, w_out=v_w_out,
                 ffn_norm=v_ffn_norm, w_up=v_w_up, conv_w=v_conv_w, conv_b=v_conv_b, w_down=v_w_down)
    n_seq, seq, _ = x.shape
    T = n_seq * seq
    core = lax.axis_index("c")
    chip = 2 * lax.axis_index("x") + lax.axis_index("y")
    tm = min(512, seq)
    tm_ff = min(256, seq)

    me = 2 * chip + core
    names = [r[0] for r in REGIONS]
    hrs = {name: rows // N_DEV for name, rows, _, _ in REGIONS}
    placed = [{}, {}]
    for name, rows, cols, transposed in REGIONS:
        shard = (jnp.swapaxes(weights[name], 1, 2) if transposed else weights[name]).reshape(2, 2, hrs[name], cols)
        for l in range(2):
            placed[l][name] = _place(shard, l, core, N_DEV, me, bf16, f"place_{name}_{l}").reshape(rows, cols)
    group_keys = [[(0, "w_in")], [(0, n) for n in names[1:]], [(1, n) for n in names]]
    group_bufs = [[placed[l][n] for l, n in keys] for keys in group_keys]
    group_bufs[0] += [conv_w, jnp.zeros((N_CHIP,) + conv_w.shape, f32)]
    n_direct = [1, 0, 0]
    plans = [_gather_plan([hrs[n] for _, n in keys], nd) for keys, nd in zip(group_keys, n_direct)]
    started, _ = _split_start(list(zip(group_bufs, plans)), "gather_start")
    gathered = [{}, {}]

    def finish_gather(g, after):
        send, recv, bufs = started[g]
        keys = group_keys[g]
        hr_list = [hrs[n] for _, n in keys]
        bufs = _split_wait(bufs, send, recv, _gather_plan(hr_list, n_direct[g]), after, f"gather_wait_{g}")
        passed = _pass_to_sibling(bufs[:len(keys)], hr_list, f"gather_pass_{g}")
        for (l, n), b in zip(keys, passed):
            gathered[l][n] = b
        return bufs[len(keys):]

    xs = x.reshape(T, D_MODEL)
    _, conv_w_land = finish_gather(0, xs)
    conv_w_all = lax.dynamic_update_slice(conv_w_land, conv_w[None], (chip, 0, 0, 0))
    conv_w_full = jnp.concatenate([conv_w_all[j] for j in range(N_CHIP)], axis=-1)
    saved = []
    cur = xs
    for l in range(2):
        wl = gathered[l]
        b_col = b_s[l].reshape(SGU_GROUPS, BLOCK, 1)
        qkv, su, sv, ga, gb, h = _in_proj(cur, mix_norm[l][None], wl["w_in"], l, tm)
        y_att = _attn_fwd(qkv, q_norm[l][None], k_norm[l][None], sinks[l], n_seq, seq)
        y_sgu = _sgu_fwd(su, sv, sgu_norm[l][None], w_s[l], b_col, tm)
        if l == 0:
            finish_gather(1, y_sgu)
        x1, merged, a_o, b_o = _merge_fwd(cur, y_att, y_sgu, ga, gb, wl["w_oa"], wl["w_ob"], wl["w_out"], l, tm)
        h2, z, act = _ffn_up(x1, ffn_norm[l][None], wl["w_up"], conv_w_full[l], conv_b[l][None], l, seq, tm_ff)
        x2 = _ffn_down(x1, act, wl["w_down"], l, tm)
        if l == 0:
            finish_gather(2, x2)
        saved.append(dict(x=cur, qkv=qkv, su=su, sv=sv, ga=ga, gb=gb, h=h, y_att=y_att, y_sgu=y_sgu, x1=x1,
                          merged=merged, a=a_o, b=b_o, h2=h2, z=z, act=act, b_col=b_col))
        cur = x2

    dy, dyb, loss_part = _loss_head(cur, loss_target.reshape(T, D_MODEL), tm)
    loss = lax.psum(loss_part[0, 0], ("x", "y", "c"))

    core_arr = core.astype(jnp.int32).reshape(1)
    big = [{}, {}]
    small = {name: [None, None] for name in SMALL}

    def start_reduce(l, keys, tag):
        gl = [big[l][n] for n in keys]
        from_sibling = _pair_exchange(gl, f"pair_exchange_{tag}")
        pairs = [_pair_sum(g, o, core, chip, f"pair_sum_{n}_{l}") for g, o, n in zip(gl, from_sibling, keys)]
        bufs = [p[0] for p in pairs] + [p[1] for p in pairs]
        (res,), token = _split_start([(bufs, _chip_plan(len(keys), 1))], f"chip_start_{tag}")
        return (l, keys, res, tag), token[0:1, 0:1]

    def finish_reduce(state, after):
        l, keys, (send, recv, bufs), tag = state
        bufs = _split_wait(bufs, send, recv, _chip_plan(len(keys), 1), after, f"chip_wait_{tag}")
        return {(l, n): _chip_sum(p, core_arr, f"chip_sum_{n}_{l}") for n, p in zip(keys, bufs[len(keys):])}

    rest = [n for n in SMALL if n != "w_s"]
    rest_shapes = [weights[n].shape[1:] if n != "conv_w" else (3, 2 * D_FF) for n in rest]
    zero = jnp.zeros((), jnp.int32)

    def start_small(l):
        packs = [(_pack_small([small[n][l] for n in rest]), f32, "small"), (small["w_s"][l].reshape(-1, 1024), bf16, "w_s")]
        bufs = [_place(p[None, None], 0, zero, N_DEV, me, dt, f"place_{tag}_{l}") for p, dt, tag in packs]
        (res,), token = _split_start([(bufs, _all_to_all_plan(2))], f"small_start_{l}")
        return res, token[0:1, 0:1]

    def finish_small(l, res, after):
        send, recv, bufs = res
        bufs = _split_wait(bufs, send, recv, _all_to_all_plan(2), after, f"small_wait_{l}")
        out = dict(zip(rest, _unpack_small(_sum_small(bufs[0], f"sum_small_{l}"), rest_shapes)))
        out["w_s"] = _sum_small(bufs[1], f"sum_w_s_{l}").reshape(w_s.shape[1:])
        return out

    pending = []
    after_start = jnp.zeros((1, 1), f32)
    for l in (1, 0):
        s = saved[l]
        wl = gathered[l]
        dz, dconv = _ffn_bwd(dyb, s["z"], conv_w_full[l], conv_b[l][None] + after_start, wl["w_down"], l, seq, tm_ff)
        big[l]["w_down"] = _weight_grad(s["act"], dyb, 1408, tm, f"dw_down_{l}")
        big[l]["w_up"] = _weight_grad(dz, s["h2"], 1408, tm, f"dw_up_{l}")
        ffn_gain = ffn_norm[l][None]
        if l == 0:
            state, tok = start_reduce(0, ["w_down", "w_up"], "0a")
            pending.append(state)
            ffn_gain = ffn_gain + tok
        dx1, dx1b, d_ffn = _norm_bwd([dz], wl["w_up"], l, s["x1"], ffn_gain, dy, tm, f"ffn_norm_bwd_{l}")
        small["conv_w"][l] = dconv[0:3]
        small["conv_b"][l] = dconv[3]
        small["ffn_norm"][l] = d_ffn[0]
        da, db, dga, dgb, dya, dys = _merge_bwd(dx1b, s["ga"], s["gb"], s["a"], s["b"],
                                                wl["w_oa"], wl["w_ob"], wl["w_out"], l, tm)
        big[l]["w_out"] = _weight_grad(s["merged"], dx1b, 1024, tm, f"dw_out_{l}")
        big[l]["w_oa"] = _weight_grad(da, s["y_att"], 1024, tm, f"dw_oa_{l}")
        big[l]["w_ob"] = _weight_grad(db, s["y_sgu"], 1024, tm, f"dw_ob_{l}")
        sgu_gain = sgu_norm[l][None]
        if l == 0:
            state, tok = start_reduce(0, ["w_out", "w_oa", "w_ob"], "0m")
            pending.append(state)
            sgu_gain = sgu_gain + tok
        dsu, dsv, d_ws, d_bs, d_sgu = _sgu_bwd(dys, s["su"], s["sv"], sgu_gain, w_s[l], s["b_col"], tm)
        causal = np.tril(np.ones((BLOCK, BLOCK), bool))
        small["w_s"][l] = jnp.where(causal[None], d_ws, 0.0)
        small["b_s"][l] = d_bs[:, :, 0]
        small["sgu_norm"][l] = d_sgu[0]
        dqkv, d_qg, d_kg, d_sink = _attn_bwd(dya, s["qkv"], q_norm[l][None], k_norm[l][None], sinks[l], n_seq, seq)
        small["q_norm"][l] = d_qg[0]
        small["k_norm"][l] = d_kg[0]
        small["sinks"][l] = d_sink[:, 0]
        dproj = [dqkv, dsu, dsv, dga, dgb]
        big[l]["w_in"] = _weight_grad_rows(dproj, s["h"], tm, f"dw_in_{l}")
        state, tok = start_reduce(l, names if l == 1 else ["w_in"], "1" if l == 1 else "0b")
        pending.append(state)
        dy, dyb, d_mix = _norm_bwd(dproj, wl["w_in"], l, s["x"], mix_norm[l][None] + tok, dx1, tm, f"mix_norm_bwd_{l}")
        small["mix_norm"][l] = d_mix[0]
        if l == 1:
            small_1, after_start = start_small(1)
    grad_x = dy.reshape(n_seq, seq, D_MODEL)

    small_0, _ = start_small(0)
    halves = {}
    for state in pending:
        halves.update(finish_reduce(state, dyb))
    half_keys = [(l, n) for l in range(2) for n in names]
    shared = dict(zip(half_keys, _share_halves([halves[k] for k in half_keys])))
    grad_big = {}
    for name, rows, cols, transposed in REGIONS:
        per_layer = [shared[(l, name)].reshape(rows // N_CHIP, cols) for l in range(2)]
        grad_big[name] = jnp.stack([g.T if transposed else g for g in per_layer])

    grad, delta, new_m, new_v = {}, {}, {}, {}
    for name, *_ in REGIONS:
        grad[name] = grad_big[name]
        delta[name], new_m[name], new_v[name] = _adamw(weights[name], grad[name], mom_m[name], mom_v[name], f"adamw_{name}")

    per_layer = [finish_small(0, small_0, delta["w_down"]), finish_small(1, small_1, dyb)]
    grad_small = {n: jnp.stack([per_layer[0][n], per_layer[1][n]]) for n in SMALL}
    cw_cols = conv_w.shape[-1]
    grad_small["conv_w"] = lax.dynamic_slice_in_dim(grad_small["conv_w"], chip * cw_cols, cw_cols, axis=2)

    as_rows = lambda a: a.reshape(2, -1, BLOCK)
    d, nm, nv = _adamw(as_rows(w_s), as_rows(grad_small["w_s"]), as_rows(m_w_s), as_rows(v_w_s), "adamw_w_s")
    grad["w_s"], delta["w_s"], new_m["w_s"], new_v["w_s"] = (
        grad_small["w_s"], d.reshape(w_s.shape), nm.reshape(w_s.shape), nv.reshape(w_s.shape))
    shapes = [weights[n].shape for n in rest]
    d, nm, nv = _adamw(_pack_small([weights[n] for n in rest])[None], _pack_small([grad_small[n] for n in rest])[None],
                       _pack_small([mom_m[n] for n in rest])[None], _pack_small([mom_v[n] for n in rest])[None],
                       "adamw_small")
    for n, dd, mm, vv in zip(rest, _unpack_small(d, shapes), _unpack_small(nm, shapes), _unpack_small(nv, shapes)):
        grad[n], delta[n], new_m[n], new_v[n] = grad_small[n], dd, mm, vv

    order = ["mix_norm", "w_in", "q_norm", "k_norm", "sinks", "sgu_norm", "w_s", "b_s", "w_oa", "w_ob", "w_out",
             "ffn_norm", "w_up", "conv_w", "conv_b", "w_down"]
    return (loss, grad_x, *[grad[n] for n in order], *[delta[n] for n in order],
            *[new_m[n] for n in order], *[new_v[n] for n in order])
```

```python
import functools

import numpy as np
import jax
import jax.numpy as jnp
from jax import lax
from jax.experimental import pallas as pl
from jax.experimental.pallas import tpu as pltpu

bf16 = jnp.bfloat16
f32 = jnp.float32

D_MODEL = 1024
ATT_WIDTH = 512
KV_WIDTH = 128
SGU_WIDTH = 512
HEAD_DIM = 64
N_KV_HEADS = 2
Q_GROUP = 4
BLOCK = 128
SGU_GROUPS = 8
IN_WIDTH = 3840
D_FF = 2816
NORM_EPS = 1e-6
NEG_INF = -1e30
N_DEV = 8
N_CHIP = 4

ADAM_LR = 0.001
ADAM_B1 = 0.9
ADAM_B2 = 0.999
ADAM_EPS = 1e-08
ADAM_WD = 0.01
ADAM_STEP = 10

V7X_VMEM_LIMIT = 56 * 1024 * 1024
FF_CHUNK = 256

REGIONS = (
    ("w_in", IN_WIDTH, D_MODEL, True),
    ("w_oa", D_MODEL, ATT_WIDTH, True),
    ("w_ob", D_MODEL, SGU_WIDTH, True),
    ("w_out", D_MODEL, D_MODEL, False),
    ("w_up", 2 * D_FF, D_MODEL, True),
    ("w_down", D_FF, D_MODEL, False),
)
MESH = pl.DeviceIdType.MESH
ANY = pl.BlockSpec(memory_space=pl.ANY)


def _params(sem=None, **kw):
    return pltpu.CompilerParams(dimension_semantics=sem, vmem_limit_bytes=V7X_VMEM_LIMIT, **kw)


def _wspec(rows, cols, layer=None):
    del layer
    return pl.BlockSpec((rows, cols), lambda *_: (0, 0), pipeline_mode=pl.Buffered(1))


def _full(shape):
    nd = len(shape)
    return pl.BlockSpec(shape, lambda *_: (0,) * nd)


def _dot_nn(a, b):
    return jnp.dot(a, b, preferred_element_type=f32)


def _dot_nt(a, b):
    return lax.dot_general(a, b, (((1,), (1,)), ((), ())), preferred_element_type=f32)


def _dot_tn(a, b):
    return lax.dot_general(a, b, (((0,), (0,)), ((), ())), preferred_element_type=f32)


_GELU_C = float(np.sqrt(2.0 / np.pi))


def _gelu(x):
    return 0.5 * x * (1.0 + jnp.tanh(_GELU_C * (x + 0.044715 * x * x * x)))


def _gelu_grad(x):
    t = jnp.tanh(_GELU_C * (x + 0.044715 * x * x * x))
    du = _GELU_C * (1.0 + 3.0 * 0.044715 * x * x)
    return 0.5 * (1.0 + t) + 0.5 * x * (1.0 - t * t) * du


def _rms(x):
    return lax.rsqrt(jnp.mean(x * x, axis=-1, keepdims=True) + NORM_EPS)


def _mesh_pos():
    return lax.axis_index("x"), lax.axis_index("y"), lax.axis_index("c")


def _in_proj(x, gain, w_in_t, layer, tm):
    T = x.shape[0]

    def body(x_ref, g_ref, w_ref, qkv_ref, su_ref, sv_ref, ga_ref, gb_ref, h_ref):
        xf = x_ref[...]
        h = (xf * _rms(xf) * g_ref[...]).astype(bf16)
        h_ref[...] = h
        qkv_ref[...] = _dot_nt(h, w_ref[0:768, :])
        su_ref[...] = _dot_nt(h, w_ref[768:1280, :]).astype(bf16)
        sv_ref[...] = _dot_nt(h, w_ref[1280:1792, :]).astype(bf16)
        ga_ref[...] = _dot_nt(h, w_ref[1792:2816, :]).astype(bf16)
        gb_ref[...] = _dot_nt(h, w_ref[2816:3840, :]).astype(bf16)

    row = lambda w: pl.BlockSpec((tm, w), lambda i: (i, 0))
    return pl.pallas_call(
        body, name=f"in_proj_{layer}", grid=(T // tm,),
        in_specs=[row(D_MODEL), _full((1, D_MODEL)), _wspec(IN_WIDTH, D_MODEL, layer)],
        out_specs=[row(768), row(512), row(512), row(1024), row(1024), row(D_MODEL)],
        out_shape=[jax.ShapeDtypeStruct((T, 768), f32), jax.ShapeDtypeStruct((T, 512), bf16),
                   jax.ShapeDtypeStruct((T, 512), bf16), jax.ShapeDtypeStruct((T, 1024), bf16),
                   jax.ShapeDtypeStruct((T, 1024), bf16), jax.ShapeDtypeStruct((T, D_MODEL), bf16)],
        compiler_params=_params(("parallel",)),
    )(x, gain, w_in_t)


def _attn_head_group(cur, prev, qg, kg, sink_ref, n, hk):
    lo = hk * HEAD_DIM
    k_raw = jnp.concatenate([prev[:, lo:lo + HEAD_DIM], cur[:, 512 + lo:512 + lo + HEAD_DIM]], axis=0)
    v_band = jnp.concatenate([prev[:, 128 + lo:128 + lo + HEAD_DIM], cur[:, 640 + lo:640 + lo + HEAD_DIM]], axis=0)
    rk = _rms(k_raw)
    k_hat = k_raw * rk
    kn = (k_hat * kg).astype(bf16)
    q_raw = jnp.concatenate(
        [cur[:, (hk * Q_GROUP + g) * HEAD_DIM:(hk * Q_GROUP + g + 1) * HEAD_DIM] for g in range(Q_GROUP)], axis=0)
    rq = _rms(q_raw)
    q_hat = q_raw * rq
    qn = (q_hat * qg * (HEAD_DIM ** -0.5)).astype(bf16)
    s = _dot_nt(qn, kn)
    rows = lax.broadcasted_iota(jnp.int32, (Q_GROUP * BLOCK, 1), 0)
    g_of_row = rows // BLOCK
    qi = rows - g_of_row * BLOCK
    kj = lax.broadcasted_iota(jnp.int32, (1, 2 * BLOCK), 1)
    dist = qi + BLOCK - kj
    valid = (dist >= 0) & (dist < BLOCK) & ((kj >= BLOCK) | (n > 0))
    slope = jnp.zeros((Q_GROUP * BLOCK, 1), f32)
    sink = jnp.zeros((Q_GROUP * BLOCK, 1), f32)
    for g in range(Q_GROUP):
        head = hk * Q_GROUP + g
        slope = jnp.where(g_of_row == g, float(np.exp2(-8.0 * (head + 1.0) / 8.0)), slope)
        sink = jnp.where(g_of_row == g, sink_ref[head], sink)
    s = jnp.where(valid, s - slope * dist.astype(f32), NEG_INF)
    m = jnp.maximum(jnp.max(s, axis=-1, keepdims=True), sink)
    e = jnp.exp(s - m)
    e_sink = jnp.exp(sink - m)
    inv = 1.0 / (jnp.sum(e, axis=-1, keepdims=True) + e_sink)
    return dict(k_raw=k_raw, rk=rk, k_hat=k_hat, kn=kn, v=v_band.astype(bf16), q_hat=q_hat, rq=rq, qn=qn,
                p=e * inv, p_sink=e_sink * inv)


def _attn_fwd(qkv, qg, kg, sinks, n_seq, seq):
    T = n_seq * seq
    nb = seq // BLOCK

    def body(cur_ref, prev_ref, qg_ref, kg_ref, sink_ref, y_ref):
        n = pl.program_id(1)
        cur = cur_ref[...]
        prev = prev_ref[...]
        pieces = [None] * (N_KV_HEADS * Q_GROUP)
        for hk in range(N_KV_HEADS):
            a = _attn_head_group(cur, prev, qg_ref[...], kg_ref[...], sink_ref, n, hk)
            o = _dot_nn(a["p"].astype(bf16), a["v"])
            for g in range(Q_GROUP):
                pieces[hk * Q_GROUP + g] = o[g * BLOCK:(g + 1) * BLOCK]
        y_ref[...] = jnp.concatenate(pieces, axis=1).astype(bf16)

    return pl.pallas_call(
        body, name="attn_fwd", grid=(n_seq, nb),
        in_specs=[pl.BlockSpec((BLOCK, 768), lambda b, n: (b * nb + n, 0)),
                  pl.BlockSpec((BLOCK, 256), lambda b, n: (b * nb + jnp.maximum(n - 1, 0), 2)),
                  _full((1, HEAD_DIM)), _full((1, HEAD_DIM)),
                  pl.BlockSpec(memory_space=pltpu.SMEM)],
        out_specs=pl.BlockSpec((BLOCK, ATT_WIDTH), lambda b, n: (b * nb + n, 0)),
        out_shape=jax.ShapeDtypeStruct((T, ATT_WIDTH), bf16),
        compiler_params=_params(("parallel", "parallel")),
    )(qkv, qkv, qg, kg, sinks)


def _sgu_chunk(su, sv, gain, w_ref, b_ref):
    u = _gelu(su)
    vg = _gelu(sv)
    rv = _rms(vg)
    v_hat = vg * rv
    vn = (v_hat * gain).astype(bf16)
    causal = (lax.broadcasted_iota(jnp.int32, (BLOCK, BLOCK), 0) >= lax.broadcasted_iota(jnp.int32, (BLOCK, BLOCK), 1))
    w_tril = [jnp.where(causal, w_ref[g], 0.0).astype(bf16) for g in range(SGU_GROUPS)]
    gd = SGU_WIDTH // SGU_GROUPS
    mixed = jnp.concatenate(
        [_dot_nn(w_tril[g], vn[:, g * gd:(g + 1) * gd]) + b_ref[g] for g in range(SGU_GROUPS)], axis=1)
    return u, rv, v_hat, vn, w_tril, mixed


def _sgu_fwd(su, sv, gain, w_s, b_s, tm):
    T = su.shape[0]

    def body(su_ref, sv_ref, g_ref, w_ref, b_ref, y_ref):
        for ch in range(tm // BLOCK):
            rows = slice(ch * BLOCK, (ch + 1) * BLOCK)
            u, _, _, _, _, mixed = _sgu_chunk(su_ref[rows, :].astype(f32), sv_ref[rows, :].astype(f32),
                                              g_ref[...], w_ref, b_ref)
            y_ref[rows, :] = (u * mixed).astype(bf16)

    row = pl.BlockSpec((tm, SGU_WIDTH), lambda i: (i, 0))
    return pl.pallas_call(
        body, name="sgu_fwd", grid=(T // tm,),
        in_specs=[row, row, _full((1, SGU_WIDTH)), _full((SGU_GROUPS, BLOCK, BLOCK)), _full((SGU_GROUPS, BLOCK, 1))],
        out_specs=row, out_shape=jax.ShapeDtypeStruct((T, SGU_WIDTH), bf16),
        compiler_params=_params(("parallel",)),
    )(su, sv, gain, w_s, b_s)


def _merge_fwd(x, y_att, y_sgu, ga, gb, w_oa_t, w_ob_t, w_out, layer, tm):
    T = x.shape[0]

    def body(x_ref, ya_ref, ys_ref, ga_ref, gb_ref, woa_ref, wob_ref, wout_ref, x1_ref, m_ref, a_ref, b_ref):
        a = _dot_nt(ya_ref[...], woa_ref[...])
        b = _dot_nt(ys_ref[...], wob_ref[...])
        a_ref[...] = a.astype(bf16)
        b_ref[...] = b.astype(bf16)
        merged = (jax.nn.sigmoid(ga_ref[...].astype(f32)) * a + jax.nn.sigmoid(gb_ref[...].astype(f32)) * b).astype(bf16)
        m_ref[...] = merged
        x1_ref[...] = x_ref[...] + _dot_nn(merged, wout_ref[...])

    row = lambda w: pl.BlockSpec((tm, w), lambda i: (i, 0))
    return pl.pallas_call(
        body, name=f"merge_fwd_{layer}", grid=(T // tm,),
        in_specs=[row(D_MODEL), row(512), row(512), row(1024), row(1024),
                  _wspec(D_MODEL, ATT_WIDTH, layer), _wspec(D_MODEL, SGU_WIDTH, layer), _wspec(D_MODEL, D_MODEL, layer)],
        out_specs=[row(D_MODEL)] * 4,
        out_shape=[jax.ShapeDtypeStruct((T, D_MODEL), f32)] + [jax.ShapeDtypeStruct((T, D_MODEL), bf16)] * 3,
        compiler_params=_params(("parallel",)),
    )(x, y_att, y_sgu, ga, gb, w_oa_t, w_ob_t, w_out)


def _tile_permutation(tm):
    r = np.arange(tm)
    p = np.zeros((tm, tm), np.float32)
    p[r, (r % 8) * (tm // 8) + r // 8] = 1.0
    return jnp.asarray(p, bf16), jnp.asarray(p.T, bf16)


def _stage_taps_before(buf, zz, prev, tm):
    first = lax.broadcasted_iota(jnp.int32, (8, 1), 0) == 0
    buf[16:16 + tm, :] = zz
    buf[0:8, :] = jnp.where(first, prev[7:8], pltpu.roll(buf[tm:tm + 8, :], 1, 0))
    buf[8:16, :] = jnp.where(first, prev[15:16], pltpu.roll(buf[tm + 8:tm + 16, :], 1, 0))


def _stage_taps_after(buf, nxt, tm):
    last = lax.broadcasted_iota(jnp.int32, (8, 1), 0) == 7
    buf[tm:tm + 8, :] = jnp.where(last, nxt[0:1], pltpu.roll(buf[0:8, :], 7, 0))
    buf[tm + 8:tm + 16, :] = jnp.where(last, nxt[8:9], pltpu.roll(buf[8:16, :], 7, 0))


def _conv_rows(buf, r, n, coef):
    z2 = buf[pl.ds(r, n), :]
    z1 = buf[pl.ds(pl.multiple_of(r + 8, 8), n), :]
    z0 = buf[pl.ds(pl.multiple_of(r + 16, 8), n), :]
    return coef[0] + coef[1] * z2 + coef[2] * z1 + coef[3] * z0


def _ffn_up(x1, gain, w_up_t, conv_w, conv_b, layer, seq, tm):
    T = x1.shape[0]
    tps = seq // tm
    perm, perm_t = _tile_permutation(tm)

    rg = 16

    def body(x_ref, g_ref, w_ref, cw_ref, cb_ref, p_ref, pt_ref, h2_ref, z_ref, act_ref, carry_ref,
             zg_buf, zv_buf, actp_buf):
        i = pl.program_id(0)

        @pl.when(i % tps == 0)
        def _():
            carry_ref[...] = jnp.zeros_like(carry_ref)

        xf = x_ref[...]
        h2 = (xf * _rms(xf) * g_ref[...]).astype(bf16)
        h2_ref[...] = h2
        h2p = _dot_nn(p_ref[...], h2).astype(bf16)
        for cc in range(D_FF // FF_CHUNK):
            cols_g = slice(cc * FF_CHUNK, (cc + 1) * FF_CHUNK)
            cols_v = slice(D_FF + cc * FF_CHUNK, D_FF + (cc + 1) * FF_CHUNK)
            for buf, cols in ((zg_buf, cols_g), (zv_buf, cols_v)):
                zb = _dot_nt(h2p, w_ref[cols, :]).astype(bf16)
                z_ref[:, cols] = zb
                _stage_taps_before(buf, zb.astype(f32), carry_ref[:, cols], tm)
                carry_ref[:, cols] = buf[tm:tm + 16, :]
            coef = [jnp.broadcast_to(v, (rg, FF_CHUNK)) for cols in (cols_g, cols_v)
                    for v in (cb_ref[:, cols], cw_ref[0:1, cols], cw_ref[1:2, cols], cw_ref[2:3, cols])]

            def rows_step(j, carry, coef=coef):
                r = pl.multiple_of(j * rg, rg)
                zcg, zcv = (_conv_rows(buf, r, rg, coef[4 * k:4 * k + 4]) for k, buf in enumerate((zg_buf, zv_buf)))
                actp_buf[pl.ds(r, rg), :] = (zcg * jax.nn.sigmoid(zcg) * zcv).astype(bf16)
                return carry

            lax.fori_loop(0, tm // rg, rows_step, 0, unroll=True)
            act_ref[:, cols_g] = _dot_nn(pt_ref[...], actp_buf[...]).astype(bf16)

    row = lambda w: pl.BlockSpec((tm, w), lambda i: (i, 0))
    return pl.pallas_call(
        body, name=f"ffn_up_{layer}", grid=(T // tm,),
        in_specs=[row(D_MODEL), _full((1, D_MODEL)), _wspec(2 * D_FF, D_MODEL, layer),
                  _full((3, 2 * D_FF)), _full((1, 2 * D_FF)), _full((tm, tm)), _full((tm, tm))],
        out_specs=[row(D_MODEL), row(2 * D_FF), row(D_FF)],
        out_shape=[jax.ShapeDtypeStruct((T, D_MODEL), bf16), jax.ShapeDtypeStruct((T, 2 * D_FF), bf16),
                   jax.ShapeDtypeStruct((T, D_FF), bf16)],
        scratch_shapes=[pltpu.VMEM((16, 2 * D_FF), f32), pltpu.VMEM((tm + 16, FF_CHUNK), f32),
                        pltpu.VMEM((tm + 16, FF_CHUNK), f32), pltpu.VMEM((tm, FF_CHUNK), bf16)],
        compiler_params=_params(("arbitrary",)),
    )(x1, gain, w_up_t, conv_w, conv_b, perm, perm_t)


def _ffn_down(x1, act, w_down, layer, tm):
    T = x1.shape[0]

    def body(x_ref, a_ref, w_ref, o_ref):
        o_ref[...] = x_ref[...] + _dot_nn(a_ref[...], w_ref[...])

    row = lambda w: pl.BlockSpec((tm, w), lambda i: (i, 0))
    return pl.pallas_call(
        body, name=f"ffn_down_{layer}", grid=(T // tm,),
        in_specs=[row(D_MODEL), row(D_FF), _wspec(D_FF, D_MODEL, layer)],
        out_specs=row(D_MODEL), out_shape=jax.ShapeDtypeStruct((T, D_MODEL), f32),
        compiler_params=_params(("parallel",)),
    )(x1, act, w_down)


def _loss_head(y, target, tm):
    T = y.shape[0]

    def body(y_ref, t_ref, dy_ref, dyb_ref, loss_ref):
        @pl.when(pl.program_id(0) == 0)
        def _():
            loss_ref[...] = jnp.zeros_like(loss_ref)

        diff = y_ref[...] - t_ref[...]
        loss_ref[...] += 0.5 * jnp.sum(jnp.mean(diff * diff, axis=-1, keepdims=True), axis=0, keepdims=True)
        dy = diff * (1.0 / D_MODEL)
        dy_ref[...] = dy
        dyb_ref[...] = dy.astype(bf16)

    row = pl.BlockSpec((tm, D_MODEL), lambda i: (i, 0))
    return pl.pallas_call(
        body, name="loss_head", grid=(T // tm,),
        in_specs=[row, row], out_specs=[row, row, _full((8, 128))],
        out_shape=[jax.ShapeDtypeStruct((T, D_MODEL), f32), jax.ShapeDtypeStruct((T, D_MODEL), bf16),
                   jax.ShapeDtypeStruct((8, 128), f32)],
        compiler_params=_params(("arbitrary",)),
    )(y, target)


def _ffn_bwd(dx2b, z, conv_w, conv_b, w_down, layer, seq, tm):
    T = z.shape[0]
    nt = T // tm
    tps = seq // tm

    perm, perm_t = _tile_permutation(tm)

    def body(dx_ref, z_ref, zh_ref, cw_ref, cb_ref, wd_ref, p_ref, pt_ref, dz_ref, dconv_ref, carry_ref,
             zg_buf, zv_buf, gg_buf, gv_buf, dact_buf, dzp_buf):
        i = pl.program_id(0)
        pos = (nt - 1 - i) % tps

        @pl.when(i == 0)
        def _():
            dconv_ref[...] = jnp.zeros_like(dconv_ref)

        @pl.when(pos == tps - 1)
        def _():
            carry_ref[...] = jnp.zeros_like(carry_ref)

        dxp = _dot_nn(p_ref[...], dx_ref[...]).astype(bf16)
        halo_on = (pos > 0).astype(f32)
        for cc in range(D_FF // FF_CHUNK):
            cols_g = slice(cc * FF_CHUNK, (cc + 1) * FF_CHUNK)
            cols_v = slice(D_FF + cc * FF_CHUNK, D_FF + (cc + 1) * FF_CHUNK)
            for buf, cols in ((zg_buf, cols_g), (zv_buf, cols_v)):
                _stage_taps_before(buf, z_ref[:, cols].astype(f32), zh_ref[:, cols].astype(f32) * halo_on, tm)
            dact_buf[...] = _dot_nt(dxp, wd_ref[cols_g, :])
            coef = [jnp.broadcast_to(v, (8, FF_CHUNK)) for cols in (cols_g, cols_v)
                    for v in (cb_ref[:, cols], cw_ref[0:1, cols], cw_ref[1:2, cols], cw_ref[2:3, cols])]

            def first_pass(j, sums, coef=coef):
                r = pl.multiple_of(j * 8, 8)
                rows = pl.ds(r, 8)
                zcg = _conv_rows(zg_buf, r, 8, coef[0:4])
                zcv = _conv_rows(zv_buf, r, 8, coef[4:8])
                sg = jax.nn.sigmoid(zcg)
                silu = zcg * sg
                d_act = dact_buf[rows, :]
                dg = d_act * zcv * sg * (1.0 + zcg * (1.0 - sg))
                dv = d_act * silu
                gg_buf[rows, :] = dg
                gv_buf[rows, :] = dv
                out = []
                for k, (g, buf) in enumerate(((dg, zg_buf), (dv, zv_buf))):
                    out += [sums[4 * k] + g * buf[rows, :],
                            sums[4 * k + 1] + g * buf[pl.ds(pl.multiple_of(r + 8, 8), 8), :],
                            sums[4 * k + 2] + g * buf[pl.ds(pl.multiple_of(r + 16, 8), 8), :],
                            sums[4 * k + 3] + g]
                return tuple(out)

            sums = lax.fori_loop(0, tm // 8, first_pass, tuple(jnp.zeros((8, FF_CHUNK), f32) for _ in range(8)),
                                 unroll=True)
            for k, cols in enumerate((cols_g, cols_v)):
                for tap in range(4):
                    dconv_ref[tap:tap + 1, cols] += jnp.sum(sums[4 * k + tap], axis=0, keepdims=True)
            for buf, cols in ((gg_buf, cols_g), (gv_buf, cols_v)):
                _stage_taps_after(buf, carry_ref[:, cols], tm)
                carry_ref[:, cols] = buf[0:16, :]
                w0, w1, w2 = (jnp.broadcast_to(cw_ref[k:k + 1, cols], (16, FF_CHUNK)) for k in range(3))

                def second_pass(j, carry, buf=buf, w0=w0, w1=w1, w2=w2):
                    r = pl.multiple_of(j * 16, 16)
                    dzp_buf[pl.ds(r, 16), :] = (w2 * buf[pl.ds(r, 16), :] + w1 * buf[pl.ds(pl.multiple_of(r + 8, 8), 16), :]
                                                + w0 * buf[pl.ds(pl.multiple_of(r + 16, 16), 16), :]).astype(bf16)
                    return carry

                lax.fori_loop(0, tm // 16, second_pass, 0, unroll=True)
                dz_ref[:, cols] = _dot_nn(pt_ref[...], dzp_buf[...]).astype(bf16)

    rev = lambda w: pl.BlockSpec((tm, w), lambda i: (nt - 1 - i, 0))
    return pl.pallas_call(
        body, name=f"ffn_bwd_{layer}", grid=(nt,),
        in_specs=[rev(D_MODEL), rev(2 * D_FF),
                  pl.BlockSpec((16, 2 * D_FF), lambda i: (jnp.maximum((nt - 1 - i) * (tm // 16) - 1, 0), 0)),
                  _full((3, 2 * D_FF)), _full((1, 2 * D_FF)), _wspec(D_FF, D_MODEL, layer),
                  _full((tm, tm)), _full((tm, tm))],
        out_specs=[rev(2 * D_FF), _full((8, 2 * D_FF))],
        out_shape=[jax.ShapeDtypeStruct((T, 2 * D_FF), bf16), jax.ShapeDtypeStruct((8, 2 * D_FF), f32)],
        scratch_shapes=[pltpu.VMEM((16, 2 * D_FF), f32)] + [pltpu.VMEM((tm + 16, FF_CHUNK), f32)] * 4
        + [pltpu.VMEM((tm, FF_CHUNK), f32), pltpu.VMEM((tm, FF_CHUNK), bf16)],
        compiler_params=_params(("arbitrary",)),
    )(dx2b, z, z, conv_w, conv_b, w_down, perm, perm_t)


def _norm_bwd(dys, w, layer, x, gain, dres, tm, name):
    T = dys[0].shape[0]
    widths = [d.shape[1] for d in dys]
    K = sum(widths)
    n = len(dys)

    def body(*refs):
        dy_refs = refs[:n]
        w_ref, x_ref, g_ref, dres_ref, dx_ref, dxb_ref, dg_ref = refs[n:]

        @pl.when(pl.program_id(0) == 0)
        def _():
            dg_ref[...] = jnp.zeros_like(dg_ref)

        dh, lo = None, 0
        for dy_ref, wd in zip(dy_refs, widths):
            part = _dot_nn(dy_ref[...], w_ref[lo:lo + wd, :])
            dh = part if dh is None else dh + part
            lo += wd
        xf = x_ref[...]
        r = _rms(xf)
        x_hat = xf * r
        dg_ref[...] += jnp.sum(dh * x_hat, axis=0, keepdims=True)
        dxh = dh * g_ref[...]
        dx = dres_ref[...] + r * (dxh - x_hat * jnp.mean(dxh * x_hat, axis=-1, keepdims=True))
        dx_ref[...] = dx
        dxb_ref[...] = dx.astype(bf16)

    row = lambda w_: pl.BlockSpec((tm, w_), lambda i: (i, 0))
    return pl.pallas_call(
        body, name=name, grid=(T // tm,),
        in_specs=[row(wd) for wd in widths] + [_wspec(K, D_MODEL, layer), row(D_MODEL), _full((1, D_MODEL)), row(D_MODEL)],
        out_specs=[row(D_MODEL), row(D_MODEL), _full((1, D_MODEL))],
        out_shape=[jax.ShapeDtypeStruct((T, D_MODEL), f32), jax.ShapeDtypeStruct((T, D_MODEL), bf16),
                   jax.ShapeDtypeStruct((1, D_MODEL), f32)],
        compiler_params=_params(("arbitrary",)),
    )(*dys, w, x, gain, dres)


def _merge_bwd(dx1b, ga, gb, a, b, w_oa_t, w_ob_t, w_out, layer, tm):
    T = dx1b.shape[0]

    def body(dx_ref, ga_ref, gb_ref, a_ref, b_ref, woa_ref, wob_ref, wout_ref,
             da_ref, db_ref, dga_ref, dgb_ref, dya_ref, dys_ref):
        dm = _dot_nt(dx_ref[...], wout_ref[...])
        sa = jax.nn.sigmoid(ga_ref[...].astype(f32))
        sb = jax.nn.sigmoid(gb_ref[...].astype(f32))
        da = (dm * sa).astype(bf16)
        db = (dm * sb).astype(bf16)
        da_ref[...] = da
        db_ref[...] = db
        dga_ref[...] = (dm * a_ref[...].astype(f32) * sa * (1.0 - sa)).astype(bf16)
        dgb_ref[...] = (dm * b_ref[...].astype(f32) * sb * (1.0 - sb)).astype(bf16)
        dya_ref[...] = _dot_nn(da, woa_ref[...]).astype(bf16)
        dys_ref[...] = _dot_nn(db, wob_ref[...]).astype(bf16)

    row = lambda w: pl.BlockSpec((tm, w), lambda i: (i, 0))
    return pl.pallas_call(
        body, name=f"merge_bwd_{layer}", grid=(T // tm,),
        in_specs=[row(D_MODEL)] * 5 + [_wspec(D_MODEL, ATT_WIDTH, layer), _wspec(D_MODEL, SGU_WIDTH, layer),
                                       _wspec(D_MODEL, D_MODEL, layer)],
        out_specs=[row(D_MODEL)] * 4 + [row(512)] * 2,
        out_shape=[jax.ShapeDtypeStruct((T, D_MODEL), bf16)] * 4 + [jax.ShapeDtypeStruct((T, 512), bf16)] * 2,
        compiler_params=_params(("parallel",)),
    )(dx1b, ga, gb, a, b, w_oa_t, w_ob_t, w_out)


def _sgu_bwd(dy, su, sv, gain, w_s, b_s, tm):
    T = su.shape[0]
    gd = SGU_WIDTH // SGU_GROUPS

    def body(dy_ref, su_ref, sv_ref, g_ref, w_ref, b_ref, dsu_ref, dsv_ref, dw_ref, db_ref, dg_ref):
        @pl.when(pl.program_id(0) == 0)
        def _():
            dw_ref[...] = jnp.zeros_like(dw_ref)
            db_ref[...] = jnp.zeros_like(db_ref)
            dg_ref[...] = jnp.zeros_like(dg_ref)

        gain_v = g_ref[...]
        for ch in range(tm // BLOCK):
            rows = slice(ch * BLOCK, (ch + 1) * BLOCK)
            su_c = su_ref[rows, :].astype(f32)
            sv_c = sv_ref[rows, :].astype(f32)
            u, rv, v_hat, vn, w_tril, mixed = _sgu_chunk(su_c, sv_c, gain_v, w_ref, b_ref)
            dyc = dy_ref[rows, :].astype(f32)
            dsu_ref[rows, :] = (dyc * mixed * _gelu_grad(su_c)).astype(bf16)
            dmix = dyc * u
            dmix_b = dmix.astype(bf16)
            dvn = []
            for g in range(SGU_GROUPS):
                gs = slice(g * gd, (g + 1) * gd)
                db_ref[g] += jnp.sum(dmix[:, gs], axis=1, keepdims=True)
                dw_ref[g] += _dot_nt(dmix_b[:, gs], vn[:, gs])
                dvn.append(_dot_tn(w_tril[g], dmix_b[:, gs]))
            dvn = jnp.concatenate(dvn, axis=1)
            dg_ref[...] += jnp.sum(dvn * v_hat, axis=0, keepdims=True)
            dxh = dvn * gain_v
            dvg = rv * (dxh - v_hat * jnp.mean(dxh * v_hat, axis=-1, keepdims=True))
            dsv_ref[rows, :] = (dvg * _gelu_grad(sv_c)).astype(bf16)

    row = pl.BlockSpec((tm, SGU_WIDTH), lambda i: (i, 0))
    return pl.pallas_call(
        body, name="sgu_bwd", grid=(T // tm,),
        in_specs=[row, row, row, _full((1, SGU_WIDTH)), _full((SGU_GROUPS, BLOCK, BLOCK)),
                  _full((SGU_GROUPS, BLOCK, 1))],
        out_specs=[row, row, _full((SGU_GROUPS, BLOCK, BLOCK)), _full((SGU_GROUPS, BLOCK, 1)), _full((1, SGU_WIDTH))],
        out_shape=[jax.ShapeDtypeStruct((T, SGU_WIDTH), bf16)] * 2 + [
            jax.ShapeDtypeStruct((SGU_GROUPS, BLOCK, BLOCK), f32), jax.ShapeDtypeStruct((SGU_GROUPS, BLOCK, 1), f32),
            jax.ShapeDtypeStruct((1, SGU_WIDTH), f32)],
        compiler_params=_params(("arbitrary",)),
    )(dy, su, sv, gain, w_s, b_s)


def _attn_bwd(dy, qkv, qg, kg, sinks, n_seq, seq):
    T = n_seq * seq
    nb = seq // BLOCK
    scale = HEAD_DIM ** -0.5

    def body(dy_ref, cur_ref, prev_ref, qg_ref, kg_ref, sink_ref, dqkv_ref, dqg_ref, dkg_ref, dsink_ref,
             carry_k, carry_v):
        b = pl.program_id(0)
        j = pl.program_id(1)
        n = nb - 1 - j

        @pl.when((b == 0) & (j == 0))
        def _():
            dqg_ref[...] = jnp.zeros_like(dqg_ref)
            dkg_ref[...] = jnp.zeros_like(dkg_ref)
            dsink_ref[...] = jnp.zeros_like(dsink_ref)

        @pl.when(j == 0)
        def _():
            carry_k[...] = jnp.zeros_like(carry_k)
            carry_v[...] = jnp.zeros_like(carry_v)

        cur = cur_ref[...]
        prev = prev_ref[...]
        dyf = dy_ref[...].astype(f32)
        qg_v = qg_ref[...]
        kg_v = kg_ref[...]
        dq_pieces = [None] * (N_KV_HEADS * Q_GROUP)
        dk_pieces, dv_pieces = [], []
        for hk in range(N_KV_HEADS):
            a = _attn_head_group(cur, prev, qg_v, kg_v, sink_ref, n, hk)
            do = jnp.concatenate(
                [dyf[:, (hk * Q_GROUP + g) * HEAD_DIM:(hk * Q_GROUP + g + 1) * HEAD_DIM] for g in range(Q_GROUP)],
                axis=0).astype(bf16)
            p = a["p"]
            dp = _dot_nt(do, a["v"])
            dv_band = _dot_tn(p.astype(bf16), do)
            dsum = jnp.sum(p * dp, axis=-1, keepdims=True)
            ds = (p * (dp - dsum)).astype(bf16)
            dsink_col = -a["p_sink"] * dsum
            for g in range(Q_GROUP):
                head = hk * Q_GROUP + g
                dsink_ref[head:head + 1, :] += jnp.sum(dsink_col[g * BLOCK:(g + 1) * BLOCK], axis=0, keepdims=True)
            dqn = _dot_nn(ds, a["kn"])
            dkn_band = _dot_tn(ds, a["qn"])
            dq_hat_g = dqn * scale
            dqg_ref[...] += jnp.sum(dq_hat_g * a["q_hat"], axis=0, keepdims=True)
            dxh = dq_hat_g * qg_v
            dq = a["rq"] * (dxh - a["q_hat"] * jnp.mean(dxh * a["q_hat"], axis=-1, keepdims=True))
            for g in range(Q_GROUP):
                dq_pieces[hk * Q_GROUP + g] = dq[g * BLOCK:(g + 1) * BLOCK]
            dkn = dkn_band[BLOCK:] + carry_k[hk]
            dv_pieces.append(dv_band[BLOCK:] + carry_v[hk])
            carry_k[hk] = dkn_band[:BLOCK]
            carry_v[hk] = dv_band[:BLOCK]
            k_hat = a["k_hat"][BLOCK:]
            dkg_ref[...] += jnp.sum(dkn * k_hat, axis=0, keepdims=True)
            dxk = dkn * kg_v
            dk_pieces.append(a["rk"][BLOCK:] * (dxk - k_hat * jnp.mean(dxk * k_hat, axis=-1, keepdims=True)))
        dqkv_ref[...] = jnp.concatenate(dq_pieces + dk_pieces + dv_pieces, axis=1).astype(bf16)

    blk = lambda w: pl.BlockSpec((BLOCK, w), lambda b, j: (b * nb + nb - 1 - j, 0))
    return pl.pallas_call(
        body, name="attn_bwd", grid=(n_seq, nb),
        in_specs=[blk(ATT_WIDTH), blk(768),
                  pl.BlockSpec((BLOCK, 256), lambda b, j: (b * nb + jnp.maximum(nb - 2 - j, 0), 2)),
                  _full((1, HEAD_DIM)), _full((1, HEAD_DIM)), pl.BlockSpec(memory_space=pltpu.SMEM)],
        out_specs=[blk(768), _full((1, HEAD_DIM)), _full((1, HEAD_DIM)), _full((8, 128))],
        out_shape=[jax.ShapeDtypeStruct((T, 768), bf16), jax.ShapeDtypeStruct((1, HEAD_DIM), f32),
                   jax.ShapeDtypeStruct((1, HEAD_DIM), f32), jax.ShapeDtypeStruct((8, 128), f32)],
        scratch_shapes=[pltpu.VMEM((N_KV_HEADS, BLOCK, HEAD_DIM), f32), pltpu.VMEM((N_KV_HEADS, BLOCK, HEAD_DIM), f32)],
        compiler_params=_params(("arbitrary", "arbitrary")),
    )(dy, qkv, qkv, qg, kg, sinks)


def _weight_grad(a, b, tm, tk, name):
    T, M = a.shape
    N = b.shape[1]
    nk = T // tk

    def body(a_ref, b_ref, o_ref, acc_ref):
        k = pl.program_id(1)

        @pl.when(k == 0)
        def _():
            acc_ref[...] = jnp.zeros_like(acc_ref)

        acc_ref[...] += _dot_tn(a_ref[...], b_ref[...])

        @pl.when(k == nk - 1)
        def _():
            o_ref[...] = acc_ref[...].astype(bf16)

    return pl.pallas_call(
        body, name=name, grid=(M // tm, nk),
        in_specs=[pl.BlockSpec((tk, tm), lambda i, k: (k, i)), pl.BlockSpec((tk, N), lambda i, k: (k, 0))],
        out_specs=pl.BlockSpec((None, tm, N), lambda i, k: (0, i, 0)),
        out_shape=jax.ShapeDtypeStruct((1, M, N), bf16),
        scratch_shapes=[pltpu.VMEM((tm, N), f32)],
        compiler_params=_params(("parallel", "arbitrary")),
    )(a, b)


def _weight_grad_rows(a_list, b, tk, name):
    T, N = b.shape
    widths = [a.shape[1] for a in a_list]
    M = sum(widths)
    nk = T // tk
    n = len(a_list)

    def body(*refs):
        a_refs = refs[:n]
        b_ref, o_ref, acc_ref = refs[n:]
        k = pl.program_id(0)

        @pl.when(k == 0)
        def _():
            acc_ref[...] = jnp.zeros_like(acc_ref)

        lo = 0
        for a_ref, wd in zip(a_refs, widths):
            acc_ref[lo:lo + wd, :] += _dot_tn(a_ref[...], b_ref[...])
            lo += wd

        @pl.when(k == nk - 1)
        def _():
            o_ref[...] = acc_ref[...].astype(bf16)

    return pl.pallas_call(
        body, name=name, grid=(nk,),
        in_specs=[pl.BlockSpec((tk, wd), lambda k: (k, 0)) for wd in widths] + [pl.BlockSpec((tk, N), lambda k: (k, 0))],
        out_specs=pl.BlockSpec((None, M, N), lambda k: (0, 0, 0)),
        out_shape=jax.ShapeDtypeStruct((1, M, N), bf16),
        scratch_shapes=[pltpu.VMEM((M, N), f32)],
        compiler_params=_params(("arbitrary",)),
    )(*a_list, b)


def _place(src, layer, src_slot, n_slots, dst_slot, dtype, name):
    _, _, rows, cols = src.shape
    slots = jnp.stack([src_slot, dst_slot]).astype(jnp.int32)

    def body(slots_ref, s_ref, o_ref):
        o_ref[...] = s_ref[...].astype(dtype)

    return pl.pallas_call(
        body, name=name,
        grid_spec=pltpu.PrefetchScalarGridSpec(
            num_scalar_prefetch=1, grid=(1,),
            in_specs=[pl.BlockSpec((None, None, rows, cols), lambda i, sl: (layer, sl[0], 0, 0))],
            out_specs=pl.BlockSpec((None, rows, cols), lambda i, sl: (sl[1], 0, 0))),
        out_shape=jax.ShapeDtypeStruct((n_slots, rows, cols), dtype),
        compiler_params=_params(("arbitrary",)),
    )(slots, src)


HBM = pl.BlockSpec(memory_space=pltpu.HBM)
SEM = pl.BlockSpec(memory_space=pltpu.SEMAPHORE)
DATAFLOW = pltpu.SideEffectType.DATAFLOW_SIDE_EFFECTING


def _other_chips(x, y):
    return [(1 - x, y), (x, 1 - y), (1 - x, 1 - y)]


def _split_start(groups, name):
    nb = [len(bufs) for bufs, _ in groups]
    flat = [b for bufs, _ in groups for b in bufs]
    ns = [len(plan(bufs, dry=True)) for bufs, plan in groups]
    ng = len(groups)

    def body(*refs):
        n_in = len(flat)
        sems = refs[n_in:n_in + 2 * ng]
        thru = refs[n_in + 2 * ng:2 * n_in + 2 * ng]
        token = refs[2 * n_in + 2 * ng]
        off = 0
        for g, (bufs, plan) in enumerate(groups):
            mine = thru[off:off + nb[g]]
            off += nb[g]
            for k, (src, dst, to) in enumerate(plan(mine)):
                pltpu.make_async_remote_copy(
                    src_ref=src, dst_ref=dst, send_sem=sems[2 * g].at[k], recv_sem=sems[2 * g + 1].at[k],
                    device_id=to, device_id_type=MESH).start()
        token[...] = jnp.zeros_like(token)

    out_shape = []
    for n in ns:
        out_shape += [pltpu.SemaphoreType.DMA((n,)), pltpu.SemaphoreType.DMA((n,))]
    out_shape += [pltpu.HBM(b.shape, b.dtype) for b in flat]
    out_shape.append(jax.ShapeDtypeStruct((8, 128), f32))
    res = pl.pallas_call(
        body, name=name, out_shape=tuple(out_shape),
        in_specs=[HBM] * len(flat),
        out_specs=tuple([SEM] * (2 * ng) + [HBM] * len(flat) + [pl.BlockSpec(memory_space=pltpu.VMEM)]),
        input_output_aliases={i: 2 * ng + i for i in range(len(flat))},
        compiler_params=pltpu.CompilerParams(has_side_effects=DATAFLOW),
    )(*[pltpu.with_memory_space_constraint(b, pltpu.HBM) for b in flat])
    out, off = [], 2 * ng
    for g in range(ng):
        out.append((res[2 * g], res[2 * g + 1], list(res[off:off + nb[g]])))
        off += nb[g]
    return out, res[-1]


def _split_wait(bufs, send, recv, plan, after, name):
    nb = len(bufs)

    def body(*refs):
        thru = refs[:nb]
        send_ref, recv_ref = refs[nb], refs[nb + 1]
        for k, (src, dst, to) in enumerate(plan(thru)):
            cp = pltpu.make_async_remote_copy(
                src_ref=src, dst_ref=dst, send_sem=send_ref.at[k], recv_sem=recv_ref.at[k],
                device_id=to, device_id_type=MESH)
            cp.wait_send()
            cp.wait_recv()

    res = pl.pallas_call(
        body, name=name, out_shape=tuple(pltpu.HBM(b.shape, b.dtype) for b in bufs),
        in_specs=[HBM] * nb + [SEM, SEM, ANY], out_specs=tuple([HBM] * nb),
        input_output_aliases={i: i for i in range(nb)},
        compiler_params=pltpu.CompilerParams(has_side_effects=DATAFLOW),
    )(*bufs, send, recv, after)
    return list(res)


def _gather_plan(hrs, n_direct=0):
    def plan(refs, dry=False):
        if dry:
            return [None] * (4 * len(hrs) + 3 * n_direct)
        x, y, c = _mesh_pos()
        me = 4 * x + 2 * y + c
        out = []
        for i in range(n_direct):
            src, land = refs[len(hrs) + 2 * i], refs[len(hrs) + 2 * i + 1]
            out += [(src, land.at[2 * x + y], (*chip, c)) for chip in _other_chips(x, y)]
        for ref, hr in zip(refs, hrs):
            rows = ref.at[pl.ds(pl.multiple_of(me * hr, 16), hr), :]
            out.append((rows, rows, (x, y, 1 - c)))
            out += [(rows, rows, (*chip, c)) for chip in _other_chips(x, y)]
        return out
    return plan


def _pass_plan(hrs):
    def plan(refs, dry=False):
        if dry:
            return [None] * (3 * len(hrs))
        x, y, c = _mesh_pos()
        out = []
        for ref, hr in zip(refs, hrs):
            for chip in _other_chips(x, y):
                rows = ref.at[pl.ds(pl.multiple_of((4 * chip[0] + 2 * chip[1] + c) * hr, 16), hr), :]
                out.append((rows, rows, (x, y, 1 - c)))
        return out
    return plan


def _pair_plan(hrs):
    n = len(hrs)

    def plan(refs, dry=False):
        if dry:
            return [None] * (N_CHIP * n)
        x, y, c = _mesh_pos()
        out = []
        for r in range(n):
            for j in range(N_CHIP):
                start = pl.multiple_of((2 * j + 1 - c) * hrs[r], 16)
                out.append((refs[r].at[0, pl.ds(start, hrs[r]), :], refs[n + r].at[0, j], (x, y, 1 - c)))
        return out
    return plan


def _all_to_all_plan(n):
    def plan(refs, dry=False):
        if dry:
            return [None] * (7 * n)
        x, y, c = _mesh_pos()
        out = []
        for ref in refs:
            mine = ref.at[4 * x + 2 * y + c]
            for fx in range(2):
                for fy in range(2):
                    for fc in range(2):
                        if fx or fy or fc:
                            out.append((mine, mine, (1 - x if fx else x, 1 - y if fy else y, 1 - c if fc else c)))
        return out
    return plan


def _pass_to_sibling(bufs, hrs, name):
    nb = len(bufs)

    def body(*refs):
        out = refs[nb:2 * nb]
        send, recv = refs[2 * nb:]
        x, y, c = _mesh_pos()
        chips = _other_chips(x, y)
        started = []
        for i in range(nb):
            for j, chip in enumerate(chips):
                rows = out[i].at[pl.ds(pl.multiple_of((4 * chip[0] + 2 * chip[1] + c) * hrs[i], 16), hrs[i]), :]
                cp = pltpu.make_async_remote_copy(
                    src_ref=rows, dst_ref=rows, send_sem=send.at[3 * i + j], recv_sem=recv.at[3 * i + j],
                    device_id=(x, y, 1 - c), device_id_type=MESH)
                cp.start()
                started.append(cp)
        for i in range(nb):
            for j, chip in enumerate(chips):
                rows = out[i].at[pl.ds(pl.multiple_of((4 * chip[0] + 2 * chip[1] + 1 - c) * hrs[i], 16), hrs[i]), :]
                pltpu.make_async_remote_copy(
                    src_ref=rows, dst_ref=rows, send_sem=send.at[3 * i + j], recv_sem=recv.at[3 * i + j],
                    device_id=(x, y, 1 - c), device_id_type=MESH).wait_recv()
        for cp in started:
            cp.wait_send()

    return list(pl.pallas_call(
        body, name=name, in_specs=[ANY] * nb, out_specs=[ANY] * nb,
        out_shape=[jax.ShapeDtypeStruct(b.shape, b.dtype) for b in bufs],
        input_output_aliases={i: i for i in range(nb)},
        scratch_shapes=[pltpu.SemaphoreType.DMA((3 * nb,)), pltpu.SemaphoreType.DMA((3 * nb,))],
        compiler_params=pltpu.CompilerParams(has_side_effects=True),
    )(*bufs))


def _pair_exchange(grads, name):
    nr = len(grads)
    n_l = grads[0].shape[0]
    n_sem = nr * n_l * N_CHIP

    def body(*refs):
        src = refs[:nr]
        out = refs[nr:2 * nr]
        send, recv = refs[2 * nr:]
        x, y, c = _mesh_pos()
        copies = []
        for r in range(nr):
            hr = grads[r].shape[1] // N_DEV
            for layer in range(n_l):
                for j in range(N_CHIP):
                    idx = (r * n_l + layer) * N_CHIP + j
                    start = pl.multiple_of((2 * j + 1 - c) * hr, 16)
                    cp = pltpu.make_async_remote_copy(
                        src_ref=src[r].at[layer, pl.ds(start, hr), :], dst_ref=out[r].at[layer, j],
                        send_sem=send.at[idx], recv_sem=recv.at[idx], device_id=(x, y, 1 - c), device_id_type=MESH)
                    cp.start()
                    copies.append(cp)
        for cp in copies:
            cp.wait()

    return pl.pallas_call(
        body, name=name,
        in_specs=[ANY] * nr, out_specs=[ANY] * nr,
        out_shape=[jax.ShapeDtypeStruct((n_l, N_CHIP, g.shape[1] // N_DEV, g.shape[2]), bf16) for g in grads],
        scratch_shapes=[pltpu.SemaphoreType.DMA((n_sem,)), pltpu.SemaphoreType.DMA((n_sem,))],
        compiler_params=pltpu.CompilerParams(has_side_effects=True),
    )(*grads)


def _pair_sum(grad, other, core, chip, name):
    n_l, rows, cols = grad.shape
    hr = rows // N_DEV
    g5 = grad.reshape(n_l, N_CHIP, 2, hr, cols)
    where = jnp.stack([core, chip]).astype(jnp.int32)

    def body(where_ref, g_ref, o_ref, s_ref, mine_ref):
        s_ref[...] = (g_ref[...].astype(f32) + o_ref[...].astype(f32)).astype(bf16)
        mine_ref[...] = s_ref[where_ref[1]]

    return pl.pallas_call(
        body, name=name,
        grid_spec=pltpu.PrefetchScalarGridSpec(
            num_scalar_prefetch=1, grid=(n_l,),
            in_specs=[pl.BlockSpec((None, N_CHIP, None, hr, cols), lambda l, w: (l, 0, w[0], 0, 0)),
                      pl.BlockSpec((None, N_CHIP, hr, cols), lambda l, w: (l, 0, 0, 0))],
            out_specs=[pl.BlockSpec((None, N_CHIP, hr, cols), lambda l, w: (l, 0, 0, 0)),
                       pl.BlockSpec((None, None, hr, cols), lambda l, w: (l, w[1], 0, 0))]),
        out_shape=[jax.ShapeDtypeStruct((n_l, N_CHIP, hr, cols), bf16)] * 2,
        compiler_params=_params(("arbitrary",)),
    )(where, g5, other)


def _chip_plan(nr, n_l):
    def plan(refs, dry=False):
        if dry:
            return [None] * (nr * n_l * 3)
        x, y, c = _mesh_pos()
        out = []
        for r in range(nr):
            for layer in range(n_l):
                for chip in _other_chips(x, y):
                    out.append((refs[r].at[layer, 2 * chip[0] + chip[1]], refs[nr + r].at[layer, 2 * x + y], (*chip, c)))
        return out
    return plan


def _chip_sum(parts, core, name):
    n_l, _, hr, cols = parts.shape

    def body(core_ref, p_ref, o_ref):
        acc = p_ref[0].astype(f32) + p_ref[1].astype(f32)
        acc = acc + p_ref[2].astype(f32)
        o_ref[...] = acc + p_ref[3].astype(f32)

    return pl.pallas_call(
        body, name=name,
        grid_spec=pltpu.PrefetchScalarGridSpec(
            num_scalar_prefetch=1, grid=(n_l,),
            in_specs=[pl.BlockSpec((None, N_CHIP, hr, cols), lambda l, cr: (l, 0, 0, 0))],
            out_specs=pl.BlockSpec((None, None, hr, cols), lambda l, cr: (l, cr[0], 0, 0))),
        out_shape=jax.ShapeDtypeStruct((n_l, 2, hr, cols), f32),
        compiler_params=_params(("arbitrary",)),
    )(core, parts)


def _share_halves(halves):
    nr = len(halves)

    def body(*refs):
        out = refs[nr:2 * nr]
        send, recv = refs[2 * nr:]
        x, y, c = _mesh_pos()
        copies = []
        for r in range(nr):
            cp = pltpu.make_async_remote_copy(
                src_ref=out[r].at[0, c], dst_ref=out[r].at[0, c], send_sem=send.at[r],
                recv_sem=recv.at[r], device_id=(x, y, 1 - c), device_id_type=MESH)
            cp.start()
            copies.append(cp)
        for r in range(nr):
            copies[r].wait_send()
            pltpu.make_async_remote_copy(
                src_ref=out[r].at[0, 1 - c], dst_ref=out[r].at[0, 1 - c], send_sem=send.at[r],
                recv_sem=recv.at[r], device_id=(x, y, 1 - c), device_id_type=MESH).wait_recv()

    return pl.pallas_call(
        body, name="grad_share_halves",
        in_specs=[ANY] * nr, out_specs=[ANY] * nr,
        out_shape=[jax.ShapeDtypeStruct(h.shape, h.dtype) for h in halves],
        input_output_aliases={r: r for r in range(nr)},
        scratch_shapes=[pltpu.SemaphoreType.DMA((nr,))] * 2,
        compiler_params=pltpu.CompilerParams(has_side_effects=True),
    )(*halves)


def _sum_small(parts, name):
    n, rows, cols = parts.shape

    def body(p_ref, o_ref):
        acc = p_ref[0].astype(f32)
        for d in range(1, n):
            acc = acc + p_ref[d].astype(f32)
        o_ref[...] = acc

    return pl.pallas_call(
        body, name=name, grid=(rows // 16,),
        in_specs=[pl.BlockSpec((n, 16, cols), lambda i: (0, i, 0))], out_specs=pl.BlockSpec((16, cols), lambda i: (i, 0)),
        out_shape=jax.ShapeDtypeStruct((rows, cols), f32),
        compiler_params=_params(("parallel",)),
    )(parts)


def _adamw(w, g, m, v, name):
    n_l, rows, cols = w.shape
    budget = 42 * 1024 * 1024
    tr = next(rows // d for d in range(1, rows + 1)
              if rows % d == 0 and (rows // d) % 8 == 0 and (rows // d) * cols * 4 * 14 <= budget)

    def body(w_ref, g_ref, m_ref, v_ref, d_ref, nm_ref, nv_ref):
        gg = g_ref[...]
        nm = ADAM_B1 * m_ref[...] + (1.0 - ADAM_B1) * gg
        nv = ADAM_B2 * v_ref[...] + (1.0 - ADAM_B2) * (gg * gg)
        m_hat = nm / (1.0 - ADAM_B1 ** ADAM_STEP)
        v_hat = nv / (1.0 - ADAM_B2 ** ADAM_STEP)
        d_ref[...] = -ADAM_LR * (m_hat / (jnp.sqrt(v_hat) + ADAM_EPS) + ADAM_WD * w_ref[...])
        nm_ref[...] = nm
        nv_ref[...] = nv

    blk = pl.BlockSpec((None, tr, cols), lambda l, i: (l, i, 0))
    return pl.pallas_call(
        body, name=name, grid=(n_l, rows // tr),
        in_specs=[blk] * 4, out_specs=[blk] * 3, out_shape=[jax.ShapeDtypeStruct((n_l, rows, cols), f32)] * 3,
        compiler_params=_params(("parallel", "parallel")),
    )(w, g, m, v)


SMALL = ("mix_norm", "q_norm", "k_norm", "sinks", "sgu_norm", "w_s", "b_s", "ffn_norm", "conv_b", "conv_w")


def _pack_small(arrs):
    flat = jnp.concatenate([a.reshape(-1) for a in arrs])
    pad = (-flat.shape[0]) % (16 * 1024)
    return jnp.pad(flat, (0, pad)).reshape(-1, 1024)


def _unpack_small(pack, shapes):
    flat = pack.reshape(-1)
    out, off = [], 0
    for s in shapes:
        n = int(np.prod(s))
        out.append(flat[off:off + n].reshape(s))
        off += n
    return out


def kernel(x, mix_norm, w_in, q_norm, k_norm, sinks, sgu_norm, w_s, b_s, w_oa, w_ob, w_out, ffn_norm, w_up, conv_w, conv_b, w_down, loss_target, m_mix_norm, m_w_in, m_q_norm, m_k_norm, m_sinks, m_sgu_norm, m_w_s, m_b_s, m_w_oa, m_w_ob, m_w_out, m_ffn_norm, m_w_up, m_conv_w, m_conv_b, m_w_down, v_mix_norm, v_w_in, v_q_norm, v_k_norm, v_sinks, v_sgu_norm, v_w_s, v_b_s, v_w_oa, v_w_ob, v_w_out, v_ffn_norm, v_w_up, v_conv_w, v_conv_b, v_w_down):
    weights = dict(mix_norm=mix_norm, w_in=w_in, q_norm=q_norm, k_norm=k_norm, sinks=sinks, sgu_norm=sgu_norm,
                   w_s=w_s, b_s=b_s, w_oa=w_oa, w_ob=w_ob, w_out=w_out, ffn_norm=ffn_norm, w_up=w_up,
                   conv_w=conv_w, conv_b=conv_b, w_down=w_down)
    mom_m = dict(mix_norm=m_mix_norm, w_in=m_w_in, q_norm=m_q_norm, k_norm=m_k_norm, sinks=m_sinks,
                 sgu_norm=m_sgu_norm, w_s=m_w_s, b_s=m_b_s, w_oa=m_w_oa, w_ob=m_w_ob, w_out=m_w_out,
                 ffn_norm=m_ffn_norm, w_up=m_w_up, conv_w=m_conv_w, conv_b=m_conv_b, w_down=m_w_down)
    mom_v = dict(mix_norm=v_mix_norm, w_in=v_w_in, q_norm=v_q_norm, k_norm=v_k_norm, sinks=v_sinks,
                 sgu_norm=v_sgu_norm, w_s=v_w_s, b_s=v_b_s, w_oa=v_w_oa, w_ob=v_w_ob, w_out=v_w_out,
                 ffn_norm=v_ffn_norm, w_up=v_w_up, conv_w=v_conv_w, conv_b=v_conv_b, w_down=v_w_down)
    n_seq, seq, _ = x.shape
    T = n_seq * seq
    core = lax.axis_index("c")
    chip = 2 * lax.axis_index("x") + lax.axis_index("y")
    tm = min(512, seq)
    tm_ff = min(256, seq)
    tk_dw = min(2048, T)

    me = 2 * chip + core
    names = [r[0] for r in REGIONS]
    hrs = {name: rows // N_DEV for name, rows, _, _ in REGIONS}
    placed = [{}, {}]
    for name, rows, cols, transposed in REGIONS:
        shard = (jnp.swapaxes(weights[name], 1, 2) if transposed else weights[name]).reshape(2, 2, hrs[name], cols)
        for l in range(2):
            placed[l][name] = _place(shard, l, core, N_DEV, me, bf16, f"place_{name}_{l}").reshape(rows, cols)
    group_keys = [[(0, "w_in")], [(0, n) for n in names[1:]], [(1, n) for n in names]]
    group_bufs = [[placed[l][n] for l, n in keys] for keys in group_keys]
    group_bufs[0] += [conv_w, jnp.zeros((N_CHIP,) + conv_w.shape, f32)]
    n_direct = [1, 0, 0]
    plans = [_gather_plan([hrs[n] for _, n in keys], nd) for keys, nd in zip(group_keys, n_direct)]
    started, _ = _split_start(list(zip(group_bufs, plans)), "gather_start")
    gathered = [{}, {}]

    def arrived(g, after):
        send, recv, bufs = started[g]
        hr_list = [hrs[n] for _, n in group_keys[g]]
        bufs = _split_wait(bufs, send, recv, _gather_plan(hr_list, n_direct[g]), after, f"gather_wait_{g}")
        return bufs[:len(hr_list)], bufs[len(hr_list):]

    def start_pass(g, bufs):
        (res,), token = _split_start([(bufs, _pass_plan([hrs[n] for _, n in group_keys[g]]))], f"pass_start_{g}")
        return res, token[0:1, 0:1]

    def finish_pass(g, res, after):
        send, recv, bufs = res
        use(g, _split_wait(bufs, send, recv, _pass_plan([hrs[n] for _, n in group_keys[g]]), after, f"pass_wait_{g}"))

    def use(g, bufs):
        for (l, n), b in zip(group_keys[g], bufs):
            gathered[l][n] = b

    xs = x.reshape(T, D_MODEL)
    bufs, (_, conv_w_land) = arrived(0, xs)
    use(0, _pass_to_sibling(bufs, [hrs["w_in"]], "gather_pass_0"))
    conv_w_all = lax.dynamic_update_slice(conv_w_land, conv_w[None], (chip, 0, 0, 0))
    conv_w_full = jnp.concatenate([conv_w_all[j] for j in range(N_CHIP)], axis=-1)
    saved = []
    cur = xs
    for l in range(2):
        wl = gathered[l]
        b_col = b_s[l].reshape(SGU_GROUPS, BLOCK, 1)
        qkv, su, sv, ga, gb, h = _in_proj(cur, mix_norm[l][None], wl["w_in"], l, tm)
        y_att = _attn_fwd(qkv, q_norm[l][None], k_norm[l][None], sinks[l], n_seq, seq)
        sgu_gain = sgu_norm[l][None]
        if l == 0:
            pass_1, tok = start_pass(1, arrived(1, y_att)[0])
            sgu_gain = sgu_gain + tok
        y_sgu = _sgu_fwd(su, sv, sgu_gain, w_s[l], b_col, tm)
        if l == 0:
            finish_pass(1, pass_1, y_sgu)
        x1, merged, a_o, b_o = _merge_fwd(cur, y_att, y_sgu, ga, gb, wl["w_oa"], wl["w_ob"], wl["w_out"], l, tm)
        h2, z, act = _ffn_up(x1, ffn_norm[l][None], wl["w_up"], conv_w_full[l], conv_b[l][None], l, seq, tm_ff)
        if l == 0:
            pass_2, _ = start_pass(2, arrived(2, act)[0])
        x2 = _ffn_down(x1, act, wl["w_down"], l, tm)
        if l == 0:
            finish_pass(2, pass_2, x2)
        saved.append(dict(x=cur, qkv=qkv, su=su, sv=sv, ga=ga, gb=gb, h=h, y_att=y_att, y_sgu=y_sgu, x1=x1,
                          merged=merged, a=a_o, b=b_o, h2=h2, z=z, act=act, b_col=b_col))
        cur = x2

    dy, dyb, loss_part = _loss_head(cur, loss_target.reshape(T, D_MODEL), tm)
    loss = lax.psum(loss_part[0, 0], ("x", "y", "c"))

    core_arr = core.astype(jnp.int32).reshape(1)
    big = [{}, {}]
    small = {name: [None, None] for name in SMALL}

    def start_pairs(l, keys, tag):
        gl = [big[l][n] for n in keys]
        land = [lax.empty((1, N_CHIP, hrs[n], g.shape[2]), bf16) for n, g in zip(keys, gl)]
        (res,), token = _split_start([(gl + land, _pair_plan([hrs[n] for n in keys]))], f"pair_start_{tag}")
        return (l, keys, res, tag), token[0:1, 0:1]

    def pairs_to_chips(state, after):
        l, keys, (send, recv, bufs), tag = state
        bufs = _split_wait(bufs, send, recv, _pair_plan([hrs[n] for n in keys]), after, f"pair_wait_{tag}")
        return sums_to_chips(l, keys, bufs[:len(keys)], bufs[len(keys):], tag)

    def sums_to_chips(l, keys, gl, from_sibling, tag):
        pairs = [_pair_sum(g, o, core, chip, f"pair_sum_{n}_{l}") for g, o, n in zip(gl, from_sibling, keys)]
        bufs = [p[0] for p in pairs] + [p[1] for p in pairs]
        (res,), token = _split_start([(bufs, _chip_plan(len(keys), 1))], f"chip_start_{tag}")
        return (l, keys, res, tag), token[0:1, 0:1]

    def start_reduce(l, keys, tag):
        gl = [big[l][n] for n in keys]
        return sums_to_chips(l, keys, gl, _pair_exchange(gl, f"pair_exchange_{tag}"), tag)

    def finish_reduce(state, after):
        l, keys, (send, recv, bufs), tag = state
        bufs = _split_wait(bufs, send, recv, _chip_plan(len(keys), 1), after, f"chip_wait_{tag}")
        return {(l, n): _chip_sum(p, core_arr, f"chip_sum_{n}_{l}") for n, p in zip(keys, bufs[len(keys):])}

    rest = [n for n in SMALL if n != "w_s"]
    rest_shapes = [weights[n].shape[1:] if n != "conv_w" else (3, 2 * D_FF) for n in rest]
    zero = jnp.zeros((), jnp.int32)

    def start_small(l):
        packs = [(_pack_small([small[n][l] for n in rest]), f32, "small"), (small["w_s"][l].reshape(-1, 1024), bf16, "w_s")]
        bufs = [_place(p[None, None], 0, zero, N_DEV, me, dt, f"place_{tag}_{l}") for p, dt, tag in packs]
        (res,), token = _split_start([(bufs, _all_to_all_plan(2))], f"small_start_{l}")
        return res, token[0:1, 0:1]

    def finish_small(l, res, after):
        send, recv, bufs = res
        bufs = _split_wait(bufs, send, recv, _all_to_all_plan(2), after, f"small_wait_{l}")
        out = dict(zip(rest, _unpack_small(_sum_small(bufs[0], f"sum_small_{l}"), rest_shapes)))
        out["w_s"] = _sum_small(bufs[1], f"sum_w_s_{l}").reshape(w_s.shape[1:])
        return out

    pending = []
    after_start = jnp.zeros((1, 1), f32)
    for l in (1, 0):
        s = saved[l]
        wl = gathered[l]
        dz, dconv = _ffn_bwd(dyb, s["z"], conv_w_full[l], conv_b[l][None] + after_start, wl["w_down"], l, seq, tm_ff)
        big[l]["w_down"] = _weight_grad(s["act"], dyb, 1408, tk_dw, f"dw_down_{l}")
        big[l]["w_up"] = _weight_grad(dz, s["h2"], 1408, tk_dw, f"dw_up_{l}")
        ffn_gain, sgu_gain, q_gain = ffn_norm[l][None], sgu_norm[l][None], q_norm[l][None]
        if l == 0:
            pairs_a, tok = start_pairs(0, ["w_down", "w_up"], "0a")
            ffn_gain = ffn_gain + tok
        dx1, dx1b, d_ffn = _norm_bwd([dz], wl["w_up"], l, s["x1"], ffn_gain, dy, tm, f"ffn_norm_bwd_{l}")
        if l == 0:
            state, tok = pairs_to_chips(pairs_a, dx1b)
            pending.append(state)
            sgu_gain = sgu_gain + tok
        small["conv_w"][l] = dconv[0:3]
        small["conv_b"][l] = dconv[3]
        small["ffn_norm"][l] = d_ffn[0]
        da, db, dga, dgb, dya, dys = _merge_bwd(dx1b, s["ga"], s["gb"], s["a"], s["b"],
                                                wl["w_oa"], wl["w_ob"], wl["w_out"], l, tm)
        big[l]["w_out"] = _weight_grad(s["merged"], dx1b, 1024, tk_dw, f"dw_out_{l}")
        big[l]["w_oa"] = _weight_grad(da, s["y_att"], 1024, tk_dw, f"dw_oa_{l}")
        big[l]["w_ob"] = _weight_grad(db, s["y_sgu"], 1024, tk_dw, f"dw_ob_{l}")
        if l == 0:
            pairs_m, tok = start_pairs(0, ["w_out", "w_oa", "w_ob"], "0m")
            sgu_gain = sgu_gain + tok
        dsu, dsv, d_ws, d_bs, d_sgu = _sgu_bwd(dys, s["su"], s["sv"], sgu_gain, w_s[l], s["b_col"], tm)
        if l == 0:
            state, tok = pairs_to_chips(pairs_m, dsv)
            pending.append(state)
            q_gain = q_gain + tok
        causal = np.tril(np.ones((BLOCK, BLOCK), bool))
        small["w_s"][l] = jnp.where(causal[None], d_ws, 0.0)
        small["b_s"][l] = d_bs[:, :, 0]
        small["sgu_norm"][l] = d_sgu[0]
        dqkv, d_qg, d_kg, d_sink = _attn_bwd(dya, s["qkv"], q_gain, k_norm[l][None], sinks[l], n_seq, seq)
        small["q_norm"][l] = d_qg[0]
        small["k_norm"][l] = d_kg[0]
        small["sinks"][l] = d_sink[:, 0]
        dproj = [dqkv, dsu, dsv, dga, dgb]
        big[l]["w_in"] = _weight_grad_rows(dproj, s["h"], tm, f"dw_in_{l}")
        if l == 1:
            pairs_1, tok = start_pairs(1, names, "1")
        else:
            state, tok = start_reduce(0, ["w_in"], "0b")
            pending.append(state)
        dy, dyb, d_mix = _norm_bwd(dproj, wl["w_in"], l, s["x"], mix_norm[l][None] + tok, dx1, tm, f"mix_norm_bwd_{l}")
        small["mix_norm"][l] = d_mix[0]
        if l == 1:
            state, tok = pairs_to_chips(pairs_1, dyb)
            pending.append(state)
            small_1, after_start = start_small(1)
            after_start = after_start + tok
    grad_x = dy.reshape(n_seq, seq, D_MODEL)

    small_0, _ = start_small(0)
    halves = {}
    for state in pending:
        halves.update(finish_reduce(state, dyb))
    half_keys = [(l, n) for l in range(2) for n in names]
    shared = dict(zip(half_keys, _share_halves([halves[k] for k in half_keys])))
    grad_big = {}
    for name, rows, cols, transposed in REGIONS:
        per_layer = [shared[(l, name)].reshape(rows // N_CHIP, cols) for l in range(2)]
        grad_big[name] = jnp.stack([g.T if transposed else g for g in per_layer])

    grad, delta, new_m, new_v = {}, {}, {}, {}
    for name, *_ in REGIONS:
        grad[name] = grad_big[name]
        delta[name], new_m[name], new_v[name] = _adamw(weights[name], grad[name], mom_m[name], mom_v[name], f"adamw_{name}")

    per_layer = [finish_small(0, small_0, delta["w_down"]), finish_small(1, small_1, dyb)]
    grad_small = {n: jnp.stack([per_layer[0][n], per_layer[1][n]]) for n in SMALL}
    cw_cols = conv_w.shape[-1]
    grad_small["conv_w"] = lax.dynamic_slice_in_dim(grad_small["conv_w"], chip * cw_cols, cw_cols, axis=2)

    as_rows = lambda a: a.reshape(2, -1, BLOCK)
    d, nm, nv = _adamw(as_rows(w_s), as_rows(grad_small["w_s"]), as_rows(m_w_s), as_rows(v_w_s), "adamw_w_s")
    grad["w_s"], delta["w_s"], new_m["w_s"], new_v["w_s"] = (
        grad_small["w_s"], d.reshape(w_s.shape), nm.reshape(w_s.shape), nv.reshape(w_s.shape))
    shapes = [weights[n].shape for n in rest]
    d, nm, nv = _adamw(_pack_small([weights[n] for n in rest])[None], _pack_small([grad_small[n] for n in rest])[None],
                       _pack_small([mom_m[n] for n in rest])[None], _pack_small([mom_v[n] for n in rest])[None],
                       "adamw_small")
    for n, dd, mm, vv in zip(rest, _unpack_small(d, shapes), _unpack_small(nm, shapes), _unpack_small(nv, shapes)):
        grad[n], delta[n], new_m[n], new_v[n] = grad_small[n], dd, mm, vv

    order = ["mix_norm", "w_in", "q_norm", "k_norm", "sinks", "sgu_norm", "w_s", "b_s", "w_oa", "w_ob", "w_out",
             "ffn_norm", "w_up", "conv_w", "conv_b", "w_down"]
    return (loss, grad_x, *[grad[n] for n in order], *[delta[n] for n in order],
            *[new_m[n] for n in order], *[new_v[n] for n in order])
```

```python
import functools

import numpy as np
import jax
import jax.numpy as jnp
from jax import lax
from jax.experimental import pallas as pl
from jax.experimental.pallas import tpu as pltpu

bf16 = jnp.bfloat16
f32 = jnp.float32

D_MODEL = 1024
ATT_WIDTH = 512
KV_WIDTH = 128
SGU_WIDTH = 512
HEAD_DIM = 64
N_KV_HEADS = 2
Q_GROUP = 4
BLOCK = 128
SGU_GROUPS = 8
IN_WIDTH = 3840
D_FF = 2816
NORM_EPS = 1e-6
NEG_INF = -1e30
N_DEV = 8
N_CHIP = 4

ADAM_LR = 0.001
ADAM_B1 = 0.9
ADAM_B2 = 0.999
ADAM_EPS = 1e-08
ADAM_WD = 0.01
ADAM_STEP = 10

V7X_VMEM_LIMIT = 56 * 1024 * 1024
FF_CHUNK = 256

REGIONS = (
    ("w_in", IN_WIDTH, D_MODEL, True),
    ("w_oa", D_MODEL, ATT_WIDTH, True),
    ("w_ob", D_MODEL, SGU_WIDTH, True),
    ("w_out", D_MODEL, D_MODEL, False),
    ("w_up", 2 * D_FF, D_MODEL, True),
    ("w_down", D_FF, D_MODEL, False),
)
MESH = pl.DeviceIdType.MESH
ANY = pl.BlockSpec(memory_space=pl.ANY)


def _params(sem=None, **kw):
    return pltpu.CompilerParams(dimension_semantics=sem, vmem_limit_bytes=V7X_VMEM_LIMIT, **kw)


def _wspec(rows, cols, layer=None):
    del layer
    return pl.BlockSpec((rows, cols), lambda *_: (0, 0), pipeline_mode=pl.Buffered(1))


def _full(shape):
    nd = len(shape)
    return pl.BlockSpec(shape, lambda *_: (0,) * nd)


def _dot_nn(a, b):
    return jnp.dot(a, b, preferred_element_type=f32)


def _dot_nt(a, b):
    return lax.dot_general(a, b, (((1,), (1,)), ((), ())), preferred_element_type=f32)


def _dot_tn(a, b):
    return lax.dot_general(a, b, (((0,), (0,)), ((), ())), preferred_element_type=f32)


_GELU_C = float(np.sqrt(2.0 / np.pi))


def _gelu(x):
    return 0.5 * x * (1.0 + jnp.tanh(_GELU_C * (x + 0.044715 * x * x * x)))


def _gelu_grad(x):
    t = jnp.tanh(_GELU_C * (x + 0.044715 * x * x * x))
    du = _GELU_C * (1.0 + 3.0 * 0.044715 * x * x)
    return 0.5 * (1.0 + t) + 0.5 * x * (1.0 - t * t) * du


def _rms(x):
    return lax.rsqrt(jnp.mean(x * x, axis=-1, keepdims=True) + NORM_EPS)


def _mesh_pos():
    return lax.axis_index("x"), lax.axis_index("y"), lax.axis_index("c")


def _in_proj(x, gain, w_in_t, layer, tm):
    T = x.shape[0]

    def body(x_ref, g_ref, w_ref, qkv_ref, su_ref, sv_ref, ga_ref, gb_ref, h_ref):
        xf = x_ref[...]
        h = (xf * _rms(xf) * g_ref[...]).astype(bf16)
        h_ref[...] = h
        qkv_ref[...] = _dot_nt(h, w_ref[0:768, :])
        su_ref[...] = _dot_nt(h, w_ref[768:1280, :]).astype(bf16)
        sv_ref[...] = _dot_nt(h, w_ref[1280:1792, :]).astype(bf16)
        ga_ref[...] = _dot_nt(h, w_ref[1792:2816, :]).astype(bf16)
        gb_ref[...] = _dot_nt(h, w_ref[2816:3840, :]).astype(bf16)

    row = lambda w: pl.BlockSpec((tm, w), lambda i: (i, 0))
    return pl.pallas_call(
        body, name=f"in_proj_{layer}", grid=(T // tm,),
        in_specs=[row(D_MODEL), _full((1, D_MODEL)), _wspec(IN_WIDTH, D_MODEL, layer)],
        out_specs=[row(768), row(512), row(512), row(1024), row(1024), row(D_MODEL)],
        out_shape=[jax.ShapeDtypeStruct((T, 768), f32), jax.ShapeDtypeStruct((T, 512), bf16),
                   jax.ShapeDtypeStruct((T, 512), bf16), jax.ShapeDtypeStruct((T, 1024), bf16),
                   jax.ShapeDtypeStruct((T, 1024), bf16), jax.ShapeDtypeStruct((T, D_MODEL), bf16)],
        compiler_params=_params(("parallel",)),
    )(x, gain, w_in_t)


def _attn_head_group(cur, prev, qg, kg, sink_ref, n, hk):
    lo = hk * HEAD_DIM
    k_raw = jnp.concatenate([prev[:, lo:lo + HEAD_DIM], cur[:, 512 + lo:512 + lo + HEAD_DIM]], axis=0)
    v_band = jnp.concatenate([prev[:, 128 + lo:128 + lo + HEAD_DIM], cur[:, 640 + lo:640 + lo + HEAD_DIM]], axis=0)
    rk = _rms(k_raw)
    k_hat = k_raw * rk
    kn = (k_hat * kg).astype(bf16)
    q_raw = jnp.concatenate(
        [cur[:, (hk * Q_GROUP + g) * HEAD_DIM:(hk * Q_GROUP + g + 1) * HEAD_DIM] for g in range(Q_GROUP)], axis=0)
    rq = _rms(q_raw)
    q_hat = q_raw * rq
    qn = (q_hat * qg * (HEAD_DIM ** -0.5)).astype(bf16)
    s = _dot_nt(qn, kn)
    rows = lax.broadcasted_iota(jnp.int32, (Q_GROUP * BLOCK, 1), 0)
    g_of_row = rows // BLOCK
    qi = rows - g_of_row * BLOCK
    kj = lax.broadcasted_iota(jnp.int32, (1, 2 * BLOCK), 1)
    dist = qi + BLOCK - kj
    valid = (dist >= 0) & (dist < BLOCK) & ((kj >= BLOCK) | (n > 0))
    slope = jnp.zeros((Q_GROUP * BLOCK, 1), f32)
    sink = jnp.zeros((Q_GROUP * BLOCK, 1), f32)
    for g in range(Q_GROUP):
        head = hk * Q_GROUP + g
        slope = jnp.where(g_of_row == g, float(np.exp2(-8.0 * (head + 1.0) / 8.0)), slope)
        sink = jnp.where(g_of_row == g, sink_ref[head], sink)
    s = jnp.where(valid, s - slope * dist.astype(f32), NEG_INF)
    m = jnp.maximum(jnp.max(s, axis=-1, keepdims=True), sink)
    e = jnp.exp(s - m)
    e_sink = jnp.exp(sink - m)
    inv = 1.0 / (jnp.sum(e, axis=-1, keepdims=True) + e_sink)
    return dict(k_raw=k_raw, rk=rk, k_hat=k_hat, kn=kn, v=v_band.astype(bf16), q_hat=q_hat, rq=rq, qn=qn,
                p=e * inv, p_sink=e_sink * inv)


def _attn_fwd(qkv, qg, kg, sinks, n_seq, seq):
    T = n_seq * seq
    nb = seq // BLOCK

    per = 2 if nb % 2 == 0 else 1

    def body(cur_ref, prev_ref, qg_ref, kg_ref, sink_ref, y_ref):
        for sub in range(per):
            n = pl.program_id(1) * per + sub
            cur = cur_ref[sub * BLOCK:(sub + 1) * BLOCK, :]
            prev = prev_ref[...] if sub == 0 else cur_ref[(sub - 1) * BLOCK:sub * BLOCK, 512:768]
            pieces = [None] * (N_KV_HEADS * Q_GROUP)
            for hk in range(N_KV_HEADS):
                a = _attn_head_group(cur, prev, qg_ref[...], kg_ref[...], sink_ref, n, hk)
                o = _dot_nn(a["p"].astype(bf16), a["v"])
                for g in range(Q_GROUP):
                    pieces[hk * Q_GROUP + g] = o[g * BLOCK:(g + 1) * BLOCK]
            y_ref[sub * BLOCK:(sub + 1) * BLOCK, :] = jnp.concatenate(pieces, axis=1).astype(bf16)

    return pl.pallas_call(
        body, name="attn_fwd", grid=(n_seq, nb // per),
        in_specs=[pl.BlockSpec((per * BLOCK, 768), lambda b, n: (b * (nb // per) + n, 0)),
                  pl.BlockSpec((BLOCK, 256), lambda b, n: (b * nb + jnp.maximum(n * per - 1, 0), 2)),
                  _full((1, HEAD_DIM)), _full((1, HEAD_DIM)),
                  pl.BlockSpec(memory_space=pltpu.SMEM)],
        out_specs=pl.BlockSpec((per * BLOCK, ATT_WIDTH), lambda b, n: (b * (nb // per) + n, 0)),
        out_shape=jax.ShapeDtypeStruct((T, ATT_WIDTH), bf16),
        compiler_params=_params(("parallel", "parallel")),
    )(qkv, qkv, qg, kg, sinks)


def _sgu_chunk(su, sv, gain, w_ref, b_ref):
    u = _gelu(su)
    vg = _gelu(sv)
    rv = _rms(vg)
    v_hat = vg * rv
    vn = (v_hat * gain).astype(bf16)
    causal = (lax.broadcasted_iota(jnp.int32, (BLOCK, BLOCK), 0) >= lax.broadcasted_iota(jnp.int32, (BLOCK, BLOCK), 1))
    w_tril = [jnp.where(causal, w_ref[g], 0.0).astype(bf16) for g in range(SGU_GROUPS)]
    gd = SGU_WIDTH // SGU_GROUPS
    mixed = jnp.concatenate(
        [_dot_nn(w_tril[g], vn[:, g * gd:(g + 1) * gd]) + b_ref[g] for g in range(SGU_GROUPS)], axis=1)
    return u, rv, v_hat, vn, w_tril, mixed


def _sgu_fwd(su, sv, gain, w_s, b_s, tm):
    T = su.shape[0]

    def body(su_ref, sv_ref, g_ref, w_ref, b_ref, y_ref):
        for ch in range(tm // BLOCK):
            rows = slice(ch * BLOCK, (ch + 1) * BLOCK)
            u, _, _, _, _, mixed = _sgu_chunk(su_ref[rows, :].astype(f32), sv_ref[rows, :].astype(f32),
                                              g_ref[...], w_ref, b_ref)
            y_ref[rows, :] = (u * mixed).astype(bf16)

    row = pl.BlockSpec((tm, SGU_WIDTH), lambda i: (i, 0))
    return pl.pallas_call(
        body, name="sgu_fwd", grid=(T // tm,),
        in_specs=[row, row, _full((1, SGU_WIDTH)), _full((SGU_GROUPS, BLOCK, BLOCK)), _full((SGU_GROUPS, BLOCK, 1))],
        out_specs=row, out_shape=jax.ShapeDtypeStruct((T, SGU_WIDTH), bf16),
        compiler_params=_params(("parallel",)),
    )(su, sv, gain, w_s, b_s)


def _merge_fwd(x, y_att, y_sgu, ga, gb, w_oa_t, w_ob_t, w_out, layer, tm):
    T = x.shape[0]

    def body(x_ref, ya_ref, ys_ref, ga_ref, gb_ref, woa_ref, wob_ref, wout_ref, x1_ref, m_ref, a_ref, b_ref):
        a = _dot_nt(ya_ref[...], woa_ref[...])
        b = _dot_nt(ys_ref[...], wob_ref[...])
        a_ref[...] = a.astype(bf16)
        b_ref[...] = b.astype(bf16)
        merged = (jax.nn.sigmoid(ga_ref[...].astype(f32)) * a + jax.nn.sigmoid(gb_ref[...].astype(f32)) * b).astype(bf16)
        m_ref[...] = merged
        x1_ref[...] = x_ref[...] + _dot_nn(merged, wout_ref[...])

    row = lambda w: pl.BlockSpec((tm, w), lambda i: (i, 0))
    return pl.pallas_call(
        body, name=f"merge_fwd_{layer}", grid=(T // tm,),
        in_specs=[row(D_MODEL), row(512), row(512), row(1024), row(1024),
                  _wspec(D_MODEL, ATT_WIDTH, layer), _wspec(D_MODEL, SGU_WIDTH, layer), _wspec(D_MODEL, D_MODEL, layer)],
        out_specs=[row(D_MODEL)] * 4,
        out_shape=[jax.ShapeDtypeStruct((T, D_MODEL), f32)] + [jax.ShapeDtypeStruct((T, D_MODEL), bf16)] * 3,
        compiler_params=_params(("parallel",)),
    )(x, y_att, y_sgu, ga, gb, w_oa_t, w_ob_t, w_out)


def _tile_permutation(tm):
    r = np.arange(tm)
    p = np.zeros((tm, tm), np.float32)
    p[r, (r % 8) * (tm // 8) + r // 8] = 1.0
    return jnp.asarray(p, bf16), jnp.asarray(p.T, bf16)


def _stage_taps_before(buf, zz, prev, tm):
    first = lax.broadcasted_iota(jnp.int32, (8, 1), 0) == 0
    buf[16:16 + tm, :] = zz
    buf[0:8, :] = jnp.where(first, prev[7:8], pltpu.roll(buf[tm:tm + 8, :], 1, 0))
    buf[8:16, :] = jnp.where(first, prev[15:16], pltpu.roll(buf[tm + 8:tm + 16, :], 1, 0))


def _stage_taps_after(buf, nxt, tm):
    last = lax.broadcasted_iota(jnp.int32, (8, 1), 0) == 7
    buf[tm:tm + 8, :] = jnp.where(last, nxt[0:1], pltpu.roll(buf[0:8, :], 7, 0))
    buf[tm + 8:tm + 16, :] = jnp.where(last, nxt[8:9], pltpu.roll(buf[8:16, :], 7, 0))


def _conv_rows(buf, r, n, coef):
    z2 = buf[pl.ds(r, n), :]
    z1 = buf[pl.ds(pl.multiple_of(r + 8, 8), n), :]
    z0 = buf[pl.ds(pl.multiple_of(r + 16, 8), n), :]
    return coef[0] + coef[1] * z2 + coef[2] * z1 + coef[3] * z0


def _ffn_up(x1, gain, w_up_t, conv_w, conv_b, layer, seq, tm):
    T = x1.shape[0]
    tps = seq // tm
    perm, perm_t = _tile_permutation(tm)

    rg = 16

    def body(x_ref, g_ref, w_ref, cw_ref, cb_ref, p_ref, pt_ref, h2_ref, z_ref, act_ref, carry_ref,
             zg_buf, zv_buf, actp_buf):
        i = pl.program_id(0)

        @pl.when(i % tps == 0)
        def _():
            carry_ref[...] = jnp.zeros_like(carry_ref)

        xf = x_ref[...]
        h2 = (xf * _rms(xf) * g_ref[...]).astype(bf16)
        h2_ref[...] = h2
        h2p = _dot_nn(p_ref[...], h2).astype(bf16)
        for cc in range(D_FF // FF_CHUNK):
            cols_g = slice(cc * FF_CHUNK, (cc + 1) * FF_CHUNK)
            cols_v = slice(D_FF + cc * FF_CHUNK, D_FF + (cc + 1) * FF_CHUNK)
            for buf, cols in ((zg_buf, cols_g), (zv_buf, cols_v)):
                zb = _dot_nt(h2p, w_ref[cols, :]).astype(bf16)
                z_ref[:, cols] = zb
                _stage_taps_before(buf, zb.astype(f32), carry_ref[:, cols], tm)
                carry_ref[:, cols] = buf[tm:tm + 16, :]
            coef = [jnp.broadcast_to(v, (rg, FF_CHUNK)) for cols in (cols_g, cols_v)
                    for v in (cb_ref[:, cols], cw_ref[0:1, cols], cw_ref[1:2, cols], cw_ref[2:3, cols])]

            def rows_step(j, carry, coef=coef):
                r = pl.multiple_of(j * rg, rg)
                zcg, zcv = (_conv_rows(buf, r, rg, coef[4 * k:4 * k + 4]) for k, buf in enumerate((zg_buf, zv_buf)))
                actp_buf[pl.ds(r, rg), :] = (zcg * jax.nn.sigmoid(zcg) * zcv).astype(bf16)
                return carry

            lax.fori_loop(0, tm // rg, rows_step, 0, unroll=True)
            act_ref[:, cols_g] = _dot_nn(pt_ref[...], actp_buf[...]).astype(bf16)

    row = lambda w: pl.BlockSpec((tm, w), lambda i: (i, 0))
    return pl.pallas_call(
        body, name=f"ffn_up_{layer}", grid=(T // tm,),
        in_specs=[row(D_MODEL), _full((1, D_MODEL)), _wspec(2 * D_FF, D_MODEL, layer),
                  _full((3, 2 * D_FF)), _full((1, 2 * D_FF)), _full((tm, tm)), _full((tm, tm))],
        out_specs=[row(D_MODEL), row(2 * D_FF), row(D_FF)],
        out_shape=[jax.ShapeDtypeStruct((T, D_MODEL), bf16), jax.ShapeDtypeStruct((T, 2 * D_FF), bf16),
                   jax.ShapeDtypeStruct((T, D_FF), bf16)],
        scratch_shapes=[pltpu.VMEM((16, 2 * D_FF), f32), pltpu.VMEM((tm + 16, FF_CHUNK), f32),
                        pltpu.VMEM((tm + 16, FF_CHUNK), f32), pltpu.VMEM((tm, FF_CHUNK), bf16)],
        compiler_params=_params(("arbitrary",)),
    )(x1, gain, w_up_t, conv_w, conv_b, perm, perm_t)


def _ffn_down(x1, act, w_down, after, layer, tm):
    T = x1.shape[0]

    def body(x_ref, a_ref, w_ref, after_ref, o_ref):
        o_ref[...] = x_ref[...] + _dot_nn(a_ref[...], w_ref[...])

    row = lambda w: pl.BlockSpec((tm, w), lambda i: (i, 0))
    return pl.pallas_call(
        body, name=f"ffn_down_{layer}", grid=(T // tm,),
        in_specs=[row(D_MODEL), row(D_FF), _wspec(D_FF, D_MODEL, layer), _full((1, 1))],
        out_specs=row(D_MODEL), out_shape=jax.ShapeDtypeStruct((T, D_MODEL), f32),
        compiler_params=_params(("parallel",)),
    )(x1, act, w_down, after)


def _ffn_down_loss(x1, act, w_down, target, layer, tm):
    T = x1.shape[0]

    def body(x_ref, a_ref, w_ref, t_ref, dy_ref, dyb_ref, loss_ref):
        @pl.when(pl.program_id(0) == 0)
        def _():
            loss_ref[...] = jnp.zeros_like(loss_ref)

        diff = x_ref[...] + _dot_nn(a_ref[...], w_ref[...]) - t_ref[...]
        loss_ref[...] += 0.5 * jnp.sum(jnp.mean(diff * diff, axis=-1, keepdims=True), axis=0, keepdims=True)
        dy = diff * (1.0 / D_MODEL)
        dy_ref[...] = dy
        dyb_ref[...] = dy.astype(bf16)

    row = lambda w: pl.BlockSpec((tm, w), lambda i: (i, 0))
    return pl.pallas_call(
        body, name=f"ffn_down_loss_{layer}", grid=(T // tm,),
        in_specs=[row(D_MODEL), row(D_FF), _wspec(D_FF, D_MODEL, layer), row(D_MODEL)],
        out_specs=[row(D_MODEL), row(D_MODEL), _full((8, 128))],
        out_shape=[jax.ShapeDtypeStruct((T, D_MODEL), f32), jax.ShapeDtypeStruct((T, D_MODEL), bf16),
                   jax.ShapeDtypeStruct((8, 128), f32)],
        compiler_params=_params(("arbitrary",)),
    )(x1, act, w_down, target)


def _ffn_bwd(dx2b, z, conv_w, conv_b, w_down, layer, seq, tm):
    T = z.shape[0]
    nt = T // tm
    tps = seq // tm

    perm, perm_t = _tile_permutation(tm)

    def body(dx_ref, z_ref, zh_ref, cw_ref, cb_ref, wd_ref, p_ref, pt_ref, dz_ref, dconv_ref, carry_ref,
             zg_buf, zv_buf, gg_buf, gv_buf, dact_buf, dzp_buf):
        i = pl.program_id(0)
        pos = (nt - 1 - i) % tps

        @pl.when(i == 0)
        def _():
            dconv_ref[...] = jnp.zeros_like(dconv_ref)

        @pl.when(pos == tps - 1)
        def _():
            carry_ref[...] = jnp.zeros_like(carry_ref)

        dxp = _dot_nn(p_ref[...], dx_ref[...]).astype(bf16)
        halo_on = (pos > 0).astype(f32)
        for cc in range(D_FF // FF_CHUNK):
            cols_g = slice(cc * FF_CHUNK, (cc + 1) * FF_CHUNK)
            cols_v = slice(D_FF + cc * FF_CHUNK, D_FF + (cc + 1) * FF_CHUNK)
            for buf, cols in ((zg_buf, cols_g), (zv_buf, cols_v)):
                _stage_taps_before(buf, z_ref[:, cols].astype(f32), zh_ref[:, cols].astype(f32) * halo_on, tm)
            dact_buf[...] = _dot_nt(dxp, wd_ref[cols_g, :])
            coef = [jnp.broadcast_to(v, (8, FF_CHUNK)) for cols in (cols_g, cols_v)
                    for v in (cb_ref[:, cols], cw_ref[0:1, cols], cw_ref[1:2, cols], cw_ref[2:3, cols])]

            def first_pass(j, sums, coef=coef):
                r = pl.multiple_of(j * 8, 8)
                rows = pl.ds(r, 8)
                zcg = _conv_rows(zg_buf, r, 8, coef[0:4])
                zcv = _conv_rows(zv_buf, r, 8, coef[4:8])
                sg = jax.nn.sigmoid(zcg)
                silu = zcg * sg
                d_act = dact_buf[rows, :]
                dg = d_act * zcv * sg * (1.0 + zcg * (1.0 - sg))
                dv = d_act * silu
                gg_buf[rows, :] = dg
                gv_buf[rows, :] = dv
                out = []
                for k, (g, buf) in enumerate(((dg, zg_buf), (dv, zv_buf))):
                    out += [sums[4 * k] + g * buf[rows, :],
                            sums[4 * k + 1] + g * buf[pl.ds(pl.multiple_of(r + 8, 8), 8), :],
                            sums[4 * k + 2] + g * buf[pl.ds(pl.multiple_of(r + 16, 8), 8), :],
                            sums[4 * k + 3] + g]
                return tuple(out)

            sums = lax.fori_loop(0, tm // 8, first_pass, tuple(jnp.zeros((8, FF_CHUNK), f32) for _ in range(8)),
                                 unroll=True)
            for k, cols in enumerate((cols_g, cols_v)):
                for tap in range(4):
                    dconv_ref[tap:tap + 1, cols] += jnp.sum(sums[4 * k + tap], axis=0, keepdims=True)
            for buf, cols in ((gg_buf, cols_g), (gv_buf, cols_v)):
                _stage_taps_after(buf, carry_ref[:, cols], tm)
                carry_ref[:, cols] = buf[0:16, :]
                w0, w1, w2 = (jnp.broadcast_to(cw_ref[k:k + 1, cols], (16, FF_CHUNK)) for k in range(3))

                def second_pass(j, carry, buf=buf, w0=w0, w1=w1, w2=w2):
                    r = pl.multiple_of(j * 16, 16)
                    dzp_buf[pl.ds(r, 16), :] = (w2 * buf[pl.ds(r, 16), :] + w1 * buf[pl.ds(pl.multiple_of(r + 8, 8), 16), :]
                                                + w0 * buf[pl.ds(pl.multiple_of(r + 16, 16), 16), :]).astype(bf16)
                    return carry

                lax.fori_loop(0, tm // 16, second_pass, 0, unroll=True)
                dz_ref[:, cols] = _dot_nn(pt_ref[...], dzp_buf[...]).astype(bf16)

    rev = lambda w: pl.BlockSpec((tm, w), lambda i: (nt - 1 - i, 0))
    return pl.pallas_call(
        body, name=f"ffn_bwd_{layer}", grid=(nt,),
        in_specs=[rev(D_MODEL), rev(2 * D_FF),
                  pl.BlockSpec((16, 2 * D_FF), lambda i: (jnp.maximum((nt - 1 - i) * (tm // 16) - 1, 0), 0)),
                  _full((3, 2 * D_FF)), _full((1, 2 * D_FF)), _wspec(D_FF, D_MODEL, layer),
                  _full((tm, tm)), _full((tm, tm))],
        out_specs=[rev(2 * D_FF), _full((8, 2 * D_FF))],
        out_shape=[jax.ShapeDtypeStruct((T, 2 * D_FF), bf16), jax.ShapeDtypeStruct((8, 2 * D_FF), f32)],
        scratch_shapes=[pltpu.VMEM((16, 2 * D_FF), f32)] + [pltpu.VMEM((tm + 16, FF_CHUNK), f32)] * 4
        + [pltpu.VMEM((tm, FF_CHUNK), f32), pltpu.VMEM((tm, FF_CHUNK), bf16)],
        compiler_params=_params(("arbitrary",)),
    )(dx2b, z, z, conv_w, conv_b, w_down, perm, perm_t)


def _norm_bwd(dys, w, layer, x, gain, dres, tm, name):
    T = dys[0].shape[0]
    widths = [d.shape[1] for d in dys]
    K = sum(widths)
    n = len(dys)

    def body(*refs):
        dy_refs = refs[:n]
        w_ref, x_ref, g_ref, dres_ref, dx_ref, dxb_ref, dg_ref = refs[n:]

        @pl.when(pl.program_id(0) == 0)
        def _():
            dg_ref[...] = jnp.zeros_like(dg_ref)

        dh, lo = None, 0
        for dy_ref, wd in zip(dy_refs, widths):
            part = _dot_nn(dy_ref[...], w_ref[lo:lo + wd, :])
            dh = part if dh is None else dh + part
            lo += wd
        xf = x_ref[...]
        r = _rms(xf)
        x_hat = xf * r
        dg_ref[...] += jnp.sum(dh * x_hat, axis=0, keepdims=True)
        dxh = dh * g_ref[...]
        dx = dres_ref[...] + r * (dxh - x_hat * jnp.mean(dxh * x_hat, axis=-1, keepdims=True))
        dx_ref[...] = dx
        dxb_ref[...] = dx.astype(bf16)

    row = lambda w_: pl.BlockSpec((tm, w_), lambda i: (i, 0))
    return pl.pallas_call(
        body, name=name, grid=(T // tm,),
        in_specs=[row(wd) for wd in widths] + [_wspec(K, D_MODEL, layer), row(D_MODEL), _full((1, D_MODEL)), row(D_MODEL)],
        out_specs=[row(D_MODEL), row(D_MODEL), _full((1, D_MODEL))],
        out_shape=[jax.ShapeDtypeStruct((T, D_MODEL), f32), jax.ShapeDtypeStruct((T, D_MODEL), bf16),
                   jax.ShapeDtypeStruct((1, D_MODEL), f32)],
        compiler_params=_params(("arbitrary",)),
    )(*dys, w, x, gain, dres)


def _merge_bwd(dx1b, ga, gb, a, b, w_oa_t, w_ob_t, w_out, layer, tm):
    T = dx1b.shape[0]

    def body(dx_ref, ga_ref, gb_ref, a_ref, b_ref, woa_ref, wob_ref, wout_ref,
             da_ref, db_ref, dga_ref, dgb_ref, dya_ref, dys_ref):
        dm = _dot_nt(dx_ref[...], wout_ref[...])
        sa = jax.nn.sigmoid(ga_ref[...].astype(f32))
        sb = jax.nn.sigmoid(gb_ref[...].astype(f32))
        da = (dm * sa).astype(bf16)
        db = (dm * sb).astype(bf16)
        da_ref[...] = da
        db_ref[...] = db
        dga_ref[...] = (dm * a_ref[...].astype(f32) * sa * (1.0 - sa)).astype(bf16)
        dgb_ref[...] = (dm * b_ref[...].astype(f32) * sb * (1.0 - sb)).astype(bf16)
        dya_ref[...] = _dot_nn(da, woa_ref[...]).astype(bf16)
        dys_ref[...] = _dot_nn(db, wob_ref[...]).astype(bf16)

    row = lambda w: pl.BlockSpec((tm, w), lambda i: (i, 0))
    return pl.pallas_call(
        body, name=f"merge_bwd_{layer}", grid=(T // tm,),
        in_specs=[row(D_MODEL)] * 5 + [_wspec(D_MODEL, ATT_WIDTH, layer), _wspec(D_MODEL, SGU_WIDTH, layer),
                                       _wspec(D_MODEL, D_MODEL, layer)],
        out_specs=[row(D_MODEL)] * 4 + [row(512)] * 2,
        out_shape=[jax.ShapeDtypeStruct((T, D_MODEL), bf16)] * 4 + [jax.ShapeDtypeStruct((T, 512), bf16)] * 2,
        compiler_params=_params(("parallel",)),
    )(dx1b, ga, gb, a, b, w_oa_t, w_ob_t, w_out)


def _sgu_bwd(dy, su, sv, gain, w_s, b_s, tm):
    T = su.shape[0]
    gd = SGU_WIDTH // SGU_GROUPS

    def body(dy_ref, su_ref, sv_ref, g_ref, w_ref, b_ref, dsu_ref, dsv_ref, dw_ref, db_ref, dg_ref):
        @pl.when(pl.program_id(0) == 0)
        def _():
            dw_ref[...] = jnp.zeros_like(dw_ref)
            db_ref[...] = jnp.zeros_like(db_ref)
            dg_ref[...] = jnp.zeros_like(dg_ref)

        gain_v = g_ref[...]
        for ch in range(tm // BLOCK):
            rows = slice(ch * BLOCK, (ch + 1) * BLOCK)
            su_c = su_ref[rows, :].astype(f32)
            sv_c = sv_ref[rows, :].astype(f32)
            u, rv, v_hat, vn, w_tril, mixed = _sgu_chunk(su_c, sv_c, gain_v, w_ref, b_ref)
            dyc = dy_ref[rows, :].astype(f32)
            dsu_ref[rows, :] = (dyc * mixed * _gelu_grad(su_c)).astype(bf16)
            dmix = dyc * u
            dmix_b = dmix.astype(bf16)
            dvn = []
            for g in range(SGU_GROUPS):
                gs = slice(g * gd, (g + 1) * gd)
                db_ref[g] += jnp.sum(dmix[:, gs], axis=1, keepdims=True)
                dw_ref[g] += _dot_nt(dmix_b[:, gs], vn[:, gs])
                dvn.append(_dot_tn(w_tril[g], dmix_b[:, gs]))
            dvn = jnp.concatenate(dvn, axis=1)
            dg_ref[...] += jnp.sum(dvn * v_hat, axis=0, keepdims=True)
            dxh = dvn * gain_v
            dvg = rv * (dxh - v_hat * jnp.mean(dxh * v_hat, axis=-1, keepdims=True))
            dsv_ref[rows, :] = (dvg * _gelu_grad(sv_c)).astype(bf16)

    row = pl.BlockSpec((tm, SGU_WIDTH), lambda i: (i, 0))
    return pl.pallas_call(
        body, name="sgu_bwd", grid=(T // tm,),
        in_specs=[row, row, row, _full((1, SGU_WIDTH)), _full((SGU_GROUPS, BLOCK, BLOCK)),
                  _full((SGU_GROUPS, BLOCK, 1))],
        out_specs=[row, row, _full((SGU_GROUPS, BLOCK, BLOCK)), _full((SGU_GROUPS, BLOCK, 1)), _full((1, SGU_WIDTH))],
        out_shape=[jax.ShapeDtypeStruct((T, SGU_WIDTH), bf16)] * 2 + [
            jax.ShapeDtypeStruct((SGU_GROUPS, BLOCK, BLOCK), f32), jax.ShapeDtypeStruct((SGU_GROUPS, BLOCK, 1), f32),
            jax.ShapeDtypeStruct((1, SGU_WIDTH), f32)],
        compiler_params=_params(("arbitrary",)),
    )(dy, su, sv, gain, w_s, b_s)


def _attn_bwd(dy, qkv, qg, kg, sinks, n_seq, seq):
    T = n_seq * seq
    nb = seq // BLOCK
    scale = HEAD_DIM ** -0.5

    def body(dy_ref, cur_ref, prev_ref, qg_ref, kg_ref, sink_ref, dqkv_ref, dqg_ref, dkg_ref, dsink_ref,
             carry_k, carry_v):
        b = pl.program_id(0)
        j = pl.program_id(1)
        n = nb - 1 - j

        @pl.when((b == 0) & (j == 0))
        def _():
            dqg_ref[...] = jnp.zeros_like(dqg_ref)
            dkg_ref[...] = jnp.zeros_like(dkg_ref)
            dsink_ref[...] = jnp.zeros_like(dsink_ref)

        @pl.when(j == 0)
        def _():
            carry_k[...] = jnp.zeros_like(carry_k)
            carry_v[...] = jnp.zeros_like(carry_v)

        cur = cur_ref[...]
        prev = prev_ref[...]
        dyf = dy_ref[...].astype(f32)
        qg_v = qg_ref[...]
        kg_v = kg_ref[...]
        dq_pieces = [None] * (N_KV_HEADS * Q_GROUP)
        dk_pieces, dv_pieces = [], []
        for hk in range(N_KV_HEADS):
            a = _attn_head_group(cur, prev, qg_v, kg_v, sink_ref, n, hk)
            do = jnp.concatenate(
                [dyf[:, (hk * Q_GROUP + g) * HEAD_DIM:(hk * Q_GROUP + g + 1) * HEAD_DIM] for g in range(Q_GROUP)],
                axis=0).astype(bf16)
            p = a["p"]
            dp = _dot_nt(do, a["v"])
            dv_band = _dot_tn(p.astype(bf16), do)
            dsum = jnp.sum(p * dp, axis=-1, keepdims=True)
            ds = (p * (dp - dsum)).astype(bf16)
            dsink_col = -a["p_sink"] * dsum
            for g in range(Q_GROUP):
                head = hk * Q_GROUP + g
                dsink_ref[head:head + 1, :] += jnp.sum(dsink_col[g * BLOCK:(g + 1) * BLOCK], axis=0, keepdims=True)
            dqn = _dot_nn(ds, a["kn"])
            dkn_band = _dot_tn(ds, a["qn"])
            dq_hat_g = dqn * scale
            dqg_ref[...] += jnp.sum(dq_hat_g * a["q_hat"], axis=0, keepdims=True)
            dxh = dq_hat_g * qg_v
            dq = a["rq"] * (dxh - a["q_hat"] * jnp.mean(dxh * a["q_hat"], axis=-1, keepdims=True))
            for g in range(Q_GROUP):
                dq_pieces[hk * Q_GROUP + g] = dq[g * BLOCK:(g + 1) * BLOCK]
            dkn = dkn_band[BLOCK:] + carry_k[hk]
            dv_pieces.append(dv_band[BLOCK:] + carry_v[hk])
            carry_k[hk] = dkn_band[:BLOCK]
            carry_v[hk] = dv_band[:BLOCK]
            k_hat = a["k_hat"][BLOCK:]
            dkg_ref[...] += jnp.sum(dkn * k_hat, axis=0, keepdims=True)
            dxk = dkn * kg_v
            dk_pieces.append(a["rk"][BLOCK:] * (dxk - k_hat * jnp.mean(dxk * k_hat, axis=-1, keepdims=True)))
        dqkv_ref[...] = jnp.concatenate(dq_pieces + dk_pieces + dv_pieces, axis=1).astype(bf16)

    blk = lambda w: pl.BlockSpec((BLOCK, w), lambda b, j: (b * nb + nb - 1 - j, 0))
    return pl.pallas_call(
        body, name="attn_bwd", grid=(n_seq, nb),
        in_specs=[blk(ATT_WIDTH), blk(768),
                  pl.BlockSpec((BLOCK, 256), lambda b, j: (b * nb + jnp.maximum(nb - 2 - j, 0), 2)),
                  _full((1, HEAD_DIM)), _full((1, HEAD_DIM)), pl.BlockSpec(memory_space=pltpu.SMEM)],
        out_specs=[blk(768), _full((1, HEAD_DIM)), _full((1, HEAD_DIM)), _full((8, 128))],
        out_shape=[jax.ShapeDtypeStruct((T, 768), bf16), jax.ShapeDtypeStruct((1, HEAD_DIM), f32),
                   jax.ShapeDtypeStruct((1, HEAD_DIM), f32), jax.ShapeDtypeStruct((8, 128), f32)],
        scratch_shapes=[pltpu.VMEM((N_KV_HEADS, BLOCK, HEAD_DIM), f32), pltpu.VMEM((N_KV_HEADS, BLOCK, HEAD_DIM), f32)],
        compiler_params=_params(("arbitrary", "arbitrary")),
    )(dy, qkv, qkv, qg, kg, sinks)


def _weight_grad(a, b, tm, tk, name):
    T, M = a.shape
    N = b.shape[1]
    nk = T // tk

    def body(a_ref, b_ref, o_ref, acc_ref):
        k = pl.program_id(1)

        @pl.when(k == 0)
        def _():
            acc_ref[...] = jnp.zeros_like(acc_ref)

        acc_ref[...] += _dot_tn(a_ref[...], b_ref[...])

        @pl.when(k == nk - 1)
        def _():
            o_ref[...] = acc_ref[...].astype(bf16)

    return pl.pallas_call(
        body, name=name, grid=(M // tm, nk),
        in_specs=[pl.BlockSpec((tk, tm), lambda i, k: (k, i)), pl.BlockSpec((tk, N), lambda i, k: (k, 0))],
        out_specs=pl.BlockSpec((None, tm, N), lambda i, k: (0, i, 0)),
        out_shape=jax.ShapeDtypeStruct((1, M, N), bf16),
        scratch_shapes=[pltpu.VMEM((tm, N), f32)],
        compiler_params=_params(("parallel", "arbitrary")),
    )(a, b)


def _weight_grad_rows(a_list, b, tk, name):
    T, N = b.shape
    widths = [a.shape[1] for a in a_list]
    M = sum(widths)
    nk = T // tk
    n = len(a_list)

    def body(*refs):
        a_refs = refs[:n]
        b_ref, o_ref, acc_ref = refs[n:]
        k = pl.program_id(0)

        @pl.when(k == 0)
        def _():
            acc_ref[...] = jnp.zeros_like(acc_ref)

        lo = 0
        for a_ref, wd in zip(a_refs, widths):
            acc_ref[lo:lo + wd, :] += _dot_tn(a_ref[...], b_ref[...])
            lo += wd

        @pl.when(k == nk - 1)
        def _():
            o_ref[...] = acc_ref[...].astype(bf16)

    return pl.pallas_call(
        body, name=name, grid=(nk,),
        in_specs=[pl.BlockSpec((tk, wd), lambda k: (k, 0)) for wd in widths] + [pl.BlockSpec((tk, N), lambda k: (k, 0))],
        out_specs=pl.BlockSpec((None, M, N), lambda k: (0, 0, 0)),
        out_shape=jax.ShapeDtypeStruct((1, M, N), bf16),
        scratch_shapes=[pltpu.VMEM((M, N), f32)],
        compiler_params=_params(("arbitrary",)),
    )(*a_list, b)


def _place(src, layer, src_slot, n_slots, dst_slot, dtype, name):
    _, _, rows, cols = src.shape
    slots = jnp.stack([src_slot, dst_slot]).astype(jnp.int32)

    def body(slots_ref, s_ref, o_ref):
        o_ref[...] = s_ref[...].astype(dtype)

    return pl.pallas_call(
        body, name=name,
        grid_spec=pltpu.PrefetchScalarGridSpec(
            num_scalar_prefetch=1, grid=(1,),
            in_specs=[pl.BlockSpec((None, None, rows, cols), lambda i, sl: (layer, sl[0], 0, 0))],
            out_specs=pl.BlockSpec((None, rows, cols), lambda i, sl: (sl[1], 0, 0))),
        out_shape=jax.ShapeDtypeStruct((n_slots, rows, cols), dtype),
        compiler_params=_params(("arbitrary",)),
    )(slots, src)


HBM = pl.BlockSpec(memory_space=pltpu.HBM)
SEM = pl.BlockSpec(memory_space=pltpu.SEMAPHORE)
DATAFLOW = pltpu.SideEffectType.DATAFLOW_SIDE_EFFECTING


def _other_chips(x, y):
    return [(1 - x, y), (x, 1 - y), (1 - x, 1 - y)]


def _split_start(groups, name):
    nb = [len(bufs) for bufs, _ in groups]
    flat = [b for bufs, _ in groups for b in bufs]
    ns = [len(plan(bufs, dry=True)) for bufs, plan in groups]
    ng = len(groups)

    def body(*refs):
        n_in = len(flat)
        sems = refs[n_in:n_in + 2 * ng]
        thru = refs[n_in + 2 * ng:2 * n_in + 2 * ng]
        token = refs[2 * n_in + 2 * ng]
        off = 0
        for g, (bufs, plan) in enumerate(groups):
            mine = thru[off:off + nb[g]]
            off += nb[g]
            for k, (src, dst, to) in enumerate(plan(mine)):
                pltpu.make_async_remote_copy(
                    src_ref=src, dst_ref=dst, send_sem=sems[2 * g].at[k], recv_sem=sems[2 * g + 1].at[k],
                    device_id=to, device_id_type=MESH).start()
        token[...] = jnp.zeros_like(token)

    out_shape = []
    for n in ns:
        out_shape += [pltpu.SemaphoreType.DMA((n,)), pltpu.SemaphoreType.DMA((n,))]
    out_shape += [pltpu.HBM(b.shape, b.dtype) for b in flat]
    out_shape.append(jax.ShapeDtypeStruct((8, 128), f32))
    res = pl.pallas_call(
        body, name=name, out_shape=tuple(out_shape),
        in_specs=[HBM] * len(flat),
        out_specs=tuple([SEM] * (2 * ng) + [HBM] * len(flat) + [pl.BlockSpec(memory_space=pltpu.VMEM)]),
        input_output_aliases={i: 2 * ng + i for i in range(len(flat))},
        compiler_params=pltpu.CompilerParams(has_side_effects=DATAFLOW),
    )(*[pltpu.with_memory_space_constraint(b, pltpu.HBM) for b in flat])
    out, off = [], 2 * ng
    for g in range(ng):
        out.append((res[2 * g], res[2 * g + 1], list(res[off:off + nb[g]])))
        off += nb[g]
    return out, res[-1]


def _split_wait(bufs, send, recv, plan, after, name):
    nb = len(bufs)

    def body(*refs):
        thru = refs[:nb]
        send_ref, recv_ref = refs[nb], refs[nb + 1]
        for k, (src, dst, to) in enumerate(plan(thru)):
            cp = pltpu.make_async_remote_copy(
                src_ref=src, dst_ref=dst, send_sem=send_ref.at[k], recv_sem=recv_ref.at[k],
                device_id=to, device_id_type=MESH)
            cp.wait_send()
            cp.wait_recv()

    res = pl.pallas_call(
        body, name=name, out_shape=tuple(pltpu.HBM(b.shape, b.dtype) for b in bufs),
        in_specs=[HBM] * nb + [SEM, SEM, ANY], out_specs=tuple([HBM] * nb),
        input_output_aliases={i: i for i in range(nb)},
        compiler_params=pltpu.CompilerParams(has_side_effects=DATAFLOW),
    )(*bufs, send, recv, after)
    return list(res)


def _gather_plan(hrs, n_direct=0):
    def plan(refs, dry=False):
        if dry:
            return [None] * (4 * len(hrs) + 3 * n_direct)
        x, y, c = _mesh_pos()
        me = 4 * x + 2 * y + c
        out = []
        for i in range(n_direct):
            src, land = refs[len(hrs) + 2 * i], refs[len(hrs) + 2 * i + 1]
            out += [(src, land.at[2 * x + y], (*chip, c)) for chip in _other_chips(x, y)]
        for ref, hr in zip(refs, hrs):
            rows = ref.at[pl.ds(pl.multiple_of(me * hr, 16), hr), :]
            out.append((rows, rows, (x, y, 1 - c)))
            out += [(rows, rows, (*chip, c)) for chip in _other_chips(x, y)]
        return out
    return plan


def _pass_plan(hrs):
    def plan(refs, dry=False):
        if dry:
            return [None] * (3 * len(hrs))
        x, y, c = _mesh_pos()
        out = []
        for ref, hr in zip(refs, hrs):
            for chip in _other_chips(x, y):
                rows = ref.at[pl.ds(pl.multiple_of((4 * chip[0] + 2 * chip[1] + c) * hr, 16), hr), :]
                out.append((rows, rows, (x, y, 1 - c)))
        return out
    return plan


def _pair_plan(hrs):
    n = len(hrs)

    def plan(refs, dry=False):
        if dry:
            return [None] * (N_CHIP * n)
        x, y, c = _mesh_pos()
        out = []
        for r in range(n):
            for j in range(N_CHIP):
                start = pl.multiple_of((2 * j + 1 - c) * hrs[r], 16)
                out.append((refs[r].at[0, pl.ds(start, hrs[r]), :], refs[n + r].at[0, j], (x, y, 1 - c)))
        return out
    return plan


def _all_to_all_plan(n):
    def plan(refs, dry=False):
        if dry:
            return [None] * (7 * n)
        x, y, c = _mesh_pos()
        out = []
        for ref in refs:
            mine = ref.at[4 * x + 2 * y + c]
            for fx in range(2):
                for fy in range(2):
                    for fc in range(2):
                        if fx or fy or fc:
                            out.append((mine, mine, (1 - x if fx else x, 1 - y if fy else y, 1 - c if fc else c)))
        return out
    return plan


def _pass_to_sibling(bufs, hrs, name):
    nb = len(bufs)

    def body(*refs):
        out = refs[nb:2 * nb]
        send, recv = refs[2 * nb:]
        x, y, c = _mesh_pos()
        chips = _other_chips(x, y)
        started = []
        for i in range(nb):
            for j, chip in enumerate(chips):
                rows = out[i].at[pl.ds(pl.multiple_of((4 * chip[0] + 2 * chip[1] + c) * hrs[i], 16), hrs[i]), :]
                cp = pltpu.make_async_remote_copy(
                    src_ref=rows, dst_ref=rows, send_sem=send.at[3 * i + j], recv_sem=recv.at[3 * i + j],
                    device_id=(x, y, 1 - c), device_id_type=MESH)
                cp.start()
                started.append(cp)
        for i in range(nb):
            for j, chip in enumerate(chips):
                rows = out[i].at[pl.ds(pl.multiple_of((4 * chip[0] + 2 * chip[1] + 1 - c) * hrs[i], 16), hrs[i]), :]
                pltpu.make_async_remote_copy(
                    src_ref=rows, dst_ref=rows, send_sem=send.at[3 * i + j], recv_sem=recv.at[3 * i + j],
                    device_id=(x, y, 1 - c), device_id_type=MESH).wait_recv()
        for cp in started:
            cp.wait_send()

    return list(pl.pallas_call(
        body, name=name, in_specs=[ANY] * nb, out_specs=[ANY] * nb,
        out_shape=[jax.ShapeDtypeStruct(b.shape, b.dtype) for b in bufs],
        input_output_aliases={i: i for i in range(nb)},
        scratch_shapes=[pltpu.SemaphoreType.DMA((3 * nb,)), pltpu.SemaphoreType.DMA((3 * nb,))],
        compiler_params=pltpu.CompilerParams(has_side_effects=True),
    )(*bufs))


def _pair_exchange(grads, name):
    nr = len(grads)
    n_l = grads[0].shape[0]
    n_sem = nr * n_l * N_CHIP

    def body(*refs):
        src = refs[:nr]
        out = refs[nr:2 * nr]
        send, recv = refs[2 * nr:]
        x, y, c = _mesh_pos()
        copies = []
        for r in range(nr):
            hr = grads[r].shape[1] // N_DEV
            for layer in range(n_l):
                for j in range(N_CHIP):
                    idx = (r * n_l + layer) * N_CHIP + j
                    start = pl.multiple_of((2 * j + 1 - c) * hr, 16)
                    cp = pltpu.make_async_remote_copy(
                        src_ref=src[r].at[layer, pl.ds(start, hr), :], dst_ref=out[r].at[layer, j],
                        send_sem=send.at[idx], recv_sem=recv.at[idx], device_id=(x, y, 1 - c), device_id_type=MESH)
                    cp.start()
                    copies.append(cp)
        for cp in copies:
            cp.wait()

    return pl.pallas_call(
        body, name=name,
        in_specs=[ANY] * nr, out_specs=[ANY] * nr,
        out_shape=[jax.ShapeDtypeStruct((n_l, N_CHIP, g.shape[1] // N_DEV, g.shape[2]), bf16) for g in grads],
        scratch_shapes=[pltpu.SemaphoreType.DMA((n_sem,)), pltpu.SemaphoreType.DMA((n_sem,))],
        compiler_params=pltpu.CompilerParams(has_side_effects=True),
    )(*grads)


def _pair_sum(grad, other, core, chip, name):
    n_l, rows, cols = grad.shape
    hr = rows // N_DEV
    g5 = grad.reshape(n_l, N_CHIP, 2, hr, cols)
    where = jnp.stack([core, chip]).astype(jnp.int32)

    def body(where_ref, g_ref, o_ref, s_ref, mine_ref):
        s_ref[...] = (g_ref[...].astype(f32) + o_ref[...].astype(f32)).astype(bf16)
        mine_ref[...] = s_ref[where_ref[1]]

    return pl.pallas_call(
        body, name=name,
        grid_spec=pltpu.PrefetchScalarGridSpec(
            num_scalar_prefetch=1, grid=(n_l,),
            in_specs=[pl.BlockSpec((None, N_CHIP, None, hr, cols), lambda l, w: (l, 0, w[0], 0, 0)),
                      pl.BlockSpec((None, N_CHIP, hr, cols), lambda l, w: (l, 0, 0, 0))],
            out_specs=[pl.BlockSpec((None, N_CHIP, hr, cols), lambda l, w: (l, 0, 0, 0)),
                       pl.BlockSpec((None, None, hr, cols), lambda l, w: (l, w[1], 0, 0))]),
        out_shape=[jax.ShapeDtypeStruct((n_l, N_CHIP, hr, cols), bf16)] * 2,
        compiler_params=_params(("arbitrary",)),
    )(where, g5, other)


def _chip_plan(nr, n_l):
    def plan(refs, dry=False):
        if dry:
            return [None] * (nr * n_l * 3)
        x, y, c = _mesh_pos()
        out = []
        for r in range(nr):
            for layer in range(n_l):
                for chip in _other_chips(x, y):
                    out.append((refs[r].at[layer, 2 * chip[0] + chip[1]], refs[nr + r].at[layer, 2 * x + y], (*chip, c)))
        return out
    return plan


def _chip_sum(parts, core, name):
    n_l, _, hr, cols = parts.shape

    def body(core_ref, p_ref, o_ref):
        acc = p_ref[0].astype(f32) + p_ref[1].astype(f32)
        acc = acc + p_ref[2].astype(f32)
        o_ref[...] = acc + p_ref[3].astype(f32)

    return pl.pallas_call(
        body, name=name,
        grid_spec=pltpu.PrefetchScalarGridSpec(
            num_scalar_prefetch=1, grid=(n_l,),
            in_specs=[pl.BlockSpec((None, N_CHIP, hr, cols), lambda l, cr: (l, 0, 0, 0))],
            out_specs=pl.BlockSpec((None, None, hr, cols), lambda l, cr: (l, cr[0], 0, 0))),
        out_shape=jax.ShapeDtypeStruct((n_l, 2, hr, cols), f32),
        compiler_params=_params(("arbitrary",)),
    )(core, parts)


def _share_halves(halves):
    nr = len(halves)

    def body(*refs):
        out = refs[nr:2 * nr]
        send, recv = refs[2 * nr:]
        x, y, c = _mesh_pos()
        copies = []
        for r in range(nr):
            cp = pltpu.make_async_remote_copy(
                src_ref=out[r].at[0, c], dst_ref=out[r].at[0, c], send_sem=send.at[r],
                recv_sem=recv.at[r], device_id=(x, y, 1 - c), device_id_type=MESH)
            cp.start()
            copies.append(cp)
        for r in range(nr):
            copies[r].wait_send()
            pltpu.make_async_remote_copy(
                src_ref=out[r].at[0, 1 - c], dst_ref=out[r].at[0, 1 - c], send_sem=send.at[r],
                recv_sem=recv.at[r], device_id=(x, y, 1 - c), device_id_type=MESH).wait_recv()

    return pl.pallas_call(
        body, name="grad_share_halves",
        in_specs=[ANY] * nr, out_specs=[ANY] * nr,
        out_shape=[jax.ShapeDtypeStruct(h.shape, h.dtype) for h in halves],
        input_output_aliases={r: r for r in range(nr)},
        scratch_shapes=[pltpu.SemaphoreType.DMA((nr,))] * 2,
        compiler_params=pltpu.CompilerParams(has_side_effects=True),
    )(*halves)


def _sum_small(parts, name):
    n, rows, cols = parts.shape

    def body(p_ref, o_ref):
        acc = p_ref[0].astype(f32)
        for d in range(1, n):
            acc = acc + p_ref[d].astype(f32)
        o_ref[...] = acc

    return pl.pallas_call(
        body, name=name, grid=(rows // 16,),
        in_specs=[pl.BlockSpec((n, 16, cols), lambda i: (0, i, 0))], out_specs=pl.BlockSpec((16, cols), lambda i: (i, 0)),
        out_shape=jax.ShapeDtypeStruct((rows, cols), f32),
        compiler_params=_params(("parallel",)),
    )(parts)


def _adamw(w, g, m, v, name):
    n_l, rows, cols = w.shape
    budget = 42 * 1024 * 1024
    tr = next(rows // d for d in range(1, rows + 1)
              if rows % d == 0 and (rows // d) % 8 == 0 and (rows // d) * cols * 4 * 14 <= budget)

    def body(w_ref, g_ref, m_ref, v_ref, d_ref, nm_ref, nv_ref):
        gg = g_ref[...]
        nm = ADAM_B1 * m_ref[...] + (1.0 - ADAM_B1) * gg
        nv = ADAM_B2 * v_ref[...] + (1.0 - ADAM_B2) * (gg * gg)
        m_hat = nm / (1.0 - ADAM_B1 ** ADAM_STEP)
        v_hat = nv / (1.0 - ADAM_B2 ** ADAM_STEP)
        d_ref[...] = -ADAM_LR * (m_hat / (jnp.sqrt(v_hat) + ADAM_EPS) + ADAM_WD * w_ref[...])
        nm_ref[...] = nm
        nv_ref[...] = nv

    blk = pl.BlockSpec((None, tr, cols), lambda l, i: (l, i, 0))
    return pl.pallas_call(
        body, name=name, grid=(n_l, rows // tr),
        in_specs=[blk] * 4, out_specs=[blk] * 3, out_shape=[jax.ShapeDtypeStruct((n_l, rows, cols), f32)] * 3,
        compiler_params=_params(("parallel", "parallel")),
    )(w, g, m, v)


SMALL = ("mix_norm", "q_norm", "k_norm", "sinks", "sgu_norm", "w_s", "b_s", "ffn_norm", "conv_b", "conv_w")


def _pack_small(arrs):
    flat = jnp.concatenate([a.reshape(-1) for a in arrs])
    pad = (-flat.shape[0]) % (16 * 1024)
    return jnp.pad(flat, (0, pad)).reshape(-1, 1024)


def _unpack_small(pack, shapes):
    flat = pack.reshape(-1)
    out, off = [], 0
    for s in shapes:
        n = int(np.prod(s))
        out.append(flat[off:off + n].reshape(s))
        off += n
    return out


def kernel(x, mix_norm, w_in, q_norm, k_norm, sinks, sgu_norm, w_s, b_s, w_oa, w_ob, w_out, ffn_norm, w_up, conv_w, conv_b, w_down, loss_target, m_mix_norm, m_w_in, m_q_norm, m_k_norm, m_sinks, m_sgu_norm, m_w_s, m_b_s, m_w_oa, m_w_ob, m_w_out, m_ffn_norm, m_w_up, m_conv_w, m_conv_b, m_w_down, v_mix_norm, v_w_in, v_q_norm, v_k_norm, v_sinks, v_sgu_norm, v_w_s, v_b_s, v_w_oa, v_w_ob, v_w_out, v_ffn_norm, v_w_up, v_conv_w, v_conv_b, v_w_down):
    weights = dict(mix_norm=mix_norm, w_in=w_in, q_norm=q_norm, k_norm=k_norm, sinks=sinks, sgu_norm=sgu_norm,
                   w_s=w_s, b_s=b_s, w_oa=w_oa, w_ob=w_ob, w_out=w_out, ffn_norm=ffn_norm, w_up=w_up,
                   conv_w=conv_w, conv_b=conv_b, w_down=w_down)
    mom_m = dict(mix_norm=m_mix_norm, w_in=m_w_in, q_norm=m_q_norm, k_norm=m_k_norm, sinks=m_sinks,
                 sgu_norm=m_sgu_norm, w_s=m_w_s, b_s=m_b_s, w_oa=m_w_oa, w_ob=m_w_ob, w_out=m_w_out,
                 ffn_norm=m_ffn_norm, w_up=m_w_up, conv_w=m_conv_w, conv_b=m_conv_b, w_down=m_w_down)
    mom_v = dict(mix_norm=v_mix_norm, w_in=v_w_in, q_norm=v_q_norm, k_norm=v_k_norm, sinks=v_sinks,
                 sgu_norm=v_sgu_norm, w_s=v_w_s, b_s=v_b_s, w_oa=v_w_oa, w_ob=v_w_ob, w_out=v_w_out,
                 ffn_norm=v_ffn_norm, w_up=v_w_up, conv_w=v_conv_w, conv_b=v_conv_b, w_down=v_w_down)
    n_seq, seq, _ = x.shape
    T = n_seq * seq
    core = lax.axis_index("c")
    chip = 2 * lax.axis_index("x") + lax.axis_index("y")
    tm = min(512, seq)
    tm_ff = min(256, seq)
    tk_dw = min(2048, T)

    me = 2 * chip + core
    names = [r[0] for r in REGIONS]
    hrs = {name: rows // N_DEV for name, rows, _, _ in REGIONS}
    placed = [{}, {}]
    for name, rows, cols, transposed in REGIONS:
        shard = (jnp.swapaxes(weights[name], 1, 2) if transposed else weights[name]).reshape(2, 2, hrs[name], cols)
        for l in range(2):
            placed[l][name] = _place(shard, l, core, N_DEV, me, bf16, f"place_{name}_{l}").reshape(rows, cols)
    group_keys = [[(0, "w_in")], [(0, n) for n in names[1:]], [(1, n) for n in names]]
    group_bufs = [[placed[l][n] for l, n in keys] for keys in group_keys]
    group_bufs[0] += [conv_w, jnp.zeros((N_CHIP,) + conv_w.shape, f32)]
    n_direct = [1, 0, 0]
    plans = [_gather_plan([hrs[n] for _, n in keys], nd) for keys, nd in zip(group_keys, n_direct)]
    started, _ = _split_start(list(zip(group_bufs, plans)), "gather_start")
    gathered = [{}, {}]

    def arrived(g, after):
        send, recv, bufs = started[g]
        hr_list = [hrs[n] for _, n in group_keys[g]]
        bufs = _split_wait(bufs, send, recv, _gather_plan(hr_list, n_direct[g]), after, f"gather_wait_{g}")
        return bufs[:len(hr_list)], bufs[len(hr_list):]

    def start_pass(g, bufs):
        (res,), token = _split_start([(bufs, _pass_plan([hrs[n] for _, n in group_keys[g]]))], f"pass_start_{g}")
        return res, token[0:1, 0:1]

    def finish_pass(g, res, after):
        send, recv, bufs = res
        use(g, _split_wait(bufs, send, recv, _pass_plan([hrs[n] for _, n in group_keys[g]]), after, f"pass_wait_{g}"))

    def use(g, bufs):
        for (l, n), b in zip(group_keys[g], bufs):
            gathered[l][n] = b

    xs = x.reshape(T, D_MODEL)
    bufs, (_, conv_w_land) = arrived(0, xs)
    use(0, _pass_to_sibling(bufs, [hrs["w_in"]], "gather_pass_0"))
    conv_w_all = lax.dynamic_update_slice(conv_w_land, conv_w[None], (chip, 0, 0, 0))
    conv_w_full = jnp.concatenate([conv_w_all[j] for j in range(N_CHIP)], axis=-1)
    saved = []
    cur = xs
    for l in range(2):
        wl = gathered[l]
        b_col = b_s[l].reshape(SGU_GROUPS, BLOCK, 1)
        qkv, su, sv, ga, gb, h = _in_proj(cur, mix_norm[l][None], wl["w_in"], l, tm)
        y_att = _attn_fwd(qkv, q_norm[l][None], k_norm[l][None], sinks[l], n_seq, seq)
        sgu_gain = sgu_norm[l][None]
        if l == 0:
            pass_1, tok = start_pass(1, arrived(1, y_att)[0])
            sgu_gain = sgu_gain + tok
        y_sgu = _sgu_fwd(su, sv, sgu_gain, w_s[l], b_col, tm)
        if l == 0:
            finish_pass(1, pass_1, y_sgu)
        x1, merged, a_o, b_o = _merge_fwd(cur, y_att, y_sgu, ga, gb, wl["w_oa"], wl["w_ob"], wl["w_out"], l, tm)
        h2, z, act = _ffn_up(x1, ffn_norm[l][None], wl["w_up"], conv_w_full[l], conv_b[l][None], l, seq, tm_ff)
        saved.append(dict(x=cur, qkv=qkv, su=su, sv=sv, ga=ga, gb=gb, h=h, y_att=y_att, y_sgu=y_sgu, x1=x1,
                          merged=merged, a=a_o, b=b_o, h2=h2, z=z, act=act, b_col=b_col))
        if l == 0:
            pass_2, tok = start_pass(2, arrived(2, act)[0])
            cur = _ffn_down(x1, act, wl["w_down"], tok, l, tm)
            finish_pass(2, pass_2, cur)
        else:
            dy, dyb, loss_part = _ffn_down_loss(x1, act, wl["w_down"], loss_target.reshape(T, D_MODEL), l, tm)
    loss = lax.psum(loss_part[0, 0], ("x", "y", "c"))

    core_arr = core.astype(jnp.int32).reshape(1)
    big = [{}, {}]
    small = {name: [None, None] for name in SMALL}

    def start_pairs(l, keys, tag):
        gl = [big[l][n] for n in keys]
        land = [lax.empty((1, N_CHIP, hrs[n], g.shape[2]), bf16) for n, g in zip(keys, gl)]
        (res,), token = _split_start([(gl + land, _pair_plan([hrs[n] for n in keys]))], f"pair_start_{tag}")
        return (l, keys, res, tag), token[0:1, 0:1]

    def pairs_to_chips(state, after):
        l, keys, (send, recv, bufs), tag = state
        bufs = _split_wait(bufs, send, recv, _pair_plan([hrs[n] for n in keys]), after, f"pair_wait_{tag}")
        return sums_to_chips(l, keys, bufs[:len(keys)], bufs[len(keys):], tag)

    def sums_to_chips(l, keys, gl, from_sibling, tag):
        pairs = [_pair_sum(g, o, core, chip, f"pair_sum_{n}_{l}") for g, o, n in zip(gl, from_sibling, keys)]
        bufs = [p[0] for p in pairs] + [p[1] for p in pairs]
        (res,), token = _split_start([(bufs, _chip_plan(len(keys), 1))], f"chip_start_{tag}")
        return (l, keys, res, tag), token[0:1, 0:1]

    def start_reduce(l, keys, tag):
        gl = [big[l][n] for n in keys]
        return sums_to_chips(l, keys, gl, _pair_exchange(gl, f"pair_exchange_{tag}"), tag)

    def finish_reduce(state, after):
        l, keys, (send, recv, bufs), tag = state
        bufs = _split_wait(bufs, send, recv, _chip_plan(len(keys), 1), after, f"chip_wait_{tag}")
        return {(l, n): _chip_sum(p, core_arr, f"chip_sum_{n}_{l}") for n, p in zip(keys, bufs[len(keys):])}

    rest = [n for n in SMALL if n != "w_s"]
    rest_shapes = [weights[n].shape[1:] if n != "conv_w" else (3, 2 * D_FF) for n in rest]
    zero = jnp.zeros((), jnp.int32)

    def start_small(l):
        packs = [(_pack_small([small[n][l] for n in rest]), f32, "small"), (small["w_s"][l].reshape(-1, 1024), bf16, "w_s")]
        bufs = [_place(p[None, None], 0, zero, N_DEV, me, dt, f"place_{tag}_{l}") for p, dt, tag in packs]
        (res,), token = _split_start([(bufs, _all_to_all_plan(2))], f"small_start_{l}")
        return res, token[0:1, 0:1]

    def finish_small(l, res, after):
        send, recv, bufs = res
        bufs = _split_wait(bufs, send, recv, _all_to_all_plan(2), after, f"small_wait_{l}")
        out = dict(zip(rest, _unpack_small(_sum_small(bufs[0], f"sum_small_{l}"), rest_shapes)))
        out["w_s"] = _sum_small(bufs[1], f"sum_w_s_{l}").reshape(w_s.shape[1:])
        return out

    pending = []
    after_start = jnp.zeros((1, 1), f32)
    for l in (1, 0):
        s = saved[l]
        wl = gathered[l]
        dz, dconv = _ffn_bwd(dyb, s["z"], conv_w_full[l], conv_b[l][None] + after_start, wl["w_down"], l, seq, tm_ff)
        big[l]["w_down"] = _weight_grad(s["act"], dyb, 1408, tk_dw, f"dw_down_{l}")
        big[l]["w_up"] = _weight_grad(dz, s["h2"], 1408, tk_dw, f"dw_up_{l}")
        ffn_gain, sgu_gain, q_gain = ffn_norm[l][None], sgu_norm[l][None], q_norm[l][None]
        if l == 0:
            pairs_a, tok = start_pairs(0, ["w_down", "w_up"], "0a")
            ffn_gain = ffn_gain + tok
        dx1, dx1b, d_ffn = _norm_bwd([dz], wl["w_up"], l, s["x1"], ffn_gain, dy, tm, f"ffn_norm_bwd_{l}")
        if l == 0:
            state, tok = pairs_to_chips(pairs_a, dx1b)
            pending.append(state)
            sgu_gain = sgu_gain + tok
        small["conv_w"][l] = dconv[0:3]
        small["conv_b"][l] = dconv[3]
        small["ffn_norm"][l] = d_ffn[0]
        da, db, dga, dgb, dya, dys = _merge_bwd(dx1b, s["ga"], s["gb"], s["a"], s["b"],
                                                wl["w_oa"], wl["w_ob"], wl["w_out"], l, tm)
        big[l]["w_out"] = _weight_grad(s["merged"], dx1b, 1024, tk_dw, f"dw_out_{l}")
        big[l]["w_oa"] = _weight_grad(da, s["y_att"], 1024, tk_dw, f"dw_oa_{l}")
        big[l]["w_ob"] = _weight_grad(db, s["y_sgu"], 1024, tk_dw, f"dw_ob_{l}")
        if l == 0:
            pairs_m, tok = start_pairs(0, ["w_out", "w_oa", "w_ob"], "0m")
            sgu_gain = sgu_gain + tok
        dsu, dsv, d_ws, d_bs, d_sgu = _sgu_bwd(dys, s["su"], s["sv"], sgu_gain, w_s[l], s["b_col"], tm)
        if l == 0:
            state, tok = pairs_to_chips(pairs_m, dsv)
            pending.append(state)
            q_gain = q_gain + tok
        causal = np.tril(np.ones((BLOCK, BLOCK), bool))
        small["w_s"][l] = jnp.where(causal[None], d_ws, 0.0)
        small["b_s"][l] = d_bs[:, :, 0]
        small["sgu_norm"][l] = d_sgu[0]
        dqkv, d_qg, d_kg, d_sink = _attn_bwd(dya, s["qkv"], q_gain, k_norm[l][None], sinks[l], n_seq, seq)
        small["q_norm"][l] = d_qg[0]
        small["k_norm"][l] = d_kg[0]
        small["sinks"][l] = d_sink[:, 0]
        dproj = [dqkv, dsu, dsv, dga, dgb]
        big[l]["w_in"] = _weight_grad_rows(dproj, s["h"], tm, f"dw_in_{l}")
        if l == 1:
            pairs_1, tok = start_pairs(1, names, "1")
        else:
            state, tok = start_reduce(0, ["w_in"], "0b")
            pending.append(state)
        dy, dyb, d_mix = _norm_bwd(dproj, wl["w_in"], l, s["x"], mix_norm[l][None] + tok, dx1, tm, f"mix_norm_bwd_{l}")
        small["mix_norm"][l] = d_mix[0]
        if l == 1:
            state, tok = pairs_to_chips(pairs_1, dyb)
            pending.append(state)
            small_1, after_start = start_small(1)
            after_start = after_start + tok
    grad_x = dy.reshape(n_seq, seq, D_MODEL)

    small_0, _ = start_small(0)
    halves = {}
    for state in pending:
        halves.update(finish_reduce(state, dyb))
    half_keys = [(l, n) for l in range(2) for n in names]
    shared = dict(zip(half_keys, _share_halves([halves[k] for k in half_keys])))
    grad, delta, new_m, new_v = {}, {}, {}, {}
    flip = lambda a: jnp.swapaxes(a, 1, 2)
    for name, rows, cols, transposed in REGIONS:
        per_layer = [shared[(l, name)].reshape(rows // N_CHIP, cols) for l in range(2)]
        if transposed and weights[name].shape[2] % 128:
            g = jnp.stack(per_layer)
            d, nm, nv = _adamw(flip(weights[name]), g, flip(mom_m[name]), flip(mom_v[name]), f"adamw_{name}")
            grad[name], delta[name], new_m[name], new_v[name] = flip(g), flip(d), flip(nm), flip(nv)
        else:
            grad[name] = jnp.stack([g.T if transposed else g for g in per_layer])
            delta[name], new_m[name], new_v[name] = _adamw(weights[name], grad[name], mom_m[name], mom_v[name],
                                                           f"adamw_{name}")

    per_layer = [finish_small(0, small_0, delta["w_down"]), finish_small(1, small_1, dyb)]
    grad_small = {n: jnp.stack([per_layer[0][n], per_layer[1][n]]) for n in SMALL}
    cw_cols = conv_w.shape[-1]
    grad_small["conv_w"] = lax.dynamic_slice_in_dim(grad_small["conv_w"], chip * cw_cols, cw_cols, axis=2)

    as_rows = lambda a: a.reshape(2, -1, BLOCK)
    d, nm, nv = _adamw(as_rows(w_s), as_rows(grad_small["w_s"]), as_rows(m_w_s), as_rows(v_w_s), "adamw_w_s")
    grad["w_s"], delta["w_s"], new_m["w_s"], new_v["w_s"] = (
        grad_small["w_s"], d.reshape(w_s.shape), nm.reshape(w_s.shape), nv.reshape(w_s.shape))
    shapes = [weights[n].shape for n in rest]
    d, nm, nv = _adamw(_pack_small([weights[n] for n in rest])[None], _pack_small([grad_small[n] for n in rest])[None],
                       _pack_small([mom_m[n] for n in rest])[None], _pack_small([mom_v[n] for n in rest])[None],
                       "adamw_small")
    for n, dd, mm, vv in zip(rest, _unpack_small(d, shapes), _unpack_small(nm, shapes), _unpack_small(nv, shapes)):
        grad[n], delta[n], new_m[n], new_v[n] = grad_small[n], dd, mm, vv

    order = ["mix_norm", "w_in", "q_norm", "k_norm", "sinks", "sgu_norm", "w_s", "b_s", "w_oa", "w_ob", "w_out",
             "ffn_norm", "w_up", "conv_w", "conv_b", "w_down"]
    return (loss, grad_x, *[grad[n] for n in order], *[delta[n] for n in order],
            *[new_m[n] for n in order], *[new_v[n] for n in order])
```

```python
import functools

import numpy as np
import jax
import jax.numpy as jnp
from jax import lax
from jax.experimental import pallas as pl
from jax.experimental.pallas import tpu as pltpu

bf16 = jnp.bfloat16
f32 = jnp.float32

D_MODEL = 1024
ATT_WIDTH = 512
KV_WIDTH = 128
SGU_WIDTH = 512
HEAD_DIM = 64
N_KV_HEADS = 2
Q_GROUP = 4
BLOCK = 128
SGU_GROUPS = 8
IN_WIDTH = 3840
D_FF = 2816
NORM_EPS = 1e-6
NEG_INF = -1e30
N_DEV = 8
N_CHIP = 4

ADAM_LR = 0.001
ADAM_B1 = 0.9
ADAM_B2 = 0.999
ADAM_EPS = 1e-08
ADAM_WD = 0.01
ADAM_STEP = 10

V7X_VMEM_LIMIT = 56 * 1024 * 1024
FF_CHUNK = 2816

REGIONS = (
    ("w_in", IN_WIDTH, D_MODEL, True),
    ("w_oa", D_MODEL, ATT_WIDTH, True),
    ("w_ob", D_MODEL, SGU_WIDTH, True),
    ("w_out", D_MODEL, D_MODEL, False),
    ("w_up", 2 * D_FF, D_MODEL, True),
    ("w_down", D_FF, D_MODEL, False),
)
MESH = pl.DeviceIdType.MESH
ANY = pl.BlockSpec(memory_space=pl.ANY)


def _params(sem=None, **kw):
    return pltpu.CompilerParams(dimension_semantics=sem, vmem_limit_bytes=V7X_VMEM_LIMIT, **kw)


def _wspec(rows, cols, layer=None):
    del layer
    return pl.BlockSpec((rows, cols), lambda *_: (0, 0), pipeline_mode=pl.Buffered(1))


def _full(shape):
    nd = len(shape)
    return pl.BlockSpec(shape, lambda *_: (0,) * nd)


def _dot_nn(a, b):
    return jnp.dot(a, b, preferred_element_type=f32)


def _dot_nt(a, b):
    return lax.dot_general(a, b, (((1,), (1,)), ((), ())), preferred_element_type=f32)


def _dot_tn(a, b):
    return lax.dot_general(a, b, (((0,), (0,)), ((), ())), preferred_element_type=f32)


_GELU_C = float(np.sqrt(2.0 / np.pi))


def _gelu(x):
    return 0.5 * x * (1.0 + jnp.tanh(_GELU_C * (x + 0.044715 * x * x * x)))


def _gelu_grad(x):
    t = jnp.tanh(_GELU_C * (x + 0.044715 * x * x * x))
    du = _GELU_C * (1.0 + 3.0 * 0.044715 * x * x)
    return 0.5 * (1.0 + t) + 0.5 * x * (1.0 - t * t) * du


def _rms(x):
    return lax.rsqrt(jnp.mean(x * x, axis=-1, keepdims=True) + NORM_EPS)


def _mesh_pos():
    return lax.axis_index("x"), lax.axis_index("y"), lax.axis_index("c")


def _in_proj(x, gain, w_in_t, layer, tm):
    T = x.shape[0]

    def body(x_ref, g_ref, w_ref, qkv_ref, su_ref, sv_ref, ga_ref, gb_ref, h_ref):
        xf = x_ref[...]
        h = (xf * _rms(xf) * g_ref[...]).astype(bf16)
        h_ref[...] = h
        qkv_ref[...] = _dot_nt(h, w_ref[0:768, :])
        su_ref[...] = _dot_nt(h, w_ref[768:1280, :]).astype(bf16)
        sv_ref[...] = _dot_nt(h, w_ref[1280:1792, :]).astype(bf16)
        ga_ref[...] = _dot_nt(h, w_ref[1792:2816, :]).astype(bf16)
        gb_ref[...] = _dot_nt(h, w_ref[2816:3840, :]).astype(bf16)

    row = lambda w: pl.BlockSpec((tm, w), lambda i: (i, 0))
    return pl.pallas_call(
        body, name=f"in_proj_{layer}", grid=(T // tm,),
        in_specs=[row(D_MODEL), _full((1, D_MODEL)), _wspec(IN_WIDTH, D_MODEL, layer)],
        out_specs=[row(768), row(512), row(512), row(1024), row(1024), row(D_MODEL)],
        out_shape=[jax.ShapeDtypeStruct((T, 768), f32), jax.ShapeDtypeStruct((T, 512), bf16),
                   jax.ShapeDtypeStruct((T, 512), bf16), jax.ShapeDtypeStruct((T, 1024), bf16),
                   jax.ShapeDtypeStruct((T, 1024), bf16), jax.ShapeDtypeStruct((T, D_MODEL), bf16)],
        compiler_params=_params(("parallel",)),
    )(x, gain, w_in_t)


def _attn_head_group(cur, prev, qg, kg, sink_ref, n, hk):
    lo = hk * HEAD_DIM
    k_raw = jnp.concatenate([prev[:, lo:lo + HEAD_DIM], cur[:, 512 + lo:512 + lo + HEAD_DIM]], axis=0)
    v_band = jnp.concatenate([prev[:, 128 + lo:128 + lo + HEAD_DIM], cur[:, 640 + lo:640 + lo + HEAD_DIM]], axis=0)
    rk = _rms(k_raw)
    k_hat = k_raw * rk
    kn = (k_hat * kg).astype(bf16)
    q_raw = jnp.concatenate(
        [cur[:, (hk * Q_GROUP + g) * HEAD_DIM:(hk * Q_GROUP + g + 1) * HEAD_DIM] for g in range(Q_GROUP)], axis=0)
    rq = _rms(q_raw)
    q_hat = q_raw * rq
    qn = (q_hat * qg * (HEAD_DIM ** -0.5)).astype(bf16)
    s = _dot_nt(qn, kn)
    rows = lax.broadcasted_iota(jnp.int32, (Q_GROUP * BLOCK, 1), 0)
    g_of_row = rows // BLOCK
    qi = rows - g_of_row * BLOCK
    kj = lax.broadcasted_iota(jnp.int32, (1, 2 * BLOCK), 1)
    dist = qi + BLOCK - kj
    valid = (dist >= 0) & (dist < BLOCK) & ((kj >= BLOCK) | (n > 0))
    slope = jnp.zeros((Q_GROUP * BLOCK, 1), f32)
    sink = jnp.zeros((Q_GROUP * BLOCK, 1), f32)
    for g in range(Q_GROUP):
        head = hk * Q_GROUP + g
        slope = jnp.where(g_of_row == g, float(np.exp2(-8.0 * (head + 1.0) / 8.0)), slope)
        sink = jnp.where(g_of_row == g, sink_ref[head], sink)
    s = jnp.where(valid, s - slope * dist.astype(f32), NEG_INF)
    m = jnp.maximum(jnp.max(s, axis=-1, keepdims=True), sink)
    e = jnp.exp(s - m)
    e_sink = jnp.exp(sink - m)
    inv = 1.0 / (jnp.sum(e, axis=-1, keepdims=True) + e_sink)
    return dict(k_raw=k_raw, rk=rk, k_hat=k_hat, kn=kn, v=v_band.astype(bf16), q_hat=q_hat, rq=rq, qn=qn,
                p=e * inv, p_sink=e_sink * inv)


def _attn_fwd(qkv, qg, kg, sinks, n_seq, seq):
    T = n_seq * seq
    nb = seq // BLOCK

    per = 2 if nb % 2 == 0 else 1

    def body(cur_ref, prev_ref, qg_ref, kg_ref, sink_ref, y_ref):
        for sub in range(per):
            n = pl.program_id(1) * per + sub
            cur = cur_ref[sub * BLOCK:(sub + 1) * BLOCK, :]
            prev = prev_ref[...] if sub == 0 else cur_ref[(sub - 1) * BLOCK:sub * BLOCK, 512:768]
            pieces = [None] * (N_KV_HEADS * Q_GROUP)
            for hk in range(N_KV_HEADS):
                a = _attn_head_group(cur, prev, qg_ref[...], kg_ref[...], sink_ref, n, hk)
                o = _dot_nn(a["p"].astype(bf16), a["v"])
                for g in range(Q_GROUP):
                    pieces[hk * Q_GROUP + g] = o[g * BLOCK:(g + 1) * BLOCK]
            y_ref[sub * BLOCK:(sub + 1) * BLOCK, :] = jnp.concatenate(pieces, axis=1).astype(bf16)

    return pl.pallas_call(
        body, name="attn_fwd", grid=(n_seq, nb // per),
        in_specs=[pl.BlockSpec((per * BLOCK, 768), lambda b, n: (b * (nb // per) + n, 0)),
                  pl.BlockSpec((BLOCK, 256), lambda b, n: (b * nb + jnp.maximum(n * per - 1, 0), 2)),
                  _full((1, HEAD_DIM)), _full((1, HEAD_DIM)),
                  pl.BlockSpec(memory_space=pltpu.SMEM)],
        out_specs=pl.BlockSpec((per * BLOCK, ATT_WIDTH), lambda b, n: (b * (nb // per) + n, 0)),
        out_shape=jax.ShapeDtypeStruct((T, ATT_WIDTH), bf16),
        compiler_params=_params(("parallel", "parallel")),
    )(qkv, qkv, qg, kg, sinks)


def _sgu_chunk(su, sv, gain, w_ref, b_ref):
    u = _gelu(su)
    vg = _gelu(sv)
    rv = _rms(vg)
    v_hat = vg * rv
    vn = (v_hat * gain).astype(bf16)
    causal = (lax.broadcasted_iota(jnp.int32, (BLOCK, BLOCK), 0) >= lax.broadcasted_iota(jnp.int32, (BLOCK, BLOCK), 1))
    w_tril = [jnp.where(causal, w_ref[g], 0.0).astype(bf16) for g in range(SGU_GROUPS)]
    gd = SGU_WIDTH // SGU_GROUPS
    mixed = jnp.concatenate(
        [_dot_nn(w_tril[g], vn[:, g * gd:(g + 1) * gd]) + b_ref[g] for g in range(SGU_GROUPS)], axis=1)
    return u, rv, v_hat, vn, w_tril, mixed


def _sgu_fwd(su, sv, gain, w_s, b_s, tm):
    T = su.shape[0]

    def body(su_ref, sv_ref, g_ref, w_ref, b_ref, y_ref):
        for ch in range(tm // BLOCK):
            rows = slice(ch * BLOCK, (ch + 1) * BLOCK)
            u, _, _, _, _, mixed = _sgu_chunk(su_ref[rows, :].astype(f32), sv_ref[rows, :].astype(f32),
                                              g_ref[...], w_ref, b_ref)
            y_ref[rows, :] = (u * mixed).astype(bf16)

    row = pl.BlockSpec((tm, SGU_WIDTH), lambda i: (i, 0))
    return pl.pallas_call(
        body, name="sgu_fwd", grid=(T // tm,),
        in_specs=[row, row, _full((1, SGU_WIDTH)), _full((SGU_GROUPS, BLOCK, BLOCK)), _full((SGU_GROUPS, BLOCK, 1))],
        out_specs=row, out_shape=jax.ShapeDtypeStruct((T, SGU_WIDTH), bf16),
        compiler_params=_params(("parallel",)),
    )(su, sv, gain, w_s, b_s)


def _merge_fwd(x, y_att, y_sgu, ga, gb, w_oa_t, w_ob_t, w_out, layer, tm):
    T = x.shape[0]

    def body(x_ref, ya_ref, ys_ref, ga_ref, gb_ref, woa_ref, wob_ref, wout_ref, x1_ref, m_ref, a_ref, b_ref):
        a = _dot_nt(ya_ref[...], woa_ref[...])
        b = _dot_nt(ys_ref[...], wob_ref[...])
        a_ref[...] = a.astype(bf16)
        b_ref[...] = b.astype(bf16)
        merged = (jax.nn.sigmoid(ga_ref[...].astype(f32)) * a + jax.nn.sigmoid(gb_ref[...].astype(f32)) * b).astype(bf16)
        m_ref[...] = merged
        x1_ref[...] = x_ref[...] + _dot_nn(merged, wout_ref[...])

    row = lambda w: pl.BlockSpec((tm, w), lambda i: (i, 0))
    return pl.pallas_call(
        body, name=f"merge_fwd_{layer}", grid=(T // tm,),
        in_specs=[row(D_MODEL), row(512), row(512), row(1024), row(1024),
                  _wspec(D_MODEL, ATT_WIDTH, layer), _wspec(D_MODEL, SGU_WIDTH, layer), _wspec(D_MODEL, D_MODEL, layer)],
        out_specs=[row(D_MODEL)] * 4,
        out_shape=[jax.ShapeDtypeStruct((T, D_MODEL), f32)] + [jax.ShapeDtypeStruct((T, D_MODEL), bf16)] * 3,
        compiler_params=_params(("parallel",)),
    )(x, y_att, y_sgu, ga, gb, w_oa_t, w_ob_t, w_out)


def _tile_permutation(tm):
    r = np.arange(tm)
    p = np.zeros((tm, tm), np.float32)
    p[r, (r % 8) * (tm // 8) + r // 8] = 1.0
    return jnp.asarray(p, bf16), jnp.asarray(p.T, bf16)


def _stage_taps_before(buf, zz, prev, tm):
    first = lax.broadcasted_iota(jnp.int32, (8, 1), 0) == 0
    buf[16:16 + tm, :] = zz
    buf[0:8, :] = jnp.where(first, prev[7:8], pltpu.roll(buf[tm:tm + 8, :], 1, 0))
    buf[8:16, :] = jnp.where(first, prev[15:16], pltpu.roll(buf[tm + 8:tm + 16, :], 1, 0))


def _stage_taps_after(buf, nxt, tm):
    last = lax.broadcasted_iota(jnp.int32, (8, 1), 0) == 7
    buf[tm:tm + 8, :] = jnp.where(last, nxt[0:1], pltpu.roll(buf[0:8, :], 7, 0))
    buf[tm + 8:tm + 16, :] = jnp.where(last, nxt[8:9], pltpu.roll(buf[8:16, :], 7, 0))


def _conv_rows(buf, r, n, coef):
    z2 = buf[pl.ds(r, n), :]
    z1 = buf[pl.ds(pl.multiple_of(r + 8, 8), n), :]
    z0 = buf[pl.ds(pl.multiple_of(r + 16, 8), n), :]
    return coef[0] + coef[1] * z2 + coef[2] * z1 + coef[3] * z0


def _ffn_up(x1, gain, w_up_t, conv_w, conv_b, layer, seq, tm):
    T = x1.shape[0]
    tps = seq // tm
    perm, perm_t = _tile_permutation(tm)

    rg = 16

    def body(x_ref, g_ref, w_ref, cw_ref, cb_ref, p_ref, pt_ref, h2_ref, z_ref, act_ref, carry_ref,
             zg_buf, zv_buf, actp_buf):
        i = pl.program_id(0)

        @pl.when(i % tps == 0)
        def _():
            carry_ref[...] = jnp.zeros_like(carry_ref)

        xf = x_ref[...]
        h2 = (xf * _rms(xf) * g_ref[...]).astype(bf16)
        h2_ref[...] = h2
        h2p = _dot_nn(p_ref[...], h2).astype(bf16)
        for cc in range(D_FF // FF_CHUNK):
            cols_g = slice(cc * FF_CHUNK, (cc + 1) * FF_CHUNK)
            cols_v = slice(D_FF + cc * FF_CHUNK, D_FF + (cc + 1) * FF_CHUNK)
            for buf, cols in ((zg_buf, cols_g), (zv_buf, cols_v)):
                zb = _dot_nt(h2p, w_ref[cols, :]).astype(bf16)
                z_ref[:, cols] = zb
                _stage_taps_before(buf, zb.astype(f32), carry_ref[:, cols], tm)
                carry_ref[:, cols] = buf[tm:tm + 16, :]
            coef = [jnp.broadcast_to(v, (rg, FF_CHUNK)) for cols in (cols_g, cols_v)
                    for v in (cb_ref[:, cols], cw_ref[0:1, cols], cw_ref[1:2, cols], cw_ref[2:3, cols])]

            def rows_step(j, carry, coef=coef):
                r = pl.multiple_of(j * rg, rg)
                zcg, zcv = (_conv_rows(buf, r, rg, coef[4 * k:4 * k + 4]) for k, buf in enumerate((zg_buf, zv_buf)))
                actp_buf[pl.ds(r, rg), :] = (zcg * jax.nn.sigmoid(zcg) * zcv).astype(bf16)
                return carry

            lax.fori_loop(0, tm // rg, rows_step, 0, unroll=True)
            act_ref[:, cols_g] = _dot_nn(pt_ref[...], actp_buf[...]).astype(bf16)

    row = lambda w: pl.BlockSpec((tm, w), lambda i: (i, 0))
    return pl.pallas_call(
        body, name=f"ffn_up_{layer}", grid=(T // tm,),
        in_specs=[row(D_MODEL), _full((1, D_MODEL)), _wspec(2 * D_FF, D_MODEL, layer),
                  _full((3, 2 * D_FF)), _full((1, 2 * D_FF)), _full((tm, tm)), _full((tm, tm))],
        out_specs=[row(D_MODEL), row(2 * D_FF), row(D_FF)],
        out_shape=[jax.ShapeDtypeStruct((T, D_MODEL), bf16), jax.ShapeDtypeStruct((T, 2 * D_FF), bf16),
                   jax.ShapeDtypeStruct((T, D_FF), bf16)],
        scratch_shapes=[pltpu.VMEM((16, 2 * D_FF), f32), pltpu.VMEM((tm + 16, FF_CHUNK), f32),
                        pltpu.VMEM((tm + 16, FF_CHUNK), f32), pltpu.VMEM((tm, FF_CHUNK), bf16)],
        compiler_params=_params(("arbitrary",)),
    )(x1, gain, w_up_t, conv_w, conv_b, perm, perm_t)


def _ffn_down(x1, act, w_down, after, layer, tm):
    T = x1.shape[0]

    def body(x_ref, a_ref, w_ref, after_ref, o_ref):
        o_ref[...] = x_ref[...] + _dot_nn(a_ref[...], w_ref[...])

    row = lambda w: pl.BlockSpec((tm, w), lambda i: (i, 0))
    return pl.pallas_call(
        body, name=f"ffn_down_{layer}", grid=(T // tm,),
        in_specs=[row(D_MODEL), row(D_FF), _wspec(D_FF, D_MODEL, layer), _full((1, 1))],
        out_specs=row(D_MODEL), out_shape=jax.ShapeDtypeStruct((T, D_MODEL), f32),
        compiler_params=_params(("parallel",)),
    )(x1, act, w_down, after)


def _ffn_down_loss(x1, act, w_down, target, layer, tm):
    T = x1.shape[0]

    def body(x_ref, a_ref, w_ref, t_ref, dy_ref, dyb_ref, loss_ref):
        @pl.when(pl.program_id(0) == 0)
        def _():
            loss_ref[...] = jnp.zeros_like(loss_ref)

        diff = x_ref[...] + _dot_nn(a_ref[...], w_ref[...]) - t_ref[...]
        loss_ref[...] += 0.5 * jnp.sum(jnp.mean(diff * diff, axis=-1, keepdims=True), axis=0, keepdims=True)
        dy = diff * (1.0 / D_MODEL)
        dy_ref[...] = dy
        dyb_ref[...] = dy.astype(bf16)

    row = lambda w: pl.BlockSpec((tm, w), lambda i: (i, 0))
    return pl.pallas_call(
        body, name=f"ffn_down_loss_{layer}", grid=(T // tm,),
        in_specs=[row(D_MODEL), row(D_FF), _wspec(D_FF, D_MODEL, layer), row(D_MODEL)],
        out_specs=[row(D_MODEL), row(D_MODEL), _full((8, 128))],
        out_shape=[jax.ShapeDtypeStruct((T, D_MODEL), f32), jax.ShapeDtypeStruct((T, D_MODEL), bf16),
                   jax.ShapeDtypeStruct((8, 128), f32)],
        compiler_params=_params(("arbitrary",)),
    )(x1, act, w_down, target)


def _ffn_bwd(dx2b, z, conv_w, conv_b, w_down, layer, seq, tm):
    T = z.shape[0]
    nt = T // tm
    tps = seq // tm

    perm, perm_t = _tile_permutation(tm)

    def body(dx_ref, z_ref, zh_ref, cw_ref, cb_ref, wd_ref, p_ref, pt_ref, dz_ref, dconv_ref, carry_ref,
             zg_buf, zv_buf, gg_buf, gv_buf, dact_buf, dzp_buf):
        i = pl.program_id(0)
        pos = (nt - 1 - i) % tps

        @pl.when(i == 0)
        def _():
            dconv_ref[...] = jnp.zeros_like(dconv_ref)

        @pl.when(pos == tps - 1)
        def _():
            carry_ref[...] = jnp.zeros_like(carry_ref)

        dxp = _dot_nn(p_ref[...], dx_ref[...]).astype(bf16)
        halo_on = (pos > 0).astype(f32)
        for cc in range(D_FF // FF_CHUNK):
            cols_g = slice(cc * FF_CHUNK, (cc + 1) * FF_CHUNK)
            cols_v = slice(D_FF + cc * FF_CHUNK, D_FF + (cc + 1) * FF_CHUNK)
            for buf, cols in ((zg_buf, cols_g), (zv_buf, cols_v)):
                _stage_taps_before(buf, z_ref[:, cols].astype(f32), zh_ref[:, cols].astype(f32) * halo_on, tm)
            dact_buf[...] = _dot_nt(dxp, wd_ref[cols_g, :])
            coef = [jnp.broadcast_to(v, (8, FF_CHUNK)) for cols in (cols_g, cols_v)
                    for v in (cb_ref[:, cols], cw_ref[0:1, cols], cw_ref[1:2, cols], cw_ref[2:3, cols])]

            def first_pass(j, sums, coef=coef):
                r = pl.multiple_of(j * 8, 8)
                rows = pl.ds(r, 8)
                zcg = _conv_rows(zg_buf, r, 8, coef[0:4])
                zcv = _conv_rows(zv_buf, r, 8, coef[4:8])
                sg = jax.nn.sigmoid(zcg)
                silu = zcg * sg
                d_act = dact_buf[rows, :]
                dg = d_act * zcv * sg * (1.0 + zcg * (1.0 - sg))
                dv = d_act * silu
                gg_buf[rows, :] = dg
                gv_buf[rows, :] = dv
                out = []
                for k, (g, buf) in enumerate(((dg, zg_buf), (dv, zv_buf))):
                    out += [sums[4 * k] + g * buf[rows, :],
                            sums[4 * k + 1] + g * buf[pl.ds(pl.multiple_of(r + 8, 8), 8), :],
                            sums[4 * k + 2] + g * buf[pl.ds(pl.multiple_of(r + 16, 8), 8), :],
                            sums[4 * k + 3] + g]
                return tuple(out)

            sums = lax.fori_loop(0, tm // 8, first_pass, tuple(jnp.zeros((8, FF_CHUNK), f32) for _ in range(8)),
                                 unroll=True)
            for k, cols in enumerate((cols_g, cols_v)):
                for tap in range(4):
                    dconv_ref[tap:tap + 1, cols] += jnp.sum(sums[4 * k + tap], axis=0, keepdims=True)
            for buf, cols in ((gg_buf, cols_g), (gv_buf, cols_v)):
                _stage_taps_after(buf, carry_ref[:, cols], tm)
                carry_ref[:, cols] = buf[0:16, :]
                w0, w1, w2 = (jnp.broadcast_to(cw_ref[k:k + 1, cols], (16, FF_CHUNK)) for k in range(3))

                def second_pass(j, carry, buf=buf, w0=w0, w1=w1, w2=w2):
                    r = pl.multiple_of(j * 16, 16)
                    dzp_buf[pl.ds(r, 16), :] = (w2 * buf[pl.ds(r, 16), :] + w1 * buf[pl.ds(pl.multiple_of(r + 8, 8), 16), :]
                                                + w0 * buf[pl.ds(pl.multiple_of(r + 16, 16), 16), :]).astype(bf16)
                    return carry

                lax.fori_loop(0, tm // 16, second_pass, 0, unroll=True)
                dz_ref[:, cols] = _dot_nn(pt_ref[...], dzp_buf[...]).astype(bf16)

    rev = lambda w: pl.BlockSpec((tm, w), lambda i: (nt - 1 - i, 0))
    return pl.pallas_call(
        body, name=f"ffn_bwd_{layer}", grid=(nt,),
        in_specs=[rev(D_MODEL), rev(2 * D_FF),
                  pl.BlockSpec((16, 2 * D_FF), lambda i: (jnp.maximum((nt - 1 - i) * (tm // 16) - 1, 0), 0)),
                  _full((3, 2 * D_FF)), _full((1, 2 * D_FF)), _wspec(D_FF, D_MODEL, layer),
                  _full((tm, tm)), _full((tm, tm))],
        out_specs=[rev(2 * D_FF), _full((8, 2 * D_FF))],
        out_shape=[jax.ShapeDtypeStruct((T, 2 * D_FF), bf16), jax.ShapeDtypeStruct((8, 2 * D_FF), f32)],
        scratch_shapes=[pltpu.VMEM((16, 2 * D_FF), f32)] + [pltpu.VMEM((tm + 16, FF_CHUNK), f32)] * 4
        + [pltpu.VMEM((tm, FF_CHUNK), f32), pltpu.VMEM((tm, FF_CHUNK), bf16)],
        compiler_params=_params(("arbitrary",)),
    )(dx2b, z, z, conv_w, conv_b, w_down, perm, perm_t)


def _norm_bwd(dys, w, layer, x, gain, dres, tm, name):
    T = dys[0].shape[0]
    widths = [d.shape[1] for d in dys]
    K = sum(widths)
    n = len(dys)

    def body(*refs):
        dy_refs = refs[:n]
        w_ref, x_ref, g_ref, dres_ref, dx_ref, dxb_ref, dg_ref = refs[n:]

        @pl.when(pl.program_id(0) == 0)
        def _():
            dg_ref[...] = jnp.zeros_like(dg_ref)

        dh, lo = None, 0
        for dy_ref, wd in zip(dy_refs, widths):
            part = _dot_nn(dy_ref[...], w_ref[lo:lo + wd, :])
            dh = part if dh is None else dh + part
            lo += wd
        xf = x_ref[...]
        r = _rms(xf)
        x_hat = xf * r
        dg_ref[...] += jnp.sum(dh * x_hat, axis=0, keepdims=True)
        dxh = dh * g_ref[...]
        dx = dres_ref[...] + r * (dxh - x_hat * jnp.mean(dxh * x_hat, axis=-1, keepdims=True))
        dx_ref[...] = dx
        dxb_ref[...] = dx.astype(bf16)

    row = lambda w_: pl.BlockSpec((tm, w_), lambda i: (i, 0))
    return pl.pallas_call(
        body, name=name, grid=(T // tm,),
        in_specs=[row(wd) for wd in widths] + [_wspec(K, D_MODEL, layer), row(D_MODEL), _full((1, D_MODEL)), row(D_MODEL)],
        out_specs=[row(D_MODEL), row(D_MODEL), _full((1, D_MODEL))],
        out_shape=[jax.ShapeDtypeStruct((T, D_MODEL), f32), jax.ShapeDtypeStruct((T, D_MODEL), bf16),
                   jax.ShapeDtypeStruct((1, D_MODEL), f32)],
        compiler_params=_params(("arbitrary",)),
    )(*dys, w, x, gain, dres)


def _merge_bwd(dx1b, ga, gb, a, b, w_oa_t, w_ob_t, w_out, layer, tm):
    T = dx1b.shape[0]

    def body(dx_ref, ga_ref, gb_ref, a_ref, b_ref, woa_ref, wob_ref, wout_ref,
             da_ref, db_ref, dga_ref, dgb_ref, dya_ref, dys_ref):
        dm = _dot_nt(dx_ref[...], wout_ref[...])
        sa = jax.nn.sigmoid(ga_ref[...].astype(f32))
        sb = jax.nn.sigmoid(gb_ref[...].astype(f32))
        da = (dm * sa).astype(bf16)
        db = (dm * sb).astype(bf16)
        da_ref[...] = da
        db_ref[...] = db
        dga_ref[...] = (dm * a_ref[...].astype(f32) * sa * (1.0 - sa)).astype(bf16)
        dgb_ref[...] = (dm * b_ref[...].astype(f32) * sb * (1.0 - sb)).astype(bf16)
        dya_ref[...] = _dot_nn(da, woa_ref[...]).astype(bf16)
        dys_ref[...] = _dot_nn(db, wob_ref[...]).astype(bf16)

    row = lambda w: pl.BlockSpec((tm, w), lambda i: (i, 0))
    return pl.pallas_call(
        body, name=f"merge_bwd_{layer}", grid=(T // tm,),
        in_specs=[row(D_MODEL)] * 5 + [_wspec(D_MODEL, ATT_WIDTH, layer), _wspec(D_MODEL, SGU_WIDTH, layer),
                                       _wspec(D_MODEL, D_MODEL, layer)],
        out_specs=[row(D_MODEL)] * 4 + [row(512)] * 2,
        out_shape=[jax.ShapeDtypeStruct((T, D_MODEL), bf16)] * 4 + [jax.ShapeDtypeStruct((T, 512), bf16)] * 2,
        compiler_params=_params(("parallel",)),
    )(dx1b, ga, gb, a, b, w_oa_t, w_ob_t, w_out)


def _sgu_bwd(dy, su, sv, gain, w_s, b_s, tm):
    T = su.shape[0]
    gd = SGU_WIDTH // SGU_GROUPS

    def body(dy_ref, su_ref, sv_ref, g_ref, w_ref, b_ref, dsu_ref, dsv_ref, dw_ref, db_ref, dg_ref):
        @pl.when(pl.program_id(0) == 0)
        def _():
            dw_ref[...] = jnp.zeros_like(dw_ref)
            db_ref[...] = jnp.zeros_like(db_ref)
            dg_ref[...] = jnp.zeros_like(dg_ref)

        gain_v = g_ref[...]
        for ch in range(tm // BLOCK):
            rows = slice(ch * BLOCK, (ch + 1) * BLOCK)
            su_c = su_ref[rows, :].astype(f32)
            sv_c = sv_ref[rows, :].astype(f32)
            u, rv, v_hat, vn, w_tril, mixed = _sgu_chunk(su_c, sv_c, gain_v, w_ref, b_ref)
            dyc = dy_ref[rows, :].astype(f32)
            dsu_ref[rows, :] = (dyc * mixed * _gelu_grad(su_c)).astype(bf16)
            dmix = dyc * u
            dmix_b = dmix.astype(bf16)
            dvn = []
            for g in range(SGU_GROUPS):
                gs = slice(g * gd, (g + 1) * gd)
                db_ref[g] += jnp.sum(dmix[:, gs], axis=1, keepdims=True)
                dw_ref[g] += _dot_nt(dmix_b[:, gs], vn[:, gs])
                dvn.append(_dot_tn(w_tril[g], dmix_b[:, gs]))
            dvn = jnp.concatenate(dvn, axis=1)
            dg_ref[...] += jnp.sum(dvn * v_hat, axis=0, keepdims=True)
            dxh = dvn * gain_v
            dvg = rv * (dxh - v_hat * jnp.mean(dxh * v_hat, axis=-1, keepdims=True))
            dsv_ref[rows, :] = (dvg * _gelu_grad(sv_c)).astype(bf16)

    row = pl.BlockSpec((tm, SGU_WIDTH), lambda i: (i, 0))
    return pl.pallas_call(
        body, name="sgu_bwd", grid=(T // tm,),
        in_specs=[row, row, row, _full((1, SGU_WIDTH)), _full((SGU_GROUPS, BLOCK, BLOCK)),
                  _full((SGU_GROUPS, BLOCK, 1))],
        out_specs=[row, row, _full((SGU_GROUPS, BLOCK, BLOCK)), _full((SGU_GROUPS, BLOCK, 1)), _full((1, SGU_WIDTH))],
        out_shape=[jax.ShapeDtypeStruct((T, SGU_WIDTH), bf16)] * 2 + [
            jax.ShapeDtypeStruct((SGU_GROUPS, BLOCK, BLOCK), f32), jax.ShapeDtypeStruct((SGU_GROUPS, BLOCK, 1), f32),
            jax.ShapeDtypeStruct((1, SGU_WIDTH), f32)],
        compiler_params=_params(("arbitrary",)),
    )(dy, su, sv, gain, w_s, b_s)


def _attn_bwd(dy, qkv, qg, kg, sinks, n_seq, seq):
    T = n_seq * seq
    nb = seq // BLOCK
    scale = HEAD_DIM ** -0.5

    def body(dy_ref, cur_ref, prev_ref, qg_ref, kg_ref, sink_ref, dqkv_ref, dqg_ref, dkg_ref, dsink_ref,
             carry_k, carry_v):
        b = pl.program_id(0)
        j = pl.program_id(1)
        n = nb - 1 - j

        @pl.when((b == 0) & (j == 0))
        def _():
            dqg_ref[...] = jnp.zeros_like(dqg_ref)
            dkg_ref[...] = jnp.zeros_like(dkg_ref)
            dsink_ref[...] = jnp.zeros_like(dsink_ref)

        @pl.when(j == 0)
        def _():
            carry_k[...] = jnp.zeros_like(carry_k)
            carry_v[...] = jnp.zeros_like(carry_v)

        cur = cur_ref[...]
        prev = prev_ref[...]
        dyf = dy_ref[...].astype(f32)
        qg_v = qg_ref[...]
        kg_v = kg_ref[...]
        dq_pieces = [None] * (N_KV_HEADS * Q_GROUP)
        dk_pieces, dv_pieces = [], []
        for hk in range(N_KV_HEADS):
            a = _attn_head_group(cur, prev, qg_v, kg_v, sink_ref, n, hk)
            do = jnp.concatenate(
                [dyf[:, (hk * Q_GROUP + g) * HEAD_DIM:(hk * Q_GROUP + g + 1) * HEAD_DIM] for g in range(Q_GROUP)],
                axis=0).astype(bf16)
            p = a["p"]
            dp = _dot_nt(do, a["v"])
            dv_band = _dot_tn(p.astype(bf16), do)
            dsum = jnp.sum(p * dp, axis=-1, keepdims=True)
            ds = (p * (dp - dsum)).astype(bf16)
            dsink_col = -a["p_sink"] * dsum
            for g in range(Q_GROUP):
                head = hk * Q_GROUP + g
                dsink_ref[head:head + 1, :] += jnp.sum(dsink_col[g * BLOCK:(g + 1) * BLOCK], axis=0, keepdims=True)
            dqn = _dot_nn(ds, a["kn"])
            dkn_band = _dot_tn(ds, a["qn"])
            dq_hat_g = dqn * scale
            dqg_ref[...] += jnp.sum(dq_hat_g * a["q_hat"], axis=0, keepdims=True)
            dxh = dq_hat_g * qg_v
            dq = a["rq"] * (dxh - a["q_hat"] * jnp.mean(dxh * a["q_hat"], axis=-1, keepdims=True))
            for g in range(Q_GROUP):
                dq_pieces[hk * Q_GROUP + g] = dq[g * BLOCK:(g + 1) * BLOCK]
            dkn = dkn_band[BLOCK:] + carry_k[hk]
            dv_pieces.append(dv_band[BLOCK:] + carry_v[hk])
            carry_k[hk] = dkn_band[:BLOCK]
            carry_v[hk] = dv_band[:BLOCK]
            k_hat = a["k_hat"][BLOCK:]
            dkg_ref[...] += jnp.sum(dkn * k_hat, axis=0, keepdims=True)
            dxk = dkn * kg_v
            dk_pieces.append(a["rk"][BLOCK:] * (dxk - k_hat * jnp.mean(dxk * k_hat, axis=-1, keepdims=True)))
        dqkv_ref[...] = jnp.concatenate(dq_pieces + dk_pieces + dv_pieces, axis=1).astype(bf16)

    blk = lambda w: pl.BlockSpec((BLOCK, w), lambda b, j: (b * nb + nb - 1 - j, 0))
    return pl.pallas_call(
        body, name="attn_bwd", grid=(n_seq, nb),
        in_specs=[blk(ATT_WIDTH), blk(768),
                  pl.BlockSpec((BLOCK, 256), lambda b, j: (b * nb + jnp.maximum(nb - 2 - j, 0), 2)),
                  _full((1, HEAD_DIM)), _full((1, HEAD_DIM)), pl.BlockSpec(memory_space=pltpu.SMEM)],
        out_specs=[blk(768), _full((1, HEAD_DIM)), _full((1, HEAD_DIM)), _full((8, 128))],
        out_shape=[jax.ShapeDtypeStruct((T, 768), bf16), jax.ShapeDtypeStruct((1, HEAD_DIM), f32),
                   jax.ShapeDtypeStruct((1, HEAD_DIM), f32), jax.ShapeDtypeStruct((8, 128), f32)],
        scratch_shapes=[pltpu.VMEM((N_KV_HEADS, BLOCK, HEAD_DIM), f32), pltpu.VMEM((N_KV_HEADS, BLOCK, HEAD_DIM), f32)],
        compiler_params=_params(("arbitrary", "arbitrary")),
    )(dy, qkv, qkv, qg, kg, sinks)


def _weight_grad(a, b, tm, tk, name):
    T, M = a.shape
    N = b.shape[1]
    nk = T // tk

    def body(a_ref, b_ref, o_ref, acc_ref):
        k = pl.program_id(1)

        @pl.when(k == 0)
        def _():
            acc_ref[...] = jnp.zeros_like(acc_ref)

        acc_ref[...] += _dot_tn(a_ref[...], b_ref[...])

        @pl.when(k == nk - 1)
        def _():
            o_ref[...] = acc_ref[...].astype(bf16)

    return pl.pallas_call(
        body, name=name, grid=(M // tm, nk),
        in_specs=[pl.BlockSpec((tk, tm), lambda i, k: (k, i)), pl.BlockSpec((tk, N), lambda i, k: (k, 0))],
        out_specs=pl.BlockSpec((None, tm, N), lambda i, k: (0, i, 0)),
        out_shape=jax.ShapeDtypeStruct((1, M, N), bf16),
        scratch_shapes=[pltpu.VMEM((tm, N), f32)],
        compiler_params=_params(("parallel", "arbitrary")),
    )(a, b)


def _weight_grad_rows(a_list, b, tk, name):
    T, N = b.shape
    widths = [a.shape[1] for a in a_list]
    M = sum(widths)
    nk = T // tk
    n = len(a_list)

    def body(*refs):
        a_refs = refs[:n]
        b_ref, o_ref, acc_ref = refs[n:]
        k = pl.program_id(0)

        @pl.when(k == 0)
        def _():
            acc_ref[...] = jnp.zeros_like(acc_ref)

        lo = 0
        for a_ref, wd in zip(a_refs, widths):
            acc_ref[lo:lo + wd, :] += _dot_tn(a_ref[...], b_ref[...])
            lo += wd

        @pl.when(k == nk - 1)
        def _():
            o_ref[...] = acc_ref[...].astype(bf16)

    return pl.pallas_call(
        body, name=name, grid=(nk,),
        in_specs=[pl.BlockSpec((tk, wd), lambda k: (k, 0)) for wd in widths] + [pl.BlockSpec((tk, N), lambda k: (k, 0))],
        out_specs=pl.BlockSpec((None, M, N), lambda k: (0, 0, 0)),
        out_shape=jax.ShapeDtypeStruct((1, M, N), bf16),
        scratch_shapes=[pltpu.VMEM((M, N), f32)],
        compiler_params=_params(("arbitrary",)),
    )(*a_list, b)


def _place(src, layer, src_slot, n_slots, dst_slot, dtype, name, after=None):
    _, _, rows, cols = src.shape
    slots = jnp.stack([src_slot, dst_slot]).astype(jnp.int32)

    def body(slots_ref, s_ref, *rest):
        rest[-1][...] = s_ref[...].astype(dtype)

    return pl.pallas_call(
        body, name=name,
        grid_spec=pltpu.PrefetchScalarGridSpec(
            num_scalar_prefetch=1, grid=(1,),
            in_specs=[pl.BlockSpec((None, None, rows, cols), lambda i, sl: (layer, sl[0], 0, 0))]
            + ([] if after is None else [ANY]),
            out_specs=pl.BlockSpec((None, rows, cols), lambda i, sl: (sl[1], 0, 0))),
        out_shape=jax.ShapeDtypeStruct((n_slots, rows, cols), dtype),
        compiler_params=_params(("arbitrary",)),
    )(slots, src, *([] if after is None else [after]))


HBM = pl.BlockSpec(memory_space=pltpu.HBM)
SEM = pl.BlockSpec(memory_space=pltpu.SEMAPHORE)
DATAFLOW = pltpu.SideEffectType.DATAFLOW_SIDE_EFFECTING


def _other_chips(x, y):
    return [(1 - x, y), (x, 1 - y), (1 - x, 1 - y)]


def _split_start(groups, name):
    nb = [len(bufs) for bufs, _ in groups]
    flat = [b for bufs, _ in groups for b in bufs]
    ns = [len(plan(bufs, dry=True)) for bufs, plan in groups]
    ng = len(groups)

    def body(*refs):
        n_in = len(flat)
        sems = refs[n_in:n_in + 2 * ng]
        thru = refs[n_in + 2 * ng:2 * n_in + 2 * ng]
        token = refs[2 * n_in + 2 * ng]
        off = 0
        for g, (bufs, plan) in enumerate(groups):
            mine = thru[off:off + nb[g]]
            off += nb[g]
            for k, (src, dst, to) in enumerate(plan(mine)):
                pltpu.make_async_remote_copy(
                    src_ref=src, dst_ref=dst, send_sem=sems[2 * g].at[k], recv_sem=sems[2 * g + 1].at[k],
                    device_id=to, device_id_type=MESH).start()
        token[...] = jnp.zeros_like(token)

    out_shape = []
    for n in ns:
        out_shape += [pltpu.SemaphoreType.DMA((n,)), pltpu.SemaphoreType.DMA((n,))]
    out_shape += [pltpu.HBM(b.shape, b.dtype) for b in flat]
    out_shape.append(jax.ShapeDtypeStruct((8, 128), f32))
    res = pl.pallas_call(
        body, name=name, out_shape=tuple(out_shape),
        in_specs=[HBM] * len(flat),
        out_specs=tuple([SEM] * (2 * ng) + [HBM] * len(flat) + [pl.BlockSpec(memory_space=pltpu.VMEM)]),
        input_output_aliases={i: 2 * ng + i for i in range(len(flat))},
        compiler_params=pltpu.CompilerParams(has_side_effects=DATAFLOW),
    )(*[pltpu.with_memory_space_constraint(b, pltpu.HBM) for b in flat])
    out, off = [], 2 * ng
    for g in range(ng):
        out.append((res[2 * g], res[2 * g + 1], list(res[off:off + nb[g]])))
        off += nb[g]
    return out, res[-1]


def _split_wait(bufs, send, recv, plan, after, name):
    nb = len(bufs)

    def body(*refs):
        thru = refs[:nb]
        send_ref, recv_ref = refs[nb], refs[nb + 1]
        for k, (src, dst, to) in enumerate(plan(thru)):
            cp = pltpu.make_async_remote_copy(
                src_ref=src, dst_ref=dst, send_sem=send_ref.at[k], recv_sem=recv_ref.at[k],
                device_id=to, device_id_type=MESH)
            cp.wait_send()
            cp.wait_recv()

    res = pl.pallas_call(
        body, name=name, out_shape=tuple(pltpu.HBM(b.shape, b.dtype) for b in bufs),
        in_specs=[HBM] * nb + [SEM, SEM, ANY], out_specs=tuple([HBM] * nb),
        input_output_aliases={i: i for i in range(nb)},
        compiler_params=pltpu.CompilerParams(has_side_effects=DATAFLOW),
    )(*bufs, send, recv, after)
    return list(res)


def _gather_plan(hrs, n_direct=0):
    def plan(refs, dry=False):
        if dry:
            return [None] * (4 * len(hrs) + 3 * n_direct)
        x, y, c = _mesh_pos()
        me = 4 * x + 2 * y + c
        out = []
        for i in range(n_direct):
            src, land = refs[len(hrs) + 2 * i], refs[len(hrs) + 2 * i + 1]
            out += [(src, land.at[2 * x + y], (*chip, c)) for chip in _other_chips(x, y)]
        for ref, hr in zip(refs, hrs):
            rows = ref.at[pl.ds(pl.multiple_of(me * hr, 16), hr), :]
            out.append((rows, rows, (x, y, 1 - c)))
            out += [(rows, rows, (*chip, c)) for chip in _other_chips(x, y)]
        return out
    return plan


def _pass_plan(hrs):
    def plan(refs, dry=False):
        if dry:
            return [None] * (3 * len(hrs))
        x, y, c = _mesh_pos()
        out = []
        for ref, hr in zip(refs, hrs):
            for chip in _other_chips(x, y):
                rows = ref.at[pl.ds(pl.multiple_of((4 * chip[0] + 2 * chip[1] + c) * hr, 16), hr), :]
                out.append((rows, rows, (x, y, 1 - c)))
        return out
    return plan


def _pair_plan(hrs):
    n = len(hrs)

    def plan(refs, dry=False):
        if dry:
            return [None] * (N_CHIP * n)
        x, y, c = _mesh_pos()
        out = []
        for r in range(n):
            for j in range(N_CHIP):
                start = pl.multiple_of((2 * j + 1 - c) * hrs[r], 16)
                out.append((refs[r].at[0, pl.ds(start, hrs[r]), :], refs[n + r].at[0, j], (x, y, 1 - c)))
        return out
    return plan


def _all_to_all_plan(n):
    def plan(refs, dry=False):
        if dry:
            return [None] * (7 * n)
        x, y, c = _mesh_pos()
        out = []
        for ref in refs:
            mine = ref.at[4 * x + 2 * y + c]
            for fx in range(2):
                for fy in range(2):
                    for fc in range(2):
                        if fx or fy or fc:
                            out.append((mine, mine, (1 - x if fx else x, 1 - y if fy else y, 1 - c if fc else c)))
        return out
    return plan


def _pass_to_sibling(bufs, hrs, name):
    nb = len(bufs)

    def body(*refs):
        out = refs[nb:2 * nb]
        send, recv = refs[2 * nb:]
        x, y, c = _mesh_pos()
        chips = _other_chips(x, y)
        started = []
        for i in range(nb):
            for j, chip in enumerate(chips):
                rows = out[i].at[pl.ds(pl.multiple_of((4 * chip[0] + 2 * chip[1] + c) * hrs[i], 16), hrs[i]), :]
                cp = pltpu.make_async_remote_copy(
                    src_ref=rows, dst_ref=rows, send_sem=send.at[3 * i + j], recv_sem=recv.at[3 * i + j],
                    device_id=(x, y, 1 - c), device_id_type=MESH)
                cp.start()
                started.append(cp)
        for i in range(nb):
            for j, chip in enumerate(chips):
                rows = out[i].at[pl.ds(pl.multiple_of((4 * chip[0] + 2 * chip[1] + 1 - c) * hrs[i], 16), hrs[i]), :]
                pltpu.make_async_remote_copy(
                    src_ref=rows, dst_ref=rows, send_sem=send.at[3 * i + j], recv_sem=recv.at[3 * i + j],
                    device_id=(x, y, 1 - c), device_id_type=MESH).wait_recv()
        for cp in started:
            cp.wait_send()

    return list(pl.pallas_call(
        body, name=name, in_specs=[ANY] * nb, out_specs=[ANY] * nb,
        out_shape=[jax.ShapeDtypeStruct(b.shape, b.dtype) for b in bufs],
        input_output_aliases={i: i for i in range(nb)},
        scratch_shapes=[pltpu.SemaphoreType.DMA((3 * nb,)), pltpu.SemaphoreType.DMA((3 * nb,))],
        compiler_params=pltpu.CompilerParams(has_side_effects=True),
    )(*bufs))


def _pair_exchange(grads, name):
    nr = len(grads)
    n_l = grads[0].shape[0]
    n_sem = nr * n_l * N_CHIP

    def body(*refs):
        src = refs[:nr]
        out = refs[nr:2 * nr]
        send, recv = refs[2 * nr:]
        x, y, c = _mesh_pos()
        copies = []
        for r in range(nr):
            hr = grads[r].shape[1] // N_DEV
            for layer in range(n_l):
                for j in range(N_CHIP):
                    idx = (r * n_l + layer) * N_CHIP + j
                    start = pl.multiple_of((2 * j + 1 - c) * hr, 16)
                    cp = pltpu.make_async_remote_copy(
                        src_ref=src[r].at[layer, pl.ds(start, hr), :], dst_ref=out[r].at[layer, j],
                        send_sem=send.at[idx], recv_sem=recv.at[idx], device_id=(x, y, 1 - c), device_id_type=MESH)
                    cp.start()
                    copies.append(cp)
        for cp in copies:
            cp.wait()

    return pl.pallas_call(
        body, name=name,
        in_specs=[ANY] * nr, out_specs=[ANY] * nr,
        out_shape=[jax.ShapeDtypeStruct((n_l, N_CHIP, g.shape[1] // N_DEV, g.shape[2]), bf16) for g in grads],
        scratch_shapes=[pltpu.SemaphoreType.DMA((n_sem,)), pltpu.SemaphoreType.DMA((n_sem,))],
        compiler_params=pltpu.CompilerParams(has_side_effects=True),
    )(*grads)


def _pair_sum(grad, other, core, chip, name):
    n_l, rows, cols = grad.shape
    hr = rows // N_DEV
    g5 = grad.reshape(n_l, N_CHIP, 2, hr, cols)
    where = jnp.stack([core, chip]).astype(jnp.int32)

    def body(where_ref, g_ref, o_ref, s_ref, mine_ref):
        s_ref[...] = (g_ref[...].astype(f32) + o_ref[...].astype(f32)).astype(bf16)
        mine_ref[...] = s_ref[where_ref[1]]

    return pl.pallas_call(
        body, name=name,
        grid_spec=pltpu.PrefetchScalarGridSpec(
            num_scalar_prefetch=1, grid=(n_l,),
            in_specs=[pl.BlockSpec((None, N_CHIP, None, hr, cols), lambda l, w: (l, 0, w[0], 0, 0)),
                      pl.BlockSpec((None, N_CHIP, hr, cols), lambda l, w: (l, 0, 0, 0))],
            out_specs=[pl.BlockSpec((None, N_CHIP, hr, cols), lambda l, w: (l, 0, 0, 0)),
                       pl.BlockSpec((None, None, hr, cols), lambda l, w: (l, w[1], 0, 0))]),
        out_shape=[jax.ShapeDtypeStruct((n_l, N_CHIP, hr, cols), bf16)] * 2,
        compiler_params=_params(("arbitrary",)),
    )(where, g5, other)


def _chip_plan(nr, n_l):
    def plan(refs, dry=False):
        if dry:
            return [None] * (nr * n_l * 3)
        x, y, c = _mesh_pos()
        out = []
        for r in range(nr):
            for layer in range(n_l):
                for chip in _other_chips(x, y):
                    out.append((refs[r].at[layer, 2 * chip[0] + chip[1]], refs[nr + r].at[layer, 2 * x + y], (*chip, c)))
        return out
    return plan


def _chip_sum(parts, core, name):
    n_l, _, hr, cols = parts.shape

    def body(core_ref, p_ref, o_ref):
        acc = p_ref[0].astype(f32) + p_ref[1].astype(f32)
        acc = acc + p_ref[2].astype(f32)
        o_ref[...] = acc + p_ref[3].astype(f32)

    return pl.pallas_call(
        body, name=name,
        grid_spec=pltpu.PrefetchScalarGridSpec(
            num_scalar_prefetch=1, grid=(n_l,),
            in_specs=[pl.BlockSpec((None, N_CHIP, hr, cols), lambda l, cr: (l, 0, 0, 0))],
            out_specs=pl.BlockSpec((None, None, hr, cols), lambda l, cr: (l, cr[0], 0, 0))),
        out_shape=jax.ShapeDtypeStruct((n_l, 2, hr, cols), f32),
        compiler_params=_params(("arbitrary",)),
    )(core, parts)


def _share_halves(halves):
    nr = len(halves)

    def body(*refs):
        out = refs[nr:2 * nr]
        send, recv = refs[2 * nr:]
        x, y, c = _mesh_pos()
        copies = []
        for r in range(nr):
            cp = pltpu.make_async_remote_copy(
                src_ref=out[r].at[0, c], dst_ref=out[r].at[0, c], send_sem=send.at[r],
                recv_sem=recv.at[r], device_id=(x, y, 1 - c), device_id_type=MESH)
            cp.start()
            copies.append(cp)
        for r in range(nr):
            copies[r].wait_send()
            pltpu.make_async_remote_copy(
                src_ref=out[r].at[0, 1 - c], dst_ref=out[r].at[0, 1 - c], send_sem=send.at[r],
                recv_sem=recv.at[r], device_id=(x, y, 1 - c), device_id_type=MESH).wait_recv()

    return pl.pallas_call(
        body, name="grad_share_halves",
        in_specs=[ANY] * nr, out_specs=[ANY] * nr,
        out_shape=[jax.ShapeDtypeStruct(h.shape, h.dtype) for h in halves],
        input_output_aliases={r: r for r in range(nr)},
        scratch_shapes=[pltpu.SemaphoreType.DMA((nr,))] * 2,
        compiler_params=pltpu.CompilerParams(has_side_effects=True),
    )(*halves)


def _sum_small(parts, name):
    n, rows, cols = parts.shape

    def body(p_ref, o_ref):
        acc = p_ref[0].astype(f32)
        for d in range(1, n):
            acc = acc + p_ref[d].astype(f32)
        o_ref[...] = acc

    return pl.pallas_call(
        body, name=name, grid=(rows // 16,),
        in_specs=[pl.BlockSpec((n, 16, cols), lambda i: (0, i, 0))], out_specs=pl.BlockSpec((16, cols), lambda i: (i, 0)),
        out_shape=jax.ShapeDtypeStruct((rows, cols), f32),
        compiler_params=_params(("parallel",)),
    )(parts)


def _adamw(w, g, m, v, name):
    n_l, rows, cols = w.shape
    budget = 42 * 1024 * 1024
    tr = next(rows // d for d in range(1, rows + 1)
              if rows % d == 0 and (rows // d) % 8 == 0 and (rows // d) * cols * 4 * 14 <= budget)

    def body(w_ref, g_ref, m_ref, v_ref, d_ref, nm_ref, nv_ref):
        gg = g_ref[...]
        nm = ADAM_B1 * m_ref[...] + (1.0 - ADAM_B1) * gg
        nv = ADAM_B2 * v_ref[...] + (1.0 - ADAM_B2) * (gg * gg)
        m_hat = nm / (1.0 - ADAM_B1 ** ADAM_STEP)
        v_hat = nv / (1.0 - ADAM_B2 ** ADAM_STEP)
        d_ref[...] = -ADAM_LR * (m_hat / (jnp.sqrt(v_hat) + ADAM_EPS) + ADAM_WD * w_ref[...])
        nm_ref[...] = nm
        nv_ref[...] = nv

    blk = pl.BlockSpec((None, tr, cols), lambda l, i: (l, i, 0))
    return pl.pallas_call(
        body, name=name, grid=(n_l, rows // tr),
        in_specs=[blk] * 4, out_specs=[blk] * 3, out_shape=[jax.ShapeDtypeStruct((n_l, rows, cols), f32)] * 3,
        compiler_params=_params(("parallel", "parallel")),
    )(w, g, m, v)


SMALL = ("mix_norm", "q_norm", "k_norm", "sinks", "sgu_norm", "w_s", "b_s", "ffn_norm", "conv_b", "conv_w")


def _pack_small(arrs):
    flat = jnp.concatenate([a.reshape(-1) for a in arrs])
    pad = (-flat.shape[0]) % (16 * 1024)
    return jnp.pad(flat, (0, pad)).reshape(-1, 1024)


def _unpack_small(pack, shapes):
    flat = pack.reshape(-1)
    out, off = [], 0
    for s in shapes:
        n = int(np.prod(s))
        out.append(flat[off:off + n].reshape(s))
        off += n
    return out


def kernel(x, mix_norm, w_in, q_norm, k_norm, sinks, sgu_norm, w_s, b_s, w_oa, w_ob, w_out, ffn_norm, w_up, conv_w, conv_b, w_down, loss_target, m_mix_norm, m_w_in, m_q_norm, m_k_norm, m_sinks, m_sgu_norm, m_w_s, m_b_s, m_w_oa, m_w_ob, m_w_out, m_ffn_norm, m_w_up, m_conv_w, m_conv_b, m_w_down, v_mix_norm, v_w_in, v_q_norm, v_k_norm, v_sinks, v_sgu_norm, v_w_s, v_b_s, v_w_oa, v_w_ob, v_w_out, v_ffn_norm, v_w_up, v_conv_w, v_conv_b, v_w_down):
    weights = dict(mix_norm=mix_norm, w_in=w_in, q_norm=q_norm, k_norm=k_norm, sinks=sinks, sgu_norm=sgu_norm,
                   w_s=w_s, b_s=b_s, w_oa=w_oa, w_ob=w_ob, w_out=w_out, ffn_norm=ffn_norm, w_up=w_up,
                   conv_w=conv_w, conv_b=conv_b, w_down=w_down)
    mom_m = dict(mix_norm=m_mix_norm, w_in=m_w_in, q_norm=m_q_norm, k_norm=m_k_norm, sinks=m_sinks,
                 sgu_norm=m_sgu_norm, w_s=m_w_s, b_s=m_b_s, w_oa=m_w_oa, w_ob=m_w_ob, w_out=m_w_out,
                 ffn_norm=m_ffn_norm, w_up=m_w_up, conv_w=m_conv_w, conv_b=m_conv_b, w_down=m_w_down)
    mom_v = dict(mix_norm=v_mix_norm, w_in=v_w_in, q_norm=v_q_norm, k_norm=v_k_norm, sinks=v_sinks,
                 sgu_norm=v_sgu_norm, w_s=v_w_s, b_s=v_b_s, w_oa=v_w_oa, w_ob=v_w_ob, w_out=v_w_out,
                 ffn_norm=v_ffn_norm, w_up=v_w_up, conv_w=v_conv_w, conv_b=v_conv_b, w_down=v_w_down)
    n_seq, seq, _ = x.shape
    T = n_seq * seq
    core = lax.axis_index("c")
    chip = 2 * lax.axis_index("x") + lax.axis_index("y")
    tm = min(512, seq)
    tm_ff = min(256, seq)
    tk_dw = min(2048, T)

    me = 2 * chip + core
    names = [r[0] for r in REGIONS]
    hrs = {name: rows // N_DEV for name, rows, _, _ in REGIONS}
    def placed(l, name, after=None):
        _, rows, cols, transposed = next(r for r in REGIONS if r[0] == name)
        shard = (jnp.swapaxes(weights[name], 1, 2) if transposed else weights[name]).reshape(2, 2, hrs[name], cols)
        return _place(shard, l, core, N_DEV, me, bf16, f"place_{name}_{l}", after).reshape(rows, cols)

    group_keys = [[(0, "w_in")], [(0, n) for n in names[1:]], [(1, n) for n in names]]
    n_direct = [1, 0, 0]
    plans = [_gather_plan([hrs[n] for _, n in keys], nd) for keys, nd in zip(group_keys, n_direct)]
    first_bufs = [placed(0, "w_in"), conv_w, jnp.zeros((N_CHIP,) + conv_w.shape, f32)]
    started, tok = _split_start([(first_bufs, plans[0])], "gather_start_0")
    first_start_done = jnp.broadcast_to(tok[0:1, 0:1], (512, D_MODEL))
    rest_bufs = [[placed(l, n, first_start_done) for l, n in keys] for keys in group_keys[1:]]
    more, tok = _split_start(list(zip(rest_bufs, plans[1:])), "gather_start_1")
    started += more
    second_start_done = jnp.broadcast_to(tok[0:1, 0:1], (512, D_MODEL))
    gathered = [{}, {}]

    def arrived(g, after):
        send, recv, bufs = started[g]
        hr_list = [hrs[n] for _, n in group_keys[g]]
        bufs = _split_wait(bufs, send, recv, _gather_plan(hr_list, n_direct[g]), after, f"gather_wait_{g}")
        return bufs[:len(hr_list)], bufs[len(hr_list):]

    def start_pass(g, bufs):
        (res,), token = _split_start([(bufs, _pass_plan([hrs[n] for _, n in group_keys[g]]))], f"pass_start_{g}")
        return res, token[0:1, 0:1]

    def finish_pass(g, res, after):
        send, recv, bufs = res
        use(g, _split_wait(bufs, send, recv, _pass_plan([hrs[n] for _, n in group_keys[g]]), after, f"pass_wait_{g}"))

    def use(g, bufs):
        for (l, n), b in zip(group_keys[g], bufs):
            gathered[l][n] = b

    xs = x.reshape(T, D_MODEL)
    bufs, (_, conv_w_land) = arrived(0, second_start_done)
    use(0, _pass_to_sibling(bufs, [hrs["w_in"]], "gather_pass_0"))
    conv_w_all = lax.dynamic_update_slice(conv_w_land, conv_w[None], (chip, 0, 0, 0))
    conv_w_full = jnp.concatenate([conv_w_all[j] for j in range(N_CHIP)], axis=-1)
    saved = []
    cur = xs
    for l in range(2):
        wl = gathered[l]
        b_col = b_s[l].reshape(SGU_GROUPS, BLOCK, 1)
        qkv, su, sv, ga, gb, h = _in_proj(cur, mix_norm[l][None], wl["w_in"], l, tm)
        y_att = _attn_fwd(qkv, q_norm[l][None], k_norm[l][None], sinks[l], n_seq, seq)
        sgu_gain = sgu_norm[l][None]
        if l == 0:
            pass_1, tok = start_pass(1, arrived(1, y_att)[0])
            sgu_gain = sgu_gain + tok
        y_sgu = _sgu_fwd(su, sv, sgu_gain, w_s[l], b_col, tm)
        if l == 0:
            finish_pass(1, pass_1, y_sgu)
        x1, merged, a_o, b_o = _merge_fwd(cur, y_att, y_sgu, ga, gb, wl["w_oa"], wl["w_ob"], wl["w_out"], l, tm)
        h2, z, act = _ffn_up(x1, ffn_norm[l][None], wl["w_up"], conv_w_full[l], conv_b[l][None], l, seq, tm_ff)
        saved.append(dict(x=cur, qkv=qkv, su=su, sv=sv, ga=ga, gb=gb, h=h, y_att=y_att, y_sgu=y_sgu, x1=x1,
                          merged=merged, a=a_o, b=b_o, h2=h2, z=z, act=act, b_col=b_col))
        if l == 0:
            pass_2, tok = start_pass(2, arrived(2, act)[0])
            cur = _ffn_down(x1, act, wl["w_down"], tok, l, tm)
            finish_pass(2, pass_2, cur)
        else:
            dy, dyb, loss_part = _ffn_down_loss(x1, act, wl["w_down"], loss_target.reshape(T, D_MODEL), l, tm)
    loss = lax.psum(loss_part[0, 0], ("x", "y", "c"))

    core_arr = core.astype(jnp.int32).reshape(1)
    big = [{}, {}]
    small = {name: [None, None] for name in SMALL}

    def start_pairs(l, keys, tag):
        gl = [big[l][n] for n in keys]
        land = [lax.empty((1, N_CHIP, hrs[n], g.shape[2]), bf16) for n, g in zip(keys, gl)]
        (res,), token = _split_start([(gl + land, _pair_plan([hrs[n] for n in keys]))], f"pair_start_{tag}")
        return (l, keys, res, tag), token[0:1, 0:1]

    def pairs_to_chips(state, after):
        l, keys, (send, recv, bufs), tag = state
        bufs = _split_wait(bufs, send, recv, _pair_plan([hrs[n] for n in keys]), after, f"pair_wait_{tag}")
        return sums_to_chips(l, keys, bufs[:len(keys)], bufs[len(keys):], tag)

    def sums_to_chips(l, keys, gl, from_sibling, tag):
        pairs = [_pair_sum(g, o, core, chip, f"pair_sum_{n}_{l}") for g, o, n in zip(gl, from_sibling, keys)]
        bufs = [p[0] for p in pairs] + [p[1] for p in pairs]
        (res,), token = _split_start([(bufs, _chip_plan(len(keys), 1))], f"chip_start_{tag}")
        return (l, keys, res, tag), token[0:1, 0:1]

    def start_reduce(l, keys, tag):
        gl = [big[l][n] for n in keys]
        return sums_to_chips(l, keys, gl, _pair_exchange(gl, f"pair_exchange_{tag}"), tag)

    def finish_reduce(state, after):
        l, keys, (send, recv, bufs), tag = state
        bufs = _split_wait(bufs, send, recv, _chip_plan(len(keys), 1), after, f"chip_wait_{tag}")
        return {(l, n): _chip_sum(p, core_arr, f"chip_sum_{n}_{l}") for n, p in zip(keys, bufs[len(keys):])}

    rest = [n for n in SMALL if n != "w_s"]
    rest_shapes = [weights[n].shape[1:] if n != "conv_w" else (3, 2 * D_FF) for n in rest]
    zero = jnp.zeros((), jnp.int32)

    def start_small(l):
        packs = [(_pack_small([small[n][l] for n in rest]), f32, "small"), (small["w_s"][l].reshape(-1, 1024), bf16, "w_s")]
        bufs = [_place(p[None, None], 0, zero, N_DEV, me, dt, f"place_{tag}_{l}") for p, dt, tag in packs]
        (res,), token = _split_start([(bufs, _all_to_all_plan(2))], f"small_start_{l}")
        return res, token[0:1, 0:1]

    def finish_small(l, res, after):
        send, recv, bufs = res
        bufs = _split_wait(bufs, send, recv, _all_to_all_plan(2), after, f"small_wait_{l}")
        out = dict(zip(rest, _unpack_small(_sum_small(bufs[0], f"sum_small_{l}"), rest_shapes)))
        out["w_s"] = _sum_small(bufs[1], f"sum_w_s_{l}").reshape(w_s.shape[1:])
        return out

    pending = []
    after_start = jnp.zeros((1, 1), f32)
    for l in (1, 0):
        s = saved[l]
        wl = gathered[l]
        dz, dconv = _ffn_bwd(dyb, s["z"], conv_w_full[l], conv_b[l][None] + after_start, wl["w_down"], l, seq, tm_ff)
        big[l]["w_down"] = _weight_grad(s["act"], dyb, 1408, tk_dw, f"dw_down_{l}")
        big[l]["w_up"] = _weight_grad(dz, s["h2"], 1408, tk_dw, f"dw_up_{l}")
        ffn_gain, sgu_gain, q_gain = ffn_norm[l][None], sgu_norm[l][None], q_norm[l][None]
        if l == 0:
            pairs_a, tok = start_pairs(0, ["w_down", "w_up"], "0a")
            ffn_gain = ffn_gain + tok
        dx1, dx1b, d_ffn = _norm_bwd([dz], wl["w_up"], l, s["x1"], ffn_gain, dy, tm, f"ffn_norm_bwd_{l}")
        if l == 0:
            state, tok = pairs_to_chips(pairs_a, dx1b)
            pending.append(state)
            sgu_gain = sgu_gain + tok
        small["conv_w"][l] = dconv[0:3]
        small["conv_b"][l] = dconv[3]
        small["ffn_norm"][l] = d_ffn[0]
        da, db, dga, dgb, dya, dys = _merge_bwd(dx1b, s["ga"], s["gb"], s["a"], s["b"],
                                                wl["w_oa"], wl["w_ob"], wl["w_out"], l, tm)
        big[l]["w_out"] = _weight_grad(s["merged"], dx1b, 1024, tk_dw, f"dw_out_{l}")
        big[l]["w_oa"] = _weight_grad(da, s["y_att"], 1024, tk_dw, f"dw_oa_{l}")
        big[l]["w_ob"] = _weight_grad(db, s["y_sgu"], 1024, tk_dw, f"dw_ob_{l}")
        if l == 0:
            pairs_m, tok = start_pairs(0, ["w_out", "w_oa", "w_ob"], "0m")
            sgu_gain = sgu_gain + tok
        dsu, dsv, d_ws, d_bs, d_sgu = _sgu_bwd(dys, s["su"], s["sv"], sgu_gain, w_s[l], s["b_col"], tm)
        if l == 0:
            state, tok = pairs_to_chips(pairs_m, dsv)
            pending.append(state)
            q_gain = q_gain + tok
        causal = np.tril(np.ones((BLOCK, BLOCK), bool))
        small["w_s"][l] = jnp.where(causal[None], d_ws, 0.0)
        small["b_s"][l] = d_bs[:, :, 0]
        small["sgu_norm"][l] = d_sgu[0]
        dqkv, d_qg, d_kg, d_sink = _attn_bwd(dya, s["qkv"], q_gain, k_norm[l][None], sinks[l], n_seq, seq)
        small["q_norm"][l] = d_qg[0]
        small["k_norm"][l] = d_kg[0]
        small["sinks"][l] = d_sink[:, 0]
        dproj = [dqkv, dsu, dsv, dga, dgb]
        big[l]["w_in"] = _weight_grad_rows(dproj, s["h"], tm, f"dw_in_{l}")
        if l == 1:
            pairs_1, tok = start_pairs(1, names, "1")
        else:
            state, tok = start_reduce(0, ["w_in"], "0b")
            pending.append(state)
        dy, dyb, d_mix = _norm_bwd(dproj, wl["w_in"], l, s["x"], mix_norm[l][None] + tok, dx1, tm, f"mix_norm_bwd_{l}")
        small["mix_norm"][l] = d_mix[0]
        if l == 1:
            state, tok = pairs_to_chips(pairs_1, dyb)
            pending.append(state)
            small_1, after_start = start_small(1)
            after_start = after_start + tok
    grad_x = dy.reshape(n_seq, seq, D_MODEL)

    small_0, _ = start_small(0)
    halves = {}
    for state in pending:
        halves.update(finish_reduce(state, dyb))
    half_keys = [(l, n) for l in range(2) for n in names]
    shared = dict(zip(half_keys, _share_halves([halves[k] for k in half_keys])))
    grad, delta, new_m, new_v = {}, {}, {}, {}
    flip = lambda a: jnp.swapaxes(a, 1, 2)
    for name, rows, cols, transposed in REGIONS:
        per_layer = [shared[(l, name)].reshape(rows // N_CHIP, cols) for l in range(2)]
        if transposed and weights[name].shape[2] % 128:
            g = jnp.stack(per_layer)
            d, nm, nv = _adamw(flip(weights[name]), g, flip(mom_m[name]), flip(mom_v[name]), f"adamw_{name}")
            grad[name], delta[name], new_m[name], new_v[name] = flip(g), flip(d), flip(nm), flip(nv)
        else:
            grad[name] = jnp.stack([g.T if transposed else g for g in per_layer])
            delta[name], new_m[name], new_v[name] = _adamw(weights[name], grad[name], mom_m[name], mom_v[name],
                                                           f"adamw_{name}")

    per_layer = [finish_small(0, small_0, delta["w_down"]), finish_small(1, small_1, dyb)]
    grad_small = {n: jnp.stack([per_layer[0][n], per_layer[1][n]]) for n in SMALL}
    cw_cols = conv_w.shape[-1]
    grad_small["conv_w"] = lax.dynamic_slice_in_dim(grad_small["conv_w"], chip * cw_cols, cw_cols, axis=2)

    as_rows = lambda a: a.reshape(2, -1, BLOCK)
    d, nm, nv = _adamw(as_rows(w_s), as_rows(grad_small["w_s"]), as_rows(m_w_s), as_rows(v_w_s), "adamw_w_s")
    grad["w_s"], delta["w_s"], new_m["w_s"], new_v["w_s"] = (
        grad_small["w_s"], d.reshape(w_s.shape), nm.reshape(w_s.shape), nv.reshape(w_s.shape))
    shapes = [weights[n].shape for n in rest]
    d, nm, nv = _adamw(_pack_small([weights[n] for n in rest])[None], _pack_small([grad_small[n] for n in rest])[None],
                       _pack_small([mom_m[n] for n in rest])[None], _pack_small([mom_v[n] for n in rest])[None],
                       "adamw_small")
    for n, dd, mm, vv in zip(rest, _unpack_small(d, shapes), _unpack_small(nm, shapes), _unpack_small(nv, shapes)):
        grad[n], delta[n], new_m[n], new_v[n] = grad_small[n], dd, mm, vv

    order = ["mix_norm", "w_in", "q_norm", "k_norm", "sinks", "sgu_norm", "w_s", "b_s", "w_oa", "w_ob", "w_out",
             "ffn_norm", "w_up", "conv_w", "conv_b", "w_down"]
    return (loss, grad_x, *[grad[n] for n in order], *[delta[n] for n in order],
            *[new_m[n] for n in order], *[new_v[n] for n in order])
```

```python
import functools

import numpy as np
import jax
import jax.numpy as jnp
from jax import lax
from jax.experimental import pallas as pl
from jax.experimental.pallas import tpu as pltpu

bf16 = jnp.bfloat16
f32 = jnp.float32

D_MODEL = 1024
ATT_WIDTH = 512
KV_WIDTH = 128
SGU_WIDTH = 512
HEAD_DIM = 64
N_KV_HEADS = 2
Q_GROUP = 4
BLOCK = 128
SGU_GROUPS = 8
IN_WIDTH = 3840
D_FF = 2816
NORM_EPS = 1e-6
NEG_INF = -1e30
N_DEV = 8
N_CHIP = 4

ADAM_LR = 0.001
ADAM_B1 = 0.9
ADAM_B2 = 0.999
ADAM_EPS = 1e-08
ADAM_WD = 0.01
ADAM_STEP = 10

V7X_VMEM_LIMIT = 56 * 1024 * 1024
FF_CHUNK = 2816

REGIONS = (
    ("w_in", IN_WIDTH, D_MODEL, True),
    ("w_oa", D_MODEL, ATT_WIDTH, True),
    ("w_ob", D_MODEL, SGU_WIDTH, True),
    ("w_out", D_MODEL, D_MODEL, False),
    ("w_up", 2 * D_FF, D_MODEL, True),
    ("w_down", D_FF, D_MODEL, False),
)
MESH = pl.DeviceIdType.MESH
ANY = pl.BlockSpec(memory_space=pl.ANY)


def _params(sem=None, **kw):
    return pltpu.CompilerParams(dimension_semantics=sem, vmem_limit_bytes=V7X_VMEM_LIMIT, **kw)


def _wspec(rows, cols, layer=None):
    del layer
    return pl.BlockSpec((rows, cols), lambda *_: (0, 0), pipeline_mode=pl.Buffered(1))


def _full(shape):
    nd = len(shape)
    return pl.BlockSpec(shape, lambda *_: (0,) * nd)


def _dot_nn(a, b):
    return jnp.dot(a, b, preferred_element_type=f32)


def _dot_nt(a, b):
    return lax.dot_general(a, b, (((1,), (1,)), ((), ())), preferred_element_type=f32)


def _dot_tn(a, b):
    return lax.dot_general(a, b, (((0,), (0,)), ((), ())), preferred_element_type=f32)


_GELU_C = float(np.sqrt(2.0 / np.pi))


def _gelu(x):
    return 0.5 * x * (1.0 + jnp.tanh(_GELU_C * (x + 0.044715 * x * x * x)))


def _gelu_grad(x):
    t = jnp.tanh(_GELU_C * (x + 0.044715 * x * x * x))
    du = _GELU_C * (1.0 + 3.0 * 0.044715 * x * x)
    return 0.5 * (1.0 + t) + 0.5 * x * (1.0 - t * t) * du


def _rms(x):
    return lax.rsqrt(jnp.mean(x * x, axis=-1, keepdims=True) + NORM_EPS)


def _mesh_pos():
    return lax.axis_index("x"), lax.axis_index("y"), lax.axis_index("c")


def _in_proj(x, gain, w_in_t, layer, tm):
    T = x.shape[0]

    def body(x_ref, g_ref, w_ref, qkv_ref, su_ref, sv_ref, ga_ref, gb_ref, h_ref):
        xf = x_ref[...]
        h = (xf * _rms(xf) * g_ref[...]).astype(bf16)
        h_ref[...] = h
        qkv_ref[...] = _dot_nt(h, w_ref[0:768, :])
        su_ref[...] = _dot_nt(h, w_ref[768:1280, :]).astype(bf16)
        sv_ref[...] = _dot_nt(h, w_ref[1280:1792, :]).astype(bf16)
        ga_ref[...] = _dot_nt(h, w_ref[1792:2816, :]).astype(bf16)
        gb_ref[...] = _dot_nt(h, w_ref[2816:3840, :]).astype(bf16)

    row = lambda w: pl.BlockSpec((tm, w), lambda i: (i, 0))
    return pl.pallas_call(
        body, name=f"in_proj_{layer}", grid=(T // tm,),
        in_specs=[row(D_MODEL), _full((1, D_MODEL)), _wspec(IN_WIDTH, D_MODEL, layer)],
        out_specs=[row(768), row(512), row(512), row(1024), row(1024), row(D_MODEL)],
        out_shape=[jax.ShapeDtypeStruct((T, 768), f32), jax.ShapeDtypeStruct((T, 512), bf16),
                   jax.ShapeDtypeStruct((T, 512), bf16), jax.ShapeDtypeStruct((T, 1024), bf16),
                   jax.ShapeDtypeStruct((T, 1024), bf16), jax.ShapeDtypeStruct((T, D_MODEL), bf16)],
        compiler_params=_params(("parallel",)),
    )(x, gain, w_in_t)


def _attn_head_group(cur, prev, qg, kg, sink_ref, n, hk):
    lo = hk * HEAD_DIM
    k_raw = jnp.concatenate([prev[:, lo:lo + HEAD_DIM], cur[:, 512 + lo:512 + lo + HEAD_DIM]], axis=0)
    v_band = jnp.concatenate([prev[:, 128 + lo:128 + lo + HEAD_DIM], cur[:, 640 + lo:640 + lo + HEAD_DIM]], axis=0)
    rk = _rms(k_raw)
    k_hat = k_raw * rk
    kn = (k_hat * kg).astype(bf16)
    q_raw = jnp.concatenate(
        [cur[:, (hk * Q_GROUP + g) * HEAD_DIM:(hk * Q_GROUP + g + 1) * HEAD_DIM] for g in range(Q_GROUP)], axis=0)
    rq = _rms(q_raw)
    q_hat = q_raw * rq
    qn = (q_hat * qg * (HEAD_DIM ** -0.5)).astype(bf16)
    s = _dot_nt(qn, kn)
    rows = lax.broadcasted_iota(jnp.int32, (Q_GROUP * BLOCK, 1), 0)
    g_of_row = rows // BLOCK
    qi = rows - g_of_row * BLOCK
    kj = lax.broadcasted_iota(jnp.int32, (1, 2 * BLOCK), 1)
    dist = qi + BLOCK - kj
    valid = (dist >= 0) & (dist < BLOCK) & ((kj >= BLOCK) | (n > 0))
    slope = jnp.zeros((Q_GROUP * BLOCK, 1), f32)
    sink = jnp.zeros((Q_GROUP * BLOCK, 1), f32)
    for g in range(Q_GROUP):
        head = hk * Q_GROUP + g
        slope = jnp.where(g_of_row == g, float(np.exp2(-8.0 * (head + 1.0) / 8.0)), slope)
        sink = jnp.where(g_of_row == g, sink_ref[head], sink)
    s = jnp.where(valid, s - slope * dist.astype(f32), NEG_INF)
    m = jnp.maximum(jnp.max(s, axis=-1, keepdims=True), sink)
    e = jnp.exp(s - m)
    e_sink = jnp.exp(sink - m)
    inv = 1.0 / (jnp.sum(e, axis=-1, keepdims=True) + e_sink)
    return dict(k_raw=k_raw, rk=rk, k_hat=k_hat, kn=kn, v=v_band.astype(bf16), q_hat=q_hat, rq=rq, qn=qn,
                p=e * inv, p_sink=e_sink * inv)


def _attn_fwd(qkv, qg, kg, sinks, n_seq, seq):
    T = n_seq * seq
    nb = seq // BLOCK

    per = 2 if nb % 2 == 0 else 1

    def body(cur_ref, prev_ref, qg_ref, kg_ref, sink_ref, y_ref):
        for sub in range(per):
            n = pl.program_id(1) * per + sub
            cur = cur_ref[sub * BLOCK:(sub + 1) * BLOCK, :]
            prev = prev_ref[...] if sub == 0 else cur_ref[(sub - 1) * BLOCK:sub * BLOCK, 512:768]
            pieces = [None] * (N_KV_HEADS * Q_GROUP)
            for hk in range(N_KV_HEADS):
                a = _attn_head_group(cur, prev, qg_ref[...], kg_ref[...], sink_ref, n, hk)
                o = _dot_nn(a["p"].astype(bf16), a["v"])
                for g in range(Q_GROUP):
                    pieces[hk * Q_GROUP + g] = o[g * BLOCK:(g + 1) * BLOCK]
            y_ref[sub * BLOCK:(sub + 1) * BLOCK, :] = jnp.concatenate(pieces, axis=1).astype(bf16)

    return pl.pallas_call(
        body, name="attn_fwd", grid=(n_seq, nb // per),
        in_specs=[pl.BlockSpec((per * BLOCK, 768), lambda b, n: (b * (nb // per) + n, 0)),
                  pl.BlockSpec((BLOCK, 256), lambda b, n: (b * nb + jnp.maximum(n * per - 1, 0), 2)),
                  _full((1, HEAD_DIM)), _full((1, HEAD_DIM)),
                  pl.BlockSpec(memory_space=pltpu.SMEM)],
        out_specs=pl.BlockSpec((per * BLOCK, ATT_WIDTH), lambda b, n: (b * (nb // per) + n, 0)),
        out_shape=jax.ShapeDtypeStruct((T, ATT_WIDTH), bf16),
        compiler_params=_params(("parallel", "parallel")),
    )(qkv, qkv, qg, kg, sinks)


def _sgu_chunk(su, sv, gain, w_ref, b_ref):
    u = _gelu(su)
    vg = _gelu(sv)
    rv = _rms(vg)
    v_hat = vg * rv
    vn = (v_hat * gain).astype(bf16)
    causal = (lax.broadcasted_iota(jnp.int32, (BLOCK, BLOCK), 0) >= lax.broadcasted_iota(jnp.int32, (BLOCK, BLOCK), 1))
    w_tril = [jnp.where(causal, w_ref[g], 0.0).astype(bf16) for g in range(SGU_GROUPS)]
    gd = SGU_WIDTH // SGU_GROUPS
    mixed = jnp.concatenate(
        [_dot_nn(w_tril[g], vn[:, g * gd:(g + 1) * gd]) + b_ref[g] for g in range(SGU_GROUPS)], axis=1)
    return u, rv, v_hat, vn, w_tril, mixed


def _sgu_fwd(su, sv, gain, w_s, b_s, tm):
    T = su.shape[0]

    def body(su_ref, sv_ref, g_ref, w_ref, b_ref, y_ref):
        for ch in range(tm // BLOCK):
            rows = slice(ch * BLOCK, (ch + 1) * BLOCK)
            u, _, _, _, _, mixed = _sgu_chunk(su_ref[rows, :].astype(f32), sv_ref[rows, :].astype(f32),
                                              g_ref[...], w_ref, b_ref)
            y_ref[rows, :] = (u * mixed).astype(bf16)

    row = pl.BlockSpec((tm, SGU_WIDTH), lambda i: (i, 0))
    return pl.pallas_call(
        body, name="sgu_fwd", grid=(T // tm,),
        in_specs=[row, row, _full((1, SGU_WIDTH)), _full((SGU_GROUPS, BLOCK, BLOCK)), _full((SGU_GROUPS, BLOCK, 1))],
        out_specs=row, out_shape=jax.ShapeDtypeStruct((T, SGU_WIDTH), bf16),
        compiler_params=_params(("parallel",)),
    )(su, sv, gain, w_s, b_s)


def _merge_fwd(x, y_att, y_sgu, ga, gb, w_oa_t, w_ob_t, w_out, layer, tm):
    T = x.shape[0]

    def body(x_ref, ya_ref, ys_ref, ga_ref, gb_ref, woa_ref, wob_ref, wout_ref, x1_ref, m_ref, a_ref, b_ref):
        a = _dot_nt(ya_ref[...], woa_ref[...])
        b = _dot_nt(ys_ref[...], wob_ref[...])
        a_ref[...] = a.astype(bf16)
        b_ref[...] = b.astype(bf16)
        merged = (jax.nn.sigmoid(ga_ref[...].astype(f32)) * a + jax.nn.sigmoid(gb_ref[...].astype(f32)) * b).astype(bf16)
        m_ref[...] = merged
        x1_ref[...] = x_ref[...] + _dot_nn(merged, wout_ref[...])

    row = lambda w: pl.BlockSpec((tm, w), lambda i: (i, 0))
    return pl.pallas_call(
        body, name=f"merge_fwd_{layer}", grid=(T // tm,),
        in_specs=[row(D_MODEL), row(512), row(512), row(1024), row(1024),
                  _wspec(D_MODEL, ATT_WIDTH, layer), _wspec(D_MODEL, SGU_WIDTH, layer), _wspec(D_MODEL, D_MODEL, layer)],
        out_specs=[row(D_MODEL)] * 4,
        out_shape=[jax.ShapeDtypeStruct((T, D_MODEL), f32)] + [jax.ShapeDtypeStruct((T, D_MODEL), bf16)] * 3,
        compiler_params=_params(("parallel",)),
    )(x, y_att, y_sgu, ga, gb, w_oa_t, w_ob_t, w_out)


def _tile_permutation(tm):
    r = np.arange(tm)
    p = np.zeros((tm, tm), np.float32)
    p[r, (r % 8) * (tm // 8) + r // 8] = 1.0
    return jnp.asarray(p, bf16), jnp.asarray(p.T, bf16)


def _stage_taps_before(buf, zz, prev, tm):
    first = lax.broadcasted_iota(jnp.int32, (8, 1), 0) == 0
    buf[16:16 + tm, :] = zz
    buf[0:8, :] = jnp.where(first, prev[7:8], pltpu.roll(buf[tm:tm + 8, :], 1, 0))
    buf[8:16, :] = jnp.where(first, prev[15:16], pltpu.roll(buf[tm + 8:tm + 16, :], 1, 0))


def _stage_taps_after(buf, nxt, tm):
    last = lax.broadcasted_iota(jnp.int32, (8, 1), 0) == 7
    buf[tm:tm + 8, :] = jnp.where(last, nxt[0:1], pltpu.roll(buf[0:8, :], 7, 0))
    buf[tm + 8:tm + 16, :] = jnp.where(last, nxt[8:9], pltpu.roll(buf[8:16, :], 7, 0))


def _conv_rows(buf, r, n, coef):
    z2 = buf[pl.ds(r, n), :]
    z1 = buf[pl.ds(pl.multiple_of(r + 8, 8), n), :]
    z0 = buf[pl.ds(pl.multiple_of(r + 16, 8), n), :]
    return coef[0] + coef[1] * z2 + coef[2] * z1 + coef[3] * z0


def _ffn_up(x1, gain, w_up_t, conv_w, conv_b, layer, seq, tm):
    T = x1.shape[0]
    tps = seq // tm
    perm, perm_t = _tile_permutation(tm)

    rg = 16

    def body(x_ref, g_ref, w_ref, cw_ref, cb_ref, p_ref, pt_ref, h2_ref, z_ref, act_ref, carry_ref,
             zg_buf, zv_buf, actp_buf):
        i = pl.program_id(0)

        @pl.when(i % tps == 0)
        def _():
            carry_ref[...] = jnp.zeros_like(carry_ref)

        xf = x_ref[...]
        h2 = (xf * _rms(xf) * g_ref[...]).astype(bf16)
        h2_ref[...] = h2
        h2p = _dot_nn(p_ref[...], h2).astype(bf16)
        for cc in range(D_FF // FF_CHUNK):
            cols_g = slice(cc * FF_CHUNK, (cc + 1) * FF_CHUNK)
            cols_v = slice(D_FF + cc * FF_CHUNK, D_FF + (cc + 1) * FF_CHUNK)
            for buf, cols in ((zg_buf, cols_g), (zv_buf, cols_v)):
                zb = _dot_nt(h2p, w_ref[cols, :]).astype(bf16)
                z_ref[:, cols] = zb
                _stage_taps_before(buf, zb.astype(f32), carry_ref[:, cols], tm)
                carry_ref[:, cols] = buf[tm:tm + 16, :]
            coef = [jnp.broadcast_to(v, (rg, FF_CHUNK)) for cols in (cols_g, cols_v)
                    for v in (cb_ref[:, cols], cw_ref[0:1, cols], cw_ref[1:2, cols], cw_ref[2:3, cols])]

            def rows_step(j, carry, coef=coef):
                r = pl.multiple_of(j * rg, rg)
                zcg, zcv = (_conv_rows(buf, r, rg, coef[4 * k:4 * k + 4]) for k, buf in enumerate((zg_buf, zv_buf)))
                actp_buf[pl.ds(r, rg), :] = (zcg * jax.nn.sigmoid(zcg) * zcv).astype(bf16)
                return carry

            lax.fori_loop(0, tm // rg, rows_step, 0, unroll=True)
            act_ref[:, cols_g] = _dot_nn(pt_ref[...], actp_buf[...]).astype(bf16)

    row = lambda w: pl.BlockSpec((tm, w), lambda i: (i, 0))
    return pl.pallas_call(
        body, name=f"ffn_up_{layer}", grid=(T // tm,),
        in_specs=[row(D_MODEL), _full((1, D_MODEL)), _wspec(2 * D_FF, D_MODEL, layer),
                  _full((3, 2 * D_FF)), _full((1, 2 * D_FF)), _full((tm, tm)), _full((tm, tm))],
        out_specs=[row(D_MODEL), row(2 * D_FF), row(D_FF)],
        out_shape=[jax.ShapeDtypeStruct((T, D_MODEL), bf16), jax.ShapeDtypeStruct((T, 2 * D_FF), bf16),
                   jax.ShapeDtypeStruct((T, D_FF), bf16)],
        scratch_shapes=[pltpu.VMEM((16, 2 * D_FF), f32), pltpu.VMEM((tm + 16, FF_CHUNK), f32),
                        pltpu.VMEM((tm + 16, FF_CHUNK), f32), pltpu.VMEM((tm, FF_CHUNK), bf16)],
        compiler_params=_params(("arbitrary",)),
    )(x1, gain, w_up_t, conv_w, conv_b, perm, perm_t)


def _ffn_down(x1, act, w_down, after, layer, tm):
    T = x1.shape[0]

    def body(x_ref, a_ref, w_ref, after_ref, o_ref):
        o_ref[...] = x_ref[...] + _dot_nn(a_ref[...], w_ref[...])

    row = lambda w: pl.BlockSpec((tm, w), lambda i: (i, 0))
    return pl.pallas_call(
        body, name=f"ffn_down_{layer}", grid=(T // tm,),
        in_specs=[row(D_MODEL), row(D_FF), _wspec(D_FF, D_MODEL, layer), _full((1, 1))],
        out_specs=row(D_MODEL), out_shape=jax.ShapeDtypeStruct((T, D_MODEL), f32),
        compiler_params=_params(("parallel",)),
    )(x1, act, w_down, after)


def _ffn_down_loss(x1, act, w_down, target, layer, tm):
    T = x1.shape[0]

    def body(x_ref, a_ref, w_ref, t_ref, dy_ref, dyb_ref, loss_ref):
        @pl.when(pl.program_id(0) == 0)
        def _():
            loss_ref[...] = jnp.zeros_like(loss_ref)

        diff = x_ref[...] + _dot_nn(a_ref[...], w_ref[...]) - t_ref[...]
        loss_ref[...] += 0.5 * jnp.sum(jnp.mean(diff * diff, axis=-1, keepdims=True), axis=0, keepdims=True)
        dy = diff * (1.0 / D_MODEL)
        dy_ref[...] = dy
        dyb_ref[...] = dy.astype(bf16)

    row = lambda w: pl.BlockSpec((tm, w), lambda i: (i, 0))
    return pl.pallas_call(
        body, name=f"ffn_down_loss_{layer}", grid=(T // tm,),
        in_specs=[row(D_MODEL), row(D_FF), _wspec(D_FF, D_MODEL, layer), row(D_MODEL)],
        out_specs=[row(D_MODEL), row(D_MODEL), _full((8, 128))],
        out_shape=[jax.ShapeDtypeStruct((T, D_MODEL), f32), jax.ShapeDtypeStruct((T, D_MODEL), bf16),
                   jax.ShapeDtypeStruct((8, 128), f32)],
        compiler_params=_params(("arbitrary",)),
    )(x1, act, w_down, target)


def _ffn_bwd(dx2b, z, conv_w, conv_b, w_down, layer, seq, tm):
    T = z.shape[0]
    nt = T // tm
    tps = seq // tm

    perm, perm_t = _tile_permutation(tm)

    def body(dx_ref, z_ref, zh_ref, cw_ref, cb_ref, wd_ref, p_ref, pt_ref, dz_ref, dconv_ref, carry_ref,
             zg_buf, zv_buf, gg_buf, gv_buf, dact_buf, dzp_buf):
        i = pl.program_id(0)
        pos = (nt - 1 - i) % tps

        @pl.when(i == 0)
        def _():
            dconv_ref[...] = jnp.zeros_like(dconv_ref)

        @pl.when(pos == tps - 1)
        def _():
            carry_ref[...] = jnp.zeros_like(carry_ref)

        dxp = _dot_nn(p_ref[...], dx_ref[...]).astype(bf16)
        halo_on = (pos > 0).astype(f32)
        for cc in range(D_FF // FF_CHUNK):
            cols_g = slice(cc * FF_CHUNK, (cc + 1) * FF_CHUNK)
            cols_v = slice(D_FF + cc * FF_CHUNK, D_FF + (cc + 1) * FF_CHUNK)
            for buf, cols in ((zg_buf, cols_g), (zv_buf, cols_v)):
                _stage_taps_before(buf, z_ref[:, cols].astype(f32), zh_ref[:, cols].astype(f32) * halo_on, tm)
            dact_buf[...] = _dot_nt(dxp, wd_ref[cols_g, :])
            coef = [jnp.broadcast_to(v, (8, FF_CHUNK)) for cols in (cols_g, cols_v)
                    for v in (cb_ref[:, cols], cw_ref[0:1, cols], cw_ref[1:2, cols], cw_ref[2:3, cols])]

            def first_pass(j, sums, coef=coef):
                r = pl.multiple_of(j * 8, 8)
                rows = pl.ds(r, 8)
                zcg = _conv_rows(zg_buf, r, 8, coef[0:4])
                zcv = _conv_rows(zv_buf, r, 8, coef[4:8])
                sg = jax.nn.sigmoid(zcg)
                silu = zcg * sg
                d_act = dact_buf[rows, :]
                dg = d_act * zcv * sg * (1.0 + zcg * (1.0 - sg))
                dv = d_act * silu
                gg_buf[rows, :] = dg
                gv_buf[rows, :] = dv
                out = []
                for k, (g, buf) in enumerate(((dg, zg_buf), (dv, zv_buf))):
                    out += [sums[4 * k] + g * buf[rows, :],
                            sums[4 * k + 1] + g * buf[pl.ds(pl.multiple_of(r + 8, 8), 8), :],
                            sums[4 * k + 2] + g * buf[pl.ds(pl.multiple_of(r + 16, 8), 8), :],
                            sums[4 * k + 3] + g]
                return tuple(out)

            sums = lax.fori_loop(0, tm // 8, first_pass, tuple(jnp.zeros((8, FF_CHUNK), f32) for _ in range(8)),
                                 unroll=True)
            for k, cols in enumerate((cols_g, cols_v)):
                for tap in range(4):
                    dconv_ref[tap:tap + 1, cols] += jnp.sum(sums[4 * k + tap], axis=0, keepdims=True)
            for buf, cols in ((gg_buf, cols_g), (gv_buf, cols_v)):
                _stage_taps_after(buf, carry_ref[:, cols], tm)
                carry_ref[:, cols] = buf[0:16, :]
                w0, w1, w2 = (jnp.broadcast_to(cw_ref[k:k + 1, cols], (16, FF_CHUNK)) for k in range(3))

                def second_pass(j, carry, buf=buf, w0=w0, w1=w1, w2=w2):
                    r = pl.multiple_of(j * 16, 16)
                    dzp_buf[pl.ds(r, 16), :] = (w2 * buf[pl.ds(r, 16), :] + w1 * buf[pl.ds(pl.multiple_of(r + 8, 8), 16), :]
                                                + w0 * buf[pl.ds(pl.multiple_of(r + 16, 16), 16), :]).astype(bf16)
                    return carry

                lax.fori_loop(0, tm // 16, second_pass, 0, unroll=True)
                dz_ref[:, cols] = _dot_nn(pt_ref[...], dzp_buf[...]).astype(bf16)

    rev = lambda w: pl.BlockSpec((tm, w), lambda i: (nt - 1 - i, 0))
    return pl.pallas_call(
        body, name=f"ffn_bwd_{layer}", grid=(nt,),
        in_specs=[rev(D_MODEL), rev(2 * D_FF),
                  pl.BlockSpec((16, 2 * D_FF), lambda i: (jnp.maximum((nt - 1 - i) * (tm // 16) - 1, 0), 0)),
                  _full((3, 2 * D_FF)), _full((1, 2 * D_FF)), _wspec(D_FF, D_MODEL, layer),
                  _full((tm, tm)), _full((tm, tm))],
        out_specs=[rev(2 * D_FF), _full((8, 2 * D_FF))],
        out_shape=[jax.ShapeDtypeStruct((T, 2 * D_FF), bf16), jax.ShapeDtypeStruct((8, 2 * D_FF), f32)],
        scratch_shapes=[pltpu.VMEM((16, 2 * D_FF), f32)] + [pltpu.VMEM((tm + 16, FF_CHUNK), f32)] * 4
        + [pltpu.VMEM((tm, FF_CHUNK), f32), pltpu.VMEM((tm, FF_CHUNK), bf16)],
        compiler_params=_params(("arbitrary",)),
    )(dx2b, z, z, conv_w, conv_b, w_down, perm, perm_t)


def _norm_bwd(dys, w, layer, x, gain, dres, tm, name):
    T = dys[0].shape[0]
    widths = [d.shape[1] for d in dys]
    K = sum(widths)
    n = len(dys)

    def body(*refs):
        dy_refs = refs[:n]
        w_ref, x_ref, g_ref, dres_ref, dx_ref, dxb_ref, dg_ref = refs[n:]

        @pl.when(pl.program_id(0) == 0)
        def _():
            dg_ref[...] = jnp.zeros_like(dg_ref)

        dh, lo = None, 0
        for dy_ref, wd in zip(dy_refs, widths):
            part = _dot_nn(dy_ref[...], w_ref[lo:lo + wd, :])
            dh = part if dh is None else dh + part
            lo += wd
        xf = x_ref[...]
        r = _rms(xf)
        x_hat = xf * r
        dg_ref[...] += jnp.sum(dh * x_hat, axis=0, keepdims=True)
        dxh = dh * g_ref[...]
        dx = dres_ref[...] + r * (dxh - x_hat * jnp.mean(dxh * x_hat, axis=-1, keepdims=True))
        dx_ref[...] = dx
        dxb_ref[...] = dx.astype(bf16)

    row = lambda w_: pl.BlockSpec((tm, w_), lambda i: (i, 0))
    return pl.pallas_call(
        body, name=name, grid=(T // tm,),
        in_specs=[row(wd) for wd in widths] + [_wspec(K, D_MODEL, layer), row(D_MODEL), _full((1, D_MODEL)), row(D_MODEL)],
        out_specs=[row(D_MODEL), row(D_MODEL), _full((1, D_MODEL))],
        out_shape=[jax.ShapeDtypeStruct((T, D_MODEL), f32), jax.ShapeDtypeStruct((T, D_MODEL), bf16),
                   jax.ShapeDtypeStruct((1, D_MODEL), f32)],
        compiler_params=_params(("arbitrary",)),
    )(*dys, w, x, gain, dres)


def _merge_bwd(dx1b, ga, gb, a, b, w_oa_t, w_ob_t, w_out, layer, tm):
    T = dx1b.shape[0]

    def body(dx_ref, ga_ref, gb_ref, a_ref, b_ref, woa_ref, wob_ref, wout_ref,
             da_ref, db_ref, dga_ref, dgb_ref, dya_ref, dys_ref):
        dm = _dot_nt(dx_ref[...], wout_ref[...])
        sa = jax.nn.sigmoid(ga_ref[...].astype(f32))
        sb = jax.nn.sigmoid(gb_ref[...].astype(f32))
        da = (dm * sa).astype(bf16)
        db = (dm * sb).astype(bf16)
        da_ref[...] = da
        db_ref[...] = db
        dga_ref[...] = (dm * a_ref[...].astype(f32) * sa * (1.0 - sa)).astype(bf16)
        dgb_ref[...] = (dm * b_ref[...].astype(f32) * sb * (1.0 - sb)).astype(bf16)
        dya_ref[...] = _dot_nn(da, woa_ref[...]).astype(bf16)
        dys_ref[...] = _dot_nn(db, wob_ref[...]).astype(bf16)

    row = lambda w: pl.BlockSpec((tm, w), lambda i: (i, 0))
    return pl.pallas_call(
        body, name=f"merge_bwd_{layer}", grid=(T // tm,),
        in_specs=[row(D_MODEL)] * 5 + [_wspec(D_MODEL, ATT_WIDTH, layer), _wspec(D_MODEL, SGU_WIDTH, layer),
                                       _wspec(D_MODEL, D_MODEL, layer)],
        out_specs=[row(D_MODEL)] * 4 + [row(512)] * 2,
        out_shape=[jax.ShapeDtypeStruct((T, D_MODEL), bf16)] * 4 + [jax.ShapeDtypeStruct((T, 512), bf16)] * 2,
        compiler_params=_params(("parallel",)),
    )(dx1b, ga, gb, a, b, w_oa_t, w_ob_t, w_out)


def _sgu_bwd(dy, su, sv, gain, w_s, b_s, tm):
    T = su.shape[0]
    gd = SGU_WIDTH // SGU_GROUPS

    def body(dy_ref, su_ref, sv_ref, g_ref, w_ref, b_ref, dsu_ref, dsv_ref, dw_ref, db_ref, dg_ref):
        @pl.when(pl.program_id(0) == 0)
        def _():
            dw_ref[...] = jnp.zeros_like(dw_ref)
            db_ref[...] = jnp.zeros_like(db_ref)
            dg_ref[...] = jnp.zeros_like(dg_ref)

        gain_v = g_ref[...]
        for ch in range(tm // BLOCK):
            rows = slice(ch * BLOCK, (ch + 1) * BLOCK)
            su_c = su_ref[rows, :].astype(f32)
            sv_c = sv_ref[rows, :].astype(f32)
            u, rv, v_hat, vn, w_tril, mixed = _sgu_chunk(su_c, sv_c, gain_v, w_ref, b_ref)
            dyc = dy_ref[rows, :].astype(f32)
            dsu_ref[rows, :] = (dyc * mixed * _gelu_grad(su_c)).astype(bf16)
            dmix = dyc * u
            dmix_b = dmix.astype(bf16)
            dvn = []
            for g in range(SGU_GROUPS):
                gs = slice(g * gd, (g + 1) * gd)
                db_ref[g] += jnp.sum(dmix[:, gs], axis=1, keepdims=True)
                dw_ref[g] += _dot_nt(dmix_b[:, gs], vn[:, gs])
                dvn.append(_dot_tn(w_tril[g], dmix_b[:, gs]))
            dvn = jnp.concatenate(dvn, axis=1)
            dg_ref[...] += jnp.sum(dvn * v_hat, axis=0, keepdims=True)
            dxh = dvn * gain_v
            dvg = rv * (dxh - v_hat * jnp.mean(dxh * v_hat, axis=-1, keepdims=True))
            dsv_ref[rows, :] = (dvg * _gelu_grad(sv_c)).astype(bf16)

    row = pl.BlockSpec((tm, SGU_WIDTH), lambda i: (i, 0))
    return pl.pallas_call(
        body, name="sgu_bwd", grid=(T // tm,),
        in_specs=[row, row, row, _full((1, SGU_WIDTH)), _full((SGU_GROUPS, BLOCK, BLOCK)),
                  _full((SGU_GROUPS, BLOCK, 1))],
        out_specs=[row, row, _full((SGU_GROUPS, BLOCK, BLOCK)), _full((SGU_GROUPS, BLOCK, 1)), _full((1, SGU_WIDTH))],
        out_shape=[jax.ShapeDtypeStruct((T, SGU_WIDTH), bf16)] * 2 + [
            jax.ShapeDtypeStruct((SGU_GROUPS, BLOCK, BLOCK), f32), jax.ShapeDtypeStruct((SGU_GROUPS, BLOCK, 1), f32),
            jax.ShapeDtypeStruct((1, SGU_WIDTH), f32)],
        compiler_params=_params(("arbitrary",)),
    )(dy, su, sv, gain, w_s, b_s)


def _attn_bwd(dy, qkv, qg, kg, sinks, n_seq, seq):
    T = n_seq * seq
    nb = seq // BLOCK
    scale = HEAD_DIM ** -0.5
    per = 1
    ng = nb // per

    def body(dy_ref, cur_ref, prev_ref, qg_ref, kg_ref, sink_ref, dqkv_ref, dqg_ref, dkg_ref, dsink_ref,
             carry_k, carry_v):
        b = pl.program_id(0)
        j = pl.program_id(1)

        @pl.when((b == 0) & (j == 0))
        def _():
            dqg_ref[...] = jnp.zeros_like(dqg_ref)
            dkg_ref[...] = jnp.zeros_like(dkg_ref)
            dsink_ref[...] = jnp.zeros_like(dsink_ref)

        @pl.when(j == 0)
        def _():
            carry_k[...] = jnp.zeros_like(carry_k)
            carry_v[...] = jnp.zeros_like(carry_v)

        for sub in reversed(range(per)):
            rows = slice(sub * BLOCK, (sub + 1) * BLOCK)
            prev = prev_ref[...] if sub == 0 else cur_ref[(sub - 1) * BLOCK:sub * BLOCK, 512:768]
            one_block(dy_ref[rows, :].astype(f32), cur_ref[rows, :], prev, (ng - 1 - j) * per + sub,
                      qg_ref[...], kg_ref[...], sink_ref, dqkv_ref.at[rows, :], dqg_ref, dkg_ref, dsink_ref,
                      carry_k, carry_v)

    def one_block(dyf, cur, prev, n, qg_v, kg_v, sink_ref, dqkv_ref, dqg_ref, dkg_ref, dsink_ref, carry_k, carry_v):
        dq_pieces = [None] * (N_KV_HEADS * Q_GROUP)
        dk_pieces, dv_pieces = [], []
        for hk in range(N_KV_HEADS):
            a = _attn_head_group(cur, prev, qg_v, kg_v, sink_ref, n, hk)
            do = jnp.concatenate(
                [dyf[:, (hk * Q_GROUP + g) * HEAD_DIM:(hk * Q_GROUP + g + 1) * HEAD_DIM] for g in range(Q_GROUP)],
                axis=0).astype(bf16)
            p = a["p"]
            dp = _dot_nt(do, a["v"])
            dv_band = _dot_tn(p.astype(bf16), do)
            dsum = jnp.sum(p * dp, axis=-1, keepdims=True)
            ds = (p * (dp - dsum)).astype(bf16)
            dsink_col = -a["p_sink"] * dsum
            for g in range(Q_GROUP):
                head = hk * Q_GROUP + g
                dsink_ref[head:head + 1, :] += jnp.sum(dsink_col[g * BLOCK:(g + 1) * BLOCK], axis=0, keepdims=True)
            dqn = _dot_nn(ds, a["kn"])
            dkn_band = _dot_tn(ds, a["qn"])
            dq_hat_g = dqn * scale
            dqg_ref[...] += jnp.sum(dq_hat_g * a["q_hat"], axis=0, keepdims=True)
            dxh = dq_hat_g * qg_v
            dq = a["rq"] * (dxh - a["q_hat"] * jnp.mean(dxh * a["q_hat"], axis=-1, keepdims=True))
            for g in range(Q_GROUP):
                dq_pieces[hk * Q_GROUP + g] = dq[g * BLOCK:(g + 1) * BLOCK]
            dkn = dkn_band[BLOCK:] + carry_k[hk]
            dv_pieces.append(dv_band[BLOCK:] + carry_v[hk])
            carry_k[hk] = dkn_band[:BLOCK]
            carry_v[hk] = dv_band[:BLOCK]
            k_hat = a["k_hat"][BLOCK:]
            dkg_ref[...] += jnp.sum(dkn * k_hat, axis=0, keepdims=True)
            dxk = dkn * kg_v
            dk_pieces.append(a["rk"][BLOCK:] * (dxk - k_hat * jnp.mean(dxk * k_hat, axis=-1, keepdims=True)))
        dqkv_ref[...] = jnp.concatenate(dq_pieces + dk_pieces + dv_pieces, axis=1).astype(bf16)

    blk = lambda w: pl.BlockSpec((per * BLOCK, w), lambda b, j: (b * ng + ng - 1 - j, 0))
    return pl.pallas_call(
        body, name="attn_bwd", grid=(n_seq, ng),
        in_specs=[blk(ATT_WIDTH), blk(768),
                  pl.BlockSpec((BLOCK, 256), lambda b, j: (b * nb + jnp.maximum((ng - 1 - j) * per - 1, 0), 2)),
                  _full((1, HEAD_DIM)), _full((1, HEAD_DIM)), pl.BlockSpec(memory_space=pltpu.SMEM)],
        out_specs=[blk(768), _full((1, HEAD_DIM)), _full((1, HEAD_DIM)), _full((8, 128))],
        out_shape=[jax.ShapeDtypeStruct((T, 768), bf16), jax.ShapeDtypeStruct((1, HEAD_DIM), f32),
                   jax.ShapeDtypeStruct((1, HEAD_DIM), f32), jax.ShapeDtypeStruct((8, 128), f32)],
        scratch_shapes=[pltpu.VMEM((N_KV_HEADS, BLOCK, HEAD_DIM), f32), pltpu.VMEM((N_KV_HEADS, BLOCK, HEAD_DIM), f32)],
        compiler_params=_params(("arbitrary", "arbitrary")),
    )(dy, qkv, qkv, qg, kg, sinks)


def _weight_grad(a, b, tm, tk, name):
    T, M = a.shape
    N = b.shape[1]
    nk = T // tk

    def body(a_ref, b_ref, o_ref, acc_ref):
        k = pl.program_id(1)

        @pl.when(k == 0)
        def _():
            acc_ref[...] = jnp.zeros_like(acc_ref)

        acc_ref[...] += _dot_tn(a_ref[...], b_ref[...])

        @pl.when(k == nk - 1)
        def _():
            o_ref[...] = acc_ref[...].astype(bf16)

    return pl.pallas_call(
        body, name=name, grid=(M // tm, nk),
        in_specs=[pl.BlockSpec((tk, tm), lambda i, k: (k, i)), pl.BlockSpec((tk, N), lambda i, k: (k, 0))],
        out_specs=pl.BlockSpec((None, tm, N), lambda i, k: (0, i, 0)),
        out_shape=jax.ShapeDtypeStruct((1, M, N), bf16),
        scratch_shapes=[pltpu.VMEM((tm, N), f32)],
        compiler_params=_params(("parallel", "arbitrary")),
    )(a, b)


def _weight_grad_rows(a_list, b, tk, name):
    T, N = b.shape
    widths = [a.shape[1] for a in a_list]
    M = sum(widths)
    nk = T // tk
    n = len(a_list)

    def body(*refs):
        a_refs = refs[:n]
        b_ref, o_ref, acc_ref = refs[n:]
        k = pl.program_id(0)

        @pl.when(k == 0)
        def _():
            acc_ref[...] = jnp.zeros_like(acc_ref)

        lo = 0
        for a_ref, wd in zip(a_refs, widths):
            acc_ref[lo:lo + wd, :] += _dot_tn(a_ref[...], b_ref[...])
            lo += wd

        @pl.when(k == nk - 1)
        def _():
            o_ref[...] = acc_ref[...].astype(bf16)

    return pl.pallas_call(
        body, name=name, grid=(nk,),
        in_specs=[pl.BlockSpec((tk, wd), lambda k: (k, 0)) for wd in widths] + [pl.BlockSpec((tk, N), lambda k: (k, 0))],
        out_specs=pl.BlockSpec((None, M, N), lambda k: (0, 0, 0)),
        out_shape=jax.ShapeDtypeStruct((1, M, N), bf16),
        scratch_shapes=[pltpu.VMEM((M, N), f32)],
        compiler_params=_params(("arbitrary",)),
    )(*a_list, b)


def _place(src, layer, src_slot, n_slots, dst_slot, dtype, name, after=None):
    _, _, rows, cols = src.shape
    slots = jnp.stack([src_slot, dst_slot]).astype(jnp.int32)

    def body(slots_ref, s_ref, *rest):
        rest[-1][...] = s_ref[...].astype(dtype)

    return pl.pallas_call(
        body, name=name,
        grid_spec=pltpu.PrefetchScalarGridSpec(
            num_scalar_prefetch=1, grid=(1,),
            in_specs=[pl.BlockSpec((None, None, rows, cols), lambda i, sl: (layer, sl[0], 0, 0))]
            + ([] if after is None else [ANY]),
            out_specs=pl.BlockSpec((None, rows, cols), lambda i, sl: (sl[1], 0, 0))),
        out_shape=jax.ShapeDtypeStruct((n_slots, rows, cols), dtype),
        compiler_params=_params(("arbitrary",)),
    )(slots, src, *([] if after is None else [after]))


HBM = pl.BlockSpec(memory_space=pltpu.HBM)
SEM = pl.BlockSpec(memory_space=pltpu.SEMAPHORE)
DATAFLOW = pltpu.SideEffectType.DATAFLOW_SIDE_EFFECTING


def _other_chips(x, y):
    return [(1 - x, y), (x, 1 - y), (1 - x, 1 - y)]


def _split_start(groups, name):
    nb = [len(bufs) for bufs, _ in groups]
    flat = [b for bufs, _ in groups for b in bufs]
    ns = [len(plan(bufs, dry=True)) for bufs, plan in groups]
    ng = len(groups)

    def body(*refs):
        n_in = len(flat)
        sems = refs[n_in:n_in + 2 * ng]
        thru = refs[n_in + 2 * ng:2 * n_in + 2 * ng]
        token = refs[2 * n_in + 2 * ng]
        off = 0
        for g, (bufs, plan) in enumerate(groups):
            mine = thru[off:off + nb[g]]
            off += nb[g]
            for k, (src, dst, to) in enumerate(plan(mine)):
                pltpu.make_async_remote_copy(
                    src_ref=src, dst_ref=dst, send_sem=sems[2 * g].at[k], recv_sem=sems[2 * g + 1].at[k],
                    device_id=to, device_id_type=MESH).start()
        token[...] = jnp.zeros_like(token)

    out_shape = []
    for n in ns:
        out_shape += [pltpu.SemaphoreType.DMA((n,)), pltpu.SemaphoreType.DMA((n,))]
    out_shape += [pltpu.HBM(b.shape, b.dtype) for b in flat]
    out_shape.append(jax.ShapeDtypeStruct((8, 128), f32))
    res = pl.pallas_call(
        body, name=name, out_shape=tuple(out_shape),
        in_specs=[HBM] * len(flat),
        out_specs=tuple([SEM] * (2 * ng) + [HBM] * len(flat) + [pl.BlockSpec(memory_space=pltpu.VMEM)]),
        input_output_aliases={i: 2 * ng + i for i in range(len(flat))},
        compiler_params=pltpu.CompilerParams(has_side_effects=DATAFLOW),
    )(*[pltpu.with_memory_space_constraint(b, pltpu.HBM) for b in flat])
    out, off = [], 2 * ng
    for g in range(ng):
        out.append((res[2 * g], res[2 * g + 1], list(res[off:off + nb[g]])))
        off += nb[g]
    return out, res[-1]


def _split_wait(bufs, send, recv, plan, after, name):
    nb = len(bufs)

    def body(*refs):
        thru = refs[:nb]
        send_ref, recv_ref = refs[nb], refs[nb + 1]
        for k, (src, dst, to) in enumerate(plan(thru)):
            cp = pltpu.make_async_remote_copy(
                src_ref=src, dst_ref=dst, send_sem=send_ref.at[k], recv_sem=recv_ref.at[k],
                device_id=to, device_id_type=MESH)
            cp.wait_send()
            cp.wait_recv()

    res = pl.pallas_call(
        body, name=name, out_shape=tuple(pltpu.HBM(b.shape, b.dtype) for b in bufs),
        in_specs=[HBM] * nb + [SEM, SEM, ANY], out_specs=tuple([HBM] * nb),
        input_output_aliases={i: i for i in range(nb)},
        compiler_params=pltpu.CompilerParams(has_side_effects=DATAFLOW),
    )(*bufs, send, recv, after)
    return list(res)


def _gather_plan(hrs, n_direct=0):
    def plan(refs, dry=False):
        if dry:
            return [None] * (4 * len(hrs) + 3 * n_direct)
        x, y, c = _mesh_pos()
        me = 4 * x + 2 * y + c
        out = []
        for i in range(n_direct):
            src, land = refs[len(hrs) + 2 * i], refs[len(hrs) + 2 * i + 1]
            out += [(src, land.at[2 * x + y], (*chip, c)) for chip in _other_chips(x, y)]
        for ref, hr in zip(refs, hrs):
            rows = ref.at[pl.ds(pl.multiple_of(me * hr, 16), hr), :]
            out.append((rows, rows, (x, y, 1 - c)))
            out += [(rows, rows, (*chip, c)) for chip in _other_chips(x, y)]
        return out
    return plan


def _pass_plan(hrs):
    def plan(refs, dry=False):
        if dry:
            return [None] * (3 * len(hrs))
        x, y, c = _mesh_pos()
        out = []
        for ref, hr in zip(refs, hrs):
            for chip in _other_chips(x, y):
                rows = ref.at[pl.ds(pl.multiple_of((4 * chip[0] + 2 * chip[1] + c) * hr, 16), hr), :]
                out.append((rows, rows, (x, y, 1 - c)))
        return out
    return plan


def _pair_plan(hrs):
    n = len(hrs)

    def plan(refs, dry=False):
        if dry:
            return [None] * (N_CHIP * n)
        x, y, c = _mesh_pos()
        out = []
        for r in range(n):
            for j in range(N_CHIP):
                start = pl.multiple_of((2 * j + 1 - c) * hrs[r], 16)
                out.append((refs[r].at[0, pl.ds(start, hrs[r]), :], refs[n + r].at[0, j], (x, y, 1 - c)))
        return out
    return plan


def _all_to_all_plan(n):
    def plan(refs, dry=False):
        if dry:
            return [None] * (7 * n)
        x, y, c = _mesh_pos()
        out = []
        for ref in refs:
            mine = ref.at[4 * x + 2 * y + c]
            for fx in range(2):
                for fy in range(2):
                    for fc in range(2):
                        if fx or fy or fc:
                            out.append((mine, mine, (1 - x if fx else x, 1 - y if fy else y, 1 - c if fc else c)))
        return out
    return plan


def _pass_to_sibling(bufs, hrs, name):
    nb = len(bufs)

    def body(*refs):
        out = refs[nb:2 * nb]
        send, recv = refs[2 * nb:]
        x, y, c = _mesh_pos()
        chips = _other_chips(x, y)
        started = []
        for i in range(nb):
            for j, chip in enumerate(chips):
                rows = out[i].at[pl.ds(pl.multiple_of((4 * chip[0] + 2 * chip[1] + c) * hrs[i], 16), hrs[i]), :]
                cp = pltpu.make_async_remote_copy(
                    src_ref=rows, dst_ref=rows, send_sem=send.at[3 * i + j], recv_sem=recv.at[3 * i + j],
                    device_id=(x, y, 1 - c), device_id_type=MESH)
                cp.start()
                started.append(cp)
        for i in range(nb):
            for j, chip in enumerate(chips):
                rows = out[i].at[pl.ds(pl.multiple_of((4 * chip[0] + 2 * chip[1] + 1 - c) * hrs[i], 16), hrs[i]), :]
                pltpu.make_async_remote_copy(
                    src_ref=rows, dst_ref=rows, send_sem=send.at[3 * i + j], recv_sem=recv.at[3 * i + j],
                    device_id=(x, y, 1 - c), device_id_type=MESH).wait_recv()
        for cp in started:
            cp.wait_send()

    return list(pl.pallas_call(
        body, name=name, in_specs=[ANY] * nb, out_specs=[ANY] * nb,
        out_shape=[jax.ShapeDtypeStruct(b.shape, b.dtype) for b in bufs],
        input_output_aliases={i: i for i in range(nb)},
        scratch_shapes=[pltpu.SemaphoreType.DMA((3 * nb,)), pltpu.SemaphoreType.DMA((3 * nb,))],
        compiler_params=pltpu.CompilerParams(has_side_effects=True),
    )(*bufs))


def _pair_exchange(grads, name):
    nr = len(grads)
    n_l = grads[0].shape[0]
    n_sem = nr * n_l * N_CHIP

    def body(*refs):
        src = refs[:nr]
        out = refs[nr:2 * nr]
        send, recv = refs[2 * nr:]
        x, y, c = _mesh_pos()
        copies = []
        for r in range(nr):
            hr = grads[r].shape[1] // N_DEV
            for layer in range(n_l):
                for j in range(N_CHIP):
                    idx = (r * n_l + layer) * N_CHIP + j
                    start = pl.multiple_of((2 * j + 1 - c) * hr, 16)
                    cp = pltpu.make_async_remote_copy(
                        src_ref=src[r].at[layer, pl.ds(start, hr), :], dst_ref=out[r].at[layer, j],
                        send_sem=send.at[idx], recv_sem=recv.at[idx], device_id=(x, y, 1 - c), device_id_type=MESH)
                    cp.start()
                    copies.append(cp)
        for cp in copies:
            cp.wait()

    return pl.pallas_call(
        body, name=name,
        in_specs=[ANY] * nr, out_specs=[ANY] * nr,
        out_shape=[jax.ShapeDtypeStruct((n_l, N_CHIP, g.shape[1] // N_DEV, g.shape[2]), bf16) for g in grads],
        scratch_shapes=[pltpu.SemaphoreType.DMA((n_sem,)), pltpu.SemaphoreType.DMA((n_sem,))],
        compiler_params=pltpu.CompilerParams(has_side_effects=True),
    )(*grads)


def _pair_sum(grad, other, core, chip, name):
    n_l, rows, cols = grad.shape
    hr = rows // N_DEV
    g5 = grad.reshape(n_l, N_CHIP, 2, hr, cols)
    where = jnp.stack([core, chip]).astype(jnp.int32)

    def body(where_ref, g_ref, o_ref, s_ref, mine_ref):
        s_ref[...] = (g_ref[...].astype(f32) + o_ref[...].astype(f32)).astype(bf16)
        mine_ref[...] = s_ref[where_ref[1]]

    return pl.pallas_call(
        body, name=name,
        grid_spec=pltpu.PrefetchScalarGridSpec(
            num_scalar_prefetch=1, grid=(n_l,),
            in_specs=[pl.BlockSpec((None, N_CHIP, None, hr, cols), lambda l, w: (l, 0, w[0], 0, 0)),
                      pl.BlockSpec((None, N_CHIP, hr, cols), lambda l, w: (l, 0, 0, 0))],
            out_specs=[pl.BlockSpec((None, N_CHIP, hr, cols), lambda l, w: (l, 0, 0, 0)),
                       pl.BlockSpec((None, None, hr, cols), lambda l, w: (l, w[1], 0, 0))]),
        out_shape=[jax.ShapeDtypeStruct((n_l, N_CHIP, hr, cols), bf16)] * 2,
        compiler_params=_params(("arbitrary",)),
    )(where, g5, other)


def _chip_plan(nr, n_l):
    def plan(refs, dry=False):
        if dry:
            return [None] * (nr * n_l * 3)
        x, y, c = _mesh_pos()
        out = []
        for r in range(nr):
            for layer in range(n_l):
                for chip in _other_chips(x, y):
                    out.append((refs[r].at[layer, 2 * chip[0] + chip[1]], refs[nr + r].at[layer, 2 * x + y], (*chip, c)))
        return out
    return plan


def _chip_sum(parts, layer, core, into, name):
    _, _, hr, cols = parts.shape

    def body(core_ref, p_ref, *rest):
        acc = p_ref[0].astype(f32) + p_ref[1].astype(f32)
        acc = acc + p_ref[2].astype(f32)
        rest[-1][...] = acc + p_ref[3].astype(f32)

    return pl.pallas_call(
        body, name=name,
        grid_spec=pltpu.PrefetchScalarGridSpec(
            num_scalar_prefetch=1, grid=(1,),
            in_specs=[pl.BlockSpec((None, N_CHIP, hr, cols), lambda i, cr: (0, 0, 0, 0))]
            + ([] if into is None else [ANY]),
            out_specs=pl.BlockSpec((None, None, hr, cols), lambda i, cr: (layer, cr[0], 0, 0))),
        out_shape=jax.ShapeDtypeStruct((2, 2, hr, cols), f32),
        input_output_aliases={} if into is None else {2: 0},
        compiler_params=_params(("arbitrary",)),
    )(core, parts, *([] if into is None else [into]))


def _share_halves(halves):
    nr = len(halves)

    def body(*refs):
        out = refs[nr:2 * nr]
        send, recv = refs[2 * nr:]
        x, y, c = _mesh_pos()
        copies = []
        for r in range(nr):
            for layer in range(2):
                cp = pltpu.make_async_remote_copy(
                    src_ref=out[r].at[layer, c], dst_ref=out[r].at[layer, c], send_sem=send.at[2 * r + layer],
                    recv_sem=recv.at[2 * r + layer], device_id=(x, y, 1 - c), device_id_type=MESH)
                cp.start()
                copies.append(cp)
        for r in range(nr):
            for layer in range(2):
                copies[2 * r + layer].wait_send()
                pltpu.make_async_remote_copy(
                    src_ref=out[r].at[layer, 1 - c], dst_ref=out[r].at[layer, 1 - c], send_sem=send.at[2 * r + layer],
                    recv_sem=recv.at[2 * r + layer], device_id=(x, y, 1 - c), device_id_type=MESH).wait_recv()

    return pl.pallas_call(
        body, name="grad_share_halves",
        in_specs=[ANY] * nr, out_specs=[ANY] * nr,
        out_shape=[jax.ShapeDtypeStruct(h.shape, h.dtype) for h in halves],
        input_output_aliases={r: r for r in range(nr)},
        scratch_shapes=[pltpu.SemaphoreType.DMA((2 * nr,))] * 2,
        compiler_params=pltpu.CompilerParams(has_side_effects=True),
    )(*halves)


def _sum_small(parts, name):
    n, rows, cols = parts.shape

    def body(p_ref, o_ref):
        acc = p_ref[0].astype(f32)
        for d in range(1, n):
            acc = acc + p_ref[d].astype(f32)
        o_ref[...] = acc

    return pl.pallas_call(
        body, name=name, grid=(rows // 16,),
        in_specs=[pl.BlockSpec((n, 16, cols), lambda i: (0, i, 0))], out_specs=pl.BlockSpec((16, cols), lambda i: (i, 0)),
        out_shape=jax.ShapeDtypeStruct((rows, cols), f32),
        compiler_params=_params(("parallel",)),
    )(parts)


def _adamw(w, g, m, v, name):
    n_l, rows, cols = w.shape
    budget = 42 * 1024 * 1024
    tr = next(rows // d for d in range(1, rows + 1)
              if rows % d == 0 and (rows // d) % 8 == 0 and (rows // d) * cols * 4 * 14 <= budget)

    def body(w_ref, g_ref, m_ref, v_ref, d_ref, nm_ref, nv_ref):
        gg = g_ref[...]
        nm = ADAM_B1 * m_ref[...] + (1.0 - ADAM_B1) * gg
        nv = ADAM_B2 * v_ref[...] + (1.0 - ADAM_B2) * (gg * gg)
        m_hat = nm / (1.0 - ADAM_B1 ** ADAM_STEP)
        v_hat = nv / (1.0 - ADAM_B2 ** ADAM_STEP)
        d_ref[...] = -ADAM_LR * (m_hat / (jnp.sqrt(v_hat) + ADAM_EPS) + ADAM_WD * w_ref[...])
        nm_ref[...] = nm
        nv_ref[...] = nv

    blk = pl.BlockSpec((None, tr, cols), lambda l, i: (l, i, 0))
    return pl.pallas_call(
        body, name=name, grid=(n_l, rows // tr),
        in_specs=[blk] * 4, out_specs=[blk] * 3, out_shape=[jax.ShapeDtypeStruct((n_l, rows, cols), f32)] * 3,
        compiler_params=_params(("parallel", "parallel")),
    )(w, g, m, v)


SMALL = ("mix_norm", "q_norm", "k_norm", "sinks", "sgu_norm", "w_s", "b_s", "ffn_norm", "conv_b", "conv_w")


def _pack_small(arrs):
    flat = jnp.concatenate([a.reshape(-1) for a in arrs])
    pad = (-flat.shape[0]) % (16 * 1024)
    return jnp.pad(flat, (0, pad)).reshape(-1, 1024)


def _unpack_small(pack, shapes):
    flat = pack.reshape(-1)
    out, off = [], 0
    for s in shapes:
        n = int(np.prod(s))
        out.append(flat[off:off + n].reshape(s))
        off += n
    return out


def kernel(x, mix_norm, w_in, q_norm, k_norm, sinks, sgu_norm, w_s, b_s, w_oa, w_ob, w_out, ffn_norm, w_up, conv_w, conv_b, w_down, loss_target, m_mix_norm, m_w_in, m_q_norm, m_k_norm, m_sinks, m_sgu_norm, m_w_s, m_b_s, m_w_oa, m_w_ob, m_w_out, m_ffn_norm, m_w_up, m_conv_w, m_conv_b, m_w_down, v_mix_norm, v_w_in, v_q_norm, v_k_norm, v_sinks, v_sgu_norm, v_w_s, v_b_s, v_w_oa, v_w_ob, v_w_out, v_ffn_norm, v_w_up, v_conv_w, v_conv_b, v_w_down):
    weights = dict(mix_norm=mix_norm, w_in=w_in, q_norm=q_norm, k_norm=k_norm, sinks=sinks, sgu_norm=sgu_norm,
                   w_s=w_s, b_s=b_s, w_oa=w_oa, w_ob=w_ob, w_out=w_out, ffn_norm=ffn_norm, w_up=w_up,
                   conv_w=conv_w, conv_b=conv_b, w_down=w_down)
    mom_m = dict(mix_norm=m_mix_norm, w_in=m_w_in, q_norm=m_q_norm, k_norm=m_k_norm, sinks=m_sinks,
                 sgu_norm=m_sgu_norm, w_s=m_w_s, b_s=m_b_s, w_oa=m_w_oa, w_ob=m_w_ob, w_out=m_w_out,
                 ffn_norm=m_ffn_norm, w_up=m_w_up, conv_w=m_conv_w, conv_b=m_conv_b, w_down=m_w_down)
    mom_v = dict(mix_norm=v_mix_norm, w_in=v_w_in, q_norm=v_q_norm, k_norm=v_k_norm, sinks=v_sinks,
                 sgu_norm=v_sgu_norm, w_s=v_w_s, b_s=v_b_s, w_oa=v_w_oa, w_ob=v_w_ob, w_out=v_w_out,
                 ffn_norm=v_ffn_norm, w_up=v_w_up, conv_w=v_conv_w, conv_b=v_conv_b, w_down=v_w_down)
    n_seq, seq, _ = x.shape
    T = n_seq * seq
    core = lax.axis_index("c")
    chip = 2 * lax.axis_index("x") + lax.axis_index("y")
    tm = min(512, seq)
    tm_ff = min(256, seq)
    tm_sgu = min(512, seq)
    tk_dw = min(2048, T)

    me = 2 * chip + core
    names = [r[0] for r in REGIONS]
    hrs = {name: rows // N_DEV for name, rows, _, _ in REGIONS}
    def placed(l, name, after=None):
        _, rows, cols, transposed = next(r for r in REGIONS if r[0] == name)
        shard = (jnp.swapaxes(weights[name], 1, 2) if transposed else weights[name]).reshape(2, 2, hrs[name], cols)
        return _place(shard, l, core, N_DEV, me, bf16, f"place_{name}_{l}", after).reshape(rows, cols)

    group_keys = [[(0, "w_in")], [(0, n) for n in names[1:]],
                  [(1, n) for n in ("w_in", "w_oa", "w_ob", "w_out")], [(1, "w_up"), (1, "w_down")]]
    n_direct = [1, 0, 0, 0]
    plans = [_gather_plan([hrs[n] for _, n in keys], nd) for keys, nd in zip(group_keys, n_direct)]
    first_bufs = [placed(0, "w_in"), conv_w, jnp.zeros((N_CHIP,) + conv_w.shape, f32)]
    started, tok = _split_start([(first_bufs, plans[0])], "gather_start_0")
    first_start_done = jnp.broadcast_to(tok[0:1, 0:1], (512, D_MODEL))
    rest_bufs = [[placed(l, n, first_start_done) for l, n in keys] for keys in group_keys[1:]]
    more, tok = _split_start(list(zip(rest_bufs, plans[1:])), "gather_start_1")
    started += more
    second_start_done = jnp.broadcast_to(tok[0:1, 0:1], (512, D_MODEL))
    gathered = [{}, {}]

    def arrived(g, after):
        send, recv, bufs = started[g]
        hr_list = [hrs[n] for _, n in group_keys[g]]
        bufs = _split_wait(bufs, send, recv, _gather_plan(hr_list, n_direct[g]), after, f"gather_wait_{g}")
        return bufs[:len(hr_list)], bufs[len(hr_list):]

    def start_pass(g, bufs):
        (res,), token = _split_start([(bufs, _pass_plan([hrs[n] for _, n in group_keys[g]]))], f"pass_start_{g}")
        return res, token[0:1, 0:1]

    def finish_pass(g, res, after):
        send, recv, bufs = res
        use(g, _split_wait(bufs, send, recv, _pass_plan([hrs[n] for _, n in group_keys[g]]), after, f"pass_wait_{g}"))

    def use(g, bufs):
        for (l, n), b in zip(group_keys[g], bufs):
            gathered[l][n] = b

    xs = x.reshape(T, D_MODEL)
    bufs, (_, conv_w_land) = arrived(0, second_start_done)
    use(0, _pass_to_sibling(bufs, [hrs["w_in"]], "gather_pass_0"))
    conv_w_all = lax.dynamic_update_slice(conv_w_land, conv_w[None], (chip, 0, 0, 0))
    conv_w_full = jnp.concatenate([conv_w_all[j] for j in range(N_CHIP)], axis=-1)
    saved = []
    cur = xs
    for l in range(2):
        wl = gathered[l]
        b_col = b_s[l].reshape(SGU_GROUPS, BLOCK, 1)
        qkv, su, sv, ga, gb, h = _in_proj(cur, mix_norm[l][None], wl["w_in"], l, tm)
        y_att = _attn_fwd(qkv, q_norm[l][None], k_norm[l][None], sinks[l], n_seq, seq)
        later = 1 if l == 0 else 3
        passing, tok = start_pass(later, arrived(later, y_att)[0])
        y_sgu = _sgu_fwd(su, sv, sgu_norm[l][None] + tok, w_s[l], b_col, tm_sgu)
        finish_pass(later, passing, y_sgu)
        x1, merged, a_o, b_o = _merge_fwd(cur, y_att, y_sgu, ga, gb, wl["w_oa"], wl["w_ob"], wl["w_out"], l, tm)
        h2, z, act = _ffn_up(x1, ffn_norm[l][None], wl["w_up"], conv_w_full[l], conv_b[l][None], l, seq, tm_ff)
        saved.append(dict(x=cur, qkv=qkv, su=su, sv=sv, ga=ga, gb=gb, h=h, y_att=y_att, y_sgu=y_sgu, x1=x1,
                          merged=merged, a=a_o, b=b_o, h2=h2, z=z, act=act, b_col=b_col))
        if l == 0:
            pass_2, tok = start_pass(2, arrived(2, act)[0])
            cur = _ffn_down(x1, act, wl["w_down"], tok, l, tm)
            finish_pass(2, pass_2, cur)
        else:
            dy, dyb, loss_part = _ffn_down_loss(x1, act, wl["w_down"], loss_target.reshape(T, D_MODEL), l, tm)

    core_arr = core.astype(jnp.int32).reshape(1)
    big = [{}, {}]
    small = {name: [None, None] for name in SMALL}

    def start_pairs(l, keys, tag):
        gl = [big[l][n] for n in keys]
        land = [lax.empty((1, N_CHIP, hrs[n], g.shape[2]), bf16) for n, g in zip(keys, gl)]
        (res,), token = _split_start([(gl + land, _pair_plan([hrs[n] for n in keys]))], f"pair_start_{tag}")
        return (l, keys, res, tag), token[0:1, 0:1]

    def pairs_to_chips(state, after):
        l, keys, (send, recv, bufs), tag = state
        bufs = _split_wait(bufs, send, recv, _pair_plan([hrs[n] for n in keys]), after, f"pair_wait_{tag}")
        return sums_to_chips(l, keys, bufs[:len(keys)], bufs[len(keys):], tag)

    def sums_to_chips(l, keys, gl, from_sibling, tag):
        pairs = [_pair_sum(g, o, core, chip, f"pair_sum_{n}_{l}") for g, o, n in zip(gl, from_sibling, keys)]
        bufs = [p[0] for p in pairs] + [p[1] for p in pairs]
        (res,), token = _split_start([(bufs, _chip_plan(len(keys), 1))], f"chip_start_{tag}")
        return (l, keys, res, tag), token[0:1, 0:1]

    def start_reduce(l, keys, tag):
        gl = [big[l][n] for n in keys]
        return sums_to_chips(l, keys, gl, _pair_exchange(gl, f"pair_exchange_{tag}"), tag)

    halves = {}

    def finish_reduce(state, after):
        l, keys, (send, recv, bufs), tag = state
        bufs = _split_wait(bufs, send, recv, _chip_plan(len(keys), 1), after, f"chip_wait_{tag}")
        for n, p in zip(keys, bufs[len(keys):]):
            halves[n] = _chip_sum(p, l, core_arr, halves.get(n), f"chip_sum_{n}_{l}")

    rest = [n for n in SMALL if n != "w_s"]
    rest_shapes = [weights[n].shape[1:] if n != "conv_w" else (3, 2 * D_FF) for n in rest]
    zero = jnp.zeros((), jnp.int32)

    def start_small(l):
        extra = loss_part[0, 0:1] if l == 1 else jnp.zeros((1,), f32)
        packs = [(_pack_small([small[n][l] for n in rest] + [extra]), f32, "small"),
                 (small["w_s"][l].reshape(-1, 1024), bf16, "w_s")]
        bufs = [_place(p[None, None], 0, zero, N_DEV, me, dt, f"place_{tag}_{l}") for p, dt, tag in packs]
        (res,), token = _split_start([(bufs, _all_to_all_plan(2))], f"small_start_{l}")
        return res, token[0:1, 0:1]

    def finish_small(l, res, after):
        send, recv, bufs = res
        bufs = _split_wait(bufs, send, recv, _all_to_all_plan(2), after, f"small_wait_{l}")
        out = dict(zip(rest + ["loss"], _unpack_small(_sum_small(bufs[0], f"sum_small_{l}"), rest_shapes + [(1,)])))
        out["w_s"] = _sum_small(bufs[1], f"sum_w_s_{l}").reshape(w_s.shape[1:])
        return out

    pending = []
    after_start = jnp.zeros((1, 1), f32)
    for l in (1, 0):
        s = saved[l]
        wl = gathered[l]
        dz, dconv = _ffn_bwd(dyb, s["z"], conv_w_full[l], conv_b[l][None] + after_start, wl["w_down"], l, seq, tm_ff)
        big[l]["w_down"] = _weight_grad(s["act"], dyb, 1408, tk_dw, f"dw_down_{l}")
        big[l]["w_up"] = _weight_grad(dz, s["h2"], 1408, tk_dw, f"dw_up_{l}")
        ffn_gain, sgu_gain, q_gain = ffn_norm[l][None], sgu_norm[l][None], q_norm[l][None]
        if l == 0:
            pairs_a, tok = start_pairs(0, ["w_down", "w_up"], "0a")
            ffn_gain = ffn_gain + tok
        dx1, dx1b, d_ffn = _norm_bwd([dz], wl["w_up"], l, s["x1"], ffn_gain, dy, tm, f"ffn_norm_bwd_{l}")
        if l == 0:
            state, tok = pairs_to_chips(pairs_a, dx1b)
            pending.append(state)
            sgu_gain = sgu_gain + tok
        small["conv_w"][l] = dconv[0:3]
        small["conv_b"][l] = dconv[3]
        small["ffn_norm"][l] = d_ffn[0]
        da, db, dga, dgb, dya, dys = _merge_bwd(dx1b, s["ga"], s["gb"], s["a"], s["b"],
                                                wl["w_oa"], wl["w_ob"], wl["w_out"], l, tm)
        big[l]["w_out"] = _weight_grad(s["merged"], dx1b, 1024, tk_dw, f"dw_out_{l}")
        big[l]["w_oa"] = _weight_grad(da, s["y_att"], 1024, tk_dw, f"dw_oa_{l}")
        big[l]["w_ob"] = _weight_grad(db, s["y_sgu"], 1024, tk_dw, f"dw_ob_{l}")
        if l == 0:
            pairs_m, tok = start_pairs(0, ["w_out", "w_oa", "w_ob"], "0m")
            sgu_gain = sgu_gain + tok
        dsu, dsv, d_ws, d_bs, d_sgu = _sgu_bwd(dys, s["su"], s["sv"], sgu_gain, w_s[l], s["b_col"], tm_sgu)
        if l == 0:
            state, tok = pairs_to_chips(pairs_m, dsv)
            pending.append(state)
            q_gain = q_gain + tok
        causal = np.tril(np.ones((BLOCK, BLOCK), bool))
        small["w_s"][l] = jnp.where(causal[None], d_ws, 0.0)
        small["b_s"][l] = d_bs[:, :, 0]
        small["sgu_norm"][l] = d_sgu[0]
        dqkv, d_qg, d_kg, d_sink = _attn_bwd(dya, s["qkv"], q_gain, k_norm[l][None], sinks[l], n_seq, seq)
        small["q_norm"][l] = d_qg[0]
        small["k_norm"][l] = d_kg[0]
        small["sinks"][l] = d_sink[:, 0]
        dproj = [dqkv, dsu, dsv, dga, dgb]
        big[l]["w_in"] = _weight_grad_rows(dproj, s["h"], tm, f"dw_in_{l}")
        if l == 1:
            pairs_1, tok = start_pairs(1, names, "1")
        else:
            state, tok = start_reduce(0, ["w_in"], "0b")
            pending.append(state)
        dy, dyb, d_mix = _norm_bwd(dproj, wl["w_in"], l, s["x"], mix_norm[l][None] + tok, dx1, tm, f"mix_norm_bwd_{l}")
        small["mix_norm"][l] = d_mix[0]
        if l == 1:
            state, tok = pairs_to_chips(pairs_1, dyb)
            pending.append(state)
            small_1, after_start = start_small(1)
            after_start = after_start + tok
    grad_x = dy.reshape(n_seq, seq, D_MODEL)

    small_0, _ = start_small(0)
    for state in pending:
        finish_reduce(state, dyb)
    shared = dict(zip(names, _share_halves([halves[n] for n in names])))
    grad, delta, new_m, new_v = {}, {}, {}, {}
    flip = lambda a: jnp.swapaxes(a, 1, 2)
    for name, rows, cols, transposed in REGIONS:
        g = shared[name].reshape(2, rows // N_CHIP, cols)
        if transposed and weights[name].shape[2] % 128:
            d, nm, nv = _adamw(flip(weights[name]), g, flip(mom_m[name]), flip(mom_v[name]), f"adamw_{name}")
            grad[name], delta[name], new_m[name], new_v[name] = flip(g), flip(d), flip(nm), flip(nv)
        else:
            grad[name] = flip(g) if transposed else g
            delta[name], new_m[name], new_v[name] = _adamw(weights[name], grad[name], mom_m[name], mom_v[name],
                                                           f"adamw_{name}")

    per_layer = [finish_small(0, small_0, delta["w_down"]), finish_small(1, small_1, dyb)]
    loss = per_layer[1]["loss"][0]
    grad_small = {n: jnp.stack([per_layer[0][n], per_layer[1][n]]) for n in SMALL}
    cw_cols = conv_w.shape[-1]
    grad_small["conv_w"] = lax.dynamic_slice_in_dim(grad_small["conv_w"], chip * cw_cols, cw_cols, axis=2)

    as_rows = lambda a: a.reshape(2, -1, BLOCK)
    d, nm, nv = _adamw(as_rows(w_s), as_rows(grad_small["w_s"]), as_rows(m_w_s), as_rows(v_w_s), "adamw_w_s")
    grad["w_s"], delta["w_s"], new_m["w_s"], new_v["w_s"] = (
        grad_small["w_s"], d.reshape(w_s.shape), nm.reshape(w_s.shape), nv.reshape(w_s.shape))
    shapes = [weights[n].shape for n in rest]
    d, nm, nv = _adamw(_pack_small([weights[n] for n in rest])[None], _pack_small([grad_small[n] for n in rest])[None],
                       _pack_small([mom_m[n] for n in rest])[None], _pack_small([mom_v[n] for n in rest])[None],
                       "adamw_small")
    for n, dd, mm, vv in zip(rest, _unpack_small(d, shapes), _unpack_small(nm, shapes), _unpack_small(nv, shapes)):
        grad[n], delta[n], new_m[n], new_v[n] = grad_small[n], dd, mm, vv

    order = ["mix_norm", "w_in", "q_norm", "k_norm", "sinks", "sgu_norm", "w_s", "b_s", "w_oa", "w_ob", "w_out",
             "ffn_norm", "w_up", "conv_w", "conv_b", "w_down"]
    return (loss, grad_x, *[grad[n] for n in order], *[delta[n] for n in order],
            *[new_m[n] for n in order], *[new_v[n] for n in order])
```

```python
import functools

import numpy as np
import jax
import jax.numpy as jnp
from jax import lax
from jax.experimental import pallas as pl
from jax.experimental.pallas import tpu as pltpu

bf16 = jnp.bfloat16
f32 = jnp.float32

D_MODEL = 1024
ATT_WIDTH = 512
KV_WIDTH = 128
SGU_WIDTH = 512
HEAD_DIM = 64
N_KV_HEADS = 2
Q_GROUP = 4
BLOCK = 128
SGU_GROUPS = 8
IN_WIDTH = 3840
D_FF = 2816
NORM_EPS = 1e-6
NEG_INF = -1e30
N_DEV = 8
N_CHIP = 4

ADAM_LR = 0.001
ADAM_B1 = 0.9
ADAM_B2 = 0.999
ADAM_EPS = 1e-08
ADAM_WD = 0.01
ADAM_STEP = 10

V7X_VMEM_LIMIT = 56 * 1024 * 1024
FF_CHUNK = 2816

REGIONS = (
    ("w_in", IN_WIDTH, D_MODEL, True),
    ("w_oa", D_MODEL, ATT_WIDTH, True),
    ("w_ob", D_MODEL, SGU_WIDTH, True),
    ("w_out", D_MODEL, D_MODEL, False),
    ("w_up", 2 * D_FF, D_MODEL, True),
    ("w_down", D_FF, D_MODEL, False),
)
MESH = pl.DeviceIdType.MESH
ANY = pl.BlockSpec(memory_space=pl.ANY)


def _params(sem=None, **kw):
    return pltpu.CompilerParams(dimension_semantics=sem, vmem_limit_bytes=V7X_VMEM_LIMIT, **kw)


def _wspec(rows, cols, layer=None):
    del layer
    return pl.BlockSpec((rows, cols), lambda *_: (0, 0), pipeline_mode=pl.Buffered(1))


def _full(shape):
    nd = len(shape)
    return pl.BlockSpec(shape, lambda *_: (0,) * nd)


def _dot_nn(a, b):
    return jnp.dot(a, b, preferred_element_type=f32)


def _dot_nt(a, b):
    return lax.dot_general(a, b, (((1,), (1,)), ((), ())), preferred_element_type=f32)


def _dot_tn(a, b):
    return lax.dot_general(a, b, (((0,), (0,)), ((), ())), preferred_element_type=f32)


_GELU_C = float(np.sqrt(2.0 / np.pi))


def _gelu(x):
    return 0.5 * x * (1.0 + jnp.tanh(_GELU_C * (x + 0.044715 * x * x * x)))


def _gelu_grad(x):
    t = jnp.tanh(_GELU_C * (x + 0.044715 * x * x * x))
    du = _GELU_C * (1.0 + 3.0 * 0.044715 * x * x)
    return 0.5 * (1.0 + t) + 0.5 * x * (1.0 - t * t) * du


def _rms(x):
    return lax.rsqrt(jnp.mean(x * x, axis=-1, keepdims=True) + NORM_EPS)


def _mesh_pos():
    return lax.axis_index("x"), lax.axis_index("y"), lax.axis_index("c")


def _in_proj(x, gain, w_in_t, layer, tm):
    T = x.shape[0]

    def body(x_ref, g_ref, w_ref, qkv_ref, su_ref, sv_ref, ga_ref, gb_ref, h_ref):
        xf = x_ref[...]
        h = (xf * _rms(xf) * g_ref[...]).astype(bf16)
        h_ref[...] = h
        qkv_ref[...] = _dot_nt(h, w_ref[0:768, :])
        su_ref[...] = _dot_nt(h, w_ref[768:1280, :]).astype(bf16)
        sv_ref[...] = _dot_nt(h, w_ref[1280:1792, :]).astype(bf16)
        ga_ref[...] = _dot_nt(h, w_ref[1792:2816, :]).astype(bf16)
        gb_ref[...] = _dot_nt(h, w_ref[2816:3840, :]).astype(bf16)

    row = lambda w: pl.BlockSpec((tm, w), lambda i: (i, 0))
    return pl.pallas_call(
        body, name=f"in_proj_{layer}", grid=(T // tm,),
        in_specs=[row(D_MODEL), _full((1, D_MODEL)), _wspec(IN_WIDTH, D_MODEL, layer)],
        out_specs=[row(768), row(512), row(512), row(1024), row(1024), row(D_MODEL)],
        out_shape=[jax.ShapeDtypeStruct((T, 768), f32), jax.ShapeDtypeStruct((T, 512), bf16),
                   jax.ShapeDtypeStruct((T, 512), bf16), jax.ShapeDtypeStruct((T, 1024), bf16),
                   jax.ShapeDtypeStruct((T, 1024), bf16), jax.ShapeDtypeStruct((T, D_MODEL), bf16)],
        compiler_params=_params(("parallel",)),
    )(x, gain, w_in_t)


def _attn_head_group(cur, prev, qg, kg, sink_ref, n, hk):
    lo = hk * HEAD_DIM
    k_raw = jnp.concatenate([prev[:, lo:lo + HEAD_DIM], cur[:, 512 + lo:512 + lo + HEAD_DIM]], axis=0)
    v_band = jnp.concatenate([prev[:, 128 + lo:128 + lo + HEAD_DIM], cur[:, 640 + lo:640 + lo + HEAD_DIM]], axis=0)
    rk = _rms(k_raw)
    k_hat = k_raw * rk
    kn = (k_hat * kg).astype(bf16)
    q_raw = jnp.concatenate(
        [cur[:, (hk * Q_GROUP + g) * HEAD_DIM:(hk * Q_GROUP + g + 1) * HEAD_DIM] for g in range(Q_GROUP)], axis=0)
    rq = _rms(q_raw)
    q_hat = q_raw * rq
    qn = (q_hat * qg * (HEAD_DIM ** -0.5)).astype(bf16)
    s = _dot_nt(qn, kn)
    rows = lax.broadcasted_iota(jnp.int32, (Q_GROUP * BLOCK, 1), 0)
    g_of_row = rows // BLOCK
    qi = rows - g_of_row * BLOCK
    kj = lax.broadcasted_iota(jnp.int32, (1, 2 * BLOCK), 1)
    dist = qi + BLOCK - kj
    valid = (dist >= 0) & (dist < BLOCK) & ((kj >= BLOCK) | (n > 0))
    slope = jnp.zeros((Q_GROUP * BLOCK, 1), f32)
    sink = jnp.zeros((Q_GROUP * BLOCK, 1), f32)
    for g in range(Q_GROUP):
        head = hk * Q_GROUP + g
        slope = jnp.where(g_of_row == g, float(np.exp2(-8.0 * (head + 1.0) / 8.0)), slope)
        sink = jnp.where(g_of_row == g, sink_ref[head], sink)
    s = jnp.where(valid, s - slope * dist.astype(f32), NEG_INF)
    m = jnp.maximum(jnp.max(s, axis=-1, keepdims=True), sink)
    e = jnp.exp(s - m)
    e_sink = jnp.exp(sink - m)
    inv = 1.0 / (jnp.sum(e, axis=-1, keepdims=True) + e_sink)
    return dict(k_raw=k_raw, rk=rk, k_hat=k_hat, kn=kn, v=v_band.astype(bf16), q_hat=q_hat, rq=rq, qn=qn,
                p=e * inv, p_sink=e_sink * inv)


def _attn_fwd(qkv, qg, kg, sinks, n_seq, seq):
    T = n_seq * seq
    nb = seq // BLOCK

    per = 2 if nb % 2 == 0 else 1

    def body(cur_ref, prev_ref, qg_ref, kg_ref, sink_ref, y_ref):
        for sub in range(per):
            n = pl.program_id(1) * per + sub
            cur = cur_ref[sub * BLOCK:(sub + 1) * BLOCK, :]
            prev = prev_ref[...] if sub == 0 else cur_ref[(sub - 1) * BLOCK:sub * BLOCK, 512:768]
            pieces = [None] * (N_KV_HEADS * Q_GROUP)
            for hk in range(N_KV_HEADS):
                a = _attn_head_group(cur, prev, qg_ref[...], kg_ref[...], sink_ref, n, hk)
                o = _dot_nn(a["p"].astype(bf16), a["v"])
                for g in range(Q_GROUP):
                    pieces[hk * Q_GROUP + g] = o[g * BLOCK:(g + 1) * BLOCK]
            y_ref[sub * BLOCK:(sub + 1) * BLOCK, :] = jnp.concatenate(pieces, axis=1).astype(bf16)

    return pl.pallas_call(
        body, name="attn_fwd", grid=(n_seq, nb // per),
        in_specs=[pl.BlockSpec((per * BLOCK, 768), lambda b, n: (b * (nb // per) + n, 0)),
                  pl.BlockSpec((BLOCK, 256), lambda b, n: (b * nb + jnp.maximum(n * per - 1, 0), 2)),
                  _full((1, HEAD_DIM)), _full((1, HEAD_DIM)),
                  pl.BlockSpec(memory_space=pltpu.SMEM)],
        out_specs=pl.BlockSpec((per * BLOCK, ATT_WIDTH), lambda b, n: (b * (nb // per) + n, 0)),
        out_shape=jax.ShapeDtypeStruct((T, ATT_WIDTH), bf16),
        compiler_params=_params(("parallel", "parallel")),
    )(qkv, qkv, qg, kg, sinks)


def _sgu_chunk(su, sv, gain, w_ref, b_ref):
    u = _gelu(su)
    vg = _gelu(sv)
    rv = _rms(vg)
    v_hat = vg * rv
    vn = (v_hat * gain).astype(bf16)
    causal = (lax.broadcasted_iota(jnp.int32, (BLOCK, BLOCK), 0) >= lax.broadcasted_iota(jnp.int32, (BLOCK, BLOCK), 1))
    w_tril = [jnp.where(causal, w_ref[g], 0.0).astype(bf16) for g in range(SGU_GROUPS)]
    gd = SGU_WIDTH // SGU_GROUPS
    mixed = jnp.concatenate(
        [_dot_nn(w_tril[g], vn[:, g * gd:(g + 1) * gd]) + b_ref[g] for g in range(SGU_GROUPS)], axis=1)
    return u, rv, v_hat, vn, w_tril, mixed


def _sgu_fwd(su, sv, gain, w_s, b_s, tm):
    T = su.shape[0]

    def body(su_ref, sv_ref, g_ref, w_ref, b_ref, y_ref):
        for ch in range(tm // BLOCK):
            rows = slice(ch * BLOCK, (ch + 1) * BLOCK)
            u, _, _, _, _, mixed = _sgu_chunk(su_ref[rows, :].astype(f32), sv_ref[rows, :].astype(f32),
                                              g_ref[...], w_ref, b_ref)
            y_ref[rows, :] = (u * mixed).astype(bf16)

    row = pl.BlockSpec((tm, SGU_WIDTH), lambda i: (i, 0))
    return pl.pallas_call(
        body, name="sgu_fwd", grid=(T // tm,),
        in_specs=[row, row, _full((1, SGU_WIDTH)), _full((SGU_GROUPS, BLOCK, BLOCK)), _full((SGU_GROUPS, BLOCK, 1))],
        out_specs=row, out_shape=jax.ShapeDtypeStruct((T, SGU_WIDTH), bf16),
        compiler_params=_params(("parallel",)),
    )(su, sv, gain, w_s, b_s)


def _merge_fwd(x, y_att, y_sgu, ga, gb, w_oa_t, w_ob_t, w_out, layer, tm):
    T = x.shape[0]

    def body(x_ref, ya_ref, ys_ref, ga_ref, gb_ref, woa_ref, wob_ref, wout_ref, x1_ref, m_ref, a_ref, b_ref):
        a = _dot_nt(ya_ref[...], woa_ref[...])
        b = _dot_nt(ys_ref[...], wob_ref[...])
        a_ref[...] = a.astype(bf16)
        b_ref[...] = b.astype(bf16)
        merged = (jax.nn.sigmoid(ga_ref[...].astype(f32)) * a + jax.nn.sigmoid(gb_ref[...].astype(f32)) * b).astype(bf16)
        m_ref[...] = merged
        x1_ref[...] = x_ref[...] + _dot_nn(merged, wout_ref[...])

    row = lambda w: pl.BlockSpec((tm, w), lambda i: (i, 0))
    return pl.pallas_call(
        body, name=f"merge_fwd_{layer}", grid=(T // tm,),
        in_specs=[row(D_MODEL), row(512), row(512), row(1024), row(1024),
                  _wspec(D_MODEL, ATT_WIDTH, layer), _wspec(D_MODEL, SGU_WIDTH, layer), _wspec(D_MODEL, D_MODEL, layer)],
        out_specs=[row(D_MODEL)] * 4,
        out_shape=[jax.ShapeDtypeStruct((T, D_MODEL), f32)] + [jax.ShapeDtypeStruct((T, D_MODEL), bf16)] * 3,
        compiler_params=_params(("parallel",)),
    )(x, y_att, y_sgu, ga, gb, w_oa_t, w_ob_t, w_out)


def _tile_permutation(tm):
    r = np.arange(tm)
    p = np.zeros((tm, tm), np.float32)
    p[r, (r % 8) * (tm // 8) + r // 8] = 1.0
    return jnp.asarray(p, bf16), jnp.asarray(p.T, bf16)


def _stage_taps_before(buf, zz, prev, tm):
    first = lax.broadcasted_iota(jnp.int32, (8, 1), 0) == 0
    buf[16:16 + tm, :] = zz
    buf[0:8, :] = jnp.where(first, prev[7:8], pltpu.roll(buf[tm:tm + 8, :], 1, 0))
    buf[8:16, :] = jnp.where(first, prev[15:16], pltpu.roll(buf[tm + 8:tm + 16, :], 1, 0))


def _stage_taps_after(buf, nxt, tm):
    last = lax.broadcasted_iota(jnp.int32, (8, 1), 0) == 7
    buf[tm:tm + 8, :] = jnp.where(last, nxt[0:1], pltpu.roll(buf[0:8, :], 7, 0))
    buf[tm + 8:tm + 16, :] = jnp.where(last, nxt[8:9], pltpu.roll(buf[8:16, :], 7, 0))


def _conv_rows(buf, r, n, coef):
    z2 = buf[pl.ds(r, n), :]
    z1 = buf[pl.ds(pl.multiple_of(r + 8, 8), n), :]
    z0 = buf[pl.ds(pl.multiple_of(r + 16, 8), n), :]
    return coef[0] + coef[1] * z2 + coef[2] * z1 + coef[3] * z0


def _ffn_up(x1, gain, w_up_t, conv_w, conv_b, layer, seq, tm):
    T = x1.shape[0]
    tps = seq // tm
    perm, perm_t = _tile_permutation(tm)

    rg = 16

    def body(x_ref, g_ref, w_ref, cw_ref, cb_ref, p_ref, pt_ref, h2_ref, z_ref, act_ref, carry_ref,
             zg_buf, zv_buf, actp_buf):
        i = pl.program_id(0)

        @pl.when(i % tps == 0)
        def _():
            carry_ref[...] = jnp.zeros_like(carry_ref)

        xf = x_ref[...]
        h2 = (xf * _rms(xf) * g_ref[...]).astype(bf16)
        h2_ref[...] = h2
        h2p = _dot_nn(p_ref[...], h2).astype(bf16)
        for cc in range(D_FF // FF_CHUNK):
            cols_g = slice(cc * FF_CHUNK, (cc + 1) * FF_CHUNK)
            cols_v = slice(D_FF + cc * FF_CHUNK, D_FF + (cc + 1) * FF_CHUNK)
            for buf, cols in ((zg_buf, cols_g), (zv_buf, cols_v)):
                zb = _dot_nt(h2p, w_ref[cols, :]).astype(bf16)
                z_ref[:, cols] = zb
                _stage_taps_before(buf, zb.astype(f32), carry_ref[:, cols], tm)
                carry_ref[:, cols] = buf[tm:tm + 16, :]
            coef = [jnp.broadcast_to(v, (rg, FF_CHUNK)) for cols in (cols_g, cols_v)
                    for v in (cb_ref[:, cols], cw_ref[0:1, cols], cw_ref[1:2, cols], cw_ref[2:3, cols])]

            def rows_step(j, carry, coef=coef):
                r = pl.multiple_of(j * rg, rg)
                zcg, zcv = (_conv_rows(buf, r, rg, coef[4 * k:4 * k + 4]) for k, buf in enumerate((zg_buf, zv_buf)))
                actp_buf[pl.ds(r, rg), :] = (zcg * jax.nn.sigmoid(zcg) * zcv).astype(bf16)
                return carry

            lax.fori_loop(0, tm // rg, rows_step, 0, unroll=True)
            act_ref[:, cols_g] = _dot_nn(pt_ref[...], actp_buf[...]).astype(bf16)

    row = lambda w: pl.BlockSpec((tm, w), lambda i: (i, 0))
    return pl.pallas_call(
        body, name=f"ffn_up_{layer}", grid=(T // tm,),
        in_specs=[row(D_MODEL), _full((1, D_MODEL)), _wspec(2 * D_FF, D_MODEL, layer),
                  _full((3, 2 * D_FF)), _full((1, 2 * D_FF)), _full((tm, tm)), _full((tm, tm))],
        out_specs=[row(D_MODEL), row(2 * D_FF), row(D_FF)],
        out_shape=[jax.ShapeDtypeStruct((T, D_MODEL), bf16), jax.ShapeDtypeStruct((T, 2 * D_FF), bf16),
                   jax.ShapeDtypeStruct((T, D_FF), bf16)],
        scratch_shapes=[pltpu.VMEM((16, 2 * D_FF), f32), pltpu.VMEM((tm + 16, FF_CHUNK), f32),
                        pltpu.VMEM((tm + 16, FF_CHUNK), f32), pltpu.VMEM((tm, FF_CHUNK), bf16)],
        compiler_params=_params(("arbitrary",)),
    )(x1, gain, w_up_t, conv_w, conv_b, perm, perm_t)


def _ffn_down(x1, act, w_down, after, layer, tm):
    T = x1.shape[0]

    def body(x_ref, a_ref, w_ref, after_ref, o_ref):
        o_ref[...] = x_ref[...] + _dot_nn(a_ref[...], w_ref[...])

    row = lambda w: pl.BlockSpec((tm, w), lambda i: (i, 0))
    return pl.pallas_call(
        body, name=f"ffn_down_{layer}", grid=(T // tm,),
        in_specs=[row(D_MODEL), row(D_FF), _wspec(D_FF, D_MODEL, layer), _full((1, 1))],
        out_specs=row(D_MODEL), out_shape=jax.ShapeDtypeStruct((T, D_MODEL), f32),
        compiler_params=_params(("parallel",)),
    )(x1, act, w_down, after)


def _ffn_down_loss(x1, act, w_down, target, layer, tm):
    T = x1.shape[0]

    def body(x_ref, a_ref, w_ref, t_ref, dy_ref, dyb_ref, loss_ref):
        @pl.when(pl.program_id(0) == 0)
        def _():
            loss_ref[...] = jnp.zeros_like(loss_ref)

        diff = x_ref[...] + _dot_nn(a_ref[...], w_ref[...]) - t_ref[...]
        loss_ref[...] += 0.5 * jnp.sum(jnp.mean(diff * diff, axis=-1, keepdims=True), axis=0, keepdims=True)
        dy = diff * (1.0 / D_MODEL)
        dy_ref[...] = dy
        dyb_ref[...] = dy.astype(bf16)

    row = lambda w: pl.BlockSpec((tm, w), lambda i: (i, 0))
    return pl.pallas_call(
        body, name=f"ffn_down_loss_{layer}", grid=(T // tm,),
        in_specs=[row(D_MODEL), row(D_FF), _wspec(D_FF, D_MODEL, layer), row(D_MODEL)],
        out_specs=[row(D_MODEL), row(D_MODEL), _full((8, 128))],
        out_shape=[jax.ShapeDtypeStruct((T, D_MODEL), f32), jax.ShapeDtypeStruct((T, D_MODEL), bf16),
                   jax.ShapeDtypeStruct((8, 128), f32)],
        compiler_params=_params(("arbitrary",)),
    )(x1, act, w_down, target)


def _ffn_bwd(dx2b, z, conv_w, conv_b, w_down, layer, seq, tm):
    T = z.shape[0]
    nt = T // tm
    tps = seq // tm

    perm, perm_t = _tile_permutation(tm)

    def body(dx_ref, z_ref, zh_ref, cw_ref, cb_ref, wd_ref, p_ref, pt_ref, dz_ref, dconv_ref, carry_ref,
             zg_buf, zv_buf, gg_buf, gv_buf, dact_buf, dzp_buf):
        i = pl.program_id(0)
        pos = (nt - 1 - i) % tps

        @pl.when(i == 0)
        def _():
            dconv_ref[...] = jnp.zeros_like(dconv_ref)

        @pl.when(pos == tps - 1)
        def _():
            carry_ref[...] = jnp.zeros_like(carry_ref)

        dxp = _dot_nn(p_ref[...], dx_ref[...]).astype(bf16)
        halo_on = (pos > 0).astype(f32)
        for cc in range(D_FF // FF_CHUNK):
            cols_g = slice(cc * FF_CHUNK, (cc + 1) * FF_CHUNK)
            cols_v = slice(D_FF + cc * FF_CHUNK, D_FF + (cc + 1) * FF_CHUNK)
            for buf, cols in ((zg_buf, cols_g), (zv_buf, cols_v)):
                _stage_taps_before(buf, z_ref[:, cols].astype(f32), zh_ref[:, cols].astype(f32) * halo_on, tm)
            dact_buf[...] = _dot_nt(dxp, wd_ref[cols_g, :])
            coef = [jnp.broadcast_to(v, (8, FF_CHUNK)) for cols in (cols_g, cols_v)
                    for v in (cb_ref[:, cols], cw_ref[0:1, cols], cw_ref[1:2, cols], cw_ref[2:3, cols])]

            def first_pass(j, sums, coef=coef):
                r = pl.multiple_of(j * 8, 8)
                rows = pl.ds(r, 8)
                zcg = _conv_rows(zg_buf, r, 8, coef[0:4])
                zcv = _conv_rows(zv_buf, r, 8, coef[4:8])
                sg = jax.nn.sigmoid(zcg)
                silu = zcg * sg
                d_act = dact_buf[rows, :]
                dg = d_act * zcv * sg * (1.0 + zcg * (1.0 - sg))
                dv = d_act * silu
                gg_buf[rows, :] = dg
                gv_buf[rows, :] = dv
                out = []
                for k, (g, buf) in enumerate(((dg, zg_buf), (dv, zv_buf))):
                    out += [sums[4 * k] + g * buf[rows, :],
                            sums[4 * k + 1] + g * buf[pl.ds(pl.multiple_of(r + 8, 8), 8), :],
                            sums[4 * k + 2] + g * buf[pl.ds(pl.multiple_of(r + 16, 8), 8), :],
                            sums[4 * k + 3] + g]
                return tuple(out)

            sums = lax.fori_loop(0, tm // 8, first_pass, tuple(jnp.zeros((8, FF_CHUNK), f32) for _ in range(8)),
                                 unroll=True)
            for k, cols in enumerate((cols_g, cols_v)):
                for tap in range(4):
                    dconv_ref[tap:tap + 1, cols] += jnp.sum(sums[4 * k + tap], axis=0, keepdims=True)
            for buf, cols in ((gg_buf, cols_g), (gv_buf, cols_v)):
                _stage_taps_after(buf, carry_ref[:, cols], tm)
                carry_ref[:, cols] = buf[0:16, :]
                w0, w1, w2 = (jnp.broadcast_to(cw_ref[k:k + 1, cols], (16, FF_CHUNK)) for k in range(3))

                def second_pass(j, carry, buf=buf, w0=w0, w1=w1, w2=w2):
                    r = pl.multiple_of(j * 16, 16)
                    dzp_buf[pl.ds(r, 16), :] = (w2 * buf[pl.ds(r, 16), :] + w1 * buf[pl.ds(pl.multiple_of(r + 8, 8), 16), :]
                                                + w0 * buf[pl.ds(pl.multiple_of(r + 16, 16), 16), :]).astype(bf16)
                    return carry

                lax.fori_loop(0, tm // 16, second_pass, 0, unroll=True)
                dz_ref[:, cols] = _dot_nn(pt_ref[...], dzp_buf[...]).astype(bf16)

    rev = lambda w: pl.BlockSpec((tm, w), lambda i: (nt - 1 - i, 0))
    return pl.pallas_call(
        body, name=f"ffn_bwd_{layer}", grid=(nt,),
        in_specs=[rev(D_MODEL), rev(2 * D_FF),
                  pl.BlockSpec((16, 2 * D_FF), lambda i: (jnp.maximum((nt - 1 - i) * (tm // 16) - 1, 0), 0)),
                  _full((3, 2 * D_FF)), _full((1, 2 * D_FF)), _wspec(D_FF, D_MODEL, layer),
                  _full((tm, tm)), _full((tm, tm))],
        out_specs=[rev(2 * D_FF), _full((8, 2 * D_FF))],
        out_shape=[jax.ShapeDtypeStruct((T, 2 * D_FF), bf16), jax.ShapeDtypeStruct((8, 2 * D_FF), f32)],
        scratch_shapes=[pltpu.VMEM((16, 2 * D_FF), f32)] + [pltpu.VMEM((tm + 16, FF_CHUNK), f32)] * 4
        + [pltpu.VMEM((tm, FF_CHUNK), f32), pltpu.VMEM((tm, FF_CHUNK), bf16)],
        compiler_params=_params(("arbitrary",)),
    )(dx2b, z, z, conv_w, conv_b, w_down, perm, perm_t)


def _norm_bwd(dys, w, layer, x, gain, dres, tm, name):
    T = dys[0].shape[0]
    widths = [d.shape[1] for d in dys]
    K = sum(widths)
    n = len(dys)

    def body(*refs):
        dy_refs = refs[:n]
        w_ref, x_ref, g_ref, dres_ref, dx_ref, dxb_ref, dg_ref = refs[n:]

        @pl.when(pl.program_id(0) == 0)
        def _():
            dg_ref[...] = jnp.zeros_like(dg_ref)

        dh, lo = None, 0
        for dy_ref, wd in zip(dy_refs, widths):
            part = _dot_nn(dy_ref[...], w_ref[lo:lo + wd, :])
            dh = part if dh is None else dh + part
            lo += wd
        xf = x_ref[...]
        r = _rms(xf)
        x_hat = xf * r
        dg_ref[...] += jnp.sum(dh * x_hat, axis=0, keepdims=True)
        dxh = dh * g_ref[...]
        dx = dres_ref[...] + r * (dxh - x_hat * jnp.mean(dxh * x_hat, axis=-1, keepdims=True))
        dx_ref[...] = dx
        dxb_ref[...] = dx.astype(bf16)

    row = lambda w_: pl.BlockSpec((tm, w_), lambda i: (i, 0))
    return pl.pallas_call(
        body, name=name, grid=(T // tm,),
        in_specs=[row(wd) for wd in widths] + [_wspec(K, D_MODEL, layer), row(D_MODEL), _full((1, D_MODEL)), row(D_MODEL)],
        out_specs=[row(D_MODEL), row(D_MODEL), _full((1, D_MODEL))],
        out_shape=[jax.ShapeDtypeStruct((T, D_MODEL), f32), jax.ShapeDtypeStruct((T, D_MODEL), bf16),
                   jax.ShapeDtypeStruct((1, D_MODEL), f32)],
        compiler_params=_params(("arbitrary",)),
    )(*dys, w, x, gain, dres)


def _merge_bwd(dx1b, ga, gb, a, b, w_oa_t, w_ob_t, w_out, layer, tm):
    T = dx1b.shape[0]

    def body(dx_ref, ga_ref, gb_ref, a_ref, b_ref, woa_ref, wob_ref, wout_ref,
             da_ref, db_ref, dga_ref, dgb_ref, dya_ref, dys_ref):
        dm = _dot_nt(dx_ref[...], wout_ref[...])
        sa = jax.nn.sigmoid(ga_ref[...].astype(f32))
        sb = jax.nn.sigmoid(gb_ref[...].astype(f32))
        da = (dm * sa).astype(bf16)
        db = (dm * sb).astype(bf16)
        da_ref[...] = da
        db_ref[...] = db
        dga_ref[...] = (dm * a_ref[...].astype(f32) * sa * (1.0 - sa)).astype(bf16)
        dgb_ref[...] = (dm * b_ref[...].astype(f32) * sb * (1.0 - sb)).astype(bf16)
        dya_ref[...] = _dot_nn(da, woa_ref[...]).astype(bf16)
        dys_ref[...] = _dot_nn(db, wob_ref[...]).astype(bf16)

    row = lambda w: pl.BlockSpec((tm, w), lambda i: (i, 0))
    return pl.pallas_call(
        body, name=f"merge_bwd_{layer}", grid=(T // tm,),
        in_specs=[row(D_MODEL)] * 5 + [_wspec(D_MODEL, ATT_WIDTH, layer), _wspec(D_MODEL, SGU_WIDTH, layer),
                                       _wspec(D_MODEL, D_MODEL, layer)],
        out_specs=[row(D_MODEL)] * 4 + [row(512)] * 2,
        out_shape=[jax.ShapeDtypeStruct((T, D_MODEL), bf16)] * 4 + [jax.ShapeDtypeStruct((T, 512), bf16)] * 2,
        compiler_params=_params(("parallel",)),
    )(dx1b, ga, gb, a, b, w_oa_t, w_ob_t, w_out)


def _sgu_bwd(dy, su, sv, gain, w_s, b_s, tm):
    T = su.shape[0]
    gd = SGU_WIDTH // SGU_GROUPS

    def body(dy_ref, su_ref, sv_ref, g_ref, w_ref, b_ref, dsu_ref, dsv_ref, dw_ref, db_ref, dg_ref):
        @pl.when(pl.program_id(0) == 0)
        def _():
            dw_ref[...] = jnp.zeros_like(dw_ref)
            db_ref[...] = jnp.zeros_like(db_ref)
            dg_ref[...] = jnp.zeros_like(dg_ref)

        gain_v = g_ref[...]
        for ch in range(tm // BLOCK):
            rows = slice(ch * BLOCK, (ch + 1) * BLOCK)
            su_c = su_ref[rows, :].astype(f32)
            sv_c = sv_ref[rows, :].astype(f32)
            u, rv, v_hat, vn, w_tril, mixed = _sgu_chunk(su_c, sv_c, gain_v, w_ref, b_ref)
            dyc = dy_ref[rows, :].astype(f32)
            dsu_ref[rows, :] = (dyc * mixed * _gelu_grad(su_c)).astype(bf16)
            dmix = dyc * u
            dmix_b = dmix.astype(bf16)
            dvn = []
            for g in range(SGU_GROUPS):
                gs = slice(g * gd, (g + 1) * gd)
                db_ref[g] += jnp.sum(dmix[:, gs], axis=1, keepdims=True)
                dw_ref[g] += _dot_nt(dmix_b[:, gs], vn[:, gs])
                dvn.append(_dot_tn(w_tril[g], dmix_b[:, gs]))
            dvn = jnp.concatenate(dvn, axis=1)
            dg_ref[...] += jnp.sum(dvn * v_hat, axis=0, keepdims=True)
            dxh = dvn * gain_v
            dvg = rv * (dxh - v_hat * jnp.mean(dxh * v_hat, axis=-1, keepdims=True))
            dsv_ref[rows, :] = (dvg * _gelu_grad(sv_c)).astype(bf16)

    row = pl.BlockSpec((tm, SGU_WIDTH), lambda i: (i, 0))
    return pl.pallas_call(
        body, name="sgu_bwd", grid=(T // tm,),
        in_specs=[row, row, row, _full((1, SGU_WIDTH)), _full((SGU_GROUPS, BLOCK, BLOCK)),
                  _full((SGU_GROUPS, BLOCK, 1))],
        out_specs=[row, row, _full((SGU_GROUPS, BLOCK, BLOCK)), _full((SGU_GROUPS, BLOCK, 1)), _full((1, SGU_WIDTH))],
        out_shape=[jax.ShapeDtypeStruct((T, SGU_WIDTH), bf16)] * 2 + [
            jax.ShapeDtypeStruct((SGU_GROUPS, BLOCK, BLOCK), f32), jax.ShapeDtypeStruct((SGU_GROUPS, BLOCK, 1), f32),
            jax.ShapeDtypeStruct((1, SGU_WIDTH), f32)],
        compiler_params=_params(("arbitrary",)),
    )(dy, su, sv, gain, w_s, b_s)


def _attn_bwd(dy, qkv, qg, kg, sinks, n_seq, seq):
    T = n_seq * seq
    nb = seq // BLOCK
    scale = HEAD_DIM ** -0.5
    per = 1
    ng = nb // per

    def body(dy_ref, cur_ref, prev_ref, qg_ref, kg_ref, sink_ref, dqkv_ref, dqg_ref, dkg_ref, dsink_ref,
             carry_k, carry_v):
        b = pl.program_id(0)
        j = pl.program_id(1)

        @pl.when((b == 0) & (j == 0))
        def _():
            dqg_ref[...] = jnp.zeros_like(dqg_ref)
            dkg_ref[...] = jnp.zeros_like(dkg_ref)
            dsink_ref[...] = jnp.zeros_like(dsink_ref)

        @pl.when(j == 0)
        def _():
            carry_k[...] = jnp.zeros_like(carry_k)
            carry_v[...] = jnp.zeros_like(carry_v)

        for sub in reversed(range(per)):
            rows = slice(sub * BLOCK, (sub + 1) * BLOCK)
            prev = prev_ref[...] if sub == 0 else cur_ref[(sub - 1) * BLOCK:sub * BLOCK, 512:768]
            one_block(dy_ref[rows, :].astype(f32), cur_ref[rows, :], prev, (ng - 1 - j) * per + sub,
                      qg_ref[...], kg_ref[...], sink_ref, dqkv_ref.at[rows, :], dqg_ref, dkg_ref, dsink_ref,
                      carry_k, carry_v)

    def one_block(dyf, cur, prev, n, qg_v, kg_v, sink_ref, dqkv_ref, dqg_ref, dkg_ref, dsink_ref, carry_k, carry_v):
        dq_pieces = [None] * (N_KV_HEADS * Q_GROUP)
        dk_pieces, dv_pieces = [], []
        for hk in range(N_KV_HEADS):
            a = _attn_head_group(cur, prev, qg_v, kg_v, sink_ref, n, hk)
            do = jnp.concatenate(
                [dyf[:, (hk * Q_GROUP + g) * HEAD_DIM:(hk * Q_GROUP + g + 1) * HEAD_DIM] for g in range(Q_GROUP)],
                axis=0).astype(bf16)
            p = a["p"]
            dp = _dot_nt(do, a["v"])
            dv_band = _dot_tn(p.astype(bf16), do)
            dsum = jnp.sum(p * dp, axis=-1, keepdims=True)
            ds = (p * (dp - dsum)).astype(bf16)
            dsink_col = -a["p_sink"] * dsum
            for g in range(Q_GROUP):
                head = hk * Q_GROUP + g
                dsink_ref[head:head + 1, :] += jnp.sum(dsink_col[g * BLOCK:(g + 1) * BLOCK], axis=0, keepdims=True)
            dqn = _dot_nn(ds, a["kn"])
            dkn_band = _dot_tn(ds, a["qn"])
            dq_hat_g = dqn * scale
            dqg_ref[...] += jnp.sum(dq_hat_g * a["q_hat"], axis=0, keepdims=True)
            dxh = dq_hat_g * qg_v
            dq = a["rq"] * (dxh - a["q_hat"] * jnp.mean(dxh * a["q_hat"], axis=-1, keepdims=True))
            for g in range(Q_GROUP):
                dq_pieces[hk * Q_GROUP + g] = dq[g * BLOCK:(g + 1) * BLOCK]
            dkn = dkn_band[BLOCK:] + carry_k[hk]
            dv_pieces.append(dv_band[BLOCK:] + carry_v[hk])
            carry_k[hk] = dkn_band[:BLOCK]
            carry_v[hk] = dv_band[:BLOCK]
            k_hat = a["k_hat"][BLOCK:]
            dkg_ref[...] += jnp.sum(dkn * k_hat, axis=0, keepdims=True)
            dxk = dkn * kg_v
            dk_pieces.append(a["rk"][BLOCK:] * (dxk - k_hat * jnp.mean(dxk * k_hat, axis=-1, keepdims=True)))
        dqkv_ref[...] = jnp.concatenate(dq_pieces + dk_pieces + dv_pieces, axis=1).astype(bf16)

    blk = lambda w: pl.BlockSpec((per * BLOCK, w), lambda b, j: (b * ng + ng - 1 - j, 0))
    return pl.pallas_call(
        body, name="attn_bwd", grid=(n_seq, ng),
        in_specs=[blk(ATT_WIDTH), blk(768),
                  pl.BlockSpec((BLOCK, 256), lambda b, j: (b * nb + jnp.maximum((ng - 1 - j) * per - 1, 0), 2)),
                  _full((1, HEAD_DIM)), _full((1, HEAD_DIM)), pl.BlockSpec(memory_space=pltpu.SMEM)],
        out_specs=[blk(768), _full((1, HEAD_DIM)), _full((1, HEAD_DIM)), _full((8, 128))],
        out_shape=[jax.ShapeDtypeStruct((T, 768), bf16), jax.ShapeDtypeStruct((1, HEAD_DIM), f32),
                   jax.ShapeDtypeStruct((1, HEAD_DIM), f32), jax.ShapeDtypeStruct((8, 128), f32)],
        scratch_shapes=[pltpu.VMEM((N_KV_HEADS, BLOCK, HEAD_DIM), f32), pltpu.VMEM((N_KV_HEADS, BLOCK, HEAD_DIM), f32)],
        compiler_params=_params(("arbitrary", "arbitrary")),
    )(dy, qkv, qkv, qg, kg, sinks)


def _weight_grad(a, b, tm, tk, name):
    T, M = a.shape
    N = b.shape[1]
    nk = T // tk

    def body(a_ref, b_ref, o_ref, acc_ref):
        k = pl.program_id(1)

        @pl.when(k == 0)
        def _():
            acc_ref[...] = jnp.zeros_like(acc_ref)

        acc_ref[...] += _dot_tn(a_ref[...], b_ref[...])

        @pl.when(k == nk - 1)
        def _():
            o_ref[...] = acc_ref[...].astype(bf16)

    return pl.pallas_call(
        body, name=name, grid=(M // tm, nk),
        in_specs=[pl.BlockSpec((tk, tm), lambda i, k: (k, i)), pl.BlockSpec((tk, N), lambda i, k: (k, 0))],
        out_specs=pl.BlockSpec((None, tm, N), lambda i, k: (0, i, 0)),
        out_shape=jax.ShapeDtypeStruct((1, M, N), bf16),
        scratch_shapes=[pltpu.VMEM((tm, N), f32)],
        compiler_params=_params(("parallel", "arbitrary")),
    )(a, b)


def _weight_grad_rows(a_list, b, tk, name):
    T, N = b.shape
    widths = [a.shape[1] for a in a_list]
    M = sum(widths)
    nk = T // tk
    n = len(a_list)

    def body(*refs):
        a_refs = refs[:n]
        b_ref, o_ref, acc_ref = refs[n:]
        k = pl.program_id(0)

        @pl.when(k == 0)
        def _():
            acc_ref[...] = jnp.zeros_like(acc_ref)

        lo = 0
        for a_ref, wd in zip(a_refs, widths):
            acc_ref[lo:lo + wd, :] += _dot_tn(a_ref[...], b_ref[...])
            lo += wd

        @pl.when(k == nk - 1)
        def _():
            o_ref[...] = acc_ref[...].astype(bf16)

    return pl.pallas_call(
        body, name=name, grid=(nk,),
        in_specs=[pl.BlockSpec((tk, wd), lambda k: (k, 0)) for wd in widths] + [pl.BlockSpec((tk, N), lambda k: (k, 0))],
        out_specs=pl.BlockSpec((None, M, N), lambda k: (0, 0, 0), pipeline_mode=pl.Buffered(1)),
        out_shape=jax.ShapeDtypeStruct((1, M, N), bf16),
        scratch_shapes=[pltpu.VMEM((M, N), f32)],
        compiler_params=_params(("arbitrary",)),
    )(*a_list, b)


def _place(src, layer, src_slot, n_slots, dst_slot, dtype, name, after=None):
    _, _, rows, cols = src.shape
    slots = jnp.stack([src_slot, dst_slot]).astype(jnp.int32)

    def body(slots_ref, s_ref, *rest):
        rest[-1][...] = s_ref[...].astype(dtype)

    return pl.pallas_call(
        body, name=name,
        grid_spec=pltpu.PrefetchScalarGridSpec(
            num_scalar_prefetch=1, grid=(1,),
            in_specs=[pl.BlockSpec((None, None, rows, cols), lambda i, sl: (layer, sl[0], 0, 0))]
            + ([] if after is None else [ANY]),
            out_specs=pl.BlockSpec((None, rows, cols), lambda i, sl: (sl[1], 0, 0))),
        out_shape=jax.ShapeDtypeStruct((n_slots, rows, cols), dtype),
        compiler_params=_params(("arbitrary",)),
    )(slots, src, *([] if after is None else [after]))


HBM = pl.BlockSpec(memory_space=pltpu.HBM)
SEM = pl.BlockSpec(memory_space=pltpu.SEMAPHORE)
DATAFLOW = pltpu.SideEffectType.DATAFLOW_SIDE_EFFECTING


def _other_chips(x, y):
    return [(1 - x, y), (x, 1 - y), (1 - x, 1 - y)]


def _split_start(groups, name):
    nb = [len(bufs) for bufs, _ in groups]
    flat = [b for bufs, _ in groups for b in bufs]
    ns = [len(plan(bufs, dry=True)) for bufs, plan in groups]
    ng = len(groups)

    def body(*refs):
        n_in = len(flat)
        sems = refs[n_in:n_in + 2 * ng]
        thru = refs[n_in + 2 * ng:2 * n_in + 2 * ng]
        token = refs[2 * n_in + 2 * ng]
        off = 0
        for g, (bufs, plan) in enumerate(groups):
            mine = thru[off:off + nb[g]]
            off += nb[g]
            for k, (src, dst, to) in enumerate(plan(mine)):
                pltpu.make_async_remote_copy(
                    src_ref=src, dst_ref=dst, send_sem=sems[2 * g].at[k], recv_sem=sems[2 * g + 1].at[k],
                    device_id=to, device_id_type=MESH).start()
        token[...] = jnp.zeros_like(token)

    out_shape = []
    for n in ns:
        out_shape += [pltpu.SemaphoreType.DMA((n,)), pltpu.SemaphoreType.DMA((n,))]
    out_shape += [pltpu.HBM(b.shape, b.dtype) for b in flat]
    out_shape.append(jax.ShapeDtypeStruct((8, 128), f32))
    res = pl.pallas_call(
        body, name=name, out_shape=tuple(out_shape),
        in_specs=[HBM] * len(flat),
        out_specs=tuple([SEM] * (2 * ng) + [HBM] * len(flat) + [pl.BlockSpec(memory_space=pltpu.VMEM)]),
        input_output_aliases={i: 2 * ng + i for i in range(len(flat))},
        compiler_params=pltpu.CompilerParams(has_side_effects=DATAFLOW),
    )(*[pltpu.with_memory_space_constraint(b, pltpu.HBM) for b in flat])
    out, off = [], 2 * ng
    for g in range(ng):
        out.append((res[2 * g], res[2 * g + 1], list(res[off:off + nb[g]])))
        off += nb[g]
    return out, res[-1]


def _split_wait(bufs, send, recv, plan, after, name):
    nb = len(bufs)

    def body(*refs):
        thru = refs[:nb]
        send_ref, recv_ref = refs[nb], refs[nb + 1]
        for k, (src, dst, to) in enumerate(plan(thru)):
            cp = pltpu.make_async_remote_copy(
                src_ref=src, dst_ref=dst, send_sem=send_ref.at[k], recv_sem=recv_ref.at[k],
                device_id=to, device_id_type=MESH)
            cp.wait_send()
            cp.wait_recv()

    res = pl.pallas_call(
        body, name=name, out_shape=tuple(pltpu.HBM(b.shape, b.dtype) for b in bufs),
        in_specs=[HBM] * nb + [SEM, SEM, ANY], out_specs=tuple([HBM] * nb),
        input_output_aliases={i: i for i in range(nb)},
        compiler_params=pltpu.CompilerParams(has_side_effects=DATAFLOW),
    )(*bufs, send, recv, after)
    return list(res)


def _gather_plan(hrs, n_direct=0):
    def plan(refs, dry=False):
        if dry:
            return [None] * (4 * len(hrs) + 3 * n_direct)
        x, y, c = _mesh_pos()
        me = 4 * x + 2 * y + c
        out = []
        for i in range(n_direct):
            src, land = refs[len(hrs) + 2 * i], refs[len(hrs) + 2 * i + 1]
            out += [(src, land.at[2 * x + y], (*chip, c)) for chip in _other_chips(x, y)]
        for ref, hr in zip(refs, hrs):
            rows = ref.at[pl.ds(pl.multiple_of(me * hr, 16), hr), :]
            out.append((rows, rows, (x, y, 1 - c)))
            out += [(rows, rows, (*chip, c)) for chip in _other_chips(x, y)]
        return out
    return plan


def _pass_plan(hrs):
    def plan(refs, dry=False):
        if dry:
            return [None] * (3 * len(hrs))
        x, y, c = _mesh_pos()
        out = []
        for ref, hr in zip(refs, hrs):
            for chip in _other_chips(x, y):
                rows = ref.at[pl.ds(pl.multiple_of((4 * chip[0] + 2 * chip[1] + c) * hr, 16), hr), :]
                out.append((rows, rows, (x, y, 1 - c)))
        return out
    return plan


def _pair_plan(hrs):
    n = len(hrs)

    def plan(refs, dry=False):
        if dry:
            return [None] * (N_CHIP * n)
        x, y, c = _mesh_pos()
        out = []
        for r in range(n):
            for j in range(N_CHIP):
                start = pl.multiple_of((2 * j + 1 - c) * hrs[r], 16)
                out.append((refs[r].at[0, pl.ds(start, hrs[r]), :], refs[n + r].at[0, j], (x, y, 1 - c)))
        return out
    return plan


def _all_to_all_plan(n):
    def plan(refs, dry=False):
        if dry:
            return [None] * (7 * n)
        x, y, c = _mesh_pos()
        out = []
        for ref in refs:
            mine = ref.at[4 * x + 2 * y + c]
            for fx in range(2):
                for fy in range(2):
                    for fc in range(2):
                        if fx or fy or fc:
                            out.append((mine, mine, (1 - x if fx else x, 1 - y if fy else y, 1 - c if fc else c)))
        return out
    return plan


def _pass_to_sibling(bufs, hrs, name):
    nb = len(bufs)

    def body(*refs):
        out = refs[nb:2 * nb]
        send, recv = refs[2 * nb:]
        x, y, c = _mesh_pos()
        chips = _other_chips(x, y)
        started = []
        for i in range(nb):
            for j, chip in enumerate(chips):
                rows = out[i].at[pl.ds(pl.multiple_of((4 * chip[0] + 2 * chip[1] + c) * hrs[i], 16), hrs[i]), :]
                cp = pltpu.make_async_remote_copy(
                    src_ref=rows, dst_ref=rows, send_sem=send.at[3 * i + j], recv_sem=recv.at[3 * i + j],
                    device_id=(x, y, 1 - c), device_id_type=MESH)
                cp.start()
                started.append(cp)
        for i in range(nb):
            for j, chip in enumerate(chips):
                rows = out[i].at[pl.ds(pl.multiple_of((4 * chip[0] + 2 * chip[1] + 1 - c) * hrs[i], 16), hrs[i]), :]
                pltpu.make_async_remote_copy(
                    src_ref=rows, dst_ref=rows, send_sem=send.at[3 * i + j], recv_sem=recv.at[3 * i + j],
                    device_id=(x, y, 1 - c), device_id_type=MESH).wait_recv()
        for cp in started:
            cp.wait_send()

    return list(pl.pallas_call(
        body, name=name, in_specs=[ANY] * nb, out_specs=[ANY] * nb,
        out_shape=[jax.ShapeDtypeStruct(b.shape, b.dtype) for b in bufs],
        input_output_aliases={i: i for i in range(nb)},
        scratch_shapes=[pltpu.SemaphoreType.DMA((3 * nb,)), pltpu.SemaphoreType.DMA((3 * nb,))],
        compiler_params=pltpu.CompilerParams(has_side_effects=True),
    )(*bufs))


def _pair_exchange(grads, name):
    nr = len(grads)
    n_l = grads[0].shape[0]
    n_sem = nr * n_l * N_CHIP

    def body(*refs):
        src = refs[:nr]
        out = refs[nr:2 * nr]
        send, recv = refs[2 * nr:]
        x, y, c = _mesh_pos()
        copies = []
        for r in range(nr):
            hr = grads[r].shape[1] // N_DEV
            for layer in range(n_l):
                for j in range(N_CHIP):
                    idx = (r * n_l + layer) * N_CHIP + j
                    start = pl.multiple_of((2 * j + 1 - c) * hr, 16)
                    cp = pltpu.make_async_remote_copy(
                        src_ref=src[r].at[layer, pl.ds(start, hr), :], dst_ref=out[r].at[layer, j],
                        send_sem=send.at[idx], recv_sem=recv.at[idx], device_id=(x, y, 1 - c), device_id_type=MESH)
                    cp.start()
                    copies.append(cp)
        for cp in copies:
            cp.wait()

    return pl.pallas_call(
        body, name=name,
        in_specs=[ANY] * nr, out_specs=[ANY] * nr,
        out_shape=[jax.ShapeDtypeStruct((n_l, N_CHIP, g.shape[1] // N_DEV, g.shape[2]), bf16) for g in grads],
        scratch_shapes=[pltpu.SemaphoreType.DMA((n_sem,)), pltpu.SemaphoreType.DMA((n_sem,))],
        compiler_params=pltpu.CompilerParams(has_side_effects=True),
    )(*grads)


def _pair_sum(grad, other, core, chip, name):
    n_l, rows, cols = grad.shape
    hr = rows // N_DEV
    g5 = grad.reshape(n_l, N_CHIP, 2, hr, cols)
    where = jnp.stack([core, chip]).astype(jnp.int32)

    def body(where_ref, g_ref, o_ref, s_ref, mine_ref):
        s_ref[...] = (g_ref[...].astype(f32) + o_ref[...].astype(f32)).astype(bf16)
        mine_ref[...] = s_ref[where_ref[1]]

    return pl.pallas_call(
        body, name=name,
        grid_spec=pltpu.PrefetchScalarGridSpec(
            num_scalar_prefetch=1, grid=(n_l,),
            in_specs=[pl.BlockSpec((None, N_CHIP, None, hr, cols), lambda l, w: (l, 0, w[0], 0, 0)),
                      pl.BlockSpec((None, N_CHIP, hr, cols), lambda l, w: (l, 0, 0, 0))],
            out_specs=[pl.BlockSpec((None, N_CHIP, hr, cols), lambda l, w: (l, 0, 0, 0)),
                       pl.BlockSpec((None, None, hr, cols), lambda l, w: (l, w[1], 0, 0))]),
        out_shape=[jax.ShapeDtypeStruct((n_l, N_CHIP, hr, cols), bf16)] * 2,
        compiler_params=_params(("arbitrary",)),
    )(where, g5, other)


def _chip_plan(nr, n_l):
    def plan(refs, dry=False):
        if dry:
            return [None] * (nr * n_l * 3)
        x, y, c = _mesh_pos()
        out = []
        for r in range(nr):
            for layer in range(n_l):
                for chip in _other_chips(x, y):
                    out.append((refs[r].at[layer, 2 * chip[0] + chip[1]], refs[nr + r].at[layer, 2 * x + y], (*chip, c)))
        return out
    return plan


def _chip_sum(parts, core, name):
    _, _, hr, cols = parts[0].shape

    def body(core_ref, p0_ref, p1_ref, o_ref):
        def total(p_ref):
            acc = p_ref[0].astype(f32) + p_ref[1].astype(f32)
            acc = acc + p_ref[2].astype(f32)
            return acc + p_ref[3].astype(f32)

        @pl.when(pl.program_id(0) == 0)
        def _():
            o_ref[...] = total(p0_ref)

        @pl.when(pl.program_id(0) == 1)
        def _():
            o_ref[...] = total(p1_ref)

    spec = pl.BlockSpec((None, N_CHIP, hr, cols), lambda l, cr: (0, 0, 0, 0))
    return pl.pallas_call(
        body, name=name,
        grid_spec=pltpu.PrefetchScalarGridSpec(
            num_scalar_prefetch=1, grid=(2,), in_specs=[spec, spec],
            out_specs=pl.BlockSpec((None, None, hr, cols), lambda l, cr: (l, cr[0], 0, 0))),
        out_shape=jax.ShapeDtypeStruct((2, 2, hr, cols), f32),
        compiler_params=_params(("arbitrary",)),
    )(core, parts[0], parts[1])


def _share_halves(halves):
    nr = len(halves)

    def body(*refs):
        out = refs[nr:2 * nr]
        send, recv = refs[2 * nr:]
        x, y, c = _mesh_pos()
        copies = []
        for r in range(nr):
            for layer in range(2):
                cp = pltpu.make_async_remote_copy(
                    src_ref=out[r].at[layer, c], dst_ref=out[r].at[layer, c], send_sem=send.at[2 * r + layer],
                    recv_sem=recv.at[2 * r + layer], device_id=(x, y, 1 - c), device_id_type=MESH)
                cp.start()
                copies.append(cp)
        for r in range(nr):
            for layer in range(2):
                copies[2 * r + layer].wait_send()
                pltpu.make_async_remote_copy(
                    src_ref=out[r].at[layer, 1 - c], dst_ref=out[r].at[layer, 1 - c], send_sem=send.at[2 * r + layer],
                    recv_sem=recv.at[2 * r + layer], device_id=(x, y, 1 - c), device_id_type=MESH).wait_recv()

    return pl.pallas_call(
        body, name="grad_share_halves",
        in_specs=[ANY] * nr, out_specs=[ANY] * nr,
        out_shape=[jax.ShapeDtypeStruct(h.shape, h.dtype) for h in halves],
        input_output_aliases={r: r for r in range(nr)},
        scratch_shapes=[pltpu.SemaphoreType.DMA((2 * nr,))] * 2,
        compiler_params=pltpu.CompilerParams(has_side_effects=True),
    )(*halves)


def _sum_small(parts, name):
    n, rows, cols = parts.shape

    def body(p_ref, o_ref):
        acc = p_ref[0].astype(f32)
        for d in range(1, n):
            acc = acc + p_ref[d].astype(f32)
        o_ref[...] = acc

    return pl.pallas_call(
        body, name=name, grid=(rows // 16,),
        in_specs=[pl.BlockSpec((n, 16, cols), lambda i: (0, i, 0))], out_specs=pl.BlockSpec((16, cols), lambda i: (i, 0)),
        out_shape=jax.ShapeDtypeStruct((rows, cols), f32),
        compiler_params=_params(("parallel",)),
    )(parts)


def _adamw(w, g, m, v, name):
    n_l, rows, cols = w.shape
    budget = 42 * 1024 * 1024
    tr = next(rows // d for d in range(1, rows + 1)
              if rows % d == 0 and (rows // d) % 8 == 0 and (rows // d) * cols * 4 * 14 <= budget)

    def body(w_ref, g_ref, m_ref, v_ref, d_ref, nm_ref, nv_ref):
        gg = g_ref[...]
        nm = ADAM_B1 * m_ref[...] + (1.0 - ADAM_B1) * gg
        nv = ADAM_B2 * v_ref[...] + (1.0 - ADAM_B2) * (gg * gg)
        m_hat = nm / (1.0 - ADAM_B1 ** ADAM_STEP)
        v_hat = nv / (1.0 - ADAM_B2 ** ADAM_STEP)
        d_ref[...] = -ADAM_LR * (m_hat / (jnp.sqrt(v_hat) + ADAM_EPS) + ADAM_WD * w_ref[...])
        nm_ref[...] = nm
        nv_ref[...] = nv

    blk = pl.BlockSpec((None, tr, cols), lambda l, i: (l, i, 0))
    return pl.pallas_call(
        body, name=name, grid=(n_l, rows // tr),
        in_specs=[blk] * 4, out_specs=[blk] * 3, out_shape=[jax.ShapeDtypeStruct((n_l, rows, cols), f32)] * 3,
        compiler_params=_params(("parallel", "parallel")),
    )(w, g, m, v)


SMALL = ("mix_norm", "q_norm", "k_norm", "sinks", "sgu_norm", "w_s", "b_s", "ffn_norm", "conv_b", "conv_w")


def _pack_small(arrs):
    flat = jnp.concatenate([a.reshape(-1) for a in arrs])
    pad = (-flat.shape[0]) % (16 * 1024)
    return jnp.pad(flat, (0, pad)).reshape(-1, 1024)


def _unpack_small(pack, shapes):
    flat = pack.reshape(-1)
    out, off = [], 0
    for s in shapes:
        n = int(np.prod(s))
        out.append(flat[off:off + n].reshape(s))
        off += n
    return out


def kernel(x, mix_norm, w_in, q_norm, k_norm, sinks, sgu_norm, w_s, b_s, w_oa, w_ob, w_out, ffn_norm, w_up, conv_w, conv_b, w_down, loss_target, m_mix_norm, m_w_in, m_q_norm, m_k_norm, m_sinks, m_sgu_norm, m_w_s, m_b_s, m_w_oa, m_w_ob, m_w_out, m_ffn_norm, m_w_up, m_conv_w, m_conv_b, m_w_down, v_mix_norm, v_w_in, v_q_norm, v_k_norm, v_sinks, v_sgu_norm, v_w_s, v_b_s, v_w_oa, v_w_ob, v_w_out, v_ffn_norm, v_w_up, v_conv_w, v_conv_b, v_w_down):
    weights = dict(mix_norm=mix_norm, w_in=w_in, q_norm=q_norm, k_norm=k_norm, sinks=sinks, sgu_norm=sgu_norm,
                   w_s=w_s, b_s=b_s, w_oa=w_oa, w_ob=w_ob, w_out=w_out, ffn_norm=ffn_norm, w_up=w_up,
                   conv_w=conv_w, conv_b=conv_b, w_down=w_down)
    mom_m = dict(mix_norm=m_mix_norm, w_in=m_w_in, q_norm=m_q_norm, k_norm=m_k_norm, sinks=m_sinks,
                 sgu_norm=m_sgu_norm, w_s=m_w_s, b_s=m_b_s, w_oa=m_w_oa, w_ob=m_w_ob, w_out=m_w_out,
                 ffn_norm=m_ffn_norm, w_up=m_w_up, conv_w=m_conv_w, conv_b=m_conv_b, w_down=m_w_down)
    mom_v = dict(mix_norm=v_mix_norm, w_in=v_w_in, q_norm=v_q_norm, k_norm=v_k_norm, sinks=v_sinks,
                 sgu_norm=v_sgu_norm, w_s=v_w_s, b_s=v_b_s, w_oa=v_w_oa, w_ob=v_w_ob, w_out=v_w_out,
                 ffn_norm=v_ffn_norm, w_up=v_w_up, conv_w=v_conv_w, conv_b=v_conv_b, w_down=v_w_down)
    n_seq, seq, _ = x.shape
    T = n_seq * seq
    core = lax.axis_index("c")
    chip = 2 * lax.axis_index("x") + lax.axis_index("y")
    tm = min(512, seq)
    tm_ff = min(256, seq)
    tm_sgu = min(512, seq)
    tk_dw = min(2048, T)

    me = 2 * chip + core
    names = [r[0] for r in REGIONS]
    hrs = {name: rows // N_DEV for name, rows, _, _ in REGIONS}
    def placed(l, name, after=None):
        _, rows, cols, transposed = next(r for r in REGIONS if r[0] == name)
        shard = (jnp.swapaxes(weights[name], 1, 2) if transposed else weights[name]).reshape(2, 2, hrs[name], cols)
        return _place(shard, l, core, N_DEV, me, bf16, f"place_{name}_{l}", after).reshape(rows, cols)

    group_keys = [[(0, "w_in")], [(0, n) for n in names[1:]],
                  [(1, n) for n in ("w_in", "w_oa", "w_ob", "w_out")], [(1, "w_up"), (1, "w_down")]]
    n_direct = [1, 0, 0, 0]
    plans = [_gather_plan([hrs[n] for _, n in keys], nd) for keys, nd in zip(group_keys, n_direct)]
    first_bufs = [placed(0, "w_in"), conv_w, jnp.zeros((N_CHIP,) + conv_w.shape, f32)]
    started, tok = _split_start([(first_bufs, plans[0])], "gather_start_0")
    first_start_done = jnp.broadcast_to(tok[0:1, 0:1], (512, D_MODEL))
    rest_bufs = [[placed(l, n, first_start_done) for l, n in keys] for keys in group_keys[1:]]
    more, tok = _split_start(list(zip(rest_bufs, plans[1:])), "gather_start_1")
    started += more
    second_start_done = jnp.broadcast_to(tok[0:1, 0:1], (512, D_MODEL))
    gathered = [{}, {}]

    def arrived(g, after):
        send, recv, bufs = started[g]
        hr_list = [hrs[n] for _, n in group_keys[g]]
        bufs = _split_wait(bufs, send, recv, _gather_plan(hr_list, n_direct[g]), after, f"gather_wait_{g}")
        return bufs[:len(hr_list)], bufs[len(hr_list):]

    def start_pass(g, bufs):
        (res,), token = _split_start([(bufs, _pass_plan([hrs[n] for _, n in group_keys[g]]))], f"pass_start_{g}")
        return res, token[0:1, 0:1]

    def finish_pass(g, res, after):
        send, recv, bufs = res
        use(g, _split_wait(bufs, send, recv, _pass_plan([hrs[n] for _, n in group_keys[g]]), after, f"pass_wait_{g}"))

    def use(g, bufs):
        for (l, n), b in zip(group_keys[g], bufs):
            gathered[l][n] = b

    xs = x.reshape(T, D_MODEL)
    bufs, (_, conv_w_land) = arrived(0, second_start_done)
    use(0, _pass_to_sibling(bufs, [hrs["w_in"]], "gather_pass_0"))
    conv_w_all = lax.dynamic_update_slice(conv_w_land, conv_w[None], (chip, 0, 0, 0))
    conv_w_full = jnp.concatenate([conv_w_all[j] for j in range(N_CHIP)], axis=-1)
    saved = []
    cur = xs
    for l in range(2):
        wl = gathered[l]
        b_col = b_s[l].reshape(SGU_GROUPS, BLOCK, 1)
        qkv, su, sv, ga, gb, h = _in_proj(cur, mix_norm[l][None], wl["w_in"], l, tm)
        y_att = _attn_fwd(qkv, q_norm[l][None], k_norm[l][None], sinks[l], n_seq, seq)
        later = 1 if l == 0 else 3
        passing, tok = start_pass(later, arrived(later, y_att)[0])
        y_sgu = _sgu_fwd(su, sv, sgu_norm[l][None] + tok, w_s[l], b_col, tm_sgu)
        finish_pass(later, passing, y_sgu)
        x1, merged, a_o, b_o = _merge_fwd(cur, y_att, y_sgu, ga, gb, wl["w_oa"], wl["w_ob"], wl["w_out"], l, tm)
        h2, z, act = _ffn_up(x1, ffn_norm[l][None], wl["w_up"], conv_w_full[l], conv_b[l][None], l, seq, tm_ff)
        saved.append(dict(x=cur, qkv=qkv, su=su, sv=sv, ga=ga, gb=gb, h=h, y_att=y_att, y_sgu=y_sgu, x1=x1,
                          merged=merged, a=a_o, b=b_o, h2=h2, z=z, act=act, b_col=b_col))
        if l == 0:
            pass_2, tok = start_pass(2, arrived(2, act)[0])
            cur = _ffn_down(x1, act, wl["w_down"], tok, l, tm)
            finish_pass(2, pass_2, cur)
        else:
            dy, dyb, loss_part = _ffn_down_loss(x1, act, wl["w_down"], loss_target.reshape(T, D_MODEL), l, tm)

    core_arr = core.astype(jnp.int32).reshape(1)
    big = [{}, {}]
    small = {name: [None, None] for name in SMALL}

    def start_pairs(l, keys, tag):
        gl = [big[l][n] for n in keys]
        land = [lax.empty((1, N_CHIP, hrs[n], g.shape[2]), bf16) for n, g in zip(keys, gl)]
        (res,), token = _split_start([(gl + land, _pair_plan([hrs[n] for n in keys]))], f"pair_start_{tag}")
        return (l, keys, res, tag), token[0:1, 0:1]

    def pairs_to_chips(state, after):
        l, keys, (send, recv, bufs), tag = state
        bufs = _split_wait(bufs, send, recv, _pair_plan([hrs[n] for n in keys]), after, f"pair_wait_{tag}")
        return sums_to_chips(l, keys, bufs[:len(keys)], bufs[len(keys):], tag)

    def sums_to_chips(l, keys, gl, from_sibling, tag):
        pairs = [_pair_sum(g, o, core, chip, f"pair_sum_{n}_{l}") for g, o, n in zip(gl, from_sibling, keys)]
        bufs = [p[0] for p in pairs] + [p[1] for p in pairs]
        (res,), token = _split_start([(bufs, _chip_plan(len(keys), 1))], f"chip_start_{tag}")
        return (l, keys, res, tag), token[0:1, 0:1]

    def start_reduce(l, keys, tag):
        gl = [big[l][n] for n in keys]
        return sums_to_chips(l, keys, gl, _pair_exchange(gl, f"pair_exchange_{tag}"), tag)

    landed = {}

    def finish_reduce(state, after):
        l, keys, (send, recv, bufs), tag = state
        bufs = _split_wait(bufs, send, recv, _chip_plan(len(keys), 1), after, f"chip_wait_{tag}")
        for n, p in zip(keys, bufs[len(keys):]):
            landed[(l, n)] = p

    rest = [n for n in SMALL if n != "w_s"]
    rest_shapes = [weights[n].shape[1:] if n != "conv_w" else (3, 2 * D_FF) for n in rest]
    zero = jnp.zeros((), jnp.int32)

    def start_small(l):
        extra = loss_part[0, 0:1] if l == 1 else jnp.zeros((1,), f32)
        packs = [(_pack_small([small[n][l] for n in rest] + [extra]), f32, "small"),
                 (small["w_s"][l].reshape(-1, 1024), bf16, "w_s")]
        bufs = [_place(p[None, None], 0, zero, N_DEV, me, dt, f"place_{tag}_{l}") for p, dt, tag in packs]
        (res,), token = _split_start([(bufs, _all_to_all_plan(2))], f"small_start_{l}")
        return res, token[0:1, 0:1]

    def finish_small(l, res, after):
        send, recv, bufs = res
        bufs = _split_wait(bufs, send, recv, _all_to_all_plan(2), after, f"small_wait_{l}")
        out = dict(zip(rest + ["loss"], _unpack_small(_sum_small(bufs[0], f"sum_small_{l}"), rest_shapes + [(1,)])))
        out["w_s"] = _sum_small(bufs[1], f"sum_w_s_{l}").reshape(w_s.shape[1:])
        return out

    pending = []
    after_start = jnp.zeros((1, 1), f32)
    for l in (1, 0):
        s = saved[l]
        wl = gathered[l]
        dz, dconv = _ffn_bwd(dyb, s["z"], conv_w_full[l], conv_b[l][None] + after_start, wl["w_down"], l, seq, tm_ff)
        big[l]["w_down"] = _weight_grad(s["act"], dyb, 1408, tk_dw, f"dw_down_{l}")
        big[l]["w_up"] = _weight_grad(dz, s["h2"], 1408, tk_dw, f"dw_up_{l}")
        ffn_gain, sgu_gain, q_gain = ffn_norm[l][None], sgu_norm[l][None], q_norm[l][None]
        if l == 0:
            pairs_a, tok = start_pairs(0, ["w_down", "w_up"], "0a")
            ffn_gain = ffn_gain + tok
        dx1, dx1b, d_ffn = _norm_bwd([dz], wl["w_up"], l, s["x1"], ffn_gain, dy, tm, f"ffn_norm_bwd_{l}")
        if l == 0:
            state, tok = pairs_to_chips(pairs_a, dx1b)
            pending.append(state)
            sgu_gain = sgu_gain + tok
        small["conv_w"][l] = dconv[0:3]
        small["conv_b"][l] = dconv[3]
        small["ffn_norm"][l] = d_ffn[0]
        da, db, dga, dgb, dya, dys = _merge_bwd(dx1b, s["ga"], s["gb"], s["a"], s["b"],
                                                wl["w_oa"], wl["w_ob"], wl["w_out"], l, tm)
        big[l]["w_out"] = _weight_grad(s["merged"], dx1b, 1024, tk_dw, f"dw_out_{l}")
        big[l]["w_oa"] = _weight_grad(da, s["y_att"], 1024, tk_dw, f"dw_oa_{l}")
        big[l]["w_ob"] = _weight_grad(db, s["y_sgu"], 1024, tk_dw, f"dw_ob_{l}")
        if l == 0:
            pairs_m, tok = start_pairs(0, ["w_out", "w_oa", "w_ob"], "0m")
            sgu_gain = sgu_gain + tok
        dsu, dsv, d_ws, d_bs, d_sgu = _sgu_bwd(dys, s["su"], s["sv"], sgu_gain, w_s[l], s["b_col"], tm_sgu)
        if l == 0:
            state, tok = pairs_to_chips(pairs_m, dsv)
            pending.append(state)
            q_gain = q_gain + tok
        causal = np.tril(np.ones((BLOCK, BLOCK), bool))
        small["w_s"][l] = jnp.where(causal[None], d_ws, 0.0)
        small["b_s"][l] = d_bs[:, :, 0]
        small["sgu_norm"][l] = d_sgu[0]
        dqkv, d_qg, d_kg, d_sink = _attn_bwd(dya, s["qkv"], q_gain, k_norm[l][None], sinks[l], n_seq, seq)
        small["q_norm"][l] = d_qg[0]
        small["k_norm"][l] = d_kg[0]
        small["sinks"][l] = d_sink[:, 0]
        dproj = [dqkv, dsu, dsv, dga, dgb]
        big[l]["w_in"] = _weight_grad_rows(dproj, s["h"], min(1024, T), f"dw_in_{l}")
        if l == 1:
            pairs_1, tok = start_pairs(1, names, "1")
        else:
            state, tok = start_reduce(0, ["w_in"], "0b")
            pending.append(state)
        dy, dyb, d_mix = _norm_bwd(dproj, wl["w_in"], l, s["x"], mix_norm[l][None] + tok, dx1, tm, f"mix_norm_bwd_{l}")
        small["mix_norm"][l] = d_mix[0]
        if l == 1:
            state, tok = pairs_to_chips(pairs_1, dyb)
            pending.append(state)
            small_1, after_start = start_small(1)
            after_start = after_start + tok
    grad_x = dy.reshape(n_seq, seq, D_MODEL)

    small_0, _ = start_small(0)
    for state in pending:
        finish_reduce(state, dyb)
    halves = [_chip_sum([landed[(0, n)], landed[(1, n)]], core_arr, f"chip_sum_{n}") for n in names]
    shared = dict(zip(names, _share_halves(halves)))
    grad, delta, new_m, new_v = {}, {}, {}, {}
    flip = lambda a: jnp.swapaxes(a, 1, 2)
    for name, rows, cols, transposed in REGIONS:
        g = shared[name].reshape(2, rows // N_CHIP, cols)
        if transposed and weights[name].shape[2] % 128:
            d, nm, nv = _adamw(flip(weights[name]), g, flip(mom_m[name]), flip(mom_v[name]), f"adamw_{name}")
            grad[name], delta[name], new_m[name], new_v[name] = flip(g), flip(d), flip(nm), flip(nv)
        else:
            grad[name] = flip(g) if transposed else g
            delta[name], new_m[name], new_v[name] = _adamw(weights[name], grad[name], mom_m[name], mom_v[name],
                                                           f"adamw_{name}")

    per_layer = [finish_small(0, small_0, delta["w_down"]), finish_small(1, small_1, dyb)]
    loss = per_layer[1]["loss"][0]
    grad_small = {n: jnp.stack([per_layer[0][n], per_layer[1][n]]) for n in SMALL}
    cw_cols = conv_w.shape[-1]
    grad_small["conv_w"] = lax.dynamic_slice_in_dim(grad_small["conv_w"], chip * cw_cols, cw_cols, axis=2)

    as_rows = lambda a: a.reshape(2, -1, BLOCK)
    d, nm, nv = _adamw(as_rows(w_s), as_rows(grad_small["w_s"]), as_rows(m_w_s), as_rows(v_w_s), "adamw_w_s")
    grad["w_s"], delta["w_s"], new_m["w_s"], new_v["w_s"] = (
        grad_small["w_s"], d.reshape(w_s.shape), nm.reshape(w_s.shape), nv.reshape(w_s.shape))
    shapes = [weights[n].shape for n in rest]
    d, nm, nv = _adamw(_pack_small([weights[n] for n in rest])[None], _pack_small([grad_small[n] for n in rest])[None],
                       _pack_small([mom_m[n] for n in rest])[None], _pack_small([mom_v[n] for n in rest])[None],
                       "adamw_small")
    for n, dd, mm, vv in zip(rest, _unpack_small(d, shapes), _unpack_small(nm, shapes), _unpack_small(nv, shapes)):
        grad[n], delta[n], new_m[n], new_v[n] = grad_small[n], dd, mm, vv

    order = ["mix_norm", "w_in", "q_norm", "k_norm", "sinks", "sgu_norm", "w_s", "b_s", "w_oa", "w_ob", "w_out",
             "ffn_norm", "w_up", "conv_w", "conv_b", "w_down"]
    return (loss, grad_x, *[grad[n] for n in order], *[delta[n] for n in order],
            *[new_m[n] for n in order], *[new_v[n] for n in order])
```

```python
import functools

import numpy as np
import jax
import jax.numpy as jnp
from jax import lax
from jax.experimental import pallas as pl
from jax.experimental.pallas import tpu as pltpu

bf16 = jnp.bfloat16
f32 = jnp.float32

D_MODEL = 1024
ATT_WIDTH = 512
KV_WIDTH = 128
SGU_WIDTH = 512
HEAD_DIM = 64
N_KV_HEADS = 2
Q_GROUP = 4
BLOCK = 128
SGU_GROUPS = 8
IN_WIDTH = 3840
D_FF = 2816
NORM_EPS = 1e-6
NEG_INF = -1e30
N_DEV = 8
N_CHIP = 4

ADAM_LR = 0.001
ADAM_B1 = 0.9
ADAM_B2 = 0.999
ADAM_EPS = 1e-08
ADAM_WD = 0.01
ADAM_STEP = 10

V7X_VMEM_LIMIT = 56 * 1024 * 1024
FF_CHUNK = 2816

REGIONS = (
    ("w_in", IN_WIDTH, D_MODEL, True),
    ("w_oa", D_MODEL, ATT_WIDTH, True),
    ("w_ob", D_MODEL, SGU_WIDTH, True),
    ("w_out", D_MODEL, D_MODEL, False),
    ("w_up", 2 * D_FF, D_MODEL, True),
    ("w_down", D_FF, D_MODEL, False),
)
MESH = pl.DeviceIdType.MESH
ANY = pl.BlockSpec(memory_space=pl.ANY)


def _params(sem=None, **kw):
    return pltpu.CompilerParams(dimension_semantics=sem, vmem_limit_bytes=V7X_VMEM_LIMIT, **kw)


def _wspec(rows, cols, layer=None):
    del layer
    return pl.BlockSpec((rows, cols), lambda *_: (0, 0), pipeline_mode=pl.Buffered(1))


def _full(shape):
    nd = len(shape)
    return pl.BlockSpec(shape, lambda *_: (0,) * nd)


def _dot_nn(a, b):
    return jnp.dot(a, b, preferred_element_type=f32)


def _dot_nt(a, b):
    return lax.dot_general(a, b, (((1,), (1,)), ((), ())), preferred_element_type=f32)


def _dot_tn(a, b):
    return lax.dot_general(a, b, (((0,), (0,)), ((), ())), preferred_element_type=f32)


_GELU_C = float(np.sqrt(2.0 / np.pi))


def _gelu(x):
    return 0.5 * x * (1.0 + jnp.tanh(_GELU_C * (x + 0.044715 * x * x * x)))


def _gelu_grad(x):
    t = jnp.tanh(_GELU_C * (x + 0.044715 * x * x * x))
    du = _GELU_C * (1.0 + 3.0 * 0.044715 * x * x)
    return 0.5 * (1.0 + t) + 0.5 * x * (1.0 - t * t) * du


def _rms(x):
    return lax.rsqrt(jnp.mean(x * x, axis=-1, keepdims=True) + NORM_EPS)


def _mesh_pos():
    return lax.axis_index("x"), lax.axis_index("y"), lax.axis_index("c")


def _in_proj(x, gain, w_in_t, layer, tm):
    T = x.shape[0]

    def body(x_ref, g_ref, w_ref, qkv_ref, su_ref, sv_ref, ga_ref, gb_ref, h_ref):
        xf = x_ref[...]
        h = (xf * _rms(xf) * g_ref[...]).astype(bf16)
        h_ref[...] = h
        qkv_ref[...] = _dot_nt(h, w_ref[0:768, :])
        su_ref[...] = _dot_nt(h, w_ref[768:1280, :]).astype(bf16)
        sv_ref[...] = _dot_nt(h, w_ref[1280:1792, :]).astype(bf16)
        ga_ref[...] = _dot_nt(h, w_ref[1792:2816, :]).astype(bf16)
        gb_ref[...] = _dot_nt(h, w_ref[2816:3840, :]).astype(bf16)

    row = lambda w: pl.BlockSpec((tm, w), lambda i: (i, 0))
    return pl.pallas_call(
        body, name=f"in_proj_{layer}", grid=(T // tm,),
        in_specs=[row(D_MODEL), _full((1, D_MODEL)), _wspec(IN_WIDTH, D_MODEL, layer)],
        out_specs=[row(768), row(512), row(512), row(1024), row(1024), row(D_MODEL)],
        out_shape=[jax.ShapeDtypeStruct((T, 768), f32), jax.ShapeDtypeStruct((T, 512), bf16),
                   jax.ShapeDtypeStruct((T, 512), bf16), jax.ShapeDtypeStruct((T, 1024), bf16),
                   jax.ShapeDtypeStruct((T, 1024), bf16), jax.ShapeDtypeStruct((T, D_MODEL), bf16)],
        compiler_params=_params(("parallel",)),
    )(x, gain, w_in_t)


def _attn_head_group(cur, prev, qg, kg, sink_ref, n, hk):
    lo = hk * HEAD_DIM
    k_raw = jnp.concatenate([prev[:, lo:lo + HEAD_DIM], cur[:, 512 + lo:512 + lo + HEAD_DIM]], axis=0)
    v_band = jnp.concatenate([prev[:, 128 + lo:128 + lo + HEAD_DIM], cur[:, 640 + lo:640 + lo + HEAD_DIM]], axis=0)
    rk = _rms(k_raw)
    k_hat = k_raw * rk
    kn = (k_hat * kg).astype(bf16)
    q_raw = jnp.concatenate(
        [cur[:, (hk * Q_GROUP + g) * HEAD_DIM:(hk * Q_GROUP + g + 1) * HEAD_DIM] for g in range(Q_GROUP)], axis=0)
    rq = _rms(q_raw)
    q_hat = q_raw * rq
    qn = (q_hat * qg * (HEAD_DIM ** -0.5)).astype(bf16)
    s = _dot_nt(qn, kn)
    rows = lax.broadcasted_iota(jnp.int32, (Q_GROUP * BLOCK, 1), 0)
    g_of_row = rows // BLOCK
    qi = rows - g_of_row * BLOCK
    kj = lax.broadcasted_iota(jnp.int32, (1, 2 * BLOCK), 1)
    dist = qi + BLOCK - kj
    valid = (dist >= 0) & (dist < BLOCK) & ((kj >= BLOCK) | (n > 0))
    slope = jnp.zeros((Q_GROUP * BLOCK, 1), f32)
    sink = jnp.zeros((Q_GROUP * BLOCK, 1), f32)
    for g in range(Q_GROUP):
        head = hk * Q_GROUP + g
        slope = jnp.where(g_of_row == g, float(np.exp2(-8.0 * (head + 1.0) / 8.0)), slope)
        sink = jnp.where(g_of_row == g, sink_ref[head], sink)
    s = jnp.where(valid, s - slope * dist.astype(f32), NEG_INF)
    m = jnp.maximum(jnp.max(s, axis=-1, keepdims=True), sink)
    e = jnp.exp(s - m)
    e_sink = jnp.exp(sink - m)
    inv = 1.0 / (jnp.sum(e, axis=-1, keepdims=True) + e_sink)
    return dict(k_raw=k_raw, rk=rk, k_hat=k_hat, kn=kn, v=v_band.astype(bf16), q_hat=q_hat, rq=rq, qn=qn,
                p=e * inv, p_sink=e_sink * inv)


def _attn_fwd(qkv, qg, kg, sinks, n_seq, seq):
    T = n_seq * seq
    nb = seq // BLOCK

    per = 2 if nb % 2 == 0 else 1

    def body(cur_ref, prev_ref, qg_ref, kg_ref, sink_ref, y_ref):
        for sub in range(per):
            n = pl.program_id(1) * per + sub
            cur = cur_ref[sub * BLOCK:(sub + 1) * BLOCK, :]
            prev = prev_ref[...] if sub == 0 else cur_ref[(sub - 1) * BLOCK:sub * BLOCK, 512:768]
            pieces = [None] * (N_KV_HEADS * Q_GROUP)
            for hk in range(N_KV_HEADS):
                a = _attn_head_group(cur, prev, qg_ref[...], kg_ref[...], sink_ref, n, hk)
                o = _dot_nn(a["p"].astype(bf16), a["v"])
                for g in range(Q_GROUP):
                    pieces[hk * Q_GROUP + g] = o[g * BLOCK:(g + 1) * BLOCK]
            y_ref[sub * BLOCK:(sub + 1) * BLOCK, :] = jnp.concatenate(pieces, axis=1).astype(bf16)

    return pl.pallas_call(
        body, name="attn_fwd", grid=(n_seq, nb // per),
        in_specs=[pl.BlockSpec((per * BLOCK, 768), lambda b, n: (b * (nb // per) + n, 0)),
                  pl.BlockSpec((BLOCK, 256), lambda b, n: (b * nb + jnp.maximum(n * per - 1, 0), 2)),
                  _full((1, HEAD_DIM)), _full((1, HEAD_DIM)),
                  pl.BlockSpec(memory_space=pltpu.SMEM)],
        out_specs=pl.BlockSpec((per * BLOCK, ATT_WIDTH), lambda b, n: (b * (nb // per) + n, 0)),
        out_shape=jax.ShapeDtypeStruct((T, ATT_WIDTH), bf16),
        compiler_params=_params(("parallel", "parallel")),
    )(qkv, qkv, qg, kg, sinks)


def _sgu_chunk(su, sv, gain, w_ref, b_ref):
    u = _gelu(su)
    vg = _gelu(sv)
    rv = _rms(vg)
    v_hat = vg * rv
    vn = (v_hat * gain).astype(bf16)
    causal = (lax.broadcasted_iota(jnp.int32, (BLOCK, BLOCK), 0) >= lax.broadcasted_iota(jnp.int32, (BLOCK, BLOCK), 1))
    w_tril = [jnp.where(causal, w_ref[g], 0.0).astype(bf16) for g in range(SGU_GROUPS)]
    gd = SGU_WIDTH // SGU_GROUPS
    mixed = jnp.concatenate(
        [_dot_nn(w_tril[g], vn[:, g * gd:(g + 1) * gd]) + b_ref[g] for g in range(SGU_GROUPS)], axis=1)
    return u, rv, v_hat, vn, w_tril, mixed


def _sgu_fwd(su, sv, gain, w_s, b_s, tm):
    T = su.shape[0]

    def body(su_ref, sv_ref, g_ref, w_ref, b_ref, y_ref):
        for ch in range(tm // BLOCK):
            rows = slice(ch * BLOCK, (ch + 1) * BLOCK)
            u, _, _, _, _, mixed = _sgu_chunk(su_ref[rows, :].astype(f32), sv_ref[rows, :].astype(f32),
                                              g_ref[...], w_ref, b_ref)
            y_ref[rows, :] = (u * mixed).astype(bf16)

    row = pl.BlockSpec((tm, SGU_WIDTH), lambda i: (i, 0))
    return pl.pallas_call(
        body, name="sgu_fwd", grid=(T // tm,),
        in_specs=[row, row, _full((1, SGU_WIDTH)), _full((SGU_GROUPS, BLOCK, BLOCK)), _full((SGU_GROUPS, BLOCK, 1))],
        out_specs=row, out_shape=jax.ShapeDtypeStruct((T, SGU_WIDTH), bf16),
        compiler_params=_params(("parallel",)),
    )(su, sv, gain, w_s, b_s)


def _merge_fwd(x, y_att, y_sgu, ga, gb, w_oa_t, w_ob_t, w_out, layer, tm):
    T = x.shape[0]

    def body(x_ref, ya_ref, ys_ref, ga_ref, gb_ref, woa_ref, wob_ref, wout_ref, x1_ref, m_ref, a_ref, b_ref):
        a = _dot_nt(ya_ref[...], woa_ref[...])
        b = _dot_nt(ys_ref[...], wob_ref[...])
        a_ref[...] = a.astype(bf16)
        b_ref[...] = b.astype(bf16)
        merged = (jax.nn.sigmoid(ga_ref[...].astype(f32)) * a + jax.nn.sigmoid(gb_ref[...].astype(f32)) * b).astype(bf16)
        m_ref[...] = merged
        x1_ref[...] = x_ref[...] + _dot_nn(merged, wout_ref[...])

    row = lambda w: pl.BlockSpec((tm, w), lambda i: (i, 0))
    return pl.pallas_call(
        body, name=f"merge_fwd_{layer}", grid=(T // tm,),
        in_specs=[row(D_MODEL), row(512), row(512), row(1024), row(1024),
                  _wspec(D_MODEL, ATT_WIDTH, layer), _wspec(D_MODEL, SGU_WIDTH, layer), _wspec(D_MODEL, D_MODEL, layer)],
        out_specs=[row(D_MODEL)] * 4,
        out_shape=[jax.ShapeDtypeStruct((T, D_MODEL), f32)] + [jax.ShapeDtypeStruct((T, D_MODEL), bf16)] * 3,
        compiler_params=_params(("parallel",)),
    )(x, y_att, y_sgu, ga, gb, w_oa_t, w_ob_t, w_out)


def _tile_permutation(tm):
    r = np.arange(tm)
    p = np.zeros((tm, tm), np.float32)
    p[r, (r % 8) * (tm // 8) + r // 8] = 1.0
    return jnp.asarray(p, bf16), jnp.asarray(p.T, bf16)


def _stage_taps_before(buf, zz, prev, tm):
    first = lax.broadcasted_iota(jnp.int32, (8, 1), 0) == 0
    buf[16:16 + tm, :] = zz
    buf[0:8, :] = jnp.where(first, prev[7:8], pltpu.roll(buf[tm:tm + 8, :], 1, 0))
    buf[8:16, :] = jnp.where(first, prev[15:16], pltpu.roll(buf[tm + 8:tm + 16, :], 1, 0))


def _stage_taps_after(buf, nxt, tm):
    last = lax.broadcasted_iota(jnp.int32, (8, 1), 0) == 7
    buf[tm:tm + 8, :] = jnp.where(last, nxt[0:1], pltpu.roll(buf[0:8, :], 7, 0))
    buf[tm + 8:tm + 16, :] = jnp.where(last, nxt[8:9], pltpu.roll(buf[8:16, :], 7, 0))


def _conv_rows(buf, r, n, coef):
    z2 = buf[pl.ds(r, n), :]
    z1 = buf[pl.ds(pl.multiple_of(r + 8, 8), n), :]
    z0 = buf[pl.ds(pl.multiple_of(r + 16, 8), n), :]
    return coef[0] + coef[1] * z2 + coef[2] * z1 + coef[3] * z0


def _ffn_up(x1, gain, w_up_t, conv_w, conv_b, layer, seq, tm):
    T = x1.shape[0]
    tps = seq // tm
    perm, perm_t = _tile_permutation(tm)

    rg = 16

    def body(x_ref, g_ref, w_ref, cw_ref, cb_ref, p_ref, pt_ref, h2_ref, z_ref, act_ref, carry_ref,
             zg_buf, zv_buf, actp_buf):
        i = pl.program_id(0)

        @pl.when(i % tps == 0)
        def _():
            carry_ref[...] = jnp.zeros_like(carry_ref)

        xf = x_ref[...]
        h2 = (xf * _rms(xf) * g_ref[...]).astype(bf16)
        h2_ref[...] = h2
        h2p = _dot_nn(p_ref[...], h2).astype(bf16)
        for cc in range(D_FF // FF_CHUNK):
            cols_g = slice(cc * FF_CHUNK, (cc + 1) * FF_CHUNK)
            cols_v = slice(D_FF + cc * FF_CHUNK, D_FF + (cc + 1) * FF_CHUNK)
            for buf, cols in ((zg_buf, cols_g), (zv_buf, cols_v)):
                zb = _dot_nt(h2p, w_ref[cols, :]).astype(bf16)
                z_ref[:, cols] = zb
                _stage_taps_before(buf, zb.astype(f32), carry_ref[:, cols], tm)
                carry_ref[:, cols] = buf[tm:tm + 16, :]
            coef = [jnp.broadcast_to(v, (rg, FF_CHUNK)) for cols in (cols_g, cols_v)
                    for v in (cb_ref[:, cols], cw_ref[0:1, cols], cw_ref[1:2, cols], cw_ref[2:3, cols])]

            def rows_step(j, carry, coef=coef):
                r = pl.multiple_of(j * rg, rg)
                zcg, zcv = (_conv_rows(buf, r, rg, coef[4 * k:4 * k + 4]) for k, buf in enumerate((zg_buf, zv_buf)))
                actp_buf[pl.ds(r, rg), :] = (zcg * jax.nn.sigmoid(zcg) * zcv).astype(bf16)
                return carry

            lax.fori_loop(0, tm // rg, rows_step, 0, unroll=True)
            act_ref[:, cols_g] = _dot_nn(pt_ref[...], actp_buf[...]).astype(bf16)

    row = lambda w: pl.BlockSpec((tm, w), lambda i: (i, 0))
    return pl.pallas_call(
        body, name=f"ffn_up_{layer}", grid=(T // tm,),
        in_specs=[row(D_MODEL), _full((1, D_MODEL)), _wspec(2 * D_FF, D_MODEL, layer),
                  _full((3, 2 * D_FF)), _full((1, 2 * D_FF)), _full((tm, tm)), _full((tm, tm))],
        out_specs=[row(D_MODEL), row(2 * D_FF), row(D_FF)],
        out_shape=[jax.ShapeDtypeStruct((T, D_MODEL), bf16), jax.ShapeDtypeStruct((T, 2 * D_FF), bf16),
                   jax.ShapeDtypeStruct((T, D_FF), bf16)],
        scratch_shapes=[pltpu.VMEM((16, 2 * D_FF), f32), pltpu.VMEM((tm + 16, FF_CHUNK), f32),
                        pltpu.VMEM((tm + 16, FF_CHUNK), f32), pltpu.VMEM((tm, FF_CHUNK), bf16)],
        compiler_params=_params(("arbitrary",)),
    )(x1, gain, w_up_t, conv_w, conv_b, perm, perm_t)


def _ffn_down(x1, act, w_down, after, layer, tm):
    T = x1.shape[0]

    def body(x_ref, a_ref, w_ref, after_ref, o_ref):
        o_ref[...] = x_ref[...] + _dot_nn(a_ref[...], w_ref[...])

    row = lambda w: pl.BlockSpec((tm, w), lambda i: (i, 0))
    return pl.pallas_call(
        body, name=f"ffn_down_{layer}", grid=(T // tm,),
        in_specs=[row(D_MODEL), row(D_FF), _wspec(D_FF, D_MODEL, layer), _full((1, 1))],
        out_specs=row(D_MODEL), out_shape=jax.ShapeDtypeStruct((T, D_MODEL), f32),
        compiler_params=_params(("parallel",)),
    )(x1, act, w_down, after)


def _ffn_down_loss(x1, act, w_down, target, layer, tm):
    T = x1.shape[0]

    def body(x_ref, a_ref, w_ref, t_ref, dy_ref, dyb_ref, loss_ref):
        @pl.when(pl.program_id(0) == 0)
        def _():
            loss_ref[...] = jnp.zeros_like(loss_ref)

        diff = x_ref[...] + _dot_nn(a_ref[...], w_ref[...]) - t_ref[...]
        loss_ref[...] += 0.5 * jnp.sum(jnp.mean(diff * diff, axis=-1, keepdims=True), axis=0, keepdims=True)
        dy = diff * (1.0 / D_MODEL)
        dy_ref[...] = dy
        dyb_ref[...] = dy.astype(bf16)

    row = lambda w: pl.BlockSpec((tm, w), lambda i: (i, 0))
    return pl.pallas_call(
        body, name=f"ffn_down_loss_{layer}", grid=(T // tm,),
        in_specs=[row(D_MODEL), row(D_FF), _wspec(D_FF, D_MODEL, layer), row(D_MODEL)],
        out_specs=[row(D_MODEL), row(D_MODEL), _full((8, 128))],
        out_shape=[jax.ShapeDtypeStruct((T, D_MODEL), f32), jax.ShapeDtypeStruct((T, D_MODEL), bf16),
                   jax.ShapeDtypeStruct((8, 128), f32)],
        compiler_params=_params(("arbitrary",)),
    )(x1, act, w_down, target)


def _ffn_bwd(dx2b, z, conv_w, conv_b, w_down, layer, seq, tm):
    T = z.shape[0]
    nt = T // tm
    tps = seq // tm

    perm, perm_t = _tile_permutation(tm)

    def body(dx_ref, z_ref, zh_ref, cw_ref, cb_ref, wd_ref, p_ref, pt_ref, dz_ref, dconv_ref, carry_ref,
             zg_buf, zv_buf, gg_buf, gv_buf, dact_buf, dzp_buf):
        i = pl.program_id(0)
        pos = (nt - 1 - i) % tps

        @pl.when(i == 0)
        def _():
            dconv_ref[...] = jnp.zeros_like(dconv_ref)

        @pl.when(pos == tps - 1)
        def _():
            carry_ref[...] = jnp.zeros_like(carry_ref)

        dxp = _dot_nn(p_ref[...], dx_ref[...]).astype(bf16)
        halo_on = (pos > 0).astype(f32)
        for cc in range(D_FF // FF_CHUNK):
            cols_g = slice(cc * FF_CHUNK, (cc + 1) * FF_CHUNK)
            cols_v = slice(D_FF + cc * FF_CHUNK, D_FF + (cc + 1) * FF_CHUNK)
            for buf, cols in ((zg_buf, cols_g), (zv_buf, cols_v)):
                _stage_taps_before(buf, z_ref[:, cols].astype(f32), zh_ref[:, cols].astype(f32) * halo_on, tm)
            dact_buf[...] = _dot_nt(dxp, wd_ref[cols_g, :])
            coef = [jnp.broadcast_to(v, (8, FF_CHUNK)) for cols in (cols_g, cols_v)
                    for v in (cb_ref[:, cols], cw_ref[0:1, cols], cw_ref[1:2, cols], cw_ref[2:3, cols])]

            def first_pass(j, sums, coef=coef):
                r = pl.multiple_of(j * 8, 8)
                rows = pl.ds(r, 8)
                zcg = _conv_rows(zg_buf, r, 8, coef[0:4])
                zcv = _conv_rows(zv_buf, r, 8, coef[4:8])
                sg = jax.nn.sigmoid(zcg)
                silu = zcg * sg
                d_act = dact_buf[rows, :]
                dg = d_act * zcv * sg * (1.0 + zcg * (1.0 - sg))
                dv = d_act * silu
                gg_buf[rows, :] = dg
                gv_buf[rows, :] = dv
                out = []
                for k, (g, buf) in enumerate(((dg, zg_buf), (dv, zv_buf))):
                    out += [sums[4 * k] + g * buf[rows, :],
                            sums[4 * k + 1] + g * buf[pl.ds(pl.multiple_of(r + 8, 8), 8), :],
                            sums[4 * k + 2] + g * buf[pl.ds(pl.multiple_of(r + 16, 8), 8), :],
                            sums[4 * k + 3] + g]
                return tuple(out)

            sums = lax.fori_loop(0, tm // 8, first_pass, tuple(jnp.zeros((8, FF_CHUNK), f32) for _ in range(8)),
                                 unroll=True)
            for k, cols in enumerate((cols_g, cols_v)):
                for tap in range(4):
                    dconv_ref[tap:tap + 1, cols] += jnp.sum(sums[4 * k + tap], axis=0, keepdims=True)
            for buf, cols in ((gg_buf, cols_g), (gv_buf, cols_v)):
                _stage_taps_after(buf, carry_ref[:, cols], tm)
                carry_ref[:, cols] = buf[0:16, :]
                w0, w1, w2 = (jnp.broadcast_to(cw_ref[k:k + 1, cols], (16, FF_CHUNK)) for k in range(3))

                def second_pass(j, carry, buf=buf, w0=w0, w1=w1, w2=w2):
                    r = pl.multiple_of(j * 16, 16)
                    dzp_buf[pl.ds(r, 16), :] = (w2 * buf[pl.ds(r, 16), :] + w1 * buf[pl.ds(pl.multiple_of(r + 8, 8), 16), :]
                                                + w0 * buf[pl.ds(pl.multiple_of(r + 16, 16), 16), :]).astype(bf16)
                    return carry

                lax.fori_loop(0, tm // 16, second_pass, 0, unroll=True)
                dz_ref[:, cols] = _dot_nn(pt_ref[...], dzp_buf[...]).astype(bf16)

    rev = lambda w: pl.BlockSpec((tm, w), lambda i: (nt - 1 - i, 0))
    return pl.pallas_call(
        body, name=f"ffn_bwd_{layer}", grid=(nt,),
        in_specs=[rev(D_MODEL), rev(2 * D_FF),
                  pl.BlockSpec((16, 2 * D_FF), lambda i: (jnp.maximum((nt - 1 - i) * (tm // 16) - 1, 0), 0)),
                  _full((3, 2 * D_FF)), _full((1, 2 * D_FF)), _wspec(D_FF, D_MODEL, layer),
                  _full((tm, tm)), _full((tm, tm))],
        out_specs=[rev(2 * D_FF), _full((8, 2 * D_FF))],
        out_shape=[jax.ShapeDtypeStruct((T, 2 * D_FF), bf16), jax.ShapeDtypeStruct((8, 2 * D_FF), f32)],
        scratch_shapes=[pltpu.VMEM((16, 2 * D_FF), f32)] + [pltpu.VMEM((tm + 16, FF_CHUNK), f32)] * 4
        + [pltpu.VMEM((tm, FF_CHUNK), f32), pltpu.VMEM((tm, FF_CHUNK), bf16)],
        compiler_params=_params(("arbitrary",)),
    )(dx2b, z, z, conv_w, conv_b, w_down, perm, perm_t)


def _norm_bwd(dys, w, layer, x, gain, dres, tm, name):
    T = dys[0].shape[0]
    widths = [d.shape[1] for d in dys]
    K = sum(widths)
    n = len(dys)

    def body(*refs):
        dy_refs = refs[:n]
        w_ref, x_ref, g_ref, dres_ref, dx_ref, dxb_ref, dg_ref = refs[n:]

        @pl.when(pl.program_id(0) == 0)
        def _():
            dg_ref[...] = jnp.zeros_like(dg_ref)

        dh, lo = None, 0
        for dy_ref, wd in zip(dy_refs, widths):
            part = _dot_nn(dy_ref[...], w_ref[lo:lo + wd, :])
            dh = part if dh is None else dh + part
            lo += wd
        xf = x_ref[...]
        r = _rms(xf)
        x_hat = xf * r
        dg_ref[...] += jnp.sum(dh * x_hat, axis=0, keepdims=True)
        dxh = dh * g_ref[...]
        dx = dres_ref[...] + r * (dxh - x_hat * jnp.mean(dxh * x_hat, axis=-1, keepdims=True))
        dx_ref[...] = dx
        dxb_ref[...] = dx.astype(bf16)

    row = lambda w_: pl.BlockSpec((tm, w_), lambda i: (i, 0))
    return pl.pallas_call(
        body, name=name, grid=(T // tm,),
        in_specs=[row(wd) for wd in widths] + [_wspec(K, D_MODEL, layer), row(D_MODEL), _full((1, D_MODEL)), row(D_MODEL)],
        out_specs=[row(D_MODEL), row(D_MODEL), _full((1, D_MODEL))],
        out_shape=[jax.ShapeDtypeStruct((T, D_MODEL), f32), jax.ShapeDtypeStruct((T, D_MODEL), bf16),
                   jax.ShapeDtypeStruct((1, D_MODEL), f32)],
        compiler_params=_params(("arbitrary",)),
    )(*dys, w, x, gain, dres)


def _merge_bwd(dx1b, ga, gb, a, b, w_oa_t, w_ob_t, w_out, layer, tm):
    T = dx1b.shape[0]

    def body(dx_ref, ga_ref, gb_ref, a_ref, b_ref, woa_ref, wob_ref, wout_ref,
             da_ref, db_ref, dga_ref, dgb_ref, dya_ref, dys_ref):
        dm = _dot_nt(dx_ref[...], wout_ref[...])
        sa = jax.nn.sigmoid(ga_ref[...].astype(f32))
        sb = jax.nn.sigmoid(gb_ref[...].astype(f32))
        da = (dm * sa).astype(bf16)
        db = (dm * sb).astype(bf16)
        da_ref[...] = da
        db_ref[...] = db
        dga_ref[...] = (dm * a_ref[...].astype(f32) * sa * (1.0 - sa)).astype(bf16)
        dgb_ref[...] = (dm * b_ref[...].astype(f32) * sb * (1.0 - sb)).astype(bf16)
        dya_ref[...] = _dot_nn(da, woa_ref[...]).astype(bf16)
        dys_ref[...] = _dot_nn(db, wob_ref[...]).astype(bf16)

    row = lambda w: pl.BlockSpec((tm, w), lambda i: (i, 0))
    return pl.pallas_call(
        body, name=f"merge_bwd_{layer}", grid=(T // tm,),
        in_specs=[row(D_MODEL)] * 5 + [_wspec(D_MODEL, ATT_WIDTH, layer), _wspec(D_MODEL, SGU_WIDTH, layer),
                                       _wspec(D_MODEL, D_MODEL, layer)],
        out_specs=[row(D_MODEL)] * 4 + [row(512)] * 2,
        out_shape=[jax.ShapeDtypeStruct((T, D_MODEL), bf16)] * 4 + [jax.ShapeDtypeStruct((T, 512), bf16)] * 2,
        compiler_params=_params(("parallel",)),
    )(dx1b, ga, gb, a, b, w_oa_t, w_ob_t, w_out)


def _sgu_bwd(dy, su, sv, gain, w_s, b_s, tm):
    T = su.shape[0]
    gd = SGU_WIDTH // SGU_GROUPS

    def body(dy_ref, su_ref, sv_ref, g_ref, w_ref, b_ref, dsu_ref, dsv_ref, dw_ref, db_ref, dg_ref):
        @pl.when(pl.program_id(0) == 0)
        def _():
            dw_ref[...] = jnp.zeros_like(dw_ref)
            db_ref[...] = jnp.zeros_like(db_ref)
            dg_ref[...] = jnp.zeros_like(dg_ref)

        gain_v = g_ref[...]
        for ch in range(tm // BLOCK):
            rows = slice(ch * BLOCK, (ch + 1) * BLOCK)
            su_c = su_ref[rows, :].astype(f32)
            sv_c = sv_ref[rows, :].astype(f32)
            u, rv, v_hat, vn, w_tril, mixed = _sgu_chunk(su_c, sv_c, gain_v, w_ref, b_ref)
            dyc = dy_ref[rows, :].astype(f32)
            dsu_ref[rows, :] = (dyc * mixed * _gelu_grad(su_c)).astype(bf16)
            dmix = dyc * u
            dmix_b = dmix.astype(bf16)
            dvn = []
            for g in range(SGU_GROUPS):
                gs = slice(g * gd, (g + 1) * gd)
                db_ref[g] += jnp.sum(dmix[:, gs], axis=1, keepdims=True)
                dw_ref[g] += _dot_nt(dmix_b[:, gs], vn[:, gs])
                dvn.append(_dot_tn(w_tril[g], dmix_b[:, gs]))
            dvn = jnp.concatenate(dvn, axis=1)
            dg_ref[...] += jnp.sum(dvn * v_hat, axis=0, keepdims=True)
            dxh = dvn * gain_v
            dvg = rv * (dxh - v_hat * jnp.mean(dxh * v_hat, axis=-1, keepdims=True))
            dsv_ref[rows, :] = (dvg * _gelu_grad(sv_c)).astype(bf16)

    row = pl.BlockSpec((tm, SGU_WIDTH), lambda i: (i, 0))
    return pl.pallas_call(
        body, name="sgu_bwd", grid=(T // tm,),
        in_specs=[row, row, row, _full((1, SGU_WIDTH)), _full((SGU_GROUPS, BLOCK, BLOCK)),
                  _full((SGU_GROUPS, BLOCK, 1))],
        out_specs=[row, row, _full((SGU_GROUPS, BLOCK, BLOCK)), _full((SGU_GROUPS, BLOCK, 1)), _full((1, SGU_WIDTH))],
        out_shape=[jax.ShapeDtypeStruct((T, SGU_WIDTH), bf16)] * 2 + [
            jax.ShapeDtypeStruct((SGU_GROUPS, BLOCK, BLOCK), f32), jax.ShapeDtypeStruct((SGU_GROUPS, BLOCK, 1), f32),
            jax.ShapeDtypeStruct((1, SGU_WIDTH), f32)],
        compiler_params=_params(("arbitrary",)),
    )(dy, su, sv, gain, w_s, b_s)


def _attn_bwd(dy, qkv, qg, kg, sinks, n_seq, seq):
    T = n_seq * seq
    nb = seq // BLOCK
    scale = HEAD_DIM ** -0.5
    per = 1
    ng = nb // per

    def body(dy_ref, cur_ref, prev_ref, qg_ref, kg_ref, sink_ref, dqkv_ref, dqg_ref, dkg_ref, dsink_ref,
             carry_k, carry_v):
        b = pl.program_id(0)
        j = pl.program_id(1)

        @pl.when((b == 0) & (j == 0))
        def _():
            dqg_ref[...] = jnp.zeros_like(dqg_ref)
            dkg_ref[...] = jnp.zeros_like(dkg_ref)
            dsink_ref[...] = jnp.zeros_like(dsink_ref)

        @pl.when(j == 0)
        def _():
            carry_k[...] = jnp.zeros_like(carry_k)
            carry_v[...] = jnp.zeros_like(carry_v)

        for sub in reversed(range(per)):
            rows = slice(sub * BLOCK, (sub + 1) * BLOCK)
            prev = prev_ref[...] if sub == 0 else cur_ref[(sub - 1) * BLOCK:sub * BLOCK, 512:768]
            one_block(dy_ref[rows, :].astype(f32), cur_ref[rows, :], prev, (ng - 1 - j) * per + sub,
                      qg_ref[...], kg_ref[...], sink_ref, dqkv_ref.at[rows, :], dqg_ref, dkg_ref, dsink_ref,
                      carry_k, carry_v)

    def one_block(dyf, cur, prev, n, qg_v, kg_v, sink_ref, dqkv_ref, dqg_ref, dkg_ref, dsink_ref, carry_k, carry_v):
        dq_pieces = [None] * (N_KV_HEADS * Q_GROUP)
        dk_pieces, dv_pieces = [], []
        for hk in range(N_KV_HEADS):
            a = _attn_head_group(cur, prev, qg_v, kg_v, sink_ref, n, hk)
            do = jnp.concatenate(
                [dyf[:, (hk * Q_GROUP + g) * HEAD_DIM:(hk * Q_GROUP + g + 1) * HEAD_DIM] for g in range(Q_GROUP)],
                axis=0).astype(bf16)
            p = a["p"]
            dp = _dot_nt(do, a["v"])
            dv_band = _dot_tn(p.astype(bf16), do)
            dsum = jnp.sum(p * dp, axis=-1, keepdims=True)
            ds = (p * (dp - dsum)).astype(bf16)
            dsink_col = -a["p_sink"] * dsum
            for g in range(Q_GROUP):
                head = hk * Q_GROUP + g
                dsink_ref[head:head + 1, :] += jnp.sum(dsink_col[g * BLOCK:(g + 1) * BLOCK], axis=0, keepdims=True)
            dqn = _dot_nn(ds, a["kn"])
            dkn_band = _dot_tn(ds, a["qn"])
            dq_hat_g = dqn * scale
            dqg_ref[...] += jnp.sum(dq_hat_g * a["q_hat"], axis=0, keepdims=True)
            dxh = dq_hat_g * qg_v
            dq = a["rq"] * (dxh - a["q_hat"] * jnp.mean(dxh * a["q_hat"], axis=-1, keepdims=True))
            for g in range(Q_GROUP):
                dq_pieces[hk * Q_GROUP + g] = dq[g * BLOCK:(g + 1) * BLOCK]
            dkn = dkn_band[BLOCK:] + carry_k[hk]
            dv_pieces.append(dv_band[BLOCK:] + carry_v[hk])
            carry_k[hk] = dkn_band[:BLOCK]
            carry_v[hk] = dv_band[:BLOCK]
            k_hat = a["k_hat"][BLOCK:]
            dkg_ref[...] += jnp.sum(dkn * k_hat, axis=0, keepdims=True)
            dxk = dkn * kg_v
            dk_pieces.append(a["rk"][BLOCK:] * (dxk - k_hat * jnp.mean(dxk * k_hat, axis=-1, keepdims=True)))
        dqkv_ref[...] = jnp.concatenate(dq_pieces + dk_pieces + dv_pieces, axis=1).astype(bf16)

    blk = lambda w: pl.BlockSpec((per * BLOCK, w), lambda b, j: (b * ng + ng - 1 - j, 0))
    return pl.pallas_call(
        body, name="attn_bwd", grid=(n_seq, ng),
        in_specs=[blk(ATT_WIDTH), blk(768),
                  pl.BlockSpec((BLOCK, 256), lambda b, j: (b * nb + jnp.maximum((ng - 1 - j) * per - 1, 0), 2)),
                  _full((1, HEAD_DIM)), _full((1, HEAD_DIM)), pl.BlockSpec(memory_space=pltpu.SMEM)],
        out_specs=[blk(768), _full((1, HEAD_DIM)), _full((1, HEAD_DIM)), _full((8, 128))],
        out_shape=[jax.ShapeDtypeStruct((T, 768), bf16), jax.ShapeDtypeStruct((1, HEAD_DIM), f32),
                   jax.ShapeDtypeStruct((1, HEAD_DIM), f32), jax.ShapeDtypeStruct((8, 128), f32)],
        scratch_shapes=[pltpu.VMEM((N_KV_HEADS, BLOCK, HEAD_DIM), f32), pltpu.VMEM((N_KV_HEADS, BLOCK, HEAD_DIM), f32)],
        compiler_params=_params(("arbitrary", "arbitrary")),
    )(dy, qkv, qkv, qg, kg, sinks)


def _weight_grad(a, b, tm, tk, name):
    T, M = a.shape
    N = b.shape[1]
    nk = T // tk

    def body(a_ref, b_ref, o_ref, acc_ref):
        k = pl.program_id(1)

        @pl.when(k == 0)
        def _():
            acc_ref[...] = jnp.zeros_like(acc_ref)

        acc_ref[...] += _dot_tn(a_ref[...], b_ref[...])

        @pl.when(k == nk - 1)
        def _():
            o_ref[...] = acc_ref[...].astype(bf16)

    return pl.pallas_call(
        body, name=name, grid=(M // tm, nk),
        in_specs=[pl.BlockSpec((tk, tm), lambda i, k: (k, i)), pl.BlockSpec((tk, N), lambda i, k: (k, 0))],
        out_specs=pl.BlockSpec((None, tm, N), lambda i, k: (0, i, 0)),
        out_shape=jax.ShapeDtypeStruct((1, M, N), bf16),
        scratch_shapes=[pltpu.VMEM((tm, N), f32)],
        compiler_params=_params(("parallel", "arbitrary")),
    )(a, b)


def _weight_grad_rows(a_list, b, tk, name):
    T, N = b.shape
    widths = [a.shape[1] for a in a_list]
    M = sum(widths)
    nk = T // tk
    n = len(a_list)

    def body(*refs):
        a_refs = refs[:n]
        b_ref, o_ref, acc_ref = refs[n:]
        k = pl.program_id(0)

        @pl.when(k == 0)
        def _():
            acc_ref[...] = jnp.zeros_like(acc_ref)

        lo = 0
        for a_ref, wd in zip(a_refs, widths):
            acc_ref[lo:lo + wd, :] += _dot_tn(a_ref[...], b_ref[...])
            lo += wd

        @pl.when(k == nk - 1)
        def _():
            o_ref[...] = acc_ref[...].astype(bf16)

    return pl.pallas_call(
        body, name=name, grid=(nk,),
        in_specs=[pl.BlockSpec((tk, wd), lambda k: (k, 0)) for wd in widths] + [pl.BlockSpec((tk, N), lambda k: (k, 0))],
        out_specs=pl.BlockSpec((None, M, N), lambda k: (0, 0, 0), pipeline_mode=pl.Buffered(1)),
        out_shape=jax.ShapeDtypeStruct((1, M, N), bf16),
        scratch_shapes=[pltpu.VMEM((M, N), f32)],
        compiler_params=_params(("arbitrary",)),
    )(*a_list, b)


def _place(src, layer, src_slot, n_slots, dst_slot, dtype, name, after=None):
    _, _, rows, cols = src.shape
    slots = jnp.stack([src_slot, dst_slot]).astype(jnp.int32)

    def body(slots_ref, s_ref, *rest):
        rest[-1][...] = s_ref[...].astype(dtype)

    return pl.pallas_call(
        body, name=name,
        grid_spec=pltpu.PrefetchScalarGridSpec(
            num_scalar_prefetch=1, grid=(1,),
            in_specs=[pl.BlockSpec((None, None, rows, cols), lambda i, sl: (layer, sl[0], 0, 0))]
            + ([] if after is None else [ANY]),
            out_specs=pl.BlockSpec((None, rows, cols), lambda i, sl: (sl[1], 0, 0))),
        out_shape=jax.ShapeDtypeStruct((n_slots, rows, cols), dtype),
        compiler_params=_params(("arbitrary",)),
    )(slots, src, *([] if after is None else [after]))


HBM = pl.BlockSpec(memory_space=pltpu.HBM)
SEM = pl.BlockSpec(memory_space=pltpu.SEMAPHORE)
DATAFLOW = pltpu.SideEffectType.DATAFLOW_SIDE_EFFECTING


def _other_chips(x, y):
    return [(1 - x, y), (x, 1 - y), (1 - x, 1 - y)]


def _split_start(groups, name):
    nb = [len(bufs) for bufs, _ in groups]
    flat = [b for bufs, _ in groups for b in bufs]
    ns = [len(plan(bufs, dry=True)) for bufs, plan in groups]
    ng = len(groups)

    def body(*refs):
        n_in = len(flat)
        sems = refs[n_in:n_in + 2 * ng]
        thru = refs[n_in + 2 * ng:2 * n_in + 2 * ng]
        token = refs[2 * n_in + 2 * ng]
        off = 0
        for g, (bufs, plan) in enumerate(groups):
            mine = thru[off:off + nb[g]]
            off += nb[g]
            for k, (src, dst, to) in enumerate(plan(mine)):
                pltpu.make_async_remote_copy(
                    src_ref=src, dst_ref=dst, send_sem=sems[2 * g].at[k], recv_sem=sems[2 * g + 1].at[k],
                    device_id=to, device_id_type=MESH).start()
        token[...] = jnp.zeros_like(token)

    out_shape = []
    for n in ns:
        out_shape += [pltpu.SemaphoreType.DMA((n,)), pltpu.SemaphoreType.DMA((n,))]
    out_shape += [pltpu.HBM(b.shape, b.dtype) for b in flat]
    out_shape.append(jax.ShapeDtypeStruct((8, 128), f32))
    res = pl.pallas_call(
        body, name=name, out_shape=tuple(out_shape),
        in_specs=[HBM] * len(flat),
        out_specs=tuple([SEM] * (2 * ng) + [HBM] * len(flat) + [pl.BlockSpec(memory_space=pltpu.VMEM)]),
        input_output_aliases={i: 2 * ng + i for i in range(len(flat))},
        compiler_params=pltpu.CompilerParams(has_side_effects=DATAFLOW),
    )(*[pltpu.with_memory_space_constraint(b, pltpu.HBM) for b in flat])
    out, off = [], 2 * ng
    for g in range(ng):
        out.append((res[2 * g], res[2 * g + 1], list(res[off:off + nb[g]])))
        off += nb[g]
    return out, res[-1]


def _split_wait(bufs, send, recv, plan, after, name):
    nb = len(bufs)

    def body(*refs):
        thru = refs[:nb]
        send_ref, recv_ref = refs[nb], refs[nb + 1]
        for k, (src, dst, to) in enumerate(plan(thru)):
            cp = pltpu.make_async_remote_copy(
                src_ref=src, dst_ref=dst, send_sem=send_ref.at[k], recv_sem=recv_ref.at[k],
                device_id=to, device_id_type=MESH)
            cp.wait_send()
            cp.wait_recv()

    res = pl.pallas_call(
        body, name=name, out_shape=tuple(pltpu.HBM(b.shape, b.dtype) for b in bufs),
        in_specs=[HBM] * nb + [SEM, SEM, ANY], out_specs=tuple([HBM] * nb),
        input_output_aliases={i: i for i in range(nb)},
        compiler_params=pltpu.CompilerParams(has_side_effects=DATAFLOW),
    )(*bufs, send, recv, after)
    return list(res)


def _gather_plan(hrs, n_direct=0):
    def plan(refs, dry=False):
        if dry:
            return [None] * (4 * len(hrs) + 3 * n_direct)
        x, y, c = _mesh_pos()
        me = 4 * x + 2 * y + c
        out = []
        for i in range(n_direct):
            src, land = refs[len(hrs) + 2 * i], refs[len(hrs) + 2 * i + 1]
            out += [(src, land.at[2 * x + y], (*chip, c)) for chip in _other_chips(x, y)]
        for ref, hr in zip(refs, hrs):
            rows = ref.at[pl.ds(pl.multiple_of(me * hr, 16), hr), :]
            out.append((rows, rows, (x, y, 1 - c)))
            out += [(rows, rows, (*chip, c)) for chip in _other_chips(x, y)]
        return out
    return plan


def _pass_plan(hrs):
    def plan(refs, dry=False):
        if dry:
            return [None] * (3 * len(hrs))
        x, y, c = _mesh_pos()
        out = []
        for ref, hr in zip(refs, hrs):
            for chip in _other_chips(x, y):
                rows = ref.at[pl.ds(pl.multiple_of((4 * chip[0] + 2 * chip[1] + c) * hr, 16), hr), :]
                out.append((rows, rows, (x, y, 1 - c)))
        return out
    return plan


def _pair_plan(hrs):
    n = len(hrs)

    def plan(refs, dry=False):
        if dry:
            return [None] * (N_CHIP * n)
        x, y, c = _mesh_pos()
        out = []
        for r in range(n):
            for j in range(N_CHIP):
                start = pl.multiple_of((2 * j + 1 - c) * hrs[r], 16)
                out.append((refs[r].at[0, pl.ds(start, hrs[r]), :], refs[n + r].at[0, j], (x, y, 1 - c)))
        return out
    return plan


def _all_to_all_plan(n):
    def plan(refs, dry=False):
        if dry:
            return [None] * (7 * n)
        x, y, c = _mesh_pos()
        out = []
        for ref in refs:
            mine = ref.at[4 * x + 2 * y + c]
            for fx in range(2):
                for fy in range(2):
                    for fc in range(2):
                        if fx or fy or fc:
                            out.append((mine, mine, (1 - x if fx else x, 1 - y if fy else y, 1 - c if fc else c)))
        return out
    return plan


def _pass_to_sibling(bufs, hrs, name):
    nb = len(bufs)

    def body(*refs):
        out = refs[nb:2 * nb]
        send, recv = refs[2 * nb:]
        x, y, c = _mesh_pos()
        chips = _other_chips(x, y)
        started = []
        for i in range(nb):
            for j, chip in enumerate(chips):
                rows = out[i].at[pl.ds(pl.multiple_of((4 * chip[0] + 2 * chip[1] + c) * hrs[i], 16), hrs[i]), :]
                cp = pltpu.make_async_remote_copy(
                    src_ref=rows, dst_ref=rows, send_sem=send.at[3 * i + j], recv_sem=recv.at[3 * i + j],
                    device_id=(x, y, 1 - c), device_id_type=MESH)
                cp.start()
                started.append(cp)
        for i in range(nb):
            for j, chip in enumerate(chips):
                rows = out[i].at[pl.ds(pl.multiple_of((4 * chip[0] + 2 * chip[1] + 1 - c) * hrs[i], 16), hrs[i]), :]
                pltpu.make_async_remote_copy(
                    src_ref=rows, dst_ref=rows, send_sem=send.at[3 * i + j], recv_sem=recv.at[3 * i + j],
                    device_id=(x, y, 1 - c), device_id_type=MESH).wait_recv()
        for cp in started:
            cp.wait_send()

    return list(pl.pallas_call(
        body, name=name, in_specs=[ANY] * nb, out_specs=[ANY] * nb,
        out_shape=[jax.ShapeDtypeStruct(b.shape, b.dtype) for b in bufs],
        input_output_aliases={i: i for i in range(nb)},
        scratch_shapes=[pltpu.SemaphoreType.DMA((3 * nb,)), pltpu.SemaphoreType.DMA((3 * nb,))],
        compiler_params=pltpu.CompilerParams(has_side_effects=True),
    )(*bufs))


def _pair_exchange(grads, name):
    nr = len(grads)
    n_l = grads[0].shape[0]
    n_sem = nr * n_l * N_CHIP

    def body(*refs):
        src = refs[:nr]
        out = refs[nr:2 * nr]
        send, recv = refs[2 * nr:]
        x, y, c = _mesh_pos()
        copies = []
        for r in range(nr):
            hr = grads[r].shape[1] // N_DEV
            for layer in range(n_l):
                for j in range(N_CHIP):
                    idx = (r * n_l + layer) * N_CHIP + j
                    start = pl.multiple_of((2 * j + 1 - c) * hr, 16)
                    cp = pltpu.make_async_remote_copy(
                        src_ref=src[r].at[layer, pl.ds(start, hr), :], dst_ref=out[r].at[layer, j],
                        send_sem=send.at[idx], recv_sem=recv.at[idx], device_id=(x, y, 1 - c), device_id_type=MESH)
                    cp.start()
                    copies.append(cp)
        for cp in copies:
            cp.wait()

    return pl.pallas_call(
        body, name=name,
        in_specs=[ANY] * nr, out_specs=[ANY] * nr,
        out_shape=[jax.ShapeDtypeStruct((n_l, N_CHIP, g.shape[1] // N_DEV, g.shape[2]), bf16) for g in grads],
        scratch_shapes=[pltpu.SemaphoreType.DMA((n_sem,)), pltpu.SemaphoreType.DMA((n_sem,))],
        compiler_params=pltpu.CompilerParams(has_side_effects=True),
    )(*grads)


def _pair_sum(grad, other, core, chip, name):
    n_l, rows, cols = grad.shape
    hr = rows // N_DEV
    g5 = grad.reshape(n_l, N_CHIP, 2, hr, cols)
    where = jnp.stack([core, chip]).astype(jnp.int32)

    def body(where_ref, g_ref, o_ref, s_ref, mine_ref):
        s_ref[...] = (g_ref[...].astype(f32) + o_ref[...].astype(f32)).astype(bf16)
        mine_ref[...] = s_ref[where_ref[1]]

    return pl.pallas_call(
        body, name=name,
        grid_spec=pltpu.PrefetchScalarGridSpec(
            num_scalar_prefetch=1, grid=(n_l,),
            in_specs=[pl.BlockSpec((None, N_CHIP, None, hr, cols), lambda l, w: (l, 0, w[0], 0, 0)),
                      pl.BlockSpec((None, N_CHIP, hr, cols), lambda l, w: (l, 0, 0, 0))],
            out_specs=[pl.BlockSpec((None, N_CHIP, hr, cols), lambda l, w: (l, 0, 0, 0)),
                       pl.BlockSpec((None, None, hr, cols), lambda l, w: (l, w[1], 0, 0))]),
        out_shape=[jax.ShapeDtypeStruct((n_l, N_CHIP, hr, cols), bf16)] * 2,
        compiler_params=_params(("arbitrary",)),
    )(where, g5, other)


def _chip_plan(nr, n_l):
    def plan(refs, dry=False):
        if dry:
            return [None] * (nr * n_l * 3)
        x, y, c = _mesh_pos()
        out = []
        for r in range(nr):
            for layer in range(n_l):
                for chip in _other_chips(x, y):
                    out.append((refs[r].at[layer, 2 * chip[0] + chip[1]], refs[nr + r].at[layer, 2 * x + y], (*chip, c)))
        return out
    return plan


def _chip_sum(parts, core, name):
    _, _, hr, cols = parts[0].shape

    def body(core_ref, p0_ref, p1_ref, o_ref):
        def total(p_ref):
            acc = p_ref[0].astype(f32) + p_ref[1].astype(f32)
            acc = acc + p_ref[2].astype(f32)
            return acc + p_ref[3].astype(f32)

        @pl.when(pl.program_id(0) == 0)
        def _():
            o_ref[...] = total(p0_ref)

        @pl.when(pl.program_id(0) == 1)
        def _():
            o_ref[...] = total(p1_ref)

    spec = pl.BlockSpec((None, N_CHIP, hr, cols), lambda l, cr: (0, 0, 0, 0))
    return pl.pallas_call(
        body, name=name,
        grid_spec=pltpu.PrefetchScalarGridSpec(
            num_scalar_prefetch=1, grid=(2,), in_specs=[spec, spec],
            out_specs=pl.BlockSpec((None, None, hr, cols), lambda l, cr: (l, cr[0], 0, 0))),
        out_shape=jax.ShapeDtypeStruct((2, 2, hr, cols), f32),
        compiler_params=_params(("arbitrary",)),
    )(core, parts[0], parts[1])


def _share_plan(n):
    def plan(refs, dry=False):
        if dry:
            return [None] * (2 * n)
        x, y, c = _mesh_pos()
        return [(ref.at[layer, c], ref.at[layer, c], (x, y, 1 - c)) for ref in refs for layer in range(2)]
    return plan


def _sum_small(parts, name):
    n, rows, cols = parts.shape

    def body(p_ref, o_ref):
        acc = p_ref[0].astype(f32)
        for d in range(1, n):
            acc = acc + p_ref[d].astype(f32)
        o_ref[...] = acc

    return pl.pallas_call(
        body, name=name, grid=(rows // 16,),
        in_specs=[pl.BlockSpec((n, 16, cols), lambda i: (0, i, 0))], out_specs=pl.BlockSpec((16, cols), lambda i: (i, 0)),
        out_shape=jax.ShapeDtypeStruct((rows, cols), f32),
        compiler_params=_params(("parallel",)),
    )(parts)


def _adamw(w, g, m, v, name):
    n_l, rows, cols = w.shape
    budget = 42 * 1024 * 1024
    tr = next(rows // d for d in range(1, rows + 1)
              if rows % d == 0 and (rows // d) % 8 == 0 and (rows // d) * cols * 4 * 14 <= budget)

    def body(w_ref, g_ref, m_ref, v_ref, d_ref, nm_ref, nv_ref):
        gg = g_ref[...]
        nm = ADAM_B1 * m_ref[...] + (1.0 - ADAM_B1) * gg
        nv = ADAM_B2 * v_ref[...] + (1.0 - ADAM_B2) * (gg * gg)
        m_hat = nm / (1.0 - ADAM_B1 ** ADAM_STEP)
        v_hat = nv / (1.0 - ADAM_B2 ** ADAM_STEP)
        d_ref[...] = -ADAM_LR * (m_hat / (jnp.sqrt(v_hat) + ADAM_EPS) + ADAM_WD * w_ref[...])
        nm_ref[...] = nm
        nv_ref[...] = nv

    blk = pl.BlockSpec((None, tr, cols), lambda l, i: (l, i, 0))
    return pl.pallas_call(
        body, name=name, grid=(n_l, rows // tr),
        in_specs=[blk] * 4, out_specs=[blk] * 3, out_shape=[jax.ShapeDtypeStruct((n_l, rows, cols), f32)] * 3,
        compiler_params=_params(("parallel", "parallel")),
    )(w, g, m, v)


SMALL = ("mix_norm", "q_norm", "k_norm", "sinks", "sgu_norm", "w_s", "b_s", "ffn_norm", "conv_b", "conv_w")


def _pack_small(arrs):
    flat = jnp.concatenate([a.reshape(-1) for a in arrs])
    pad = (-flat.shape[0]) % (16 * 1024)
    return jnp.pad(flat, (0, pad)).reshape(-1, 1024)


def _unpack_small(pack, shapes):
    flat = pack.reshape(-1)
    out, off = [], 0
    for s in shapes:
        n = int(np.prod(s))
        out.append(flat[off:off + n].reshape(s))
        off += n
    return out


def kernel(x, mix_norm, w_in, q_norm, k_norm, sinks, sgu_norm, w_s, b_s, w_oa, w_ob, w_out, ffn_norm, w_up, conv_w, conv_b, w_down, loss_target, m_mix_norm, m_w_in, m_q_norm, m_k_norm, m_sinks, m_sgu_norm, m_w_s, m_b_s, m_w_oa, m_w_ob, m_w_out, m_ffn_norm, m_w_up, m_conv_w, m_conv_b, m_w_down, v_mix_norm, v_w_in, v_q_norm, v_k_norm, v_sinks, v_sgu_norm, v_w_s, v_b_s, v_w_oa, v_w_ob, v_w_out, v_ffn_norm, v_w_up, v_conv_w, v_conv_b, v_w_down):
    weights = dict(mix_norm=mix_norm, w_in=w_in, q_norm=q_norm, k_norm=k_norm, sinks=sinks, sgu_norm=sgu_norm,
                   w_s=w_s, b_s=b_s, w_oa=w_oa, w_ob=w_ob, w_out=w_out, ffn_norm=ffn_norm, w_up=w_up,
                   conv_w=conv_w, conv_b=conv_b, w_down=w_down)
    mom_m = dict(mix_norm=m_mix_norm, w_in=m_w_in, q_norm=m_q_norm, k_norm=m_k_norm, sinks=m_sinks,
                 sgu_norm=m_sgu_norm, w_s=m_w_s, b_s=m_b_s, w_oa=m_w_oa, w_ob=m_w_ob, w_out=m_w_out,
                 ffn_norm=m_ffn_norm, w_up=m_w_up, conv_w=m_conv_w, conv_b=m_conv_b, w_down=m_w_down)
    mom_v = dict(mix_norm=v_mix_norm, w_in=v_w_in, q_norm=v_q_norm, k_norm=v_k_norm, sinks=v_sinks,
                 sgu_norm=v_sgu_norm, w_s=v_w_s, b_s=v_b_s, w_oa=v_w_oa, w_ob=v_w_ob, w_out=v_w_out,
                 ffn_norm=v_ffn_norm, w_up=v_w_up, conv_w=v_conv_w, conv_b=v_conv_b, w_down=v_w_down)
    n_seq, seq, _ = x.shape
    T = n_seq * seq
    core = lax.axis_index("c")
    chip = 2 * lax.axis_index("x") + lax.axis_index("y")
    tm = min(512, seq)
    tm_ff = min(256, seq)
    tm_sgu = min(512, seq)
    tk_dw = min(2048, T)

    me = 2 * chip + core
    names = [r[0] for r in REGIONS]
    hrs = {name: rows // N_DEV for name, rows, _, _ in REGIONS}
    def placed(l, name, after=None):
        _, rows, cols, transposed = next(r for r in REGIONS if r[0] == name)
        shard = (jnp.swapaxes(weights[name], 1, 2) if transposed else weights[name]).reshape(2, 2, hrs[name], cols)
        return _place(shard, l, core, N_DEV, me, bf16, f"place_{name}_{l}", after).reshape(rows, cols)

    group_keys = [[(0, "w_in")], [(0, n) for n in names[1:]],
                  [(1, n) for n in ("w_in", "w_oa", "w_ob", "w_out")], [(1, "w_up"), (1, "w_down")]]
    n_direct = [1, 0, 0, 0]
    plans = [_gather_plan([hrs[n] for _, n in keys], nd) for keys, nd in zip(group_keys, n_direct)]
    first_bufs = [placed(0, "w_in"), conv_w, jnp.zeros((N_CHIP,) + conv_w.shape, f32)]
    started, tok = _split_start([(first_bufs, plans[0])], "gather_start_0")
    first_start_done = jnp.broadcast_to(tok[0:1, 0:1], (512, D_MODEL))
    rest_bufs = [[placed(l, n, first_start_done) for l, n in keys] for keys in group_keys[1:]]
    more, tok = _split_start(list(zip(rest_bufs, plans[1:])), "gather_start_1")
    started += more
    second_start_done = jnp.broadcast_to(tok[0:1, 0:1], (512, D_MODEL))
    gathered = [{}, {}]

    def arrived(g, after):
        send, recv, bufs = started[g]
        hr_list = [hrs[n] for _, n in group_keys[g]]
        bufs = _split_wait(bufs, send, recv, _gather_plan(hr_list, n_direct[g]), after, f"gather_wait_{g}")
        return bufs[:len(hr_list)], bufs[len(hr_list):]

    def start_pass(g, bufs):
        (res,), token = _split_start([(bufs, _pass_plan([hrs[n] for _, n in group_keys[g]]))], f"pass_start_{g}")
        return res, token[0:1, 0:1]

    def finish_pass(g, res, after):
        send, recv, bufs = res
        use(g, _split_wait(bufs, send, recv, _pass_plan([hrs[n] for _, n in group_keys[g]]), after, f"pass_wait_{g}"))

    def use(g, bufs):
        for (l, n), b in zip(group_keys[g], bufs):
            gathered[l][n] = b

    xs = x.reshape(T, D_MODEL)
    bufs, (_, conv_w_land) = arrived(0, second_start_done)
    use(0, _pass_to_sibling(bufs, [hrs["w_in"]], "gather_pass_0"))
    conv_w_all = lax.dynamic_update_slice(conv_w_land, conv_w[None], (chip, 0, 0, 0))
    conv_w_full = jnp.concatenate([conv_w_all[j] for j in range(N_CHIP)], axis=-1)
    saved = []
    cur = xs
    for l in range(2):
        wl = gathered[l]
        b_col = b_s[l].reshape(SGU_GROUPS, BLOCK, 1)
        qkv, su, sv, ga, gb, h = _in_proj(cur, mix_norm[l][None], wl["w_in"], l, tm)
        y_att = _attn_fwd(qkv, q_norm[l][None], k_norm[l][None], sinks[l], n_seq, seq)
        later = 1 if l == 0 else 3
        passing, tok = start_pass(later, arrived(later, y_att)[0])
        y_sgu = _sgu_fwd(su, sv, sgu_norm[l][None] + tok, w_s[l], b_col, tm_sgu)
        finish_pass(later, passing, y_sgu)
        x1, merged, a_o, b_o = _merge_fwd(cur, y_att, y_sgu, ga, gb, wl["w_oa"], wl["w_ob"], wl["w_out"], l, tm)
        h2, z, act = _ffn_up(x1, ffn_norm[l][None], wl["w_up"], conv_w_full[l], conv_b[l][None], l, seq, tm_ff)
        saved.append(dict(x=cur, qkv=qkv, su=su, sv=sv, ga=ga, gb=gb, h=h, y_att=y_att, y_sgu=y_sgu, x1=x1,
                          merged=merged, a=a_o, b=b_o, h2=h2, z=z, act=act, b_col=b_col))
        if l == 0:
            pass_2, tok = start_pass(2, arrived(2, act)[0])
            cur = _ffn_down(x1, act, wl["w_down"], tok, l, tm)
            finish_pass(2, pass_2, cur)
        else:
            dy, dyb, loss_part = _ffn_down_loss(x1, act, wl["w_down"], loss_target.reshape(T, D_MODEL), l, tm)

    core_arr = core.astype(jnp.int32).reshape(1)
    big = [{}, {}]
    small = {name: [None, None] for name in SMALL}

    def start_pairs(l, keys, tag):
        gl = [big[l][n] for n in keys]
        land = [lax.empty((1, N_CHIP, hrs[n], g.shape[2]), bf16) for n, g in zip(keys, gl)]
        (res,), token = _split_start([(gl + land, _pair_plan([hrs[n] for n in keys]))], f"pair_start_{tag}")
        return (l, keys, res, tag), token[0:1, 0:1]

    def pairs_to_chips(state, after):
        l, keys, (send, recv, bufs), tag = state
        bufs = _split_wait(bufs, send, recv, _pair_plan([hrs[n] for n in keys]), after, f"pair_wait_{tag}")
        return sums_to_chips(l, keys, bufs[:len(keys)], bufs[len(keys):], tag)

    def sums_to_chips(l, keys, gl, from_sibling, tag):
        pairs = [_pair_sum(g, o, core, chip, f"pair_sum_{n}_{l}") for g, o, n in zip(gl, from_sibling, keys)]
        bufs = [p[0] for p in pairs] + [p[1] for p in pairs]
        (res,), token = _split_start([(bufs, _chip_plan(len(keys), 1))], f"chip_start_{tag}")
        return (l, keys, res, tag), token[0:1, 0:1]

    def start_reduce(l, keys, tag):
        gl = [big[l][n] for n in keys]
        return sums_to_chips(l, keys, gl, _pair_exchange(gl, f"pair_exchange_{tag}"), tag)

    landed = {}

    def finish_reduce(state, after):
        l, keys, (send, recv, bufs), tag = state
        bufs = _split_wait(bufs, send, recv, _chip_plan(len(keys), 1), after, f"chip_wait_{tag}")
        for n, p in zip(keys, bufs[len(keys):]):
            landed[(l, n)] = p

    rest = [n for n in SMALL if n != "w_s"]
    rest_shapes = [weights[n].shape[1:] if n != "conv_w" else (3, 2 * D_FF) for n in rest]
    zero = jnp.zeros((), jnp.int32)

    def start_small(l):
        extra = loss_part[0, 0:1] if l == 1 else jnp.zeros((1,), f32)
        packs = [(_pack_small([small[n][l] for n in rest] + [extra]), f32, "small"),
                 (small["w_s"][l].reshape(-1, 1024), bf16, "w_s")]
        bufs = [_place(p[None, None], 0, zero, N_DEV, me, dt, f"place_{tag}_{l}") for p, dt, tag in packs]
        (res,), token = _split_start([(bufs, _all_to_all_plan(2))], f"small_start_{l}")
        return res, token[0:1, 0:1]

    def finish_small(l, res, after):
        send, recv, bufs = res
        bufs = _split_wait(bufs, send, recv, _all_to_all_plan(2), after, f"small_wait_{l}")
        out = dict(zip(rest + ["loss"], _unpack_small(_sum_small(bufs[0], f"sum_small_{l}"), rest_shapes + [(1,)])))
        out["w_s"] = _sum_small(bufs[1], f"sum_w_s_{l}").reshape(w_s.shape[1:])
        return out

    pending = []
    after_start = jnp.zeros((1, 1), f32)
    for l in (1, 0):
        s = saved[l]
        wl = gathered[l]
        dz, dconv = _ffn_bwd(dyb, s["z"], conv_w_full[l], conv_b[l][None] + after_start, wl["w_down"], l, seq, tm_ff)
        big[l]["w_down"] = _weight_grad(s["act"], dyb, 1408, tk_dw, f"dw_down_{l}")
        big[l]["w_up"] = _weight_grad(dz, s["h2"], 1408, tk_dw, f"dw_up_{l}")
        ffn_gain, sgu_gain, q_gain = ffn_norm[l][None], sgu_norm[l][None], q_norm[l][None]
        if l == 0:
            pairs_a, tok = start_pairs(0, ["w_down", "w_up"], "0a")
            ffn_gain = ffn_gain + tok
        dx1, dx1b, d_ffn = _norm_bwd([dz], wl["w_up"], l, s["x1"], ffn_gain, dy, tm, f"ffn_norm_bwd_{l}")
        if l == 0:
            state, tok = pairs_to_chips(pairs_a, dx1b)
            pending.append(state)
            sgu_gain = sgu_gain + tok
        small["conv_w"][l] = dconv[0:3]
        small["conv_b"][l] = dconv[3]
        small["ffn_norm"][l] = d_ffn[0]
        da, db, dga, dgb, dya, dys = _merge_bwd(dx1b, s["ga"], s["gb"], s["a"], s["b"],
                                                wl["w_oa"], wl["w_ob"], wl["w_out"], l, tm)
        big[l]["w_out"] = _weight_grad(s["merged"], dx1b, 1024, tk_dw, f"dw_out_{l}")
        big[l]["w_oa"] = _weight_grad(da, s["y_att"], 1024, tk_dw, f"dw_oa_{l}")
        big[l]["w_ob"] = _weight_grad(db, s["y_sgu"], 1024, tk_dw, f"dw_ob_{l}")
        if l == 0:
            pairs_m, tok = start_pairs(0, ["w_out", "w_oa", "w_ob"], "0m")
            sgu_gain = sgu_gain + tok
        dsu, dsv, d_ws, d_bs, d_sgu = _sgu_bwd(dys, s["su"], s["sv"], sgu_gain, w_s[l], s["b_col"], tm_sgu)
        if l == 0:
            state, tok = pairs_to_chips(pairs_m, dsv)
            pending.append(state)
            q_gain = q_gain + tok
        causal = np.tril(np.ones((BLOCK, BLOCK), bool))
        small["w_s"][l] = jnp.where(causal[None], d_ws, 0.0)
        small["b_s"][l] = d_bs[:, :, 0]
        small["sgu_norm"][l] = d_sgu[0]
        dqkv, d_qg, d_kg, d_sink = _attn_bwd(dya, s["qkv"], q_gain, k_norm[l][None], sinks[l], n_seq, seq)
        small["q_norm"][l] = d_qg[0]
        small["k_norm"][l] = d_kg[0]
        small["sinks"][l] = d_sink[:, 0]
        dproj = [dqkv, dsu, dsv, dga, dgb]
        big[l]["w_in"] = _weight_grad_rows(dproj, s["h"], min(1024, T), f"dw_in_{l}")
        if l == 1:
            pairs_1, tok = start_pairs(1, names, "1")
        else:
            state, tok = start_reduce(0, ["w_in"], "0b")
            pending.append(state)
        dy, dyb, d_mix = _norm_bwd(dproj, wl["w_in"], l, s["x"], mix_norm[l][None] + tok, dx1, tm, f"mix_norm_bwd_{l}")
        small["mix_norm"][l] = d_mix[0]
        if l == 1:
            state, tok = pairs_to_chips(pairs_1, dyb)
            pending.append(state)
            small_1, after_start = start_small(1)
            after_start = after_start + tok
    grad_x = dy.reshape(n_seq, seq, D_MODEL)

    small_0, _ = start_small(0)
    for state in pending:
        finish_reduce(state, dyb)
    share_keys = [["w_in", "w_oa", "w_ob", "w_out"], ["w_up", "w_down"]]
    halves = {n: _chip_sum([landed[(0, n)], landed[(1, n)]], core_arr, f"chip_sum_{n}") for n in names}
    swaps, _ = _split_start([([halves[n] for n in keys], _share_plan(len(keys))) for keys in share_keys], "share_start")
    shared = {}

    def finish_share(k, after):
        send, recv, bufs = swaps[k]
        shared.update(zip(share_keys[k], _split_wait(bufs, send, recv, _share_plan(len(bufs)), after, f"share_wait_{k}")))

    finish_share(0, dyb)
    grad, delta, new_m, new_v = {}, {}, {}, {}
    flip = lambda a: jnp.swapaxes(a, 1, 2)
    for name, rows, cols, transposed in sorted(REGIONS, key=lambda r: r[0] in share_keys[1]):
        if name == share_keys[1][0]:
            finish_share(1, delta[share_keys[0][-1]])
        g = shared[name].reshape(2, rows // N_CHIP, cols)
        if transposed and weights[name].shape[2] % 128:
            d, nm, nv = _adamw(flip(weights[name]), g, flip(mom_m[name]), flip(mom_v[name]), f"adamw_{name}")
            grad[name], delta[name], new_m[name], new_v[name] = flip(g), flip(d), flip(nm), flip(nv)
        else:
            grad[name] = flip(g) if transposed else g
            delta[name], new_m[name], new_v[name] = _adamw(weights[name], grad[name], mom_m[name], mom_v[name],
                                                           f"adamw_{name}")

    per_layer = [finish_small(0, small_0, delta["w_down"]), finish_small(1, small_1, dyb)]
    loss = per_layer[1]["loss"][0]
    grad_small = {n: jnp.stack([per_layer[0][n], per_layer[1][n]]) for n in SMALL}
    cw_cols = conv_w.shape[-1]
    grad_small["conv_w"] = lax.dynamic_slice_in_dim(grad_small["conv_w"], chip * cw_cols, cw_cols, axis=2)

    as_rows = lambda a: a.reshape(2, -1, BLOCK)
    d, nm, nv = _adamw(as_rows(w_s), as_rows(grad_small["w_s"]), as_rows(m_w_s), as_rows(v_w_s), "adamw_w_s")
    grad["w_s"], delta["w_s"], new_m["w_s"], new_v["w_s"] = (
        grad_small["w_s"], d.reshape(w_s.shape), nm.reshape(w_s.shape), nv.reshape(w_s.shape))
    shapes = [weights[n].shape for n in rest]
    d, nm, nv = _adamw(_pack_small([weights[n] for n in rest])[None], _pack_small([grad_small[n] for n in rest])[None],
                       _pack_small([mom_m[n] for n in rest])[None], _pack_small([mom_v[n] for n in rest])[None],
                       "adamw_small")
    for n, dd, mm, vv in zip(rest, _unpack_small(d, shapes), _unpack_small(nm, shapes), _unpack_small(nv, shapes)):
        grad[n], delta[n], new_m[n], new_v[n] = grad_small[n], dd, mm, vv

    order = ["mix_norm", "w_in", "q_norm", "k_norm", "sinks", "sgu_norm", "w_s", "b_s", "w_oa", "w_ob", "w_out",
             "ffn_norm", "w_up", "conv_w", "conv_b", "w_down"]
    return (loss, grad_x, *[grad[n] for n in order], *[delta[n] for n in order],
            *[new_m[n] for n in order], *[new_v[n] for n in order])
```

```python
import functools

import numpy as np
import jax
import jax.numpy as jnp
from jax import lax
from jax.experimental import pallas as pl
from jax.experimental.pallas import tpu as pltpu

bf16 = jnp.bfloat16
f32 = jnp.float32

D_MODEL = 1024
ATT_WIDTH = 512
KV_WIDTH = 128
SGU_WIDTH = 512
HEAD_DIM = 64
N_KV_HEADS = 2
Q_GROUP = 4
BLOCK = 128
SGU_GROUPS = 8
IN_WIDTH = 3840
D_FF = 2816
NORM_EPS = 1e-6
NEG_INF = -1e30
N_DEV = 8
N_CHIP = 4

ADAM_LR = 0.001
ADAM_B1 = 0.9
ADAM_B2 = 0.999
ADAM_EPS = 1e-08
ADAM_WD = 0.01
ADAM_STEP = 10

V7X_VMEM_LIMIT = 56 * 1024 * 1024
FF_CHUNK = 2816

REGIONS = (
    ("w_in", IN_WIDTH, D_MODEL, True),
    ("w_oa", D_MODEL, ATT_WIDTH, True),
    ("w_ob", D_MODEL, SGU_WIDTH, True),
    ("w_out", D_MODEL, D_MODEL, False),
    ("w_up", 2 * D_FF, D_MODEL, True),
    ("w_down", D_FF, D_MODEL, False),
)
MESH = pl.DeviceIdType.MESH
ANY = pl.BlockSpec(memory_space=pl.ANY)


def _params(sem=None, **kw):
    return pltpu.CompilerParams(dimension_semantics=sem, vmem_limit_bytes=V7X_VMEM_LIMIT, **kw)


def _wspec(rows, cols, layer=None):
    del layer
    return pl.BlockSpec((rows, cols), lambda *_: (0, 0), pipeline_mode=pl.Buffered(1))


def _full(shape):
    nd = len(shape)
    return pl.BlockSpec(shape, lambda *_: (0,) * nd)


def _dot_nn(a, b):
    return jnp.dot(a, b, preferred_element_type=f32)


def _dot_nt(a, b):
    return lax.dot_general(a, b, (((1,), (1,)), ((), ())), preferred_element_type=f32)


def _dot_tn(a, b):
    return lax.dot_general(a, b, (((0,), (0,)), ((), ())), preferred_element_type=f32)


_GELU_C = float(np.sqrt(2.0 / np.pi))


def _gelu(x):
    return 0.5 * x * (1.0 + jnp.tanh(_GELU_C * (x + 0.044715 * x * x * x)))


def _gelu_grad(x):
    t = jnp.tanh(_GELU_C * (x + 0.044715 * x * x * x))
    du = _GELU_C * (1.0 + 3.0 * 0.044715 * x * x)
    return 0.5 * (1.0 + t) + 0.5 * x * (1.0 - t * t) * du


def _sigmoid(x):
    return 0.5 * jnp.tanh(0.5 * x) + 0.5


def _rms(x):
    return lax.rsqrt(jnp.mean(x * x, axis=-1, keepdims=True) + NORM_EPS)


def _mesh_pos():
    return lax.axis_index("x"), lax.axis_index("y"), lax.axis_index("c")


def _in_proj(x, gain, w_in_t, layer, tm):
    T = x.shape[0]

    def body(x_ref, g_ref, w_ref, qkv_ref, su_ref, sv_ref, ga_ref, gb_ref, h_ref):
        xf = x_ref[...]
        h = (xf * _rms(xf) * g_ref[...]).astype(bf16)
        h_ref[...] = h
        qkv_ref[...] = _dot_nt(h, w_ref[0:768, :])
        su_ref[...] = _dot_nt(h, w_ref[768:1280, :]).astype(bf16)
        sv_ref[...] = _dot_nt(h, w_ref[1280:1792, :]).astype(bf16)
        ga_ref[...] = _dot_nt(h, w_ref[1792:2816, :]).astype(bf16)
        gb_ref[...] = _dot_nt(h, w_ref[2816:3840, :]).astype(bf16)

    row = lambda w: pl.BlockSpec((tm, w), lambda i: (i, 0))
    return pl.pallas_call(
        body, name=f"in_proj_{layer}", grid=(T // tm,),
        in_specs=[row(D_MODEL), _full((1, D_MODEL)), _wspec(IN_WIDTH, D_MODEL, layer)],
        out_specs=[row(768), row(512), row(512), row(1024), row(1024), row(D_MODEL)],
        out_shape=[jax.ShapeDtypeStruct((T, 768), f32), jax.ShapeDtypeStruct((T, 512), bf16),
                   jax.ShapeDtypeStruct((T, 512), bf16), jax.ShapeDtypeStruct((T, 1024), bf16),
                   jax.ShapeDtypeStruct((T, 1024), bf16), jax.ShapeDtypeStruct((T, D_MODEL), bf16)],
        compiler_params=_params(("parallel",)),
    )(x, gain, w_in_t)


def _attn_head_group(cur, prev, qg, kg, sink_ref, n, hk):
    lo = hk * HEAD_DIM
    k_raw = jnp.concatenate([prev[:, lo:lo + HEAD_DIM], cur[:, 512 + lo:512 + lo + HEAD_DIM]], axis=0)
    v_band = jnp.concatenate([prev[:, 128 + lo:128 + lo + HEAD_DIM], cur[:, 640 + lo:640 + lo + HEAD_DIM]], axis=0)
    rk = _rms(k_raw)
    k_hat = k_raw * rk
    kn = (k_hat * kg).astype(bf16)
    q_raw = jnp.concatenate(
        [cur[:, (hk * Q_GROUP + g) * HEAD_DIM:(hk * Q_GROUP + g + 1) * HEAD_DIM] for g in range(Q_GROUP)], axis=0)
    rq = _rms(q_raw)
    q_hat = q_raw * rq
    qn = (q_hat * qg * (HEAD_DIM ** -0.5)).astype(bf16)
    s = _dot_nt(qn, kn)
    rows = lax.broadcasted_iota(jnp.int32, (Q_GROUP * BLOCK, 1), 0)
    g_of_row = rows // BLOCK
    qi = rows - g_of_row * BLOCK
    kj = lax.broadcasted_iota(jnp.int32, (1, 2 * BLOCK), 1)
    dist = qi + BLOCK - kj
    valid = (dist >= 0) & (dist < BLOCK) & ((kj >= BLOCK) | (n > 0))
    slope = jnp.zeros((Q_GROUP * BLOCK, 1), f32)
    sink = jnp.zeros((Q_GROUP * BLOCK, 1), f32)
    for g in range(Q_GROUP):
        head = hk * Q_GROUP + g
        slope = jnp.where(g_of_row == g, float(np.exp2(-8.0 * (head + 1.0) / 8.0)), slope)
        sink = jnp.where(g_of_row == g, sink_ref[head], sink)
    s = jnp.where(valid, s - slope * dist.astype(f32), NEG_INF)
    m = jnp.maximum(jnp.max(s, axis=-1, keepdims=True), sink)
    e = jnp.exp(s - m)
    e_sink = jnp.exp(sink - m)
    inv = 1.0 / (jnp.sum(e, axis=-1, keepdims=True) + e_sink)
    return dict(k_raw=k_raw, rk=rk, k_hat=k_hat, kn=kn, v=v_band.astype(bf16), q_hat=q_hat, rq=rq, qn=qn,
                p=e * inv, p_sink=e_sink * inv)


def _attn_fwd(qkv, qg, kg, sinks, n_seq, seq):
    T = n_seq * seq
    nb = seq // BLOCK

    per = 2 if nb % 2 == 0 else 1

    def body(cur_ref, prev_ref, qg_ref, kg_ref, sink_ref, y_ref):
        for sub in range(per):
            n = pl.program_id(1) * per + sub
            cur = cur_ref[sub * BLOCK:(sub + 1) * BLOCK, :]
            prev = prev_ref[...] if sub == 0 else cur_ref[(sub - 1) * BLOCK:sub * BLOCK, 512:768]
            pieces = [None] * (N_KV_HEADS * Q_GROUP)
            for hk in range(N_KV_HEADS):
                a = _attn_head_group(cur, prev, qg_ref[...], kg_ref[...], sink_ref, n, hk)
                o = _dot_nn(a["p"].astype(bf16), a["v"])
                for g in range(Q_GROUP):
                    pieces[hk * Q_GROUP + g] = o[g * BLOCK:(g + 1) * BLOCK]
            y_ref[sub * BLOCK:(sub + 1) * BLOCK, :] = jnp.concatenate(pieces, axis=1).astype(bf16)

    return pl.pallas_call(
        body, name="attn_fwd", grid=(n_seq, nb // per),
        in_specs=[pl.BlockSpec((per * BLOCK, 768), lambda b, n: (b * (nb // per) + n, 0)),
                  pl.BlockSpec((BLOCK, 256), lambda b, n: (b * nb + jnp.maximum(n * per - 1, 0), 2)),
                  _full((1, HEAD_DIM)), _full((1, HEAD_DIM)),
                  pl.BlockSpec(memory_space=pltpu.SMEM)],
        out_specs=pl.BlockSpec((per * BLOCK, ATT_WIDTH), lambda b, n: (b * (nb // per) + n, 0)),
        out_shape=jax.ShapeDtypeStruct((T, ATT_WIDTH), bf16),
        compiler_params=_params(("parallel", "parallel")),
    )(qkv, qkv, qg, kg, sinks)


def _sgu_chunk(su, sv, gain, w_ref, b_ref):
    u = _gelu(su)
    vg = _gelu(sv)
    rv = _rms(vg)
    v_hat = vg * rv
    vn = (v_hat * gain).astype(bf16)
    causal = (lax.broadcasted_iota(jnp.int32, (BLOCK, BLOCK), 0) >= lax.broadcasted_iota(jnp.int32, (BLOCK, BLOCK), 1))
    w_tril = [jnp.where(causal, w_ref[g], 0.0).astype(bf16) for g in range(SGU_GROUPS)]
    gd = SGU_WIDTH // SGU_GROUPS
    mixed = jnp.concatenate(
        [_dot_nn(w_tril[g], vn[:, g * gd:(g + 1) * gd]) + b_ref[g] for g in range(SGU_GROUPS)], axis=1)
    return u, rv, v_hat, vn, w_tril, mixed


def _sgu_fwd(su, sv, gain, w_s, b_s, tm):
    T = su.shape[0]

    def body(su_ref, sv_ref, g_ref, w_ref, b_ref, y_ref):
        for ch in range(tm // BLOCK):
            rows = slice(ch * BLOCK, (ch + 1) * BLOCK)
            u, _, _, _, _, mixed = _sgu_chunk(su_ref[rows, :].astype(f32), sv_ref[rows, :].astype(f32),
                                              g_ref[...], w_ref, b_ref)
            y_ref[rows, :] = (u * mixed).astype(bf16)

    row = pl.BlockSpec((tm, SGU_WIDTH), lambda i: (i, 0))
    return pl.pallas_call(
        body, name="sgu_fwd", grid=(T // tm,),
        in_specs=[row, row, _full((1, SGU_WIDTH)), _full((SGU_GROUPS, BLOCK, BLOCK)), _full((SGU_GROUPS, BLOCK, 1))],
        out_specs=row, out_shape=jax.ShapeDtypeStruct((T, SGU_WIDTH), bf16),
        compiler_params=_params(("parallel",)),
    )(su, sv, gain, w_s, b_s)


def _merge_fwd(x, y_att, y_sgu, ga, gb, w_oa_t, w_ob_t, w_out, layer, tm):
    T = x.shape[0]

    def body(x_ref, ya_ref, ys_ref, ga_ref, gb_ref, woa_ref, wob_ref, wout_ref, x1_ref, m_ref, a_ref, b_ref):
        a = _dot_nt(ya_ref[...], woa_ref[...])
        b = _dot_nt(ys_ref[...], wob_ref[...])
        a_ref[...] = a.astype(bf16)
        b_ref[...] = b.astype(bf16)
        merged = (_sigmoid(ga_ref[...].astype(f32)) * a + _sigmoid(gb_ref[...].astype(f32)) * b).astype(bf16)
        m_ref[...] = merged
        x1_ref[...] = x_ref[...] + _dot_nn(merged, wout_ref[...])

    row = lambda w: pl.BlockSpec((tm, w), lambda i: (i, 0))
    return pl.pallas_call(
        body, name=f"merge_fwd_{layer}", grid=(T // tm,),
        in_specs=[row(D_MODEL), row(512), row(512), row(1024), row(1024),
                  _wspec(D_MODEL, ATT_WIDTH, layer), _wspec(D_MODEL, SGU_WIDTH, layer), _wspec(D_MODEL, D_MODEL, layer)],
        out_specs=[row(D_MODEL)] * 4,
        out_shape=[jax.ShapeDtypeStruct((T, D_MODEL), f32)] + [jax.ShapeDtypeStruct((T, D_MODEL), bf16)] * 3,
        compiler_params=_params(("parallel",)),
    )(x, y_att, y_sgu, ga, gb, w_oa_t, w_ob_t, w_out)


def _tile_permutation(tm):
    r = np.arange(tm)
    p = np.zeros((tm, tm), np.float32)
    p[r, (r % 8) * (tm // 8) + r // 8] = 1.0
    return jnp.asarray(p, bf16), jnp.asarray(p.T, bf16)


def _stage_taps_before(buf, zz, prev, tm):
    first = lax.broadcasted_iota(jnp.int32, (8, 1), 0) == 0
    buf[16:16 + tm, :] = zz
    buf[0:8, :] = jnp.where(first, prev[7:8], pltpu.roll(buf[tm:tm + 8, :], 1, 0))
    buf[8:16, :] = jnp.where(first, prev[15:16], pltpu.roll(buf[tm + 8:tm + 16, :], 1, 0))


def _stage_taps_after(buf, nxt, tm):
    last = lax.broadcasted_iota(jnp.int32, (8, 1), 0) == 7
    buf[tm:tm + 8, :] = jnp.where(last, nxt[0:1], pltpu.roll(buf[0:8, :], 7, 0))
    buf[tm + 8:tm + 16, :] = jnp.where(last, nxt[8:9], pltpu.roll(buf[8:16, :], 7, 0))


def _conv_rows(buf, r, n, coef):
    z2 = buf[pl.ds(r, n), :]
    z1 = buf[pl.ds(pl.multiple_of(r + 8, 8), n), :]
    z0 = buf[pl.ds(pl.multiple_of(r + 16, 8), n), :]
    return coef[0] + coef[1] * z2 + coef[2] * z1 + coef[3] * z0


def _ffn_up(x1, gain, w_up_t, conv_w, conv_b, layer, seq, tm):
    T = x1.shape[0]
    tps = seq // tm
    perm, perm_t = _tile_permutation(tm)

    rg = 16

    def body(x_ref, g_ref, w_ref, cw_ref, cb_ref, p_ref, pt_ref, h2_ref, z_ref, act_ref, carry_ref,
             zg_buf, zv_buf, actp_buf):
        i = pl.program_id(0)

        @pl.when(i % tps == 0)
        def _():
            carry_ref[...] = jnp.zeros_like(carry_ref)

        xf = x_ref[...]
        h2 = (xf * _rms(xf) * g_ref[...]).astype(bf16)
        h2_ref[...] = h2
        h2p = _dot_nn(p_ref[...], h2).astype(bf16)
        for cc in range(D_FF // FF_CHUNK):
            cols_g = slice(cc * FF_CHUNK, (cc + 1) * FF_CHUNK)
            cols_v = slice(D_FF + cc * FF_CHUNK, D_FF + (cc + 1) * FF_CHUNK)
            for buf, cols in ((zg_buf, cols_g), (zv_buf, cols_v)):
                zb = _dot_nt(h2p, w_ref[cols, :]).astype(bf16)
                z_ref[:, cols] = zb
                _stage_taps_before(buf, zb.astype(f32), carry_ref[:, cols], tm)
                carry_ref[:, cols] = buf[tm:tm + 16, :]
            coef = [jnp.broadcast_to(v, (rg, FF_CHUNK)) for cols in (cols_g, cols_v)
                    for v in (cb_ref[:, cols], cw_ref[0:1, cols], cw_ref[1:2, cols], cw_ref[2:3, cols])]

            def rows_step(j, carry, coef=coef):
                r = pl.multiple_of(j * rg, rg)
                zcg, zcv = (_conv_rows(buf, r, rg, coef[4 * k:4 * k + 4]) for k, buf in enumerate((zg_buf, zv_buf)))
                actp_buf[pl.ds(r, rg), :] = (zcg * _sigmoid(zcg) * zcv).astype(bf16)
                return carry

            lax.fori_loop(0, tm // rg, rows_step, 0, unroll=True)
            act_ref[:, cols_g] = _dot_nn(pt_ref[...], actp_buf[...]).astype(bf16)

    row = lambda w: pl.BlockSpec((tm, w), lambda i: (i, 0))
    return pl.pallas_call(
        body, name=f"ffn_up_{layer}", grid=(T // tm,),
        in_specs=[row(D_MODEL), _full((1, D_MODEL)), _wspec(2 * D_FF, D_MODEL, layer),
                  _full((3, 2 * D_FF)), _full((1, 2 * D_FF)), _full((tm, tm)), _full((tm, tm))],
        out_specs=[row(D_MODEL), row(2 * D_FF), row(D_FF)],
        out_shape=[jax.ShapeDtypeStruct((T, D_MODEL), bf16), jax.ShapeDtypeStruct((T, 2 * D_FF), bf16),
                   jax.ShapeDtypeStruct((T, D_FF), bf16)],
        scratch_shapes=[pltpu.VMEM((16, 2 * D_FF), f32), pltpu.VMEM((tm + 16, FF_CHUNK), f32),
                        pltpu.VMEM((tm + 16, FF_CHUNK), f32), pltpu.VMEM((tm, FF_CHUNK), bf16)],
        compiler_params=_params(("arbitrary",)),
    )(x1, gain, w_up_t, conv_w, conv_b, perm, perm_t)


def _ffn_down(x1, act, w_down, after, layer, tm):
    T = x1.shape[0]

    def body(x_ref, a_ref, w_ref, after_ref, o_ref):
        o_ref[...] = x_ref[...] + _dot_nn(a_ref[...], w_ref[...])

    row = lambda w: pl.BlockSpec((tm, w), lambda i: (i, 0))
    return pl.pallas_call(
        body, name=f"ffn_down_{layer}", grid=(T // tm,),
        in_specs=[row(D_MODEL), row(D_FF), _wspec(D_FF, D_MODEL, layer), _full((1, 1))],
        out_specs=row(D_MODEL), out_shape=jax.ShapeDtypeStruct((T, D_MODEL), f32),
        compiler_params=_params(("parallel",)),
    )(x1, act, w_down, after)


def _ffn_down_loss(x1, act, w_down, target, layer, tm):
    T = x1.shape[0]

    def body(x_ref, a_ref, w_ref, t_ref, dy_ref, dyb_ref, loss_ref):
        @pl.when(pl.program_id(0) == 0)
        def _():
            loss_ref[...] = jnp.zeros_like(loss_ref)

        diff = x_ref[...] + _dot_nn(a_ref[...], w_ref[...]) - t_ref[...]
        loss_ref[...] += 0.5 * jnp.sum(jnp.mean(diff * diff, axis=-1, keepdims=True), axis=0, keepdims=True)
        dy = diff * (1.0 / D_MODEL)
        dy_ref[...] = dy
        dyb_ref[...] = dy.astype(bf16)

    row = lambda w: pl.BlockSpec((tm, w), lambda i: (i, 0))
    return pl.pallas_call(
        body, name=f"ffn_down_loss_{layer}", grid=(T // tm,),
        in_specs=[row(D_MODEL), row(D_FF), _wspec(D_FF, D_MODEL, layer), row(D_MODEL)],
        out_specs=[row(D_MODEL), row(D_MODEL), _full((8, 128))],
        out_shape=[jax.ShapeDtypeStruct((T, D_MODEL), f32), jax.ShapeDtypeStruct((T, D_MODEL), bf16),
                   jax.ShapeDtypeStruct((8, 128), f32)],
        compiler_params=_params(("arbitrary",)),
    )(x1, act, w_down, target)


def _ffn_bwd(dx2b, z, conv_w, conv_b, w_down, layer, seq, tm):
    T = z.shape[0]
    nt = T // tm
    tps = seq // tm

    perm, perm_t = _tile_permutation(tm)

    def body(dx_ref, z_ref, zh_ref, cw_ref, cb_ref, wd_ref, p_ref, pt_ref, dz_ref, dconv_ref, carry_ref,
             zg_buf, zv_buf, gg_buf, gv_buf, dact_buf, dzp_buf):
        i = pl.program_id(0)
        pos = (nt - 1 - i) % tps

        @pl.when(i == 0)
        def _():
            dconv_ref[...] = jnp.zeros_like(dconv_ref)

        @pl.when(pos == tps - 1)
        def _():
            carry_ref[...] = jnp.zeros_like(carry_ref)

        dxp = _dot_nn(p_ref[...], dx_ref[...]).astype(bf16)
        halo_on = (pos > 0).astype(f32)
        for cc in range(D_FF // FF_CHUNK):
            cols_g = slice(cc * FF_CHUNK, (cc + 1) * FF_CHUNK)
            cols_v = slice(D_FF + cc * FF_CHUNK, D_FF + (cc + 1) * FF_CHUNK)
            for buf, cols in ((zg_buf, cols_g), (zv_buf, cols_v)):
                _stage_taps_before(buf, z_ref[:, cols].astype(f32), zh_ref[:, cols].astype(f32) * halo_on, tm)
            dact_buf[...] = _dot_nt(dxp, wd_ref[cols_g, :])
            coef = [jnp.broadcast_to(v, (8, FF_CHUNK)) for cols in (cols_g, cols_v)
                    for v in (cb_ref[:, cols], cw_ref[0:1, cols], cw_ref[1:2, cols], cw_ref[2:3, cols])]

            def first_pass(j, sums, coef=coef):
                r = pl.multiple_of(j * 8, 8)
                rows = pl.ds(r, 8)
                zcg = _conv_rows(zg_buf, r, 8, coef[0:4])
                zcv = _conv_rows(zv_buf, r, 8, coef[4:8])
                sg = _sigmoid(zcg)
                silu = zcg * sg
                d_act = dact_buf[rows, :]
                dg = d_act * zcv * sg * (1.0 + zcg * (1.0 - sg))
                dv = d_act * silu
                gg_buf[rows, :] = dg
                gv_buf[rows, :] = dv
                out = []
                for k, (g, buf) in enumerate(((dg, zg_buf), (dv, zv_buf))):
                    out += [sums[4 * k] + g * buf[rows, :],
                            sums[4 * k + 1] + g * buf[pl.ds(pl.multiple_of(r + 8, 8), 8), :],
                            sums[4 * k + 2] + g * buf[pl.ds(pl.multiple_of(r + 16, 8), 8), :],
                            sums[4 * k + 3] + g]
                return tuple(out)

            sums = lax.fori_loop(0, tm // 8, first_pass, tuple(jnp.zeros((8, FF_CHUNK), f32) for _ in range(8)),
                                 unroll=True)
            for k, cols in enumerate((cols_g, cols_v)):
                for tap in range(4):
                    dconv_ref[tap:tap + 1, cols] += jnp.sum(sums[4 * k + tap], axis=0, keepdims=True)
            for buf, cols in ((gg_buf, cols_g), (gv_buf, cols_v)):
                _stage_taps_after(buf, carry_ref[:, cols], tm)
                carry_ref[:, cols] = buf[0:16, :]
                w0, w1, w2 = (jnp.broadcast_to(cw_ref[k:k + 1, cols], (16, FF_CHUNK)) for k in range(3))

                def second_pass(j, carry, buf=buf, w0=w0, w1=w1, w2=w2):
                    r = pl.multiple_of(j * 16, 16)
                    dzp_buf[pl.ds(r, 16), :] = (w2 * buf[pl.ds(r, 16), :] + w1 * buf[pl.ds(pl.multiple_of(r + 8, 8), 16), :]
                                                + w0 * buf[pl.ds(pl.multiple_of(r + 16, 16), 16), :]).astype(bf16)
                    return carry

                lax.fori_loop(0, tm // 16, second_pass, 0, unroll=True)
                dz_ref[:, cols] = _dot_nn(pt_ref[...], dzp_buf[...]).astype(bf16)

    rev = lambda w: pl.BlockSpec((tm, w), lambda i: (nt - 1 - i, 0))
    return pl.pallas_call(
        body, name=f"ffn_bwd_{layer}", grid=(nt,),
        in_specs=[rev(D_MODEL), rev(2 * D_FF),
                  pl.BlockSpec((16, 2 * D_FF), lambda i: (jnp.maximum((nt - 1 - i) * (tm // 16) - 1, 0), 0)),
                  _full((3, 2 * D_FF)), _full((1, 2 * D_FF)), _wspec(D_FF, D_MODEL, layer),
                  _full((tm, tm)), _full((tm, tm))],
        out_specs=[rev(2 * D_FF), _full((8, 2 * D_FF))],
        out_shape=[jax.ShapeDtypeStruct((T, 2 * D_FF), bf16), jax.ShapeDtypeStruct((8, 2 * D_FF), f32)],
        scratch_shapes=[pltpu.VMEM((16, 2 * D_FF), f32)] + [pltpu.VMEM((tm + 16, FF_CHUNK), f32)] * 4
        + [pltpu.VMEM((tm, FF_CHUNK), f32), pltpu.VMEM((tm, FF_CHUNK), bf16)],
        compiler_params=_params(("arbitrary",)),
    )(dx2b, z, z, conv_w, conv_b, w_down, perm, perm_t)


def _norm_bwd(dys, w, layer, x, gain, dres, tm, name):
    T = dys[0].shape[0]
    widths = [d.shape[1] for d in dys]
    K = sum(widths)
    n = len(dys)

    def body(*refs):
        dy_refs = refs[:n]
        w_ref, x_ref, g_ref, dres_ref, dx_ref, dxb_ref, dg_ref = refs[n:]

        @pl.when(pl.program_id(0) == 0)
        def _():
            dg_ref[...] = jnp.zeros_like(dg_ref)

        dh, lo = None, 0
        for dy_ref, wd in zip(dy_refs, widths):
            part = _dot_nn(dy_ref[...], w_ref[lo:lo + wd, :])
            dh = part if dh is None else dh + part
            lo += wd
        xf = x_ref[...]
        r = _rms(xf)
        x_hat = xf * r
        dg_ref[...] += jnp.sum(dh * x_hat, axis=0, keepdims=True)
        dxh = dh * g_ref[...]
        dx = dres_ref[...] + r * (dxh - x_hat * jnp.mean(dxh * x_hat, axis=-1, keepdims=True))
        dx_ref[...] = dx
        dxb_ref[...] = dx.astype(bf16)

    row = lambda w_: pl.BlockSpec((tm, w_), lambda i: (i, 0))
    return pl.pallas_call(
        body, name=name, grid=(T // tm,),
        in_specs=[row(wd) for wd in widths] + [_wspec(K, D_MODEL, layer), row(D_MODEL), _full((1, D_MODEL)), row(D_MODEL)],
        out_specs=[row(D_MODEL), row(D_MODEL), _full((1, D_MODEL))],
        out_shape=[jax.ShapeDtypeStruct((T, D_MODEL), f32), jax.ShapeDtypeStruct((T, D_MODEL), bf16),
                   jax.ShapeDtypeStruct((1, D_MODEL), f32)],
        compiler_params=_params(("arbitrary",)),
    )(*dys, w, x, gain, dres)


def _merge_bwd(dx1b, ga, gb, a, b, w_oa_t, w_ob_t, w_out, layer, tm):
    T = dx1b.shape[0]

    def body(dx_ref, ga_ref, gb_ref, a_ref, b_ref, woa_ref, wob_ref, wout_ref,
             da_ref, db_ref, dga_ref, dgb_ref, dya_ref, dys_ref):
        dm = _dot_nt(dx_ref[...], wout_ref[...])
        sa = _sigmoid(ga_ref[...].astype(f32))
        sb = _sigmoid(gb_ref[...].astype(f32))
        da = (dm * sa).astype(bf16)
        db = (dm * sb).astype(bf16)
        da_ref[...] = da
        db_ref[...] = db
        dga_ref[...] = (dm * a_ref[...].astype(f32) * sa * (1.0 - sa)).astype(bf16)
        dgb_ref[...] = (dm * b_ref[...].astype(f32) * sb * (1.0 - sb)).astype(bf16)
        dya_ref[...] = _dot_nn(da, woa_ref[...]).astype(bf16)
        dys_ref[...] = _dot_nn(db, wob_ref[...]).astype(bf16)

    row = lambda w: pl.BlockSpec((tm, w), lambda i: (i, 0))
    return pl.pallas_call(
        body, name=f"merge_bwd_{layer}", grid=(T // tm,),
        in_specs=[row(D_MODEL)] * 5 + [_wspec(D_MODEL, ATT_WIDTH, layer), _wspec(D_MODEL, SGU_WIDTH, layer),
                                       _wspec(D_MODEL, D_MODEL, layer)],
        out_specs=[row(D_MODEL)] * 4 + [row(512)] * 2,
        out_shape=[jax.ShapeDtypeStruct((T, D_MODEL), bf16)] * 4 + [jax.ShapeDtypeStruct((T, 512), bf16)] * 2,
        compiler_params=_params(("parallel",)),
    )(dx1b, ga, gb, a, b, w_oa_t, w_ob_t, w_out)


def _sgu_bwd(dy, su, sv, gain, w_s, b_s, tm):
    T = su.shape[0]
    gd = SGU_WIDTH // SGU_GROUPS

    def body(dy_ref, su_ref, sv_ref, g_ref, w_ref, b_ref, dsu_ref, dsv_ref, dw_ref, db_ref, dg_ref):
        @pl.when(pl.program_id(0) == 0)
        def _():
            dw_ref[...] = jnp.zeros_like(dw_ref)
            db_ref[...] = jnp.zeros_like(db_ref)
            dg_ref[...] = jnp.zeros_like(dg_ref)

        gain_v = g_ref[...]
        for ch in range(tm // BLOCK):
            rows = slice(ch * BLOCK, (ch + 1) * BLOCK)
            su_c = su_ref[rows, :].astype(f32)
            sv_c = sv_ref[rows, :].astype(f32)
            u, rv, v_hat, vn, w_tril, mixed = _sgu_chunk(su_c, sv_c, gain_v, w_ref, b_ref)
            dyc = dy_ref[rows, :].astype(f32)
            dsu_ref[rows, :] = (dyc * mixed * _gelu_grad(su_c)).astype(bf16)
            dmix = dyc * u
            dmix_b = dmix.astype(bf16)
            dvn = []
            for g in range(SGU_GROUPS):
                gs = slice(g * gd, (g + 1) * gd)
                db_ref[g] += jnp.sum(dmix[:, gs], axis=1, keepdims=True)
                dw_ref[g] += _dot_nt(dmix_b[:, gs], vn[:, gs])
                dvn.append(_dot_tn(w_tril[g], dmix_b[:, gs]))
            dvn = jnp.concatenate(dvn, axis=1)
            dg_ref[...] += jnp.sum(dvn * v_hat, axis=0, keepdims=True)
            dxh = dvn * gain_v
            dvg = rv * (dxh - v_hat * jnp.mean(dxh * v_hat, axis=-1, keepdims=True))
            dsv_ref[rows, :] = (dvg * _gelu_grad(sv_c)).astype(bf16)

    row = pl.BlockSpec((tm, SGU_WIDTH), lambda i: (i, 0))
    return pl.pallas_call(
        body, name="sgu_bwd", grid=(T // tm,),
        in_specs=[row, row, row, _full((1, SGU_WIDTH)), _full((SGU_GROUPS, BLOCK, BLOCK)),
                  _full((SGU_GROUPS, BLOCK, 1))],
        out_specs=[row, row, _full((SGU_GROUPS, BLOCK, BLOCK)), _full((SGU_GROUPS, BLOCK, 1)), _full((1, SGU_WIDTH))],
        out_shape=[jax.ShapeDtypeStruct((T, SGU_WIDTH), bf16)] * 2 + [
            jax.ShapeDtypeStruct((SGU_GROUPS, BLOCK, BLOCK), f32), jax.ShapeDtypeStruct((SGU_GROUPS, BLOCK, 1), f32),
            jax.ShapeDtypeStruct((1, SGU_WIDTH), f32)],
        compiler_params=_params(("arbitrary",)),
    )(dy, su, sv, gain, w_s, b_s)


def _attn_bwd(dy, qkv, qg, kg, sinks, n_seq, seq):
    T = n_seq * seq
    nb = seq // BLOCK
    scale = HEAD_DIM ** -0.5
    per = 1
    ng = nb // per

    def body(dy_ref, cur_ref, prev_ref, qg_ref, kg_ref, sink_ref, dqkv_ref, dqg_ref, dkg_ref, dsink_ref,
             carry_k, carry_v):
        b = pl.program_id(0)
        j = pl.program_id(1)

        @pl.when((b == 0) & (j == 0))
        def _():
            dqg_ref[...] = jnp.zeros_like(dqg_ref)
            dkg_ref[...] = jnp.zeros_like(dkg_ref)
            dsink_ref[...] = jnp.zeros_like(dsink_ref)

        @pl.when(j == 0)
        def _():
            carry_k[...] = jnp.zeros_like(carry_k)
            carry_v[...] = jnp.zeros_like(carry_v)

        for sub in reversed(range(per)):
            rows = slice(sub * BLOCK, (sub + 1) * BLOCK)
            prev = prev_ref[...] if sub == 0 else cur_ref[(sub - 1) * BLOCK:sub * BLOCK, 512:768]
            one_block(dy_ref[rows, :].astype(f32), cur_ref[rows, :], prev, (ng - 1 - j) * per + sub,
                      qg_ref[...], kg_ref[...], sink_ref, dqkv_ref.at[rows, :], dqg_ref, dkg_ref, dsink_ref,
                      carry_k, carry_v)

    def one_block(dyf, cur, prev, n, qg_v, kg_v, sink_ref, dqkv_ref, dqg_ref, dkg_ref, dsink_ref, carry_k, carry_v):
        dq_pieces = [None] * (N_KV_HEADS * Q_GROUP)
        dk_pieces, dv_pieces = [], []
        for hk in range(N_KV_HEADS):
            a = _attn_head_group(cur, prev, qg_v, kg_v, sink_ref, n, hk)
            do = jnp.concatenate(
                [dyf[:, (hk * Q_GROUP + g) * HEAD_DIM:(hk * Q_GROUP + g + 1) * HEAD_DIM] for g in range(Q_GROUP)],
                axis=0).astype(bf16)
            p = a["p"]
            dp = _dot_nt(do, a["v"])
            dv_band = _dot_tn(p.astype(bf16), do)
            dsum = jnp.sum(p * dp, axis=-1, keepdims=True)
            ds = (p * (dp - dsum)).astype(bf16)
            dsink_col = -a["p_sink"] * dsum
            for g in range(Q_GROUP):
                head = hk * Q_GROUP + g
                dsink_ref[head:head + 1, :] += jnp.sum(dsink_col[g * BLOCK:(g + 1) * BLOCK], axis=0, keepdims=True)
            dqn = _dot_nn(ds, a["kn"])
            dkn_band = _dot_tn(ds, a["qn"])
            dq_hat_g = dqn * scale
            dqg_ref[...] += jnp.sum(dq_hat_g * a["q_hat"], axis=0, keepdims=True)
            dxh = dq_hat_g * qg_v
            dq = a["rq"] * (dxh - a["q_hat"] * jnp.mean(dxh * a["q_hat"], axis=-1, keepdims=True))
            for g in range(Q_GROUP):
                dq_pieces[hk * Q_GROUP + g] = dq[g * BLOCK:(g + 1) * BLOCK]
            dkn = dkn_band[BLOCK:] + carry_k[hk]
            dv_pieces.append(dv_band[BLOCK:] + carry_v[hk])
            carry_k[hk] = dkn_band[:BLOCK]
            carry_v[hk] = dv_band[:BLOCK]
            k_hat = a["k_hat"][BLOCK:]
            dkg_ref[...] += jnp.sum(dkn * k_hat, axis=0, keepdims=True)
            dxk = dkn * kg_v
            dk_pieces.append(a["rk"][BLOCK:] * (dxk - k_hat * jnp.mean(dxk * k_hat, axis=-1, keepdims=True)))
        dqkv_ref[...] = jnp.concatenate(dq_pieces + dk_pieces + dv_pieces, axis=1).astype(bf16)

    blk = lambda w: pl.BlockSpec((per * BLOCK, w), lambda b, j: (b * ng + ng - 1 - j, 0))
    return pl.pallas_call(
        body, name="attn_bwd", grid=(n_seq, ng),
        in_specs=[blk(ATT_WIDTH), blk(768),
                  pl.BlockSpec((BLOCK, 256), lambda b, j: (b * nb + jnp.maximum((ng - 1 - j) * per - 1, 0), 2)),
                  _full((1, HEAD_DIM)), _full((1, HEAD_DIM)), pl.BlockSpec(memory_space=pltpu.SMEM)],
        out_specs=[blk(768), _full((1, HEAD_DIM)), _full((1, HEAD_DIM)), _full((8, 128))],
        out_shape=[jax.ShapeDtypeStruct((T, 768), bf16), jax.ShapeDtypeStruct((1, HEAD_DIM), f32),
                   jax.ShapeDtypeStruct((1, HEAD_DIM), f32), jax.ShapeDtypeStruct((8, 128), f32)],
        scratch_shapes=[pltpu.VMEM((N_KV_HEADS, BLOCK, HEAD_DIM), f32), pltpu.VMEM((N_KV_HEADS, BLOCK, HEAD_DIM), f32)],
        compiler_params=_params(("arbitrary", "arbitrary")),
    )(dy, qkv, qkv, qg, kg, sinks)


def _weight_grad(a, b, tm, tk, name):
    T, M = a.shape
    N = b.shape[1]
    nk = T // tk

    def body(a_ref, b_ref, o_ref, acc_ref):
        k = pl.program_id(1)

        @pl.when(k == 0)
        def _():
            acc_ref[...] = jnp.zeros_like(acc_ref)

        acc_ref[...] += _dot_tn(a_ref[...], b_ref[...])

        @pl.when(k == nk - 1)
        def _():
            o_ref[...] = acc_ref[...].astype(bf16)

    return pl.pallas_call(
        body, name=name, grid=(M // tm, nk),
        in_specs=[pl.BlockSpec((tk, tm), lambda i, k: (k, i)), pl.BlockSpec((tk, N), lambda i, k: (k, 0))],
        out_specs=pl.BlockSpec((None, tm, N), lambda i, k: (0, i, 0)),
        out_shape=jax.ShapeDtypeStruct((1, M, N), bf16),
        scratch_shapes=[pltpu.VMEM((tm, N), f32)],
        compiler_params=_params(("parallel", "arbitrary")),
    )(a, b)


def _weight_grad_rows(a_list, b, tk, name):
    T, N = b.shape
    widths = [a.shape[1] for a in a_list]
    M = sum(widths)
    nk = T // tk
    n = len(a_list)

    def body(*refs):
        a_refs = refs[:n]
        b_ref, o_ref, acc_ref = refs[n:]
        k = pl.program_id(0)

        @pl.when(k == 0)
        def _():
            acc_ref[...] = jnp.zeros_like(acc_ref)

        lo = 0
        for a_ref, wd in zip(a_refs, widths):
            acc_ref[lo:lo + wd, :] += _dot_tn(a_ref[...], b_ref[...])
            lo += wd

        @pl.when(k == nk - 1)
        def _():
            o_ref[...] = acc_ref[...].astype(bf16)

    return pl.pallas_call(
        body, name=name, grid=(nk,),
        in_specs=[pl.BlockSpec((tk, wd), lambda k: (k, 0)) for wd in widths] + [pl.BlockSpec((tk, N), lambda k: (k, 0))],
        out_specs=pl.BlockSpec((None, M, N), lambda k: (0, 0, 0), pipeline_mode=pl.Buffered(1)),
        out_shape=jax.ShapeDtypeStruct((1, M, N), bf16),
        scratch_shapes=[pltpu.VMEM((M, N), f32)],
        compiler_params=_params(("arbitrary",)),
    )(*a_list, b)


def _place(src, layer, src_slot, n_slots, dst_slot, dtype, name, after=None):
    _, _, rows, cols = src.shape
    slots = jnp.stack([src_slot, dst_slot]).astype(jnp.int32)

    def body(slots_ref, s_ref, *rest):
        rest[-1][...] = s_ref[...].astype(dtype)

    return pl.pallas_call(
        body, name=name,
        grid_spec=pltpu.PrefetchScalarGridSpec(
            num_scalar_prefetch=1, grid=(1,),
            in_specs=[pl.BlockSpec((None, None, rows, cols), lambda i, sl: (layer, sl[0], 0, 0))]
            + ([] if after is None else [ANY]),
            out_specs=pl.BlockSpec((None, rows, cols), lambda i, sl: (sl[1], 0, 0))),
        out_shape=jax.ShapeDtypeStruct((n_slots, rows, cols), dtype),
        compiler_params=_params(("arbitrary",)),
    )(slots, src, *([] if after is None else [after]))


HBM = pl.BlockSpec(memory_space=pltpu.HBM)
SEM = pl.BlockSpec(memory_space=pltpu.SEMAPHORE)
DATAFLOW = pltpu.SideEffectType.DATAFLOW_SIDE_EFFECTING


def _other_chips(x, y):
    return [(1 - x, y), (x, 1 - y), (1 - x, 1 - y)]


def _split_start(groups, name):
    nb = [len(bufs) for bufs, _ in groups]
    flat = [b for bufs, _ in groups for b in bufs]
    ns = [len(plan(bufs, dry=True)) for bufs, plan in groups]
    ng = len(groups)

    def body(*refs):
        n_in = len(flat)
        sems = refs[n_in:n_in + 2 * ng]
        thru = refs[n_in + 2 * ng:2 * n_in + 2 * ng]
        token = refs[2 * n_in + 2 * ng]
        off = 0
        for g, (bufs, plan) in enumerate(groups):
            mine = thru[off:off + nb[g]]
            off += nb[g]
            for k, (src, dst, to) in enumerate(plan(mine)):
                pltpu.make_async_remote_copy(
                    src_ref=src, dst_ref=dst, send_sem=sems[2 * g].at[k], recv_sem=sems[2 * g + 1].at[k],
                    device_id=to, device_id_type=MESH).start()
        token[...] = jnp.zeros_like(token)

    out_shape = []
    for n in ns:
        out_shape += [pltpu.SemaphoreType.DMA((n,)), pltpu.SemaphoreType.DMA((n,))]
    out_shape += [pltpu.HBM(b.shape, b.dtype) for b in flat]
    out_shape.append(jax.ShapeDtypeStruct((8, 128), f32))
    res = pl.pallas_call(
        body, name=name, out_shape=tuple(out_shape),
        in_specs=[HBM] * len(flat),
        out_specs=tuple([SEM] * (2 * ng) + [HBM] * len(flat) + [pl.BlockSpec(memory_space=pltpu.VMEM)]),
        input_output_aliases={i: 2 * ng + i for i in range(len(flat))},
        compiler_params=pltpu.CompilerParams(has_side_effects=DATAFLOW),
    )(*[pltpu.with_memory_space_constraint(b, pltpu.HBM) for b in flat])
    out, off = [], 2 * ng
    for g in range(ng):
        out.append((res[2 * g], res[2 * g + 1], list(res[off:off + nb[g]])))
        off += nb[g]
    return out, res[-1]


def _split_wait(bufs, send, recv, plan, after, name):
    nb = len(bufs)

    def body(*refs):
        thru = refs[:nb]
        send_ref, recv_ref = refs[nb], refs[nb + 1]
        for k, (src, dst, to) in enumerate(plan(thru)):
            cp = pltpu.make_async_remote_copy(
                src_ref=src, dst_ref=dst, send_sem=send_ref.at[k], recv_sem=recv_ref.at[k],
                device_id=to, device_id_type=MESH)
            cp.wait_send()
            cp.wait_recv()

    res = pl.pallas_call(
        body, name=name, out_shape=tuple(pltpu.HBM(b.shape, b.dtype) for b in bufs),
        in_specs=[HBM] * nb + [SEM, SEM, ANY], out_specs=tuple([HBM] * nb),
        input_output_aliases={i: i for i in range(nb)},
        compiler_params=pltpu.CompilerParams(has_side_effects=DATAFLOW),
    )(*bufs, send, recv, after)
    return list(res)


def _gather_plan(hrs, n_direct=0):
    def plan(refs, dry=False):
        if dry:
            return [None] * (4 * len(hrs) + 3 * n_direct)
        x, y, c = _mesh_pos()
        me = 4 * x + 2 * y + c
        out = []
        for i in range(n_direct):
            src, land = refs[len(hrs) + 2 * i], refs[len(hrs) + 2 * i + 1]
            out += [(src, land.at[2 * x + y], (*chip, c)) for chip in _other_chips(x, y)]
        for ref, hr in zip(refs, hrs):
            rows = ref.at[pl.ds(pl.multiple_of(me * hr, 16), hr), :]
            out.append((rows, rows, (x, y, 1 - c)))
            out += [(rows, rows, (*chip, c)) for chip in _other_chips(x, y)]
        return out
    return plan


def _pass_plan(hrs):
    def plan(refs, dry=False):
        if dry:
            return [None] * (3 * len(hrs))
        x, y, c = _mesh_pos()
        out = []
        for ref, hr in zip(refs, hrs):
            for chip in _other_chips(x, y):
                rows = ref.at[pl.ds(pl.multiple_of((4 * chip[0] + 2 * chip[1] + c) * hr, 16), hr), :]
                out.append((rows, rows, (x, y, 1 - c)))
        return out
    return plan


def _pair_plan(hrs):
    n = len(hrs)

    def plan(refs, dry=False):
        if dry:
            return [None] * (N_CHIP * n)
        x, y, c = _mesh_pos()
        out = []
        for r in range(n):
            for j in range(N_CHIP):
                start = pl.multiple_of((2 * j + 1 - c) * hrs[r], 16)
                out.append((refs[r].at[0, pl.ds(start, hrs[r]), :], refs[n + r].at[0, j], (x, y, 1 - c)))
        return out
    return plan


def _all_to_all_plan(n):
    def plan(refs, dry=False):
        if dry:
            return [None] * (7 * n)
        x, y, c = _mesh_pos()
        out = []
        for ref in refs:
            mine = ref.at[4 * x + 2 * y + c]
            for fx in range(2):
                for fy in range(2):
                    for fc in range(2):
                        if fx or fy or fc:
                            out.append((mine, mine, (1 - x if fx else x, 1 - y if fy else y, 1 - c if fc else c)))
        return out
    return plan


def _pass_to_sibling(bufs, hrs, name):
    nb = len(bufs)

    def body(*refs):
        out = refs[nb:2 * nb]
        send, recv = refs[2 * nb:]
        x, y, c = _mesh_pos()
        chips = _other_chips(x, y)
        started = []
        for i in range(nb):
            for j, chip in enumerate(chips):
                rows = out[i].at[pl.ds(pl.multiple_of((4 * chip[0] + 2 * chip[1] + c) * hrs[i], 16), hrs[i]), :]
                cp = pltpu.make_async_remote_copy(
                    src_ref=rows, dst_ref=rows, send_sem=send.at[3 * i + j], recv_sem=recv.at[3 * i + j],
                    device_id=(x, y, 1 - c), device_id_type=MESH)
                cp.start()
                started.append(cp)
        for i in range(nb):
            for j, chip in enumerate(chips):
                rows = out[i].at[pl.ds(pl.multiple_of((4 * chip[0] + 2 * chip[1] + 1 - c) * hrs[i], 16), hrs[i]), :]
                pltpu.make_async_remote_copy(
                    src_ref=rows, dst_ref=rows, send_sem=send.at[3 * i + j], recv_sem=recv.at[3 * i + j],
                    device_id=(x, y, 1 - c), device_id_type=MESH).wait_recv()
        for cp in started:
            cp.wait_send()

    return list(pl.pallas_call(
        body, name=name, in_specs=[ANY] * nb, out_specs=[ANY] * nb,
        out_shape=[jax.ShapeDtypeStruct(b.shape, b.dtype) for b in bufs],
        input_output_aliases={i: i for i in range(nb)},
        scratch_shapes=[pltpu.SemaphoreType.DMA((3 * nb,)), pltpu.SemaphoreType.DMA((3 * nb,))],
        compiler_params=pltpu.CompilerParams(has_side_effects=True),
    )(*bufs))


def _pair_exchange(grads, name):
    nr = len(grads)
    n_l = grads[0].shape[0]
    n_sem = nr * n_l * N_CHIP

    def body(*refs):
        src = refs[:nr]
        out = refs[nr:2 * nr]
        send, recv = refs[2 * nr:]
        x, y, c = _mesh_pos()
        copies = []
        for r in range(nr):
            hr = grads[r].shape[1] // N_DEV
            for layer in range(n_l):
                for j in range(N_CHIP):
                    idx = (r * n_l + layer) * N_CHIP + j
                    start = pl.multiple_of((2 * j + 1 - c) * hr, 16)
                    cp = pltpu.make_async_remote_copy(
                        src_ref=src[r].at[layer, pl.ds(start, hr), :], dst_ref=out[r].at[layer, j],
                        send_sem=send.at[idx], recv_sem=recv.at[idx], device_id=(x, y, 1 - c), device_id_type=MESH)
                    cp.start()
                    copies.append(cp)
        for cp in copies:
            cp.wait()

    return pl.pallas_call(
        body, name=name,
        in_specs=[ANY] * nr, out_specs=[ANY] * nr,
        out_shape=[jax.ShapeDtypeStruct((n_l, N_CHIP, g.shape[1] // N_DEV, g.shape[2]), bf16) for g in grads],
        scratch_shapes=[pltpu.SemaphoreType.DMA((n_sem,)), pltpu.SemaphoreType.DMA((n_sem,))],
        compiler_params=pltpu.CompilerParams(has_side_effects=True),
    )(*grads)


def _pair_sum(grad, other, core, chip, name):
    n_l, rows, cols = grad.shape
    hr = rows // N_DEV
    g5 = grad.reshape(n_l, N_CHIP, 2, hr, cols)
    where = jnp.stack([core, chip]).astype(jnp.int32)

    def body(where_ref, g_ref, o_ref, s_ref, mine_ref):
        s_ref[...] = (g_ref[...].astype(f32) + o_ref[...].astype(f32)).astype(bf16)
        mine_ref[...] = s_ref[where_ref[1]]

    return pl.pallas_call(
        body, name=name,
        grid_spec=pltpu.PrefetchScalarGridSpec(
            num_scalar_prefetch=1, grid=(n_l,),
            in_specs=[pl.BlockSpec((None, N_CHIP, None, hr, cols), lambda l, w: (l, 0, w[0], 0, 0)),
                      pl.BlockSpec((None, N_CHIP, hr, cols), lambda l, w: (l, 0, 0, 0))],
            out_specs=[pl.BlockSpec((None, N_CHIP, hr, cols), lambda l, w: (l, 0, 0, 0)),
                       pl.BlockSpec((None, None, hr, cols), lambda l, w: (l, w[1], 0, 0))]),
        out_shape=[jax.ShapeDtypeStruct((n_l, N_CHIP, hr, cols), bf16)] * 2,
        compiler_params=_params(("arbitrary",)),
    )(where, g5, other)


def _chip_plan(nr, n_l):
    def plan(refs, dry=False):
        if dry:
            return [None] * (nr * n_l * 3)
        x, y, c = _mesh_pos()
        out = []
        for r in range(nr):
            for layer in range(n_l):
                for chip in _other_chips(x, y):
                    out.append((refs[r].at[layer, 2 * chip[0] + chip[1]], refs[nr + r].at[layer, 2 * x + y], (*chip, c)))
        return out
    return plan


def _chip_sum(parts, core, name):
    _, _, hr, cols = parts[0].shape

    def body(core_ref, p0_ref, p1_ref, o_ref):
        def total(p_ref):
            acc = p_ref[0].astype(f32) + p_ref[1].astype(f32)
            acc = acc + p_ref[2].astype(f32)
            return acc + p_ref[3].astype(f32)

        @pl.when(pl.program_id(0) == 0)
        def _():
            o_ref[...] = total(p0_ref)

        @pl.when(pl.program_id(0) == 1)
        def _():
            o_ref[...] = total(p1_ref)

    spec = pl.BlockSpec((None, N_CHIP, hr, cols), lambda l, cr: (0, 0, 0, 0))
    return pl.pallas_call(
        body, name=name,
        grid_spec=pltpu.PrefetchScalarGridSpec(
            num_scalar_prefetch=1, grid=(2,), in_specs=[spec, spec],
            out_specs=pl.BlockSpec((None, None, hr, cols), lambda l, cr: (l, cr[0], 0, 0))),
        out_shape=jax.ShapeDtypeStruct((2, 2, hr, cols), f32),
        compiler_params=_params(("arbitrary",)),
    )(core, parts[0], parts[1])


def _share_plan(n):
    def plan(refs, dry=False):
        if dry:
            return [None] * (2 * n)
        x, y, c = _mesh_pos()
        return [(ref.at[layer, c], ref.at[layer, c], (x, y, 1 - c)) for ref in refs for layer in range(2)]
    return plan


def _sum_small(parts, name):
    n, rows, cols = parts.shape

    def body(p_ref, o_ref):
        acc = p_ref[0].astype(f32)
        for d in range(1, n):
            acc = acc + p_ref[d].astype(f32)
        o_ref[...] = acc

    return pl.pallas_call(
        body, name=name, grid=(rows // 16,),
        in_specs=[pl.BlockSpec((n, 16, cols), lambda i: (0, i, 0))], out_specs=pl.BlockSpec((16, cols), lambda i: (i, 0)),
        out_shape=jax.ShapeDtypeStruct((rows, cols), f32),
        compiler_params=_params(("parallel",)),
    )(parts)


def _adamw(w, g, m, v, name):
    n_l, rows, cols = w.shape
    budget = 42 * 1024 * 1024
    tr = next(rows // d for d in range(1, rows + 1)
              if rows % d == 0 and (rows // d) % 8 == 0 and (rows // d) * cols * 4 * 14 <= budget)

    def body(w_ref, g_ref, m_ref, v_ref, d_ref, nm_ref, nv_ref):
        gg = g_ref[...]
        nm = ADAM_B1 * m_ref[...] + (1.0 - ADAM_B1) * gg
        nv = ADAM_B2 * v_ref[...] + (1.0 - ADAM_B2) * (gg * gg)
        m_hat = nm / (1.0 - ADAM_B1 ** ADAM_STEP)
        v_hat = nv / (1.0 - ADAM_B2 ** ADAM_STEP)
        d_ref[...] = -ADAM_LR * (m_hat / (jnp.sqrt(v_hat) + ADAM_EPS) + ADAM_WD * w_ref[...])
        nm_ref[...] = nm
        nv_ref[...] = nv

    blk = pl.BlockSpec((None, tr, cols), lambda l, i: (l, i, 0))
    return pl.pallas_call(
        body, name=name, grid=(n_l, rows // tr),
        in_specs=[blk] * 4, out_specs=[blk] * 3, out_shape=[jax.ShapeDtypeStruct((n_l, rows, cols), f32)] * 3,
        compiler_params=_params(("parallel", "parallel")),
    )(w, g, m, v)


SMALL = ("mix_norm", "q_norm", "k_norm", "sinks", "sgu_norm", "w_s", "b_s", "ffn_norm", "conv_b", "conv_w")


def _pack_small(arrs):
    flat = jnp.concatenate([a.reshape(-1) for a in arrs])
    pad = (-flat.shape[0]) % (16 * 1024)
    return jnp.pad(flat, (0, pad)).reshape(-1, 1024)


def _unpack_small(pack, shapes):
    flat = pack.reshape(-1)
    out, off = [], 0
    for s in shapes:
        n = int(np.prod(s))
        out.append(flat[off:off + n].reshape(s))
        off += n
    return out


def kernel(x, mix_norm, w_in, q_norm, k_norm, sinks, sgu_norm, w_s, b_s, w_oa, w_ob, w_out, ffn_norm, w_up, conv_w, conv_b, w_down, loss_target, m_mix_norm, m_w_in, m_q_norm, m_k_norm, m_sinks, m_sgu_norm, m_w_s, m_b_s, m_w_oa, m_w_ob, m_w_out, m_ffn_norm, m_w_up, m_conv_w, m_conv_b, m_w_down, v_mix_norm, v_w_in, v_q_norm, v_k_norm, v_sinks, v_sgu_norm, v_w_s, v_b_s, v_w_oa, v_w_ob, v_w_out, v_ffn_norm, v_w_up, v_conv_w, v_conv_b, v_w_down):
    weights = dict(mix_norm=mix_norm, w_in=w_in, q_norm=q_norm, k_norm=k_norm, sinks=sinks, sgu_norm=sgu_norm,
                   w_s=w_s, b_s=b_s, w_oa=w_oa, w_ob=w_ob, w_out=w_out, ffn_norm=ffn_norm, w_up=w_up,
                   conv_w=conv_w, conv_b=conv_b, w_down=w_down)
    mom_m = dict(mix_norm=m_mix_norm, w_in=m_w_in, q_norm=m_q_norm, k_norm=m_k_norm, sinks=m_sinks,
                 sgu_norm=m_sgu_norm, w_s=m_w_s, b_s=m_b_s, w_oa=m_w_oa, w_ob=m_w_ob, w_out=m_w_out,
                 ffn_norm=m_ffn_norm, w_up=m_w_up, conv_w=m_conv_w, conv_b=m_conv_b, w_down=m_w_down)
    mom_v = dict(mix_norm=v_mix_norm, w_in=v_w_in, q_norm=v_q_norm, k_norm=v_k_norm, sinks=v_sinks,
                 sgu_norm=v_sgu_norm, w_s=v_w_s, b_s=v_b_s, w_oa=v_w_oa, w_ob=v_w_ob, w_out=v_w_out,
                 ffn_norm=v_ffn_norm, w_up=v_w_up, conv_w=v_conv_w, conv_b=v_conv_b, w_down=v_w_down)
    n_seq, seq, _ = x.shape
    T = n_seq * seq
    core = lax.axis_index("c")
    chip = 2 * lax.axis_index("x") + lax.axis_index("y")
    tm = min(512, seq)
    tm_ff = min(256, seq)
    tm_sgu = min(512, seq)
    tk_dw = min(2048, T)

    me = 2 * chip + core
    names = [r[0] for r in REGIONS]
    hrs = {name: rows // N_DEV for name, rows, _, _ in REGIONS}
    def placed(l, name, after=None):
        _, rows, cols, transposed = next(r for r in REGIONS if r[0] == name)
        shard = (jnp.swapaxes(weights[name], 1, 2) if transposed else weights[name]).reshape(2, 2, hrs[name], cols)
        return _place(shard, l, core, N_DEV, me, bf16, f"place_{name}_{l}", after).reshape(rows, cols)

    group_keys = [[(0, "w_in")], [(0, n) for n in names[1:]],
                  [(1, n) for n in ("w_in", "w_oa", "w_ob", "w_out")], [(1, "w_up"), (1, "w_down")]]
    n_direct = [1, 0, 0, 0]
    plans = [_gather_plan([hrs[n] for _, n in keys], nd) for keys, nd in zip(group_keys, n_direct)]
    first_bufs = [placed(0, "w_in"), conv_w, jnp.zeros((N_CHIP,) + conv_w.shape, f32)]
    started, tok = _split_start([(first_bufs, plans[0])], "gather_start_0")
    first_start_done = jnp.broadcast_to(tok[0:1, 0:1], (512, D_MODEL))
    rest_bufs = [[placed(l, n, first_start_done) for l, n in keys] for keys in group_keys[1:]]
    more, tok = _split_start(list(zip(rest_bufs, plans[1:])), "gather_start_1")
    started += more
    second_start_done = jnp.broadcast_to(tok[0:1, 0:1], (512, D_MODEL))
    gathered = [{}, {}]

    def arrived(g, after):
        send, recv, bufs = started[g]
        hr_list = [hrs[n] for _, n in group_keys[g]]
        bufs = _split_wait(bufs, send, recv, _gather_plan(hr_list, n_direct[g]), after, f"gather_wait_{g}")
        return bufs[:len(hr_list)], bufs[len(hr_list):]

    def start_pass(g, bufs):
        (res,), token = _split_start([(bufs, _pass_plan([hrs[n] for _, n in group_keys[g]]))], f"pass_start_{g}")
        return res, token[0:1, 0:1]

    def finish_pass(g, res, after):
        send, recv, bufs = res
        use(g, _split_wait(bufs, send, recv, _pass_plan([hrs[n] for _, n in group_keys[g]]), after, f"pass_wait_{g}"))

    def use(g, bufs):
        for (l, n), b in zip(group_keys[g], bufs):
            gathered[l][n] = b

    xs = x.reshape(T, D_MODEL)
    bufs, (_, conv_w_land) = arrived(0, second_start_done)
    use(0, _pass_to_sibling(bufs, [hrs["w_in"]], "gather_pass_0"))
    conv_w_all = lax.dynamic_update_slice(conv_w_land, conv_w[None], (chip, 0, 0, 0))
    conv_w_full = jnp.concatenate([conv_w_all[j] for j in range(N_CHIP)], axis=-1)
    saved = []
    cur = xs
    for l in range(2):
        wl = gathered[l]
        b_col = b_s[l].reshape(SGU_GROUPS, BLOCK, 1)
        qkv, su, sv, ga, gb, h = _in_proj(cur, mix_norm[l][None], wl["w_in"], l, tm)
        y_att = _attn_fwd(qkv, q_norm[l][None], k_norm[l][None], sinks[l], n_seq, seq)
        later = 1 if l == 0 else 3
        passing, tok = start_pass(later, arrived(later, y_att)[0])
        y_sgu = _sgu_fwd(su, sv, sgu_norm[l][None] + tok, w_s[l], b_col, tm_sgu)
        finish_pass(later, passing, y_sgu)
        x1, merged, a_o, b_o = _merge_fwd(cur, y_att, y_sgu, ga, gb, wl["w_oa"], wl["w_ob"], wl["w_out"], l, tm)
        h2, z, act = _ffn_up(x1, ffn_norm[l][None], wl["w_up"], conv_w_full[l], conv_b[l][None], l, seq, tm_ff)
        saved.append(dict(x=cur, qkv=qkv, su=su, sv=sv, ga=ga, gb=gb, h=h, y_att=y_att, y_sgu=y_sgu, x1=x1,
                          merged=merged, a=a_o, b=b_o, h2=h2, z=z, act=act, b_col=b_col))
        if l == 0:
            pass_2, tok = start_pass(2, arrived(2, act)[0])
            cur = _ffn_down(x1, act, wl["w_down"], tok, l, tm)
            finish_pass(2, pass_2, cur)
        else:
            dy, dyb, loss_part = _ffn_down_loss(x1, act, wl["w_down"], loss_target.reshape(T, D_MODEL), l, tm)

    core_arr = core.astype(jnp.int32).reshape(1)
    big = [{}, {}]
    small = {name: [None, None] for name in SMALL}

    def start_pairs(l, keys, tag):
        gl = [big[l][n] for n in keys]
        land = [lax.empty((1, N_CHIP, hrs[n], g.shape[2]), bf16) for n, g in zip(keys, gl)]
        (res,), token = _split_start([(gl + land, _pair_plan([hrs[n] for n in keys]))], f"pair_start_{tag}")
        return (l, keys, res, tag), token[0:1, 0:1]

    def pairs_to_chips(state, after):
        l, keys, (send, recv, bufs), tag = state
        bufs = _split_wait(bufs, send, recv, _pair_plan([hrs[n] for n in keys]), after, f"pair_wait_{tag}")
        return sums_to_chips(l, keys, bufs[:len(keys)], bufs[len(keys):], tag)

    def sums_to_chips(l, keys, gl, from_sibling, tag):
        pairs = [_pair_sum(g, o, core, chip, f"pair_sum_{n}_{l}") for g, o, n in zip(gl, from_sibling, keys)]
        bufs = [p[0] for p in pairs] + [p[1] for p in pairs]
        (res,), token = _split_start([(bufs, _chip_plan(len(keys), 1))], f"chip_start_{tag}")
        return (l, keys, res, tag), token[0:1, 0:1]

    def start_reduce(l, keys, tag):
        gl = [big[l][n] for n in keys]
        return sums_to_chips(l, keys, gl, _pair_exchange(gl, f"pair_exchange_{tag}"), tag)

    landed = {}

    def finish_reduce(state, after):
        l, keys, (send, recv, bufs), tag = state
        bufs = _split_wait(bufs, send, recv, _chip_plan(len(keys), 1), after, f"chip_wait_{tag}")
        for n, p in zip(keys, bufs[len(keys):]):
            landed[(l, n)] = p

    rest = [n for n in SMALL if n != "w_s"]
    rest_shapes = [weights[n].shape[1:] if n != "conv_w" else (3, 2 * D_FF) for n in rest]
    zero = jnp.zeros((), jnp.int32)

    def start_small(l):
        extra = loss_part[0, 0:1] if l == 1 else jnp.zeros((1,), f32)
        packs = [(_pack_small([small[n][l] for n in rest] + [extra]), f32, "small"),
                 (small["w_s"][l].reshape(-1, 1024), bf16, "w_s")]
        bufs = [_place(p[None, None], 0, zero, N_DEV, me, dt, f"place_{tag}_{l}") for p, dt, tag in packs]
        (res,), token = _split_start([(bufs, _all_to_all_plan(2))], f"small_start_{l}")
        return res, token[0:1, 0:1]

    def finish_small(l, res, after):
        send, recv, bufs = res
        bufs = _split_wait(bufs, send, recv, _all_to_all_plan(2), after, f"small_wait_{l}")
        out = dict(zip(rest + ["loss"], _unpack_small(_sum_small(bufs[0], f"sum_small_{l}"), rest_shapes + [(1,)])))
        out["w_s"] = _sum_small(bufs[1], f"sum_w_s_{l}").reshape(w_s.shape[1:])
        return out

    pending = []
    after_start = jnp.zeros((1, 1), f32)
    for l in (1, 0):
        s = saved[l]
        wl = gathered[l]
        dz, dconv = _ffn_bwd(dyb, s["z"], conv_w_full[l], conv_b[l][None] + after_start, wl["w_down"], l, seq, tm_ff)
        big[l]["w_down"] = _weight_grad(s["act"], dyb, 1408, tk_dw, f"dw_down_{l}")
        big[l]["w_up"] = _weight_grad(dz, s["h2"], 1408, tk_dw, f"dw_up_{l}")
        ffn_gain, sgu_gain, q_gain = ffn_norm[l][None], sgu_norm[l][None], q_norm[l][None]
        if l == 0:
            pairs_a, tok = start_pairs(0, ["w_down", "w_up"], "0a")
            ffn_gain = ffn_gain + tok
        dx1, dx1b, d_ffn = _norm_bwd([dz], wl["w_up"], l, s["x1"], ffn_gain, dy, tm, f"ffn_norm_bwd_{l}")
        if l == 0:
            state, tok = pairs_to_chips(pairs_a, dx1b)
            pending.append(state)
            sgu_gain = sgu_gain + tok
        small["conv_w"][l] = dconv[0:3]
        small["conv_b"][l] = dconv[3]
        small["ffn_norm"][l] = d_ffn[0]
        da, db, dga, dgb, dya, dys = _merge_bwd(dx1b, s["ga"], s["gb"], s["a"], s["b"],
                                                wl["w_oa"], wl["w_ob"], wl["w_out"], l, tm)
        big[l]["w_out"] = _weight_grad(s["merged"], dx1b, 1024, tk_dw, f"dw_out_{l}")
        big[l]["w_oa"] = _weight_grad(da, s["y_att"], 1024, tk_dw, f"dw_oa_{l}")
        big[l]["w_ob"] = _weight_grad(db, s["y_sgu"], 1024, tk_dw, f"dw_ob_{l}")
        if l == 0:
            pairs_m, tok = start_pairs(0, ["w_out", "w_oa", "w_ob"], "0m")
            sgu_gain = sgu_gain + tok
        dsu, dsv, d_ws, d_bs, d_sgu = _sgu_bwd(dys, s["su"], s["sv"], sgu_gain, w_s[l], s["b_col"], tm_sgu)
        if l == 0:
            state, tok = pairs_to_chips(pairs_m, dsv)
            pending.append(state)
            q_gain = q_gain + tok
        causal = np.tril(np.ones((BLOCK, BLOCK), bool))
        small["w_s"][l] = jnp.where(causal[None], d_ws, 0.0)
        small["b_s"][l] = d_bs[:, :, 0]
        small["sgu_norm"][l] = d_sgu[0]
        dqkv, d_qg, d_kg, d_sink = _attn_bwd(dya, s["qkv"], q_gain, k_norm[l][None], sinks[l], n_seq, seq)
        small["q_norm"][l] = d_qg[0]
        small["k_norm"][l] = d_kg[0]
        small["sinks"][l] = d_sink[:, 0]
        dproj = [dqkv, dsu, dsv, dga, dgb]
        big[l]["w_in"] = _weight_grad_rows(dproj, s["h"], min(1024, T), f"dw_in_{l}")
        if l == 1:
            pairs_1, tok = start_pairs(1, names, "1")
        else:
            state, tok = start_reduce(0, ["w_in"], "0b")
            pending.append(state)
        dy, dyb, d_mix = _norm_bwd(dproj, wl["w_in"], l, s["x"], mix_norm[l][None] + tok, dx1, tm, f"mix_norm_bwd_{l}")
        small["mix_norm"][l] = d_mix[0]
        if l == 1:
            state, tok = pairs_to_chips(pairs_1, dyb)
            pending.append(state)
            small_1, after_start = start_small(1)
            after_start = after_start + tok
    grad_x = dy.reshape(n_seq, seq, D_MODEL)

    small_0, _ = start_small(0)
    for state in pending:
        finish_reduce(state, dyb)
    share_keys = [["w_in", "w_oa", "w_ob", "w_out"], ["w_up", "w_down"]]
    halves = {n: _chip_sum([landed[(0, n)], landed[(1, n)]], core_arr, f"chip_sum_{n}") for n in names}
    swaps, _ = _split_start([([halves[n] for n in keys], _share_plan(len(keys))) for keys in share_keys], "share_start")
    shared = {}

    def finish_share(k, after):
        send, recv, bufs = swaps[k]
        shared.update(zip(share_keys[k], _split_wait(bufs, send, recv, _share_plan(len(bufs)), after, f"share_wait_{k}")))

    finish_share(0, dyb)
    grad, delta, new_m, new_v = {}, {}, {}, {}
    flip = lambda a: jnp.swapaxes(a, 1, 2)
    for name, rows, cols, transposed in sorted(REGIONS, key=lambda r: r[0] in share_keys[1]):
        if name == share_keys[1][0]:
            finish_share(1, delta[share_keys[0][-1]])
        g = shared[name].reshape(2, rows // N_CHIP, cols)
        if transposed and weights[name].shape[2] % 128:
            d, nm, nv = _adamw(flip(weights[name]), g, flip(mom_m[name]), flip(mom_v[name]), f"adamw_{name}")
            grad[name], delta[name], new_m[name], new_v[name] = flip(g), flip(d), flip(nm), flip(nv)
        else:
            grad[name] = flip(g) if transposed else g
            delta[name], new_m[name], new_v[name] = _adamw(weights[name], grad[name], mom_m[name], mom_v[name],
                                                           f"adamw_{name}")

    per_layer = [finish_small(0, small_0, delta["w_down"]), finish_small(1, small_1, dyb)]
    loss = per_layer[1]["loss"][0]
    grad_small = {n: jnp.stack([per_layer[0][n], per_layer[1][n]]) for n in SMALL}
    cw_cols = conv_w.shape[-1]
    grad_small["conv_w"] = lax.dynamic_slice_in_dim(grad_small["conv_w"], chip * cw_cols, cw_cols, axis=2)

    as_rows = lambda a: a.reshape(2, -1, BLOCK)
    d, nm, nv = _adamw(as_rows(w_s), as_rows(grad_small["w_s"]), as_rows(m_w_s), as_rows(v_w_s), "adamw_w_s")
    grad["w_s"], delta["w_s"], new_m["w_s"], new_v["w_s"] = (
        grad_small["w_s"], d.reshape(w_s.shape), nm.reshape(w_s.shape), nv.reshape(w_s.shape))
    shapes = [weights[n].shape for n in rest]
    d, nm, nv = _adamw(_pack_small([weights[n] for n in rest])[None], _pack_small([grad_small[n] for n in rest])[None],
                       _pack_small([mom_m[n] for n in rest])[None], _pack_small([mom_v[n] for n in rest])[None],
                       "adamw_small")
    for n, dd, mm, vv in zip(rest, _unpack_small(d, shapes), _unpack_small(nm, shapes), _unpack_small(nv, shapes)):
        grad[n], delta[n], new_m[n], new_v[n] = grad_small[n], dd, mm, vv

    order = ["mix_norm", "w_in", "q_norm", "k_norm", "sinks", "sgu_norm", "w_s", "b_s", "w_oa", "w_ob", "w_out",
             "ffn_norm", "w_up", "conv_w", "conv_b", "w_down"]
    return (loss, grad_x, *[grad[n] for n in order], *[delta[n] for n in order],
            *[new_m[n] for n in order], *[new_v[n] for n in order])
```

```python
import numpy as np
import jax
import jax.numpy as jnp
from jax import lax
from jax.experimental import pallas as pl
from jax.experimental.pallas import tpu as pltpu

bf16 = jnp.bfloat16
f32 = jnp.float32

D_MODEL = 1024
ATT_WIDTH = 512
KV_WIDTH = 128
SGU_WIDTH = 512
HEAD_DIM = 64
N_KV_HEADS = 2
Q_GROUP = 4
BLOCK = 128
SGU_GROUPS = 8
IN_WIDTH = 3840
D_FF = 2816
NORM_EPS = 1e-6
NEG_INF = -1e30
N_DEV = 8
N_CHIP = 4

ADAM_LR = 0.001
ADAM_B1 = 0.9
ADAM_B2 = 0.999
ADAM_EPS = 1e-08
ADAM_WD = 0.01
ADAM_STEP = 10

V7X_VMEM_LIMIT = 56 * 1024 * 1024
FF_CHUNK = 2816

REGIONS = (
    ("w_in", IN_WIDTH, D_MODEL, True),
    ("w_oa", D_MODEL, ATT_WIDTH, True),
    ("w_ob", D_MODEL, SGU_WIDTH, True),
    ("w_out", D_MODEL, D_MODEL, False),
    ("w_up", 2 * D_FF, D_MODEL, True),
    ("w_down", D_FF, D_MODEL, False),
)
MESH = pl.DeviceIdType.MESH
ANY = pl.BlockSpec(memory_space=pl.ANY)


def _params(sem=None, **kw):
    return pltpu.CompilerParams(dimension_semantics=sem, vmem_limit_bytes=V7X_VMEM_LIMIT, **kw)


def _wspec(rows, cols, layer=None):
    del layer
    return pl.BlockSpec((rows, cols), lambda *_: (0, 0), pipeline_mode=pl.Buffered(1))


def _full(shape):
    nd = len(shape)
    return pl.BlockSpec(shape, lambda *_: (0,) * nd)


def _dot_nn(a, b):
    return jnp.dot(a, b, preferred_element_type=f32)


def _dot_nt(a, b):
    return lax.dot_general(a, b, (((1,), (1,)), ((), ())), preferred_element_type=f32)


def _dot_tn(a, b):
    return lax.dot_general(a, b, (((0,), (0,)), ((), ())), preferred_element_type=f32)


_GELU_C = float(np.sqrt(2.0 / np.pi))


def _gelu(x):
    return 0.5 * x * (1.0 + jnp.tanh(_GELU_C * (x + 0.044715 * x * x * x)))


def _gelu_grad(x):
    t = jnp.tanh(_GELU_C * (x + 0.044715 * x * x * x))
    du = _GELU_C * (1.0 + 3.0 * 0.044715 * x * x)
    return 0.5 * (1.0 + t) + 0.5 * x * (1.0 - t * t) * du


def _sigmoid(x):
    return 0.5 * jnp.tanh(0.5 * x) + 0.5


def _rms(x):
    return lax.rsqrt(jnp.mean(x * x, axis=-1, keepdims=True) + NORM_EPS)


def _rms_heads(x):
    sq = x * x
    hi = sq.astype(bf16)
    lo = (sq - hi.astype(f32)).astype(bf16)
    ones = jnp.ones((HEAD_DIM, HEAD_DIM), bf16)
    return lax.rsqrt((_dot_nn(hi, ones) + _dot_nn(lo, ones)) * (1.0 / HEAD_DIM) + NORM_EPS)


def _mesh_pos():
    return lax.axis_index("x"), lax.axis_index("y"), lax.axis_index("c")


def _in_proj(x, gain, w_in_t, layer, tm):
    T = x.shape[0]

    def body(x_ref, g_ref, w_ref, qkv_ref, su_ref, sv_ref, ga_ref, gb_ref, h_ref):
        xf = x_ref[...]
        h = (xf * _rms(xf) * g_ref[...]).astype(bf16)
        h_ref[...] = h
        qkv_ref[...] = _dot_nt(h, w_ref[0:768, :])
        su_ref[...] = _dot_nt(h, w_ref[768:1280, :]).astype(bf16)
        sv_ref[...] = _dot_nt(h, w_ref[1280:1792, :]).astype(bf16)
        ga_ref[...] = _dot_nt(h, w_ref[1792:2816, :]).astype(bf16)
        gb_ref[...] = _dot_nt(h, w_ref[2816:3840, :]).astype(bf16)

    row = lambda w: pl.BlockSpec((tm, w), lambda i: (i, 0))
    return pl.pallas_call(
        body, name=f"in_proj_{layer}", grid=(T // tm,),
        in_specs=[row(D_MODEL), _full((1, D_MODEL)), _wspec(IN_WIDTH, D_MODEL, layer)],
        out_specs=[row(768), row(512), row(512), row(1024), row(1024), row(D_MODEL)],
        out_shape=[jax.ShapeDtypeStruct((T, 768), f32), jax.ShapeDtypeStruct((T, 512), bf16),
                   jax.ShapeDtypeStruct((T, 512), bf16), jax.ShapeDtypeStruct((T, 1024), bf16),
                   jax.ShapeDtypeStruct((T, 1024), bf16), jax.ShapeDtypeStruct((T, D_MODEL), bf16)],
        compiler_params=_params(("parallel",)),
    )(x, gain, w_in_t)


def _attn_head_group(cur, prev, qg, kg, sink_ref, n, hk):
    lo = hk * HEAD_DIM
    k_raw = jnp.concatenate([prev[:, lo:lo + HEAD_DIM], cur[:, 512 + lo:512 + lo + HEAD_DIM]], axis=0)
    v_band = jnp.concatenate([prev[:, 128 + lo:128 + lo + HEAD_DIM], cur[:, 640 + lo:640 + lo + HEAD_DIM]], axis=0)
    rk = _rms_heads(k_raw)
    k_hat = k_raw * rk
    kn = (k_hat * kg).astype(bf16)
    q_raw = jnp.concatenate(
        [cur[:, (hk * Q_GROUP + g) * HEAD_DIM:(hk * Q_GROUP + g + 1) * HEAD_DIM] for g in range(Q_GROUP)], axis=0)
    rq = _rms_heads(q_raw)
    q_hat = q_raw * rq
    qn = (q_hat * qg * (HEAD_DIM ** -0.5)).astype(bf16)
    s = _dot_nt(qn, kn)
    rows = lax.broadcasted_iota(jnp.int32, (Q_GROUP * BLOCK, 1), 0)
    g_of_row = rows // BLOCK
    qi = rows - g_of_row * BLOCK
    kj = lax.broadcasted_iota(jnp.int32, (1, 2 * BLOCK), 1)
    dist = qi + BLOCK - kj
    valid = (dist >= 0) & (dist < BLOCK) & ((kj >= BLOCK) | (n > 0))
    slope = jnp.zeros((Q_GROUP * BLOCK, 1), f32)
    sink = jnp.zeros((Q_GROUP * BLOCK, 1), f32)
    for g in range(Q_GROUP):
        head = hk * Q_GROUP + g
        slope = jnp.where(g_of_row == g, float(np.exp2(-8.0 * (head + 1.0) / 8.0)), slope)
        sink = jnp.where(g_of_row == g, sink_ref[head], sink)
    s = jnp.where(valid, s - slope * dist.astype(f32), NEG_INF)
    m = jnp.maximum(jnp.max(s, axis=-1, keepdims=True), sink)
    e = jnp.exp(s - m)
    e_sink = jnp.exp(sink - m)
    inv = 1.0 / (jnp.sum(e, axis=-1, keepdims=True) + e_sink)
    return dict(k_raw=k_raw, rk=rk, k_hat=k_hat, kn=kn, v=v_band.astype(bf16), q_hat=q_hat, rq=rq, qn=qn,
                p=e * inv, p_sink=e_sink * inv)


def _attn_fwd(qkv, qg, kg, sinks, n_seq, seq):
    T = n_seq * seq
    nb = seq // BLOCK

    per = 2 if nb % 2 == 0 else 1

    def body(cur_ref, prev_ref, qg_ref, kg_ref, sink_ref, y_ref):
        for sub in range(per):
            n = pl.program_id(1) * per + sub
            cur = cur_ref[sub * BLOCK:(sub + 1) * BLOCK, :]
            prev = prev_ref[...] if sub == 0 else cur_ref[(sub - 1) * BLOCK:sub * BLOCK, 512:768]
            pieces = [None] * (N_KV_HEADS * Q_GROUP)
            for hk in range(N_KV_HEADS):
                a = _attn_head_group(cur, prev, qg_ref[...], kg_ref[...], sink_ref, n, hk)
                o = _dot_nn(a["p"].astype(bf16), a["v"])
                for g in range(Q_GROUP):
                    pieces[hk * Q_GROUP + g] = o[g * BLOCK:(g + 1) * BLOCK]
            y_ref[sub * BLOCK:(sub + 1) * BLOCK, :] = jnp.concatenate(pieces, axis=1).astype(bf16)

    return pl.pallas_call(
        body, name="attn_fwd", grid=(n_seq, nb // per),
        in_specs=[pl.BlockSpec((per * BLOCK, 768), lambda b, n: (b * (nb // per) + n, 0)),
                  pl.BlockSpec((BLOCK, 256), lambda b, n: (b * nb + jnp.maximum(n * per - 1, 0), 2)),
                  _full((1, HEAD_DIM)), _full((1, HEAD_DIM)),
                  pl.BlockSpec(memory_space=pltpu.SMEM)],
        out_specs=pl.BlockSpec((per * BLOCK, ATT_WIDTH), lambda b, n: (b * (nb // per) + n, 0)),
        out_shape=jax.ShapeDtypeStruct((T, ATT_WIDTH), bf16),
        compiler_params=_params(("parallel", "parallel")),
    )(qkv, qkv, qg, kg, sinks)


def _sgu_chunk(su, sv, gain, w_ref, b_ref):
    u = _gelu(su)
    vg = _gelu(sv)
    rv = _rms(vg)
    v_hat = vg * rv
    vn = (v_hat * gain).astype(bf16)
    causal = (lax.broadcasted_iota(jnp.int32, (BLOCK, BLOCK), 0) >= lax.broadcasted_iota(jnp.int32, (BLOCK, BLOCK), 1))
    w_tril = [jnp.where(causal, w_ref[g], 0.0).astype(bf16) for g in range(SGU_GROUPS)]
    gd = SGU_WIDTH // SGU_GROUPS
    mixed = jnp.concatenate(
        [_dot_nn(w_tril[g], vn[:, g * gd:(g + 1) * gd]) + b_ref[g] for g in range(SGU_GROUPS)], axis=1)
    return u, rv, v_hat, vn, w_tril, mixed


def _sgu_fwd(su, sv, gain, w_s, b_s, tm):
    T = su.shape[0]

    def body(su_ref, sv_ref, g_ref, w_ref, b_ref, y_ref):
        for ch in range(tm // BLOCK):
            rows = slice(ch * BLOCK, (ch + 1) * BLOCK)
            u, _, _, _, _, mixed = _sgu_chunk(su_ref[rows, :].astype(f32), sv_ref[rows, :].astype(f32),
                                              g_ref[...], w_ref, b_ref)
            y_ref[rows, :] = (u * mixed).astype(bf16)

    row = pl.BlockSpec((tm, SGU_WIDTH), lambda i: (i, 0))
    return pl.pallas_call(
        body, name="sgu_fwd", grid=(T // tm,),
        in_specs=[row, row, _full((1, SGU_WIDTH)), _full((SGU_GROUPS, BLOCK, BLOCK)), _full((SGU_GROUPS, BLOCK, 1))],
        out_specs=row, out_shape=jax.ShapeDtypeStruct((T, SGU_WIDTH), bf16),
        compiler_params=_params(("parallel",)),
    )(su, sv, gain, w_s, b_s)


def _merge_fwd(x, y_att, y_sgu, ga, gb, w_oa_t, w_ob_t, w_out, layer, tm):
    T = x.shape[0]

    def body(x_ref, ya_ref, ys_ref, ga_ref, gb_ref, woa_ref, wob_ref, wout_ref, x1_ref, m_ref, a_ref, b_ref):
        a = _dot_nt(ya_ref[...], woa_ref[...])
        b = _dot_nt(ys_ref[...], wob_ref[...])
        a_ref[...] = a.astype(bf16)
        b_ref[...] = b.astype(bf16)
        merged = (_sigmoid(ga_ref[...].astype(f32)) * a + _sigmoid(gb_ref[...].astype(f32)) * b).astype(bf16)
        m_ref[...] = merged
        x1_ref[...] = x_ref[...] + _dot_nn(merged, wout_ref[...])

    row = lambda w: pl.BlockSpec((tm, w), lambda i: (i, 0))
    return pl.pallas_call(
        body, name=f"merge_fwd_{layer}", grid=(T // tm,),
        in_specs=[row(D_MODEL), row(512), row(512), row(1024), row(1024),
                  _wspec(D_MODEL, ATT_WIDTH, layer), _wspec(D_MODEL, SGU_WIDTH, layer), _wspec(D_MODEL, D_MODEL, layer)],
        out_specs=[row(D_MODEL)] * 4,
        out_shape=[jax.ShapeDtypeStruct((T, D_MODEL), f32)] + [jax.ShapeDtypeStruct((T, D_MODEL), bf16)] * 3,
        compiler_params=_params(("parallel",)),
    )(x, y_att, y_sgu, ga, gb, w_oa_t, w_ob_t, w_out)


def _tile_permutation(tm):
    r = np.arange(tm)
    p = np.zeros((tm, tm), np.float32)
    p[r, (r % 8) * (tm // 8) + r // 8] = 1.0
    return jnp.asarray(p, bf16), jnp.asarray(p.T, bf16)


def _stage_taps_before(buf, zz, prev, tm):
    first = lax.broadcasted_iota(jnp.int32, (8, 1), 0) == 0
    buf[16:16 + tm, :] = zz
    buf[0:8, :] = jnp.where(first, prev[7:8], pltpu.roll(buf[tm:tm + 8, :], 1, 0))
    buf[8:16, :] = jnp.where(first, prev[15:16], pltpu.roll(buf[tm + 8:tm + 16, :], 1, 0))


def _stage_taps_after(buf, nxt, tm):
    last = lax.broadcasted_iota(jnp.int32, (8, 1), 0) == 7
    buf[tm:tm + 8, :] = jnp.where(last, nxt[0:1], pltpu.roll(buf[0:8, :], 7, 0))
    buf[tm + 8:tm + 16, :] = jnp.where(last, nxt[8:9], pltpu.roll(buf[8:16, :], 7, 0))


def _conv_rows(buf, r, n, coef):
    z2 = buf[pl.ds(r, n), :]
    z1 = buf[pl.ds(pl.multiple_of(r + 8, 8), n), :]
    z0 = buf[pl.ds(pl.multiple_of(r + 16, 8), n), :]
    return coef[0] + coef[1] * z2 + coef[2] * z1 + coef[3] * z0


def _ffn_up(x1, gain, w_up_t, conv_w, conv_b, layer, seq, tm):
    T = x1.shape[0]
    tps = seq // tm
    perm, perm_t = _tile_permutation(tm)

    rg = 16

    def body(x_ref, g_ref, w_ref, cw_ref, cb_ref, p_ref, pt_ref, h2_ref, z_ref, act_ref, carry_ref,
             zg_buf, zv_buf, actp_buf):
        i = pl.program_id(0)

        @pl.when(i % tps == 0)
        def _():
            carry_ref[...] = jnp.zeros_like(carry_ref)

        xf = x_ref[...]
        h2 = (xf * _rms(xf) * g_ref[...]).astype(bf16)
        h2_ref[...] = h2
        h2p = _dot_nn(p_ref[...], h2).astype(bf16)
        for cc in range(D_FF // FF_CHUNK):
            cols_g = slice(cc * FF_CHUNK, (cc + 1) * FF_CHUNK)
            cols_v = slice(D_FF + cc * FF_CHUNK, D_FF + (cc + 1) * FF_CHUNK)
            for buf, cols in ((zg_buf, cols_g), (zv_buf, cols_v)):
                zb = _dot_nt(h2p, w_ref[cols, :]).astype(bf16)
                z_ref[:, cols] = zb
                _stage_taps_before(buf, zb.astype(f32), carry_ref[:, cols], tm)
                carry_ref[:, cols] = buf[tm:tm + 16, :]
            coef = [jnp.broadcast_to(v, (rg, FF_CHUNK)) for cols in (cols_g, cols_v)
                    for v in (cb_ref[:, cols], cw_ref[0:1, cols], cw_ref[1:2, cols], cw_ref[2:3, cols])]

            def rows_step(j, carry, coef=coef):
                r = pl.multiple_of(j * rg, rg)
                zcg, zcv = (_conv_rows(buf, r, rg, coef[4 * k:4 * k + 4]) for k, buf in enumerate((zg_buf, zv_buf)))
                actp_buf[pl.ds(r, rg), :] = (zcg * _sigmoid(zcg) * zcv).astype(bf16)
                return carry

            lax.fori_loop(0, tm // rg, rows_step, 0, unroll=True)
            act_ref[:, cols_g] = _dot_nn(pt_ref[...], actp_buf[...]).astype(bf16)

    row = lambda w: pl.BlockSpec((tm, w), lambda i: (i, 0))
    return pl.pallas_call(
        body, name=f"ffn_up_{layer}", grid=(T // tm,),
        in_specs=[row(D_MODEL), _full((1, D_MODEL)), _wspec(2 * D_FF, D_MODEL, layer),
                  _full((3, 2 * D_FF)), _full((1, 2 * D_FF)), _full((tm, tm)), _full((tm, tm))],
        out_specs=[row(D_MODEL), row(2 * D_FF), row(D_FF)],
        out_shape=[jax.ShapeDtypeStruct((T, D_MODEL), bf16), jax.ShapeDtypeStruct((T, 2 * D_FF), bf16),
                   jax.ShapeDtypeStruct((T, D_FF), bf16)],
        scratch_shapes=[pltpu.VMEM((16, 2 * D_FF), f32), pltpu.VMEM((tm + 16, FF_CHUNK), f32),
                        pltpu.VMEM((tm + 16, FF_CHUNK), f32), pltpu.VMEM((tm, FF_CHUNK), bf16)],
        compiler_params=_params(("arbitrary",)),
    )(x1, gain, w_up_t, conv_w, conv_b, perm, perm_t)


def _ffn_down(x1, act, w_down, after, layer, tm):
    T = x1.shape[0]

    def body(x_ref, a_ref, w_ref, after_ref, o_ref):
        o_ref[...] = x_ref[...] + _dot_nn(a_ref[...], w_ref[...])

    row = lambda w: pl.BlockSpec((tm, w), lambda i: (i, 0))
    return pl.pallas_call(
        body, name=f"ffn_down_{layer}", grid=(T // tm,),
        in_specs=[row(D_MODEL), row(D_FF), _wspec(D_FF, D_MODEL, layer), _full((1, 1))],
        out_specs=row(D_MODEL), out_shape=jax.ShapeDtypeStruct((T, D_MODEL), f32),
        compiler_params=_params(("parallel",)),
    )(x1, act, w_down, after)


def _ffn_down_loss(x1, act, w_down, target, layer, tm):
    T = x1.shape[0]

    def body(x_ref, a_ref, w_ref, t_ref, dy_ref, dyb_ref, loss_ref):
        @pl.when(pl.program_id(0) == 0)
        def _():
            loss_ref[...] = jnp.zeros_like(loss_ref)

        diff = x_ref[...] + _dot_nn(a_ref[...], w_ref[...]) - t_ref[...]
        loss_ref[...] += 0.5 * jnp.sum(jnp.mean(diff * diff, axis=-1, keepdims=True), axis=0, keepdims=True)
        dy = diff * (1.0 / D_MODEL)
        dy_ref[...] = dy
        dyb_ref[...] = dy.astype(bf16)

    row = lambda w: pl.BlockSpec((tm, w), lambda i: (i, 0))
    return pl.pallas_call(
        body, name=f"ffn_down_loss_{layer}", grid=(T // tm,),
        in_specs=[row(D_MODEL), row(D_FF), _wspec(D_FF, D_MODEL, layer), row(D_MODEL)],
        out_specs=[row(D_MODEL), row(D_MODEL), _full((8, 128))],
        out_shape=[jax.ShapeDtypeStruct((T, D_MODEL), f32), jax.ShapeDtypeStruct((T, D_MODEL), bf16),
                   jax.ShapeDtypeStruct((8, 128), f32)],
        compiler_params=_params(("arbitrary",)),
    )(x1, act, w_down, target)


def _ffn_bwd(dx2b, z, conv_w, conv_b, w_down, layer, seq, tm):
    T = z.shape[0]
    nt = T // tm
    tps = seq // tm

    perm, perm_t = _tile_permutation(tm)

    def body(dx_ref, z_ref, zh_ref, cw_ref, cb_ref, wd_ref, p_ref, pt_ref, dz_ref, dconv_ref, carry_ref,
             zg_buf, zv_buf, gg_buf, gv_buf, dact_buf, dzp_buf):
        i = pl.program_id(0)
        pos = (nt - 1 - i) % tps

        @pl.when(i == 0)
        def _():
            dconv_ref[...] = jnp.zeros_like(dconv_ref)

        @pl.when(pos == tps - 1)
        def _():
            carry_ref[...] = jnp.zeros_like(carry_ref)

        dxp = _dot_nn(p_ref[...], dx_ref[...]).astype(bf16)
        halo_on = (pos > 0).astype(f32)
        for cc in range(D_FF // FF_CHUNK):
            cols_g = slice(cc * FF_CHUNK, (cc + 1) * FF_CHUNK)
            cols_v = slice(D_FF + cc * FF_CHUNK, D_FF + (cc + 1) * FF_CHUNK)
            for buf, cols in ((zg_buf, cols_g), (zv_buf, cols_v)):
                _stage_taps_before(buf, z_ref[:, cols].astype(f32), zh_ref[:, cols].astype(f32) * halo_on, tm)
            dact_buf[...] = _dot_nt(dxp, wd_ref[cols_g, :])
            coef = [jnp.broadcast_to(v, (8, FF_CHUNK)) for cols in (cols_g, cols_v)
                    for v in (cb_ref[:, cols], cw_ref[0:1, cols], cw_ref[1:2, cols], cw_ref[2:3, cols])]

            def first_pass(j, sums, coef=coef):
                r = pl.multiple_of(j * 8, 8)
                rows = pl.ds(r, 8)
                zcg = _conv_rows(zg_buf, r, 8, coef[0:4])
                zcv = _conv_rows(zv_buf, r, 8, coef[4:8])
                sg = _sigmoid(zcg)
                silu = zcg * sg
                d_act = dact_buf[rows, :]
                dg = d_act * zcv * sg * (1.0 + zcg * (1.0 - sg))
                dv = d_act * silu
                gg_buf[rows, :] = dg
                gv_buf[rows, :] = dv
                out = []
                for k, (g, buf) in enumerate(((dg, zg_buf), (dv, zv_buf))):
                    out += [sums[4 * k] + g * buf[rows, :],
                            sums[4 * k + 1] + g * buf[pl.ds(pl.multiple_of(r + 8, 8), 8), :],
                            sums[4 * k + 2] + g * buf[pl.ds(pl.multiple_of(r + 16, 8), 8), :],
                            sums[4 * k + 3] + g]
                return tuple(out)

            sums = lax.fori_loop(0, tm // 8, first_pass, tuple(jnp.zeros((8, FF_CHUNK), f32) for _ in range(8)),
                                 unroll=True)
            for k, cols in enumerate((cols_g, cols_v)):
                for tap in range(4):
                    dconv_ref[tap:tap + 1, cols] += jnp.sum(sums[4 * k + tap], axis=0, keepdims=True)
            for buf, cols in ((gg_buf, cols_g), (gv_buf, cols_v)):
                _stage_taps_after(buf, carry_ref[:, cols], tm)
                carry_ref[:, cols] = buf[0:16, :]
                w0, w1, w2 = (jnp.broadcast_to(cw_ref[k:k + 1, cols], (16, FF_CHUNK)) for k in range(3))

                def second_pass(j, carry, buf=buf, w0=w0, w1=w1, w2=w2):
                    r = pl.multiple_of(j * 16, 16)
                    dzp_buf[pl.ds(r, 16), :] = (w2 * buf[pl.ds(r, 16), :] + w1 * buf[pl.ds(pl.multiple_of(r + 8, 8), 16), :]
                                                + w0 * buf[pl.ds(pl.multiple_of(r + 16, 16), 16), :]).astype(bf16)
                    return carry

                lax.fori_loop(0, tm // 16, second_pass, 0, unroll=True)
                dz_ref[:, cols] = _dot_nn(pt_ref[...], dzp_buf[...]).astype(bf16)

    rev = lambda w: pl.BlockSpec((tm, w), lambda i: (nt - 1 - i, 0))
    return pl.pallas_call(
        body, name=f"ffn_bwd_{layer}", grid=(nt,),
        in_specs=[rev(D_MODEL), rev(2 * D_FF),
                  pl.BlockSpec((16, 2 * D_FF), lambda i: (jnp.maximum((nt - 1 - i) * (tm // 16) - 1, 0), 0)),
                  _full((3, 2 * D_FF)), _full((1, 2 * D_FF)), _wspec(D_FF, D_MODEL, layer),
                  _full((tm, tm)), _full((tm, tm))],
        out_specs=[rev(2 * D_FF), _full((8, 2 * D_FF))],
        out_shape=[jax.ShapeDtypeStruct((T, 2 * D_FF), bf16), jax.ShapeDtypeStruct((8, 2 * D_FF), f32)],
        scratch_shapes=[pltpu.VMEM((16, 2 * D_FF), f32)] + [pltpu.VMEM((tm + 16, FF_CHUNK), f32)] * 4
        + [pltpu.VMEM((tm, FF_CHUNK), f32), pltpu.VMEM((tm, FF_CHUNK), bf16)],
        compiler_params=_params(("arbitrary",)),
    )(dx2b, z, z, conv_w, conv_b, w_down, perm, perm_t)


def _norm_bwd(dys, w, layer, x, gain, dres, tm, name):
    T = dys[0].shape[0]
    widths = [d.shape[1] for d in dys]
    K = sum(widths)
    n = len(dys)

    def body(*refs):
        dy_refs = refs[:n]
        w_ref, x_ref, g_ref, dres_ref, dx_ref, dxb_ref, dg_ref = refs[n:]

        @pl.when(pl.program_id(0) == 0)
        def _():
            dg_ref[...] = jnp.zeros_like(dg_ref)

        dh, lo = None, 0
        for dy_ref, wd in zip(dy_refs, widths):
            part = _dot_nn(dy_ref[...], w_ref[lo:lo + wd, :])
            dh = part if dh is None else dh + part
            lo += wd
        xf = x_ref[...]
        r = _rms(xf)
        x_hat = xf * r
        dg_ref[...] += jnp.sum(dh * x_hat, axis=0, keepdims=True)
        dxh = dh * g_ref[...]
        dx = dres_ref[...] + r * (dxh - x_hat * jnp.mean(dxh * x_hat, axis=-1, keepdims=True))
        dx_ref[...] = dx
        dxb_ref[...] = dx.astype(bf16)

    row = lambda w_: pl.BlockSpec((tm, w_), lambda i: (i, 0))
    return pl.pallas_call(
        body, name=name, grid=(T // tm,),
        in_specs=[row(wd) for wd in widths] + [_wspec(K, D_MODEL, layer), row(D_MODEL), _full((1, D_MODEL)), row(D_MODEL)],
        out_specs=[row(D_MODEL), row(D_MODEL), _full((1, D_MODEL))],
        out_shape=[jax.ShapeDtypeStruct((T, D_MODEL), f32), jax.ShapeDtypeStruct((T, D_MODEL), bf16),
                   jax.ShapeDtypeStruct((1, D_MODEL), f32)],
        compiler_params=_params(("arbitrary",)),
    )(*dys, w, x, gain, dres)


def _merge_bwd(dx1b, ga, gb, a, b, w_oa_t, w_ob_t, w_out, layer, tm):
    T = dx1b.shape[0]

    def body(dx_ref, ga_ref, gb_ref, a_ref, b_ref, woa_ref, wob_ref, wout_ref,
             da_ref, db_ref, dga_ref, dgb_ref, dya_ref, dys_ref):
        dm = _dot_nt(dx_ref[...], wout_ref[...])
        sa = _sigmoid(ga_ref[...].astype(f32))
        sb = _sigmoid(gb_ref[...].astype(f32))
        da = (dm * sa).astype(bf16)
        db = (dm * sb).astype(bf16)
        da_ref[...] = da
        db_ref[...] = db
        dga_ref[...] = (dm * a_ref[...].astype(f32) * sa * (1.0 - sa)).astype(bf16)
        dgb_ref[...] = (dm * b_ref[...].astype(f32) * sb * (1.0 - sb)).astype(bf16)
        dya_ref[...] = _dot_nn(da, woa_ref[...]).astype(bf16)
        dys_ref[...] = _dot_nn(db, wob_ref[...]).astype(bf16)

    row = lambda w: pl.BlockSpec((tm, w), lambda i: (i, 0))
    return pl.pallas_call(
        body, name=f"merge_bwd_{layer}", grid=(T // tm,),
        in_specs=[row(D_MODEL)] * 5 + [_wspec(D_MODEL, ATT_WIDTH, layer), _wspec(D_MODEL, SGU_WIDTH, layer),
                                       _wspec(D_MODEL, D_MODEL, layer)],
        out_specs=[row(D_MODEL)] * 4 + [row(512)] * 2,
        out_shape=[jax.ShapeDtypeStruct((T, D_MODEL), bf16)] * 4 + [jax.ShapeDtypeStruct((T, 512), bf16)] * 2,
        compiler_params=_params(("parallel",)),
    )(dx1b, ga, gb, a, b, w_oa_t, w_ob_t, w_out)


def _sgu_bwd(dy, su, sv, gain, w_s, b_s, tm):
    T = su.shape[0]
    gd = SGU_WIDTH // SGU_GROUPS

    def body(dy_ref, su_ref, sv_ref, g_ref, w_ref, b_ref, dsu_ref, dsv_ref, dw_ref, db_ref, dg_ref):
        @pl.when(pl.program_id(0) == 0)
        def _():
            dw_ref[...] = jnp.zeros_like(dw_ref)
            db_ref[...] = jnp.zeros_like(db_ref)
            dg_ref[...] = jnp.zeros_like(dg_ref)

        gain_v = g_ref[...]
        for ch in range(tm // BLOCK):
            rows = slice(ch * BLOCK, (ch + 1) * BLOCK)
            su_c = su_ref[rows, :].astype(f32)
            sv_c = sv_ref[rows, :].astype(f32)
            u, rv, v_hat, vn, w_tril, mixed = _sgu_chunk(su_c, sv_c, gain_v, w_ref, b_ref)
            dyc = dy_ref[rows, :].astype(f32)
            dsu_ref[rows, :] = (dyc * mixed * _gelu_grad(su_c)).astype(bf16)
            dmix = dyc * u
            dmix_b = dmix.astype(bf16)
            dvn = []
            for g in range(SGU_GROUPS):
                gs = slice(g * gd, (g + 1) * gd)
                db_ref[g] += jnp.sum(dmix[:, gs], axis=1, keepdims=True)
                dw_ref[g] += _dot_nt(dmix_b[:, gs], vn[:, gs])
                dvn.append(_dot_tn(w_tril[g], dmix_b[:, gs]))
            dvn = jnp.concatenate(dvn, axis=1)
            dg_ref[...] += jnp.sum(dvn * v_hat, axis=0, keepdims=True)
            dxh = dvn * gain_v
            dvg = rv * (dxh - v_hat * jnp.mean(dxh * v_hat, axis=-1, keepdims=True))
            dsv_ref[rows, :] = (dvg * _gelu_grad(sv_c)).astype(bf16)

    row = pl.BlockSpec((tm, SGU_WIDTH), lambda i: (i, 0))
    return pl.pallas_call(
        body, name="sgu_bwd", grid=(T // tm,),
        in_specs=[row, row, row, _full((1, SGU_WIDTH)), _full((SGU_GROUPS, BLOCK, BLOCK)),
                  _full((SGU_GROUPS, BLOCK, 1))],
        out_specs=[row, row, _full((SGU_GROUPS, BLOCK, BLOCK)), _full((SGU_GROUPS, BLOCK, 1)), _full((1, SGU_WIDTH))],
        out_shape=[jax.ShapeDtypeStruct((T, SGU_WIDTH), bf16)] * 2 + [
            jax.ShapeDtypeStruct((SGU_GROUPS, BLOCK, BLOCK), f32), jax.ShapeDtypeStruct((SGU_GROUPS, BLOCK, 1), f32),
            jax.ShapeDtypeStruct((1, SGU_WIDTH), f32)],
        compiler_params=_params(("arbitrary",)),
    )(dy, su, sv, gain, w_s, b_s)


def _attn_bwd(dy, qkv, qg, kg, sinks, n_seq, seq):
    T = n_seq * seq
    nb = seq // BLOCK
    scale = HEAD_DIM ** -0.5
    per = 1
    ng = nb // per

    def body(dy_ref, cur_ref, prev_ref, qg_ref, kg_ref, sink_ref, dqkv_ref, dqg_ref, dkg_ref, dsink_ref,
             carry_k, carry_v):
        b = pl.program_id(0)
        j = pl.program_id(1)

        @pl.when((b == 0) & (j == 0))
        def _():
            dqg_ref[...] = jnp.zeros_like(dqg_ref)
            dkg_ref[...] = jnp.zeros_like(dkg_ref)
            dsink_ref[...] = jnp.zeros_like(dsink_ref)

        @pl.when(j == 0)
        def _():
            carry_k[...] = jnp.zeros_like(carry_k)
            carry_v[...] = jnp.zeros_like(carry_v)

        for sub in reversed(range(per)):
            rows = slice(sub * BLOCK, (sub + 1) * BLOCK)
            prev = prev_ref[...] if sub == 0 else cur_ref[(sub - 1) * BLOCK:sub * BLOCK, 512:768]
            one_block(dy_ref[rows, :].astype(f32), cur_ref[rows, :], prev, (ng - 1 - j) * per + sub,
                      qg_ref[...], kg_ref[...], sink_ref, dqkv_ref.at[rows, :], dqg_ref, dkg_ref, dsink_ref,
                      carry_k, carry_v)

    def one_block(dyf, cur, prev, n, qg_v, kg_v, sink_ref, dqkv_ref, dqg_ref, dkg_ref, dsink_ref, carry_k, carry_v):
        dq_pieces = [None] * (N_KV_HEADS * Q_GROUP)
        dk_pieces, dv_pieces = [], []
        for hk in range(N_KV_HEADS):
            a = _attn_head_group(cur, prev, qg_v, kg_v, sink_ref, n, hk)
            do = jnp.concatenate(
                [dyf[:, (hk * Q_GROUP + g) * HEAD_DIM:(hk * Q_GROUP + g + 1) * HEAD_DIM] for g in range(Q_GROUP)],
                axis=0).astype(bf16)
            p = a["p"]
            dp = _dot_nt(do, a["v"])
            dv_band = _dot_tn(p.astype(bf16), do)
            dsum = jnp.sum(p * dp, axis=-1, keepdims=True)
            ds = (p * (dp - dsum)).astype(bf16)
            dsink_col = -a["p_sink"] * dsum
            for g in range(Q_GROUP):
                head = hk * Q_GROUP + g
                dsink_ref[head:head + 1, :] += jnp.sum(dsink_col[g * BLOCK:(g + 1) * BLOCK], axis=0, keepdims=True)
            dqn = _dot_nn(ds, a["kn"])
            dkn_band = _dot_tn(ds, a["qn"])
            dq_hat_g = dqn * scale
            dqg_ref[...] += jnp.sum(dq_hat_g * a["q_hat"], axis=0, keepdims=True)
            dxh = dq_hat_g * qg_v
            dq = a["rq"] * (dxh - a["q_hat"] * jnp.mean(dxh * a["q_hat"], axis=-1, keepdims=True))
            for g in range(Q_GROUP):
                dq_pieces[hk * Q_GROUP + g] = dq[g * BLOCK:(g + 1) * BLOCK]
            dkn = dkn_band[BLOCK:] + carry_k[hk]
            dv_pieces.append(dv_band[BLOCK:] + carry_v[hk])
            carry_k[hk] = dkn_band[:BLOCK]
            carry_v[hk] = dv_band[:BLOCK]
            k_hat = a["k_hat"][BLOCK:]
            dkg_ref[...] += jnp.sum(dkn * k_hat, axis=0, keepdims=True)
            dxk = dkn * kg_v
            dk_pieces.append(a["rk"][BLOCK:] * (dxk - k_hat * jnp.mean(dxk * k_hat, axis=-1, keepdims=True)))
        dqkv_ref[...] = jnp.concatenate(dq_pieces + dk_pieces + dv_pieces, axis=1).astype(bf16)

    blk = lambda w: pl.BlockSpec((per * BLOCK, w), lambda b, j: (b * ng + ng - 1 - j, 0))
    return pl.pallas_call(
        body, name="attn_bwd", grid=(n_seq, ng),
        in_specs=[blk(ATT_WIDTH), blk(768),
                  pl.BlockSpec((BLOCK, 256), lambda b, j: (b * nb + jnp.maximum((ng - 1 - j) * per - 1, 0), 2)),
                  _full((1, HEAD_DIM)), _full((1, HEAD_DIM)), pl.BlockSpec(memory_space=pltpu.SMEM)],
        out_specs=[blk(768), _full((1, HEAD_DIM)), _full((1, HEAD_DIM)), _full((8, 128))],
        out_shape=[jax.ShapeDtypeStruct((T, 768), bf16), jax.ShapeDtypeStruct((1, HEAD_DIM), f32),
                   jax.ShapeDtypeStruct((1, HEAD_DIM), f32), jax.ShapeDtypeStruct((8, 128), f32)],
        scratch_shapes=[pltpu.VMEM((N_KV_HEADS, BLOCK, HEAD_DIM), f32), pltpu.VMEM((N_KV_HEADS, BLOCK, HEAD_DIM), f32)],
        compiler_params=_params(("arbitrary", "arbitrary")),
    )(dy, qkv, qkv, qg, kg, sinks)


def _weight_grad(a, b, tm, tk, name):
    T, M = a.shape
    N = b.shape[1]
    nk = T // tk

    def body(a_ref, b_ref, o_ref, acc_ref):
        k = pl.program_id(1)

        @pl.when(k == 0)
        def _():
            acc_ref[...] = jnp.zeros_like(acc_ref)

        acc_ref[...] += _dot_tn(a_ref[...], b_ref[...])

        @pl.when(k == nk - 1)
        def _():
            o_ref[...] = acc_ref[...].astype(bf16)

    return pl.pallas_call(
        body, name=name, grid=(M // tm, nk),
        in_specs=[pl.BlockSpec((tk, tm), lambda i, k: (k, i)), pl.BlockSpec((tk, N), lambda i, k: (k, 0))],
        out_specs=pl.BlockSpec((None, tm, N), lambda i, k: (0, i, 0)),
        out_shape=jax.ShapeDtypeStruct((1, M, N), bf16),
        scratch_shapes=[pltpu.VMEM((tm, N), f32)],
        compiler_params=_params(("parallel", "arbitrary")),
    )(a, b)


def _weight_grad_rows(a_list, b, tk, name):
    T, N = b.shape
    widths = [a.shape[1] for a in a_list]
    M = sum(widths)
    nk = T // tk
    n = len(a_list)

    def body(*refs):
        a_refs = refs[:n]
        b_ref, o_ref, acc_ref = refs[n:]
        k = pl.program_id(0)

        @pl.when(k == 0)
        def _():
            acc_ref[...] = jnp.zeros_like(acc_ref)

        lo = 0
        for a_ref, wd in zip(a_refs, widths):
            acc_ref[lo:lo + wd, :] += _dot_tn(a_ref[...], b_ref[...])
            lo += wd

        @pl.when(k == nk - 1)
        def _():
            o_ref[...] = acc_ref[...].astype(bf16)

    return pl.pallas_call(
        body, name=name, grid=(nk,),
        in_specs=[pl.BlockSpec((tk, wd), lambda k: (k, 0)) for wd in widths] + [pl.BlockSpec((tk, N), lambda k: (k, 0))],
        out_specs=pl.BlockSpec((None, M, N), lambda k: (0, 0, 0), pipeline_mode=pl.Buffered(1)),
        out_shape=jax.ShapeDtypeStruct((1, M, N), bf16),
        scratch_shapes=[pltpu.VMEM((M, N), f32)],
        compiler_params=_params(("arbitrary",)),
    )(*a_list, b)


def _place(src, layer, src_slot, n_slots, dst_slot, dtype, name, after=None):
    _, _, rows, cols = src.shape
    slots = jnp.stack([src_slot, dst_slot]).astype(jnp.int32)

    def body(slots_ref, s_ref, *rest):
        rest[-1][...] = s_ref[...].astype(dtype)

    return pl.pallas_call(
        body, name=name,
        grid_spec=pltpu.PrefetchScalarGridSpec(
            num_scalar_prefetch=1, grid=(1,),
            in_specs=[pl.BlockSpec((None, None, rows, cols), lambda i, sl: (layer, sl[0], 0, 0))]
            + ([] if after is None else [ANY]),
            out_specs=pl.BlockSpec((None, rows, cols), lambda i, sl: (sl[1], 0, 0))),
        out_shape=jax.ShapeDtypeStruct((n_slots, rows, cols), dtype),
        compiler_params=_params(("arbitrary",)),
    )(slots, src, *([] if after is None else [after]))


HBM = pl.BlockSpec(memory_space=pltpu.HBM)
SEM = pl.BlockSpec(memory_space=pltpu.SEMAPHORE)
DATAFLOW = pltpu.SideEffectType.DATAFLOW_SIDE_EFFECTING


def _other_chips(x, y):
    return [(1 - x, y), (x, 1 - y), (1 - x, 1 - y)]


def _split_start(groups, name):
    nb = [len(bufs) for bufs, _ in groups]
    flat = [b for bufs, _ in groups for b in bufs]
    ns = [len(plan(bufs, dry=True)) for bufs, plan in groups]
    ng = len(groups)

    def body(*refs):
        n_in = len(flat)
        sems = refs[n_in:n_in + 2 * ng]
        thru = refs[n_in + 2 * ng:2 * n_in + 2 * ng]
        token = refs[2 * n_in + 2 * ng]
        off = 0
        for g, (bufs, plan) in enumerate(groups):
            mine = thru[off:off + nb[g]]
            off += nb[g]
            for k, (src, dst, to) in enumerate(plan(mine)):
                pltpu.make_async_remote_copy(
                    src_ref=src, dst_ref=dst, send_sem=sems[2 * g].at[k], recv_sem=sems[2 * g + 1].at[k],
                    device_id=to, device_id_type=MESH).start()
        token[...] = jnp.zeros_like(token)

    out_shape = []
    for n in ns:
        out_shape += [pltpu.SemaphoreType.DMA((n,)), pltpu.SemaphoreType.DMA((n,))]
    out_shape += [pltpu.HBM(b.shape, b.dtype) for b in flat]
    out_shape.append(jax.ShapeDtypeStruct((8, 128), f32))
    res = pl.pallas_call(
        body, name=name, out_shape=tuple(out_shape),
        in_specs=[HBM] * len(flat),
        out_specs=tuple([SEM] * (2 * ng) + [HBM] * len(flat) + [pl.BlockSpec(memory_space=pltpu.VMEM)]),
        input_output_aliases={i: 2 * ng + i for i in range(len(flat))},
        compiler_params=pltpu.CompilerParams(has_side_effects=DATAFLOW),
    )(*[pltpu.with_memory_space_constraint(b, pltpu.HBM) for b in flat])
    out, off = [], 2 * ng
    for g in range(ng):
        out.append((res[2 * g], res[2 * g + 1], list(res[off:off + nb[g]])))
        off += nb[g]
    return out, res[-1]


def _split_wait(bufs, send, recv, plan, after, name):
    nb = len(bufs)

    def body(*refs):
        thru = refs[:nb]
        send_ref, recv_ref = refs[nb], refs[nb + 1]
        for k, (src, dst, to) in enumerate(plan(thru)):
            cp = pltpu.make_async_remote_copy(
                src_ref=src, dst_ref=dst, send_sem=send_ref.at[k], recv_sem=recv_ref.at[k],
                device_id=to, device_id_type=MESH)
            cp.wait_send()
            cp.wait_recv()

    res = pl.pallas_call(
        body, name=name, out_shape=tuple(pltpu.HBM(b.shape, b.dtype) for b in bufs),
        in_specs=[HBM] * nb + [SEM, SEM, ANY], out_specs=tuple([HBM] * nb),
        input_output_aliases={i: i for i in range(nb)},
        compiler_params=pltpu.CompilerParams(has_side_effects=DATAFLOW),
    )(*bufs, send, recv, after)
    return list(res)


def _gather_plan(hrs, n_direct=0):
    def plan(refs, dry=False):
        if dry:
            return [None] * (4 * len(hrs) + 3 * n_direct)
        x, y, c = _mesh_pos()
        me = 4 * x + 2 * y + c
        out = []
        for i in range(n_direct):
            src, land = refs[len(hrs) + 2 * i], refs[len(hrs) + 2 * i + 1]
            out += [(src, land.at[2 * x + y], (*chip, c)) for chip in _other_chips(x, y)]
        for ref, hr in zip(refs, hrs):
            rows = ref.at[pl.ds(pl.multiple_of(me * hr, 16), hr), :]
            out.append((rows, rows, (x, y, 1 - c)))
            out += [(rows, rows, (*chip, c)) for chip in _other_chips(x, y)]
        return out
    return plan


def _pass_plan(hrs):
    def plan(refs, dry=False):
        if dry:
            return [None] * (3 * len(hrs))
        x, y, c = _mesh_pos()
        out = []
        for ref, hr in zip(refs, hrs):
            for chip in _other_chips(x, y):
                rows = ref.at[pl.ds(pl.multiple_of((4 * chip[0] + 2 * chip[1] + c) * hr, 16), hr), :]
                out.append((rows, rows, (x, y, 1 - c)))
        return out
    return plan


def _pair_plan(hrs):
    n = len(hrs)

    def plan(refs, dry=False):
        if dry:
            return [None] * (N_CHIP * n)
        x, y, c = _mesh_pos()
        out = []
        for r in range(n):
            for j in range(N_CHIP):
                start = pl.multiple_of((2 * j + 1 - c) * hrs[r], 16)
                out.append((refs[r].at[0, pl.ds(start, hrs[r]), :], refs[n + r].at[0, j], (x, y, 1 - c)))
        return out
    return plan


def _all_to_all_plan(n):
    def plan(refs, dry=False):
        if dry:
            return [None] * (7 * n)
        x, y, c = _mesh_pos()
        out = []
        for ref in refs:
            mine = ref.at[4 * x + 2 * y + c]
            for fx in range(2):
                for fy in range(2):
                    for fc in range(2):
                        if fx or fy or fc:
                            out.append((mine, mine, (1 - x if fx else x, 1 - y if fy else y, 1 - c if fc else c)))
        return out
    return plan


def _pass_to_sibling(bufs, hrs, name):
    nb = len(bufs)

    def body(*refs):
        out = refs[nb:2 * nb]
        send, recv = refs[2 * nb:]
        x, y, c = _mesh_pos()
        chips = _other_chips(x, y)
        started = []
        for i in range(nb):
            for j, chip in enumerate(chips):
                rows = out[i].at[pl.ds(pl.multiple_of((4 * chip[0] + 2 * chip[1] + c) * hrs[i], 16), hrs[i]), :]
                cp = pltpu.make_async_remote_copy(
                    src_ref=rows, dst_ref=rows, send_sem=send.at[3 * i + j], recv_sem=recv.at[3 * i + j],
                    device_id=(x, y, 1 - c), device_id_type=MESH)
                cp.start()
                started.append(cp)
        for i in range(nb):
            for j, chip in enumerate(chips):
                rows = out[i].at[pl.ds(pl.multiple_of((4 * chip[0] + 2 * chip[1] + 1 - c) * hrs[i], 16), hrs[i]), :]
                pltpu.make_async_remote_copy(
                    src_ref=rows, dst_ref=rows, send_sem=send.at[3 * i + j], recv_sem=recv.at[3 * i + j],
                    device_id=(x, y, 1 - c), device_id_type=MESH).wait_recv()
        for cp in started:
            cp.wait_send()

    return list(pl.pallas_call(
        body, name=name, in_specs=[ANY] * nb, out_specs=[ANY] * nb,
        out_shape=[jax.ShapeDtypeStruct(b.shape, b.dtype) for b in bufs],
        input_output_aliases={i: i for i in range(nb)},
        scratch_shapes=[pltpu.SemaphoreType.DMA((3 * nb,)), pltpu.SemaphoreType.DMA((3 * nb,))],
        compiler_params=pltpu.CompilerParams(has_side_effects=True),
    )(*bufs))


def _pair_exchange(grads, name):
    nr = len(grads)
    n_l = grads[0].shape[0]
    n_sem = nr * n_l * N_CHIP

    def body(*refs):
        src = refs[:nr]
        out = refs[nr:2 * nr]
        send, recv = refs[2 * nr:]
        x, y, c = _mesh_pos()
        copies = []
        for r in range(nr):
            hr = grads[r].shape[1] // N_DEV
            for layer in range(n_l):
                for j in range(N_CHIP):
                    idx = (r * n_l + layer) * N_CHIP + j
                    start = pl.multiple_of((2 * j + 1 - c) * hr, 16)
                    cp = pltpu.make_async_remote_copy(
                        src_ref=src[r].at[layer, pl.ds(start, hr), :], dst_ref=out[r].at[layer, j],
                        send_sem=send.at[idx], recv_sem=recv.at[idx], device_id=(x, y, 1 - c), device_id_type=MESH)
                    cp.start()
                    copies.append(cp)
        for cp in copies:
            cp.wait()

    return pl.pallas_call(
        body, name=name,
        in_specs=[ANY] * nr, out_specs=[ANY] * nr,
        out_shape=[jax.ShapeDtypeStruct((n_l, N_CHIP, g.shape[1] // N_DEV, g.shape[2]), bf16) for g in grads],
        scratch_shapes=[pltpu.SemaphoreType.DMA((n_sem,)), pltpu.SemaphoreType.DMA((n_sem,))],
        compiler_params=pltpu.CompilerParams(has_side_effects=True),
    )(*grads)


def _pair_sum(grad, other, core, chip, name):
    n_l, rows, cols = grad.shape
    hr = rows // N_DEV
    g5 = grad.reshape(n_l, N_CHIP, 2, hr, cols)
    where = jnp.stack([core, chip]).astype(jnp.int32)

    def body(where_ref, g_ref, o_ref, s_ref, mine_ref):
        s_ref[...] = (g_ref[...].astype(f32) + o_ref[...].astype(f32)).astype(bf16)
        mine_ref[...] = s_ref[where_ref[1]]

    return pl.pallas_call(
        body, name=name,
        grid_spec=pltpu.PrefetchScalarGridSpec(
            num_scalar_prefetch=1, grid=(n_l,),
            in_specs=[pl.BlockSpec((None, N_CHIP, None, hr, cols), lambda l, w: (l, 0, w[0], 0, 0)),
                      pl.BlockSpec((None, N_CHIP, hr, cols), lambda l, w: (l, 0, 0, 0))],
            out_specs=[pl.BlockSpec((None, N_CHIP, hr, cols), lambda l, w: (l, 0, 0, 0)),
                       pl.BlockSpec((None, None, hr, cols), lambda l, w: (l, w[1], 0, 0))]),
        out_shape=[jax.ShapeDtypeStruct((n_l, N_CHIP, hr, cols), bf16)] * 2,
        compiler_params=_params(("arbitrary",)),
    )(where, g5, other)


def _chip_plan(nr, n_l):
    def plan(refs, dry=False):
        if dry:
            return [None] * (nr * n_l * 3)
        x, y, c = _mesh_pos()
        out = []
        for r in range(nr):
            for layer in range(n_l):
                for chip in _other_chips(x, y):
                    out.append((refs[r].at[layer, 2 * chip[0] + chip[1]], refs[nr + r].at[layer, 2 * x + y], (*chip, c)))
        return out
    return plan


def _chip_sum(parts, core, name):
    _, _, hr, cols = parts[0].shape

    def body(core_ref, p0_ref, p1_ref, o_ref):
        def total(p_ref):
            acc = p_ref[0].astype(f32) + p_ref[1].astype(f32)
            acc = acc + p_ref[2].astype(f32)
            return acc + p_ref[3].astype(f32)

        @pl.when(pl.program_id(0) == 0)
        def _():
            o_ref[...] = total(p0_ref)

        @pl.when(pl.program_id(0) == 1)
        def _():
            o_ref[...] = total(p1_ref)

    spec = pl.BlockSpec((None, N_CHIP, hr, cols), lambda l, cr: (0, 0, 0, 0))
    return pl.pallas_call(
        body, name=name,
        grid_spec=pltpu.PrefetchScalarGridSpec(
            num_scalar_prefetch=1, grid=(2,), in_specs=[spec, spec],
            out_specs=pl.BlockSpec((None, None, hr, cols), lambda l, cr: (l, cr[0], 0, 0))),
        out_shape=jax.ShapeDtypeStruct((2, 2, hr, cols), f32),
        compiler_params=_params(("arbitrary",)),
    )(core, parts[0], parts[1])


def _share_plan(n):
    def plan(refs, dry=False):
        if dry:
            return [None] * (2 * n)
        x, y, c = _mesh_pos()
        return [(ref.at[layer, c], ref.at[layer, c], (x, y, 1 - c)) for ref in refs for layer in range(2)]
    return plan


def _sum_small(parts, name):
    n, rows, cols = parts.shape

    def body(p_ref, o_ref):
        acc = p_ref[0].astype(f32)
        for d in range(1, n):
            acc = acc + p_ref[d].astype(f32)
        o_ref[...] = acc

    return pl.pallas_call(
        body, name=name, grid=(rows // 16,),
        in_specs=[pl.BlockSpec((n, 16, cols), lambda i: (0, i, 0))], out_specs=pl.BlockSpec((16, cols), lambda i: (i, 0)),
        out_shape=jax.ShapeDtypeStruct((rows, cols), f32),
        compiler_params=_params(("parallel",)),
    )(parts)


def _adamw(w, g, m, v, name):
    n_l, rows, cols = w.shape
    budget = 42 * 1024 * 1024
    tr = next(rows // d for d in range(1, rows + 1)
              if rows % d == 0 and (rows // d) % 8 == 0 and (rows // d) * cols * 4 * 14 <= budget)

    def body(w_ref, g_ref, m_ref, v_ref, d_ref, nm_ref, nv_ref):
        gg = g_ref[...]
        nm = ADAM_B1 * m_ref[...] + (1.0 - ADAM_B1) * gg
        nv = ADAM_B2 * v_ref[...] + (1.0 - ADAM_B2) * (gg * gg)
        m_hat = nm / (1.0 - ADAM_B1 ** ADAM_STEP)
        v_hat = nv / (1.0 - ADAM_B2 ** ADAM_STEP)
        d_ref[...] = -ADAM_LR * (m_hat / (jnp.sqrt(v_hat) + ADAM_EPS) + ADAM_WD * w_ref[...])
        nm_ref[...] = nm
        nv_ref[...] = nv

    blk = pl.BlockSpec((None, tr, cols), lambda l, i: (l, i, 0))
    return pl.pallas_call(
        body, name=name, grid=(n_l, rows // tr),
        in_specs=[blk] * 4, out_specs=[blk] * 3, out_shape=[jax.ShapeDtypeStruct((n_l, rows, cols), f32)] * 3,
        compiler_params=_params(("parallel", "parallel")),
    )(w, g, m, v)


SMALL = ("mix_norm", "q_norm", "k_norm", "sinks", "sgu_norm", "w_s", "b_s", "ffn_norm", "conv_b", "conv_w")


def _pack_small(arrs):
    flat = jnp.concatenate([a.reshape(-1) for a in arrs])
    pad = (-flat.shape[0]) % (16 * 1024)
    return jnp.pad(flat, (0, pad)).reshape(-1, 1024)


def _unpack_small(pack, shapes):
    flat = pack.reshape(-1)
    out, off = [], 0
    for s in shapes:
        n = int(np.prod(s))
        out.append(flat[off:off + n].reshape(s))
        off += n
    return out


def kernel(x, mix_norm, w_in, q_norm, k_norm, sinks, sgu_norm, w_s, b_s, w_oa, w_ob, w_out, ffn_norm, w_up, conv_w, conv_b, w_down, loss_target, m_mix_norm, m_w_in, m_q_norm, m_k_norm, m_sinks, m_sgu_norm, m_w_s, m_b_s, m_w_oa, m_w_ob, m_w_out, m_ffn_norm, m_w_up, m_conv_w, m_conv_b, m_w_down, v_mix_norm, v_w_in, v_q_norm, v_k_norm, v_sinks, v_sgu_norm, v_w_s, v_b_s, v_w_oa, v_w_ob, v_w_out, v_ffn_norm, v_w_up, v_conv_w, v_conv_b, v_w_down):
    weights = dict(mix_norm=mix_norm, w_in=w_in, q_norm=q_norm, k_norm=k_norm, sinks=sinks, sgu_norm=sgu_norm,
                   w_s=w_s, b_s=b_s, w_oa=w_oa, w_ob=w_ob, w_out=w_out, ffn_norm=ffn_norm, w_up=w_up,
                   conv_w=conv_w, conv_b=conv_b, w_down=w_down)
    mom_m = dict(mix_norm=m_mix_norm, w_in=m_w_in, q_norm=m_q_norm, k_norm=m_k_norm, sinks=m_sinks,
                 sgu_norm=m_sgu_norm, w_s=m_w_s, b_s=m_b_s, w_oa=m_w_oa, w_ob=m_w_ob, w_out=m_w_out,
                 ffn_norm=m_ffn_norm, w_up=m_w_up, conv_w=m_conv_w, conv_b=m_conv_b, w_down=m_w_down)
    mom_v = dict(mix_norm=v_mix_norm, w_in=v_w_in, q_norm=v_q_norm, k_norm=v_k_norm, sinks=v_sinks,
                 sgu_norm=v_sgu_norm, w_s=v_w_s, b_s=v_b_s, w_oa=v_w_oa, w_ob=v_w_ob, w_out=v_w_out,
                 ffn_norm=v_ffn_norm, w_up=v_w_up, conv_w=v_conv_w, conv_b=v_conv_b, w_down=v_w_down)
    n_seq, seq, _ = x.shape
    T = n_seq * seq
    core = lax.axis_index("c")
    chip = 2 * lax.axis_index("x") + lax.axis_index("y")
    tm = min(512, seq)
    tm_ff = min(256, seq)
    tm_sgu = min(512, seq)
    tk_dw = min(2048, T)

    me = 2 * chip + core
    names = [r[0] for r in REGIONS]
    hrs = {name: rows // N_DEV for name, rows, _, _ in REGIONS}
    def placed(l, name, after=None):
        _, rows, cols, transposed = next(r for r in REGIONS if r[0] == name)
        shard = (jnp.swapaxes(weights[name], 1, 2) if transposed else weights[name]).reshape(2, 2, hrs[name], cols)
        return _place(shard, l, core, N_DEV, me, bf16, f"place_{name}_{l}", after).reshape(rows, cols)

    group_keys = [[(0, "w_in")], [(0, n) for n in names[1:]],
                  [(1, n) for n in ("w_in", "w_oa", "w_ob", "w_out")], [(1, "w_up"), (1, "w_down")]]
    n_direct = [1, 0, 0, 0]
    plans = [_gather_plan([hrs[n] for _, n in keys], nd) for keys, nd in zip(group_keys, n_direct)]
    first_bufs = [placed(0, "w_in"), conv_w, jnp.zeros((N_CHIP,) + conv_w.shape, f32)]
    started, tok = _split_start([(first_bufs, plans[0])], "gather_start_0")
    first_start_done = jnp.broadcast_to(tok[0:1, 0:1], (512, D_MODEL))
    rest_bufs = [[placed(l, n, first_start_done) for l, n in keys] for keys in group_keys[1:]]
    more, tok = _split_start(list(zip(rest_bufs, plans[1:])), "gather_start_1")
    started += more
    second_start_done = jnp.broadcast_to(tok[0:1, 0:1], (512, D_MODEL))
    gathered = [{}, {}]

    def arrived(g, after):
        send, recv, bufs = started[g]
        hr_list = [hrs[n] for _, n in group_keys[g]]
        bufs = _split_wait(bufs, send, recv, _gather_plan(hr_list, n_direct[g]), after, f"gather_wait_{g}")
        return bufs[:len(hr_list)], bufs[len(hr_list):]

    def start_pass(g, bufs):
        (res,), token = _split_start([(bufs, _pass_plan([hrs[n] for _, n in group_keys[g]]))], f"pass_start_{g}")
        return res, token[0:1, 0:1]

    def finish_pass(g, res, after):
        send, recv, bufs = res
        use(g, _split_wait(bufs, send, recv, _pass_plan([hrs[n] for _, n in group_keys[g]]), after, f"pass_wait_{g}"))

    def use(g, bufs):
        for (l, n), b in zip(group_keys[g], bufs):
            gathered[l][n] = b

    xs = x.reshape(T, D_MODEL)
    bufs, (_, conv_w_land) = arrived(0, second_start_done)
    use(0, _pass_to_sibling(bufs, [hrs["w_in"]], "gather_pass_0"))
    conv_w_all = lax.dynamic_update_slice(conv_w_land, conv_w[None], (chip, 0, 0, 0))
    conv_w_full = jnp.concatenate([conv_w_all[j] for j in range(N_CHIP)], axis=-1)
    saved = []
    cur = xs
    for l in range(2):
        wl = gathered[l]
        b_col = b_s[l].reshape(SGU_GROUPS, BLOCK, 1)
        qkv, su, sv, ga, gb, h = _in_proj(cur, mix_norm[l][None], wl["w_in"], l, tm)
        y_att = _attn_fwd(qkv, q_norm[l][None], k_norm[l][None], sinks[l], n_seq, seq)
        later = 1 if l == 0 else 3
        passing, tok = start_pass(later, arrived(later, y_att)[0])
        y_sgu = _sgu_fwd(su, sv, sgu_norm[l][None] + tok, w_s[l], b_col, tm_sgu)
        finish_pass(later, passing, y_sgu)
        x1, merged, a_o, b_o = _merge_fwd(cur, y_att, y_sgu, ga, gb, wl["w_oa"], wl["w_ob"], wl["w_out"], l, tm)
        h2, z, act = _ffn_up(x1, ffn_norm[l][None], wl["w_up"], conv_w_full[l], conv_b[l][None], l, seq, tm_ff)
        saved.append(dict(x=cur, qkv=qkv, su=su, sv=sv, ga=ga, gb=gb, h=h, y_att=y_att, y_sgu=y_sgu, x1=x1,
                          merged=merged, a=a_o, b=b_o, h2=h2, z=z, act=act, b_col=b_col))
        if l == 0:
            pass_2, tok = start_pass(2, arrived(2, act)[0])
            cur = _ffn_down(x1, act, wl["w_down"], tok, l, tm)
            finish_pass(2, pass_2, cur)
        else:
            dy, dyb, loss_part = _ffn_down_loss(x1, act, wl["w_down"], loss_target.reshape(T, D_MODEL), l, tm)

    core_arr = core.astype(jnp.int32).reshape(1)
    big = [{}, {}]
    small = {name: [None, None] for name in SMALL}

    def start_pairs(l, keys, tag):
        gl = [big[l][n] for n in keys]
        land = [lax.empty((1, N_CHIP, hrs[n], g.shape[2]), bf16) for n, g in zip(keys, gl)]
        (res,), token = _split_start([(gl + land, _pair_plan([hrs[n] for n in keys]))], f"pair_start_{tag}")
        return (l, keys, res, tag), token[0:1, 0:1]

    def pairs_to_chips(state, after):
        l, keys, (send, recv, bufs), tag = state
        bufs = _split_wait(bufs, send, recv, _pair_plan([hrs[n] for n in keys]), after, f"pair_wait_{tag}")
        return sums_to_chips(l, keys, bufs[:len(keys)], bufs[len(keys):], tag)

    def sums_to_chips(l, keys, gl, from_sibling, tag):
        pairs = [_pair_sum(g, o, core, chip, f"pair_sum_{n}_{l}") for g, o, n in zip(gl, from_sibling, keys)]
        bufs = [p[0] for p in pairs] + [p[1] for p in pairs]
        (res,), token = _split_start([(bufs, _chip_plan(len(keys), 1))], f"chip_start_{tag}")
        return (l, keys, res, tag), token[0:1, 0:1]

    def start_reduce(l, keys, tag):
        gl = [big[l][n] for n in keys]
        return sums_to_chips(l, keys, gl, _pair_exchange(gl, f"pair_exchange_{tag}"), tag)

    landed = {}

    def finish_reduce(state, after):
        l, keys, (send, recv, bufs), tag = state
        bufs = _split_wait(bufs, send, recv, _chip_plan(len(keys), 1), after, f"chip_wait_{tag}")
        for n, p in zip(keys, bufs[len(keys):]):
            landed[(l, n)] = p

    rest = [n for n in SMALL if n != "w_s"]
    rest_shapes = [weights[n].shape[1:] if n != "conv_w" else (3, 2 * D_FF) for n in rest]
    zero = jnp.zeros((), jnp.int32)

    def start_small(l):
        extra = loss_part[0, 0:1] if l == 1 else jnp.zeros((1,), f32)
        packs = [(_pack_small([small[n][l] for n in rest] + [extra]), f32, "small"),
                 (small["w_s"][l].reshape(-1, 1024), bf16, "w_s")]
        bufs = [_place(p[None, None], 0, zero, N_DEV, me, dt, f"place_{tag}_{l}") for p, dt, tag in packs]
        (res,), token = _split_start([(bufs, _all_to_all_plan(2))], f"small_start_{l}")
        return res, token[0:1, 0:1]

    def finish_small(l, res, after):
        send, recv, bufs = res
        bufs = _split_wait(bufs, send, recv, _all_to_all_plan(2), after, f"small_wait_{l}")
        out = dict(zip(rest + ["loss"], _unpack_small(_sum_small(bufs[0], f"sum_small_{l}"), rest_shapes + [(1,)])))
        out["w_s"] = _sum_small(bufs[1], f"sum_w_s_{l}").reshape(w_s.shape[1:])
        return out

    pending = []
    after_start = jnp.zeros((1, 1), f32)
    for l in (1, 0):
        s = saved[l]
        wl = gathered[l]
        dz, dconv = _ffn_bwd(dyb, s["z"], conv_w_full[l], conv_b[l][None] + after_start, wl["w_down"], l, seq, tm_ff)
        big[l]["w_down"] = _weight_grad(s["act"], dyb, 1408, tk_dw, f"dw_down_{l}")
        big[l]["w_up"] = _weight_grad(dz, s["h2"], 1408, tk_dw, f"dw_up_{l}")
        ffn_gain, sgu_gain, q_gain = ffn_norm[l][None], sgu_norm[l][None], q_norm[l][None]
        if l == 0:
            pairs_a, tok = start_pairs(0, ["w_down", "w_up"], "0a")
            ffn_gain = ffn_gain + tok
        dx1, dx1b, d_ffn = _norm_bwd([dz], wl["w_up"], l, s["x1"], ffn_gain, dy, tm, f"ffn_norm_bwd_{l}")
        if l == 0:
            state, tok = pairs_to_chips(pairs_a, dx1b)
            pending.append(state)
            sgu_gain = sgu_gain + tok
        small["conv_w"][l] = dconv[0:3]
        small["conv_b"][l] = dconv[3]
        small["ffn_norm"][l] = d_ffn[0]
        da, db, dga, dgb, dya, dys = _merge_bwd(dx1b, s["ga"], s["gb"], s["a"], s["b"],
                                                wl["w_oa"], wl["w_ob"], wl["w_out"], l, tm)
        big[l]["w_out"] = _weight_grad(s["merged"], dx1b, 1024, tk_dw, f"dw_out_{l}")
        big[l]["w_oa"] = _weight_grad(da, s["y_att"], 1024, tk_dw, f"dw_oa_{l}")
        big[l]["w_ob"] = _weight_grad(db, s["y_sgu"], 1024, tk_dw, f"dw_ob_{l}")
        if l == 0:
            pairs_m, tok = start_pairs(0, ["w_out", "w_oa", "w_ob"], "0m")
            sgu_gain = sgu_gain + tok
        dsu, dsv, d_ws, d_bs, d_sgu = _sgu_bwd(dys, s["su"], s["sv"], sgu_gain, w_s[l], s["b_col"], tm_sgu)
        if l == 0:
            state, tok = pairs_to_chips(pairs_m, dsv)
            pending.append(state)
            q_gain = q_gain + tok
        causal = np.tril(np.ones((BLOCK, BLOCK), bool))
        small["w_s"][l] = jnp.where(causal[None], d_ws, 0.0)
        small["b_s"][l] = d_bs[:, :, 0]
        small["sgu_norm"][l] = d_sgu[0]
        dqkv, d_qg, d_kg, d_sink = _attn_bwd(dya, s["qkv"], q_gain, k_norm[l][None], sinks[l], n_seq, seq)
        small["q_norm"][l] = d_qg[0]
        small["k_norm"][l] = d_kg[0]
        small["sinks"][l] = d_sink[:, 0]
        dproj = [dqkv, dsu, dsv, dga, dgb]
        big[l]["w_in"] = _weight_grad_rows(dproj, s["h"], min(1024, T), f"dw_in_{l}")
        if l == 1:
            pairs_1, tok = start_pairs(1, names, "1")
        else:
            state, tok = start_reduce(0, ["w_in"], "0b")
            pending.append(state)
        dy, dyb, d_mix = _norm_bwd(dproj, wl["w_in"], l, s["x"], mix_norm[l][None] + tok, dx1, tm, f"mix_norm_bwd_{l}")
        small["mix_norm"][l] = d_mix[0]
        if l == 1:
            state, tok = pairs_to_chips(pairs_1, dyb)
            pending.append(state)
            small_1, after_start = start_small(1)
            after_start = after_start + tok
    grad_x = dy.reshape(n_seq, seq, D_MODEL)

    small_0, _ = start_small(0)
    for state in pending:
        finish_reduce(state, dyb)
    share_keys = [["w_in", "w_oa", "w_ob", "w_out"], ["w_up", "w_down"]]
    halves = {n: _chip_sum([landed[(0, n)], landed[(1, n)]], core_arr, f"chip_sum_{n}") for n in names}
    swaps, _ = _split_start([([halves[n] for n in keys], _share_plan(len(keys))) for keys in share_keys], "share_start")
    shared = {}

    def finish_share(k, after):
        send, recv, bufs = swaps[k]
        shared.update(zip(share_keys[k], _split_wait(bufs, send, recv, _share_plan(len(bufs)), after, f"share_wait_{k}")))

    finish_share(0, dyb)
    grad, delta, new_m, new_v = {}, {}, {}, {}
    flip = lambda a: jnp.swapaxes(a, 1, 2)
    for name, rows, cols, transposed in sorted(REGIONS, key=lambda r: r[0] in share_keys[1]):
        if name == share_keys[1][0]:
            finish_share(1, delta[share_keys[0][-1]])
        g = shared[name].reshape(2, rows // N_CHIP, cols)
        if transposed and weights[name].shape[2] % 128:
            d, nm, nv = _adamw(flip(weights[name]), g, flip(mom_m[name]), flip(mom_v[name]), f"adamw_{name}")
            grad[name], delta[name], new_m[name], new_v[name] = flip(g), flip(d), flip(nm), flip(nv)
        else:
            grad[name] = flip(g) if transposed else g
            delta[name], new_m[name], new_v[name] = _adamw(weights[name], grad[name], mom_m[name], mom_v[name],
                                                           f"adamw_{name}")

    per_layer = [finish_small(0, small_0, delta["w_down"]), finish_small(1, small_1, dyb)]
    loss = per_layer[1]["loss"][0]
    grad_small = {n: jnp.stack([per_layer[0][n], per_layer[1][n]]) for n in SMALL}
    cw_cols = conv_w.shape[-1]
    grad_small["conv_w"] = lax.dynamic_slice_in_dim(grad_small["conv_w"], chip * cw_cols, cw_cols, axis=2)

    as_rows = lambda a: a.reshape(2, -1, BLOCK)
    d, nm, nv = _adamw(as_rows(w_s), as_rows(grad_small["w_s"]), as_rows(m_w_s), as_rows(v_w_s), "adamw_w_s")
    grad["w_s"], delta["w_s"], new_m["w_s"], new_v["w_s"] = (
        grad_small["w_s"], d.reshape(w_s.shape), nm.reshape(w_s.shape), nv.reshape(w_s.shape))
    shapes = [weights[n].shape for n in rest]
    d, nm, nv = _adamw(_pack_small([weights[n] for n in rest])[None], _pack_small([grad_small[n] for n in rest])[None],
                       _pack_small([mom_m[n] for n in rest])[None], _pack_small([mom_v[n] for n in rest])[None],
                       "adamw_small")
    for n, dd, mm, vv in zip(rest, _unpack_small(d, shapes), _unpack_small(nm, shapes), _unpack_small(nv, shapes)):
        grad[n], delta[n], new_m[n], new_v[n] = grad_small[n], dd, mm, vv

    order = ["mix_norm", "w_in", "q_norm", "k_norm", "sinks", "sgu_norm", "w_s", "b_s", "w_oa", "w_ob", "w_out",
             "ffn_norm", "w_up", "conv_w", "conv_b", "w_down"]
    return (loss, grad_x, *[grad[n] for n in order], *[delta[n] for n in order],
            *[new_m[n] for n in order], *[new_v[n] for n in order])
```

```python
import numpy as np
import jax
import jax.numpy as jnp
from jax import lax
from jax.experimental import pallas as pl
from jax.experimental.pallas import tpu as pltpu

bf16 = jnp.bfloat16
f32 = jnp.float32

D_MODEL = 1024
ATT_WIDTH = 512
KV_WIDTH = 128
SGU_WIDTH = 512
HEAD_DIM = 64
N_KV_HEADS = 2
Q_GROUP = 4
BLOCK = 128
SGU_GROUPS = 8
IN_WIDTH = 3840
D_FF = 2816
NORM_EPS = 1e-6
NEG_INF = -1e30
N_DEV = 8
N_CHIP = 4

ADAM_LR = 0.001
ADAM_B1 = 0.9
ADAM_B2 = 0.999
ADAM_EPS = 1e-08
ADAM_WD = 0.01
ADAM_STEP = 10

V7X_VMEM_LIMIT = 56 * 1024 * 1024
FF_CHUNK = 2816

REGIONS = (
    ("w_in", IN_WIDTH, D_MODEL, True),
    ("w_oa", D_MODEL, ATT_WIDTH, True),
    ("w_ob", D_MODEL, SGU_WIDTH, True),
    ("w_out", D_MODEL, D_MODEL, False),
    ("w_up", 2 * D_FF, D_MODEL, True),
    ("w_down", D_FF, D_MODEL, False),
)
MESH = pl.DeviceIdType.MESH
ANY = pl.BlockSpec(memory_space=pl.ANY)


def _params(sem=None, **kw):
    return pltpu.CompilerParams(dimension_semantics=sem, vmem_limit_bytes=V7X_VMEM_LIMIT, **kw)


def _wspec(rows, cols, layer=None):
    del layer
    return pl.BlockSpec((rows, cols), lambda *_: (0, 0), pipeline_mode=pl.Buffered(1))


def _full(shape):
    nd = len(shape)
    return pl.BlockSpec(shape, lambda *_: (0,) * nd)


def _dot_nn(a, b):
    return jnp.dot(a, b, preferred_element_type=f32)


def _dot_nt(a, b):
    return lax.dot_general(a, b, (((1,), (1,)), ((), ())), preferred_element_type=f32)


def _dot_tn(a, b):
    return lax.dot_general(a, b, (((0,), (0,)), ((), ())), preferred_element_type=f32)


_GELU_C = float(np.sqrt(2.0 / np.pi))


def _gelu(x):
    return 0.5 * x * (1.0 + jnp.tanh(_GELU_C * (x + 0.044715 * x * x * x)))


def _gelu_grad(x):
    t = jnp.tanh(_GELU_C * (x + 0.044715 * x * x * x))
    du = _GELU_C * (1.0 + 3.0 * 0.044715 * x * x)
    return 0.5 * (1.0 + t) + 0.5 * x * (1.0 - t * t) * du


def _sigmoid(x):
    return 0.5 * jnp.tanh(0.5 * x) + 0.5


def _rms(x):
    return lax.rsqrt(jnp.mean(x * x, axis=-1, keepdims=True) + NORM_EPS)


def _rms_heads(x):
    return lax.rsqrt(_mean_heads(x * x) + NORM_EPS)


def _mean_heads(v):
    hi = v.astype(bf16)
    lo = (v - hi.astype(f32)).astype(bf16)
    ones = jnp.ones((HEAD_DIM, HEAD_DIM), bf16)
    return (_dot_nn(hi, ones) + _dot_nn(lo, ones)) * (1.0 / HEAD_DIM)


def _mesh_pos():
    return lax.axis_index("x"), lax.axis_index("y"), lax.axis_index("c")


def _in_proj(x, gain, w_in_t, layer, tm):
    T = x.shape[0]

    def body(x_ref, g_ref, w_ref, qkv_ref, su_ref, sv_ref, ga_ref, gb_ref, h_ref):
        xf = x_ref[...]
        h = (xf * _rms(xf) * g_ref[...]).astype(bf16)
        h_ref[...] = h
        qkv_ref[...] = _dot_nt(h, w_ref[0:768, :])
        su_ref[...] = _dot_nt(h, w_ref[768:1280, :]).astype(bf16)
        sv_ref[...] = _dot_nt(h, w_ref[1280:1792, :]).astype(bf16)
        ga_ref[...] = _dot_nt(h, w_ref[1792:2816, :]).astype(bf16)
        gb_ref[...] = _dot_nt(h, w_ref[2816:3840, :]).astype(bf16)

    row = lambda w: pl.BlockSpec((tm, w), lambda i: (i, 0))
    return pl.pallas_call(
        body, name=f"in_proj_{layer}", grid=(T // tm,),
        in_specs=[row(D_MODEL), _full((1, D_MODEL)), _wspec(IN_WIDTH, D_MODEL, layer)],
        out_specs=[row(768), row(512), row(512), row(1024), row(1024), row(D_MODEL)],
        out_shape=[jax.ShapeDtypeStruct((T, 768), f32), jax.ShapeDtypeStruct((T, 512), bf16),
                   jax.ShapeDtypeStruct((T, 512), bf16), jax.ShapeDtypeStruct((T, 1024), bf16),
                   jax.ShapeDtypeStruct((T, 1024), bf16), jax.ShapeDtypeStruct((T, D_MODEL), bf16)],
        compiler_params=_params(("parallel",)),
    )(x, gain, w_in_t)


def _attn_head_group(cur, prev, qg, kg, sink_ref, n, hk):
    lo = hk * HEAD_DIM
    k_raw = jnp.concatenate([prev[:, lo:lo + HEAD_DIM], cur[:, 512 + lo:512 + lo + HEAD_DIM]], axis=0)
    v_band = jnp.concatenate([prev[:, 128 + lo:128 + lo + HEAD_DIM], cur[:, 640 + lo:640 + lo + HEAD_DIM]], axis=0)
    rk = _rms_heads(k_raw)
    k_hat = k_raw * rk
    kn = (k_hat * kg).astype(bf16)
    q_raw = jnp.concatenate(
        [cur[:, (hk * Q_GROUP + g) * HEAD_DIM:(hk * Q_GROUP + g + 1) * HEAD_DIM] for g in range(Q_GROUP)], axis=0)
    rq = _rms_heads(q_raw)
    q_hat = q_raw * rq
    qn = (q_hat * qg * (HEAD_DIM ** -0.5)).astype(bf16)
    s = _dot_nt(qn, kn)
    rows = lax.broadcasted_iota(jnp.int32, (Q_GROUP * BLOCK, 1), 0)
    g_of_row = rows // BLOCK
    qi = rows - g_of_row * BLOCK
    kj = lax.broadcasted_iota(jnp.int32, (1, 2 * BLOCK), 1)
    dist = qi + BLOCK - kj
    valid = (dist >= 0) & (dist < BLOCK) & ((kj >= BLOCK) | (n > 0))
    slope = jnp.zeros((Q_GROUP * BLOCK, 1), f32)
    sink = jnp.zeros((Q_GROUP * BLOCK, 1), f32)
    for g in range(Q_GROUP):
        head = hk * Q_GROUP + g
        slope = jnp.where(g_of_row == g, float(np.exp2(-8.0 * (head + 1.0) / 8.0)), slope)
        sink = jnp.where(g_of_row == g, sink_ref[head], sink)
    s = jnp.where(valid, s - slope * dist.astype(f32), NEG_INF)
    m = jnp.maximum(jnp.max(s, axis=-1, keepdims=True), sink)
    e = jnp.exp(s - m)
    e_sink = jnp.exp(sink - m)
    inv = 1.0 / (jnp.sum(e, axis=-1, keepdims=True) + e_sink)
    return dict(k_raw=k_raw, rk=rk, k_hat=k_hat, kn=kn, v=v_band.astype(bf16), q_hat=q_hat, rq=rq, qn=qn,
                p=e * inv, p_sink=e_sink * inv)


def _attn_fwd(qkv, qg, kg, sinks, n_seq, seq):
    T = n_seq * seq
    nb = seq // BLOCK

    per = 2 if nb % 2 == 0 else 1

    def body(cur_ref, prev_ref, qg_ref, kg_ref, sink_ref, y_ref):
        for sub in range(per):
            n = pl.program_id(1) * per + sub
            cur = cur_ref[sub * BLOCK:(sub + 1) * BLOCK, :]
            prev = prev_ref[...] if sub == 0 else cur_ref[(sub - 1) * BLOCK:sub * BLOCK, 512:768]
            pieces = [None] * (N_KV_HEADS * Q_GROUP)
            for hk in range(N_KV_HEADS):
                a = _attn_head_group(cur, prev, qg_ref[...], kg_ref[...], sink_ref, n, hk)
                o = _dot_nn(a["p"].astype(bf16), a["v"])
                for g in range(Q_GROUP):
                    pieces[hk * Q_GROUP + g] = o[g * BLOCK:(g + 1) * BLOCK]
            y_ref[sub * BLOCK:(sub + 1) * BLOCK, :] = jnp.concatenate(pieces, axis=1).astype(bf16)

    return pl.pallas_call(
        body, name="attn_fwd", grid=(n_seq, nb // per),
        in_specs=[pl.BlockSpec((per * BLOCK, 768), lambda b, n: (b * (nb // per) + n, 0)),
                  pl.BlockSpec((BLOCK, 256), lambda b, n: (b * nb + jnp.maximum(n * per - 1, 0), 2)),
                  _full((1, HEAD_DIM)), _full((1, HEAD_DIM)),
                  pl.BlockSpec(memory_space=pltpu.SMEM)],
        out_specs=pl.BlockSpec((per * BLOCK, ATT_WIDTH), lambda b, n: (b * (nb // per) + n, 0)),
        out_shape=jax.ShapeDtypeStruct((T, ATT_WIDTH), bf16),
        compiler_params=_params(("parallel", "parallel")),
    )(qkv, qkv, qg, kg, sinks)


def _sgu_chunk(su, sv, gain, w_ref, b_ref):
    u = _gelu(su)
    vg = _gelu(sv)
    rv = _rms(vg)
    v_hat = vg * rv
    vn = (v_hat * gain).astype(bf16)
    causal = (lax.broadcasted_iota(jnp.int32, (BLOCK, BLOCK), 0) >= lax.broadcasted_iota(jnp.int32, (BLOCK, BLOCK), 1))
    w_tril = [jnp.where(causal, w_ref[g], 0.0).astype(bf16) for g in range(SGU_GROUPS)]
    gd = SGU_WIDTH // SGU_GROUPS
    mixed = jnp.concatenate(
        [_dot_nn(w_tril[g], vn[:, g * gd:(g + 1) * gd]) + b_ref[g] for g in range(SGU_GROUPS)], axis=1)
    return u, rv, v_hat, vn, w_tril, mixed


def _sgu_fwd(su, sv, gain, w_s, b_s, tm):
    T = su.shape[0]

    def body(su_ref, sv_ref, g_ref, w_ref, b_ref, y_ref):
        for ch in range(tm // BLOCK):
            rows = slice(ch * BLOCK, (ch + 1) * BLOCK)
            u, _, _, _, _, mixed = _sgu_chunk(su_ref[rows, :].astype(f32), sv_ref[rows, :].astype(f32),
                                              g_ref[...], w_ref, b_ref)
            y_ref[rows, :] = (u * mixed).astype(bf16)

    row = pl.BlockSpec((tm, SGU_WIDTH), lambda i: (i, 0))
    return pl.pallas_call(
        body, name="sgu_fwd", grid=(T // tm,),
        in_specs=[row, row, _full((1, SGU_WIDTH)), _full((SGU_GROUPS, BLOCK, BLOCK)), _full((SGU_GROUPS, BLOCK, 1))],
        out_specs=row, out_shape=jax.ShapeDtypeStruct((T, SGU_WIDTH), bf16),
        compiler_params=_params(("parallel",)),
    )(su, sv, gain, w_s, b_s)


def _merge_fwd(x, y_att, y_sgu, ga, gb, w_oa_t, w_ob_t, w_out, layer, tm):
    T = x.shape[0]

    def body(x_ref, ya_ref, ys_ref, ga_ref, gb_ref, woa_ref, wob_ref, wout_ref, x1_ref, m_ref, a_ref, b_ref):
        a = _dot_nt(ya_ref[...], woa_ref[...])
        b = _dot_nt(ys_ref[...], wob_ref[...])
        a_ref[...] = a.astype(bf16)
        b_ref[...] = b.astype(bf16)
        merged = (_sigmoid(ga_ref[...].astype(f32)) * a + _sigmoid(gb_ref[...].astype(f32)) * b).astype(bf16)
        m_ref[...] = merged
        x1_ref[...] = x_ref[...] + _dot_nn(merged, wout_ref[...])

    row = lambda w: pl.BlockSpec((tm, w), lambda i: (i, 0))
    return pl.pallas_call(
        body, name=f"merge_fwd_{layer}", grid=(T // tm,),
        in_specs=[row(D_MODEL), row(512), row(512), row(1024), row(1024),
                  _wspec(D_MODEL, ATT_WIDTH, layer), _wspec(D_MODEL, SGU_WIDTH, layer), _wspec(D_MODEL, D_MODEL, layer)],
        out_specs=[row(D_MODEL)] * 4,
        out_shape=[jax.ShapeDtypeStruct((T, D_MODEL), f32)] + [jax.ShapeDtypeStruct((T, D_MODEL), bf16)] * 3,
        compiler_params=_params(("parallel",)),
    )(x, y_att, y_sgu, ga, gb, w_oa_t, w_ob_t, w_out)


def _tile_permutation(tm):
    r = np.arange(tm)
    p = np.zeros((tm, tm), np.float32)
    p[r, (r % 8) * (tm // 8) + r // 8] = 1.0
    return jnp.asarray(p, bf16), jnp.asarray(p.T, bf16)


def _stage_taps_before(buf, zz, prev, tm):
    first = lax.broadcasted_iota(jnp.int32, (8, 1), 0) == 0
    buf[16:16 + tm, :] = zz
    buf[0:8, :] = jnp.where(first, prev[7:8], pltpu.roll(buf[tm:tm + 8, :], 1, 0))
    buf[8:16, :] = jnp.where(first, prev[15:16], pltpu.roll(buf[tm + 8:tm + 16, :], 1, 0))


def _stage_taps_after(buf, nxt, tm):
    last = lax.broadcasted_iota(jnp.int32, (8, 1), 0) == 7
    buf[tm:tm + 8, :] = jnp.where(last, nxt[0:1], pltpu.roll(buf[0:8, :], 7, 0))
    buf[tm + 8:tm + 16, :] = jnp.where(last, nxt[8:9], pltpu.roll(buf[8:16, :], 7, 0))


def _conv_rows(buf, r, n, coef):
    z2 = buf[pl.ds(r, n), :]
    z1 = buf[pl.ds(pl.multiple_of(r + 8, 8), n), :]
    z0 = buf[pl.ds(pl.multiple_of(r + 16, 8), n), :]
    return coef[0] + coef[1] * z2 + coef[2] * z1 + coef[3] * z0


def _ffn_up(x1, gain, w_up_t, conv_w, conv_b, layer, seq, tm):
    T = x1.shape[0]
    tps = seq // tm
    perm, perm_t = _tile_permutation(tm)

    rg = 16

    def body(x_ref, g_ref, w_ref, cw_ref, cb_ref, p_ref, pt_ref, h2_ref, z_ref, act_ref, carry_ref,
             zg_buf, zv_buf, actp_buf):
        i = pl.program_id(0)

        @pl.when(i % tps == 0)
        def _():
            carry_ref[...] = jnp.zeros_like(carry_ref)

        xf = x_ref[...]
        h2 = (xf * _rms(xf) * g_ref[...]).astype(bf16)
        h2_ref[...] = h2
        h2p = _dot_nn(p_ref[...], h2).astype(bf16)
        for cc in range(D_FF // FF_CHUNK):
            cols_g = slice(cc * FF_CHUNK, (cc + 1) * FF_CHUNK)
            cols_v = slice(D_FF + cc * FF_CHUNK, D_FF + (cc + 1) * FF_CHUNK)
            for buf, cols in ((zg_buf, cols_g), (zv_buf, cols_v)):
                zb = _dot_nt(h2p, w_ref[cols, :]).astype(bf16)
                z_ref[:, cols] = zb
                _stage_taps_before(buf, zb.astype(f32), carry_ref[:, cols], tm)
                carry_ref[:, cols] = buf[tm:tm + 16, :]
            coef = [jnp.broadcast_to(v, (rg, FF_CHUNK)) for cols in (cols_g, cols_v)
                    for v in (cb_ref[:, cols], cw_ref[0:1, cols], cw_ref[1:2, cols], cw_ref[2:3, cols])]

            def rows_step(j, carry, coef=coef):
                r = pl.multiple_of(j * rg, rg)
                zcg, zcv = (_conv_rows(buf, r, rg, coef[4 * k:4 * k + 4]) for k, buf in enumerate((zg_buf, zv_buf)))
                actp_buf[pl.ds(r, rg), :] = (zcg * _sigmoid(zcg) * zcv).astype(bf16)
                return carry

            lax.fori_loop(0, tm // rg, rows_step, 0, unroll=True)
            act_ref[:, cols_g] = _dot_nn(pt_ref[...], actp_buf[...]).astype(bf16)

    row = lambda w: pl.BlockSpec((tm, w), lambda i: (i, 0))
    return pl.pallas_call(
        body, name=f"ffn_up_{layer}", grid=(T // tm,),
        in_specs=[row(D_MODEL), _full((1, D_MODEL)), _wspec(2 * D_FF, D_MODEL, layer),
                  _full((3, 2 * D_FF)), _full((1, 2 * D_FF)), _full((tm, tm)), _full((tm, tm))],
        out_specs=[row(D_MODEL), row(2 * D_FF), row(D_FF)],
        out_shape=[jax.ShapeDtypeStruct((T, D_MODEL), bf16), jax.ShapeDtypeStruct((T, 2 * D_FF), bf16),
                   jax.ShapeDtypeStruct((T, D_FF), bf16)],
        scratch_shapes=[pltpu.VMEM((16, 2 * D_FF), f32), pltpu.VMEM((tm + 16, FF_CHUNK), f32),
                        pltpu.VMEM((tm + 16, FF_CHUNK), f32), pltpu.VMEM((tm, FF_CHUNK), bf16)],
        compiler_params=_params(("arbitrary",)),
    )(x1, gain, w_up_t, conv_w, conv_b, perm, perm_t)


def _ffn_down(x1, act, w_down, after, layer, tm):
    T = x1.shape[0]

    def body(x_ref, a_ref, w_ref, after_ref, o_ref):
        o_ref[...] = x_ref[...] + _dot_nn(a_ref[...], w_ref[...])

    row = lambda w: pl.BlockSpec((tm, w), lambda i: (i, 0))
    return pl.pallas_call(
        body, name=f"ffn_down_{layer}", grid=(T // tm,),
        in_specs=[row(D_MODEL), row(D_FF), _wspec(D_FF, D_MODEL, layer), _full((1, 1))],
        out_specs=row(D_MODEL), out_shape=jax.ShapeDtypeStruct((T, D_MODEL), f32),
        compiler_params=_params(("parallel",)),
    )(x1, act, w_down, after)


def _ffn_down_loss(x1, act, w_down, target, layer, tm):
    T = x1.shape[0]

    def body(x_ref, a_ref, w_ref, t_ref, dy_ref, dyb_ref, loss_ref):
        @pl.when(pl.program_id(0) == 0)
        def _():
            loss_ref[...] = jnp.zeros_like(loss_ref)

        diff = x_ref[...] + _dot_nn(a_ref[...], w_ref[...]) - t_ref[...]
        loss_ref[...] += 0.5 * jnp.sum(jnp.mean(diff * diff, axis=-1, keepdims=True), axis=0, keepdims=True)
        dy = diff * (1.0 / D_MODEL)
        dy_ref[...] = dy
        dyb_ref[...] = dy.astype(bf16)

    row = lambda w: pl.BlockSpec((tm, w), lambda i: (i, 0))
    return pl.pallas_call(
        body, name=f"ffn_down_loss_{layer}", grid=(T // tm,),
        in_specs=[row(D_MODEL), row(D_FF), _wspec(D_FF, D_MODEL, layer), row(D_MODEL)],
        out_specs=[row(D_MODEL), row(D_MODEL), _full((8, 128))],
        out_shape=[jax.ShapeDtypeStruct((T, D_MODEL), f32), jax.ShapeDtypeStruct((T, D_MODEL), bf16),
                   jax.ShapeDtypeStruct((8, 128), f32)],
        compiler_params=_params(("arbitrary",)),
    )(x1, act, w_down, target)


def _ffn_bwd(dx2b, z, conv_w, conv_b, w_down, layer, seq, tm):
    T = z.shape[0]
    nt = T // tm
    tps = seq // tm

    perm, perm_t = _tile_permutation(tm)

    def body(dx_ref, z_ref, zh_ref, cw_ref, cb_ref, wd_ref, p_ref, pt_ref, dz_ref, dconv_ref, carry_ref,
             zg_buf, zv_buf, gg_buf, gv_buf, dact_buf, dzp_buf):
        i = pl.program_id(0)
        pos = (nt - 1 - i) % tps

        @pl.when(i == 0)
        def _():
            dconv_ref[...] = jnp.zeros_like(dconv_ref)

        @pl.when(pos == tps - 1)
        def _():
            carry_ref[...] = jnp.zeros_like(carry_ref)

        dxp = _dot_nn(p_ref[...], dx_ref[...]).astype(bf16)
        halo_on = (pos > 0).astype(f32)
        for cc in range(D_FF // FF_CHUNK):
            cols_g = slice(cc * FF_CHUNK, (cc + 1) * FF_CHUNK)
            cols_v = slice(D_FF + cc * FF_CHUNK, D_FF + (cc + 1) * FF_CHUNK)
            for buf, cols in ((zg_buf, cols_g), (zv_buf, cols_v)):
                _stage_taps_before(buf, z_ref[:, cols].astype(f32), zh_ref[:, cols].astype(f32) * halo_on, tm)
            dact_buf[...] = _dot_nt(dxp, wd_ref[cols_g, :])
            coef = [jnp.broadcast_to(v, (8, FF_CHUNK)) for cols in (cols_g, cols_v)
                    for v in (cb_ref[:, cols], cw_ref[0:1, cols], cw_ref[1:2, cols], cw_ref[2:3, cols])]

            def first_pass(j, sums, coef=coef):
                r = pl.multiple_of(j * 8, 8)
                rows = pl.ds(r, 8)
                zcg = _conv_rows(zg_buf, r, 8, coef[0:4])
                zcv = _conv_rows(zv_buf, r, 8, coef[4:8])
                sg = _sigmoid(zcg)
                silu = zcg * sg
                d_act = dact_buf[rows, :]
                dg = d_act * zcv * sg * (1.0 + zcg * (1.0 - sg))
                dv = d_act * silu
                gg_buf[rows, :] = dg
                gv_buf[rows, :] = dv
                out = []
                for k, (g, buf) in enumerate(((dg, zg_buf), (dv, zv_buf))):
                    out += [sums[4 * k] + g * buf[rows, :],
                            sums[4 * k + 1] + g * buf[pl.ds(pl.multiple_of(r + 8, 8), 8), :],
                            sums[4 * k + 2] + g * buf[pl.ds(pl.multiple_of(r + 16, 8), 8), :],
                            sums[4 * k + 3] + g]
                return tuple(out)

            sums = lax.fori_loop(0, tm // 8, first_pass, tuple(jnp.zeros((8, FF_CHUNK), f32) for _ in range(8)),
                                 unroll=True)
            for k, cols in enumerate((cols_g, cols_v)):
                for tap in range(4):
                    dconv_ref[tap:tap + 1, cols] += jnp.sum(sums[4 * k + tap], axis=0, keepdims=True)
            for buf, cols in ((gg_buf, cols_g), (gv_buf, cols_v)):
                _stage_taps_after(buf, carry_ref[:, cols], tm)
                carry_ref[:, cols] = buf[0:16, :]
                w0, w1, w2 = (jnp.broadcast_to(cw_ref[k:k + 1, cols], (16, FF_CHUNK)) for k in range(3))

                def second_pass(j, carry, buf=buf, w0=w0, w1=w1, w2=w2):
                    r = pl.multiple_of(j * 16, 16)
                    dzp_buf[pl.ds(r, 16), :] = (w2 * buf[pl.ds(r, 16), :] + w1 * buf[pl.ds(pl.multiple_of(r + 8, 8), 16), :]
                                                + w0 * buf[pl.ds(pl.multiple_of(r + 16, 16), 16), :]).astype(bf16)
                    return carry

                lax.fori_loop(0, tm // 16, second_pass, 0, unroll=True)
                dz_ref[:, cols] = _dot_nn(pt_ref[...], dzp_buf[...]).astype(bf16)

    rev = lambda w: pl.BlockSpec((tm, w), lambda i: (nt - 1 - i, 0))
    return pl.pallas_call(
        body, name=f"ffn_bwd_{layer}", grid=(nt,),
        in_specs=[rev(D_MODEL), rev(2 * D_FF),
                  pl.BlockSpec((16, 2 * D_FF), lambda i: (jnp.maximum((nt - 1 - i) * (tm // 16) - 1, 0), 0)),
                  _full((3, 2 * D_FF)), _full((1, 2 * D_FF)), _wspec(D_FF, D_MODEL, layer),
                  _full((tm, tm)), _full((tm, tm))],
        out_specs=[rev(2 * D_FF), _full((8, 2 * D_FF))],
        out_shape=[jax.ShapeDtypeStruct((T, 2 * D_FF), bf16), jax.ShapeDtypeStruct((8, 2 * D_FF), f32)],
        scratch_shapes=[pltpu.VMEM((16, 2 * D_FF), f32)] + [pltpu.VMEM((tm + 16, FF_CHUNK), f32)] * 4
        + [pltpu.VMEM((tm, FF_CHUNK), f32), pltpu.VMEM((tm, FF_CHUNK), bf16)],
        compiler_params=_params(("arbitrary",)),
    )(dx2b, z, z, conv_w, conv_b, w_down, perm, perm_t)


def _norm_bwd(dys, w, layer, x, gain, dres, tm, name):
    T = dys[0].shape[0]
    widths = [d.shape[1] for d in dys]
    K = sum(widths)
    n = len(dys)

    def body(*refs):
        dy_refs = refs[:n]
        w_ref, x_ref, g_ref, dres_ref, dx_ref, dxb_ref, dg_ref = refs[n:]

        @pl.when(pl.program_id(0) == 0)
        def _():
            dg_ref[...] = jnp.zeros_like(dg_ref)

        dh, lo = None, 0
        for dy_ref, wd in zip(dy_refs, widths):
            part = _dot_nn(dy_ref[...], w_ref[lo:lo + wd, :])
            dh = part if dh is None else dh + part
            lo += wd
        xf = x_ref[...]
        r = _rms(xf)
        x_hat = xf * r
        dg_ref[...] += jnp.sum(dh * x_hat, axis=0, keepdims=True)
        dxh = dh * g_ref[...]
        dx = dres_ref[...] + r * (dxh - x_hat * jnp.mean(dxh * x_hat, axis=-1, keepdims=True))
        dx_ref[...] = dx
        dxb_ref[...] = dx.astype(bf16)

    row = lambda w_: pl.BlockSpec((tm, w_), lambda i: (i, 0))
    return pl.pallas_call(
        body, name=name, grid=(T // tm,),
        in_specs=[row(wd) for wd in widths] + [_wspec(K, D_MODEL, layer), row(D_MODEL), _full((1, D_MODEL)), row(D_MODEL)],
        out_specs=[row(D_MODEL), row(D_MODEL), _full((1, D_MODEL))],
        out_shape=[jax.ShapeDtypeStruct((T, D_MODEL), f32), jax.ShapeDtypeStruct((T, D_MODEL), bf16),
                   jax.ShapeDtypeStruct((1, D_MODEL), f32)],
        compiler_params=_params(("arbitrary",)),
    )(*dys, w, x, gain, dres)


def _merge_bwd(dx1b, ga, gb, a, b, w_oa_t, w_ob_t, w_out, layer, tm):
    T = dx1b.shape[0]

    def body(dx_ref, ga_ref, gb_ref, a_ref, b_ref, woa_ref, wob_ref, wout_ref,
             da_ref, db_ref, dga_ref, dgb_ref, dya_ref, dys_ref):
        dm = _dot_nt(dx_ref[...], wout_ref[...])
        sa = _sigmoid(ga_ref[...].astype(f32))
        sb = _sigmoid(gb_ref[...].astype(f32))
        da = (dm * sa).astype(bf16)
        db = (dm * sb).astype(bf16)
        da_ref[...] = da
        db_ref[...] = db
        dga_ref[...] = (dm * a_ref[...].astype(f32) * sa * (1.0 - sa)).astype(bf16)
        dgb_ref[...] = (dm * b_ref[...].astype(f32) * sb * (1.0 - sb)).astype(bf16)
        dya_ref[...] = _dot_nn(da, woa_ref[...]).astype(bf16)
        dys_ref[...] = _dot_nn(db, wob_ref[...]).astype(bf16)

    row = lambda w: pl.BlockSpec((tm, w), lambda i: (i, 0))
    return pl.pallas_call(
        body, name=f"merge_bwd_{layer}", grid=(T // tm,),
        in_specs=[row(D_MODEL)] * 5 + [_wspec(D_MODEL, ATT_WIDTH, layer), _wspec(D_MODEL, SGU_WIDTH, layer),
                                       _wspec(D_MODEL, D_MODEL, layer)],
        out_specs=[row(D_MODEL)] * 4 + [row(512)] * 2,
        out_shape=[jax.ShapeDtypeStruct((T, D_MODEL), bf16)] * 4 + [jax.ShapeDtypeStruct((T, 512), bf16)] * 2,
        compiler_params=_params(("parallel",)),
    )(dx1b, ga, gb, a, b, w_oa_t, w_ob_t, w_out)


def _sgu_bwd(dy, su, sv, gain, w_s, b_s, tm):
    T = su.shape[0]
    gd = SGU_WIDTH // SGU_GROUPS

    def body(dy_ref, su_ref, sv_ref, g_ref, w_ref, b_ref, dsu_ref, dsv_ref, dw_ref, db_ref, dg_ref):
        @pl.when(pl.program_id(0) == 0)
        def _():
            dw_ref[...] = jnp.zeros_like(dw_ref)
            db_ref[...] = jnp.zeros_like(db_ref)
            dg_ref[...] = jnp.zeros_like(dg_ref)

        gain_v = g_ref[...]
        for ch in range(tm // BLOCK):
            rows = slice(ch * BLOCK, (ch + 1) * BLOCK)
            su_c = su_ref[rows, :].astype(f32)
            sv_c = sv_ref[rows, :].astype(f32)
            u, rv, v_hat, vn, w_tril, mixed = _sgu_chunk(su_c, sv_c, gain_v, w_ref, b_ref)
            dyc = dy_ref[rows, :].astype(f32)
            dsu_ref[rows, :] = (dyc * mixed * _gelu_grad(su_c)).astype(bf16)
            dmix = dyc * u
            dmix_b = dmix.astype(bf16)
            dvn = []
            for g in range(SGU_GROUPS):
                gs = slice(g * gd, (g + 1) * gd)
                db_ref[g] += jnp.sum(dmix[:, gs], axis=1, keepdims=True)
                dw_ref[g] += _dot_nt(dmix_b[:, gs], vn[:, gs])
                dvn.append(_dot_tn(w_tril[g], dmix_b[:, gs]))
            dvn = jnp.concatenate(dvn, axis=1)
            dg_ref[...] += jnp.sum(dvn * v_hat, axis=0, keepdims=True)
            dxh = dvn * gain_v
            dvg = rv * (dxh - v_hat * jnp.mean(dxh * v_hat, axis=-1, keepdims=True))
            dsv_ref[rows, :] = (dvg * _gelu_grad(sv_c)).astype(bf16)

    row = pl.BlockSpec((tm, SGU_WIDTH), lambda i: (i, 0))
    return pl.pallas_call(
        body, name="sgu_bwd", grid=(T // tm,),
        in_specs=[row, row, row, _full((1, SGU_WIDTH)), _full((SGU_GROUPS, BLOCK, BLOCK)),
                  _full((SGU_GROUPS, BLOCK, 1))],
        out_specs=[row, row, _full((SGU_GROUPS, BLOCK, BLOCK)), _full((SGU_GROUPS, BLOCK, 1)), _full((1, SGU_WIDTH))],
        out_shape=[jax.ShapeDtypeStruct((T, SGU_WIDTH), bf16)] * 2 + [
            jax.ShapeDtypeStruct((SGU_GROUPS, BLOCK, BLOCK), f32), jax.ShapeDtypeStruct((SGU_GROUPS, BLOCK, 1), f32),
            jax.ShapeDtypeStruct((1, SGU_WIDTH), f32)],
        compiler_params=_params(("arbitrary",)),
    )(dy, su, sv, gain, w_s, b_s)


def _attn_bwd(dy, qkv, qg, kg, sinks, n_seq, seq):
    T = n_seq * seq
    nb = seq // BLOCK
    scale = HEAD_DIM ** -0.5
    per = 1
    ng = nb // per

    def body(dy_ref, cur_ref, prev_ref, qg_ref, kg_ref, sink_ref, dqkv_ref, dqg_ref, dkg_ref, dsink_ref,
             carry_k, carry_v):
        b = pl.program_id(0)
        j = pl.program_id(1)

        @pl.when((b == 0) & (j == 0))
        def _():
            dqg_ref[...] = jnp.zeros_like(dqg_ref)
            dkg_ref[...] = jnp.zeros_like(dkg_ref)
            dsink_ref[...] = jnp.zeros_like(dsink_ref)

        @pl.when(j == 0)
        def _():
            carry_k[...] = jnp.zeros_like(carry_k)
            carry_v[...] = jnp.zeros_like(carry_v)

        for sub in reversed(range(per)):
            rows = slice(sub * BLOCK, (sub + 1) * BLOCK)
            prev = prev_ref[...] if sub == 0 else cur_ref[(sub - 1) * BLOCK:sub * BLOCK, 512:768]
            one_block(dy_ref[rows, :].astype(f32), cur_ref[rows, :], prev, (ng - 1 - j) * per + sub,
                      qg_ref[...], kg_ref[...], sink_ref, dqkv_ref.at[rows, :], dqg_ref, dkg_ref, dsink_ref,
                      carry_k, carry_v)

    def one_block(dyf, cur, prev, n, qg_v, kg_v, sink_ref, dqkv_ref, dqg_ref, dkg_ref, dsink_ref, carry_k, carry_v):
        dq_pieces = [None] * (N_KV_HEADS * Q_GROUP)
        dk_pieces, dv_pieces = [], []
        for hk in range(N_KV_HEADS):
            a = _attn_head_group(cur, prev, qg_v, kg_v, sink_ref, n, hk)
            do = jnp.concatenate(
                [dyf[:, (hk * Q_GROUP + g) * HEAD_DIM:(hk * Q_GROUP + g + 1) * HEAD_DIM] for g in range(Q_GROUP)],
                axis=0).astype(bf16)
            p = a["p"]
            dp = _dot_nt(do, a["v"])
            dv_band = _dot_tn(p.astype(bf16), do)
            dsum = jnp.sum(p * dp, axis=-1, keepdims=True)
            ds = (p * (dp - dsum)).astype(bf16)
            dsink_col = -a["p_sink"] * dsum
            for g in range(Q_GROUP):
                head = hk * Q_GROUP + g
                dsink_ref[head:head + 1, :] += jnp.sum(dsink_col[g * BLOCK:(g + 1) * BLOCK], axis=0, keepdims=True)
            dqn = _dot_nn(ds, a["kn"])
            dkn_band = _dot_tn(ds, a["qn"])
            dq_hat_g = dqn * scale
            dqg_ref[...] += jnp.sum(dq_hat_g * a["q_hat"], axis=0, keepdims=True)
            dxh = dq_hat_g * qg_v
            dq = a["rq"] * (dxh - a["q_hat"] * _mean_heads(dxh * a["q_hat"]))
            for g in range(Q_GROUP):
                dq_pieces[hk * Q_GROUP + g] = dq[g * BLOCK:(g + 1) * BLOCK]
            dkn = dkn_band[BLOCK:] + carry_k[hk]
            dv_pieces.append(dv_band[BLOCK:] + carry_v[hk])
            carry_k[hk] = dkn_band[:BLOCK]
            carry_v[hk] = dv_band[:BLOCK]
            k_hat = a["k_hat"][BLOCK:]
            dkg_ref[...] += jnp.sum(dkn * k_hat, axis=0, keepdims=True)
            dxk = dkn * kg_v
            dk_pieces.append(a["rk"][BLOCK:] * (dxk - k_hat * _mean_heads(dxk * k_hat)))
        dqkv_ref[...] = jnp.concatenate(dq_pieces + dk_pieces + dv_pieces, axis=1).astype(bf16)

    blk = lambda w: pl.BlockSpec((per * BLOCK, w), lambda b, j: (b * ng + ng - 1 - j, 0))
    return pl.pallas_call(
        body, name="attn_bwd", grid=(n_seq, ng),
        in_specs=[blk(ATT_WIDTH), blk(768),
                  pl.BlockSpec((BLOCK, 256), lambda b, j: (b * nb + jnp.maximum((ng - 1 - j) * per - 1, 0), 2)),
                  _full((1, HEAD_DIM)), _full((1, HEAD_DIM)), pl.BlockSpec(memory_space=pltpu.SMEM)],
        out_specs=[blk(768), _full((1, HEAD_DIM)), _full((1, HEAD_DIM)), _full((8, 128))],
        out_shape=[jax.ShapeDtypeStruct((T, 768), bf16), jax.ShapeDtypeStruct((1, HEAD_DIM), f32),
                   jax.ShapeDtypeStruct((1, HEAD_DIM), f32), jax.ShapeDtypeStruct((8, 128), f32)],
        scratch_shapes=[pltpu.VMEM((N_KV_HEADS, BLOCK, HEAD_DIM), f32), pltpu.VMEM((N_KV_HEADS, BLOCK, HEAD_DIM), f32)],
        compiler_params=_params(("arbitrary", "arbitrary")),
    )(dy, qkv, qkv, qg, kg, sinks)


def _weight_grad(a, b, tm, tk, name):
    T, M = a.shape
    N = b.shape[1]
    nk = T // tk

    def body(a_ref, b_ref, o_ref, acc_ref):
        k = pl.program_id(1)

        @pl.when(k == 0)
        def _():
            acc_ref[...] = jnp.zeros_like(acc_ref)

        acc_ref[...] += _dot_tn(a_ref[...], b_ref[...])

        @pl.when(k == nk - 1)
        def _():
            o_ref[...] = acc_ref[...].astype(bf16)

    return pl.pallas_call(
        body, name=name, grid=(M // tm, nk),
        in_specs=[pl.BlockSpec((tk, tm), lambda i, k: (k, i)), pl.BlockSpec((tk, N), lambda i, k: (k, 0))],
        out_specs=pl.BlockSpec((None, tm, N), lambda i, k: (0, i, 0)),
        out_shape=jax.ShapeDtypeStruct((1, M, N), bf16),
        scratch_shapes=[pltpu.VMEM((tm, N), f32)],
        compiler_params=_params(("parallel", "arbitrary")),
    )(a, b)


def _weight_grad_rows(a_list, b, tk, name):
    T, N = b.shape
    widths = [a.shape[1] for a in a_list]
    M = sum(widths)
    nk = T // tk
    n = len(a_list)

    def body(*refs):
        a_refs = refs[:n]
        b_ref, o_ref, acc_ref = refs[n:]
        k = pl.program_id(0)

        @pl.when(k == 0)
        def _():
            acc_ref[...] = jnp.zeros_like(acc_ref)

        lo = 0
        for a_ref, wd in zip(a_refs, widths):
            acc_ref[lo:lo + wd, :] += _dot_tn(a_ref[...], b_ref[...])
            lo += wd

        @pl.when(k == nk - 1)
        def _():
            o_ref[...] = acc_ref[...].astype(bf16)

    return pl.pallas_call(
        body, name=name, grid=(nk,),
        in_specs=[pl.BlockSpec((tk, wd), lambda k: (k, 0)) for wd in widths] + [pl.BlockSpec((tk, N), lambda k: (k, 0))],
        out_specs=pl.BlockSpec((None, M, N), lambda k: (0, 0, 0), pipeline_mode=pl.Buffered(1)),
        out_shape=jax.ShapeDtypeStruct((1, M, N), bf16),
        scratch_shapes=[pltpu.VMEM((M, N), f32)],
        compiler_params=_params(("arbitrary",)),
    )(*a_list, b)


def _place(src, layer, src_slot, n_slots, dst_slot, dtype, name, after=None):
    _, _, rows, cols = src.shape
    slots = jnp.stack([src_slot, dst_slot]).astype(jnp.int32)

    def body(slots_ref, s_ref, *rest):
        rest[-1][...] = s_ref[...].astype(dtype)

    return pl.pallas_call(
        body, name=name,
        grid_spec=pltpu.PrefetchScalarGridSpec(
            num_scalar_prefetch=1, grid=(1,),
            in_specs=[pl.BlockSpec((None, None, rows, cols), lambda i, sl: (layer, sl[0], 0, 0))]
            + ([] if after is None else [ANY]),
            out_specs=pl.BlockSpec((None, rows, cols), lambda i, sl: (sl[1], 0, 0))),
        out_shape=jax.ShapeDtypeStruct((n_slots, rows, cols), dtype),
        compiler_params=_params(("arbitrary",)),
    )(slots, src, *([] if after is None else [after]))


HBM = pl.BlockSpec(memory_space=pltpu.HBM)
SEM = pl.BlockSpec(memory_space=pltpu.SEMAPHORE)
DATAFLOW = pltpu.SideEffectType.DATAFLOW_SIDE_EFFECTING


def _other_chips(x, y):
    return [(1 - x, y), (x, 1 - y), (1 - x, 1 - y)]


def _split_start(groups, name):
    nb = [len(bufs) for bufs, _ in groups]
    flat = [b for bufs, _ in groups for b in bufs]
    ns = [len(plan(bufs, dry=True)) for bufs, plan in groups]
    ng = len(groups)

    def body(*refs):
        n_in = len(flat)
        sems = refs[n_in:n_in + 2 * ng]
        thru = refs[n_in + 2 * ng:2 * n_in + 2 * ng]
        token = refs[2 * n_in + 2 * ng]
        off = 0
        for g, (bufs, plan) in enumerate(groups):
            mine = thru[off:off + nb[g]]
            off += nb[g]
            for k, (src, dst, to) in enumerate(plan(mine)):
                pltpu.make_async_remote_copy(
                    src_ref=src, dst_ref=dst, send_sem=sems[2 * g].at[k], recv_sem=sems[2 * g + 1].at[k],
                    device_id=to, device_id_type=MESH).start()
        token[...] = jnp.zeros_like(token)

    out_shape = []
    for n in ns:
        out_shape += [pltpu.SemaphoreType.DMA((n,)), pltpu.SemaphoreType.DMA((n,))]
    out_shape += [pltpu.HBM(b.shape, b.dtype) for b in flat]
    out_shape.append(jax.ShapeDtypeStruct((8, 128), f32))
    res = pl.pallas_call(
        body, name=name, out_shape=tuple(out_shape),
        in_specs=[HBM] * len(flat),
        out_specs=tuple([SEM] * (2 * ng) + [HBM] * len(flat) + [pl.BlockSpec(memory_space=pltpu.VMEM)]),
        input_output_aliases={i: 2 * ng + i for i in range(len(flat))},
        compiler_params=pltpu.CompilerParams(has_side_effects=DATAFLOW),
    )(*[pltpu.with_memory_space_constraint(b, pltpu.HBM) for b in flat])
    out, off = [], 2 * ng
    for g in range(ng):
        out.append((res[2 * g], res[2 * g + 1], list(res[off:off + nb[g]])))
        off += nb[g]
    return out, res[-1]


def _split_wait(bufs, send, recv, plan, after, name):
    nb = len(bufs)

    def body(*refs):
        thru = refs[:nb]
        send_ref, recv_ref = refs[nb], refs[nb + 1]
        for k, (src, dst, to) in enumerate(plan(thru)):
            cp = pltpu.make_async_remote_copy(
                src_ref=src, dst_ref=dst, send_sem=send_ref.at[k], recv_sem=recv_ref.at[k],
                device_id=to, device_id_type=MESH)
            cp.wait_send()
            cp.wait_recv()

    res = pl.pallas_call(
        body, name=name, out_shape=tuple(pltpu.HBM(b.shape, b.dtype) for b in bufs),
        in_specs=[HBM] * nb + [SEM, SEM, ANY], out_specs=tuple([HBM] * nb),
        input_output_aliases={i: i for i in range(nb)},
        compiler_params=pltpu.CompilerParams(has_side_effects=DATAFLOW),
    )(*bufs, send, recv, after)
    return list(res)


def _gather_plan(hrs, n_direct=0):
    def plan(refs, dry=False):
        if dry:
            return [None] * (4 * len(hrs) + 3 * n_direct)
        x, y, c = _mesh_pos()
        me = 4 * x + 2 * y + c
        out = []
        for i in range(n_direct):
            src, land = refs[len(hrs) + 2 * i], refs[len(hrs) + 2 * i + 1]
            out += [(src, land.at[2 * x + y], (*chip, c)) for chip in _other_chips(x, y)]
        for ref, hr in zip(refs, hrs):
            rows = ref.at[pl.ds(pl.multiple_of(me * hr, 16), hr), :]
            out.append((rows, rows, (x, y, 1 - c)))
            out += [(rows, rows, (*chip, c)) for chip in _other_chips(x, y)]
        return out
    return plan


def _pass_plan(hrs):
    def plan(refs, dry=False):
        if dry:
            return [None] * (3 * len(hrs))
        x, y, c = _mesh_pos()
        out = []
        for ref, hr in zip(refs, hrs):
            for chip in _other_chips(x, y):
                rows = ref.at[pl.ds(pl.multiple_of((4 * chip[0] + 2 * chip[1] + c) * hr, 16), hr), :]
                out.append((rows, rows, (x, y, 1 - c)))
        return out
    return plan


def _pair_plan(hrs):
    n = len(hrs)

    def plan(refs, dry=False):
        if dry:
            return [None] * (N_CHIP * n)
        x, y, c = _mesh_pos()
        out = []
        for r in range(n):
            for j in range(N_CHIP):
                start = pl.multiple_of((2 * j + 1 - c) * hrs[r], 16)
                out.append((refs[r].at[0, pl.ds(start, hrs[r]), :], refs[n + r].at[0, j], (x, y, 1 - c)))
        return out
    return plan


def _all_to_all_plan(n):
    def plan(refs, dry=False):
        if dry:
            return [None] * (7 * n)
        x, y, c = _mesh_pos()
        out = []
        for ref in refs:
            mine = ref.at[4 * x + 2 * y + c]
            for fx in range(2):
                for fy in range(2):
                    for fc in range(2):
                        if fx or fy or fc:
                            out.append((mine, mine, (1 - x if fx else x, 1 - y if fy else y, 1 - c if fc else c)))
        return out
    return plan


def _pass_to_sibling(bufs, hrs, name):
    nb = len(bufs)

    def body(*refs):
        out = refs[nb:2 * nb]
        send, recv = refs[2 * nb:]
        x, y, c = _mesh_pos()
        chips = _other_chips(x, y)
        started = []
        for i in range(nb):
            for j, chip in enumerate(chips):
                rows = out[i].at[pl.ds(pl.multiple_of((4 * chip[0] + 2 * chip[1] + c) * hrs[i], 16), hrs[i]), :]
                cp = pltpu.make_async_remote_copy(
                    src_ref=rows, dst_ref=rows, send_sem=send.at[3 * i + j], recv_sem=recv.at[3 * i + j],
                    device_id=(x, y, 1 - c), device_id_type=MESH)
                cp.start()
                started.append(cp)
        for i in range(nb):
            for j, chip in enumerate(chips):
                rows = out[i].at[pl.ds(pl.multiple_of((4 * chip[0] + 2 * chip[1] + 1 - c) * hrs[i], 16), hrs[i]), :]
                pltpu.make_async_remote_copy(
                    src_ref=rows, dst_ref=rows, send_sem=send.at[3 * i + j], recv_sem=recv.at[3 * i + j],
                    device_id=(x, y, 1 - c), device_id_type=MESH).wait_recv()
        for cp in started:
            cp.wait_send()

    return list(pl.pallas_call(
        body, name=name, in_specs=[ANY] * nb, out_specs=[ANY] * nb,
        out_shape=[jax.ShapeDtypeStruct(b.shape, b.dtype) for b in bufs],
        input_output_aliases={i: i for i in range(nb)},
        scratch_shapes=[pltpu.SemaphoreType.DMA((3 * nb,)), pltpu.SemaphoreType.DMA((3 * nb,))],
        compiler_params=pltpu.CompilerParams(has_side_effects=True),
    )(*bufs))


def _pair_exchange(grads, name):
    nr = len(grads)
    n_l = grads[0].shape[0]
    n_sem = nr * n_l * N_CHIP

    def body(*refs):
        src = refs[:nr]
        out = refs[nr:2 * nr]
        send, recv = refs[2 * nr:]
        x, y, c = _mesh_pos()
        copies = []
        for r in range(nr):
            hr = grads[r].shape[1] // N_DEV
            for layer in range(n_l):
                for j in range(N_CHIP):
                    idx = (r * n_l + layer) * N_CHIP + j
                    start = pl.multiple_of((2 * j + 1 - c) * hr, 16)
                    cp = pltpu.make_async_remote_copy(
                        src_ref=src[r].at[layer, pl.ds(start, hr), :], dst_ref=out[r].at[layer, j],
                        send_sem=send.at[idx], recv_sem=recv.at[idx], device_id=(x, y, 1 - c), device_id_type=MESH)
                    cp.start()
                    copies.append(cp)
        for cp in copies:
            cp.wait()

    return pl.pallas_call(
        body, name=name,
        in_specs=[ANY] * nr, out_specs=[ANY] * nr,
        out_shape=[jax.ShapeDtypeStruct((n_l, N_CHIP, g.shape[1] // N_DEV, g.shape[2]), bf16) for g in grads],
        scratch_shapes=[pltpu.SemaphoreType.DMA((n_sem,)), pltpu.SemaphoreType.DMA((n_sem,))],
        compiler_params=pltpu.CompilerParams(has_side_effects=True),
    )(*grads)


def _pair_sum(grad, other, core, chip, name):
    n_l, rows, cols = grad.shape
    hr = rows // N_DEV
    g5 = grad.reshape(n_l, N_CHIP, 2, hr, cols)
    where = jnp.stack([core, chip]).astype(jnp.int32)

    def body(where_ref, g_ref, o_ref, s_ref, mine_ref):
        s_ref[...] = (g_ref[...].astype(f32) + o_ref[...].astype(f32)).astype(bf16)
        mine_ref[...] = s_ref[where_ref[1]]

    return pl.pallas_call(
        body, name=name,
        grid_spec=pltpu.PrefetchScalarGridSpec(
            num_scalar_prefetch=1, grid=(n_l,),
            in_specs=[pl.BlockSpec((None, N_CHIP, None, hr, cols), lambda l, w: (l, 0, w[0], 0, 0)),
                      pl.BlockSpec((None, N_CHIP, hr, cols), lambda l, w: (l, 0, 0, 0))],
            out_specs=[pl.BlockSpec((None, N_CHIP, hr, cols), lambda l, w: (l, 0, 0, 0)),
                       pl.BlockSpec((None, None, hr, cols), lambda l, w: (l, w[1], 0, 0))]),
        out_shape=[jax.ShapeDtypeStruct((n_l, N_CHIP, hr, cols), bf16)] * 2,
        compiler_params=_params(("arbitrary",)),
    )(where, g5, other)


def _chip_plan(nr, n_l):
    def plan(refs, dry=False):
        if dry:
            return [None] * (nr * n_l * 3)
        x, y, c = _mesh_pos()
        out = []
        for r in range(nr):
            for layer in range(n_l):
                for chip in _other_chips(x, y):
                    out.append((refs[r].at[layer, 2 * chip[0] + chip[1]], refs[nr + r].at[layer, 2 * x + y], (*chip, c)))
        return out
    return plan


def _chip_sum(parts, core, name):
    _, _, hr, cols = parts[0].shape

    def body(core_ref, p0_ref, p1_ref, o_ref):
        def total(p_ref):
            acc = p_ref[0].astype(f32) + p_ref[1].astype(f32)
            acc = acc + p_ref[2].astype(f32)
            return acc + p_ref[3].astype(f32)

        @pl.when(pl.program_id(0) == 0)
        def _():
            o_ref[...] = total(p0_ref)

        @pl.when(pl.program_id(0) == 1)
        def _():
            o_ref[...] = total(p1_ref)

    spec = pl.BlockSpec((None, N_CHIP, hr, cols), lambda l, cr: (0, 0, 0, 0))
    return pl.pallas_call(
        body, name=name,
        grid_spec=pltpu.PrefetchScalarGridSpec(
            num_scalar_prefetch=1, grid=(2,), in_specs=[spec, spec],
            out_specs=pl.BlockSpec((None, None, hr, cols), lambda l, cr: (l, cr[0], 0, 0))),
        out_shape=jax.ShapeDtypeStruct((2, 2, hr, cols), f32),
        compiler_params=_params(("arbitrary",)),
    )(core, parts[0], parts[1])


def _share_plan(n):
    def plan(refs, dry=False):
        if dry:
            return [None] * (2 * n)
        x, y, c = _mesh_pos()
        return [(ref.at[layer, c], ref.at[layer, c], (x, y, 1 - c)) for ref in refs for layer in range(2)]
    return plan


def _sum_small(parts, name):
    n, rows, cols = parts.shape

    def body(p_ref, o_ref):
        acc = p_ref[0].astype(f32)
        for d in range(1, n):
            acc = acc + p_ref[d].astype(f32)
        o_ref[...] = acc

    return pl.pallas_call(
        body, name=name, grid=(rows // 16,),
        in_specs=[pl.BlockSpec((n, 16, cols), lambda i: (0, i, 0))], out_specs=pl.BlockSpec((16, cols), lambda i: (i, 0)),
        out_shape=jax.ShapeDtypeStruct((rows, cols), f32),
        compiler_params=_params(("parallel",)),
    )(parts)


def _adamw(w, g, m, v, name):
    n_l, rows, cols = w.shape
    budget = 42 * 1024 * 1024
    tr = next(rows // d for d in range(1, rows + 1)
              if rows % d == 0 and (rows // d) % 8 == 0 and (rows // d) * cols * 4 * 14 <= budget)

    def body(w_ref, g_ref, m_ref, v_ref, d_ref, nm_ref, nv_ref):
        gg = g_ref[...]
        nm = ADAM_B1 * m_ref[...] + (1.0 - ADAM_B1) * gg
        nv = ADAM_B2 * v_ref[...] + (1.0 - ADAM_B2) * (gg * gg)
        m_hat = nm / (1.0 - ADAM_B1 ** ADAM_STEP)
        v_hat = nv / (1.0 - ADAM_B2 ** ADAM_STEP)
        d_ref[...] = -ADAM_LR * (m_hat / (jnp.sqrt(v_hat) + ADAM_EPS) + ADAM_WD * w_ref[...])
        nm_ref[...] = nm
        nv_ref[...] = nv

    blk = pl.BlockSpec((None, tr, cols), lambda l, i: (l, i, 0))
    return pl.pallas_call(
        body, name=name, grid=(n_l, rows // tr),
        in_specs=[blk] * 4, out_specs=[blk] * 3, out_shape=[jax.ShapeDtypeStruct((n_l, rows, cols), f32)] * 3,
        compiler_params=_params(("parallel", "parallel")),
    )(w, g, m, v)


SMALL = ("mix_norm", "q_norm", "k_norm", "sinks", "sgu_norm", "w_s", "b_s", "ffn_norm", "conv_b", "conv_w")


def _pack_small(arrs):
    flat = jnp.concatenate([a.reshape(-1) for a in arrs])
    pad = (-flat.shape[0]) % (16 * 1024)
    return jnp.pad(flat, (0, pad)).reshape(-1, 1024)


def _unpack_small(pack, shapes):
    flat = pack.reshape(-1)
    out, off = [], 0
    for s in shapes:
        n = int(np.prod(s))
        out.append(flat[off:off + n].reshape(s))
        off += n
    return out


def kernel(x, mix_norm, w_in, q_norm, k_norm, sinks, sgu_norm, w_s, b_s, w_oa, w_ob, w_out, ffn_norm, w_up, conv_w, conv_b, w_down, loss_target, m_mix_norm, m_w_in, m_q_norm, m_k_norm, m_sinks, m_sgu_norm, m_w_s, m_b_s, m_w_oa, m_w_ob, m_w_out, m_ffn_norm, m_w_up, m_conv_w, m_conv_b, m_w_down, v_mix_norm, v_w_in, v_q_norm, v_k_norm, v_sinks, v_sgu_norm, v_w_s, v_b_s, v_w_oa, v_w_ob, v_w_out, v_ffn_norm, v_w_up, v_conv_w, v_conv_b, v_w_down):
    weights = dict(mix_norm=mix_norm, w_in=w_in, q_norm=q_norm, k_norm=k_norm, sinks=sinks, sgu_norm=sgu_norm,
                   w_s=w_s, b_s=b_s, w_oa=w_oa, w_ob=w_ob, w_out=w_out, ffn_norm=ffn_norm, w_up=w_up,
                   conv_w=conv_w, conv_b=conv_b, w_down=w_down)
    mom_m = dict(mix_norm=m_mix_norm, w_in=m_w_in, q_norm=m_q_norm, k_norm=m_k_norm, sinks=m_sinks,
                 sgu_norm=m_sgu_norm, w_s=m_w_s, b_s=m_b_s, w_oa=m_w_oa, w_ob=m_w_ob, w_out=m_w_out,
                 ffn_norm=m_ffn_norm, w_up=m_w_up, conv_w=m_conv_w, conv_b=m_conv_b, w_down=m_w_down)
    mom_v = dict(mix_norm=v_mix_norm, w_in=v_w_in, q_norm=v_q_norm, k_norm=v_k_norm, sinks=v_sinks,
                 sgu_norm=v_sgu_norm, w_s=v_w_s, b_s=v_b_s, w_oa=v_w_oa, w_ob=v_w_ob, w_out=v_w_out,
                 ffn_norm=v_ffn_norm, w_up=v_w_up, conv_w=v_conv_w, conv_b=v_conv_b, w_down=v_w_down)
    n_seq, seq, _ = x.shape
    T = n_seq * seq
    core = lax.axis_index("c")
    chip = 2 * lax.axis_index("x") + lax.axis_index("y")
    tm = min(512, seq)
    tm_ff = min(256, seq)
    tm_sgu = min(512, seq)
    tk_dw = min(2048, T)

    me = 2 * chip + core
    names = [r[0] for r in REGIONS]
    hrs = {name: rows // N_DEV for name, rows, _, _ in REGIONS}
    def placed(l, name, after=None):
        _, rows, cols, transposed = next(r for r in REGIONS if r[0] == name)
        shard = (jnp.swapaxes(weights[name], 1, 2) if transposed else weights[name]).reshape(2, 2, hrs[name], cols)
        return _place(shard, l, core, N_DEV, me, bf16, f"place_{name}_{l}", after).reshape(rows, cols)

    group_keys = [[(0, "w_in")], [(0, n) for n in names[1:]],
                  [(1, n) for n in ("w_in", "w_oa", "w_ob", "w_out")], [(1, "w_up"), (1, "w_down")]]
    n_direct = [1, 0, 0, 0]
    plans = [_gather_plan([hrs[n] for _, n in keys], nd) for keys, nd in zip(group_keys, n_direct)]
    first_bufs = [placed(0, "w_in"), conv_w, jnp.zeros((N_CHIP,) + conv_w.shape, f32)]
    started, tok = _split_start([(first_bufs, plans[0])], "gather_start_0")
    first_start_done = jnp.broadcast_to(tok[0:1, 0:1], (512, D_MODEL))
    rest_bufs = [[placed(l, n, first_start_done) for l, n in keys] for keys in group_keys[1:]]
    more, tok = _split_start(list(zip(rest_bufs, plans[1:])), "gather_start_1")
    started += more
    second_start_done = jnp.broadcast_to(tok[0:1, 0:1], (512, D_MODEL))
    gathered = [{}, {}]

    def arrived(g, after):
        send, recv, bufs = started[g]
        hr_list = [hrs[n] for _, n in group_keys[g]]
        bufs = _split_wait(bufs, send, recv, _gather_plan(hr_list, n_direct[g]), after, f"gather_wait_{g}")
        return bufs[:len(hr_list)], bufs[len(hr_list):]

    def start_pass(g, bufs):
        (res,), token = _split_start([(bufs, _pass_plan([hrs[n] for _, n in group_keys[g]]))], f"pass_start_{g}")
        return res, token[0:1, 0:1]

    def finish_pass(g, res, after):
        send, recv, bufs = res
        use(g, _split_wait(bufs, send, recv, _pass_plan([hrs[n] for _, n in group_keys[g]]), after, f"pass_wait_{g}"))

    def use(g, bufs):
        for (l, n), b in zip(group_keys[g], bufs):
            gathered[l][n] = b

    xs = x.reshape(T, D_MODEL)
    bufs, (_, conv_w_land) = arrived(0, second_start_done)
    use(0, _pass_to_sibling(bufs, [hrs["w_in"]], "gather_pass_0"))
    conv_w_all = lax.dynamic_update_slice(conv_w_land, conv_w[None], (chip, 0, 0, 0))
    conv_w_full = jnp.concatenate([conv_w_all[j] for j in range(N_CHIP)], axis=-1)
    saved = []
    cur = xs
    for l in range(2):
        wl = gathered[l]
        b_col = b_s[l].reshape(SGU_GROUPS, BLOCK, 1)
        qkv, su, sv, ga, gb, h = _in_proj(cur, mix_norm[l][None], wl["w_in"], l, tm)
        y_att = _attn_fwd(qkv, q_norm[l][None], k_norm[l][None], sinks[l], n_seq, seq)
        later = 1 if l == 0 else 3
        passing, tok = start_pass(later, arrived(later, y_att)[0])
        y_sgu = _sgu_fwd(su, sv, sgu_norm[l][None] + tok, w_s[l], b_col, tm_sgu)
        finish_pass(later, passing, y_sgu)
        x1, merged, a_o, b_o = _merge_fwd(cur, y_att, y_sgu, ga, gb, wl["w_oa"], wl["w_ob"], wl["w_out"], l, tm)
        h2, z, act = _ffn_up(x1, ffn_norm[l][None], wl["w_up"], conv_w_full[l], conv_b[l][None], l, seq, tm_ff)
        saved.append(dict(x=cur, qkv=qkv, su=su, sv=sv, ga=ga, gb=gb, h=h, y_att=y_att, y_sgu=y_sgu, x1=x1,
                          merged=merged, a=a_o, b=b_o, h2=h2, z=z, act=act, b_col=b_col))
        if l == 0:
            pass_2, tok = start_pass(2, arrived(2, act)[0])
            cur = _ffn_down(x1, act, wl["w_down"], tok, l, tm)
            finish_pass(2, pass_2, cur)
        else:
            dy, dyb, loss_part = _ffn_down_loss(x1, act, wl["w_down"], loss_target.reshape(T, D_MODEL), l, tm)

    core_arr = core.astype(jnp.int32).reshape(1)
    big = [{}, {}]
    small = {name: [None, None] for name in SMALL}

    def start_pairs(l, keys, tag):
        gl = [big[l][n] for n in keys]
        land = [lax.empty((1, N_CHIP, hrs[n], g.shape[2]), bf16) for n, g in zip(keys, gl)]
        (res,), token = _split_start([(gl + land, _pair_plan([hrs[n] for n in keys]))], f"pair_start_{tag}")
        return (l, keys, res, tag), token[0:1, 0:1]

    def pairs_to_chips(state, after):
        l, keys, (send, recv, bufs), tag = state
        bufs = _split_wait(bufs, send, recv, _pair_plan([hrs[n] for n in keys]), after, f"pair_wait_{tag}")
        return sums_to_chips(l, keys, bufs[:len(keys)], bufs[len(keys):], tag)

    def sums_to_chips(l, keys, gl, from_sibling, tag):
        pairs = [_pair_sum(g, o, core, chip, f"pair_sum_{n}_{l}") for g, o, n in zip(gl, from_sibling, keys)]
        bufs = [p[0] for p in pairs] + [p[1] for p in pairs]
        (res,), token = _split_start([(bufs, _chip_plan(len(keys), 1))], f"chip_start_{tag}")
        return (l, keys, res, tag), token[0:1, 0:1]

    def start_reduce(l, keys, tag):
        gl = [big[l][n] for n in keys]
        return sums_to_chips(l, keys, gl, _pair_exchange(gl, f"pair_exchange_{tag}"), tag)

    landed = {}

    def finish_reduce(state, after):
        l, keys, (send, recv, bufs), tag = state
        bufs = _split_wait(bufs, send, recv, _chip_plan(len(keys), 1), after, f"chip_wait_{tag}")
        for n, p in zip(keys, bufs[len(keys):]):
            landed[(l, n)] = p

    rest = [n for n in SMALL if n != "w_s"]
    rest_shapes = [weights[n].shape[1:] if n != "conv_w" else (3, 2 * D_FF) for n in rest]
    zero = jnp.zeros((), jnp.int32)

    def start_small(l):
        extra = loss_part[0, 0:1] if l == 1 else jnp.zeros((1,), f32)
        packs = [(_pack_small([small[n][l] for n in rest] + [extra]), f32, "small"),
                 (small["w_s"][l].reshape(-1, 1024), bf16, "w_s")]
        bufs = [_place(p[None, None], 0, zero, N_DEV, me, dt, f"place_{tag}_{l}") for p, dt, tag in packs]
        (res,), token = _split_start([(bufs, _all_to_all_plan(2))], f"small_start_{l}")
        return res, token[0:1, 0:1]

    def finish_small(l, res, after):
        send, recv, bufs = res
        bufs = _split_wait(bufs, send, recv, _all_to_all_plan(2), after, f"small_wait_{l}")
        out = dict(zip(rest + ["loss"], _unpack_small(_sum_small(bufs[0], f"sum_small_{l}"), rest_shapes + [(1,)])))
        out["w_s"] = _sum_small(bufs[1], f"sum_w_s_{l}").reshape(w_s.shape[1:])
        return out

    pending = []
    after_start = jnp.zeros((1, 1), f32)
    for l in (1, 0):
        s = saved[l]
        wl = gathered[l]
        dz, dconv = _ffn_bwd(dyb, s["z"], conv_w_full[l], conv_b[l][None] + after_start, wl["w_down"], l, seq, tm_ff)
        big[l]["w_down"] = _weight_grad(s["act"], dyb, 1408, tk_dw, f"dw_down_{l}")
        big[l]["w_up"] = _weight_grad(dz, s["h2"], 1408, tk_dw, f"dw_up_{l}")
        ffn_gain, sgu_gain, q_gain = ffn_norm[l][None], sgu_norm[l][None], q_norm[l][None]
        if l == 0:
            pairs_a, tok = start_pairs(0, ["w_down", "w_up"], "0a")
            ffn_gain = ffn_gain + tok
        dx1, dx1b, d_ffn = _norm_bwd([dz], wl["w_up"], l, s["x1"], ffn_gain, dy, tm, f"ffn_norm_bwd_{l}")
        if l == 0:
            state, tok = pairs_to_chips(pairs_a, dx1b)
            pending.append(state)
            sgu_gain = sgu_gain + tok
        small["conv_w"][l] = dconv[0:3]
        small["conv_b"][l] = dconv[3]
        small["ffn_norm"][l] = d_ffn[0]
        da, db, dga, dgb, dya, dys = _merge_bwd(dx1b, s["ga"], s["gb"], s["a"], s["b"],
                                                wl["w_oa"], wl["w_ob"], wl["w_out"], l, tm)
        big[l]["w_out"] = _weight_grad(s["merged"], dx1b, 1024, tk_dw, f"dw_out_{l}")
        big[l]["w_oa"] = _weight_grad(da, s["y_att"], 1024, tk_dw, f"dw_oa_{l}")
        big[l]["w_ob"] = _weight_grad(db, s["y_sgu"], 1024, tk_dw, f"dw_ob_{l}")
        if l == 0:
            pairs_m, tok = start_pairs(0, ["w_out", "w_oa", "w_ob"], "0m")
            sgu_gain = sgu_gain + tok
        dsu, dsv, d_ws, d_bs, d_sgu = _sgu_bwd(dys, s["su"], s["sv"], sgu_gain, w_s[l], s["b_col"], tm_sgu)
        if l == 0:
            state, tok = pairs_to_chips(pairs_m, dsv)
            pending.append(state)
            q_gain = q_gain + tok
        causal = np.tril(np.ones((BLOCK, BLOCK), bool))
        small["w_s"][l] = jnp.where(causal[None], d_ws, 0.0)
        small["b_s"][l] = d_bs[:, :, 0]
        small["sgu_norm"][l] = d_sgu[0]
        dqkv, d_qg, d_kg, d_sink = _attn_bwd(dya, s["qkv"], q_gain, k_norm[l][None], sinks[l], n_seq, seq)
        small["q_norm"][l] = d_qg[0]
        small["k_norm"][l] = d_kg[0]
        small["sinks"][l] = d_sink[:, 0]
        dproj = [dqkv, dsu, dsv, dga, dgb]
        big[l]["w_in"] = _weight_grad_rows(dproj, s["h"], min(1024, T), f"dw_in_{l}")
        if l == 1:
            pairs_1, tok = start_pairs(1, names, "1")
        else:
            state, tok = start_reduce(0, ["w_in"], "0b")
            pending.append(state)
        dy, dyb, d_mix = _norm_bwd(dproj, wl["w_in"], l, s["x"], mix_norm[l][None] + tok, dx1, tm, f"mix_norm_bwd_{l}")
        small["mix_norm"][l] = d_mix[0]
        if l == 1:
            state, tok = pairs_to_chips(pairs_1, dyb)
            pending.append(state)
            small_1, after_start = start_small(1)
            after_start = after_start + tok
    grad_x = dy.reshape(n_seq, seq, D_MODEL)

    small_0, _ = start_small(0)
    for state in pending:
        finish_reduce(state, dyb)
    share_keys = [["w_in", "w_oa", "w_ob", "w_out"], ["w_up", "w_down"]]
    halves = {n: _chip_sum([landed[(0, n)], landed[(1, n)]], core_arr, f"chip_sum_{n}") for n in names}
    swaps, _ = _split_start([([halves[n] for n in keys], _share_plan(len(keys))) for keys in share_keys], "share_start")
    shared = {}

    def finish_share(k, after):
        send, recv, bufs = swaps[k]
        shared.update(zip(share_keys[k], _split_wait(bufs, send, recv, _share_plan(len(bufs)), after, f"share_wait_{k}")))

    finish_share(0, dyb)
    grad, delta, new_m, new_v = {}, {}, {}, {}
    flip = lambda a: jnp.swapaxes(a, 1, 2)
    for name, rows, cols, transposed in sorted(REGIONS, key=lambda r: r[0] in share_keys[1]):
        if name == share_keys[1][0]:
            finish_share(1, delta[share_keys[0][-1]])
        g = shared[name].reshape(2, rows // N_CHIP, cols)
        if transposed and weights[name].shape[2] % 128:
            d, nm, nv = _adamw(flip(weights[name]), g, flip(mom_m[name]), flip(mom_v[name]), f"adamw_{name}")
            grad[name], delta[name], new_m[name], new_v[name] = flip(g), flip(d), flip(nm), flip(nv)
        else:
            grad[name] = flip(g) if transposed else g
            delta[name], new_m[name], new_v[name] = _adamw(weights[name], grad[name], mom_m[name], mom_v[name],
                                                           f"adamw_{name}")

    per_layer = [finish_small(0, small_0, delta["w_down"]), finish_small(1, small_1, dyb)]
    loss = per_layer[1]["loss"][0]
    grad_small = {n: jnp.stack([per_layer[0][n], per_layer[1][n]]) for n in SMALL}
    cw_cols = conv_w.shape[-1]
    grad_small["conv_w"] = lax.dynamic_slice_in_dim(grad_small["conv_w"], chip * cw_cols, cw_cols, axis=2)

    as_rows = lambda a: a.reshape(2, -1, BLOCK)
    d, nm, nv = _adamw(as_rows(w_s), as_rows(grad_small["w_s"]), as_rows(m_w_s), as_rows(v_w_s), "adamw_w_s")
    grad["w_s"], delta["w_s"], new_m["w_s"], new_v["w_s"] = (
        grad_small["w_s"], d.reshape(w_s.shape), nm.reshape(w_s.shape), nv.reshape(w_s.shape))
    shapes = [weights[n].shape for n in rest]
    d, nm, nv = _adamw(_pack_small([weights[n] for n in rest])[None], _pack_small([grad_small[n] for n in rest])[None],
                       _pack_small([mom_m[n] for n in rest])[None], _pack_small([mom_v[n] for n in rest])[None],
                       "adamw_small")
    for n, dd, mm, vv in zip(rest, _unpack_small(d, shapes), _unpack_small(nm, shapes), _unpack_small(nv, shapes)):
        grad[n], delta[n], new_m[n], new_v[n] = grad_small[n], dd, mm, vv

    order = ["mix_norm", "w_in", "q_norm", "k_norm", "sinks", "sgu_norm", "w_s", "b_s", "w_oa", "w_ob", "w_out",
             "ffn_norm", "w_up", "conv_w", "conv_b", "w_down"]
    return (loss, grad_x, *[grad[n] for n in order], *[delta[n] for n in order],
            *[new_m[n] for n in order], *[new_v[n] for n in order])
```
